```python
import math
import jax, jax.numpy as jnp
from jax import lax
import numpy as np

D_MODEL = 1024
BATCH = 8
SEQ = 8192
DEPTH = 2

EPS = 1e-6
Q_BLOCK = 128
SB_HEADS = 4
SB_HEAD_DIM = 64
SB_WIDTH = SB_HEADS * SB_HEAD_DIM
SSM_HEADS = 8
SSM_HEAD_DIM = 64
SSM_INNER = SSM_HEADS * SSM_HEAD_DIM
SSM_GROUPS = 2
SSM_STATE = 64
SSM_CONV = 4
SSM_CHUNK = 128
SSM_CONV_DIM = SSM_INNER + 2 * SSM_GROUPS * SSM_STATE
MLA_HEADS = 4
MLA_NOPE = 64
MLA_ROPE = 32
MLA_V = 64
MLA_Q_RANK = 256
MLA_KV_RANK = 128
MLA_WIDTH = MLA_HEADS * MLA_V
ROPE_THETA = 10000.0
D_MIX = SB_WIDTH + SSM_INNER + MLA_WIDTH
IN_SPLITS = (3 * SB_WIDTH, SSM_INNER, SSM_CONV_DIM, SSM_HEADS, MLA_Q_RANK, MLA_KV_RANK, MLA_ROPE)
D_IN = 3 * SB_WIDTH + SSM_INNER + SSM_CONV_DIM + SSM_HEADS + MLA_Q_RANK + MLA_KV_RANK + MLA_ROPE
D_FF = 2816
FFN_CONV = 3

kernel_name = "hymba_sb_ssd_mla_convffn"


def rms_norm(x, g):
    xf = x.astype(jnp.float32)
    y = xf * lax.rsqrt(jnp.mean(xf * xf, axis=-1, keepdims=True) + EPS)
    return (y * g.astype(jnp.float32)).astype(x.dtype)


def causal_depthwise_conv(x, w, b):
    k = w.shape[0]
    c = x.shape[-1]
    y = lax.conv_general_dilated(
        x, w[:, None, :].astype(x.dtype), window_strides=(1,), padding=((k - 1, 0),),
        dimension_numbers=('NWC', 'WIO', 'NWC'), feature_group_count=c)
    return y + b.astype(y.dtype)


def query_blocks(fn, q):
    b, s, h, d = q.shape
    nb = s // Q_BLOCK
    qb = q.reshape(b, nb, Q_BLOCK, h, d).transpose(1, 0, 2, 3, 4)
    starts = jnp.arange(nb, dtype=jnp.int32) * Q_BLOCK
    out = lax.map(lambda a: fn(a[0], a[1]), (qb, starts))
    return out.transpose(1, 0, 2, 3, 4).reshape(b, s, h, out.shape[-1])


def stick_breaking_attention(q, k, v):
    s = k.shape[1]
    scale = SB_HEAD_DIM ** -0.5
    k_idx = jnp.arange(s)

    def block(qb, t0):
        z = jnp.einsum('bqhd,bkhd->bhqk', qb, k).astype(jnp.float32) * scale
        q_idx = t0 + jnp.arange(Q_BLOCK)
        mask = k_idx[None, :] < q_idx[:, None]
        log_keep = jnp.where(mask, jax.nn.log_sigmoid(-z), 0.0)
        later = lax.cumsum(log_keep, axis=3, reverse=True) - log_keep
        log_w = jnp.where(mask, jax.nn.log_sigmoid(z) + later, -jnp.inf)
        w = jnp.exp(log_w)
        return jnp.einsum('bhqk,bkhd->bqhd', w.astype(v.dtype), v)

    return query_blocks(block, q)


def causal_softmax_attention(q, k, v, scale):
    s = k.shape[1]
    k_idx = jnp.arange(s)

    def block(qb, t0):
        sc = jnp.einsum('bqhd,bkhd->bhqk', qb, k).astype(jnp.float32) * scale
        q_idx = t0 + jnp.arange(Q_BLOCK)
        mask = k_idx[None, :] <= q_idx[:, None]
        p = jax.nn.softmax(jnp.where(mask, sc, -jnp.inf), axis=-1)
        return jnp.einsum('bhqk,bkhd->bqhd', p.astype(v.dtype), v)

    return query_blocks(block, q)


def ssd_scan(xs, dt, a, bm, cm):
    f32 = jnp.float32
    b, s, h, p = xs.shape
    g, n = bm.shape[-2:]
    r = h // g
    l = SSM_CHUNK
    nc = s // l
    x = xs.astype(f32).reshape(b, nc, l, g, r, p)
    dt = dt.astype(f32).reshape(b, nc, l, g, r)
    bm = bm.astype(f32).reshape(b, nc, l, g, n)
    cm = cm.astype(f32).reshape(b, nc, l, g, n)
    da_cs = jnp.cumsum(dt * a.astype(f32).reshape(g, r), axis=2)
    xdt = x * dt[..., None]
    causal = jnp.tril(jnp.ones((l, l), dtype=bool))
    seg = da_cs[:, :, :, None] - da_cs[:, :, None, :]
    decay = jnp.exp(jnp.where(causal[:, :, None, None], seg, -jnp.inf))
    cb = jnp.einsum('bclgn,bcsgn->bclsg', cm, bm)
    y_diag = jnp.einsum('bclsg,bclsgr,bcsgrp->bclgrp', cb, decay, xdt)
    to_end = jnp.exp(da_cs[:, :, -1:] - da_cs)
    states = jnp.einsum('bclgn,bclgr,bclgrp->bcgrpn', bm, to_end, xdt)
    chunk_decay = jnp.exp(da_cs[:, :, -1])

    def step(h_state, inp):
        st, dc = inp
        return h_state * dc[..., None, None] + st, h_state

    h0 = jnp.zeros((b, g, r, p, n), f32)
    _, prev = lax.scan(step, h0, (states.transpose(1, 0, 2, 3, 4, 5), chunk_decay.transpose(1, 0, 2, 3)))
    prev = prev.transpose(1, 0, 2, 3, 4, 5)
    y_off = jnp.einsum('bclgn,bcgrpn,bclgr->bclgrp', cm, prev, jnp.exp(da_cs))
    return (y_diag + y_off).reshape(b, s, h, p)


def rope_tables(positions):
    inv_freq = 1.0 / (ROPE_THETA ** (jnp.arange(0, MLA_ROPE, 2, dtype=jnp.float32) / MLA_ROPE))
    ang = positions.astype(jnp.float32)[..., None] * inv_freq
    return jnp.cos(ang)[:, :, None, :], jnp.sin(ang)[:, :, None, :]


def apply_rope(x, cos, sin):
    xf = x.astype(jnp.float32)
    x1, x2 = xf[..., :MLA_ROPE // 2], xf[..., MLA_ROPE // 2:]
    return jnp.concatenate([x1 * cos - x2 * sin, x2 * cos + x1 * sin], axis=-1).astype(x.dtype)


def hybrid_mixer(h, positions, w_in, sb_out_norm, conv_w, conv_b, dt_bias, a_log, d_skip,
                 ssm_out_norm, q_norm, w_uq, kv_norm, w_ukv, mla_out_norm, w_out):
    b, s, _ = h.shape
    proj = h @ w_in
    offs = [int(o) for o in np.cumsum(IN_SPLITS)[:-1]]
    sb_qkv, z, xbc, dt_raw, c_q, c_kv, k_rope = jnp.split(proj, offs, axis=-1)

    qkv = sb_qkv.reshape(b, s, 3, SB_HEADS, SB_HEAD_DIM)
    y_a = stick_breaking_attention(qkv[:, :, 0], qkv[:, :, 1], qkv[:, :, 2]).reshape(b, s, SB_WIDTH)
    y_a = rms_norm(y_a, sb_out_norm)

    xbc = jax.nn.silu(causal_depthwise_conv(xbc, conv_w, conv_b))
    gn = SSM_GROUPS * SSM_STATE
    x_ssm, b_ssm, c_ssm = jnp.split(xbc, [SSM_INNER, SSM_INNER + gn], axis=-1)
    dt = jax.nn.softplus(dt_raw.astype(jnp.float32) + dt_bias.astype(jnp.float32))
    a = -jnp.exp(a_log.astype(jnp.float32))
    x_heads = x_ssm.reshape(b, s, SSM_HEADS, SSM_HEAD_DIM)
    y_ssm = ssd_scan(x_heads, dt, a,
                     b_ssm.reshape(b, s, SSM_GROUPS, SSM_STATE),
                     c_ssm.reshape(b, s, SSM_GROUPS, SSM_STATE))
    y_ssm = y_ssm + x_heads.astype(jnp.float32) * d_skip.astype(jnp.float32)[:, None]
    y_b = rms_norm(y_ssm.reshape(b, s, SSM_INNER) * jax.nn.silu(z.astype(jnp.float32)),
                   ssm_out_norm).astype(h.dtype)

    cos, sin = rope_tables(positions)
    q_c = (rms_norm(c_q, q_norm) @ w_uq).reshape(b, s, MLA_HEADS, MLA_NOPE + MLA_ROPE)
    q_full = jnp.concatenate([q_c[..., :MLA_NOPE], apply_rope(q_c[..., MLA_NOPE:], cos, sin)], axis=-1)
    kv = (rms_norm(c_kv, kv_norm) @ w_ukv).reshape(b, s, MLA_HEADS, MLA_NOPE + MLA_V)
    k_pe = apply_rope(k_rope[:, :, None, :], cos, sin)
    k_full = jnp.concatenate(
        [kv[..., :MLA_NOPE], jnp.broadcast_to(k_pe, (b, s, MLA_HEADS, MLA_ROPE))], axis=-1)
    y_c = causal_softmax_attention(q_full, k_full, kv[..., MLA_NOPE:],
                                   (MLA_NOPE + MLA_ROPE) ** -0.5).reshape(b, s, MLA_WIDTH)
    y_c = rms_norm(y_c, mla_out_norm)

    return jnp.concatenate([y_a, y_b, y_c], axis=-1) @ w_out


def conv_glu_ffn(h, w_up, conv_w, conv_b, w_down):
    u = causal_depthwise_conv(h @ w_up, conv_w, conv_b)
    gate, val = jnp.split(u, 2, axis=-1)
    return (jax.nn.silu(gate) * val) @ w_down


def _fwd_setup_inputs(seed: int = 0) -> dict:
    key = jax.random.key(seed)
    ks = jax.random.split(key, 24)
    L = DEPTH
    nrm = jax.random.normal

    def gain(k, shape):
        return 1.0 + 0.02 * nrm(k, shape, jnp.float32)

    x = nrm(ks[0], (BATCH, SEQ, D_MODEL), jnp.float32)
    offset = jax.random.randint(ks[1], (BATCH, 1), 0, 4096)
    positions = (jnp.arange(SEQ, dtype=jnp.int32)[None, :] + offset).astype(jnp.int32)
    dt0 = jnp.exp(jax.random.uniform(ks[8], (L, SSM_HEADS), jnp.float32)
                  * (math.log(0.1) - math.log(0.001)) + math.log(0.001))
    return {
        "x": x,
        "positions": positions,
        "mix_norm": gain(ks[2], (L, D_MODEL)),
        "w_in": nrm(ks[3], (L, D_MODEL, D_IN), jnp.float32) * D_MODEL ** -0.5,
        "sb_out_norm": gain(ks[4], (L, SB_WIDTH)),
        "ssm_conv_w": nrm(ks[5], (L, SSM_CONV, SSM_CONV_DIM), jnp.float32) * SSM_CONV ** -0.5,
        "ssm_conv_b": 0.01 * nrm(ks[6], (L, SSM_CONV_DIM), jnp.float32),
        "ssm_dt_bias": dt0 + jnp.log(-jnp.expm1(-dt0)),
        "ssm_a_log": jnp.log(jax.random.uniform(ks[9], (L, SSM_HEADS), jnp.float32, 1.0, 16.0)),
        "ssm_d": 1.0 + 0.1 * nrm(ks[10], (L, SSM_HEADS), jnp.float32),
        "ssm_out_norm": gain(ks[11], (L, SSM_INNER)),
        "mla_q_norm": gain(ks[12], (L, MLA_Q_RANK)),
        "mla_w_uq": nrm(ks[13], (L, MLA_Q_RANK, MLA_HEADS * (MLA_NOPE + MLA_ROPE)), jnp.float32) * MLA_Q_RANK ** -0.5,
        "mla_kv_norm": gain(ks[14], (L, MLA_KV_RANK)),
        "mla_w_ukv": nrm(ks[15], (L, MLA_KV_RANK, MLA_HEADS * (MLA_NOPE + MLA_V)), jnp.float32) * MLA_KV_RANK ** -0.5,
        "mla_out_norm": gain(ks[16], (L, MLA_WIDTH)),
        "w_out": nrm(ks[17], (L, D_MIX, D_MODEL), jnp.float32) * D_MIX ** -0.5,
        "ffn_norm": gain(ks[18], (L, D_MODEL)),
        "ffn_w_up": nrm(ks[19], (L, D_MODEL, 2 * D_FF), jnp.float32) * D_MODEL ** -0.5,
        "ffn_conv_w": nrm(ks[20], (L, FFN_CONV, 2 * D_FF), jnp.float32) * FFN_CONV ** -0.5,
        "ffn_conv_b": 0.01 * nrm(ks[21], (L, 2 * D_FF), jnp.float32),
        "ffn_w_down": nrm(ks[22], (L, D_FF, D_MODEL), jnp.float32) * D_FF ** -0.5,
        "final_norm": gain(ks[23], (D_MODEL,)),
    }


def _fwd_reference(x, positions, mix_norm, w_in, sb_out_norm, ssm_conv_w, ssm_conv_b, ssm_dt_bias,
              ssm_a_log, ssm_d, ssm_out_norm, mla_q_norm, mla_w_uq, mla_kv_norm, mla_w_ukv,
              mla_out_norm, w_out, ffn_norm, ffn_w_up, ffn_conv_w, ffn_conv_b, ffn_w_down,
              final_norm):
    h = x
    for i in range(DEPTH):
        h = h + hybrid_mixer(rms_norm(h, mix_norm[i]), positions, w_in[i], sb_out_norm[i],
                             ssm_conv_w[i], ssm_conv_b[i], ssm_dt_bias[i], ssm_a_log[i], ssm_d[i],
                             ssm_out_norm[i], mla_q_norm[i], mla_w_uq[i], mla_kv_norm[i],
                             mla_w_ukv[i], mla_out_norm[i], w_out[i])
        h = h + conv_glu_ffn(rms_norm(h, ffn_norm[i]), ffn_w_up[i], ffn_conv_w[i],
                             ffn_conv_b[i], ffn_w_down[i])
    return rms_norm(h, final_norm)


import jax as _jax
import jax.numpy as _jnp

TWIN_FORMAT = 'train_step'
FWD_PARAMS = ['x', 'positions', 'mix_norm', 'w_in', 'sb_out_norm', 'ssm_conv_w', 'ssm_conv_b', 'ssm_dt_bias', 'ssm_a_log', 'ssm_d', 'ssm_out_norm', 'mla_q_norm', 'mla_w_uq', 'mla_kv_norm', 'mla_w_ukv', 'mla_out_norm', 'w_out', 'ffn_norm', 'ffn_w_up', 'ffn_conv_w', 'ffn_conv_b', 'ffn_w_down', 'final_norm']
TWIN_WEIGHTS = ['mix_norm', 'w_in', 'sb_out_norm', 'ssm_conv_w', 'ssm_conv_b', 'ssm_dt_bias', 'ssm_a_log', 'ssm_d', 'ssm_out_norm', 'mla_q_norm', 'mla_w_uq', 'mla_kv_norm', 'mla_w_ukv', 'mla_out_norm', 'w_out', 'ffn_norm', 'ffn_w_up', 'ffn_conv_w', 'ffn_conv_b', 'ffn_w_down', 'final_norm']
TWIN_DIFF_INPUT = 'x'
TWIN_INPUTS = ['x', 'positions', 'mix_norm', 'w_in', 'sb_out_norm', 'ssm_conv_w', 'ssm_conv_b', 'ssm_dt_bias', 'ssm_a_log', 'ssm_d', 'ssm_out_norm', 'mla_q_norm', 'mla_w_uq', 'mla_kv_norm', 'mla_w_ukv', 'mla_out_norm', 'w_out', 'ffn_norm', 'ffn_w_up', 'ffn_conv_w', 'ffn_conv_b', 'ffn_w_down', 'final_norm', 'loss_target', 'm_mix_norm', 'm_w_in', 'm_sb_out_norm', 'm_ssm_conv_w', 'm_ssm_conv_b', 'm_ssm_dt_bias', 'm_ssm_a_log', 'm_ssm_d', 'm_ssm_out_norm', 'm_mla_q_norm', 'm_mla_w_uq', 'm_mla_kv_norm', 'm_mla_w_ukv', 'm_mla_out_norm', 'm_w_out', 'm_ffn_norm', 'm_ffn_w_up', 'm_ffn_conv_w', 'm_ffn_conv_b', 'm_ffn_w_down', 'm_final_norm', 'v_mix_norm', 'v_w_in', 'v_sb_out_norm', 'v_ssm_conv_w', 'v_ssm_conv_b', 'v_ssm_dt_bias', 'v_ssm_a_log', 'v_ssm_d', 'v_ssm_out_norm', 'v_mla_q_norm', 'v_mla_w_uq', 'v_mla_kv_norm', 'v_mla_w_ukv', 'v_mla_out_norm', 'v_w_out', 'v_ffn_norm', 'v_ffn_w_up', 'v_ffn_conv_w', 'v_ffn_conv_b', 'v_ffn_w_down', 'v_final_norm']
TWIN_OUTPUTS = ['loss', 'grad_x', 'grad_mix_norm', 'grad_w_in', 'grad_sb_out_norm', 'grad_ssm_conv_w', 'grad_ssm_conv_b', 'grad_ssm_dt_bias', 'grad_ssm_a_log', 'grad_ssm_d', 'grad_ssm_out_norm', 'grad_mla_q_norm', 'grad_mla_w_uq', 'grad_mla_kv_norm', 'grad_mla_w_ukv', 'grad_mla_out_norm', 'grad_w_out', 'grad_ffn_norm', 'grad_ffn_w_up', 'grad_ffn_conv_w', 'grad_ffn_conv_b', 'grad_ffn_w_down', 'grad_final_norm', 'delta_mix_norm', 'delta_w_in', 'delta_sb_out_norm', 'delta_ssm_conv_w', 'delta_ssm_conv_b', 'delta_ssm_dt_bias', 'delta_ssm_a_log', 'delta_ssm_d', 'delta_ssm_out_norm', 'delta_mla_q_norm', 'delta_mla_w_uq', 'delta_mla_kv_norm', 'delta_mla_w_ukv', 'delta_mla_out_norm', 'delta_w_out', 'delta_ffn_norm', 'delta_ffn_w_up', 'delta_ffn_conv_w', 'delta_ffn_conv_b', 'delta_ffn_w_down', 'delta_final_norm', 'new_m_mix_norm', 'new_m_w_in', 'new_m_sb_out_norm', 'new_m_ssm_conv_w', 'new_m_ssm_conv_b', 'new_m_ssm_dt_bias', 'new_m_ssm_a_log', 'new_m_ssm_d', 'new_m_ssm_out_norm', 'new_m_mla_q_norm', 'new_m_mla_w_uq', 'new_m_mla_kv_norm', 'new_m_mla_w_ukv', 'new_m_mla_out_norm', 'new_m_w_out', 'new_m_ffn_norm', 'new_m_ffn_w_up', 'new_m_ffn_conv_w', 'new_m_ffn_conv_b', 'new_m_ffn_w_down', 'new_m_final_norm', 'new_v_mix_norm', 'new_v_w_in', 'new_v_sb_out_norm', 'new_v_ssm_conv_w', 'new_v_ssm_conv_b', 'new_v_ssm_dt_bias', 'new_v_ssm_a_log', 'new_v_ssm_d', 'new_v_ssm_out_norm', 'new_v_mla_q_norm', 'new_v_mla_w_uq', 'new_v_mla_kv_norm', 'new_v_mla_w_ukv', 'new_v_mla_out_norm', 'new_v_w_out', 'new_v_ffn_norm', 'new_v_ffn_w_up', 'new_v_ffn_conv_w', 'new_v_ffn_conv_b', 'new_v_ffn_w_down', 'new_v_final_norm']
TWIN_LEAF_KINDS = {'loss': 'loss', 'grad_x': 'grad_x', 'grad_mix_norm': 'grad_w', 'grad_w_in': 'grad_w', 'grad_sb_out_norm': 'grad_w', 'grad_ssm_conv_w': 'grad_w', 'grad_ssm_conv_b': 'grad_w', 'grad_ssm_dt_bias': 'grad_w', 'grad_ssm_a_log': 'grad_w', 'grad_ssm_d': 'grad_w', 'grad_ssm_out_norm': 'grad_w', 'grad_mla_q_norm': 'grad_w', 'grad_mla_w_uq': 'grad_w', 'grad_mla_kv_norm': 'grad_w', 'grad_mla_w_ukv': 'grad_w', 'grad_mla_out_norm': 'grad_w', 'grad_w_out': 'grad_w', 'grad_ffn_norm': 'grad_w', 'grad_ffn_w_up': 'grad_w', 'grad_ffn_conv_w': 'grad_w', 'grad_ffn_conv_b': 'grad_w', 'grad_ffn_w_down': 'grad_w', 'grad_final_norm': 'grad_w', 'delta_mix_norm': 'delta_w', 'delta_w_in': 'delta_w', 'delta_sb_out_norm': 'delta_w', 'delta_ssm_conv_w': 'delta_w', 'delta_ssm_conv_b': 'delta_w', 'delta_ssm_dt_bias': 'delta_w', 'delta_ssm_a_log': 'delta_w', 'delta_ssm_d': 'delta_w', 'delta_ssm_out_norm': 'delta_w', 'delta_mla_q_norm': 'delta_w', 'delta_mla_w_uq': 'delta_w', 'delta_mla_kv_norm': 'delta_w', 'delta_mla_w_ukv': 'delta_w', 'delta_mla_out_norm': 'delta_w', 'delta_w_out': 'delta_w', 'delta_ffn_norm': 'delta_w', 'delta_ffn_w_up': 'delta_w', 'delta_ffn_conv_w': 'delta_w', 'delta_ffn_conv_b': 'delta_w', 'delta_ffn_w_down': 'delta_w', 'delta_final_norm': 'delta_w', 'new_m_mix_norm': 'new_m', 'new_m_w_in': 'new_m', 'new_m_sb_out_norm': 'new_m', 'new_m_ssm_conv_w': 'new_m', 'new_m_ssm_conv_b': 'new_m', 'new_m_ssm_dt_bias': 'new_m', 'new_m_ssm_a_log': 'new_m', 'new_m_ssm_d': 'new_m', 'new_m_ssm_out_norm': 'new_m', 'new_m_mla_q_norm': 'new_m', 'new_m_mla_w_uq': 'new_m', 'new_m_mla_kv_norm': 'new_m', 'new_m_mla_w_ukv': 'new_m', 'new_m_mla_out_norm': 'new_m', 'new_m_w_out': 'new_m', 'new_m_ffn_norm': 'new_m', 'new_m_ffn_w_up': 'new_m', 'new_m_ffn_conv_w': 'new_m', 'new_m_ffn_conv_b': 'new_m', 'new_m_ffn_w_down': 'new_m', 'new_m_final_norm': 'new_m', 'new_v_mix_norm': 'new_v', 'new_v_w_in': 'new_v', 'new_v_sb_out_norm': 'new_v', 'new_v_ssm_conv_w': 'new_v', 'new_v_ssm_conv_b': 'new_v', 'new_v_ssm_dt_bias': 'new_v', 'new_v_ssm_a_log': 'new_v', 'new_v_ssm_d': 'new_v', 'new_v_ssm_out_norm': 'new_v', 'new_v_mla_q_norm': 'new_v', 'new_v_mla_w_uq': 'new_v', 'new_v_mla_kv_norm': 'new_v', 'new_v_mla_w_ukv': 'new_v', 'new_v_mla_out_norm': 'new_v', 'new_v_w_out': 'new_v', 'new_v_ffn_norm': 'new_v', 'new_v_ffn_w_up': 'new_v', 'new_v_ffn_conv_w': 'new_v', 'new_v_ffn_conv_b': 'new_v', 'new_v_ffn_w_down': 'new_v', 'new_v_final_norm': 'new_v'}


def _forward(args):
    return _fwd_reference(*[args[k] for k in FWD_PARAMS])


def _output_shape():
    def fwd():
        inp = _fwd_setup_inputs(0)
        return _fwd_reference(*[inp[k] for k in FWD_PARAMS])
    out = _jax.eval_shape(fwd)
    return out.shape, out.dtype

N_MICROBATCH = 1
ADAM_LR = 0.001
ADAM_B1 = 0.9
ADAM_B2 = 0.999
ADAM_EPS = 1e-08
ADAM_WD = 0.01
ADAM_STEP = 10
PER_EXAMPLE_BATCH_AXIS = {'x': 0, 'positions': 0, 'loss_target': 0}
SHARED_INPUTS = []
_WEIGHT_DTYPES = {'mix_norm': _jnp.float32, 'w_in': _jnp.float32, 'sb_out_norm': _jnp.float32, 'ssm_conv_w': _jnp.float32, 'ssm_conv_b': _jnp.float32, 'ssm_dt_bias': _jnp.float32, 'ssm_a_log': _jnp.float32, 'ssm_d': _jnp.float32, 'ssm_out_norm': _jnp.float32, 'mla_q_norm': _jnp.float32, 'mla_w_uq': _jnp.float32, 'mla_kv_norm': _jnp.float32, 'mla_w_ukv': _jnp.float32, 'mla_out_norm': _jnp.float32, 'w_out': _jnp.float32, 'ffn_norm': _jnp.float32, 'ffn_w_up': _jnp.float32, 'ffn_conv_w': _jnp.float32, 'ffn_conv_b': _jnp.float32, 'ffn_w_down': _jnp.float32, 'final_norm': _jnp.float32}
MOMENT_SCALE = {'mix_norm': 2.684447e-01, 'w_in': 1.733919e-01, 'sb_out_norm': 1.894928e-01, 'ssm_conv_w': 1.640603e-01, 'ssm_conv_b': 2.106462e-01, 'ssm_dt_bias': 3.513080e-01, 'ssm_a_log': 7.987120e-01, 'ssm_d': 2.043747e+00, 'ssm_out_norm': 1.965237e-01, 'mla_q_norm': 1.565988e-01, 'mla_w_uq': 1.349424e-01, 'mla_kv_norm': 5.448274e-01, 'mla_w_ukv': 1.912694e-01, 'mla_out_norm': 2.187910e-01, 'w_out': 1.944858e-01, 'ffn_norm': 1.481483e-01, 'ffn_w_up': 6.051376e-02, 'ffn_conv_w': 6.057342e-02, 'ffn_conv_b': 7.019214e-02, 'ffn_w_down': 9.881832e-02, 'final_norm': 6.402426e+01}


def _to_microbatches(a, axis):
    t = _jnp.moveaxis(a, axis, 0)
    t = t.reshape((N_MICROBATCH, t.shape[0] // N_MICROBATCH) + t.shape[1:])
    return _jnp.moveaxis(t, 1, axis + 1)


def setup_inputs(seed: int = 0) -> dict:
    inp = _fwd_setup_inputs(seed)
    key = _jax.random.fold_in(_jax.random.key(seed), 7919)
    shape, _ = _output_shape()
    out = dict(inp)
    out["loss_target"] = _jax.random.normal(_jax.random.fold_in(key, 0), shape, _jnp.float32)
    for i, name in enumerate(TWIN_WEIGHTS):
        w = inp[name].astype(_jnp.float32)
        if MOMENT_SCALE is None:
            s = _jnp.sqrt(_jnp.mean(_jnp.square(w)) + 1e-30)
        else:
            s = MOMENT_SCALE[name]
        km, kv = _jax.random.split(_jax.random.fold_in(key, i + 1))
        out[name] = w
        out["m_" + name] = s * _jax.random.normal(km, w.shape, _jnp.float32)
        out["v_" + name] = (s * s) * _jax.random.uniform(kv, w.shape, _jnp.float32, 0.5, 1.5)
    if N_MICROBATCH > 1:
        for name, axis in PER_EXAMPLE_BATCH_AXIS.items():
            out[name] = _to_microbatches(out[name], axis)
    return {'x': out['x'], 'positions': out['positions'], 'mix_norm': out['mix_norm'], 'w_in': out['w_in'], 'sb_out_norm': out['sb_out_norm'], 'ssm_conv_w': out['ssm_conv_w'], 'ssm_conv_b': out['ssm_conv_b'], 'ssm_dt_bias': out['ssm_dt_bias'], 'ssm_a_log': out['ssm_a_log'], 'ssm_d': out['ssm_d'], 'ssm_out_norm': out['ssm_out_norm'], 'mla_q_norm': out['mla_q_norm'], 'mla_w_uq': out['mla_w_uq'], 'mla_kv_norm': out['mla_kv_norm'], 'mla_w_ukv': out['mla_w_ukv'], 'mla_out_norm': out['mla_out_norm'], 'w_out': out['w_out'], 'ffn_norm': out['ffn_norm'], 'ffn_w_up': out['ffn_w_up'], 'ffn_conv_w': out['ffn_conv_w'], 'ffn_conv_b': out['ffn_conv_b'], 'ffn_w_down': out['ffn_w_down'], 'final_norm': out['final_norm'], 'loss_target': out['loss_target'], 'm_mix_norm': out['m_mix_norm'], 'm_w_in': out['m_w_in'], 'm_sb_out_norm': out['m_sb_out_norm'], 'm_ssm_conv_w': out['m_ssm_conv_w'], 'm_ssm_conv_b': out['m_ssm_conv_b'], 'm_ssm_dt_bias': out['m_ssm_dt_bias'], 'm_ssm_a_log': out['m_ssm_a_log'], 'm_ssm_d': out['m_ssm_d'], 'm_ssm_out_norm': out['m_ssm_out_norm'], 'm_mla_q_norm': out['m_mla_q_norm'], 'm_mla_w_uq': out['m_mla_w_uq'], 'm_mla_kv_norm': out['m_mla_kv_norm'], 'm_mla_w_ukv': out['m_mla_w_ukv'], 'm_mla_out_norm': out['m_mla_out_norm'], 'm_w_out': out['m_w_out'], 'm_ffn_norm': out['m_ffn_norm'], 'm_ffn_w_up': out['m_ffn_w_up'], 'm_ffn_conv_w': out['m_ffn_conv_w'], 'm_ffn_conv_b': out['m_ffn_conv_b'], 'm_ffn_w_down': out['m_ffn_w_down'], 'm_final_norm': out['m_final_norm'], 'v_mix_norm': out['v_mix_norm'], 'v_w_in': out['v_w_in'], 'v_sb_out_norm': out['v_sb_out_norm'], 'v_ssm_conv_w': out['v_ssm_conv_w'], 'v_ssm_conv_b': out['v_ssm_conv_b'], 'v_ssm_dt_bias': out['v_ssm_dt_bias'], 'v_ssm_a_log': out['v_ssm_a_log'], 'v_ssm_d': out['v_ssm_d'], 'v_ssm_out_norm': out['v_ssm_out_norm'], 'v_mla_q_norm': out['v_mla_q_norm'], 'v_mla_w_uq': out['v_mla_w_uq'], 'v_mla_kv_norm': out['v_mla_kv_norm'], 'v_mla_w_ukv': out['v_mla_w_ukv'], 'v_mla_out_norm': out['v_mla_out_norm'], 'v_w_out': out['v_w_out'], 'v_ffn_norm': out['v_ffn_norm'], 'v_ffn_w_up': out['v_ffn_w_up'], 'v_ffn_conv_w': out['v_ffn_conv_w'], 'v_ffn_conv_b': out['v_ffn_conv_b'], 'v_ffn_w_down': out['v_ffn_w_down'], 'v_final_norm': out['v_final_norm']}


def _loss(weights, diff, rest, loss_target):
    with _jax.named_scope("forward"):
        args = {**rest, TWIN_DIFF_INPUT: diff, **{k: w.astype(_WEIGHT_DTYPES[k]) for k, w in weights.items()}}
        y = _forward(args)
    with _jax.named_scope("loss_head"):
        err = _jnp.square(y.astype(_jnp.float32) - loss_target)
        return 0.5 * _jnp.sum(_jnp.mean(err, axis=-1)) if err.ndim else 0.5 * err


def _adamw(w, g, m, v):
    m = ADAM_B1 * m + (1.0 - ADAM_B1) * g
    v = ADAM_B2 * v + (1.0 - ADAM_B2) * _jnp.square(g)
    m_hat = m / (1.0 - ADAM_B1 ** ADAM_STEP)
    v_hat = v / (1.0 - ADAM_B2 ** ADAM_STEP)
    delta = -ADAM_LR * (m_hat / (_jnp.sqrt(v_hat) + ADAM_EPS) + ADAM_WD * w)
    return delta, m, v


def reference(x, positions, mix_norm, w_in, sb_out_norm, ssm_conv_w, ssm_conv_b, ssm_dt_bias, ssm_a_log, ssm_d, ssm_out_norm, mla_q_norm, mla_w_uq, mla_kv_norm, mla_w_ukv, mla_out_norm, w_out, ffn_norm, ffn_w_up, ffn_conv_w, ffn_conv_b, ffn_w_down, final_norm, loss_target, m_mix_norm, m_w_in, m_sb_out_norm, m_ssm_conv_w, m_ssm_conv_b, m_ssm_dt_bias, m_ssm_a_log, m_ssm_d, m_ssm_out_norm, m_mla_q_norm, m_mla_w_uq, m_mla_kv_norm, m_mla_w_ukv, m_mla_out_norm, m_w_out, m_ffn_norm, m_ffn_w_up, m_ffn_conv_w, m_ffn_conv_b, m_ffn_w_down, m_final_norm, v_mix_norm, v_w_in, v_sb_out_norm, v_ssm_conv_w, v_ssm_conv_b, v_ssm_dt_bias, v_ssm_a_log, v_ssm_d, v_ssm_out_norm, v_mla_q_norm, v_mla_w_uq, v_mla_kv_norm, v_mla_w_ukv, v_mla_out_norm, v_w_out, v_ffn_norm, v_ffn_w_up, v_ffn_conv_w, v_ffn_conv_b, v_ffn_w_down, v_final_norm):
    given = dict(x=x, positions=positions, mix_norm=mix_norm, w_in=w_in, sb_out_norm=sb_out_norm, ssm_conv_w=ssm_conv_w, ssm_conv_b=ssm_conv_b, ssm_dt_bias=ssm_dt_bias, ssm_a_log=ssm_a_log, ssm_d=ssm_d, ssm_out_norm=ssm_out_norm, mla_q_norm=mla_q_norm, mla_w_uq=mla_w_uq, mla_kv_norm=mla_kv_norm, mla_w_ukv=mla_w_ukv, mla_out_norm=mla_out_norm, w_out=w_out, ffn_norm=ffn_norm, ffn_w_up=ffn_w_up, ffn_conv_w=ffn_conv_w, ffn_conv_b=ffn_conv_b, ffn_w_down=ffn_w_down, final_norm=final_norm, loss_target=loss_target, m_mix_norm=m_mix_norm, m_w_in=m_w_in, m_sb_out_norm=m_sb_out_norm, m_ssm_conv_w=m_ssm_conv_w, m_ssm_conv_b=m_ssm_conv_b, m_ssm_dt_bias=m_ssm_dt_bias, m_ssm_a_log=m_ssm_a_log, m_ssm_d=m_ssm_d, m_ssm_out_norm=m_ssm_out_norm, m_mla_q_norm=m_mla_q_norm, m_mla_w_uq=m_mla_w_uq, m_mla_kv_norm=m_mla_kv_norm, m_mla_w_ukv=m_mla_w_ukv, m_mla_out_norm=m_mla_out_norm, m_w_out=m_w_out, m_ffn_norm=m_ffn_norm, m_ffn_w_up=m_ffn_w_up, m_ffn_conv_w=m_ffn_conv_w, m_ffn_conv_b=m_ffn_conv_b, m_ffn_w_down=m_ffn_w_down, m_final_norm=m_final_norm, v_mix_norm=v_mix_norm, v_w_in=v_w_in, v_sb_out_norm=v_sb_out_norm, v_ssm_conv_w=v_ssm_conv_w, v_ssm_conv_b=v_ssm_conv_b, v_ssm_dt_bias=v_ssm_dt_bias, v_ssm_a_log=v_ssm_a_log, v_ssm_d=v_ssm_d, v_ssm_out_norm=v_ssm_out_norm, v_mla_q_norm=v_mla_q_norm, v_mla_w_uq=v_mla_w_uq, v_mla_kv_norm=v_mla_kv_norm, v_mla_w_ukv=v_mla_w_ukv, v_mla_out_norm=v_mla_out_norm, v_w_out=v_w_out, v_ffn_norm=v_ffn_norm, v_ffn_w_up=v_ffn_w_up, v_ffn_conv_w=v_ffn_conv_w, v_ffn_conv_b=v_ffn_conv_b, v_ffn_w_down=v_ffn_w_down, v_final_norm=v_final_norm)
    weights = {n: given[n] for n in TWIN_WEIGHTS}
    shared = {n: given[n] for n in SHARED_INPUTS}
    per_example = {n: given[n] for n in ['x', 'positions']}
    grad_fn = _jax.value_and_grad(_loss, argnums=(0, 1))

    def one_microbatch(ex, loss_target):
        ex = dict(ex)
        diff = ex.pop(TWIN_DIFF_INPUT)
        return grad_fn(weights, diff, {**shared, **ex}, loss_target)

    if N_MICROBATCH == 1:
        loss, (grad_w, grad_x) = one_microbatch(per_example, given["loss_target"])
    else:
        def body(carry, xs):
            loss_sum, grad_sum = carry
            l_k, (gw_k, gx_k) = one_microbatch(xs[0], xs[1])
            with _jax.named_scope("update"):
                return (loss_sum + l_k, _jax.tree.map(_jnp.add, grad_sum, gw_k)), gx_k

        init = (_jnp.zeros((), _jnp.float32), _jax.tree.map(_jnp.zeros_like, weights))
        (loss, grad_w), grad_x = _jax.lax.scan(body, init, (per_example, given["loss_target"]))
    with _jax.named_scope("update"):
        delta_w, new_m, new_v = {}, {}, {}
        for n in TWIN_WEIGHTS:
            delta_w[n], new_m[n], new_v[n] = _adamw(weights[n], grad_w[n], given["m_" + n], given["v_" + n])
    return (loss, grad_x, *[grad_w[n] for n in TWIN_WEIGHTS], *[delta_w[n] for n in TWIN_WEIGHTS],
            *[new_m[n] for n in TWIN_WEIGHTS], *[new_v[n] for n in TWIN_WEIGHTS])
```

```python
import functools
import math

import jax
import jax.numpy as jnp
from jax import lax
from jax.experimental import pallas as pl
from jax.experimental.pallas import tpu as pltpu

F32 = jnp.float32
BF16 = jnp.bfloat16

EPS = 1e-6
D_MODEL = 1024
SB_HEADS, SB_DIM = 4, 64
SSM_HEADS, SSM_DIM, SSM_GROUPS, SSM_STATE, SSM_CHUNK = 8, 64, 2, 64, 128
SSM_INNER = SSM_HEADS * SSM_DIM
SSM_CONV_DIM = SSM_INNER + 2 * SSM_GROUPS * SSM_STATE
MLA_HEADS, MLA_NOPE, MLA_ROPE, MLA_V = 4, 64, 32, 64
MLA_QK = MLA_NOPE + MLA_ROPE
MLA_SCALE = MLA_QK ** -0.5
ROPE_THETA = 10000.0
D_FF = 2816
IN_SPLITS = (768, 512, 768, 8, 256, 128, 32)

OFF_SB, OFF_XBC, OFF_Z, OFF_CQ, OFF_CKV, OFF_DT, OFF_KRA, OFF_KRB = 0, 768, 1536, 2048, 2304, 2432, 2560, 2688
D_IN_P = 2816
LANES = 128

ADAM_LR, ADAM_B1, ADAM_B2, ADAM_EPS, ADAM_WD, ADAM_STEP = 0.001, 0.9, 0.999, 1e-08, 0.01, 10

V7X_VMEM_LIMIT = 48 * 1024 * 1024

NT = (((1,), (1,)), ((), ()))
TN = (((0,), (0,)), ((), ()))

BIG = ("w_in", "mla_w_uq", "mla_w_ukv", "w_out", "ffn_w_up", "ffn_w_down")
BIG_AXIS = {"w_in": 2, "mla_w_uq": 2, "mla_w_ukv": 2, "w_out": 1, "ffn_w_up": 2, "ffn_w_down": 1}
SMALL_REPL = ("mix_norm", "sb_out_norm", "ssm_conv_b", "ssm_dt_bias", "ssm_a_log", "ssm_d", "ssm_out_norm",
              "mla_q_norm", "mla_kv_norm", "mla_out_norm", "ffn_norm", "ffn_conv_b", "final_norm")
SMALL_SHARD = ("ssm_conv_w", "ffn_conv_w")
WEIGHTS = ("mix_norm", "w_in", "sb_out_norm", "ssm_conv_w", "ssm_conv_b", "ssm_dt_bias", "ssm_a_log", "ssm_d",
           "ssm_out_norm", "mla_q_norm", "mla_w_uq", "mla_kv_norm", "mla_w_ukv", "mla_out_norm", "w_out", "ffn_norm",
           "ffn_w_up", "ffn_conv_w", "ffn_conv_b", "ffn_w_down", "final_norm")


def _cparams(*sem):
    return pltpu.CompilerParams(dimension_semantics=sem if sem else None, vmem_limit_bytes=V7X_VMEM_LIMIT)


def _pick(n, target, mult=LANES):
    best = None
    for d in range(mult, min(n, target) + 1, mult):
        if n % d == 0:
            best = d
    return best or n


def _sigmoid(x):
    return 1.0 / (1.0 + jnp.exp(-x))


def _softplus(x):
    ax = jnp.where(x > 0, x, -x)
    return jnp.where(x > 0, x, 0.0) + jnp.log(1.0 + jnp.exp(-ax))


def _rms(x, g):
    return x * lax.rsqrt(jnp.mean(x * x, axis=-1, keepdims=True) + EPS) * g


def _raw_nn(a, b):
    return jnp.dot(a.astype(BF16), b.astype(BF16), preferred_element_type=F32)


def _raw_nt(a, b):
    return lax.dot_general(a.astype(BF16), b.astype(BF16), NT, preferred_element_type=F32)


def _raw_tn(a, b):
    return lax.dot_general(a.astype(BF16), b.astype(BF16), TN, preferred_element_type=F32)


@jax.custom_vjp
def mm_nn(a, b):
    return _raw_nn(a, b)


mm_nn.defvjp(lambda a, b: (_raw_nn(a, b), (a, b)),
             lambda r, ct: (_raw_nt(ct, r[1]), _raw_tn(r[0], ct)))


@jax.custom_vjp
def mm_nt(a, b):
    return _raw_nt(a, b)


mm_nt.defvjp(lambda a, b: (_raw_nt(a, b), (a, b)),
             lambda r, ct: (_raw_nn(ct, r[1]), _raw_tn(ct, r[0])))


@jax.custom_vjp
def mm_tn(a, b):
    return _raw_tn(a, b)


mm_tn.defvjp(lambda a, b: (_raw_tn(a, b), (a, b)),
             lambda r, ct: (_raw_nt(r[1], ct), _raw_nn(r[0], ct)))


def _split_dot(x, m, terms):
    acc = None
    r = x
    for t in range(terms):
        xt = r.astype(BF16)
        d = jnp.dot(xt, m, preferred_element_type=F32)
        acc = d if acc is None else acc + d
        if t + 1 < terms:
            r = r - xt.astype(F32)
    return acc


def _tri_dot(tri, x, terms=3):
    acc = None
    r = x
    for t in range(terms):
        xt = r.astype(BF16)
        d = jnp.dot(tri, xt, preferred_element_type=F32)
        acc = d if acc is None else acc + d
        if t + 1 < terms:
            r = r - xt.astype(F32)
    return acc


def _tri(n, cmp):
    r = lax.broadcasted_iota(jnp.int32, (n, n), 0)
    c = lax.broadcasted_iota(jnp.int32, (n, n), 1)
    return cmp(r, c).astype(BF16)


@jax.custom_vjp
def csum_rows(x):
    return _tri_dot(_tri(x.shape[0], lambda r, c: r >= c), x)


csum_rows.defvjp(lambda x: (csum_rows(x), None),
                 lambda _, ct: (_tri_dot(_tri(ct.shape[0], lambda r, c: r <= c), ct),))


def matmul(a, b, *, name, out_dtype=F32, ta=False, residual=None):
    if ta:
        K, M = a.shape
    else:
        M, K = a.shape
    N = b.shape[1]
    tm = _pick(M, 512)
    tn = _pick(N, 512)
    tk = _pick(K, 1408)
    nk = K // tk
    has_res = residual is not None

    def body(*refs):
        if has_res:
            a_ref, b_ref, r_ref, o_ref, acc = refs
        else:
            a_ref, b_ref, o_ref, acc = refs
        k = pl.program_id(2)

        @pl.when(k == 0)
        def _():
            acc[...] = jnp.zeros_like(acc)

        av = a_ref[...].astype(BF16)
        bv = b_ref[...].astype(BF16)
        if ta:
            acc[...] += lax.dot_general(av, bv, TN, preferred_element_type=F32)
        else:
            acc[...] += jnp.dot(av, bv, preferred_element_type=F32)

        @pl.when(k == nk - 1)
        def _():
            r = acc[...]
            if has_res:
                r = r + r_ref[...].astype(F32)
            o_ref[...] = r.astype(o_ref.dtype)

    a_spec = pl.BlockSpec((tk, tm), lambda i, j, k: (k, i)) if ta else pl.BlockSpec((tm, tk), lambda i, j, k: (i, k))
    in_specs = [a_spec, pl.BlockSpec((tk, tn), lambda i, j, k: (k, j))]
    args = [a, b]
    if has_res:
        in_specs.append(pl.BlockSpec((tm, tn), lambda i, j, k: (i, j)))
        args.append(residual)
    return pl.pallas_call(
        body, name=name, grid=(M // tm, N // tn, nk),
        in_specs=in_specs, out_specs=pl.BlockSpec((tm, tn), lambda i, j, k: (i, j)),
        out_shape=jax.ShapeDtypeStruct((M, N), out_dtype),
        scratch_shapes=[pltpu.VMEM((tm, tn), F32)],
        compiler_params=_cparams("parallel", "parallel", "arbitrary"),
    )(*args)


def _row_spec(entry, tl):
    if isinstance(entry, tuple):
        arr, width, cb = entry
        return arr, pl.BlockSpec((tl, width), lambda i, cb=cb: (i, cb))
    return entry, pl.BlockSpec((tl, entry.shape[1]), lambda i: (i, 0))


def _rows_T(entry):
    return (entry[0] if isinstance(entry, tuple) else entry).shape[0]


def rowwise(fn, rows, params, outs, *, name, tile=512, post=None):
    T = _rows_T(rows[0])
    tl = min(T, tile)
    nr, npar = len(rows), len(params)

    def body(*refs):
        r = [ref[...].astype(F32) for ref in refs[:nr]]
        p = [ref[...].astype(F32) for ref in refs[nr:nr + npar]]
        res = fn(*r, *p)
        if post is not None:
            res = post(*res)
        for o_ref, val in zip(refs[nr + npar:], res):
            o_ref[...] = val.astype(o_ref.dtype)

    arrs, specs = [], []
    for e in rows:
        a, s = _row_spec(e, tl)
        arrs.append(a)
        specs.append(s)
    for p in params:
        arrs.append(p)
        specs.append(pl.BlockSpec(p.shape, lambda i: (0, 0)))
    res = pl.pallas_call(
        body, name=name, grid=(T // tl,), in_specs=specs,
        out_specs=[pl.BlockSpec((tl, c), lambda i: (i, 0)) for c, _ in outs],
        out_shape=[jax.ShapeDtypeStruct((T, c), dt) for c, dt in outs],
        compiler_params=_cparams("parallel"),
    )(*arrs)
    return res


def rowwise_bwd(fn, rows, nd_rows, params, cts, grad_dtypes, *, name, tile=512, pre_ct=None, add0=None):
    T = _rows_T(rows[0])
    tl = min(T, tile)
    nr, nn, npar, nc = len(rows), len(nd_rows), len(params), len(cts)
    has_add = add0 is not None

    def body(*refs):
        pos = 0
        r = [ref[...].astype(F32) for ref in refs[pos:pos + nr]]
        pos += nr
        nd = [ref[...].astype(F32) for ref in refs[pos:pos + nn]]
        pos += nn
        p = [ref[...].astype(F32) for ref in refs[pos:pos + npar]]
        pos += npar
        c = [ref[...].astype(F32) for ref in refs[pos:pos + nc]]
        pos += nc
        if has_add:
            addv = refs[pos][...].astype(F32)
            pos += 1
        rg_refs = refs[pos:pos + nr]
        pg_refs = refs[pos + nr:pos + nr + npar]
        if pre_ct is not None:
            c = list(pre_ct(*c))
        _, vjp = jax.vjp(lambda *a: fn(*a[:nr], *nd, *a[nr:]), *r, *p)
        g = vjp(tuple(c))
        for j, ref in enumerate(rg_refs):
            val = g[j]
            if has_add and j == 0:
                val = val + addv
            ref[...] = val.astype(ref.dtype)
        if npar:
            @pl.when(pl.program_id(0) == 0)
            def _():
                for ref in pg_refs:
                    ref[...] = jnp.zeros_like(ref)
            for j, ref in enumerate(pg_refs):
                ref[...] += g[nr + j]

    arrs, specs = [], []
    widths = []
    for e in list(rows) + list(nd_rows):
        a, s = _row_spec(e, tl)
        arrs.append(a)
        specs.append(s)
        widths.append(s.block_shape[1])
    for p in params:
        arrs.append(p)
        specs.append(pl.BlockSpec(p.shape, lambda i: (0, 0)))
    for e in cts:
        a, s = _row_spec(e, tl)
        arrs.append(a)
        specs.append(s)
    if has_add:
        a, s = _row_spec(add0, tl)
        arrs.append(a)
        specs.append(s)
    out_specs = [pl.BlockSpec((tl, widths[j]), lambda i: (i, 0)) for j in range(nr)]
    out_shape = [jax.ShapeDtypeStruct((T, widths[j]), grad_dtypes[j]) for j in range(nr)]
    out_specs += [pl.BlockSpec(p.shape, lambda i: (0, 0)) for p in params]
    out_shape += [jax.ShapeDtypeStruct(p.shape, F32) for p in params]
    res = pl.pallas_call(
        body, name=name, grid=(T // tl,), in_specs=specs, out_specs=out_specs, out_shape=out_shape,
        compiler_params=_cparams("arbitrary"),
    )(*arrs)
    return list(res[:nr]), list(res[nr:])


def rms_fn(h, g):
    return (_rms(h, g),)


def merge_fn(ysb, yssm, z, ymla, g_sb, g_ssm, g_mla):
    ya = _rms(ysb, g_sb)
    yb = _rms(yssm * (z * _sigmoid(z)), g_ssm)
    yc = _rms(ymla, g_mla)
    return ya, yb, yc


def mla_prep_fn(cq, ckv, kra, krb, cos, sin, qn, kvn, wqp, wqr, wkp, wvp):
    cos4 = jnp.concatenate([cos] * MLA_HEADS, axis=1)
    sin4 = jnp.concatenate([sin] * MLA_HEADS, axis=1)
    nq = _rms(cq, qn)
    q = (mm_nn(nq, wqp) * cos4 + mm_nn(nq, wqr) * sin4) * MLA_SCALE
    nkv = _rms(ckv, kvn)
    kpe = kra * cos + krb * sin
    k = mm_nn(nkv, wkp) + jnp.concatenate([kpe] * MLA_HEADS, axis=1)
    v = mm_nn(nkv, wvp)
    return q, k, v


HALO = 8


def _prev_halo_spec(tl, tc, col_of):
    return pl.BlockSpec((HALO, tc), lambda i, j: (jnp.maximum(i * (tl // HALO) - 1, 0), col_of(j)))


def _fill_prev(buf, x_ref, halo_ref, i):
    buf[0:HALO, :] = jnp.where(i > 0, halo_ref[...].astype(F32), 0.0)
    buf[HALO:, :] = x_ref[...].astype(F32)


def _conv_from(buf, w_ref, b_ref, K, tl):
    acc = b_ref[...].astype(F32) + jnp.zeros((tl, buf.shape[1]), F32)
    for k in range(K):
        acc = acc + buf[pl.ds(HALO - (K - 1 - k), tl), :] * w_ref[k:k + 1, :].astype(F32)
    return acc


def ssm_conv_act(proj, w, b, *, name, tile=512, tc=256):
    T = proj.shape[0]
    K, C = w.shape
    tl = min(T, tile)
    c0 = OFF_XBC // tc

    def body(x_ref, halo_ref, w_ref, b_ref, o_ref, buf):
        _fill_prev(buf, x_ref, halo_ref, pl.program_id(0))
        u = _conv_from(buf, w_ref, b_ref, K, tl)
        o_ref[...] = u * _sigmoid(u)

    return pl.pallas_call(
        body, name=name, grid=(T // tl, C // tc),
        in_specs=[pl.BlockSpec((tl, tc), lambda i, j: (i, c0 + j)), _prev_halo_spec(tl, tc, lambda j: c0 + j),
                  pl.BlockSpec((K, tc), lambda i, j: (0, j)), pl.BlockSpec((1, tc), lambda i, j: (0, j))],
        out_specs=pl.BlockSpec((tl, tc), lambda i, j: (i, j)),
        out_shape=jax.ShapeDtypeStruct((T, C), F32),
        scratch_shapes=[pltpu.VMEM((tl + HALO, tc), F32)],
        compiler_params=_cparams("parallel", "parallel"),
    )(proj, proj, w, b)


def ssm_conv_bwd_a(proj, w, b, d_out, *, name, tile=512, tc=256):
    T = proj.shape[0]
    K, C = w.shape
    tl = min(T, tile)
    c0 = OFF_XBC // tc

    def body(x_ref, halo_ref, w_ref, b_ref, d_ref, o_ref, buf):
        _fill_prev(buf, x_ref, halo_ref, pl.program_id(0))
        u = _conv_from(buf, w_ref, b_ref, K, tl)
        s = _sigmoid(u)
        o_ref[...] = d_ref[...].astype(F32) * (s * (1.0 + u * (1.0 - s)))

    return pl.pallas_call(
        body, name=name, grid=(T // tl, C // tc),
        in_specs=[pl.BlockSpec((tl, tc), lambda i, j: (i, c0 + j)), _prev_halo_spec(tl, tc, lambda j: c0 + j),
                  pl.BlockSpec((K, tc), lambda i, j: (0, j)), pl.BlockSpec((1, tc), lambda i, j: (0, j)),
                  pl.BlockSpec((tl, tc), lambda i, j: (i, j))],
        out_specs=pl.BlockSpec((tl, tc), lambda i, j: (i, j)),
        out_shape=jax.ShapeDtypeStruct((T, C), F32),
        scratch_shapes=[pltpu.VMEM((tl + HALO, tc), F32)],
        compiler_params=_cparams("parallel", "parallel"),
    )(proj, proj, w, b, d_out)


def ffn_act(up, w, b, *, name, tile=512, tc=1408):
    T = up.shape[0]
    K = w.shape[0]
    tl = min(T, tile)
    nj = D_FF // tc

    def body(xg_ref, hg_ref, xv_ref, hv_ref, wg_ref, wv_ref, bg_ref, bv_ref, o_ref, bufg, bufv):
        i = pl.program_id(0)
        _fill_prev(bufg, xg_ref, hg_ref, i)
        _fill_prev(bufv, xv_ref, hv_ref, i)
        gate = _conv_from(bufg, wg_ref, bg_ref, K, tl)
        val = _conv_from(bufv, wv_ref, bv_ref, K, tl)
        o_ref[...] = (gate * _sigmoid(gate) * val).astype(o_ref.dtype)

    return pl.pallas_call(
        body, name=name, grid=(T // tl, nj),
        in_specs=[pl.BlockSpec((tl, tc), lambda i, j: (i, j)), _prev_halo_spec(tl, tc, lambda j: j),
                  pl.BlockSpec((tl, tc), lambda i, j: (i, nj + j)), _prev_halo_spec(tl, tc, lambda j: nj + j),
                  pl.BlockSpec((K, tc), lambda i, j: (0, j)), pl.BlockSpec((K, tc), lambda i, j: (0, nj + j)),
                  pl.BlockSpec((1, tc), lambda i, j: (0, j)), pl.BlockSpec((1, tc), lambda i, j: (0, nj + j))],
        out_specs=pl.BlockSpec((tl, tc), lambda i, j: (i, j)),
        out_shape=jax.ShapeDtypeStruct((T, D_FF), BF16),
        scratch_shapes=[pltpu.VMEM((tl + HALO, tc), F32), pltpu.VMEM((tl + HALO, tc), F32)],
        compiler_params=_cparams("parallel", "parallel"),
    )(up, up, up, up, w, w, b, b)


def ffn_act_bwd_a(up, w, b, d_act, *, name, tile=512, tc=1408):
    T = up.shape[0]
    K = w.shape[0]
    tl = min(T, tile)
    nj = D_FF // tc

    def body(xs_ref, hs_ref, xp_ref, hp_ref, ws_ref, wp_ref, bs_ref, bp_ref, d_ref, o_ref, bufs, bufp):
        i = pl.program_id(0)
        is_gate = pl.program_id(1) < nj
        _fill_prev(bufs, xs_ref, hs_ref, i)
        _fill_prev(bufp, xp_ref, hp_ref, i)
        us = _conv_from(bufs, ws_ref, bs_ref, K, tl)
        up_ = _conv_from(bufp, wp_ref, bp_ref, K, tl)
        gate = jnp.where(is_gate, us, up_)
        val = jnp.where(is_gate, up_, us)
        d = d_ref[...].astype(F32)
        s = _sigmoid(gate)
        o_ref[...] = jnp.where(is_gate, d * val * (s * (1.0 + gate * (1.0 - s))), d * (gate * s))

    par = lambda j: (j + nj) % (2 * nj)
    return pl.pallas_call(
        body, name=name, grid=(T // tl, 2 * nj),
        in_specs=[pl.BlockSpec((tl, tc), lambda i, j: (i, j)), _prev_halo_spec(tl, tc, lambda j: j),
                  pl.BlockSpec((tl, tc), lambda i, j: (i, par(j))), _prev_halo_spec(tl, tc, par),
                  pl.BlockSpec((K, tc), lambda i, j: (0, j)), pl.BlockSpec((K, tc), lambda i, j: (0, par(j))),
                  pl.BlockSpec((1, tc), lambda i, j: (0, j)), pl.BlockSpec((1, tc), lambda i, j: (0, par(j))),
                  pl.BlockSpec((tl, tc), lambda i, j: (i, j % nj))],
        out_specs=pl.BlockSpec((tl, tc), lambda i, j: (i, j)),
        out_shape=jax.ShapeDtypeStruct((T, 2 * D_FF), F32),
        scratch_shapes=[pltpu.VMEM((tl + HALO, tc), F32), pltpu.VMEM((tl + HALO, tc), F32)],
        compiler_params=_cparams("parallel", "parallel"),
    )(up, up, up, up, w, w, b, b, d_act)


def conv_bwd_b(du, x, x_off, w, *, name, out_dtype, tile=512, tc=256):
    T, C = du.shape
    K = w.shape[0]
    tl = min(T, tile)
    c0 = x_off // tc
    nblk = T // HALO

    def body(du_ref, nx_ref, x_ref, halo_ref, w_ref, dx_ref, dw_ref, db_ref, dbuf, xbuf):
        i = pl.program_id(1)
        last = pl.num_programs(1) - 1
        d = du_ref[...].astype(F32)
        dbuf[0:tl, :] = d
        dbuf[tl:, :] = jnp.where(i < last, nx_ref[...].astype(F32), 0.0)
        _fill_prev(xbuf, x_ref, halo_ref, i)

        @pl.when(i == 0)
        def _():
            dw_ref[...] = jnp.zeros_like(dw_ref)
            db_ref[...] = jnp.zeros_like(db_ref)

        dx = jnp.zeros((tl, tc), F32)
        for k in range(K):
            s = K - 1 - k
            dx = dx + dbuf[pl.ds(s, tl), :] * w_ref[k:k + 1, :].astype(F32)
            dw_ref[k:k + 1, :] += jnp.sum(d * xbuf[pl.ds(HALO - s, tl), :], axis=0, keepdims=True)
        db_ref[...] += jnp.sum(d, axis=0, keepdims=True)
        dx_ref[...] = dx.astype(dx_ref.dtype)

    return pl.pallas_call(
        body, name=name, grid=(C // tc, T // tl),
        in_specs=[pl.BlockSpec((tl, tc), lambda j, i: (i, j)),
                  pl.BlockSpec((HALO, tc), lambda j, i: (jnp.minimum((i + 1) * (tl // HALO), nblk - 1), j)),
                  pl.BlockSpec((tl, tc), lambda j, i: (i, c0 + j)),
                  pl.BlockSpec((HALO, tc), lambda j, i: (jnp.maximum(i * (tl // HALO) - 1, 0), c0 + j)),
                  pl.BlockSpec((K, tc), lambda j, i: (0, j))],
        out_specs=[pl.BlockSpec((tl, tc), lambda j, i: (i, j)), pl.BlockSpec((K, tc), lambda j, i: (0, j)),
                   pl.BlockSpec((1, tc), lambda j, i: (0, j))],
        out_shape=[jax.ShapeDtypeStruct((T, C), out_dtype), jax.ShapeDtypeStruct((K, C), F32),
                   jax.ShapeDtypeStruct((1, C), F32)],
        scratch_shapes=[pltpu.VMEM((tl + HALO, tc), F32), pltpu.VMEM((tl + HALO, tc), F32)],
        compiler_params=_cparams("parallel", "arbitrary"),
    )(du, du, x, x, w)


def _attn_tile(T):
    return min(T, 256)


def _log_gates(z):
    az = jnp.where(z > 0, z, -z)
    l1p = jnp.log(1.0 + jnp.exp(-az))
    a = jnp.where(z > 0, 0.0, z) - l1p
    return a, a - z


def sb_fwd(q, k, v, *, name):
    H, T, dh = q.shape
    tq = _attn_tile(T)

    def body(q_ref, k_ref, v_ref, y_ref, bt_ref):
        i = pl.program_id(1)
        qv = q_ref[...]
        dmat = lax.broadcasted_iota(jnp.int32, (tq, tq), 0) - lax.broadcasted_iota(jnp.int32, (tq, tq), 1)
        u_after = _tri(tq, lambda r, c: r > c)

        def step(j, carry):
            acc, run = carry
            k0 = pl.multiple_of((i - j) * tq, tq)
            kb = k_ref[pl.ds(k0, tq), :]
            vb = v_ref[pl.ds(k0, tq), :]
            z = lax.dot_general(qv, kb, NT, preferred_element_type=F32)
            valid = dmat > -(j * tq)
            a, b = _log_gates(z)
            b = jnp.where(valid, b, 0.0)
            later = _split_dot(b, u_after, 2)
            w = jnp.where(valid, jnp.exp(a + later + run), 0.0)
            acc = acc + jnp.dot(w.astype(BF16), vb, preferred_element_type=F32)
            run = run + jnp.sum(b, axis=1, keepdims=True)
            return acc, run

        acc, run = lax.fori_loop(0, i + 1, step, (jnp.zeros((tq, dh), F32), jnp.zeros((tq, 1), F32)))
        y_ref[...] = acc
        bt_ref[...] = jnp.broadcast_to(run, (tq, LANES))

    return pl.pallas_call(
        body, name=name, grid=(H, T // tq),
        in_specs=[pl.BlockSpec((None, tq, dh), lambda h, i: (h, i, 0)),
                  pl.BlockSpec((None, T, dh), lambda h, i: (h, 0, 0)),
                  pl.BlockSpec((None, T, dh), lambda h, i: (h, 0, 0))],
        out_specs=[pl.BlockSpec((None, tq, dh), lambda h, i: (h, i, 0)),
                   pl.BlockSpec((None, tq, LANES), lambda h, i: (h, i, 0))],
        out_shape=[jax.ShapeDtypeStruct((H, T, dh), F32), jax.ShapeDtypeStruct((H, T, LANES), F32)],
        compiler_params=_cparams("parallel", "parallel"),
    )(q, k, v)


def sb_bwd(q, k, v, dy, btot, *, name, q_scale):
    H, T, dh = q.shape
    tq = _attn_tile(T)

    def body(q_ref, k_ref, v_ref, dy_ref, bt_ref, dq_ref, dk_ref, dv_ref):
        i = pl.program_id(1)

        @pl.when(i == 0)
        def _():
            dk_ref[...] = jnp.zeros_like(dk_ref)
            dv_ref[...] = jnp.zeros_like(dv_ref)

        qv = q_ref[...]
        dyb = dy_ref[...].astype(BF16)
        btv = bt_ref[:, 0:1]
        dmat = lax.broadcasted_iota(jnp.int32, (tq, tq), 0) - lax.broadcasted_iota(jnp.int32, (tq, tq), 1)
        u_upto = _tri(tq, lambda r, c: r <= c)
        u_before = _tri(tq, lambda r, c: r < c)

        def step(kb_i, carry):
            dq, pb, pg = carry
            k0 = pl.multiple_of(kb_i * tq, tq)
            kb = k_ref[pl.ds(k0, tq), :]
            vb = v_ref[pl.ds(k0, tq), :]
            z = lax.dot_general(qv, kb, NT, preferred_element_type=F32)
            valid = dmat > (kb_i - i) * tq
            a, b = _log_gates(z)
            b = jnp.where(valid, b, 0.0)
            later = btv - pb - _split_dot(b, u_upto, 2)
            w = jnp.where(valid, jnp.exp(a + later), 0.0)
            dw = lax.dot_general(dyb, vb, NT, preferred_element_type=F32)
            g = w * dw
            gsum = pg + _split_dot(g, u_before, 2)
            dz = jnp.where(valid, g - jnp.exp(a) * (g + gsum), 0.0).astype(BF16)
            dq = dq + jnp.dot(dz, kb, preferred_element_type=F32)
            dk_ref[pl.ds(k0, tq), :] += lax.dot_general(dz, qv, TN, preferred_element_type=F32)
            dv_ref[pl.ds(k0, tq), :] += lax.dot_general(w.astype(BF16), dyb, TN, preferred_element_type=F32)
            pb = pb + jnp.sum(b, axis=1, keepdims=True)
            pg = pg + jnp.sum(g, axis=1, keepdims=True)
            return dq, pb, pg

        z1 = jnp.zeros((tq, 1), F32)
        dq, _, _ = lax.fori_loop(0, i + 1, step, (jnp.zeros((tq, dh), F32), z1, z1))
        dq_ref[...] = dq * q_scale

    return pl.pallas_call(
        body, name=name, grid=(H, T // tq),
        in_specs=[pl.BlockSpec((None, tq, dh), lambda h, i: (h, i, 0)),
                  pl.BlockSpec((None, T, dh), lambda h, i: (h, 0, 0)),
                  pl.BlockSpec((None, T, dh), lambda h, i: (h, 0, 0)),
                  pl.BlockSpec((None, tq, dh), lambda h, i: (h, i, 0)),
                  pl.BlockSpec((None, tq, LANES), lambda h, i: (h, i, 0))],
        out_specs=[pl.BlockSpec((None, tq, dh), lambda h, i: (h, i, 0)),
                   pl.BlockSpec((None, T, dh), lambda h, i: (h, 0, 0)),
                   pl.BlockSpec((None, T, dh), lambda h, i: (h, 0, 0))],
        out_shape=[jax.ShapeDtypeStruct((H, T, dh), F32)] * 3,
        compiler_params=_cparams("parallel", "arbitrary"),
    )(q, k, v, dy, btot)


NEG = -1e30


def mla_fwd(q, k, v, *, name):
    H, T, dk = q.shape
    dv = v.shape[2]
    tq = _attn_tile(T)

    def body(q_ref, k_ref, v_ref, o_ref, l_ref):
        i = pl.program_id(1)
        qv = q_ref[...]
        dmat = lax.broadcasted_iota(jnp.int32, (tq, tq), 0) - lax.broadcasted_iota(jnp.int32, (tq, tq), 1)

        def step(kb_i, carry):
            acc, m, l = carry
            k0 = pl.multiple_of(kb_i * tq, tq)
            kb = k_ref[pl.ds(k0, tq), :]
            vb = v_ref[pl.ds(k0, tq), :]
            s = lax.dot_general(qv, kb, NT, preferred_element_type=F32)
            s = jnp.where(dmat >= (kb_i - i) * tq, s, NEG)
            m_new = jnp.maximum(m, jnp.max(s, axis=1, keepdims=True))
            p = jnp.exp(s - m_new)
            alpha = jnp.exp(m - m_new)
            l = alpha * l + jnp.sum(p, axis=1, keepdims=True)
            acc = alpha * acc + jnp.dot(p.astype(BF16), vb, preferred_element_type=F32)
            return acc, m_new, l

        acc, m, l = lax.fori_loop(0, i + 1, step, (jnp.zeros((tq, dv), F32), jnp.full((tq, 1), NEG, F32),
                                                  jnp.zeros((tq, 1), F32)))
        o_ref[...] = acc / l
        l_ref[...] = jnp.broadcast_to(m + jnp.log(l), (tq, LANES))

    return pl.pallas_call(
        body, name=name, grid=(H, T // tq),
        in_specs=[pl.BlockSpec((None, tq, dk), lambda h, i: (h, i, 0)),
                  pl.BlockSpec((None, T, dk), lambda h, i: (h, 0, 0)),
                  pl.BlockSpec((None, T, dv), lambda h, i: (h, 0, 0))],
        out_specs=[pl.BlockSpec((None, tq, dv), lambda h, i: (h, i, 0)),
                   pl.BlockSpec((None, tq, LANES), lambda h, i: (h, i, 0))],
        out_shape=[jax.ShapeDtypeStruct((H, T, dv), F32), jax.ShapeDtypeStruct((H, T, LANES), F32)],
        compiler_params=_cparams("parallel", "parallel"),
    )(q, k, v)


def mla_bwd(q, k, v, do, o, lse, *, name):
    H, T, dk = q.shape
    dv = v.shape[2]
    tq = _attn_tile(T)

    def body(q_ref, k_ref, v_ref, do_ref, o_ref, l_ref, dq_ref, dk_ref, dv_ref):
        i = pl.program_id(1)

        @pl.when(i == 0)
        def _():
            dk_ref[...] = jnp.zeros_like(dk_ref)
            dv_ref[...] = jnp.zeros_like(dv_ref)

        qv = q_ref[...]
        dov = do_ref[...].astype(F32)
        dob = dov.astype(BF16)
        delta = jnp.sum(dov * o_ref[...], axis=1, keepdims=True)
        lsev = l_ref[:, 0:1]
        dmat = lax.broadcasted_iota(jnp.int32, (tq, tq), 0) - lax.broadcasted_iota(jnp.int32, (tq, tq), 1)

        def step(kb_i, dq):
            k0 = pl.multiple_of(kb_i * tq, tq)
            kb = k_ref[pl.ds(k0, tq), :]
            vb = v_ref[pl.ds(k0, tq), :]
            s = lax.dot_general(qv, kb, NT, preferred_element_type=F32)
            p = jnp.where(dmat >= (kb_i - i) * tq, jnp.exp(s - lsev), 0.0)
            dp = lax.dot_general(dob, vb, NT, preferred_element_type=F32)
            ds = (p * (dp - delta)).astype(BF16)
            dq = dq + jnp.dot(ds, kb, preferred_element_type=F32)
            dk_ref[pl.ds(k0, tq), :] += lax.dot_general(ds, qv, TN, preferred_element_type=F32)
            dv_ref[pl.ds(k0, tq), :] += lax.dot_general(p.astype(BF16), dob, TN, preferred_element_type=F32)
            return dq

        dq_ref[...] = lax.fori_loop(0, i + 1, step, jnp.zeros((tq, dk), F32))

    return pl.pallas_call(
        body, name=name, grid=(H, T // tq),
        in_specs=[pl.BlockSpec((None, tq, dk), lambda h, i: (h, i, 0)),
                  pl.BlockSpec((None, T, dk), lambda h, i: (h, 0, 0)),
                  pl.BlockSpec((None, T, dv), lambda h, i: (h, 0, 0)),
                  pl.BlockSpec((None, tq, dv), lambda h, i: (h, i, 0)),
                  pl.BlockSpec((None, tq, dv), lambda h, i: (h, i, 0)),
                  pl.BlockSpec((None, tq, LANES), lambda h, i: (h, i, 0))],
        out_specs=[pl.BlockSpec((None, tq, dk), lambda h, i: (h, i, 0)),
                   pl.BlockSpec((None, T, dk), lambda h, i: (h, 0, 0)),
                   pl.BlockSpec((None, T, dv), lambda h, i: (h, 0, 0))],
        out_shape=[jax.ShapeDtypeStruct((H, T, dk), F32), jax.ShapeDtypeStruct((H, T, dk), F32),
                   jax.ShapeDtypeStruct((H, T, dv), F32)],
        compiler_params=_cparams("parallel", "arbitrary"),
    )(q, k, v, do, o, lse)


def _lane_pick(x, h):
    lane = lax.broadcasted_iota(jnp.int32, (1, x.shape[1]), 1)
    return jnp.sum(jnp.where(lane == h, x, 0.0), axis=1, keepdims=True)


def _row_pick(x, h):
    sub = lax.broadcasted_iota(jnp.int32, (x.shape[0], 1), 0)
    return jnp.sum(jnp.where(sub == h, x, 0.0), axis=0, keepdims=True)


def ssd_chunk_fn(*args):
    nh, ng = SSM_HEADS, SSM_GROUPS
    xs = args[:nh]
    bs = args[nh:nh + ng]
    cs = args[nh + ng:nh + 2 * ng]
    dt_raw = args[nh + 2 * ng]
    st = args[nh + 2 * ng + 1:nh + 2 * ng + 1 + nh]
    dt_bias, a_log, d_skip = args[nh + 2 * ng + 1 + nh:]
    L = dt_raw.shape[0]
    dt = _softplus(dt_raw + dt_bias)
    da = dt * (-jnp.exp(a_log))
    dcs = csum_rows(da)
    dcs_t = dcs.T
    total = jnp.sum(da, axis=0, keepdims=True)
    causal = lax.broadcasted_iota(jnp.int32, (L, L), 0) >= lax.broadcasted_iota(jnp.int32, (L, L), 1)
    cb = [mm_nt(cs[g], bs[g]) for g in range(ng)]
    ys, new_st = [], []
    for h in range(nh):
        g = h // (nh // ng)
        dcs_h = _lane_pick(dcs, h)
        dt_h = _lane_pick(dt, h)
        tot_h = _lane_pick(total, h)
        dsk_h = _lane_pick(d_skip, h)
        decay = jnp.exp(jnp.where(causal, dcs_h - _row_pick(dcs_t, h), NEG))
        xdt = xs[h] * dt_h
        y = mm_nn(cb[g] * decay, xdt)
        y = y + mm_nn(cs[g] * jnp.exp(dcs_h), st[h])
        ys.append(y + xs[h] * dsk_h)
        new_st.append(st[h] * jnp.exp(tot_h) + mm_tn(bs[g] * jnp.exp(tot_h - dcs_h), xdt))
    return tuple(ys) + tuple(new_st)


def ssd_fwd(x_hm, b_hm, c_hm, proj, dt_bias, a_log, d_skip, *, name):
    nh, T, P = x_hm.shape
    ng, N = b_hm.shape[0], b_hm.shape[2]
    L = SSM_CHUNK
    nc = T // L
    dtb = OFF_DT // LANES

    def body(x_ref, b_ref, c_ref, dt_ref, db_ref, al_ref, ds_ref, y_ref, s_ref, state):
        @pl.when(pl.program_id(0) == 0)
        def _():
            state[...] = jnp.zeros_like(state)

        s_ref[...] = state[...]
        args = ([x_ref[h] for h in range(nh)] + [b_ref[g] for g in range(ng)] + [c_ref[g] for g in range(ng)]
                + [dt_ref[...]] + [state[h] for h in range(nh)] + [db_ref[...], al_ref[...], ds_ref[...]])
        res = ssd_chunk_fn(*args)
        for h in range(nh):
            y_ref[h] = res[h]
            state[h] = res[nh + h]

    par = pl.BlockSpec((1, LANES), lambda i: (0, 0))
    return pl.pallas_call(
        body, name=name, grid=(nc,),
        in_specs=[pl.BlockSpec((nh, L, P), lambda i: (0, i, 0)), pl.BlockSpec((ng, L, N), lambda i: (0, i, 0)),
                  pl.BlockSpec((ng, L, N), lambda i: (0, i, 0)), pl.BlockSpec((L, LANES), lambda i: (i, dtb)),
                  par, par, par],
        out_specs=[pl.BlockSpec((nh, L, P), lambda i: (0, i, 0)),
                   pl.BlockSpec((None, nh, N, P), lambda i: (i, 0, 0, 0))],
        out_shape=[jax.ShapeDtypeStruct((nh, T, P), F32), jax.ShapeDtypeStruct((nc, nh, N, P), F32)],
        scratch_shapes=[pltpu.VMEM((nh, N, P), F32)],
        compiler_params=_cparams("arbitrary"),
    )(x_hm, b_hm, c_hm, proj, dt_bias, a_log, d_skip)


def ssd_bwd(x_hm, b_hm, c_hm, proj, states, dt_bias, a_log, d_skip, dy_hm, *, name):
    nh, T, P = x_hm.shape
    ng, N = b_hm.shape[0], b_hm.shape[2]
    L = SSM_CHUNK
    nc = T // L
    dtb = OFF_DT // LANES

    def body(x_ref, b_ref, c_ref, dt_ref, s_ref, db_ref, al_ref, ds_ref, dy_ref,
             dx_ref, dbm_ref, dcm_ref, ddt_ref, gdb_ref, gal_ref, gds_ref, dstate):
        @pl.when(pl.program_id(0) == 0)
        def _():
            dstate[...] = jnp.zeros_like(dstate)
            gdb_ref[...] = jnp.zeros_like(gdb_ref)
            gal_ref[...] = jnp.zeros_like(gal_ref)
            gds_ref[...] = jnp.zeros_like(gds_ref)

        args = ([x_ref[h] for h in range(nh)] + [b_ref[g] for g in range(ng)] + [c_ref[g] for g in range(ng)]
                + [dt_ref[...]] + [s_ref[h] for h in range(nh)] + [db_ref[...], al_ref[...], ds_ref[...]])
        _, vjp = jax.vjp(ssd_chunk_fn, *args)
        g = vjp(tuple([dy_ref[h] for h in range(nh)] + [dstate[h] for h in range(nh)]))
        for h in range(nh):
            dx_ref[h] = g[h]
        for gi in range(ng):
            dbm_ref[gi] = g[nh + gi]
            dcm_ref[gi] = g[nh + ng + gi]
        ddt_ref[...] = g[nh + 2 * ng]
        for h in range(nh):
            dstate[h] = g[nh + 2 * ng + 1 + h]
        gdb_ref[...] += g[-3]
        gal_ref[...] += g[-2]
        gds_ref[...] += g[-1]

    rev = lambda i: nc - 1 - i
    par = pl.BlockSpec((1, LANES), lambda i: (0, 0))
    return pl.pallas_call(
        body, name=name, grid=(nc,),
        in_specs=[pl.BlockSpec((nh, L, P), lambda i: (0, rev(i), 0)), pl.BlockSpec((ng, L, N), lambda i: (0, rev(i), 0)),
                  pl.BlockSpec((ng, L, N), lambda i: (0, rev(i), 0)), pl.BlockSpec((L, LANES), lambda i: (rev(i), dtb)),
                  pl.BlockSpec((None, nh, N, P), lambda i: (rev(i), 0, 0, 0)), par, par, par,
                  pl.BlockSpec((nh, L, P), lambda i: (0, rev(i), 0))],
        out_specs=[pl.BlockSpec((nh, L, P), lambda i: (0, rev(i), 0)), pl.BlockSpec((ng, L, N), lambda i: (0, rev(i), 0)),
                   pl.BlockSpec((ng, L, N), lambda i: (0, rev(i), 0)), pl.BlockSpec((L, LANES), lambda i: (rev(i), 0)),
                   par, par, par],
        out_shape=[jax.ShapeDtypeStruct((nh, T, P), F32), jax.ShapeDtypeStruct((ng, T, N), F32),
                   jax.ShapeDtypeStruct((ng, T, N), F32), jax.ShapeDtypeStruct((T, LANES), F32),
                   jax.ShapeDtypeStruct((1, LANES), F32), jax.ShapeDtypeStruct((1, LANES), F32),
                   jax.ShapeDtypeStruct((1, LANES), F32)],
        scratch_shapes=[pltpu.VMEM((nh, N, P), F32)],
        compiler_params=_cparams("arbitrary"),
    )(x_hm, b_hm, c_hm, proj, states, dt_bias, a_log, d_skip, dy_hm)


def loss_head(h, target, g, *, name, tile=512):
    T, C = h.shape
    tl = min(T, tile)

    def body(h_ref, t_ref, g_ref, dh_ref, dg_ref, ls_ref):
        @pl.when(pl.program_id(0) == 0)
        def _():
            dg_ref[...] = jnp.zeros_like(dg_ref)
            ls_ref[...] = jnp.zeros_like(ls_ref)

        (y,), vjp = jax.vjp(rms_fn, h_ref[...], g_ref[...])
        err = y - t_ref[...]
        ls_ref[...] += jnp.sum(err * err, axis=0, keepdims=True) * (0.5 / C)
        dh, dg = vjp((err * (1.0 / C),))
        dh_ref[...] = dh
        dg_ref[...] += dg

    row = pl.BlockSpec((tl, C), lambda i: (i, 0))
    par = pl.BlockSpec((1, C), lambda i: (0, 0))
    return pl.pallas_call(
        body, name=name, grid=(T // tl,), in_specs=[row, row, par], out_specs=[row, par, par],
        out_shape=[jax.ShapeDtypeStruct((T, C), F32), jax.ShapeDtypeStruct((1, C), F32),
                   jax.ShapeDtypeStruct((1, C), F32)],
        compiler_params=_cparams("arbitrary"),
    )(h, target, g)


def adamw(w, g, m, v, *, name):
    R, C = w.shape
    tr = R
    for d in range(8, min(R, 512) + 1, 8):
        if R % d == 0:
            tr = d
    c1 = 1.0 - ADAM_B1 ** ADAM_STEP
    c2 = 1.0 - ADAM_B2 ** ADAM_STEP

    def body(w_ref, g_ref, m_ref, v_ref, d_ref, nm_ref, nv_ref):
        gv = g_ref[...]
        nm = ADAM_B1 * m_ref[...] + (1.0 - ADAM_B1) * gv
        nv = ADAM_B2 * v_ref[...] + (1.0 - ADAM_B2) * (gv * gv)
        d_ref[...] = -ADAM_LR * ((nm / c1) / (jnp.sqrt(nv / c2) + ADAM_EPS) + ADAM_WD * w_ref[...])
        nm_ref[...] = nm
        nv_ref[...] = nv

    spec = pl.BlockSpec((tr, C), lambda i: (i, 0))
    return pl.pallas_call(
        body, name=name, grid=(R // tr,), in_specs=[spec] * 4, out_specs=[spec] * 3,
        out_shape=[jax.ShapeDtypeStruct((R, C), F32)] * 3,
        compiler_params=_cparams("parallel"),
    )(w, g, m, v)


MESH = pl.DeviceIdType.MESH
HBM_SPEC = pl.BlockSpec(memory_space=pltpu.HBM)


def _place():
    return lax.axis_index("x"), lax.axis_index("y"), lax.axis_index("c")


def allgather_blocks(mine, *, name):
    R = mine.shape[0]

    def body(x_ref, out_ref, send_sems, recv_sems, local_sem):
        x, y, c = _place()
        me, sibling = (x, y, c), (x, y, 1 - c)
        chips = [(1 - x, y), (x, 1 - y), (1 - x, 1 - y)]

        def slot(px, py, pc):
            return out_ref.at[4 * px + 2 * py + pc]

        def copy(k, block, to, src=None):
            return pltpu.make_async_remote_copy(
                src_ref=slot(*block) if src is None else src, dst_ref=slot(*block),
                send_sem=send_sems.at[k], recv_sem=recv_sems.at[k], device_id=to, device_id_type=MESH)

        own = pltpu.make_async_copy(x_ref, slot(*me), local_sem)
        own.start()
        first = [copy(0, me, sibling, src=x_ref)]
        first += [copy(1 + j, me, (*chip, c), src=x_ref) for j, chip in enumerate(chips)]
        for cp in first:
            cp.start()
        passed = [copy(4 + j, (*chip, c), sibling) for j, chip in enumerate(chips)]
        for j, chip in enumerate(chips):
            copy(1 + j, (*chip, c), me).wait_recv()
            passed[j].start()
        copy(0, sibling, me).wait_recv()
        for j, chip in enumerate(chips):
            copy(4 + j, (*chip, 1 - c), me).wait_recv()
        for cp in first + passed:
            cp.wait_send()
        own.wait()

    return pl.pallas_call(
        body, name=name, out_shape=jax.ShapeDtypeStruct((8, R, LANES), mine.dtype),
        in_specs=[HBM_SPEC], out_specs=HBM_SPEC,
        scratch_shapes=[pltpu.SemaphoreType.DMA((7,)), pltpu.SemaphoreType.DMA((7,)), pltpu.SemaphoreType.DMA],
    )(mine)


def allgather_direct(mine, *, name):
    R = mine.shape[0]

    def body(x_ref, out_ref, send_sems, recv_sems, local_sem):
        x, y, c = _place()
        own = pltpu.make_async_copy(x_ref, out_ref.at[4 * x + 2 * y + c], local_sem)
        own.start()
        sends = []
        for f in range(1, 8):
            fx, fy, fc = (f >> 2) & 1, (f >> 1) & 1, f & 1
            px, py, pc = jnp.where(fx, 1 - x, x), jnp.where(fy, 1 - y, y), jnp.where(fc, 1 - c, c)
            sends.append(pltpu.make_async_remote_copy(
                src_ref=x_ref, dst_ref=out_ref.at[4 * x + 2 * y + c], send_sem=send_sems.at[f - 1],
                recv_sem=recv_sems.at[f - 1], device_id=(px, py, pc), device_id_type=MESH))
        for cp in sends:
            cp.start()
        for f in range(1, 8):
            fx, fy, fc = (f >> 2) & 1, (f >> 1) & 1, f & 1
            px, py, pc = jnp.where(fx, 1 - x, x), jnp.where(fy, 1 - y, y), jnp.where(fc, 1 - c, c)
            pltpu.make_async_remote_copy(
                src_ref=x_ref, dst_ref=out_ref.at[4 * px + 2 * py + pc], send_sem=send_sems.at[f - 1],
                recv_sem=recv_sems.at[f - 1], device_id=(px, py, pc), device_id_type=MESH).wait_recv()
        for cp in sends:
            cp.wait_send()
        own.wait()

    return pl.pallas_call(
        body, name=name, out_shape=jax.ShapeDtypeStruct((8, R, LANES), mine.dtype),
        in_specs=[HBM_SPEC], out_specs=HBM_SPEC,
        scratch_shapes=[pltpu.SemaphoreType.DMA((7,)), pltpu.SemaphoreType.DMA((7,)), pltpu.SemaphoreType.DMA],
    )(mine)


def send_to_sibling(v, *, name):
    def body(v_ref, out_ref, send_sem, recv_sem):
        x, y, c = _place()
        cp = pltpu.make_async_remote_copy(src_ref=v_ref, dst_ref=out_ref, send_sem=send_sem, recv_sem=recv_sem,
                                          device_id=(x, y, 1 - c), device_id_type=MESH)
        cp.start()
        cp.wait()

    return pl.pallas_call(
        body, name=name, out_shape=jax.ShapeDtypeStruct(v.shape, v.dtype), in_specs=[HBM_SPEC], out_specs=HBM_SPEC,
        scratch_shapes=[pltpu.SemaphoreType.DMA, pltpu.SemaphoreType.DMA],
    )(v)


def pair_gather(v, *, name):
    def body(v_ref, out_ref, send_sem, recv_sem, local_sem):
        x, y, c = _place()
        own = pltpu.make_async_copy(v_ref, out_ref.at[c], local_sem)
        own.start()
        cp = pltpu.make_async_remote_copy(src_ref=v_ref, dst_ref=out_ref.at[c], send_sem=send_sem, recv_sem=recv_sem,
                                          device_id=(x, y, 1 - c), device_id_type=MESH)
        cp.start()
        pltpu.make_async_remote_copy(src_ref=v_ref, dst_ref=out_ref.at[1 - c], send_sem=send_sem, recv_sem=recv_sem,
                                     device_id=(x, y, 1 - c), device_id_type=MESH).wait_recv()
        cp.wait_send()
        own.wait()

    return pl.pallas_call(
        body, name=name, out_shape=jax.ShapeDtypeStruct((2,) + v.shape, v.dtype), in_specs=[HBM_SPEC],
        out_specs=HBM_SPEC,
        scratch_shapes=[pltpu.SemaphoreType.DMA, pltpu.SemaphoreType.DMA, pltpu.SemaphoreType.DMA],
    )(v)


def chip_exchange(p, *, name):
    R = p.shape[1]

    def body(p_ref, out_ref, send_sems, recv_sems):
        x, y, c = _place()
        chips = [(1 - x, y), (x, 1 - y), (1 - x, 1 - y)]
        sends = [pltpu.make_async_remote_copy(
            src_ref=p_ref.at[2 * px + py], dst_ref=out_ref.at[j], send_sem=send_sems.at[j], recv_sem=recv_sems.at[j],
            device_id=(px, py, c), device_id_type=MESH) for j, (px, py) in enumerate(chips)]
        for cp in sends:
            cp.start()
        for cp in sends:
            cp.wait()

    return pl.pallas_call(
        body, name=name, out_shape=jax.ShapeDtypeStruct((3, R, LANES), p.dtype), in_specs=[HBM_SPEC],
        out_specs=HBM_SPEC,
        scratch_shapes=[pltpu.SemaphoreType.DMA((3,)), pltpu.SemaphoreType.DMA((3,))],
    )(p)


def add_blocks(terms, out_dtype, *, name, tile=1024):
    R = terms[0].shape[0]
    tr = R
    for d in range(16, min(R, tile) + 1, 16):
        if R % d == 0:
            tr = d

    def body(*refs):
        acc = refs[0][...].astype(F32)
        for ref in refs[1:-1]:
            acc = acc + ref[...].astype(F32)
        refs[-1][...] = acc.astype(out_dtype)

    spec = pl.BlockSpec((tr, LANES), lambda i: (i, 0))
    return pl.pallas_call(
        body, name=name, grid=(R // tr,), in_specs=[spec] * len(terms), out_specs=spec,
        out_shape=jax.ShapeDtypeStruct((R, LANES), out_dtype), compiler_params=_cparams("parallel"),
    )(*terms)


def _half_rows(arr, cc):
    hr = arr.shape[1] // 2
    return lax.dynamic_slice_in_dim(arr, cc * hr, hr, axis=1).reshape(-1)


def _flat_half(shards, cc, dtype):
    flat = jnp.concatenate([_half_rows(shards[n], cc).astype(dtype) for n in BIG])
    return flat.reshape(-1, LANES)


def _unflat_halves(flat_by_c, shapes):
    out, off = {}, 0
    for n in BIG:
        _, R, C = shapes[n]
        sz = 2 * (R // 2) * C
        out[n] = jnp.concatenate([flat_by_c[c][off:off + sz].reshape(2, R // 2, C) for c in range(2)], axis=1)
        off += sz
    return out


def _to_heads(a, nh):
    T = a.shape[0]
    return a.reshape(T, nh, a.shape[1] // nh).transpose(1, 0, 2)


def _from_heads(a):
    nh, T, d = a.shape
    return a.transpose(1, 0, 2).reshape(T, nh * d)


def _pad_cols(a, n):
    return jnp.pad(a, ((0, 0), (0, n - a.shape[1])))


def _pack_w_in(w):
    offs = [sum(IN_SPLITS[:i]) for i in range(len(IN_SPLITS) + 1)]
    sb, z, xbc, dt, cq, ckv, kr = [w[:, offs[i]:offs[i + 1]] for i in range(len(IN_SPLITS))]
    zeros = lambda n: jnp.zeros((w.shape[0], n), w.dtype)
    h = MLA_ROPE // 2
    kra = jnp.concatenate([zeros(MLA_NOPE), kr, zeros(LANES - MLA_QK)], axis=1)
    krb = jnp.concatenate([zeros(MLA_NOPE), -kr[:, h:], kr[:, :h], zeros(LANES - MLA_QK)], axis=1)
    return jnp.concatenate([sb, xbc, z, cq, ckv, _pad_cols(dt, LANES), kra, krb], axis=1)


def _unpack_gw_in(g):
    h = MLA_ROPE // 2
    ga, gb = g[:, OFF_KRA:OFF_KRA + LANES], g[:, OFF_KRB:OFF_KRB + LANES]
    gkr = ga[:, MLA_NOPE:MLA_QK] + jnp.concatenate([gb[:, MLA_NOPE + h:MLA_QK], -gb[:, MLA_NOPE:MLA_NOPE + h]], axis=1)
    return jnp.concatenate([g[:, OFF_SB:OFF_SB + 768], g[:, OFF_Z:OFF_Z + 512], g[:, OFF_XBC:OFF_XBC + 768],
                            g[:, OFF_DT:OFF_DT + 8], g[:, OFF_CQ:OFF_CQ + 256], g[:, OFF_CKV:OFF_CKV + 128], gkr], axis=1)


def _pack_w_uq(w):
    zeros = lambda n: jnp.zeros((w.shape[0], n), w.dtype)
    h = MLA_ROPE // 2
    pp, rr = [], []
    for i in range(MLA_HEADS):
        nope = w[:, MLA_QK * i:MLA_QK * i + MLA_NOPE]
        rope = w[:, MLA_QK * i + MLA_NOPE:MLA_QK * (i + 1)]
        pp += [nope, rope, zeros(LANES - MLA_QK)]
        rr += [zeros(MLA_NOPE), -rope[:, h:], rope[:, :h], zeros(LANES - MLA_QK)]
    return jnp.concatenate(pp, axis=1), jnp.concatenate(rr, axis=1)


def _unpack_gw_uq(gp, gr):
    h = MLA_ROPE // 2
    out = []
    for i in range(MLA_HEADS):
        b = LANES * i
        out.append(gp[:, b:b + MLA_NOPE])
        out.append(gp[:, b + MLA_NOPE:b + MLA_NOPE + h] + gr[:, b + MLA_NOPE + h:b + MLA_QK])
        out.append(gp[:, b + MLA_NOPE + h:b + MLA_QK] - gr[:, b + MLA_NOPE:b + MLA_NOPE + h])
    return jnp.concatenate(out, axis=1)


def _pack_w_ukv(w):
    zeros = lambda n: jnp.zeros((w.shape[0], n), w.dtype)
    kk, vv = [], []
    for i in range(MLA_HEADS):
        b = (MLA_NOPE + MLA_V) * i
        kk += [w[:, b:b + MLA_NOPE], zeros(LANES - MLA_NOPE)]
        vv.append(w[:, b + MLA_NOPE:b + MLA_NOPE + MLA_V])
    return jnp.concatenate(kk, axis=1), jnp.concatenate(vv, axis=1)


def _unpack_gw_ukv(gk, gv):
    out = []
    for i in range(MLA_HEADS):
        out += [gk[:, LANES * i:LANES * i + MLA_NOPE], gv[:, MLA_V * i:MLA_V * (i + 1)]]
    return jnp.concatenate(out, axis=1)


def _rope_tables(positions):
    inv_freq = 1.0 / (ROPE_THETA ** (jnp.arange(0, MLA_ROPE, 2, dtype=F32) / MLA_ROPE))
    ang = positions.astype(F32)[:, None] * inv_freq
    cos, sin = jnp.cos(ang), jnp.sin(ang)
    T = positions.shape[0]
    one, zero = jnp.ones((T, MLA_NOPE), F32), jnp.zeros((T, MLA_NOPE), F32)
    pad1, pad0 = jnp.ones((T, LANES - MLA_QK), F32), jnp.zeros((T, LANES - MLA_QK), F32)
    return jnp.concatenate([one, cos, cos, pad1], axis=1), jnp.concatenate([zero, sin, sin, pad0], axis=1)


def _row(v):
    return v.reshape(1, -1)


def _pad_row(v):
    return _pad_cols(v.reshape(1, -1), LANES)


def _layer_weights(full, small, li):
    p = {}
    p["w_in_p"] = _pack_w_in(full["w_in"][li])
    p["w_in_pt"] = p["w_in_p"].T
    p["wqp"], p["wqr"] = _pack_w_uq(full["mla_w_uq"][li])
    p["wkp"], p["wvp"] = _pack_w_ukv(full["mla_w_ukv"][li])
    p["w_out"] = full["w_out"][li]
    p["w_out_t"] = p["w_out"].T
    p["w_up"] = full["ffn_w_up"][li]
    p["w_up_t"] = p["w_up"].T
    p["w_down"] = full["ffn_w_down"][li]
    p["w_down_t"] = p["w_down"].T
    for n in ("mix_norm", "sb_out_norm", "ssm_conv_b", "ssm_out_norm", "mla_q_norm", "mla_kv_norm", "mla_out_norm",
              "ffn_norm", "ffn_conv_b"):
        p[n] = _row(small[n][li])
    for n in ("ssm_dt_bias", "ssm_a_log", "ssm_d"):
        p[n] = _pad_row(small[n][li])
    p["ssm_conv_w"] = small["ssm_conv_w"][li]
    p["ffn_conv_w"] = small["ffn_conv_w"][li]
    return p


def _layer_fwd(h, p, cos, sin, li):
    T = h.shape[0]
    nm = lambda s: "l%d_%s" % (li, s)
    s = {"h": h}
    (n1,) = rowwise(rms_fn, [h], [p["mix_norm"]], [(D_MODEL, BF16)], name=nm("mix_norm"))
    proj = matmul(n1, p["w_in_p"], name=nm("in_proj"))
    s["n1"], s["proj"] = n1, proj
    qkv = proj[:, OFF_SB:OFF_SB + 768]
    s["sb_q"] = _to_heads((qkv[:, 0:256] * (SB_DIM ** -0.5)).astype(BF16), SB_HEADS)
    s["sb_k"] = _to_heads(qkv[:, 256:512].astype(BF16), SB_HEADS)
    s["sb_v"] = _to_heads(qkv[:, 512:768].astype(BF16), SB_HEADS)
    y_sb_hm, s["sb_bt"] = sb_fwd(s["sb_q"], s["sb_k"], s["sb_v"], name=nm("sb_fwd"))
    s["y_sb"] = _from_heads(y_sb_hm)
    xbc = ssm_conv_act(proj, p["ssm_conv_w"], p["ssm_conv_b"], name=nm("ssm_conv"))
    s["x_hm"] = _to_heads(xbc[:, :SSM_INNER], SSM_HEADS)
    s["b_hm"] = _to_heads(xbc[:, SSM_INNER:SSM_INNER + 128], SSM_GROUPS)
    s["c_hm"] = _to_heads(xbc[:, SSM_INNER + 128:], SSM_GROUPS)
    y_ssm_hm, s["states"] = ssd_fwd(s["x_hm"], s["b_hm"], s["c_hm"], proj, p["ssm_dt_bias"], p["ssm_a_log"],
                                    p["ssm_d"], name=nm("ssd_fwd"))
    s["y_ssm"] = _from_heads(y_ssm_hm)
    rows = [(proj, 256, OFF_CQ // 256), (proj, 128, OFF_CKV // 128), (proj, 128, OFF_KRA // 128),
            (proj, 128, OFF_KRB // 128), cos, sin]
    qp, kp, vv = rowwise(mla_prep_fn, rows, [p["mla_q_norm"], p["mla_kv_norm"], p["wqp"], p["wqr"], p["wkp"], p["wvp"]],
                         [(512, BF16), (512, BF16), (256, BF16)], name=nm("mla_prep"))
    s["mla_q"], s["mla_k"], s["mla_v"] = _to_heads(qp, MLA_HEADS), _to_heads(kp, MLA_HEADS), _to_heads(vv, MLA_HEADS)
    s["mla_o"], s["mla_lse"] = mla_fwd(s["mla_q"], s["mla_k"], s["mla_v"], name=nm("mla_fwd"))
    s["y_mla"] = _from_heads(s["mla_o"])
    (cat,) = rowwise(merge_fn, [s["y_sb"], s["y_ssm"], (proj, 512, OFF_Z // 512), s["y_mla"]],
                     [p["sb_out_norm"], p["ssm_out_norm"], p["mla_out_norm"]], [(D_MODEL, BF16)], name=nm("merge"),
                     post=lambda a, b, c: (jnp.concatenate([a, b, c], axis=1),))
    s["cat"] = cat
    h1 = matmul(cat, p["w_out"], name=nm("out_proj"), residual=h)
    s["h1"] = h1
    (n2,) = rowwise(rms_fn, [h1], [p["ffn_norm"]], [(D_MODEL, BF16)], name=nm("ffn_norm"))
    up = matmul(n2, p["w_up"], name=nm("ffn_up"))
    act = ffn_act(up, p["ffn_conv_w"], p["ffn_conv_b"], name=nm("ffn_act"))
    s["n2"], s["up"], s["act"] = n2, up, act
    h2 = matmul(act, p["w_down"], name=nm("ffn_down"), residual=h1)
    return h2, s


def _layer_bwd(dh2, s, p, cos, sin, li):
    nm = lambda t: "l%d_%s" % (li, t)
    g = {}
    proj = s["proj"]
    g["ffn_w_down"] = matmul(s["act"], dh2, name=nm("g_w_down"), ta=True)
    d_act = matmul(dh2, p["w_down_t"], name=nm("d_act"), out_dtype=BF16)
    du = ffn_act_bwd_a(s["up"], p["ffn_conv_w"], p["ffn_conv_b"], d_act, name=nm("ffn_act_bwd"))
    d_up, g["ffn_conv_w"], gcb = conv_bwd_b(du, s["up"], 0, p["ffn_conv_w"], name=nm("ffn_conv_bwd"), out_dtype=BF16,
                                            tc=1408)
    g["ffn_conv_b"] = gcb[0]
    g["ffn_w_up"] = matmul(s["n2"], d_up, name=nm("g_w_up"), ta=True)
    d_n2 = matmul(d_up, p["w_up_t"], name=nm("d_n2"))
    (dh1,), (gn,) = rowwise_bwd(rms_fn, [s["h1"]], [], [p["ffn_norm"]], [d_n2], [F32], name=nm("ffn_norm_bwd"),
                                add0=dh2)
    g["ffn_norm"] = gn[0]
    g["w_out"] = matmul(s["cat"], dh1, name=nm("g_w_out"), ta=True)
    d_cat = matmul(dh1, p["w_out_t"], name=nm("d_cat"))
    (d_ysb, d_yssm, d_z, d_ymla), (g1, g2, g3) = rowwise_bwd(
        merge_fn, [s["y_sb"], s["y_ssm"], (proj, 512, OFF_Z // 512), s["y_mla"]], [],
        [p["sb_out_norm"], p["ssm_out_norm"], p["mla_out_norm"]], [d_cat], [F32, F32, BF16, F32], name=nm("merge_bwd"),
        pre_ct=lambda d: (d[:, 0:256], d[:, 256:768], d[:, 768:1024]))
    g["sb_out_norm"], g["ssm_out_norm"], g["mla_out_norm"] = g1[0], g2[0], g3[0]
    dq, dk, dv = sb_bwd(s["sb_q"], s["sb_k"], s["sb_v"], _to_heads(d_ysb, SB_HEADS), s["sb_bt"], name=nm("sb_bwd"),
                        q_scale=SB_DIM ** -0.5)
    d_sb = jnp.concatenate([_from_heads(dq), _from_heads(dk), _from_heads(dv)], axis=1).astype(BF16)
    dqp, dkp, dvv = mla_bwd(s["mla_q"], s["mla_k"], s["mla_v"], _to_heads(d_ymla, MLA_HEADS), s["mla_o"], s["mla_lse"],
                            name=nm("mla_bwd"))
    rows = [(proj, 256, OFF_CQ // 256), (proj, 128, OFF_CKV // 128), (proj, 128, OFF_KRA // 128),
            (proj, 128, OFF_KRB // 128)]
    (d_cq, d_ckv, d_kra, d_krb), (gqn, gkvn, gwqp, gwqr, gwkp, gwvp) = rowwise_bwd(
        mla_prep_fn, rows, [cos, sin], [p["mla_q_norm"], p["mla_kv_norm"], p["wqp"], p["wqr"], p["wkp"], p["wvp"]],
        [_from_heads(dqp), _from_heads(dkp), _from_heads(dvv)], [BF16] * 4, name=nm("mla_prep_bwd"), tile=256)
    g["mla_q_norm"], g["mla_kv_norm"] = gqn[0], gkvn[0]
    g["mla_w_uq"] = _unpack_gw_uq(gwqp, gwqr)
    g["mla_w_ukv"] = _unpack_gw_ukv(gwkp, gwvp)
    dx_hm, db_hm, dc_hm, d_dt, gdb, gal, gds = ssd_bwd(
        s["x_hm"], s["b_hm"], s["c_hm"], proj, s["states"], p["ssm_dt_bias"], p["ssm_a_log"], p["ssm_d"],
        _to_heads(d_yssm, SSM_HEADS), name=nm("ssd_bwd"))
    g["ssm_dt_bias"], g["ssm_a_log"], g["ssm_d"] = gdb[0, :8], gal[0, :8], gds[0, :8]
    d_xbc_act = jnp.concatenate([_from_heads(dx_hm), _from_heads(db_hm), _from_heads(dc_hm)], axis=1)
    d_pre = ssm_conv_bwd_a(proj, p["ssm_conv_w"], p["ssm_conv_b"], d_xbc_act, name=nm("ssm_conv_bwd_a"))
    d_xbc, g["ssm_conv_w"], gscb = conv_bwd_b(d_pre, proj, OFF_XBC, p["ssm_conv_w"], name=nm("ssm_conv_bwd_b"),
                                              out_dtype=BF16, tc=256)
    g["ssm_conv_b"] = gscb[0]
    d_proj = jnp.concatenate([d_sb, d_xbc, d_z, d_cq, d_ckv, d_dt.astype(BF16), d_kra, d_krb], axis=1)
    g["w_in"] = _unpack_gw_in(matmul(s["n1"], d_proj, name=nm("g_w_in"), ta=True))
    d_n1 = matmul(d_proj, p["w_in_pt"], name=nm("d_n1"))
    (dh0,), (gm,) = rowwise_bwd(rms_fn, [s["h"]], [], [p["mix_norm"]], [d_n1], [F32], name=nm("mix_norm_bwd"),
                                add0=dh1)
    g["mix_norm"] = gm[0]
    return dh0, g


def kernel(x, positions, mix_norm, w_in, sb_out_norm, ssm_conv_w, ssm_conv_b, ssm_dt_bias, ssm_a_log, ssm_d, ssm_out_norm, mla_q_norm, mla_w_uq, mla_kv_norm, mla_w_ukv, mla_out_norm, w_out, ffn_norm, ffn_w_up, ffn_conv_w, ffn_conv_b, ffn_w_down, final_norm, loss_target, m_mix_norm, m_w_in, m_sb_out_norm, m_ssm_conv_w, m_ssm_conv_b, m_ssm_dt_bias, m_ssm_a_log, m_ssm_d, m_ssm_out_norm, m_mla_q_norm, m_mla_w_uq, m_mla_kv_norm, m_mla_w_ukv, m_mla_out_norm, m_w_out, m_ffn_norm, m_ffn_w_up, m_ffn_conv_w, m_ffn_conv_b, m_ffn_w_down, m_final_norm, v_mix_norm, v_w_in, v_sb_out_norm, v_ssm_conv_w, v_ssm_conv_b, v_ssm_dt_bias, v_ssm_a_log, v_ssm_d, v_ssm_out_norm, v_mla_q_norm, v_mla_w_uq, v_mla_kv_norm, v_mla_w_ukv, v_mla_out_norm, v_w_out, v_ffn_norm, v_ffn_w_up, v_ffn_conv_w, v_ffn_conv_b, v_ffn_w_down, v_final_norm):
    W = dict(mix_norm=mix_norm, w_in=w_in, sb_out_norm=sb_out_norm, ssm_conv_w=ssm_conv_w, ssm_conv_b=ssm_conv_b,
             ssm_dt_bias=ssm_dt_bias, ssm_a_log=ssm_a_log, ssm_d=ssm_d, ssm_out_norm=ssm_out_norm,
             mla_q_norm=mla_q_norm, mla_w_uq=mla_w_uq, mla_kv_norm=mla_kv_norm, mla_w_ukv=mla_w_ukv,
             mla_out_norm=mla_out_norm, w_out=w_out, ffn_norm=ffn_norm, ffn_w_up=ffn_w_up, ffn_conv_w=ffn_conv_w,
             ffn_conv_b=ffn_conv_b, ffn_w_down=ffn_w_down, final_norm=final_norm)
    M = dict(mix_norm=m_mix_norm, w_in=m_w_in, sb_out_norm=m_sb_out_norm, ssm_conv_w=m_ssm_conv_w,
             ssm_conv_b=m_ssm_conv_b, ssm_dt_bias=m_ssm_dt_bias, ssm_a_log=m_ssm_a_log, ssm_d=m_ssm_d,
             ssm_out_norm=m_ssm_out_norm, mla_q_norm=m_mla_q_norm, mla_w_uq=m_mla_w_uq, mla_kv_norm=m_mla_kv_norm,
             mla_w_ukv=m_mla_w_ukv, mla_out_norm=m_mla_out_norm, w_out=m_w_out, ffn_norm=m_ffn_norm,
             ffn_w_up=m_ffn_w_up, ffn_conv_w=m_ffn_conv_w, ffn_conv_b=m_ffn_conv_b, ffn_w_down=m_ffn_w_down,
             final_norm=m_final_norm)
    V = dict(mix_norm=v_mix_norm, w_in=v_w_in, sb_out_norm=v_sb_out_norm, ssm_conv_w=v_ssm_conv_w,
             ssm_conv_b=v_ssm_conv_b, ssm_dt_bias=v_ssm_dt_bias, ssm_a_log=v_ssm_a_log, ssm_d=v_ssm_d,
             ssm_out_norm=v_ssm_out_norm, mla_q_norm=v_mla_q_norm, mla_w_uq=v_mla_w_uq, mla_kv_norm=v_mla_kv_norm,
             mla_w_ukv=v_mla_w_ukv, mla_out_norm=v_mla_out_norm, w_out=v_w_out, ffn_norm=v_ffn_norm,
             ffn_w_up=v_ffn_w_up, ffn_conv_w=v_ffn_conv_w, ffn_conv_b=v_ffn_conv_b, ffn_w_down=v_ffn_w_down,
             final_norm=v_final_norm)
    depth = mix_norm.shape[0]
    cx, cy, cc = _place()
    chip = 2 * cx + cy
    T = x.shape[1]

    shard_shapes = {n: W[n].shape for n in BIG}
    gathered = allgather_blocks(_flat_half(W, cc, BF16), name="gather_weights")
    full = {}
    per_chip = [_unflat_halves([gathered[2 * k + c].reshape(-1) for c in range(2)], shard_shapes) for k in range(4)]
    for n in BIG:
        full[n] = jnp.concatenate([per_chip[k][n] for k in range(4)], axis=BIG_AXIS[n])
    conv_full = {}
    small = {n: W[n] for n in SMALL_REPL}
    cw_flat = jnp.concatenate([W[n].reshape(-1) for n in SMALL_SHARD])
    cw_rows = -(-cw_flat.shape[0] // (8 * LANES)) * 8
    cw_all = allgather_direct(jnp.pad(cw_flat, (0, cw_rows * LANES - cw_flat.shape[0])).reshape(cw_rows, LANES),
                              name="gather_conv_taps")
    off = 0
    for n in SMALL_SHARD:
        sz = W[n].size
        conv_full[n] = jnp.concatenate(
            [cw_all[2 * k].reshape(-1)[off:off + sz].reshape(W[n].shape) for k in range(4)], axis=2)
        off += sz
    small.update(conv_full)

    cos, sin = _rope_tables(positions[0])
    params = [_layer_weights(full, small, li) for li in range(depth)]

    h = x[0]
    saved = []
    for li in range(depth):
        h, s = _layer_fwd(h, params[li], cos, sin, li)
        saved.append(s)
    dh, g_final, loss_lanes = loss_head(h, loss_target[0], _row(final_norm), name="loss_head")

    grads = [None] * depth
    for li in reversed(range(depth)):
        dh, grads[li] = _layer_bwd(dh, saved[li], params[li], cos, sin, li)
    grad_x = dh[None]
    G = {n: jnp.stack([grads[li][n] for li in range(depth)]) for n in WEIGHTS if n != "final_norm"}
    G["final_norm"] = g_final[0]

    def shard_major(n):
        a = G[n]
        ax = BIG_AXIS[n]
        parts = jnp.split(a, 4, axis=ax)
        return parts

    by_chip = {n: shard_major(n) for n in BIG}

    def flat_for(k, which):
        return _flat_half({n: by_chip[n][k] for n in BIG}, which, BF16)

    mine_first = jnp.stack([flat_for(k, cc) for k in range(4)])
    for_sibling = jnp.stack([flat_for(k, 1 - cc) for k in range(4)])
    R = mine_first.shape[1]
    from_sibling = send_to_sibling(for_sibling.reshape(4 * R, LANES), name="grads_to_sibling")
    pair = add_blocks([mine_first.reshape(4 * R, LANES), from_sibling], BF16, name="grads_pair_sum").reshape(4, R, LANES)
    others = chip_exchange(pair, name="grads_chip_exchange")
    own = lax.dynamic_index_in_dim(pair, chip, 0, keepdims=False)
    half = add_blocks([own, others[0], others[1], others[2]], F32, name="grads_chip_sum")
    both = pair_gather(half, name="grads_pair_gather")
    g_big = _unflat_halves([both[c].reshape(-1) for c in range(2)], shard_shapes)

    small_list = [G[n].reshape(-1) for n in SMALL_REPL] + [G[n].reshape(-1) for n in SMALL_SHARD]
    small_list.append(jnp.sum(loss_lanes).reshape(1))
    sm = jnp.concatenate(small_list)
    n_small = sm.shape[0]
    sm_rows = -(-n_small // (16 * LANES)) * 16
    sm_all = allgather_direct(jnp.pad(sm, (0, sm_rows * LANES - n_small)).reshape(sm_rows, LANES), name="gather_small")
    sm_sum = add_blocks([sm_all[d] for d in range(8)], F32, name="small_sum").reshape(-1)
    g_small, off = {}, 0
    for n in SMALL_REPL:
        g_small[n] = sm_sum[off:off + W[n].size].reshape(W[n].shape)
        off += W[n].size
    for n in SMALL_SHARD:
        full_shape = conv_full[n].shape
        sz = conv_full[n].size
        gfull = sm_sum[off:off + sz].reshape(full_shape)
        width = W[n].shape[2]
        g_small[n] = lax.dynamic_slice_in_dim(gfull, chip * width, width, axis=2)
        off += sz
    loss = sm_sum[off]

    grad_out, delta, new_m, new_v = {}, {}, {}, {}
    for n in BIG:
        shp = W[n].shape
        two_d = lambda a: a.reshape(shp[0] * shp[1], shp[2])
        d, nm_, nv_ = adamw(two_d(W[n]), two_d(g_big[n]), two_d(M[n]), two_d(V[n]), name="adamw_" + n)
        grad_out[n], delta[n], new_m[n], new_v[n] = g_big[n], d.reshape(shp), nm_.reshape(shp), nv_.reshape(shp)
    small_names = SMALL_REPL + SMALL_SHARD

    def flat_small(d):
        f = jnp.concatenate([d[n].reshape(-1) for n in small_names])
        rows = -(-f.shape[0] // (8 * LANES)) * 8
        return jnp.pad(f, (0, rows * LANES - f.shape[0])).reshape(rows, LANES)

    vpad = flat_small(V)
    d, nm_, nv_ = adamw(flat_small(W), flat_small(g_small), flat_small(M), vpad, name="adamw_small")
    off = 0
    for n in small_names:
        sz = W[n].size
        grad_out[n] = g_small[n]
        delta[n] = d.reshape(-1)[off:off + sz].reshape(W[n].shape)
        new_m[n] = nm_.reshape(-1)[off:off + sz].reshape(W[n].shape)
        new_v[n] = nv_.reshape(-1)[off:off + sz].reshape(W[n].shape)
        off += sz

    return (loss, grad_x, *[grad_out[n] for n in WEIGHTS], *[delta[n] for n in WEIGHTS],
            *[new_m[n] for n in WEIGHTS], *[new_v[n] for n in WEIGHTS])
```

```python
import functools
import math

import jax
import jax.numpy as jnp
from jax import lax
from jax.experimental import pallas as pl
from jax.experimental.pallas import tpu as pltpu

F32 = jnp.float32
BF16 = jnp.bfloat16

EPS = 1e-6
D_MODEL = 1024
SB_HEADS, SB_DIM = 4, 64
SSM_HEADS, SSM_DIM, SSM_GROUPS, SSM_STATE, SSM_CHUNK = 8, 64, 2, 64, 128
SSM_INNER = SSM_HEADS * SSM_DIM
SSM_CONV_DIM = SSM_INNER + 2 * SSM_GROUPS * SSM_STATE
MLA_HEADS, MLA_NOPE, MLA_ROPE, MLA_V = 4, 64, 32, 64
MLA_QK = MLA_NOPE + MLA_ROPE
MLA_SCALE = MLA_QK ** -0.5
ROPE_THETA = 10000.0
D_FF = 2816
IN_SPLITS = (768, 512, 768, 8, 256, 128, 32)

OFF_SB, OFF_XBC, OFF_Z, OFF_CQ, OFF_CKV, OFF_DT, OFF_KRA, OFF_KRB = 0, 768, 1536, 2048, 2304, 2432, 2560, 2688
D_IN_P = 2816
LANES = 128

ADAM_LR, ADAM_B1, ADAM_B2, ADAM_EPS, ADAM_WD, ADAM_STEP = 0.001, 0.9, 0.999, 1e-08, 0.01, 10

V7X_VMEM_LIMIT = 48 * 1024 * 1024

NT = (((1,), (1,)), ((), ()))
TN = (((0,), (0,)), ((), ()))

BIG = ("w_in", "mla_w_uq", "mla_w_ukv", "w_out", "ffn_w_up", "ffn_w_down")
BIG_AXIS = {"w_in": 2, "mla_w_uq": 2, "mla_w_ukv": 2, "w_out": 1, "ffn_w_up": 2, "ffn_w_down": 1}
SMALL_REPL = ("mix_norm", "sb_out_norm", "ssm_conv_b", "ssm_dt_bias", "ssm_a_log", "ssm_d", "ssm_out_norm",
              "mla_q_norm", "mla_kv_norm", "mla_out_norm", "ffn_norm", "ffn_conv_b", "final_norm")
SMALL_SHARD = ("ssm_conv_w", "ffn_conv_w")
WEIGHTS = ("mix_norm", "w_in", "sb_out_norm", "ssm_conv_w", "ssm_conv_b", "ssm_dt_bias", "ssm_a_log", "ssm_d",
           "ssm_out_norm", "mla_q_norm", "mla_w_uq", "mla_kv_norm", "mla_w_ukv", "mla_out_norm", "w_out", "ffn_norm",
           "ffn_w_up", "ffn_conv_w", "ffn_conv_b", "ffn_w_down", "final_norm")


def _cparams(*sem):
    return pltpu.CompilerParams(dimension_semantics=sem if sem else None, vmem_limit_bytes=V7X_VMEM_LIMIT)


def _pick(n, target, mult=LANES):
    best = None
    for d in range(mult, min(n, target) + 1, mult):
        if n % d == 0:
            best = d
    return best or n


def _sigmoid(x):
    return 1.0 / (1.0 + jnp.exp(-x))


def _softplus(x):
    ax = jnp.where(x > 0, x, -x)
    return jnp.where(x > 0, x, 0.0) + jnp.log(1.0 + jnp.exp(-ax))


def _rms(x, g):
    return x * lax.rsqrt(jnp.mean(x * x, axis=-1, keepdims=True) + EPS) * g


def _raw_nn(a, b):
    return jnp.dot(a.astype(BF16), b.astype(BF16), preferred_element_type=F32)


def _raw_nt(a, b):
    return lax.dot_general(a.astype(BF16), b.astype(BF16), NT, preferred_element_type=F32)


def _raw_tn(a, b):
    return lax.dot_general(a.astype(BF16), b.astype(BF16), TN, preferred_element_type=F32)


@jax.custom_vjp
def mm_nn(a, b):
    return _raw_nn(a, b)


mm_nn.defvjp(lambda a, b: (_raw_nn(a, b), (a, b)),
             lambda r, ct: (_raw_nt(ct, r[1]), _raw_tn(r[0], ct)))


@jax.custom_vjp
def mm_nt(a, b):
    return _raw_nt(a, b)


mm_nt.defvjp(lambda a, b: (_raw_nt(a, b), (a, b)),
             lambda r, ct: (_raw_nn(ct, r[1]), _raw_tn(ct, r[0])))


@jax.custom_vjp
def mm_tn(a, b):
    return _raw_tn(a, b)


mm_tn.defvjp(lambda a, b: (_raw_tn(a, b), (a, b)),
             lambda r, ct: (_raw_nt(r[1], ct), _raw_nn(r[0], ct)))


def _split_dot(x, m, terms):
    acc = None
    r = x
    for t in range(terms):
        xt = r.astype(BF16)
        d = jnp.dot(xt, m, preferred_element_type=F32)
        acc = d if acc is None else acc + d
        if t + 1 < terms:
            r = r - xt.astype(F32)
    return acc


def _tri_dot(tri, x, terms=3):
    acc = None
    r = x
    for t in range(terms):
        xt = r.astype(BF16)
        d = jnp.dot(tri, xt, preferred_element_type=F32)
        acc = d if acc is None else acc + d
        if t + 1 < terms:
            r = r - xt.astype(F32)
    return acc


def _tri(n, cmp):
    r = lax.broadcasted_iota(jnp.int32, (n, n), 0)
    c = lax.broadcasted_iota(jnp.int32, (n, n), 1)
    return cmp(r, c).astype(BF16)


@jax.custom_vjp
def csum_rows(x):
    return _tri_dot(_tri(x.shape[0], lambda r, c: r >= c), x)


csum_rows.defvjp(lambda x: (csum_rows(x), None),
                 lambda _, ct: (_tri_dot(_tri(ct.shape[0], lambda r, c: r <= c), ct),))


def matmul(a, b, *, name, out_dtype=F32, ta=False, residual=None):
    if ta:
        K, M = a.shape
    else:
        M, K = a.shape
    N = b.shape[1]
    tm = _pick(M, 512)
    tn = _pick(N, 512)
    tk = _pick(K, 1408)
    nk = K // tk
    has_res = residual is not None

    def body(*refs):
        if has_res:
            a_ref, b_ref, r_ref, o_ref, acc = refs
        else:
            a_ref, b_ref, o_ref, acc = refs
        k = pl.program_id(2)

        @pl.when(k == 0)
        def _():
            acc[...] = jnp.zeros_like(acc)

        av = a_ref[...].astype(BF16)
        bv = b_ref[...].astype(BF16)
        if ta:
            acc[...] += lax.dot_general(av, bv, TN, preferred_element_type=F32)
        else:
            acc[...] += jnp.dot(av, bv, preferred_element_type=F32)

        @pl.when(k == nk - 1)
        def _():
            r = acc[...]
            if has_res:
                r = r + r_ref[...].astype(F32)
            o_ref[...] = r.astype(o_ref.dtype)

    a_spec = pl.BlockSpec((tk, tm), lambda i, j, k: (k, i)) if ta else pl.BlockSpec((tm, tk), lambda i, j, k: (i, k))
    in_specs = [a_spec, pl.BlockSpec((tk, tn), lambda i, j, k: (k, j))]
    args = [a, b]
    if has_res:
        in_specs.append(pl.BlockSpec((tm, tn), lambda i, j, k: (i, j)))
        args.append(residual)
    return pl.pallas_call(
        body, name=name, grid=(M // tm, N // tn, nk),
        in_specs=in_specs, out_specs=pl.BlockSpec((tm, tn), lambda i, j, k: (i, j)),
        out_shape=jax.ShapeDtypeStruct((M, N), out_dtype),
        scratch_shapes=[pltpu.VMEM((tm, tn), F32)],
        compiler_params=_cparams("parallel", "parallel", "arbitrary"),
    )(*args)


def _row_spec(entry, tl):
    if isinstance(entry, tuple):
        arr, width, cb = entry
        return arr, pl.BlockSpec((tl, width), lambda i, cb=cb: (i, cb))
    return entry, pl.BlockSpec((tl, entry.shape[1]), lambda i: (i, 0))


def _rows_T(entry):
    return (entry[0] if isinstance(entry, tuple) else entry).shape[0]


def rowwise(fn, rows, params, outs, *, name, tile=512, post=None):
    T = _rows_T(rows[0])
    tl = min(T, tile)
    nr, npar = len(rows), len(params)

    def body(*refs):
        r = [ref[...].astype(F32) for ref in refs[:nr]]
        p = [ref[...].astype(F32) for ref in refs[nr:nr + npar]]
        res = fn(*r, *p)
        if post is not None:
            res = post(*res)
        for o_ref, val in zip(refs[nr + npar:], res):
            o_ref[...] = val.astype(o_ref.dtype)

    arrs, specs = [], []
    for e in rows:
        a, s = _row_spec(e, tl)
        arrs.append(a)
        specs.append(s)
    for p in params:
        arrs.append(p)
        specs.append(pl.BlockSpec(p.shape, lambda i: (0, 0)))
    res = pl.pallas_call(
        body, name=name, grid=(T // tl,), in_specs=specs,
        out_specs=[pl.BlockSpec((tl, c), lambda i: (i, 0)) for c, _ in outs],
        out_shape=[jax.ShapeDtypeStruct((T, c), dt) for c, dt in outs],
        compiler_params=_cparams("parallel"),
    )(*arrs)
    return res


def rowwise_bwd(fn, rows, nd_rows, params, cts, grad_dtypes, *, name, tile=512, pre_ct=None, add0=None):
    T = _rows_T(rows[0])
    tl = min(T, tile)
    nr, nn, npar, nc = len(rows), len(nd_rows), len(params), len(cts)
    has_add = add0 is not None

    def body(*refs):
        pos = 0
        r = [ref[...].astype(F32) for ref in refs[pos:pos + nr]]
        pos += nr
        nd = [ref[...].astype(F32) for ref in refs[pos:pos + nn]]
        pos += nn
        p = [ref[...].astype(F32) for ref in refs[pos:pos + npar]]
        pos += npar
        c = [ref[...].astype(F32) for ref in refs[pos:pos + nc]]
        pos += nc
        if has_add:
            addv = refs[pos][...].astype(F32)
            pos += 1
        rg_refs = refs[pos:pos + nr]
        pg_refs = refs[pos + nr:pos + nr + npar]
        if pre_ct is not None:
            c = list(pre_ct(*c))
        _, vjp = jax.vjp(lambda *a: fn(*a[:nr], *nd, *a[nr:]), *r, *p)
        g = vjp(tuple(c))
        for j, ref in enumerate(rg_refs):
            val = g[j]
            if has_add and j == 0:
                val = val + addv
            ref[...] = val.astype(ref.dtype)
        if npar:
            @pl.when(pl.program_id(0) == 0)
            def _():
                for ref in pg_refs:
                    ref[...] = jnp.zeros_like(ref)
            for j, ref in enumerate(pg_refs):
                ref[...] += g[nr + j]

    arrs, specs = [], []
    widths = []
    for e in list(rows) + list(nd_rows):
        a, s = _row_spec(e, tl)
        arrs.append(a)
        specs.append(s)
        widths.append(s.block_shape[1])
    for p in params:
        arrs.append(p)
        specs.append(pl.BlockSpec(p.shape, lambda i: (0, 0)))
    for e in cts:
        a, s = _row_spec(e, tl)
        arrs.append(a)
        specs.append(s)
    if has_add:
        a, s = _row_spec(add0, tl)
        arrs.append(a)
        specs.append(s)
    out_specs = [pl.BlockSpec((tl, widths[j]), lambda i: (i, 0)) for j in range(nr)]
    out_shape = [jax.ShapeDtypeStruct((T, widths[j]), grad_dtypes[j]) for j in range(nr)]
    out_specs += [pl.BlockSpec(p.shape, lambda i: (0, 0)) for p in params]
    out_shape += [jax.ShapeDtypeStruct(p.shape, F32) for p in params]
    res = pl.pallas_call(
        body, name=name, grid=(T // tl,), in_specs=specs, out_specs=out_specs, out_shape=out_shape,
        compiler_params=_cparams("arbitrary"),
    )(*arrs)
    return list(res[:nr]), list(res[nr:])


def rms_fn(h, g):
    return (_rms(h, g),)


def merge_fn(ysb, yssm, z, ymla, g_sb, g_ssm, g_mla):
    ya = _rms(ysb, g_sb)
    yb = _rms(yssm * (z * _sigmoid(z)), g_ssm)
    yc = _rms(ymla, g_mla)
    return ya, yb, yc


def mla_prep_fn(cq, ckv, kra, krb, cos, sin, qn, kvn, wqp, wqr, wkp, wvp):
    cos4 = jnp.concatenate([cos] * MLA_HEADS, axis=1)
    sin4 = jnp.concatenate([sin] * MLA_HEADS, axis=1)
    nq = _rms(cq, qn)
    q = (mm_nn(nq, wqp) * cos4 + mm_nn(nq, wqr) * sin4) * MLA_SCALE
    nkv = _rms(ckv, kvn)
    kpe = kra * cos + krb * sin
    k = mm_nn(nkv, wkp) + jnp.concatenate([kpe] * MLA_HEADS, axis=1)
    v = mm_nn(nkv, wvp)
    return q, k, v


HALO = 8


def _prev_halo_spec(tl, tc, col_of):
    return pl.BlockSpec((HALO, tc), lambda i, j: (jnp.maximum(i * (tl // HALO) - 1, 0), col_of(j)))


def _fill_prev(buf, x_ref, halo_ref, i):
    buf[0:HALO, :] = jnp.where(i > 0, halo_ref[...].astype(F32), 0.0)
    buf[HALO:, :] = x_ref[...].astype(F32)


def _conv_from(buf, w_ref, b_ref, K, tl):
    acc = b_ref[...].astype(F32) + jnp.zeros((tl, buf.shape[1]), F32)
    for k in range(K):
        acc = acc + buf[pl.ds(HALO - (K - 1 - k), tl), :] * w_ref[k:k + 1, :].astype(F32)
    return acc


def ssm_conv_act(proj, w, b, *, name, tile=512, tc=256):
    T = proj.shape[0]
    K, C = w.shape
    tl = min(T, tile)
    c0 = OFF_XBC // tc

    def body(x_ref, halo_ref, w_ref, b_ref, o_ref, buf):
        _fill_prev(buf, x_ref, halo_ref, pl.program_id(0))
        u = _conv_from(buf, w_ref, b_ref, K, tl)
        o_ref[...] = u * _sigmoid(u)

    return pl.pallas_call(
        body, name=name, grid=(T // tl, C // tc),
        in_specs=[pl.BlockSpec((tl, tc), lambda i, j: (i, c0 + j)), _prev_halo_spec(tl, tc, lambda j: c0 + j),
                  pl.BlockSpec((K, tc), lambda i, j: (0, j)), pl.BlockSpec((1, tc), lambda i, j: (0, j))],
        out_specs=pl.BlockSpec((tl, tc), lambda i, j: (i, j)),
        out_shape=jax.ShapeDtypeStruct((T, C), F32),
        scratch_shapes=[pltpu.VMEM((tl + HALO, tc), F32)],
        compiler_params=_cparams("parallel", "parallel"),
    )(proj, proj, w, b)


def ssm_conv_bwd_a(proj, w, b, d_out, *, name, tile=512, tc=256):
    T = proj.shape[0]
    K, C = w.shape
    tl = min(T, tile)
    c0 = OFF_XBC // tc

    def body(x_ref, halo_ref, w_ref, b_ref, d_ref, o_ref, buf):
        _fill_prev(buf, x_ref, halo_ref, pl.program_id(0))
        u = _conv_from(buf, w_ref, b_ref, K, tl)
        s = _sigmoid(u)
        o_ref[...] = d_ref[...].astype(F32) * (s * (1.0 + u * (1.0 - s)))

    return pl.pallas_call(
        body, name=name, grid=(T // tl, C // tc),
        in_specs=[pl.BlockSpec((tl, tc), lambda i, j: (i, c0 + j)), _prev_halo_spec(tl, tc, lambda j: c0 + j),
                  pl.BlockSpec((K, tc), lambda i, j: (0, j)), pl.BlockSpec((1, tc), lambda i, j: (0, j)),
                  pl.BlockSpec((tl, tc), lambda i, j: (i, j))],
        out_specs=pl.BlockSpec((tl, tc), lambda i, j: (i, j)),
        out_shape=jax.ShapeDtypeStruct((T, C), F32),
        scratch_shapes=[pltpu.VMEM((tl + HALO, tc), F32)],
        compiler_params=_cparams("parallel", "parallel"),
    )(proj, proj, w, b, d_out)


def ffn_act(up, w, b, *, name, tile=512, tc=1408):
    T = up.shape[0]
    K = w.shape[0]
    tl = min(T, tile)
    nj = D_FF // tc

    def body(xg_ref, hg_ref, xv_ref, hv_ref, wg_ref, wv_ref, bg_ref, bv_ref, o_ref, bufg, bufv):
        i = pl.program_id(0)
        _fill_prev(bufg, xg_ref, hg_ref, i)
        _fill_prev(bufv, xv_ref, hv_ref, i)
        gate = _conv_from(bufg, wg_ref, bg_ref, K, tl)
        val = _conv_from(bufv, wv_ref, bv_ref, K, tl)
        o_ref[...] = (gate * _sigmoid(gate) * val).astype(o_ref.dtype)

    return pl.pallas_call(
        body, name=name, grid=(T // tl, nj),
        in_specs=[pl.BlockSpec((tl, tc), lambda i, j: (i, j)), _prev_halo_spec(tl, tc, lambda j: j),
                  pl.BlockSpec((tl, tc), lambda i, j: (i, nj + j)), _prev_halo_spec(tl, tc, lambda j: nj + j),
                  pl.BlockSpec((K, tc), lambda i, j: (0, j)), pl.BlockSpec((K, tc), lambda i, j: (0, nj + j)),
                  pl.BlockSpec((1, tc), lambda i, j: (0, j)), pl.BlockSpec((1, tc), lambda i, j: (0, nj + j))],
        out_specs=pl.BlockSpec((tl, tc), lambda i, j: (i, j)),
        out_shape=jax.ShapeDtypeStruct((T, D_FF), BF16),
        scratch_shapes=[pltpu.VMEM((tl + HALO, tc), F32), pltpu.VMEM((tl + HALO, tc), F32)],
        compiler_params=_cparams("parallel", "parallel"),
    )(up, up, up, up, w, w, b, b)


def ffn_act_bwd_a(up, w, b, d_act, *, name, tile=512, tc=1408):
    T = up.shape[0]
    K = w.shape[0]
    tl = min(T, tile)
    nj = D_FF // tc

    def body(xs_ref, hs_ref, xp_ref, hp_ref, ws_ref, wp_ref, bs_ref, bp_ref, d_ref, o_ref, bufs, bufp):
        i = pl.program_id(0)
        is_gate = pl.program_id(1) < nj
        _fill_prev(bufs, xs_ref, hs_ref, i)
        _fill_prev(bufp, xp_ref, hp_ref, i)
        us = _conv_from(bufs, ws_ref, bs_ref, K, tl)
        up_ = _conv_from(bufp, wp_ref, bp_ref, K, tl)
        gate = jnp.where(is_gate, us, up_)
        val = jnp.where(is_gate, up_, us)
        d = d_ref[...].astype(F32)
        s = _sigmoid(gate)
        o_ref[...] = jnp.where(is_gate, d * val * (s * (1.0 + gate * (1.0 - s))), d * (gate * s))

    par = lambda j: (j + nj) % (2 * nj)
    return pl.pallas_call(
        body, name=name, grid=(T // tl, 2 * nj),
        in_specs=[pl.BlockSpec((tl, tc), lambda i, j: (i, j)), _prev_halo_spec(tl, tc, lambda j: j),
                  pl.BlockSpec((tl, tc), lambda i, j: (i, par(j))), _prev_halo_spec(tl, tc, par),
                  pl.BlockSpec((K, tc), lambda i, j: (0, j)), pl.BlockSpec((K, tc), lambda i, j: (0, par(j))),
                  pl.BlockSpec((1, tc), lambda i, j: (0, j)), pl.BlockSpec((1, tc), lambda i, j: (0, par(j))),
                  pl.BlockSpec((tl, tc), lambda i, j: (i, j % nj))],
        out_specs=pl.BlockSpec((tl, tc), lambda i, j: (i, j)),
        out_shape=jax.ShapeDtypeStruct((T, 2 * D_FF), F32),
        scratch_shapes=[pltpu.VMEM((tl + HALO, tc), F32), pltpu.VMEM((tl + HALO, tc), F32)],
        compiler_params=_cparams("parallel", "parallel"),
    )(up, up, up, up, w, w, b, b, d_act)


def conv_bwd_b(du, x, x_off, w, *, name, out_dtype, tile=512, tc=256):
    T, C = du.shape
    K = w.shape[0]
    tl = min(T, tile)
    c0 = x_off // tc
    nblk = T // HALO

    def body(du_ref, nx_ref, x_ref, halo_ref, w_ref, dx_ref, dw_ref, db_ref, dbuf, xbuf):
        i = pl.program_id(1)
        last = pl.num_programs(1) - 1
        d = du_ref[...].astype(F32)
        dbuf[0:tl, :] = d
        dbuf[tl:, :] = jnp.where(i < last, nx_ref[...].astype(F32), 0.0)
        _fill_prev(xbuf, x_ref, halo_ref, i)

        @pl.when(i == 0)
        def _():
            dw_ref[...] = jnp.zeros_like(dw_ref)
            db_ref[...] = jnp.zeros_like(db_ref)

        dx = jnp.zeros((tl, tc), F32)
        for k in range(K):
            s = K - 1 - k
            dx = dx + dbuf[pl.ds(s, tl), :] * w_ref[k:k + 1, :].astype(F32)
            dw_ref[k:k + 1, :] += jnp.sum(d * xbuf[pl.ds(HALO - s, tl), :], axis=0, keepdims=True)
        db_ref[...] += jnp.sum(d, axis=0, keepdims=True)
        dx_ref[...] = dx.astype(dx_ref.dtype)

    return pl.pallas_call(
        body, name=name, grid=(C // tc, T // tl),
        in_specs=[pl.BlockSpec((tl, tc), lambda j, i: (i, j)),
                  pl.BlockSpec((HALO, tc), lambda j, i: (jnp.minimum((i + 1) * (tl // HALO), nblk - 1), j)),
                  pl.BlockSpec((tl, tc), lambda j, i: (i, c0 + j)),
                  pl.BlockSpec((HALO, tc), lambda j, i: (jnp.maximum(i * (tl // HALO) - 1, 0), c0 + j)),
                  pl.BlockSpec((K, tc), lambda j, i: (0, j))],
        out_specs=[pl.BlockSpec((tl, tc), lambda j, i: (i, j)), pl.BlockSpec((K, tc), lambda j, i: (0, j)),
                   pl.BlockSpec((1, tc), lambda j, i: (0, j))],
        out_shape=[jax.ShapeDtypeStruct((T, C), out_dtype), jax.ShapeDtypeStruct((K, C), F32),
                   jax.ShapeDtypeStruct((1, C), F32)],
        scratch_shapes=[pltpu.VMEM((tl + HALO, tc), F32), pltpu.VMEM((tl + HALO, tc), F32)],
        compiler_params=_cparams("parallel", "arbitrary"),
    )(du, du, x, x, w)


def _attn_tiles(T):
    return min(T, 1024), min(T, 256)


def _below_diag(rows, cols, strict):
    d = lax.broadcasted_iota(jnp.int32, (rows, cols), 0) - lax.broadcasted_iota(jnp.int32, (rows, cols), 1)
    return d > 0 if strict else d >= 0


def _log_gates(z):
    l1p = jnp.log(1.0 + jnp.exp(-jnp.abs(z)))
    a = jnp.minimum(z, 0.0) - l1p
    return a, a - z


def _causal_sweep(i, tq, tk, block, descending):
    nb = tq // tk
    n_full = i * nb

    def band():
        order = reversed(range(nb)) if descending else range(nb)
        for bb in order:
            block(pl.multiple_of(i * tq + bb * tk, tk), bb * tk, True)

    def full():
        def step(j, c):
            kb = (n_full - 1 - j) if descending else j
            block(pl.multiple_of(kb * tk, tk), 0, False)
            return c
        lax.fori_loop(0, n_full, step, 0)

    if descending:
        band()
        full()
    else:
        full()
        band()


def sb_fwd(q, k, v, *, name):
    H, T, dh = q.shape
    tq, tk = _attn_tiles(T)

    def body(q_ref, k_ref, v_ref, y_ref, bt_ref, acc, run):
        acc[...] = jnp.zeros_like(acc)
        run[...] = jnp.zeros_like(run)
        u_after = _tri(tk, lambda r, c: r > c)

        def block(k0, r0, masked):
            kb = k_ref[pl.ds(k0, tk), :]
            vb = v_ref[pl.ds(k0, tk), :]
            z = lax.dot_general(q_ref[r0:, :], kb, NT, preferred_element_type=F32)
            a, b = _log_gates(z)
            if masked:
                valid = _below_diag(tq - r0, tk, True)
                b = jnp.where(valid, b, 0.0)
            w = jnp.exp(a + _split_dot(b, u_after, 2) + run[r0:, :])
            if masked:
                w = jnp.where(valid, w, 0.0)
            acc[r0:, :] += jnp.dot(w.astype(BF16), vb, preferred_element_type=F32)
            run[r0:, :] += jnp.sum(b, axis=1, keepdims=True)

        _causal_sweep(pl.program_id(1), tq, tk, block, descending=True)
        y_ref[...] = acc[...]
        bt_ref[...] = jnp.broadcast_to(run[...], (tq, LANES))

    return pl.pallas_call(
        body, name=name, grid=(H, T // tq),
        in_specs=[pl.BlockSpec((None, tq, dh), lambda h, i: (h, i, 0)),
                  pl.BlockSpec((None, T, dh), lambda h, i: (h, 0, 0)),
                  pl.BlockSpec((None, T, dh), lambda h, i: (h, 0, 0))],
        out_specs=[pl.BlockSpec((None, tq, dh), lambda h, i: (h, i, 0)),
                   pl.BlockSpec((None, tq, LANES), lambda h, i: (h, i, 0))],
        out_shape=[jax.ShapeDtypeStruct((H, T, dh), F32), jax.ShapeDtypeStruct((H, T, LANES), F32)],
        scratch_shapes=[pltpu.VMEM((tq, dh), F32), pltpu.VMEM((tq, 1), F32)],
        compiler_params=_cparams("parallel", "parallel"),
    )(q, k, v)


def sb_bwd(q, k, v, dy, btot, *, name, q_scale):
    H, T, dh = q.shape
    tq, tk = _attn_tiles(T)

    def body(q_ref, k_ref, v_ref, dy_ref, bt_ref, dq_ref, dk_ref, dv_ref, dq, pb, pg, dyb):
        @pl.when(pl.program_id(1) == 0)
        def _():
            dk_ref[...] = jnp.zeros_like(dk_ref)
            dv_ref[...] = jnp.zeros_like(dv_ref)

        dq[...] = jnp.zeros_like(dq)
        pb[...] = jnp.zeros_like(pb)
        pg[...] = jnp.zeros_like(pg)
        dyb[...] = dy_ref[...].astype(BF16)
        u_upto = _tri(tk, lambda r, c: r <= c)
        u_before = _tri(tk, lambda r, c: r < c)

        def block(k0, r0, masked):
            kb = k_ref[pl.ds(k0, tk), :]
            vb = v_ref[pl.ds(k0, tk), :]
            qv = q_ref[r0:, :]
            dyv = dyb[r0:, :]
            z = lax.dot_general(qv, kb, NT, preferred_element_type=F32)
            a, b = _log_gates(z)
            if masked:
                valid = _below_diag(tq - r0, tk, True)
                b = jnp.where(valid, b, 0.0)
            w = jnp.exp(a + (bt_ref[r0:, 0:1] - pb[r0:, :] - _split_dot(b, u_upto, 2)))
            if masked:
                w = jnp.where(valid, w, 0.0)
            g = w * lax.dot_general(dyv, vb, NT, preferred_element_type=F32)
            dz = g - jnp.exp(a) * (g + pg[r0:, :] + _split_dot(g, u_before, 2))
            if masked:
                dz = jnp.where(valid, dz, 0.0)
            dz = dz.astype(BF16)
            dq[r0:, :] += jnp.dot(dz, kb, preferred_element_type=F32)
            dk_ref[pl.ds(k0, tk), :] += lax.dot_general(dz, qv, TN, preferred_element_type=F32)
            dv_ref[pl.ds(k0, tk), :] += lax.dot_general(w.astype(BF16), dyv, TN, preferred_element_type=F32)
            pb[r0:, :] += jnp.sum(b, axis=1, keepdims=True)
            pg[r0:, :] += jnp.sum(g, axis=1, keepdims=True)

        _causal_sweep(pl.program_id(1), tq, tk, block, descending=False)
        dq_ref[...] = dq[...] * q_scale

    return pl.pallas_call(
        body, name=name, grid=(H, T // tq),
        in_specs=[pl.BlockSpec((None, tq, dh), lambda h, i: (h, i, 0)),
                  pl.BlockSpec((None, T, dh), lambda h, i: (h, 0, 0)),
                  pl.BlockSpec((None, T, dh), lambda h, i: (h, 0, 0)),
                  pl.BlockSpec((None, tq, dh), lambda h, i: (h, i, 0)),
                  pl.BlockSpec((None, tq, LANES), lambda h, i: (h, i, 0))],
        out_specs=[pl.BlockSpec((None, tq, dh), lambda h, i: (h, i, 0)),
                   pl.BlockSpec((None, T, dh), lambda h, i: (h, 0, 0)),
                   pl.BlockSpec((None, T, dh), lambda h, i: (h, 0, 0))],
        out_shape=[jax.ShapeDtypeStruct((H, T, dh), F32)] * 3,
        scratch_shapes=[pltpu.VMEM((tq, dh), F32), pltpu.VMEM((tq, 1), F32), pltpu.VMEM((tq, 1), F32),
                        pltpu.VMEM((tq, dh), BF16)],
        compiler_params=_cparams("parallel", "arbitrary"),
    )(q, k, v, dy, btot)


NEG = -1e30


def mla_fwd(q, k, v, *, name):
    H, T, dk = q.shape
    dv = v.shape[2]
    tq, tk = _attn_tiles(T)

    def body(q_ref, k_ref, v_ref, o_ref, l_ref, acc, m_s, l_s):
        acc[...] = jnp.zeros_like(acc)
        m_s[...] = jnp.full_like(m_s, NEG)
        l_s[...] = jnp.zeros_like(l_s)

        def block(k0, r0, masked):
            kb = k_ref[pl.ds(k0, tk), :]
            vb = v_ref[pl.ds(k0, tk), :]
            s = lax.dot_general(q_ref[r0:, :], kb, NT, preferred_element_type=F32)
            if masked:
                s = jnp.where(_below_diag(tq - r0, tk, False), s, NEG)
            m = m_s[r0:, :]
            m_new = jnp.maximum(m, jnp.max(s, axis=1, keepdims=True))
            p = jnp.exp(s - m_new)
            alpha = jnp.exp(m - m_new)
            l_s[r0:, :] = alpha * l_s[r0:, :] + jnp.sum(p, axis=1, keepdims=True)
            acc[r0:, :] = alpha * acc[r0:, :] + jnp.dot(p.astype(BF16), vb, preferred_element_type=F32)
            m_s[r0:, :] = m_new

        _causal_sweep(pl.program_id(1), tq, tk, block, descending=False)
        o_ref[...] = acc[...] / l_s[...]
        l_ref[...] = jnp.broadcast_to(m_s[...] + jnp.log(l_s[...]), (tq, LANES))

    return pl.pallas_call(
        body, name=name, grid=(H, T // tq),
        in_specs=[pl.BlockSpec((None, tq, dk), lambda h, i: (h, i, 0)),
                  pl.BlockSpec((None, T, dk), lambda h, i: (h, 0, 0)),
                  pl.BlockSpec((None, T, dv), lambda h, i: (h, 0, 0))],
        out_specs=[pl.BlockSpec((None, tq, dv), lambda h, i: (h, i, 0)),
                   pl.BlockSpec((None, tq, LANES), lambda h, i: (h, i, 0))],
        out_shape=[jax.ShapeDtypeStruct((H, T, dv), F32), jax.ShapeDtypeStruct((H, T, LANES), F32)],
        scratch_shapes=[pltpu.VMEM((tq, dv), F32), pltpu.VMEM((tq, 1), F32), pltpu.VMEM((tq, 1), F32)],
        compiler_params=_cparams("parallel", "parallel"),
    )(q, k, v)


def mla_bwd(q, k, v, do, o, lse, *, name):
    H, T, dk = q.shape
    dv = v.shape[2]
    tq, tk = _attn_tiles(T)

    def body(q_ref, k_ref, v_ref, do_ref, o_ref, l_ref, dq_ref, dk_ref, dv_ref, dq, delta, dob):
        @pl.when(pl.program_id(1) == 0)
        def _():
            dk_ref[...] = jnp.zeros_like(dk_ref)
            dv_ref[...] = jnp.zeros_like(dv_ref)

        dq[...] = jnp.zeros_like(dq)
        dov = do_ref[...].astype(F32)
        dob[...] = dov.astype(BF16)
        delta[...] = jnp.sum(dov * o_ref[...], axis=1, keepdims=True)

        def block(k0, r0, masked):
            kb = k_ref[pl.ds(k0, tk), :]
            vb = v_ref[pl.ds(k0, tk), :]
            qv = q_ref[r0:, :]
            dov_b = dob[r0:, :]
            s = lax.dot_general(qv, kb, NT, preferred_element_type=F32)
            p = jnp.exp(s - l_ref[r0:, 0:1])
            if masked:
                p = jnp.where(_below_diag(tq - r0, tk, False), p, 0.0)
            dp = lax.dot_general(dov_b, vb, NT, preferred_element_type=F32)
            ds = (p * (dp - delta[r0:, :])).astype(BF16)
            dq[r0:, :] += jnp.dot(ds, kb, preferred_element_type=F32)
            dk_ref[pl.ds(k0, tk), :] += lax.dot_general(ds, qv, TN, preferred_element_type=F32)
            dv_ref[pl.ds(k0, tk), :] += lax.dot_general(p.astype(BF16), dov_b, TN, preferred_element_type=F32)

        _causal_sweep(pl.program_id(1), tq, tk, block, descending=False)
        dq_ref[...] = dq[...]

    return pl.pallas_call(
        body, name=name, grid=(H, T // tq),
        in_specs=[pl.BlockSpec((None, tq, dk), lambda h, i: (h, i, 0)),
                  pl.BlockSpec((None, T, dk), lambda h, i: (h, 0, 0)),
                  pl.BlockSpec((None, T, dv), lambda h, i: (h, 0, 0)),
                  pl.BlockSpec((None, tq, dv), lambda h, i: (h, i, 0)),
                  pl.BlockSpec((None, tq, dv), lambda h, i: (h, i, 0)),
                  pl.BlockSpec((None, tq, LANES), lambda h, i: (h, i, 0))],
        out_specs=[pl.BlockSpec((None, tq, dk), lambda h, i: (h, i, 0)),
                   pl.BlockSpec((None, T, dk), lambda h, i: (h, 0, 0)),
                   pl.BlockSpec((None, T, dv), lambda h, i: (h, 0, 0))],
        out_shape=[jax.ShapeDtypeStruct((H, T, dk), F32), jax.ShapeDtypeStruct((H, T, dk), F32),
                   jax.ShapeDtypeStruct((H, T, dv), F32)],
        scratch_shapes=[pltpu.VMEM((tq, dk), F32), pltpu.VMEM((tq, 1), F32), pltpu.VMEM((tq, dv), BF16)],
        compiler_params=_cparams("parallel", "arbitrary"),
    )(q, k, v, do, o, lse)


def _lane_pick(x, h):
    lane = lax.broadcasted_iota(jnp.int32, (1, x.shape[1]), 1)
    return jnp.sum(jnp.where(lane == h, x, 0.0), axis=1, keepdims=True)


def _row_pick(x, h):
    sub = lax.broadcasted_iota(jnp.int32, (x.shape[0], 1), 0)
    return jnp.sum(jnp.where(sub == h, x, 0.0), axis=0, keepdims=True)


def ssd_chunk_fn(*args):
    nh, ng = SSM_HEADS, SSM_GROUPS
    xs = args[:nh]
    bs = args[nh:nh + ng]
    cs = args[nh + ng:nh + 2 * ng]
    dt_raw = args[nh + 2 * ng]
    st = args[nh + 2 * ng + 1:nh + 2 * ng + 1 + nh]
    dt_bias, a_log, d_skip = args[nh + 2 * ng + 1 + nh:]
    L = dt_raw.shape[0]
    dt = _softplus(dt_raw + dt_bias)
    da = dt * (-jnp.exp(a_log))
    dcs = csum_rows(da)
    dcs_t = dcs.T
    total = jnp.sum(da, axis=0, keepdims=True)
    causal = lax.broadcasted_iota(jnp.int32, (L, L), 0) >= lax.broadcasted_iota(jnp.int32, (L, L), 1)
    cb = [mm_nt(cs[g], bs[g]) for g in range(ng)]
    ys, new_st = [], []
    for h in range(nh):
        g = h // (nh // ng)
        dcs_h = _lane_pick(dcs, h)
        dt_h = _lane_pick(dt, h)
        tot_h = _lane_pick(total, h)
        dsk_h = _lane_pick(d_skip, h)
        decay = jnp.exp(jnp.where(causal, dcs_h - _row_pick(dcs_t, h), NEG))
        xdt = xs[h] * dt_h
        y = mm_nn(cb[g] * decay, xdt)
        y = y + mm_nn(cs[g] * jnp.exp(dcs_h), st[h])
        ys.append(y + xs[h] * dsk_h)
        new_st.append(st[h] * jnp.exp(tot_h) + mm_tn(bs[g] * jnp.exp(tot_h - dcs_h), xdt))
    return tuple(ys) + tuple(new_st)


def ssd_fwd(x_hm, b_hm, c_hm, proj, dt_bias, a_log, d_skip, *, name):
    nh, T, P = x_hm.shape
    ng, N = b_hm.shape[0], b_hm.shape[2]
    L = SSM_CHUNK
    nc = T // L
    dtb = OFF_DT // LANES

    def body(x_ref, b_ref, c_ref, dt_ref, db_ref, al_ref, ds_ref, y_ref, s_ref, state):
        @pl.when(pl.program_id(0) == 0)
        def _():
            state[...] = jnp.zeros_like(state)

        s_ref[...] = state[...]
        args = ([x_ref[h] for h in range(nh)] + [b_ref[g] for g in range(ng)] + [c_ref[g] for g in range(ng)]
                + [dt_ref[...]] + [state[h] for h in range(nh)] + [db_ref[...], al_ref[...], ds_ref[...]])
        res = ssd_chunk_fn(*args)
        for h in range(nh):
            y_ref[h] = res[h]
            state[h] = res[nh + h]

    par = pl.BlockSpec((1, LANES), lambda i: (0, 0))
    return pl.pallas_call(
        body, name=name, grid=(nc,),
        in_specs=[pl.BlockSpec((nh, L, P), lambda i: (0, i, 0)), pl.BlockSpec((ng, L, N), lambda i: (0, i, 0)),
                  pl.BlockSpec((ng, L, N), lambda i: (0, i, 0)), pl.BlockSpec((L, LANES), lambda i: (i, dtb)),
                  par, par, par],
        out_specs=[pl.BlockSpec((nh, L, P), lambda i: (0, i, 0)),
                   pl.BlockSpec((None, nh, N, P), lambda i: (i, 0, 0, 0))],
        out_shape=[jax.ShapeDtypeStruct((nh, T, P), F32), jax.ShapeDtypeStruct((nc, nh, N, P), F32)],
        scratch_shapes=[pltpu.VMEM((nh, N, P), F32)],
        compiler_params=_cparams("arbitrary"),
    )(x_hm, b_hm, c_hm, proj, dt_bias, a_log, d_skip)


def ssd_bwd(x_hm, b_hm, c_hm, proj, states, dt_bias, a_log, d_skip, dy_hm, *, name):
    nh, T, P = x_hm.shape
    ng, N = b_hm.shape[0], b_hm.shape[2]
    L = SSM_CHUNK
    nc = T // L
    dtb = OFF_DT // LANES

    def body(x_ref, b_ref, c_ref, dt_ref, s_ref, db_ref, al_ref, ds_ref, dy_ref,
             dx_ref, dbm_ref, dcm_ref, ddt_ref, gdb_ref, gal_ref, gds_ref, dstate):
        @pl.when(pl.program_id(0) == 0)
        def _():
            dstate[...] = jnp.zeros_like(dstate)
            gdb_ref[...] = jnp.zeros_like(gdb_ref)
            gal_ref[...] = jnp.zeros_like(gal_ref)
            gds_ref[...] = jnp.zeros_like(gds_ref)

        args = ([x_ref[h] for h in range(nh)] + [b_ref[g] for g in range(ng)] + [c_ref[g] for g in range(ng)]
                + [dt_ref[...]] + [s_ref[h] for h in range(nh)] + [db_ref[...], al_ref[...], ds_ref[...]])
        _, vjp = jax.vjp(ssd_chunk_fn, *args)
        g = vjp(tuple([dy_ref[h] for h in range(nh)] + [dstate[h] for h in range(nh)]))
        for h in range(nh):
            dx_ref[h] = g[h]
        for gi in range(ng):
            dbm_ref[gi] = g[nh + gi]
            dcm_ref[gi] = g[nh + ng + gi]
        ddt_ref[...] = g[nh + 2 * ng]
        for h in range(nh):
            dstate[h] = g[nh + 2 * ng + 1 + h]
        gdb_ref[...] += g[-3]
        gal_ref[...] += g[-2]
        gds_ref[...] += g[-1]

    rev = lambda i: nc - 1 - i
    par = pl.BlockSpec((1, LANES), lambda i: (0, 0))
    return pl.pallas_call(
        body, name=name, grid=(nc,),
        in_specs=[pl.BlockSpec((nh, L, P), lambda i: (0, rev(i), 0)), pl.BlockSpec((ng, L, N), lambda i: (0, rev(i), 0)),
                  pl.BlockSpec((ng, L, N), lambda i: (0, rev(i), 0)), pl.BlockSpec((L, LANES), lambda i: (rev(i), dtb)),
                  pl.BlockSpec((None, nh, N, P), lambda i: (rev(i), 0, 0, 0)), par, par, par,
                  pl.BlockSpec((nh, L, P), lambda i: (0, rev(i), 0))],
        out_specs=[pl.BlockSpec((nh, L, P), lambda i: (0, rev(i), 0)), pl.BlockSpec((ng, L, N), lambda i: (0, rev(i), 0)),
                   pl.BlockSpec((ng, L, N), lambda i: (0, rev(i), 0)), pl.BlockSpec((L, LANES), lambda i: (rev(i), 0)),
                   par, par, par],
        out_shape=[jax.ShapeDtypeStruct((nh, T, P), F32), jax.ShapeDtypeStruct((ng, T, N), F32),
                   jax.ShapeDtypeStruct((ng, T, N), F32), jax.ShapeDtypeStruct((T, LANES), F32),
                   jax.ShapeDtypeStruct((1, LANES), F32), jax.ShapeDtypeStruct((1, LANES), F32),
                   jax.ShapeDtypeStruct((1, LANES), F32)],
        scratch_shapes=[pltpu.VMEM((nh, N, P), F32)],
        compiler_params=_cparams("arbitrary"),
    )(x_hm, b_hm, c_hm, proj, states, dt_bias, a_log, d_skip, dy_hm)


def loss_head(h, target, g, *, name, tile=512):
    T, C = h.shape
    tl = min(T, tile)

    def body(h_ref, t_ref, g_ref, dh_ref, dg_ref, ls_ref):
        @pl.when(pl.program_id(0) == 0)
        def _():
            dg_ref[...] = jnp.zeros_like(dg_ref)
            ls_ref[...] = jnp.zeros_like(ls_ref)

        (y,), vjp = jax.vjp(rms_fn, h_ref[...], g_ref[...])
        err = y - t_ref[...]
        ls_ref[...] += jnp.sum(err * err, axis=0, keepdims=True) * (0.5 / C)
        dh, dg = vjp((err * (1.0 / C),))
        dh_ref[...] = dh
        dg_ref[...] += dg

    row = pl.BlockSpec((tl, C), lambda i: (i, 0))
    par = pl.BlockSpec((1, C), lambda i: (0, 0))
    return pl.pallas_call(
        body, name=name, grid=(T // tl,), in_specs=[row, row, par], out_specs=[row, par, par],
        out_shape=[jax.ShapeDtypeStruct((T, C), F32), jax.ShapeDtypeStruct((1, C), F32),
                   jax.ShapeDtypeStruct((1, C), F32)],
        compiler_params=_cparams("arbitrary"),
    )(h, target, g)


def adamw(w, g, m, v, *, name):
    R, C = w.shape
    tr = R
    for d in range(8, min(R, 512) + 1, 8):
        if R % d == 0:
            tr = d
    c1 = 1.0 - ADAM_B1 ** ADAM_STEP
    c2 = 1.0 - ADAM_B2 ** ADAM_STEP

    def body(w_ref, g_ref, m_ref, v_ref, d_ref, nm_ref, nv_ref):
        gv = g_ref[...]
        nm = ADAM_B1 * m_ref[...] + (1.0 - ADAM_B1) * gv
        nv = ADAM_B2 * v_ref[...] + (1.0 - ADAM_B2) * (gv * gv)
        d_ref[...] = -ADAM_LR * ((nm / c1) / (jnp.sqrt(nv / c2) + ADAM_EPS) + ADAM_WD * w_ref[...])
        nm_ref[...] = nm
        nv_ref[...] = nv

    spec = pl.BlockSpec((tr, C), lambda i: (i, 0))
    return pl.pallas_call(
        body, name=name, grid=(R // tr,), in_specs=[spec] * 4, out_specs=[spec] * 3,
        out_shape=[jax.ShapeDtypeStruct((R, C), F32)] * 3,
        compiler_params=_cparams("parallel"),
    )(w, g, m, v)


MESH = pl.DeviceIdType.MESH
HBM_SPEC = pl.BlockSpec(memory_space=pltpu.HBM)


def _place():
    return lax.axis_index("x"), lax.axis_index("y"), lax.axis_index("c")


def allgather_blocks(mine, *, name):
    R = mine.shape[0]

    def body(x_ref, out_ref, send_sems, recv_sems, local_sem):
        x, y, c = _place()
        me, sibling = (x, y, c), (x, y, 1 - c)
        chips = [(1 - x, y), (x, 1 - y), (1 - x, 1 - y)]

        def slot(px, py, pc):
            return out_ref.at[4 * px + 2 * py + pc]

        def copy(k, block, to, src=None):
            return pltpu.make_async_remote_copy(
                src_ref=slot(*block) if src is None else src, dst_ref=slot(*block),
                send_sem=send_sems.at[k], recv_sem=recv_sems.at[k], device_id=to, device_id_type=MESH)

        own = pltpu.make_async_copy(x_ref, slot(*me), local_sem)
        own.start()
        first = [copy(0, me, sibling, src=x_ref)]
        first += [copy(1 + j, me, (*chip, c), src=x_ref) for j, chip in enumerate(chips)]
        for cp in first:
            cp.start()
        passed = [copy(4 + j, (*chip, c), sibling) for j, chip in enumerate(chips)]
        for j, chip in enumerate(chips):
            copy(1 + j, (*chip, c), me).wait_recv()
            passed[j].start()
        copy(0, sibling, me).wait_recv()
        for j, chip in enumerate(chips):
            copy(4 + j, (*chip, 1 - c), me).wait_recv()
        for cp in first + passed:
            cp.wait_send()
        own.wait()

    return pl.pallas_call(
        body, name=name, out_shape=jax.ShapeDtypeStruct((8, R, LANES), mine.dtype),
        in_specs=[HBM_SPEC], out_specs=HBM_SPEC,
        scratch_shapes=[pltpu.SemaphoreType.DMA((7,)), pltpu.SemaphoreType.DMA((7,)), pltpu.SemaphoreType.DMA],
    )(mine)


def allgather_direct(mine, *, name):
    R = mine.shape[0]

    def body(x_ref, out_ref, send_sems, recv_sems, local_sem):
        x, y, c = _place()
        own = pltpu.make_async_copy(x_ref, out_ref.at[4 * x + 2 * y + c], local_sem)
        own.start()
        sends = []
        for f in range(1, 8):
            fx, fy, fc = (f >> 2) & 1, (f >> 1) & 1, f & 1
            px, py, pc = jnp.where(fx, 1 - x, x), jnp.where(fy, 1 - y, y), jnp.where(fc, 1 - c, c)
            sends.append(pltpu.make_async_remote_copy(
                src_ref=x_ref, dst_ref=out_ref.at[4 * x + 2 * y + c], send_sem=send_sems.at[f - 1],
                recv_sem=recv_sems.at[f - 1], device_id=(px, py, pc), device_id_type=MESH))
        for cp in sends:
            cp.start()
        for f in range(1, 8):
            fx, fy, fc = (f >> 2) & 1, (f >> 1) & 1, f & 1
            px, py, pc = jnp.where(fx, 1 - x, x), jnp.where(fy, 1 - y, y), jnp.where(fc, 1 - c, c)
            pltpu.make_async_remote_copy(
                src_ref=x_ref, dst_ref=out_ref.at[4 * px + 2 * py + pc], send_sem=send_sems.at[f - 1],
                recv_sem=recv_sems.at[f - 1], device_id=(px, py, pc), device_id_type=MESH).wait_recv()
        for cp in sends:
            cp.wait_send()
        own.wait()

    return pl.pallas_call(
        body, name=name, out_shape=jax.ShapeDtypeStruct((8, R, LANES), mine.dtype),
        in_specs=[HBM_SPEC], out_specs=HBM_SPEC,
        scratch_shapes=[pltpu.SemaphoreType.DMA((7,)), pltpu.SemaphoreType.DMA((7,)), pltpu.SemaphoreType.DMA],
    )(mine)


def send_to_sibling(v, *, name):
    def body(v_ref, out_ref, send_sem, recv_sem):
        x, y, c = _place()
        cp = pltpu.make_async_remote_copy(src_ref=v_ref, dst_ref=out_ref, send_sem=send_sem, recv_sem=recv_sem,
                                          device_id=(x, y, 1 - c), device_id_type=MESH)
        cp.start()
        cp.wait()

    return pl.pallas_call(
        body, name=name, out_shape=jax.ShapeDtypeStruct(v.shape, v.dtype), in_specs=[HBM_SPEC], out_specs=HBM_SPEC,
        scratch_shapes=[pltpu.SemaphoreType.DMA, pltpu.SemaphoreType.DMA],
    )(v)


def pair_gather(v, *, name):
    def body(v_ref, out_ref, send_sem, recv_sem, local_sem):
        x, y, c = _place()
        own = pltpu.make_async_copy(v_ref, out_ref.at[c], local_sem)
        own.start()
        cp = pltpu.make_async_remote_copy(src_ref=v_ref, dst_ref=out_ref.at[c], send_sem=send_sem, recv_sem=recv_sem,
                                          device_id=(x, y, 1 - c), device_id_type=MESH)
        cp.start()
        pltpu.make_async_remote_copy(src_ref=v_ref, dst_ref=out_ref.at[1 - c], send_sem=send_sem, recv_sem=recv_sem,
                                     device_id=(x, y, 1 - c), device_id_type=MESH).wait_recv()
        cp.wait_send()
        own.wait()

    return pl.pallas_call(
        body, name=name, out_shape=jax.ShapeDtypeStruct((2,) + v.shape, v.dtype), in_specs=[HBM_SPEC],
        out_specs=HBM_SPEC,
        scratch_shapes=[pltpu.SemaphoreType.DMA, pltpu.SemaphoreType.DMA, pltpu.SemaphoreType.DMA],
    )(v)


def chip_exchange(p, *, name):
    R = p.shape[1]

    def body(p_ref, out_ref, send_sems, recv_sems):
        x, y, c = _place()
        chips = [(1 - x, y), (x, 1 - y), (1 - x, 1 - y)]
        sends = [pltpu.make_async_remote_copy(
            src_ref=p_ref.at[2 * px + py], dst_ref=out_ref.at[j], send_sem=send_sems.at[j], recv_sem=recv_sems.at[j],
            device_id=(px, py, c), device_id_type=MESH) for j, (px, py) in enumerate(chips)]
        for cp in sends:
            cp.start()
        for cp in sends:
            cp.wait()

    return pl.pallas_call(
        body, name=name, out_shape=jax.ShapeDtypeStruct((3, R, LANES), p.dtype), in_specs=[HBM_SPEC],
        out_specs=HBM_SPEC,
        scratch_shapes=[pltpu.SemaphoreType.DMA((3,)), pltpu.SemaphoreType.DMA((3,))],
    )(p)


def add_blocks(terms, out_dtype, *, name, tile=1024):
    R = terms[0].shape[0]
    tr = R
    for d in range(16, min(R, tile) + 1, 16):
        if R % d == 0:
            tr = d

    def body(*refs):
        acc = refs[0][...].astype(F32)
        for ref in refs[1:-1]:
            acc = acc + ref[...].astype(F32)
        refs[-1][...] = acc.astype(out_dtype)

    spec = pl.BlockSpec((tr, LANES), lambda i: (i, 0))
    return pl.pallas_call(
        body, name=name, grid=(R // tr,), in_specs=[spec] * len(terms), out_specs=spec,
        out_shape=jax.ShapeDtypeStruct((R, LANES), out_dtype), compiler_params=_cparams("parallel"),
    )(*terms)


def _half_rows(arr, cc):
    hr = arr.shape[1] // 2
    return lax.dynamic_slice_in_dim(arr, cc * hr, hr, axis=1).reshape(-1)


def _flat_half(shards, cc, dtype):
    flat = jnp.concatenate([_half_rows(shards[n], cc).astype(dtype) for n in BIG])
    return flat.reshape(-1, LANES)


def _unflat_halves(flat_by_c, shapes):
    out, off = {}, 0
    for n in BIG:
        _, R, C = shapes[n]
        sz = 2 * (R // 2) * C
        out[n] = jnp.concatenate([flat_by_c[c][off:off + sz].reshape(2, R // 2, C) for c in range(2)], axis=1)
        off += sz
    return out


def _to_heads(a, nh):
    T = a.shape[0]
    return a.reshape(T, nh, a.shape[1] // nh).transpose(1, 0, 2)


def _from_heads(a):
    nh, T, d = a.shape
    return a.transpose(1, 0, 2).reshape(T, nh * d)


def _pad_cols(a, n):
    return jnp.pad(a, ((0, 0), (0, n - a.shape[1])))


def _pack_w_in(w):
    offs = [sum(IN_SPLITS[:i]) for i in range(len(IN_SPLITS) + 1)]
    sb, z, xbc, dt, cq, ckv, kr = [w[:, offs[i]:offs[i + 1]] for i in range(len(IN_SPLITS))]
    zeros = lambda n: jnp.zeros((w.shape[0], n), w.dtype)
    h = MLA_ROPE // 2
    kra = jnp.concatenate([zeros(MLA_NOPE), kr, zeros(LANES - MLA_QK)], axis=1)
    krb = jnp.concatenate([zeros(MLA_NOPE), -kr[:, h:], kr[:, :h], zeros(LANES - MLA_QK)], axis=1)
    return jnp.concatenate([sb, xbc, z, cq, ckv, _pad_cols(dt, LANES), kra, krb], axis=1)


def _unpack_gw_in(g):
    h = MLA_ROPE // 2
    ga, gb = g[:, OFF_KRA:OFF_KRA + LANES], g[:, OFF_KRB:OFF_KRB + LANES]
    gkr = ga[:, MLA_NOPE:MLA_QK] + jnp.concatenate([gb[:, MLA_NOPE + h:MLA_QK], -gb[:, MLA_NOPE:MLA_NOPE + h]], axis=1)
    return jnp.concatenate([g[:, OFF_SB:OFF_SB + 768], g[:, OFF_Z:OFF_Z + 512], g[:, OFF_XBC:OFF_XBC + 768],
                            g[:, OFF_DT:OFF_DT + 8], g[:, OFF_CQ:OFF_CQ + 256], g[:, OFF_CKV:OFF_CKV + 128], gkr], axis=1)


def _pack_w_uq(w):
    zeros = lambda n: jnp.zeros((w.shape[0], n), w.dtype)
    h = MLA_ROPE // 2
    pp, rr = [], []
    for i in range(MLA_HEADS):
        nope = w[:, MLA_QK * i:MLA_QK * i + MLA_NOPE]
        rope = w[:, MLA_QK * i + MLA_NOPE:MLA_QK * (i + 1)]
        pp += [nope, rope, zeros(LANES - MLA_QK)]
        rr += [zeros(MLA_NOPE), -rope[:, h:], rope[:, :h], zeros(LANES - MLA_QK)]
    return jnp.concatenate(pp, axis=1), jnp.concatenate(rr, axis=1)


def _unpack_gw_uq(gp, gr):
    h = MLA_ROPE // 2
    out = []
    for i in range(MLA_HEADS):
        b = LANES * i
        out.append(gp[:, b:b + MLA_NOPE])
        out.append(gp[:, b + MLA_NOPE:b + MLA_NOPE + h] + gr[:, b + MLA_NOPE + h:b + MLA_QK])
        out.append(gp[:, b + MLA_NOPE + h:b + MLA_QK] - gr[:, b + MLA_NOPE:b + MLA_NOPE + h])
    return jnp.concatenate(out, axis=1)


def _pack_w_ukv(w):
    zeros = lambda n: jnp.zeros((w.shape[0], n), w.dtype)
    kk, vv = [], []
    for i in range(MLA_HEADS):
        b = (MLA_NOPE + MLA_V) * i
        kk += [w[:, b:b + MLA_NOPE], zeros(LANES - MLA_NOPE)]
        vv.append(w[:, b + MLA_NOPE:b + MLA_NOPE + MLA_V])
    return jnp.concatenate(kk, axis=1), jnp.concatenate(vv, axis=1)


def _unpack_gw_ukv(gk, gv):
    out = []
    for i in range(MLA_HEADS):
        out += [gk[:, LANES * i:LANES * i + MLA_NOPE], gv[:, MLA_V * i:MLA_V * (i + 1)]]
    return jnp.concatenate(out, axis=1)


def _rope_tables(positions):
    inv_freq = 1.0 / (ROPE_THETA ** (jnp.arange(0, MLA_ROPE, 2, dtype=F32) / MLA_ROPE))
    ang = positions.astype(F32)[:, None] * inv_freq
    cos, sin = jnp.cos(ang), jnp.sin(ang)
    T = positions.shape[0]
    one, zero = jnp.ones((T, MLA_NOPE), F32), jnp.zeros((T, MLA_NOPE), F32)
    pad1, pad0 = jnp.ones((T, LANES - MLA_QK), F32), jnp.zeros((T, LANES - MLA_QK), F32)
    return jnp.concatenate([one, cos, cos, pad1], axis=1), jnp.concatenate([zero, sin, sin, pad0], axis=1)


def _row(v):
    return v.reshape(1, -1)


def _pad_row(v):
    return _pad_cols(v.reshape(1, -1), LANES)


def _layer_weights(full, small, li):
    p = {}
    p["w_in_p"] = _pack_w_in(full["w_in"][li])
    p["w_in_pt"] = p["w_in_p"].T
    p["wqp"], p["wqr"] = _pack_w_uq(full["mla_w_uq"][li])
    p["wkp"], p["wvp"] = _pack_w_ukv(full["mla_w_ukv"][li])
    p["w_out"] = full["w_out"][li]
    p["w_out_t"] = p["w_out"].T
    p["w_up"] = full["ffn_w_up"][li]
    p["w_up_t"] = p["w_up"].T
    p["w_down"] = full["ffn_w_down"][li]
    p["w_down_t"] = p["w_down"].T
    for n in ("mix_norm", "sb_out_norm", "ssm_conv_b", "ssm_out_norm", "mla_q_norm", "mla_kv_norm", "mla_out_norm",
              "ffn_norm", "ffn_conv_b"):
        p[n] = _row(small[n][li])
    for n in ("ssm_dt_bias", "ssm_a_log", "ssm_d"):
        p[n] = _pad_row(small[n][li])
    p["ssm_conv_w"] = small["ssm_conv_w"][li]
    p["ffn_conv_w"] = small["ffn_conv_w"][li]
    return p


def _layer_fwd(h, p, cos, sin, li):
    T = h.shape[0]
    nm = lambda s: "l%d_%s" % (li, s)
    s = {"h": h}
    (n1,) = rowwise(rms_fn, [h], [p["mix_norm"]], [(D_MODEL, BF16)], name=nm("mix_norm"))
    proj = matmul(n1, p["w_in_p"], name=nm("in_proj"))
    s["n1"], s["proj"] = n1, proj
    qkv = proj[:, OFF_SB:OFF_SB + 768]
    s["sb_q"] = _to_heads((qkv[:, 0:256] * (SB_DIM ** -0.5)).astype(BF16), SB_HEADS)
    s["sb_k"] = _to_heads(qkv[:, 256:512].astype(BF16), SB_HEADS)
    s["sb_v"] = _to_heads(qkv[:, 512:768].astype(BF16), SB_HEADS)
    y_sb_hm, s["sb_bt"] = sb_fwd(s["sb_q"], s["sb_k"], s["sb_v"], name=nm("sb_fwd"))
    s["y_sb"] = _from_heads(y_sb_hm)
    xbc = ssm_conv_act(proj, p["ssm_conv_w"], p["ssm_conv_b"], name=nm("ssm_conv"))
    s["x_hm"] = _to_heads(xbc[:, :SSM_INNER], SSM_HEADS)
    s["b_hm"] = _to_heads(xbc[:, SSM_INNER:SSM_INNER + 128], SSM_GROUPS)
    s["c_hm"] = _to_heads(xbc[:, SSM_INNER + 128:], SSM_GROUPS)
    y_ssm_hm, s["states"] = ssd_fwd(s["x_hm"], s["b_hm"], s["c_hm"], proj, p["ssm_dt_bias"], p["ssm_a_log"],
                                    p["ssm_d"], name=nm("ssd_fwd"))
    s["y_ssm"] = _from_heads(y_ssm_hm)
    rows = [(proj, 256, OFF_CQ // 256), (proj, 128, OFF_CKV // 128), (proj, 128, OFF_KRA // 128),
            (proj, 128, OFF_KRB // 128), cos, sin]
    qp, kp, vv = rowwise(mla_prep_fn, rows, [p["mla_q_norm"], p["mla_kv_norm"], p["wqp"], p["wqr"], p["wkp"], p["wvp"]],
                         [(512, BF16), (512, BF16), (256, BF16)], name=nm("mla_prep"))
    s["mla_q"], s["mla_k"], s["mla_v"] = _to_heads(qp, MLA_HEADS), _to_heads(kp, MLA_HEADS), _to_heads(vv, MLA_HEADS)
    s["mla_o"], s["mla_lse"] = mla_fwd(s["mla_q"], s["mla_k"], s["mla_v"], name=nm("mla_fwd"))
    s["y_mla"] = _from_heads(s["mla_o"])
    (cat,) = rowwise(merge_fn, [s["y_sb"], s["y_ssm"], (proj, 512, OFF_Z // 512), s["y_mla"]],
                     [p["sb_out_norm"], p["ssm_out_norm"], p["mla_out_norm"]], [(D_MODEL, BF16)], name=nm("merge"),
                     post=lambda a, b, c: (jnp.concatenate([a, b, c], axis=1),))
    s["cat"] = cat
    h1 = matmul(cat, p["w_out"], name=nm("out_proj"), residual=h)
    s["h1"] = h1
    (n2,) = rowwise(rms_fn, [h1], [p["ffn_norm"]], [(D_MODEL, BF16)], name=nm("ffn_norm"))
    up = matmul(n2, p["w_up"], name=nm("ffn_up"))
    act = ffn_act(up, p["ffn_conv_w"], p["ffn_conv_b"], name=nm("ffn_act"))
    s["n2"], s["up"], s["act"] = n2, up, act
    h2 = matmul(act, p["w_down"], name=nm("ffn_down"), residual=h1)
    return h2, s


def _layer_bwd(dh2, s, p, cos, sin, li):
    nm = lambda t: "l%d_%s" % (li, t)
    g = {}
    proj = s["proj"]
    g["ffn_w_down"] = matmul(s["act"], dh2, name=nm("g_w_down"), ta=True)
    d_act = matmul(dh2, p["w_down_t"], name=nm("d_act"), out_dtype=BF16)
    du = ffn_act_bwd_a(s["up"], p["ffn_conv_w"], p["ffn_conv_b"], d_act, name=nm("ffn_act_bwd"))
    d_up, g["ffn_conv_w"], gcb = conv_bwd_b(du, s["up"], 0, p["ffn_conv_w"], name=nm("ffn_conv_bwd"), out_dtype=BF16,
                                            tc=1408)
    g["ffn_conv_b"] = gcb[0]
    g["ffn_w_up"] = matmul(s["n2"], d_up, name=nm("g_w_up"), ta=True)
    d_n2 = matmul(d_up, p["w_up_t"], name=nm("d_n2"))
    (dh1,), (gn,) = rowwise_bwd(rms_fn, [s["h1"]], [], [p["ffn_norm"]], [d_n2], [F32], name=nm("ffn_norm_bwd"),
                                add0=dh2)
    g["ffn_norm"] = gn[0]
    g["w_out"] = matmul(s["cat"], dh1, name=nm("g_w_out"), ta=True)
    d_cat = matmul(dh1, p["w_out_t"], name=nm("d_cat"))
    (d_ysb, d_yssm, d_z, d_ymla), (g1, g2, g3) = rowwise_bwd(
        merge_fn, [s["y_sb"], s["y_ssm"], (proj, 512, OFF_Z // 512), s["y_mla"]], [],
        [p["sb_out_norm"], p["ssm_out_norm"], p["mla_out_norm"]], [d_cat], [F32, F32, BF16, F32], name=nm("merge_bwd"),
        pre_ct=lambda d: (d[:, 0:256], d[:, 256:768], d[:, 768:1024]))
    g["sb_out_norm"], g["ssm_out_norm"], g["mla_out_norm"] = g1[0], g2[0], g3[0]
    dq, dk, dv = sb_bwd(s["sb_q"], s["sb_k"], s["sb_v"], _to_heads(d_ysb, SB_HEADS), s["sb_bt"], name=nm("sb_bwd"),
                        q_scale=SB_DIM ** -0.5)
    d_sb = jnp.concatenate([_from_heads(dq), _from_heads(dk), _from_heads(dv)], axis=1).astype(BF16)
    dqp, dkp, dvv = mla_bwd(s["mla_q"], s["mla_k"], s["mla_v"], _to_heads(d_ymla, MLA_HEADS), s["mla_o"], s["mla_lse"],
                            name=nm("mla_bwd"))
    rows = [(proj, 256, OFF_CQ // 256), (proj, 128, OFF_CKV // 128), (proj, 128, OFF_KRA // 128),
            (proj, 128, OFF_KRB // 128)]
    (d_cq, d_ckv, d_kra, d_krb), (gqn, gkvn, gwqp, gwqr, gwkp, gwvp) = rowwise_bwd(
        mla_prep_fn, rows, [cos, sin], [p["mla_q_norm"], p["mla_kv_norm"], p["wqp"], p["wqr"], p["wkp"], p["wvp"]],
        [_from_heads(dqp), _from_heads(dkp), _from_heads(dvv)], [BF16] * 4, name=nm("mla_prep_bwd"), tile=256)
    g["mla_q_norm"], g["mla_kv_norm"] = gqn[0], gkvn[0]
    g["mla_w_uq"] = _unpack_gw_uq(gwqp, gwqr)
    g["mla_w_ukv"] = _unpack_gw_ukv(gwkp, gwvp)
    dx_hm, db_hm, dc_hm, d_dt, gdb, gal, gds = ssd_bwd(
        s["x_hm"], s["b_hm"], s["c_hm"], proj, s["states"], p["ssm_dt_bias"], p["ssm_a_log"], p["ssm_d"],
        _to_heads(d_yssm, SSM_HEADS), name=nm("ssd_bwd"))
    g["ssm_dt_bias"], g["ssm_a_log"], g["ssm_d"] = gdb[0, :8], gal[0, :8], gds[0, :8]
    d_xbc_act = jnp.concatenate([_from_heads(dx_hm), _from_heads(db_hm), _from_heads(dc_hm)], axis=1)
    d_pre = ssm_conv_bwd_a(proj, p["ssm_conv_w"], p["ssm_conv_b"], d_xbc_act, name=nm("ssm_conv_bwd_a"))
    d_xbc, g["ssm_conv_w"], gscb = conv_bwd_b(d_pre, proj, OFF_XBC, p["ssm_conv_w"], name=nm("ssm_conv_bwd_b"),
                                              out_dtype=BF16, tc=256)
    g["ssm_conv_b"] = gscb[0]
    d_proj = jnp.concatenate([d_sb, d_xbc, d_z, d_cq, d_ckv, d_dt.astype(BF16), d_kra, d_krb], axis=1)
    g["w_in"] = _unpack_gw_in(matmul(s["n1"], d_proj, name=nm("g_w_in"), ta=True))
    d_n1 = matmul(d_proj, p["w_in_pt"], name=nm("d_n1"))
    (dh0,), (gm,) = rowwise_bwd(rms_fn, [s["h"]], [], [p["mix_norm"]], [d_n1], [F32], name=nm("mix_norm_bwd"),
                                add0=dh1)
    g["mix_norm"] = gm[0]
    return dh0, g


def kernel(x, positions, mix_norm, w_in, sb_out_norm, ssm_conv_w, ssm_conv_b, ssm_dt_bias, ssm_a_log, ssm_d, ssm_out_norm, mla_q_norm, mla_w_uq, mla_kv_norm, mla_w_ukv, mla_out_norm, w_out, ffn_norm, ffn_w_up, ffn_conv_w, ffn_conv_b, ffn_w_down, final_norm, loss_target, m_mix_norm, m_w_in, m_sb_out_norm, m_ssm_conv_w, m_ssm_conv_b, m_ssm_dt_bias, m_ssm_a_log, m_ssm_d, m_ssm_out_norm, m_mla_q_norm, m_mla_w_uq, m_mla_kv_norm, m_mla_w_ukv, m_mla_out_norm, m_w_out, m_ffn_norm, m_ffn_w_up, m_ffn_conv_w, m_ffn_conv_b, m_ffn_w_down, m_final_norm, v_mix_norm, v_w_in, v_sb_out_norm, v_ssm_conv_w, v_ssm_conv_b, v_ssm_dt_bias, v_ssm_a_log, v_ssm_d, v_ssm_out_norm, v_mla_q_norm, v_mla_w_uq, v_mla_kv_norm, v_mla_w_ukv, v_mla_out_norm, v_w_out, v_ffn_norm, v_ffn_w_up, v_ffn_conv_w, v_ffn_conv_b, v_ffn_w_down, v_final_norm):
    W = dict(mix_norm=mix_norm, w_in=w_in, sb_out_norm=sb_out_norm, ssm_conv_w=ssm_conv_w, ssm_conv_b=ssm_conv_b,
             ssm_dt_bias=ssm_dt_bias, ssm_a_log=ssm_a_log, ssm_d=ssm_d, ssm_out_norm=ssm_out_norm,
             mla_q_norm=mla_q_norm, mla_w_uq=mla_w_uq, mla_kv_norm=mla_kv_norm, mla_w_ukv=mla_w_ukv,
             mla_out_norm=mla_out_norm, w_out=w_out, ffn_norm=ffn_norm, ffn_w_up=ffn_w_up, ffn_conv_w=ffn_conv_w,
             ffn_conv_b=ffn_conv_b, ffn_w_down=ffn_w_down, final_norm=final_norm)
    M = dict(mix_norm=m_mix_norm, w_in=m_w_in, sb_out_norm=m_sb_out_norm, ssm_conv_w=m_ssm_conv_w,
             ssm_conv_b=m_ssm_conv_b, ssm_dt_bias=m_ssm_dt_bias, ssm_a_log=m_ssm_a_log, ssm_d=m_ssm_d,
             ssm_out_norm=m_ssm_out_norm, mla_q_norm=m_mla_q_norm, mla_w_uq=m_mla_w_uq, mla_kv_norm=m_mla_kv_norm,
             mla_w_ukv=m_mla_w_ukv, mla_out_norm=m_mla_out_norm, w_out=m_w_out, ffn_norm=m_ffn_norm,
             ffn_w_up=m_ffn_w_up, ffn_conv_w=m_ffn_conv_w, ffn_conv_b=m_ffn_conv_b, ffn_w_down=m_ffn_w_down,
             final_norm=m_final_norm)
    V = dict(mix_norm=v_mix_norm, w_in=v_w_in, sb_out_norm=v_sb_out_norm, ssm_conv_w=v_ssm_conv_w,
             ssm_conv_b=v_ssm_conv_b, ssm_dt_bias=v_ssm_dt_bias, ssm_a_log=v_ssm_a_log, ssm_d=v_ssm_d,
             ssm_out_norm=v_ssm_out_norm, mla_q_norm=v_mla_q_norm, mla_w_uq=v_mla_w_uq, mla_kv_norm=v_mla_kv_norm,
             mla_w_ukv=v_mla_w_ukv, mla_out_norm=v_mla_out_norm, w_out=v_w_out, ffn_norm=v_ffn_norm,
             ffn_w_up=v_ffn_w_up, ffn_conv_w=v_ffn_conv_w, ffn_conv_b=v_ffn_conv_b, ffn_w_down=v_ffn_w_down,
             final_norm=v_final_norm)
    depth = mix_norm.shape[0]
    cx, cy, cc = _place()
    chip = 2 * cx + cy
    T = x.shape[1]

    shard_shapes = {n: W[n].shape for n in BIG}
    gathered = allgather_blocks(_flat_half(W, cc, BF16), name="gather_weights")
    full = {}
    per_chip = [_unflat_halves([gathered[2 * k + c].reshape(-1) for c in range(2)], shard_shapes) for k in range(4)]
    for n in BIG:
        full[n] = jnp.concatenate([per_chip[k][n] for k in range(4)], axis=BIG_AXIS[n])
    conv_full = {}
    small = {n: W[n] for n in SMALL_REPL}
    cw_flat = jnp.concatenate([W[n].reshape(-1) for n in SMALL_SHARD])
    cw_rows = -(-cw_flat.shape[0] // (8 * LANES)) * 8
    cw_all = allgather_direct(jnp.pad(cw_flat, (0, cw_rows * LANES - cw_flat.shape[0])).reshape(cw_rows, LANES),
                              name="gather_conv_taps")
    off = 0
    for n in SMALL_SHARD:
        sz = W[n].size
        conv_full[n] = jnp.concatenate(
            [cw_all[2 * k].reshape(-1)[off:off + sz].reshape(W[n].shape) for k in range(4)], axis=2)
        off += sz
    small.update(conv_full)

    cos, sin = _rope_tables(positions[0])
    params = [_layer_weights(full, small, li) for li in range(depth)]

    h = x[0]
    saved = []
    for li in range(depth):
        h, s = _layer_fwd(h, params[li], cos, sin, li)
        saved.append(s)
    dh, g_final, loss_lanes = loss_head(h, loss_target[0], _row(final_norm), name="loss_head")

    grads = [None] * depth
    for li in reversed(range(depth)):
        dh, grads[li] = _layer_bwd(dh, saved[li], params[li], cos, sin, li)
    grad_x = dh[None]
    G = {n: jnp.stack([grads[li][n] for li in range(depth)]) for n in WEIGHTS if n != "final_norm"}
    G["final_norm"] = g_final[0]

    def shard_major(n):
        a = G[n]
        ax = BIG_AXIS[n]
        parts = jnp.split(a, 4, axis=ax)
        return parts

    by_chip = {n: shard_major(n) for n in BIG}

    def flat_for(k, which):
        return _flat_half({n: by_chip[n][k] for n in BIG}, which, BF16)

    mine_first = jnp.stack([flat_for(k, cc) for k in range(4)])
    for_sibling = jnp.stack([flat_for(k, 1 - cc) for k in range(4)])
    R = mine_first.shape[1]
    from_sibling = send_to_sibling(for_sibling.reshape(4 * R, LANES), name="grads_to_sibling")
    pair = add_blocks([mine_first.reshape(4 * R, LANES), from_sibling], BF16, name="grads_pair_sum").reshape(4, R, LANES)
    others = chip_exchange(pair, name="grads_chip_exchange")
    own = lax.dynamic_index_in_dim(pair, chip, 0, keepdims=False)
    half = add_blocks([own, others[0], others[1], others[2]], F32, name="grads_chip_sum")
    both = pair_gather(half, name="grads_pair_gather")
    g_big = _unflat_halves([both[c].reshape(-1) for c in range(2)], shard_shapes)

    small_list = [G[n].reshape(-1) for n in SMALL_REPL] + [G[n].reshape(-1) for n in SMALL_SHARD]
    small_list.append(jnp.sum(loss_lanes).reshape(1))
    sm = jnp.concatenate(small_list)
    n_small = sm.shape[0]
    sm_rows = -(-n_small // (16 * LANES)) * 16
    sm_all = allgather_direct(jnp.pad(sm, (0, sm_rows * LANES - n_small)).reshape(sm_rows, LANES), name="gather_small")
    sm_sum = add_blocks([sm_all[d] for d in range(8)], F32, name="small_sum").reshape(-1)
    g_small, off = {}, 0
    for n in SMALL_REPL:
        g_small[n] = sm_sum[off:off + W[n].size].reshape(W[n].shape)
        off += W[n].size
    for n in SMALL_SHARD:
        full_shape = conv_full[n].shape
        sz = conv_full[n].size
        gfull = sm_sum[off:off + sz].reshape(full_shape)
        width = W[n].shape[2]
        g_small[n] = lax.dynamic_slice_in_dim(gfull, chip * width, width, axis=2)
        off += sz
    loss = sm_sum[off]

    grad_out, delta, new_m, new_v = {}, {}, {}, {}
    for n in BIG:
        shp = W[n].shape
        two_d = lambda a: a.reshape(shp[0] * shp[1], shp[2])
        d, nm_, nv_ = adamw(two_d(W[n]), two_d(g_big[n]), two_d(M[n]), two_d(V[n]), name="adamw_" + n)
        grad_out[n], delta[n], new_m[n], new_v[n] = g_big[n], d.reshape(shp), nm_.reshape(shp), nv_.reshape(shp)
    small_names = SMALL_REPL + SMALL_SHARD

    def flat_small(d):
        f = jnp.concatenate([d[n].reshape(-1) for n in small_names])
        rows = -(-f.shape[0] // (8 * LANES)) * 8
        return jnp.pad(f, (0, rows * LANES - f.shape[0])).reshape(rows, LANES)

    vpad = flat_small(V)
    d, nm_, nv_ = adamw(flat_small(W), flat_small(g_small), flat_small(M), vpad, name="adamw_small")
    off = 0
    for n in small_names:
        sz = W[n].size
        grad_out[n] = g_small[n]
        delta[n] = d.reshape(-1)[off:off + sz].reshape(W[n].shape)
        new_m[n] = nm_.reshape(-1)[off:off + sz].reshape(W[n].shape)
        new_v[n] = nv_.reshape(-1)[off:off + sz].reshape(W[n].shape)
        off += sz

    return (loss, grad_x, *[grad_out[n] for n in WEIGHTS], *[delta[n] for n in WEIGHTS],
            *[new_m[n] for n in WEIGHTS], *[new_v[n] for n in WEIGHTS])
```

```python
import functools
import math

import jax
import jax.numpy as jnp
from jax import lax
from jax.experimental import pallas as pl
from jax.experimental.pallas import tpu as pltpu

F32 = jnp.float32
BF16 = jnp.bfloat16

EPS = 1e-6
D_MODEL = 1024
SB_HEADS, SB_DIM = 4, 64
SSM_HEADS, SSM_DIM, SSM_GROUPS, SSM_STATE, SSM_CHUNK = 8, 64, 2, 64, 128
SSM_INNER = SSM_HEADS * SSM_DIM
SSM_CONV_DIM = SSM_INNER + 2 * SSM_GROUPS * SSM_STATE
MLA_HEADS, MLA_NOPE, MLA_ROPE, MLA_V = 4, 64, 32, 64
MLA_QK = MLA_NOPE + MLA_ROPE
MLA_SCALE = MLA_QK ** -0.5
ROPE_THETA = 10000.0
D_FF = 2816
IN_SPLITS = (768, 512, 768, 8, 256, 128, 32)

OFF_SB, OFF_XBC, OFF_Z, OFF_CQ, OFF_CKV, OFF_DT, OFF_KRA, OFF_KRB = 0, 768, 1536, 2048, 2304, 2432, 2560, 2688
D_IN_P = 2816
LANES = 128

ADAM_LR, ADAM_B1, ADAM_B2, ADAM_EPS, ADAM_WD, ADAM_STEP = 0.001, 0.9, 0.999, 1e-08, 0.01, 10

V7X_VMEM_LIMIT = 48 * 1024 * 1024

NT = (((1,), (1,)), ((), ()))
TN = (((0,), (0,)), ((), ()))

BIG = ("w_in", "mla_w_uq", "mla_w_ukv", "w_out", "ffn_w_up", "ffn_w_down")
BIG_AXIS = {"w_in": 2, "mla_w_uq": 2, "mla_w_ukv": 2, "w_out": 1, "ffn_w_up": 2, "ffn_w_down": 1}
SMALL_REPL = ("mix_norm", "sb_out_norm", "ssm_conv_b", "ssm_dt_bias", "ssm_a_log", "ssm_d", "ssm_out_norm",
              "mla_q_norm", "mla_kv_norm", "mla_out_norm", "ffn_norm", "ffn_conv_b", "final_norm")
SMALL_SHARD = ("ssm_conv_w", "ffn_conv_w")
WEIGHTS = ("mix_norm", "w_in", "sb_out_norm", "ssm_conv_w", "ssm_conv_b", "ssm_dt_bias", "ssm_a_log", "ssm_d",
           "ssm_out_norm", "mla_q_norm", "mla_w_uq", "mla_kv_norm", "mla_w_ukv", "mla_out_norm", "w_out", "ffn_norm",
           "ffn_w_up", "ffn_conv_w", "ffn_conv_b", "ffn_w_down", "final_norm")


def _cparams(*sem):
    return pltpu.CompilerParams(dimension_semantics=sem if sem else None, vmem_limit_bytes=V7X_VMEM_LIMIT)


def _pick(n, target, mult=LANES):
    best = None
    for d in range(mult, min(n, target) + 1, mult):
        if n % d == 0:
            best = d
    return best or n


def _sigmoid(x):
    return 1.0 / (1.0 + jnp.exp(-x))


def _softplus(x):
    ax = jnp.where(x > 0, x, -x)
    return jnp.where(x > 0, x, 0.0) + jnp.log(1.0 + jnp.exp(-ax))


def _rms(x, g):
    return x * lax.rsqrt(jnp.mean(x * x, axis=-1, keepdims=True) + EPS) * g


def _raw_nn(a, b):
    return jnp.dot(a.astype(BF16), b.astype(BF16), preferred_element_type=F32)


def _raw_nt(a, b):
    return lax.dot_general(a.astype(BF16), b.astype(BF16), NT, preferred_element_type=F32)


def _raw_tn(a, b):
    return lax.dot_general(a.astype(BF16), b.astype(BF16), TN, preferred_element_type=F32)


@jax.custom_vjp
def mm_nn(a, b):
    return _raw_nn(a, b)


mm_nn.defvjp(lambda a, b: (_raw_nn(a, b), (a, b)),
             lambda r, ct: (_raw_nt(ct, r[1]), _raw_tn(r[0], ct)))


@jax.custom_vjp
def mm_nt(a, b):
    return _raw_nt(a, b)


mm_nt.defvjp(lambda a, b: (_raw_nt(a, b), (a, b)),
             lambda r, ct: (_raw_nn(ct, r[1]), _raw_tn(ct, r[0])))


@jax.custom_vjp
def mm_tn(a, b):
    return _raw_tn(a, b)


mm_tn.defvjp(lambda a, b: (_raw_tn(a, b), (a, b)),
             lambda r, ct: (_raw_nt(r[1], ct), _raw_nn(r[0], ct)))


def _split_dot(x, m, terms):
    acc = None
    r = x
    for t in range(terms):
        xt = r.astype(BF16)
        d = jnp.dot(xt, m, preferred_element_type=F32)
        acc = d if acc is None else acc + d
        if t + 1 < terms:
            r = r - xt.astype(F32)
    return acc


def _tri_dot(tri, x, terms=3):
    acc = None
    r = x
    for t in range(terms):
        xt = r.astype(BF16)
        d = jnp.dot(tri, xt, preferred_element_type=F32)
        acc = d if acc is None else acc + d
        if t + 1 < terms:
            r = r - xt.astype(F32)
    return acc


def _tri(n, cmp):
    r = lax.broadcasted_iota(jnp.int32, (n, n), 0)
    c = lax.broadcasted_iota(jnp.int32, (n, n), 1)
    return cmp(r, c).astype(BF16)


@jax.custom_vjp
def csum_rows(x):
    return _tri_dot(_tri(x.shape[0], lambda r, c: r >= c), x)


csum_rows.defvjp(lambda x: (csum_rows(x), None),
                 lambda _, ct: (_tri_dot(_tri(ct.shape[0], lambda r, c: r <= c), ct),))


def matmul(a, b, *, name, out_dtype=F32, ta=False, residual=None):
    if ta:
        K, M = a.shape
    else:
        M, K = a.shape
    N = b.shape[1]
    tm = _pick(M, 512)
    tn = _pick(N, 512)
    tk = _pick(K, 1408)
    nk = K // tk
    has_res = residual is not None

    def body(*refs):
        if has_res:
            a_ref, b_ref, r_ref, o_ref, acc = refs
        else:
            a_ref, b_ref, o_ref, acc = refs
        k = pl.program_id(2)

        @pl.when(k == 0)
        def _():
            acc[...] = jnp.zeros_like(acc)

        av = a_ref[...].astype(BF16)
        bv = b_ref[...].astype(BF16)
        if ta:
            acc[...] += lax.dot_general(av, bv, TN, preferred_element_type=F32)
        else:
            acc[...] += jnp.dot(av, bv, preferred_element_type=F32)

        @pl.when(k == nk - 1)
        def _():
            r = acc[...]
            if has_res:
                r = r + r_ref[...].astype(F32)
            o_ref[...] = r.astype(o_ref.dtype)

    a_spec = pl.BlockSpec((tk, tm), lambda i, j, k: (k, i)) if ta else pl.BlockSpec((tm, tk), lambda i, j, k: (i, k))
    in_specs = [a_spec, pl.BlockSpec((tk, tn), lambda i, j, k: (k, j))]
    args = [a, b]
    if has_res:
        in_specs.append(pl.BlockSpec((tm, tn), lambda i, j, k: (i, j)))
        args.append(residual)
    return pl.pallas_call(
        body, name=name, grid=(M // tm, N // tn, nk),
        in_specs=in_specs, out_specs=pl.BlockSpec((tm, tn), lambda i, j, k: (i, j)),
        out_shape=jax.ShapeDtypeStruct((M, N), out_dtype),
        scratch_shapes=[pltpu.VMEM((tm, tn), F32)],
        compiler_params=_cparams("parallel", "parallel", "arbitrary"),
    )(*args)


def _row_spec(entry, tl):
    if isinstance(entry, tuple):
        arr, width, cb = entry
        return arr, pl.BlockSpec((tl, width), lambda i, cb=cb: (i, cb))
    return entry, pl.BlockSpec((tl, entry.shape[1]), lambda i: (i, 0))


def _rows_T(entry):
    return (entry[0] if isinstance(entry, tuple) else entry).shape[0]


def rowwise(fn, rows, params, outs, *, name, tile=512, post=None):
    T = _rows_T(rows[0])
    tl = min(T, tile)
    nr, npar = len(rows), len(params)

    def body(*refs):
        r = [ref[...].astype(F32) for ref in refs[:nr]]
        p = [ref[...].astype(F32) for ref in refs[nr:nr + npar]]
        res = fn(*r, *p)
        if post is not None:
            res = post(*res)
        for o_ref, val in zip(refs[nr + npar:], res):
            o_ref[...] = val.astype(o_ref.dtype)

    arrs, specs = [], []
    for e in rows:
        a, s = _row_spec(e, tl)
        arrs.append(a)
        specs.append(s)
    for p in params:
        arrs.append(p)
        specs.append(pl.BlockSpec(p.shape, lambda i: (0, 0)))
    res = pl.pallas_call(
        body, name=name, grid=(T // tl,), in_specs=specs,
        out_specs=[pl.BlockSpec((tl, c), lambda i: (i, 0)) for c, _ in outs],
        out_shape=[jax.ShapeDtypeStruct((T, c), dt) for c, dt in outs],
        compiler_params=_cparams("parallel"),
    )(*arrs)
    return res


def rowwise_bwd(fn, rows, nd_rows, params, cts, grad_dtypes, *, name, tile=512, pre_ct=None, add0=None):
    T = _rows_T(rows[0])
    tl = min(T, tile)
    nr, nn, npar, nc = len(rows), len(nd_rows), len(params), len(cts)
    has_add = add0 is not None

    def body(*refs):
        pos = 0
        r = [ref[...].astype(F32) for ref in refs[pos:pos + nr]]
        pos += nr
        nd = [ref[...].astype(F32) for ref in refs[pos:pos + nn]]
        pos += nn
        p = [ref[...].astype(F32) for ref in refs[pos:pos + npar]]
        pos += npar
        c = [ref[...].astype(F32) for ref in refs[pos:pos + nc]]
        pos += nc
        if has_add:
            addv = refs[pos][...].astype(F32)
            pos += 1
        rg_refs = refs[pos:pos + nr]
        pg_refs = refs[pos + nr:pos + nr + npar]
        if pre_ct is not None:
            c = list(pre_ct(*c))
        _, vjp = jax.vjp(lambda *a: fn(*a[:nr], *nd, *a[nr:]), *r, *p)
        g = vjp(tuple(c))
        for j, ref in enumerate(rg_refs):
            val = g[j]
            if has_add and j == 0:
                val = val + addv
            ref[...] = val.astype(ref.dtype)
        if npar:
            @pl.when(pl.program_id(0) == 0)
            def _():
                for ref in pg_refs:
                    ref[...] = jnp.zeros_like(ref)
            for j, ref in enumerate(pg_refs):
                ref[...] += g[nr + j]

    arrs, specs = [], []
    widths = []
    for e in list(rows) + list(nd_rows):
        a, s = _row_spec(e, tl)
        arrs.append(a)
        specs.append(s)
        widths.append(s.block_shape[1])
    for p in params:
        arrs.append(p)
        specs.append(pl.BlockSpec(p.shape, lambda i: (0, 0)))
    for e in cts:
        a, s = _row_spec(e, tl)
        arrs.append(a)
        specs.append(s)
    if has_add:
        a, s = _row_spec(add0, tl)
        arrs.append(a)
        specs.append(s)
    out_specs = [pl.BlockSpec((tl, widths[j]), lambda i: (i, 0)) for j in range(nr)]
    out_shape = [jax.ShapeDtypeStruct((T, widths[j]), grad_dtypes[j]) for j in range(nr)]
    out_specs += [pl.BlockSpec(p.shape, lambda i: (0, 0)) for p in params]
    out_shape += [jax.ShapeDtypeStruct(p.shape, F32) for p in params]
    res = pl.pallas_call(
        body, name=name, grid=(T // tl,), in_specs=specs, out_specs=out_specs, out_shape=out_shape,
        compiler_params=_cparams("arbitrary"),
    )(*arrs)
    return list(res[:nr]), list(res[nr:])


def rms_fn(h, g):
    return (_rms(h, g),)


def merge_fn(ysb, yssm, z, ymla, g_sb, g_ssm, g_mla):
    ya = _rms(ysb, g_sb)
    yb = _rms(yssm * (z * _sigmoid(z)), g_ssm)
    yc = _rms(ymla, g_mla)
    return ya, yb, yc


def mla_prep_fn(cq, ckv, kra, krb, cos, sin, qn, kvn, wqp, wqr, wkp, wvp):
    cos4 = jnp.concatenate([cos] * MLA_HEADS, axis=1)
    sin4 = jnp.concatenate([sin] * MLA_HEADS, axis=1)
    nq = _rms(cq, qn)
    q = (mm_nn(nq, wqp) * cos4 + mm_nn(nq, wqr) * sin4) * MLA_SCALE
    nkv = _rms(ckv, kvn)
    kpe = kra * cos + krb * sin
    k = mm_nn(nkv, wkp) + jnp.concatenate([kpe] * MLA_HEADS, axis=1)
    v = mm_nn(nkv, wvp)
    return q, k, v


HALO = 8


def _prev_halo_spec(tl, tc, col_of):
    return pl.BlockSpec((HALO, tc), lambda i, j: (jnp.maximum(i * (tl // HALO) - 1, 0), col_of(j)))


def _fill_prev(buf, x_ref, halo_ref, i):
    buf[0:HALO, :] = jnp.where(i > 0, halo_ref[...].astype(F32), 0.0)
    buf[HALO:, :] = x_ref[...].astype(F32)


def _conv_from(buf, w_ref, b_ref, K, tl):
    acc = b_ref[...].astype(F32) + jnp.zeros((tl, buf.shape[1]), F32)
    for k in range(K):
        acc = acc + buf[pl.ds(HALO - (K - 1 - k), tl), :] * w_ref[k:k + 1, :].astype(F32)
    return acc


def ssm_conv_act(proj, w, b, *, name, tile=512, tc=256):
    T = proj.shape[0]
    K, C = w.shape
    tl = min(T, tile)
    c0 = OFF_XBC // tc

    def body(x_ref, halo_ref, w_ref, b_ref, o_ref, buf):
        _fill_prev(buf, x_ref, halo_ref, pl.program_id(0))
        u = _conv_from(buf, w_ref, b_ref, K, tl)
        o_ref[...] = u * _sigmoid(u)

    return pl.pallas_call(
        body, name=name, grid=(T // tl, C // tc),
        in_specs=[pl.BlockSpec((tl, tc), lambda i, j: (i, c0 + j)), _prev_halo_spec(tl, tc, lambda j: c0 + j),
                  pl.BlockSpec((K, tc), lambda i, j: (0, j)), pl.BlockSpec((1, tc), lambda i, j: (0, j))],
        out_specs=pl.BlockSpec((tl, tc), lambda i, j: (i, j)),
        out_shape=jax.ShapeDtypeStruct((T, C), F32),
        scratch_shapes=[pltpu.VMEM((tl + HALO, tc), F32)],
        compiler_params=_cparams("parallel", "parallel"),
    )(proj, proj, w, b)


def ssm_conv_bwd_a(proj, w, b, d_out, *, name, tile=512, tc=256):
    T = proj.shape[0]
    K, C = w.shape
    tl = min(T, tile)
    c0 = OFF_XBC // tc

    def body(x_ref, halo_ref, w_ref, b_ref, d_ref, o_ref, buf):
        _fill_prev(buf, x_ref, halo_ref, pl.program_id(0))
        u = _conv_from(buf, w_ref, b_ref, K, tl)
        s = _sigmoid(u)
        o_ref[...] = d_ref[...].astype(F32) * (s * (1.0 + u * (1.0 - s)))

    return pl.pallas_call(
        body, name=name, grid=(T // tl, C // tc),
        in_specs=[pl.BlockSpec((tl, tc), lambda i, j: (i, c0 + j)), _prev_halo_spec(tl, tc, lambda j: c0 + j),
                  pl.BlockSpec((K, tc), lambda i, j: (0, j)), pl.BlockSpec((1, tc), lambda i, j: (0, j)),
                  pl.BlockSpec((tl, tc), lambda i, j: (i, j))],
        out_specs=pl.BlockSpec((tl, tc), lambda i, j: (i, j)),
        out_shape=jax.ShapeDtypeStruct((T, C), F32),
        scratch_shapes=[pltpu.VMEM((tl + HALO, tc), F32)],
        compiler_params=_cparams("parallel", "parallel"),
    )(proj, proj, w, b, d_out)


def ffn_act(up, w, b, *, name, tile=512, tc=1408):
    T = up.shape[0]
    K = w.shape[0]
    tl = min(T, tile)
    nj = D_FF // tc

    def body(xg_ref, hg_ref, xv_ref, hv_ref, wg_ref, wv_ref, bg_ref, bv_ref, o_ref, bufg, bufv):
        i = pl.program_id(0)
        _fill_prev(bufg, xg_ref, hg_ref, i)
        _fill_prev(bufv, xv_ref, hv_ref, i)
        gate = _conv_from(bufg, wg_ref, bg_ref, K, tl)
        val = _conv_from(bufv, wv_ref, bv_ref, K, tl)
        o_ref[...] = (gate * _sigmoid(gate) * val).astype(o_ref.dtype)

    return pl.pallas_call(
        body, name=name, grid=(T // tl, nj),
        in_specs=[pl.BlockSpec((tl, tc), lambda i, j: (i, j)), _prev_halo_spec(tl, tc, lambda j: j),
                  pl.BlockSpec((tl, tc), lambda i, j: (i, nj + j)), _prev_halo_spec(tl, tc, lambda j: nj + j),
                  pl.BlockSpec((K, tc), lambda i, j: (0, j)), pl.BlockSpec((K, tc), lambda i, j: (0, nj + j)),
                  pl.BlockSpec((1, tc), lambda i, j: (0, j)), pl.BlockSpec((1, tc), lambda i, j: (0, nj + j))],
        out_specs=pl.BlockSpec((tl, tc), lambda i, j: (i, j)),
        out_shape=jax.ShapeDtypeStruct((T, D_FF), BF16),
        scratch_shapes=[pltpu.VMEM((tl + HALO, tc), F32), pltpu.VMEM((tl + HALO, tc), F32)],
        compiler_params=_cparams("parallel", "parallel"),
    )(up, up, up, up, w, w, b, b)


def ffn_act_bwd_a(up, w, b, d_act, *, name, tile=512, tc=1408):
    T = up.shape[0]
    K = w.shape[0]
    tl = min(T, tile)
    nj = D_FF // tc

    def body(xs_ref, hs_ref, xp_ref, hp_ref, ws_ref, wp_ref, bs_ref, bp_ref, d_ref, o_ref, bufs, bufp):
        i = pl.program_id(0)
        is_gate = pl.program_id(1) < nj
        _fill_prev(bufs, xs_ref, hs_ref, i)
        _fill_prev(bufp, xp_ref, hp_ref, i)
        us = _conv_from(bufs, ws_ref, bs_ref, K, tl)
        up_ = _conv_from(bufp, wp_ref, bp_ref, K, tl)
        gate = jnp.where(is_gate, us, up_)
        val = jnp.where(is_gate, up_, us)
        d = d_ref[...].astype(F32)
        s = _sigmoid(gate)
        o_ref[...] = jnp.where(is_gate, d * val * (s * (1.0 + gate * (1.0 - s))), d * (gate * s))

    par = lambda j: (j + nj) % (2 * nj)
    return pl.pallas_call(
        body, name=name, grid=(T // tl, 2 * nj),
        in_specs=[pl.BlockSpec((tl, tc), lambda i, j: (i, j)), _prev_halo_spec(tl, tc, lambda j: j),
                  pl.BlockSpec((tl, tc), lambda i, j: (i, par(j))), _prev_halo_spec(tl, tc, par),
                  pl.BlockSpec((K, tc), lambda i, j: (0, j)), pl.BlockSpec((K, tc), lambda i, j: (0, par(j))),
                  pl.BlockSpec((1, tc), lambda i, j: (0, j)), pl.BlockSpec((1, tc), lambda i, j: (0, par(j))),
                  pl.BlockSpec((tl, tc), lambda i, j: (i, j % nj))],
        out_specs=pl.BlockSpec((tl, tc), lambda i, j: (i, j)),
        out_shape=jax.ShapeDtypeStruct((T, 2 * D_FF), F32),
        scratch_shapes=[pltpu.VMEM((tl + HALO, tc), F32), pltpu.VMEM((tl + HALO, tc), F32)],
        compiler_params=_cparams("parallel", "parallel"),
    )(up, up, up, up, w, w, b, b, d_act)


def conv_bwd_b(du, x, x_off, w, *, name, out_dtype, tile=512, tc=256):
    T, C = du.shape
    K = w.shape[0]
    tl = min(T, tile)
    c0 = x_off // tc
    nblk = T // HALO

    def body(du_ref, nx_ref, x_ref, halo_ref, w_ref, dx_ref, dw_ref, db_ref, dbuf, xbuf):
        i = pl.program_id(1)
        last = pl.num_programs(1) - 1
        d = du_ref[...].astype(F32)
        dbuf[0:tl, :] = d
        dbuf[tl:, :] = jnp.where(i < last, nx_ref[...].astype(F32), 0.0)
        _fill_prev(xbuf, x_ref, halo_ref, i)

        @pl.when(i == 0)
        def _():
            dw_ref[...] = jnp.zeros_like(dw_ref)
            db_ref[...] = jnp.zeros_like(db_ref)

        dx = jnp.zeros((tl, tc), F32)
        for k in range(K):
            s = K - 1 - k
            dx = dx + dbuf[pl.ds(s, tl), :] * w_ref[k:k + 1, :].astype(F32)
            dw_ref[k:k + 1, :] += jnp.sum(d * xbuf[pl.ds(HALO - s, tl), :], axis=0, keepdims=True)
        db_ref[...] += jnp.sum(d, axis=0, keepdims=True)
        dx_ref[...] = dx.astype(dx_ref.dtype)

    return pl.pallas_call(
        body, name=name, grid=(C // tc, T // tl),
        in_specs=[pl.BlockSpec((tl, tc), lambda j, i: (i, j)),
                  pl.BlockSpec((HALO, tc), lambda j, i: (jnp.minimum((i + 1) * (tl // HALO), nblk - 1), j)),
                  pl.BlockSpec((tl, tc), lambda j, i: (i, c0 + j)),
                  pl.BlockSpec((HALO, tc), lambda j, i: (jnp.maximum(i * (tl // HALO) - 1, 0), c0 + j)),
                  pl.BlockSpec((K, tc), lambda j, i: (0, j))],
        out_specs=[pl.BlockSpec((tl, tc), lambda j, i: (i, j)), pl.BlockSpec((K, tc), lambda j, i: (0, j)),
                   pl.BlockSpec((1, tc), lambda j, i: (0, j))],
        out_shape=[jax.ShapeDtypeStruct((T, C), out_dtype), jax.ShapeDtypeStruct((K, C), F32),
                   jax.ShapeDtypeStruct((1, C), F32)],
        scratch_shapes=[pltpu.VMEM((tl + HALO, tc), F32), pltpu.VMEM((tl + HALO, tc), F32)],
        compiler_params=_cparams("parallel", "arbitrary"),
    )(du, du, x, x, w)


def _attn_tiles(T):
    return min(T, 1024), min(T, 256)


def _after_diag(keys, queries, strict):
    d = lax.broadcasted_iota(jnp.int32, (keys, queries), 1) - lax.broadcasted_iota(jnp.int32, (keys, queries), 0)
    return d > 0 if strict else d >= 0


def _log_gates(z):
    l1p = jnp.log(1.0 + jnp.exp(-jnp.abs(z)))
    a = jnp.minimum(z, 0.0) - l1p
    return a, a - z


def _causal_sweep(i, tq, tk, block, descending):
    nb = tq // tk
    n_full = i * nb

    def band():
        order = reversed(range(nb)) if descending else range(nb)
        for bb in order:
            block(pl.multiple_of(i * tq + bb * tk, tk), bb * tk, True)

    def full():
        def step(j, c):
            kb = (n_full - 1 - j) if descending else j
            block(pl.multiple_of(kb * tk, tk), 0, False)
            return c
        lax.fori_loop(0, n_full, step, 0)

    if descending:
        band()
        full()
    else:
        full()
        band()


def sb_fwd(q, k, v, *, name):
    H, dh, T = q.shape
    tq, tk = _attn_tiles(T)

    def body(q_ref, k_ref, v_ref, y_ref, bt_ref, acc, run):
        acc[...] = jnp.zeros_like(acc)
        run[...] = jnp.zeros_like(run)
        u_after = _tri(tk, lambda r, c: r < c)

        def block(k0, r0, masked):
            kb = k_ref[pl.ds(k0, tk), :]
            vb = v_ref[pl.ds(k0, tk), :]
            z = jnp.dot(kb, q_ref[:, r0:], preferred_element_type=F32)
            a, b = _log_gates(z)
            if masked:
                valid = _after_diag(tk, tq - r0, True)
                b = jnp.where(valid, b, 0.0)
            w = jnp.exp(a + _tri_dot(u_after, b, 2) + run[:, r0:])
            if masked:
                w = jnp.where(valid, w, 0.0)
            acc[:, r0:] += lax.dot_general(vb, w.astype(BF16), TN, preferred_element_type=F32)
            run[:, r0:] += jnp.sum(b, axis=0, keepdims=True)

        _causal_sweep(pl.program_id(1), tq, tk, block, descending=True)
        y_ref[...] = acc[...]
        bt_ref[...] = run[...]

    return pl.pallas_call(
        body, name=name, grid=(H, T // tq),
        in_specs=[pl.BlockSpec((None, dh, tq), lambda h, i: (h, 0, i)),
                  pl.BlockSpec((None, T, dh), lambda h, i: (h, 0, 0)),
                  pl.BlockSpec((None, T, dh), lambda h, i: (h, 0, 0))],
        out_specs=[pl.BlockSpec((None, dh, tq), lambda h, i: (h, 0, i)),
                   pl.BlockSpec((None, 1, tq), lambda h, i: (h, 0, i))],
        out_shape=[jax.ShapeDtypeStruct((H, dh, T), F32), jax.ShapeDtypeStruct((H, 1, T), F32)],
        scratch_shapes=[pltpu.VMEM((dh, tq), F32), pltpu.VMEM((1, tq), F32)],
        compiler_params=_cparams("parallel", "parallel"),
    )(q, k, v)


def sb_bwd(q, k, v, dy, btot, *, name, q_scale):
    H, dh, T = q.shape
    tq, tk = _attn_tiles(T)

    def body(q_ref, k_ref, v_ref, dy_ref, bt_ref, dq_ref, dk_ref, dv_ref, dq, pb, pg, dyb):
        @pl.when(pl.program_id(1) == 0)
        def _():
            dk_ref[...] = jnp.zeros_like(dk_ref)
            dv_ref[...] = jnp.zeros_like(dv_ref)

        dq[...] = jnp.zeros_like(dq)
        pb[...] = jnp.zeros_like(pb)
        pg[...] = jnp.zeros_like(pg)
        dyb[...] = dy_ref[...].astype(BF16)
        u_upto = _tri(tk, lambda r, c: r >= c)
        u_before = _tri(tk, lambda r, c: r > c)

        def block(k0, r0, masked):
            kb = k_ref[pl.ds(k0, tk), :]
            vb = v_ref[pl.ds(k0, tk), :]
            qv = q_ref[:, r0:]
            dyv = dyb[:, r0:]
            z = jnp.dot(kb, qv, preferred_element_type=F32)
            a, b = _log_gates(z)
            if masked:
                valid = _after_diag(tk, tq - r0, True)
                b = jnp.where(valid, b, 0.0)
            w = jnp.exp(a + (bt_ref[:, r0:] - pb[:, r0:] - _tri_dot(u_upto, b, 2)))
            if masked:
                w = jnp.where(valid, w, 0.0)
            g = w * jnp.dot(vb, dyv, preferred_element_type=F32)
            dz = g - jnp.exp(a) * (g + pg[:, r0:] + _tri_dot(u_before, g, 2))
            if masked:
                dz = jnp.where(valid, dz, 0.0)
            dz = dz.astype(BF16)
            dq[:, r0:] += lax.dot_general(kb, dz, TN, preferred_element_type=F32)
            dk_ref[pl.ds(k0, tk), :] += lax.dot_general(dz, qv, NT, preferred_element_type=F32)
            dv_ref[pl.ds(k0, tk), :] += lax.dot_general(w.astype(BF16), dyv, NT, preferred_element_type=F32)
            pb[:, r0:] += jnp.sum(b, axis=0, keepdims=True)
            pg[:, r0:] += jnp.sum(g, axis=0, keepdims=True)

        _causal_sweep(pl.program_id(1), tq, tk, block, descending=False)
        dq_ref[...] = dq[...] * q_scale

    return pl.pallas_call(
        body, name=name, grid=(H, T // tq),
        in_specs=[pl.BlockSpec((None, dh, tq), lambda h, i: (h, 0, i)),
                  pl.BlockSpec((None, T, dh), lambda h, i: (h, 0, 0)),
                  pl.BlockSpec((None, T, dh), lambda h, i: (h, 0, 0)),
                  pl.BlockSpec((None, dh, tq), lambda h, i: (h, 0, i)),
                  pl.BlockSpec((None, 1, tq), lambda h, i: (h, 0, i))],
        out_specs=[pl.BlockSpec((None, dh, tq), lambda h, i: (h, 0, i)),
                   pl.BlockSpec((None, T, dh), lambda h, i: (h, 0, 0)),
                   pl.BlockSpec((None, T, dh), lambda h, i: (h, 0, 0))],
        out_shape=[jax.ShapeDtypeStruct((H, dh, T), F32), jax.ShapeDtypeStruct((H, T, dh), F32),
                   jax.ShapeDtypeStruct((H, T, dh), F32)],
        scratch_shapes=[pltpu.VMEM((dh, tq), F32), pltpu.VMEM((1, tq), F32), pltpu.VMEM((1, tq), F32),
                        pltpu.VMEM((dh, tq), BF16)],
        compiler_params=_cparams("parallel", "arbitrary"),
    )(q, k, v, dy, btot)


NEG = -1e30


def mla_fwd(q, k, v, *, name):
    H, dk, T = q.shape
    dv = v.shape[2]
    tq, tk = _attn_tiles(T)

    def body(q_ref, k_ref, v_ref, o_ref, l_ref, acc, m_s, l_s):
        acc[...] = jnp.zeros_like(acc)
        m_s[...] = jnp.full_like(m_s, NEG)
        l_s[...] = jnp.zeros_like(l_s)

        def block(k0, r0, masked):
            kb = k_ref[pl.ds(k0, tk), :]
            vb = v_ref[pl.ds(k0, tk), :]
            s = jnp.dot(kb, q_ref[:, r0:], preferred_element_type=F32)
            if masked:
                s = jnp.where(_after_diag(tk, tq - r0, False), s, NEG)
            m = m_s[:, r0:]
            m_new = jnp.maximum(m, jnp.max(s, axis=0, keepdims=True))
            p = jnp.exp(s - m_new)
            alpha = jnp.exp(m - m_new)
            l_s[:, r0:] = alpha * l_s[:, r0:] + jnp.sum(p, axis=0, keepdims=True)
            acc[:, r0:] = alpha * acc[:, r0:] + lax.dot_general(vb, p.astype(BF16), TN, preferred_element_type=F32)
            m_s[:, r0:] = m_new

        _causal_sweep(pl.program_id(1), tq, tk, block, descending=False)
        o_ref[...] = acc[...] / l_s[...]
        l_ref[...] = m_s[...] + jnp.log(l_s[...])

    return pl.pallas_call(
        body, name=name, grid=(H, T // tq),
        in_specs=[pl.BlockSpec((None, dk, tq), lambda h, i: (h, 0, i)),
                  pl.BlockSpec((None, T, dk), lambda h, i: (h, 0, 0)),
                  pl.BlockSpec((None, T, dv), lambda h, i: (h, 0, 0))],
        out_specs=[pl.BlockSpec((None, dv, tq), lambda h, i: (h, 0, i)),
                   pl.BlockSpec((None, 1, tq), lambda h, i: (h, 0, i))],
        out_shape=[jax.ShapeDtypeStruct((H, dv, T), F32), jax.ShapeDtypeStruct((H, 1, T), F32)],
        scratch_shapes=[pltpu.VMEM((dv, tq), F32), pltpu.VMEM((1, tq), F32), pltpu.VMEM((1, tq), F32)],
        compiler_params=_cparams("parallel", "parallel"),
    )(q, k, v)


def mla_bwd(q, k, v, do, o, lse, *, name):
    H, dk, T = q.shape
    dv = v.shape[2]
    tq, tk = _attn_tiles(T)

    def body(q_ref, k_ref, v_ref, do_ref, o_ref, l_ref, dq_ref, dk_ref, dv_ref, dq, delta, dob):
        @pl.when(pl.program_id(1) == 0)
        def _():
            dk_ref[...] = jnp.zeros_like(dk_ref)
            dv_ref[...] = jnp.zeros_like(dv_ref)

        dq[...] = jnp.zeros_like(dq)
        dov = do_ref[...].astype(F32)
        dob[...] = dov.astype(BF16)
        delta[...] = jnp.sum(dov * o_ref[...], axis=0, keepdims=True)

        def block(k0, r0, masked):
            kb = k_ref[pl.ds(k0, tk), :]
            vb = v_ref[pl.ds(k0, tk), :]
            qv = q_ref[:, r0:]
            dov_b = dob[:, r0:]
            s = jnp.dot(kb, qv, preferred_element_type=F32)
            p = jnp.exp(s - l_ref[:, r0:])
            if masked:
                p = jnp.where(_after_diag(tk, tq - r0, False), p, 0.0)
            dp = jnp.dot(vb, dov_b, preferred_element_type=F32)
            ds = (p * (dp - delta[:, r0:])).astype(BF16)
            dq[:, r0:] += lax.dot_general(kb, ds, TN, preferred_element_type=F32)
            dk_ref[pl.ds(k0, tk), :] += lax.dot_general(ds, qv, NT, preferred_element_type=F32)
            dv_ref[pl.ds(k0, tk), :] += lax.dot_general(p.astype(BF16), dov_b, NT, preferred_element_type=F32)

        _causal_sweep(pl.program_id(1), tq, tk, block, descending=False)
        dq_ref[...] = dq[...]

    return pl.pallas_call(
        body, name=name, grid=(H, T // tq),
        in_specs=[pl.BlockSpec((None, dk, tq), lambda h, i: (h, 0, i)),
                  pl.BlockSpec((None, T, dk), lambda h, i: (h, 0, 0)),
                  pl.BlockSpec((None, T, dv), lambda h, i: (h, 0, 0)),
                  pl.BlockSpec((None, dv, tq), lambda h, i: (h, 0, i)),
                  pl.BlockSpec((None, dv, tq), lambda h, i: (h, 0, i)),
                  pl.BlockSpec((None, 1, tq), lambda h, i: (h, 0, i))],
        out_specs=[pl.BlockSpec((None, dk, tq), lambda h, i: (h, 0, i)),
                   pl.BlockSpec((None, T, dk), lambda h, i: (h, 0, 0)),
                   pl.BlockSpec((None, T, dv), lambda h, i: (h, 0, 0))],
        out_shape=[jax.ShapeDtypeStruct((H, dk, T), F32), jax.ShapeDtypeStruct((H, T, dk), F32),
                   jax.ShapeDtypeStruct((H, T, dv), F32)],
        scratch_shapes=[pltpu.VMEM((dk, tq), F32), pltpu.VMEM((1, tq), F32), pltpu.VMEM((dv, tq), BF16)],
        compiler_params=_cparams("parallel", "arbitrary"),
    )(q, k, v, do, o, lse)


def _lane_pick(x, h):
    lane = lax.broadcasted_iota(jnp.int32, (1, x.shape[1]), 1)
    return jnp.sum(jnp.where(lane == h, x, 0.0), axis=1, keepdims=True)


def _row_pick(x, h):
    sub = lax.broadcasted_iota(jnp.int32, (x.shape[0], 1), 0)
    return jnp.sum(jnp.where(sub == h, x, 0.0), axis=0, keepdims=True)


def ssd_chunk_fn(*args):
    nh, ng = SSM_HEADS, SSM_GROUPS
    xs = args[:nh]
    bs = args[nh:nh + ng]
    cs = args[nh + ng:nh + 2 * ng]
    dt_raw = args[nh + 2 * ng]
    st = args[nh + 2 * ng + 1:nh + 2 * ng + 1 + nh]
    dt_bias, a_log, d_skip = args[nh + 2 * ng + 1 + nh:]
    L = dt_raw.shape[0]
    dt = _softplus(dt_raw + dt_bias)
    da = dt * (-jnp.exp(a_log))
    dcs = csum_rows(da)
    dcs_t = dcs.T
    total = jnp.sum(da, axis=0, keepdims=True)
    causal = lax.broadcasted_iota(jnp.int32, (L, L), 0) >= lax.broadcasted_iota(jnp.int32, (L, L), 1)
    cb = [mm_nt(cs[g], bs[g]) for g in range(ng)]
    ys, new_st = [], []
    for h in range(nh):
        g = h // (nh // ng)
        dcs_h = _lane_pick(dcs, h)
        dt_h = _lane_pick(dt, h)
        tot_h = _lane_pick(total, h)
        dsk_h = _lane_pick(d_skip, h)
        decay = jnp.exp(jnp.where(causal, dcs_h - _row_pick(dcs_t, h), NEG))
        xdt = xs[h] * dt_h
        y = mm_nn(cb[g] * decay, xdt)
        y = y + mm_nn(cs[g] * jnp.exp(dcs_h), st[h])
        ys.append(y + xs[h] * dsk_h)
        new_st.append(st[h] * jnp.exp(tot_h) + mm_tn(bs[g] * jnp.exp(tot_h - dcs_h), xdt))
    return tuple(ys) + tuple(new_st)


def ssd_fwd(x_hm, b_hm, c_hm, proj, dt_bias, a_log, d_skip, *, name):
    nh, T, P = x_hm.shape
    ng, N = b_hm.shape[0], b_hm.shape[2]
    L = SSM_CHUNK
    nc = T // L
    dtb = OFF_DT // LANES

    def body(x_ref, b_ref, c_ref, dt_ref, db_ref, al_ref, ds_ref, y_ref, s_ref, state):
        @pl.when(pl.program_id(0) == 0)
        def _():
            state[...] = jnp.zeros_like(state)

        s_ref[...] = state[...]
        args = ([x_ref[h] for h in range(nh)] + [b_ref[g] for g in range(ng)] + [c_ref[g] for g in range(ng)]
                + [dt_ref[...]] + [state[h] for h in range(nh)] + [db_ref[...], al_ref[...], ds_ref[...]])
        res = ssd_chunk_fn(*args)
        for h in range(nh):
            y_ref[h] = res[h]
            state[h] = res[nh + h]

    par = pl.BlockSpec((1, LANES), lambda i: (0, 0))
    return pl.pallas_call(
        body, name=name, grid=(nc,),
        in_specs=[pl.BlockSpec((nh, L, P), lambda i: (0, i, 0)), pl.BlockSpec((ng, L, N), lambda i: (0, i, 0)),
                  pl.BlockSpec((ng, L, N), lambda i: (0, i, 0)), pl.BlockSpec((L, LANES), lambda i: (i, dtb)),
                  par, par, par],
        out_specs=[pl.BlockSpec((nh, L, P), lambda i: (0, i, 0)),
                   pl.BlockSpec((None, nh, N, P), lambda i: (i, 0, 0, 0))],
        out_shape=[jax.ShapeDtypeStruct((nh, T, P), F32), jax.ShapeDtypeStruct((nc, nh, N, P), F32)],
        scratch_shapes=[pltpu.VMEM((nh, N, P), F32)],
        compiler_params=_cparams("arbitrary"),
    )(x_hm, b_hm, c_hm, proj, dt_bias, a_log, d_skip)


def ssd_bwd(x_hm, b_hm, c_hm, proj, states, dt_bias, a_log, d_skip, dy_hm, *, name):
    nh, T, P = x_hm.shape
    ng, N = b_hm.shape[0], b_hm.shape[2]
    L = SSM_CHUNK
    nc = T // L
    dtb = OFF_DT // LANES

    def body(x_ref, b_ref, c_ref, dt_ref, s_ref, db_ref, al_ref, ds_ref, dy_ref,
             dx_ref, dbm_ref, dcm_ref, ddt_ref, gdb_ref, gal_ref, gds_ref, dstate):
        @pl.when(pl.program_id(0) == 0)
        def _():
            dstate[...] = jnp.zeros_like(dstate)
            gdb_ref[...] = jnp.zeros_like(gdb_ref)
            gal_ref[...] = jnp.zeros_like(gal_ref)
            gds_ref[...] = jnp.zeros_like(gds_ref)

        args = ([x_ref[h] for h in range(nh)] + [b_ref[g] for g in range(ng)] + [c_ref[g] for g in range(ng)]
                + [dt_ref[...]] + [s_ref[h] for h in range(nh)] + [db_ref[...], al_ref[...], ds_ref[...]])
        _, vjp = jax.vjp(ssd_chunk_fn, *args)
        g = vjp(tuple([dy_ref[h] for h in range(nh)] + [dstate[h] for h in range(nh)]))
        for h in range(nh):
            dx_ref[h] = g[h]
        for gi in range(ng):
            dbm_ref[gi] = g[nh + gi]
            dcm_ref[gi] = g[nh + ng + gi]
        ddt_ref[...] = g[nh + 2 * ng]
        for h in range(nh):
            dstate[h] = g[nh + 2 * ng + 1 + h]
        gdb_ref[...] += g[-3]
        gal_ref[...] += g[-2]
        gds_ref[...] += g[-1]

    rev = lambda i: nc - 1 - i
    par = pl.BlockSpec((1, LANES), lambda i: (0, 0))
    return pl.pallas_call(
        body, name=name, grid=(nc,),
        in_specs=[pl.BlockSpec((nh, L, P), lambda i: (0, rev(i), 0)), pl.BlockSpec((ng, L, N), lambda i: (0, rev(i), 0)),
                  pl.BlockSpec((ng, L, N), lambda i: (0, rev(i), 0)), pl.BlockSpec((L, LANES), lambda i: (rev(i), dtb)),
                  pl.BlockSpec((None, nh, N, P), lambda i: (rev(i), 0, 0, 0)), par, par, par,
                  pl.BlockSpec((nh, L, P), lambda i: (0, rev(i), 0))],
        out_specs=[pl.BlockSpec((nh, L, P), lambda i: (0, rev(i), 0)), pl.BlockSpec((ng, L, N), lambda i: (0, rev(i), 0)),
                   pl.BlockSpec((ng, L, N), lambda i: (0, rev(i), 0)), pl.BlockSpec((L, LANES), lambda i: (rev(i), 0)),
                   par, par, par],
        out_shape=[jax.ShapeDtypeStruct((nh, T, P), F32), jax.ShapeDtypeStruct((ng, T, N), F32),
                   jax.ShapeDtypeStruct((ng, T, N), F32), jax.ShapeDtypeStruct((T, LANES), F32),
                   jax.ShapeDtypeStruct((1, LANES), F32), jax.ShapeDtypeStruct((1, LANES), F32),
                   jax.ShapeDtypeStruct((1, LANES), F32)],
        scratch_shapes=[pltpu.VMEM((nh, N, P), F32)],
        compiler_params=_cparams("arbitrary"),
    )(x_hm, b_hm, c_hm, proj, states, dt_bias, a_log, d_skip, dy_hm)


def loss_head(h, target, g, *, name, tile=512):
    T, C = h.shape
    tl = min(T, tile)

    def body(h_ref, t_ref, g_ref, dh_ref, dg_ref, ls_ref):
        @pl.when(pl.program_id(0) == 0)
        def _():
            dg_ref[...] = jnp.zeros_like(dg_ref)
            ls_ref[...] = jnp.zeros_like(ls_ref)

        (y,), vjp = jax.vjp(rms_fn, h_ref[...], g_ref[...])
        err = y - t_ref[...]
        ls_ref[...] += jnp.sum(err * err, axis=0, keepdims=True) * (0.5 / C)
        dh, dg = vjp((err * (1.0 / C),))
        dh_ref[...] = dh
        dg_ref[...] += dg

    row = pl.BlockSpec((tl, C), lambda i: (i, 0))
    par = pl.BlockSpec((1, C), lambda i: (0, 0))
    return pl.pallas_call(
        body, name=name, grid=(T // tl,), in_specs=[row, row, par], out_specs=[row, par, par],
        out_shape=[jax.ShapeDtypeStruct((T, C), F32), jax.ShapeDtypeStruct((1, C), F32),
                   jax.ShapeDtypeStruct((1, C), F32)],
        compiler_params=_cparams("arbitrary"),
    )(h, target, g)


def adamw(w, g, m, v, *, name):
    R, C = w.shape
    tr = R
    for d in range(8, min(R, 512) + 1, 8):
        if R % d == 0:
            tr = d
    c1 = 1.0 - ADAM_B1 ** ADAM_STEP
    c2 = 1.0 - ADAM_B2 ** ADAM_STEP

    def body(w_ref, g_ref, m_ref, v_ref, d_ref, nm_ref, nv_ref):
        gv = g_ref[...]
        nm = ADAM_B1 * m_ref[...] + (1.0 - ADAM_B1) * gv
        nv = ADAM_B2 * v_ref[...] + (1.0 - ADAM_B2) * (gv * gv)
        d_ref[...] = -ADAM_LR * ((nm / c1) / (jnp.sqrt(nv / c2) + ADAM_EPS) + ADAM_WD * w_ref[...])
        nm_ref[...] = nm
        nv_ref[...] = nv

    spec = pl.BlockSpec((tr, C), lambda i: (i, 0))
    return pl.pallas_call(
        body, name=name, grid=(R // tr,), in_specs=[spec] * 4, out_specs=[spec] * 3,
        out_shape=[jax.ShapeDtypeStruct((R, C), F32)] * 3,
        compiler_params=_cparams("parallel"),
    )(w, g, m, v)


MESH = pl.DeviceIdType.MESH
HBM_SPEC = pl.BlockSpec(memory_space=pltpu.HBM)


def _place():
    return lax.axis_index("x"), lax.axis_index("y"), lax.axis_index("c")


def allgather_blocks(mine, *, name):
    R = mine.shape[0]

    def body(x_ref, out_ref, send_sems, recv_sems, local_sem):
        x, y, c = _place()
        me, sibling = (x, y, c), (x, y, 1 - c)
        chips = [(1 - x, y), (x, 1 - y), (1 - x, 1 - y)]

        def slot(px, py, pc):
            return out_ref.at[4 * px + 2 * py + pc]

        def copy(k, block, to, src=None):
            return pltpu.make_async_remote_copy(
                src_ref=slot(*block) if src is None else src, dst_ref=slot(*block),
                send_sem=send_sems.at[k], recv_sem=recv_sems.at[k], device_id=to, device_id_type=MESH)

        own = pltpu.make_async_copy(x_ref, slot(*me), local_sem)
        own.start()
        first = [copy(0, me, sibling, src=x_ref)]
        first += [copy(1 + j, me, (*chip, c), src=x_ref) for j, chip in enumerate(chips)]
        for cp in first:
            cp.start()
        passed = [copy(4 + j, (*chip, c), sibling) for j, chip in enumerate(chips)]
        for j, chip in enumerate(chips):
            copy(1 + j, (*chip, c), me).wait_recv()
            passed[j].start()
        copy(0, sibling, me).wait_recv()
        for j, chip in enumerate(chips):
            copy(4 + j, (*chip, 1 - c), me).wait_recv()
        for cp in first + passed:
            cp.wait_send()
        own.wait()

    return pl.pallas_call(
        body, name=name, out_shape=jax.ShapeDtypeStruct((8, R, LANES), mine.dtype),
        in_specs=[HBM_SPEC], out_specs=HBM_SPEC,
        scratch_shapes=[pltpu.SemaphoreType.DMA((7,)), pltpu.SemaphoreType.DMA((7,)), pltpu.SemaphoreType.DMA],
    )(mine)


def allgather_direct(mine, *, name):
    R = mine.shape[0]

    def body(x_ref, out_ref, send_sems, recv_sems, local_sem):
        x, y, c = _place()
        own = pltpu.make_async_copy(x_ref, out_ref.at[4 * x + 2 * y + c], local_sem)
        own.start()
        sends = []
        for f in range(1, 8):
            fx, fy, fc = (f >> 2) & 1, (f >> 1) & 1, f & 1
            px, py, pc = jnp.where(fx, 1 - x, x), jnp.where(fy, 1 - y, y), jnp.where(fc, 1 - c, c)
            sends.append(pltpu.make_async_remote_copy(
                src_ref=x_ref, dst_ref=out_ref.at[4 * x + 2 * y + c], send_sem=send_sems.at[f - 1],
                recv_sem=recv_sems.at[f - 1], device_id=(px, py, pc), device_id_type=MESH))
        for cp in sends:
            cp.start()
        for f in range(1, 8):
            fx, fy, fc = (f >> 2) & 1, (f >> 1) & 1, f & 1
            px, py, pc = jnp.where(fx, 1 - x, x), jnp.where(fy, 1 - y, y), jnp.where(fc, 1 - c, c)
            pltpu.make_async_remote_copy(
                src_ref=x_ref, dst_ref=out_ref.at[4 * px + 2 * py + pc], send_sem=send_sems.at[f - 1],
                recv_sem=recv_sems.at[f - 1], device_id=(px, py, pc), device_id_type=MESH).wait_recv()
        for cp in sends:
            cp.wait_send()
        own.wait()

    return pl.pallas_call(
        body, name=name, out_shape=jax.ShapeDtypeStruct((8, R, LANES), mine.dtype),
        in_specs=[HBM_SPEC], out_specs=HBM_SPEC,
        scratch_shapes=[pltpu.SemaphoreType.DMA((7,)), pltpu.SemaphoreType.DMA((7,)), pltpu.SemaphoreType.DMA],
    )(mine)


def send_to_sibling(v, *, name):
    def body(v_ref, out_ref, send_sem, recv_sem):
        x, y, c = _place()
        cp = pltpu.make_async_remote_copy(src_ref=v_ref, dst_ref=out_ref, send_sem=send_sem, recv_sem=recv_sem,
                                          device_id=(x, y, 1 - c), device_id_type=MESH)
        cp.start()
        cp.wait()

    return pl.pallas_call(
        body, name=name, out_shape=jax.ShapeDtypeStruct(v.shape, v.dtype), in_specs=[HBM_SPEC], out_specs=HBM_SPEC,
        scratch_shapes=[pltpu.SemaphoreType.DMA, pltpu.SemaphoreType.DMA],
    )(v)


def pair_gather(v, *, name):
    def body(v_ref, out_ref, send_sem, recv_sem, local_sem):
        x, y, c = _place()
        own = pltpu.make_async_copy(v_ref, out_ref.at[c], local_sem)
        own.start()
        cp = pltpu.make_async_remote_copy(src_ref=v_ref, dst_ref=out_ref.at[c], send_sem=send_sem, recv_sem=recv_sem,
                                          device_id=(x, y, 1 - c), device_id_type=MESH)
        cp.start()
        pltpu.make_async_remote_copy(src_ref=v_ref, dst_ref=out_ref.at[1 - c], send_sem=send_sem, recv_sem=recv_sem,
                                     device_id=(x, y, 1 - c), device_id_type=MESH).wait_recv()
        cp.wait_send()
        own.wait()

    return pl.pallas_call(
        body, name=name, out_shape=jax.ShapeDtypeStruct((2,) + v.shape, v.dtype), in_specs=[HBM_SPEC],
        out_specs=HBM_SPEC,
        scratch_shapes=[pltpu.SemaphoreType.DMA, pltpu.SemaphoreType.DMA, pltpu.SemaphoreType.DMA],
    )(v)


def chip_exchange(p, *, name):
    R = p.shape[1]

    def body(p_ref, out_ref, send_sems, recv_sems):
        x, y, c = _place()
        chips = [(1 - x, y), (x, 1 - y), (1 - x, 1 - y)]
        sends = [pltpu.make_async_remote_copy(
            src_ref=p_ref.at[2 * px + py], dst_ref=out_ref.at[j], send_sem=send_sems.at[j], recv_sem=recv_sems.at[j],
            device_id=(px, py, c), device_id_type=MESH) for j, (px, py) in enumerate(chips)]
        for cp in sends:
            cp.start()
        for cp in sends:
            cp.wait()

    return pl.pallas_call(
        body, name=name, out_shape=jax.ShapeDtypeStruct((3, R, LANES), p.dtype), in_specs=[HBM_SPEC],
        out_specs=HBM_SPEC,
        scratch_shapes=[pltpu.SemaphoreType.DMA((3,)), pltpu.SemaphoreType.DMA((3,))],
    )(p)


def add_blocks(terms, out_dtype, *, name, tile=1024):
    R = terms[0].shape[0]
    tr = R
    for d in range(16, min(R, tile) + 1, 16):
        if R % d == 0:
            tr = d

    def body(*refs):
        acc = refs[0][...].astype(F32)
        for ref in refs[1:-1]:
            acc = acc + ref[...].astype(F32)
        refs[-1][...] = acc.astype(out_dtype)

    spec = pl.BlockSpec((tr, LANES), lambda i: (i, 0))
    return pl.pallas_call(
        body, name=name, grid=(R // tr,), in_specs=[spec] * len(terms), out_specs=spec,
        out_shape=jax.ShapeDtypeStruct((R, LANES), out_dtype), compiler_params=_cparams("parallel"),
    )(*terms)


def _half_rows(arr, cc):
    hr = arr.shape[1] // 2
    return lax.dynamic_slice_in_dim(arr, cc * hr, hr, axis=1).reshape(-1)


def _flat_half(shards, cc, dtype):
    flat = jnp.concatenate([_half_rows(shards[n], cc).astype(dtype) for n in BIG])
    return flat.reshape(-1, LANES)


def _unflat_halves(flat_by_c, shapes):
    out, off = {}, 0
    for n in BIG:
        _, R, C = shapes[n]
        sz = 2 * (R // 2) * C
        out[n] = jnp.concatenate([flat_by_c[c][off:off + sz].reshape(2, R // 2, C) for c in range(2)], axis=1)
        off += sz
    return out


def _to_heads(a, nh):
    T = a.shape[0]
    return a.reshape(T, nh, a.shape[1] // nh).transpose(1, 0, 2)


def _from_heads(a):
    nh, T, d = a.shape
    return a.transpose(1, 0, 2).reshape(T, nh * d)


def _to_heads_t(a, nh):
    T = a.shape[0]
    return a.reshape(T, nh, a.shape[1] // nh).transpose(1, 2, 0)


def _from_heads_t(a):
    nh, d, T = a.shape
    return a.transpose(2, 0, 1).reshape(T, nh * d)


def _pad_cols(a, n):
    return jnp.pad(a, ((0, 0), (0, n - a.shape[1])))


def _pack_w_in(w):
    offs = [sum(IN_SPLITS[:i]) for i in range(len(IN_SPLITS) + 1)]
    sb, z, xbc, dt, cq, ckv, kr = [w[:, offs[i]:offs[i + 1]] for i in range(len(IN_SPLITS))]
    zeros = lambda n: jnp.zeros((w.shape[0], n), w.dtype)
    h = MLA_ROPE // 2
    kra = jnp.concatenate([zeros(MLA_NOPE), kr, zeros(LANES - MLA_QK)], axis=1)
    krb = jnp.concatenate([zeros(MLA_NOPE), -kr[:, h:], kr[:, :h], zeros(LANES - MLA_QK)], axis=1)
    return jnp.concatenate([sb, xbc, z, cq, ckv, _pad_cols(dt, LANES), kra, krb], axis=1)


def _unpack_gw_in(g):
    h = MLA_ROPE // 2
    ga, gb = g[:, OFF_KRA:OFF_KRA + LANES], g[:, OFF_KRB:OFF_KRB + LANES]
    gkr = ga[:, MLA_NOPE:MLA_QK] + jnp.concatenate([gb[:, MLA_NOPE + h:MLA_QK], -gb[:, MLA_NOPE:MLA_NOPE + h]], axis=1)
    return jnp.concatenate([g[:, OFF_SB:OFF_SB + 768], g[:, OFF_Z:OFF_Z + 512], g[:, OFF_XBC:OFF_XBC + 768],
                            g[:, OFF_DT:OFF_DT + 8], g[:, OFF_CQ:OFF_CQ + 256], g[:, OFF_CKV:OFF_CKV + 128], gkr], axis=1)


def _pack_w_uq(w):
    zeros = lambda n: jnp.zeros((w.shape[0], n), w.dtype)
    h = MLA_ROPE // 2
    pp, rr = [], []
    for i in range(MLA_HEADS):
        nope = w[:, MLA_QK * i:MLA_QK * i + MLA_NOPE]
        rope = w[:, MLA_QK * i + MLA_NOPE:MLA_QK * (i + 1)]
        pp += [nope, rope, zeros(LANES - MLA_QK)]
        rr += [zeros(MLA_NOPE), -rope[:, h:], rope[:, :h], zeros(LANES - MLA_QK)]
    return jnp.concatenate(pp, axis=1), jnp.concatenate(rr, axis=1)


def _unpack_gw_uq(gp, gr):
    h = MLA_ROPE // 2
    out = []
    for i in range(MLA_HEADS):
        b = LANES * i
        out.append(gp[:, b:b + MLA_NOPE])
        out.append(gp[:, b + MLA_NOPE:b + MLA_NOPE + h] + gr[:, b + MLA_NOPE + h:b + MLA_QK])
        out.append(gp[:, b + MLA_NOPE + h:b + MLA_QK] - gr[:, b + MLA_NOPE:b + MLA_NOPE + h])
    return jnp.concatenate(out, axis=1)


def _pack_w_ukv(w):
    zeros = lambda n: jnp.zeros((w.shape[0], n), w.dtype)
    kk, vv = [], []
    for i in range(MLA_HEADS):
        b = (MLA_NOPE + MLA_V) * i
        kk += [w[:, b:b + MLA_NOPE], zeros(LANES - MLA_NOPE)]
        vv.append(w[:, b + MLA_NOPE:b + MLA_NOPE + MLA_V])
    return jnp.concatenate(kk, axis=1), jnp.concatenate(vv, axis=1)


def _unpack_gw_ukv(gk, gv):
    out = []
    for i in range(MLA_HEADS):
        out += [gk[:, LANES * i:LANES * i + MLA_NOPE], gv[:, MLA_V * i:MLA_V * (i + 1)]]
    return jnp.concatenate(out, axis=1)


def _rope_tables(positions):
    inv_freq = 1.0 / (ROPE_THETA ** (jnp.arange(0, MLA_ROPE, 2, dtype=F32) / MLA_ROPE))
    ang = positions.astype(F32)[:, None] * inv_freq
    cos, sin = jnp.cos(ang), jnp.sin(ang)
    T = positions.shape[0]
    one, zero = jnp.ones((T, MLA_NOPE), F32), jnp.zeros((T, MLA_NOPE), F32)
    pad1, pad0 = jnp.ones((T, LANES - MLA_QK), F32), jnp.zeros((T, LANES - MLA_QK), F32)
    return jnp.concatenate([one, cos, cos, pad1], axis=1), jnp.concatenate([zero, sin, sin, pad0], axis=1)


def _row(v):
    return v.reshape(1, -1)


def _pad_row(v):
    return _pad_cols(v.reshape(1, -1), LANES)


def _layer_weights(full, small, li):
    p = {}
    p["w_in_p"] = _pack_w_in(full["w_in"][li])
    p["w_in_pt"] = p["w_in_p"].T
    p["wqp"], p["wqr"] = _pack_w_uq(full["mla_w_uq"][li])
    p["wkp"], p["wvp"] = _pack_w_ukv(full["mla_w_ukv"][li])
    p["w_out"] = full["w_out"][li]
    p["w_out_t"] = p["w_out"].T
    p["w_up"] = full["ffn_w_up"][li]
    p["w_up_t"] = p["w_up"].T
    p["w_down"] = full["ffn_w_down"][li]
    p["w_down_t"] = p["w_down"].T
    for n in ("mix_norm", "sb_out_norm", "ssm_conv_b", "ssm_out_norm", "mla_q_norm", "mla_kv_norm", "mla_out_norm",
              "ffn_norm", "ffn_conv_b"):
        p[n] = _row(small[n][li])
    for n in ("ssm_dt_bias", "ssm_a_log", "ssm_d"):
        p[n] = _pad_row(small[n][li])
    p["ssm_conv_w"] = small["ssm_conv_w"][li]
    p["ffn_conv_w"] = small["ffn_conv_w"][li]
    return p


def _layer_fwd(h, p, cos, sin, li):
    T = h.shape[0]
    nm = lambda s: "l%d_%s" % (li, s)
    s = {"h": h}
    (n1,) = rowwise(rms_fn, [h], [p["mix_norm"]], [(D_MODEL, BF16)], name=nm("mix_norm"))
    proj = matmul(n1, p["w_in_p"], name=nm("in_proj"))
    s["n1"], s["proj"] = n1, proj
    qkv = proj[:, OFF_SB:OFF_SB + 768]
    s["sb_q"] = _to_heads_t((qkv[:, 0:256] * (SB_DIM ** -0.5)).astype(BF16), SB_HEADS)
    s["sb_k"] = _to_heads(qkv[:, 256:512].astype(BF16), SB_HEADS)
    s["sb_v"] = _to_heads(qkv[:, 512:768].astype(BF16), SB_HEADS)
    y_sb_hm, s["sb_bt"] = sb_fwd(s["sb_q"], s["sb_k"], s["sb_v"], name=nm("sb_fwd"))
    s["y_sb"] = _from_heads_t(y_sb_hm)
    xbc = ssm_conv_act(proj, p["ssm_conv_w"], p["ssm_conv_b"], name=nm("ssm_conv"))
    s["x_hm"] = _to_heads(xbc[:, :SSM_INNER], SSM_HEADS)
    s["b_hm"] = _to_heads(xbc[:, SSM_INNER:SSM_INNER + 128], SSM_GROUPS)
    s["c_hm"] = _to_heads(xbc[:, SSM_INNER + 128:], SSM_GROUPS)
    y_ssm_hm, s["states"] = ssd_fwd(s["x_hm"], s["b_hm"], s["c_hm"], proj, p["ssm_dt_bias"], p["ssm_a_log"],
                                    p["ssm_d"], name=nm("ssd_fwd"))
    s["y_ssm"] = _from_heads(y_ssm_hm)
    rows = [(proj, 256, OFF_CQ // 256), (proj, 128, OFF_CKV // 128), (proj, 128, OFF_KRA // 128),
            (proj, 128, OFF_KRB // 128), cos, sin]
    qp, kp, vv = rowwise(mla_prep_fn, rows, [p["mla_q_norm"], p["mla_kv_norm"], p["wqp"], p["wqr"], p["wkp"], p["wvp"]],
                         [(512, BF16), (512, BF16), (256, BF16)], name=nm("mla_prep"))
    s["mla_q"], s["mla_k"], s["mla_v"] = _to_heads_t(qp, MLA_HEADS), _to_heads(kp, MLA_HEADS), _to_heads(vv, MLA_HEADS)
    s["mla_o"], s["mla_lse"] = mla_fwd(s["mla_q"], s["mla_k"], s["mla_v"], name=nm("mla_fwd"))
    s["y_mla"] = _from_heads_t(s["mla_o"])
    (cat,) = rowwise(merge_fn, [s["y_sb"], s["y_ssm"], (proj, 512, OFF_Z // 512), s["y_mla"]],
                     [p["sb_out_norm"], p["ssm_out_norm"], p["mla_out_norm"]], [(D_MODEL, BF16)], name=nm("merge"),
                     post=lambda a, b, c: (jnp.concatenate([a, b, c], axis=1),))
    s["cat"] = cat
    h1 = matmul(cat, p["w_out"], name=nm("out_proj"), residual=h)
    s["h1"] = h1
    (n2,) = rowwise(rms_fn, [h1], [p["ffn_norm"]], [(D_MODEL, BF16)], name=nm("ffn_norm"))
    up = matmul(n2, p["w_up"], name=nm("ffn_up"))
    act = ffn_act(up, p["ffn_conv_w"], p["ffn_conv_b"], name=nm("ffn_act"))
    s["n2"], s["up"], s["act"] = n2, up, act
    h2 = matmul(act, p["w_down"], name=nm("ffn_down"), residual=h1)
    return h2, s


def _layer_bwd(dh2, s, p, cos, sin, li):
    nm = lambda t: "l%d_%s" % (li, t)
    g = {}
    proj = s["proj"]
    g["ffn_w_down"] = matmul(s["act"], dh2, name=nm("g_w_down"), ta=True)
    d_act = matmul(dh2, p["w_down_t"], name=nm("d_act"), out_dtype=BF16)
    du = ffn_act_bwd_a(s["up"], p["ffn_conv_w"], p["ffn_conv_b"], d_act, name=nm("ffn_act_bwd"))
    d_up, g["ffn_conv_w"], gcb = conv_bwd_b(du, s["up"], 0, p["ffn_conv_w"], name=nm("ffn_conv_bwd"), out_dtype=BF16,
                                            tc=1408)
    g["ffn_conv_b"] = gcb[0]
    g["ffn_w_up"] = matmul(s["n2"], d_up, name=nm("g_w_up"), ta=True)
    d_n2 = matmul(d_up, p["w_up_t"], name=nm("d_n2"))
    (dh1,), (gn,) = rowwise_bwd(rms_fn, [s["h1"]], [], [p["ffn_norm"]], [d_n2], [F32], name=nm("ffn_norm_bwd"),
                                add0=dh2)
    g["ffn_norm"] = gn[0]
    g["w_out"] = matmul(s["cat"], dh1, name=nm("g_w_out"), ta=True)
    d_cat = matmul(dh1, p["w_out_t"], name=nm("d_cat"))
    (d_ysb, d_yssm, d_z, d_ymla), (g1, g2, g3) = rowwise_bwd(
        merge_fn, [s["y_sb"], s["y_ssm"], (proj, 512, OFF_Z // 512), s["y_mla"]], [],
        [p["sb_out_norm"], p["ssm_out_norm"], p["mla_out_norm"]], [d_cat], [F32, F32, BF16, F32], name=nm("merge_bwd"),
        pre_ct=lambda d: (d[:, 0:256], d[:, 256:768], d[:, 768:1024]))
    g["sb_out_norm"], g["ssm_out_norm"], g["mla_out_norm"] = g1[0], g2[0], g3[0]
    dq, dk, dv = sb_bwd(s["sb_q"], s["sb_k"], s["sb_v"], _to_heads_t(d_ysb, SB_HEADS), s["sb_bt"], name=nm("sb_bwd"),
                        q_scale=SB_DIM ** -0.5)
    d_sb = jnp.concatenate([_from_heads_t(dq), _from_heads(dk), _from_heads(dv)], axis=1).astype(BF16)
    dqp, dkp, dvv = mla_bwd(s["mla_q"], s["mla_k"], s["mla_v"], _to_heads_t(d_ymla, MLA_HEADS), s["mla_o"], s["mla_lse"],
                            name=nm("mla_bwd"))
    rows = [(proj, 256, OFF_CQ // 256), (proj, 128, OFF_CKV // 128), (proj, 128, OFF_KRA // 128),
            (proj, 128, OFF_KRB // 128)]
    (d_cq, d_ckv, d_kra, d_krb), (gqn, gkvn, gwqp, gwqr, gwkp, gwvp) = rowwise_bwd(
        mla_prep_fn, rows, [cos, sin], [p["mla_q_norm"], p["mla_kv_norm"], p["wqp"], p["wqr"], p["wkp"], p["wvp"]],
        [_from_heads_t(dqp), _from_heads(dkp), _from_heads(dvv)], [BF16] * 4, name=nm("mla_prep_bwd"), tile=256)
    g["mla_q_norm"], g["mla_kv_norm"] = gqn[0], gkvn[0]
    g["mla_w_uq"] = _unpack_gw_uq(gwqp, gwqr)
    g["mla_w_ukv"] = _unpack_gw_ukv(gwkp, gwvp)
    dx_hm, db_hm, dc_hm, d_dt, gdb, gal, gds = ssd_bwd(
        s["x_hm"], s["b_hm"], s["c_hm"], proj, s["states"], p["ssm_dt_bias"], p["ssm_a_log"], p["ssm_d"],
        _to_heads(d_yssm, SSM_HEADS), name=nm("ssd_bwd"))
    g["ssm_dt_bias"], g["ssm_a_log"], g["ssm_d"] = gdb[0, :8], gal[0, :8], gds[0, :8]
    d_xbc_act = jnp.concatenate([_from_heads(dx_hm), _from_heads(db_hm), _from_heads(dc_hm)], axis=1)
    d_pre = ssm_conv_bwd_a(proj, p["ssm_conv_w"], p["ssm_conv_b"], d_xbc_act, name=nm("ssm_conv_bwd_a"))
    d_xbc, g["ssm_conv_w"], gscb = conv_bwd_b(d_pre, proj, OFF_XBC, p["ssm_conv_w"], name=nm("ssm_conv_bwd_b"),
                                              out_dtype=BF16, tc=256)
    g["ssm_conv_b"] = gscb[0]
    d_proj = jnp.concatenate([d_sb, d_xbc, d_z, d_cq, d_ckv, d_dt.astype(BF16), d_kra, d_krb], axis=1)
    g["w_in"] = _unpack_gw_in(matmul(s["n1"], d_proj, name=nm("g_w_in"), ta=True))
    d_n1 = matmul(d_proj, p["w_in_pt"], name=nm("d_n1"))
    (dh0,), (gm,) = rowwise_bwd(rms_fn, [s["h"]], [], [p["mix_norm"]], [d_n1], [F32], name=nm("mix_norm_bwd"),
                                add0=dh1)
    g["mix_norm"] = gm[0]
    return dh0, g


def kernel(x, positions, mix_norm, w_in, sb_out_norm, ssm_conv_w, ssm_conv_b, ssm_dt_bias, ssm_a_log, ssm_d, ssm_out_norm, mla_q_norm, mla_w_uq, mla_kv_norm, mla_w_ukv, mla_out_norm, w_out, ffn_norm, ffn_w_up, ffn_conv_w, ffn_conv_b, ffn_w_down, final_norm, loss_target, m_mix_norm, m_w_in, m_sb_out_norm, m_ssm_conv_w, m_ssm_conv_b, m_ssm_dt_bias, m_ssm_a_log, m_ssm_d, m_ssm_out_norm, m_mla_q_norm, m_mla_w_uq, m_mla_kv_norm, m_mla_w_ukv, m_mla_out_norm, m_w_out, m_ffn_norm, m_ffn_w_up, m_ffn_conv_w, m_ffn_conv_b, m_ffn_w_down, m_final_norm, v_mix_norm, v_w_in, v_sb_out_norm, v_ssm_conv_w, v_ssm_conv_b, v_ssm_dt_bias, v_ssm_a_log, v_ssm_d, v_ssm_out_norm, v_mla_q_norm, v_mla_w_uq, v_mla_kv_norm, v_mla_w_ukv, v_mla_out_norm, v_w_out, v_ffn_norm, v_ffn_w_up, v_ffn_conv_w, v_ffn_conv_b, v_ffn_w_down, v_final_norm):
    W = dict(mix_norm=mix_norm, w_in=w_in, sb_out_norm=sb_out_norm, ssm_conv_w=ssm_conv_w, ssm_conv_b=ssm_conv_b,
             ssm_dt_bias=ssm_dt_bias, ssm_a_log=ssm_a_log, ssm_d=ssm_d, ssm_out_norm=ssm_out_norm,
             mla_q_norm=mla_q_norm, mla_w_uq=mla_w_uq, mla_kv_norm=mla_kv_norm, mla_w_ukv=mla_w_ukv,
             mla_out_norm=mla_out_norm, w_out=w_out, ffn_norm=ffn_norm, ffn_w_up=ffn_w_up, ffn_conv_w=ffn_conv_w,
             ffn_conv_b=ffn_conv_b, ffn_w_down=ffn_w_down, final_norm=final_norm)
    M = dict(mix_norm=m_mix_norm, w_in=m_w_in, sb_out_norm=m_sb_out_norm, ssm_conv_w=m_ssm_conv_w,
             ssm_conv_b=m_ssm_conv_b, ssm_dt_bias=m_ssm_dt_bias, ssm_a_log=m_ssm_a_log, ssm_d=m_ssm_d,
             ssm_out_norm=m_ssm_out_norm, mla_q_norm=m_mla_q_norm, mla_w_uq=m_mla_w_uq, mla_kv_norm=m_mla_kv_norm,
             mla_w_ukv=m_mla_w_ukv, mla_out_norm=m_mla_out_norm, w_out=m_w_out, ffn_norm=m_ffn_norm,
             ffn_w_up=m_ffn_w_up, ffn_conv_w=m_ffn_conv_w, ffn_conv_b=m_ffn_conv_b, ffn_w_down=m_ffn_w_down,
             final_norm=m_final_norm)
    V = dict(mix_norm=v_mix_norm, w_in=v_w_in, sb_out_norm=v_sb_out_norm, ssm_conv_w=v_ssm_conv_w,
             ssm_conv_b=v_ssm_conv_b, ssm_dt_bias=v_ssm_dt_bias, ssm_a_log=v_ssm_a_log, ssm_d=v_ssm_d,
             ssm_out_norm=v_ssm_out_norm, mla_q_norm=v_mla_q_norm, mla_w_uq=v_mla_w_uq, mla_kv_norm=v_mla_kv_norm,
             mla_w_ukv=v_mla_w_ukv, mla_out_norm=v_mla_out_norm, w_out=v_w_out, ffn_norm=v_ffn_norm,
             ffn_w_up=v_ffn_w_up, ffn_conv_w=v_ffn_conv_w, ffn_conv_b=v_ffn_conv_b, ffn_w_down=v_ffn_w_down,
             final_norm=v_final_norm)
    depth = mix_norm.shape[0]
    cx, cy, cc = _place()
    chip = 2 * cx + cy
    T = x.shape[1]

    shard_shapes = {n: W[n].shape for n in BIG}
    gathered = allgather_blocks(_flat_half(W, cc, BF16), name="gather_weights")
    full = {}
    per_chip = [_unflat_halves([gathered[2 * k + c].reshape(-1) for c in range(2)], shard_shapes) for k in range(4)]
    for n in BIG:
        full[n] = jnp.concatenate([per_chip[k][n] for k in range(4)], axis=BIG_AXIS[n])
    conv_full = {}
    small = {n: W[n] for n in SMALL_REPL}
    cw_flat = jnp.concatenate([W[n].reshape(-1) for n in SMALL_SHARD])
    cw_rows = -(-cw_flat.shape[0] // (8 * LANES)) * 8
    cw_all = allgather_direct(jnp.pad(cw_flat, (0, cw_rows * LANES - cw_flat.shape[0])).reshape(cw_rows, LANES),
                              name="gather_conv_taps")
    off = 0
    for n in SMALL_SHARD:
        sz = W[n].size
        conv_full[n] = jnp.concatenate(
            [cw_all[2 * k].reshape(-1)[off:off + sz].reshape(W[n].shape) for k in range(4)], axis=2)
        off += sz
    small.update(conv_full)

    cos, sin = _rope_tables(positions[0])
    params = [_layer_weights(full, small, li) for li in range(depth)]

    h = x[0]
    saved = []
    for li in range(depth):
        h, s = _layer_fwd(h, params[li], cos, sin, li)
        saved.append(s)
    dh, g_final, loss_lanes = loss_head(h, loss_target[0], _row(final_norm), name="loss_head")

    grads = [None] * depth
    for li in reversed(range(depth)):
        dh, grads[li] = _layer_bwd(dh, saved[li], params[li], cos, sin, li)
    grad_x = dh[None]
    G = {n: jnp.stack([grads[li][n] for li in range(depth)]) for n in WEIGHTS if n != "final_norm"}
    G["final_norm"] = g_final[0]

    def shard_major(n):
        a = G[n]
        ax = BIG_AXIS[n]
        parts = jnp.split(a, 4, axis=ax)
        return parts

    by_chip = {n: shard_major(n) for n in BIG}

    def flat_for(k, which):
        return _flat_half({n: by_chip[n][k] for n in BIG}, which, BF16)

    mine_first = jnp.stack([flat_for(k, cc) for k in range(4)])
    for_sibling = jnp.stack([flat_for(k, 1 - cc) for k in range(4)])
    R = mine_first.shape[1]
    from_sibling = send_to_sibling(for_sibling.reshape(4 * R, LANES), name="grads_to_sibling")
    pair = add_blocks([mine_first.reshape(4 * R, LANES), from_sibling], BF16, name="grads_pair_sum").reshape(4, R, LANES)
    others = chip_exchange(pair, name="grads_chip_exchange")
    own = lax.dynamic_index_in_dim(pair, chip, 0, keepdims=False)
    half = add_blocks([own, others[0], others[1], others[2]], F32, name="grads_chip_sum")
    both = pair_gather(half, name="grads_pair_gather")
    g_big = _unflat_halves([both[c].reshape(-1) for c in range(2)], shard_shapes)

    small_list = [G[n].reshape(-1) for n in SMALL_REPL] + [G[n].reshape(-1) for n in SMALL_SHARD]
    small_list.append(jnp.sum(loss_lanes).reshape(1))
    sm = jnp.concatenate(small_list)
    n_small = sm.shape[0]
    sm_rows = -(-n_small // (16 * LANES)) * 16
    sm_all = allgather_direct(jnp.pad(sm, (0, sm_rows * LANES - n_small)).reshape(sm_rows, LANES), name="gather_small")
    sm_sum = add_blocks([sm_all[d] for d in range(8)], F32, name="small_sum").reshape(-1)
    g_small, off = {}, 0
    for n in SMALL_REPL:
        g_small[n] = sm_sum[off:off + W[n].size].reshape(W[n].shape)
        off += W[n].size
    for n in SMALL_SHARD:
        full_shape = conv_full[n].shape
        sz = conv_full[n].size
        gfull = sm_sum[off:off + sz].reshape(full_shape)
        width = W[n].shape[2]
        g_small[n] = lax.dynamic_slice_in_dim(gfull, chip * width, width, axis=2)
        off += sz
    loss = sm_sum[off]

    grad_out, delta, new_m, new_v = {}, {}, {}, {}
    for n in BIG:
        shp = W[n].shape
        two_d = lambda a: a.reshape(shp[0] * shp[1], shp[2])
        d, nm_, nv_ = adamw(two_d(W[n]), two_d(g_big[n]), two_d(M[n]), two_d(V[n]), name="adamw_" + n)
        grad_out[n], delta[n], new_m[n], new_v[n] = g_big[n], d.reshape(shp), nm_.reshape(shp), nv_.reshape(shp)
    small_names = SMALL_REPL + SMALL_SHARD

    def flat_small(d):
        f = jnp.concatenate([d[n].reshape(-1) for n in small_names])
        rows = -(-f.shape[0] // (8 * LANES)) * 8
        return jnp.pad(f, (0, rows * LANES - f.shape[0])).reshape(rows, LANES)

    vpad = flat_small(V)
    d, nm_, nv_ = adamw(flat_small(W), flat_small(g_small), flat_small(M), vpad, name="adamw_small")
    off = 0
    for n in small_names:
        sz = W[n].size
        grad_out[n] = g_small[n]
        delta[n] = d.reshape(-1)[off:off + sz].reshape(W[n].shape)
        new_m[n] = nm_.reshape(-1)[off:off + sz].reshape(W[n].shape)
        new_v[n] = nv_.reshape(-1)[off:off + sz].reshape(W[n].shape)
        off += sz

    return (loss, grad_x, *[grad_out[n] for n in WEIGHTS], *[delta[n] for n in WEIGHTS],
            *[new_m[n] for n in WEIGHTS], *[new_v[n] for n in WEIGHTS])
```

```python
import functools
import math

import jax
import jax.numpy as jnp
from jax import lax
from jax.experimental import pallas as pl
from jax.experimental.pallas import tpu as pltpu

F32 = jnp.float32
BF16 = jnp.bfloat16

EPS = 1e-6
D_MODEL = 1024
SB_HEADS, SB_DIM = 4, 64
SSM_HEADS, SSM_DIM, SSM_GROUPS, SSM_STATE, SSM_CHUNK = 8, 64, 2, 64, 128
SSM_INNER = SSM_HEADS * SSM_DIM
SSM_CONV_DIM = SSM_INNER + 2 * SSM_GROUPS * SSM_STATE
MLA_HEADS, MLA_NOPE, MLA_ROPE, MLA_V = 4, 64, 32, 64
MLA_QK = MLA_NOPE + MLA_ROPE
MLA_SCALE = MLA_QK ** -0.5
ROPE_THETA = 10000.0
D_FF = 2816
IN_SPLITS = (768, 512, 768, 8, 256, 128, 32)

OFF_SB, OFF_XBC, OFF_Z, OFF_CQ, OFF_CKV, OFF_DT, OFF_KRA, OFF_KRB = 0, 768, 1536, 2048, 2304, 2432, 2560, 2688
D_IN_P = 2816
LANES = 128

ADAM_LR, ADAM_B1, ADAM_B2, ADAM_EPS, ADAM_WD, ADAM_STEP = 0.001, 0.9, 0.999, 1e-08, 0.01, 10

V7X_VMEM_LIMIT = 48 * 1024 * 1024

NT = (((1,), (1,)), ((), ()))
TN = (((0,), (0,)), ((), ()))

BIG = ("w_in", "mla_w_uq", "mla_w_ukv", "w_out", "ffn_w_up", "ffn_w_down")
BIG_AXIS = {"w_in": 2, "mla_w_uq": 2, "mla_w_ukv": 2, "w_out": 1, "ffn_w_up": 2, "ffn_w_down": 1}
SMALL_REPL = ("mix_norm", "sb_out_norm", "ssm_conv_b", "ssm_dt_bias", "ssm_a_log", "ssm_d", "ssm_out_norm",
              "mla_q_norm", "mla_kv_norm", "mla_out_norm", "ffn_norm", "ffn_conv_b", "final_norm")
SMALL_SHARD = ("ssm_conv_w", "ffn_conv_w")
WEIGHTS = ("mix_norm", "w_in", "sb_out_norm", "ssm_conv_w", "ssm_conv_b", "ssm_dt_bias", "ssm_a_log", "ssm_d",
           "ssm_out_norm", "mla_q_norm", "mla_w_uq", "mla_kv_norm", "mla_w_ukv", "mla_out_norm", "w_out", "ffn_norm",
           "ffn_w_up", "ffn_conv_w", "ffn_conv_b", "ffn_w_down", "final_norm")


def _cparams(*sem):
    return pltpu.CompilerParams(dimension_semantics=sem if sem else None, vmem_limit_bytes=V7X_VMEM_LIMIT)


def _pick(n, target, mult=LANES):
    best = None
    for d in range(mult, min(n, target) + 1, mult):
        if n % d == 0:
            best = d
    return best or n


def _sigmoid(x):
    return 1.0 / (1.0 + jnp.exp(-x))


def _softplus(x):
    ax = jnp.where(x > 0, x, -x)
    return jnp.where(x > 0, x, 0.0) + jnp.log(1.0 + jnp.exp(-ax))


def _rms(x, g):
    return x * lax.rsqrt(jnp.mean(x * x, axis=-1, keepdims=True) + EPS) * g


def _raw_nn(a, b):
    return jnp.dot(a.astype(BF16), b.astype(BF16), preferred_element_type=F32)


def _raw_nt(a, b):
    return lax.dot_general(a.astype(BF16), b.astype(BF16), NT, preferred_element_type=F32)


def _raw_tn(a, b):
    return lax.dot_general(a.astype(BF16), b.astype(BF16), TN, preferred_element_type=F32)


@jax.custom_vjp
def mm_nn(a, b):
    return _raw_nn(a, b)


mm_nn.defvjp(lambda a, b: (_raw_nn(a, b), (a, b)),
             lambda r, ct: (_raw_nt(ct, r[1]), _raw_tn(r[0], ct)))


@jax.custom_vjp
def mm_nt(a, b):
    return _raw_nt(a, b)


mm_nt.defvjp(lambda a, b: (_raw_nt(a, b), (a, b)),
             lambda r, ct: (_raw_nn(ct, r[1]), _raw_tn(ct, r[0])))


@jax.custom_vjp
def mm_tn(a, b):
    return _raw_tn(a, b)


mm_tn.defvjp(lambda a, b: (_raw_tn(a, b), (a, b)),
             lambda r, ct: (_raw_nt(r[1], ct), _raw_nn(r[0], ct)))


def _split_dot(x, m, terms):
    acc = None
    r = x
    for t in range(terms):
        xt = r.astype(BF16)
        d = jnp.dot(xt, m, preferred_element_type=F32)
        acc = d if acc is None else acc + d
        if t + 1 < terms:
            r = r - xt.astype(F32)
    return acc


def _tri_dot(tri, x, terms=3):
    parts = []
    r = x
    for t in range(terms):
        xt = r.astype(BF16)
        parts.append(xt)
        if t + 1 < terms:
            r = r - xt.astype(F32)
    return jnp.dot(jnp.concatenate([tri] * terms, axis=1), jnp.concatenate(parts, axis=0),
                   preferred_element_type=F32)


def _tri(n, cmp):
    r = lax.broadcasted_iota(jnp.int32, (n, n), 0)
    c = lax.broadcasted_iota(jnp.int32, (n, n), 1)
    return cmp(r, c).astype(BF16)


@jax.custom_vjp
def csum_rows(x):
    return _tri_dot(_tri(x.shape[0], lambda r, c: r >= c), x)


csum_rows.defvjp(lambda x: (csum_rows(x), None),
                 lambda _, ct: (_tri_dot(_tri(ct.shape[0], lambda r, c: r <= c), ct),))


def matmul(a, b, *, name, out_dtype=F32, ta=False, residual=None):
    if ta:
        K, M = a.shape
    else:
        M, K = a.shape
    N = b.shape[1]
    tm = _pick(M, 1408)
    tn = _pick(N, 1408)
    tk = _pick(K, 1408)
    nk = K // tk
    has_res = residual is not None

    def body(*refs):
        if has_res:
            a_ref, b_ref, r_ref, o_ref, acc = refs
        else:
            a_ref, b_ref, o_ref, acc = refs
        k = pl.program_id(2)

        @pl.when(k == 0)
        def _():
            acc[...] = jnp.zeros_like(acc)

        av = a_ref[...].astype(BF16)
        bv = b_ref[...].astype(BF16)
        if ta:
            acc[...] += lax.dot_general(av, bv, TN, preferred_element_type=F32)
        else:
            acc[...] += jnp.dot(av, bv, preferred_element_type=F32)

        @pl.when(k == nk - 1)
        def _():
            r = acc[...]
            if has_res:
                r = r + r_ref[...].astype(F32)
            o_ref[...] = r.astype(o_ref.dtype)

    a_spec = pl.BlockSpec((tk, tm), lambda i, j, k: (k, i)) if ta else pl.BlockSpec((tm, tk), lambda i, j, k: (i, k))
    in_specs = [a_spec, pl.BlockSpec((tk, tn), lambda i, j, k: (k, j))]
    args = [a, b]
    if has_res:
        in_specs.append(pl.BlockSpec((tm, tn), lambda i, j, k: (i, j)))
        args.append(residual)
    return pl.pallas_call(
        body, name=name, grid=(M // tm, N // tn, nk),
        in_specs=in_specs, out_specs=pl.BlockSpec((tm, tn), lambda i, j, k: (i, j)),
        out_shape=jax.ShapeDtypeStruct((M, N), out_dtype),
        scratch_shapes=[pltpu.VMEM((tm, tn), F32)],
        compiler_params=_cparams("parallel", "parallel", "arbitrary"),
    )(*args)


def _row_spec(entry, tl):
    if isinstance(entry, tuple):
        arr, width, cb = entry
        return arr, pl.BlockSpec((tl, width), lambda i, cb=cb: (i, cb))
    return entry, pl.BlockSpec((tl, entry.shape[1]), lambda i: (i, 0))


def _rows_T(entry):
    return (entry[0] if isinstance(entry, tuple) else entry).shape[0]


def rowwise(fn, rows, params, outs, *, name, tile=512, post=None):
    T = _rows_T(rows[0])
    tl = min(T, tile)
    nr, npar = len(rows), len(params)

    def body(*refs):
        r = [ref[...].astype(F32) for ref in refs[:nr]]
        p = [ref[...].astype(F32) for ref in refs[nr:nr + npar]]
        res = fn(*r, *p)
        if post is not None:
            res = post(*res)
        for o_ref, val in zip(refs[nr + npar:], res):
            o_ref[...] = val.astype(o_ref.dtype)

    arrs, specs = [], []
    for e in rows:
        a, s = _row_spec(e, tl)
        arrs.append(a)
        specs.append(s)
    for p in params:
        arrs.append(p)
        specs.append(pl.BlockSpec(p.shape, lambda i: (0, 0)))
    res = pl.pallas_call(
        body, name=name, grid=(T // tl,), in_specs=specs,
        out_specs=[pl.BlockSpec((tl, c), lambda i: (i, 0)) for c, _ in outs],
        out_shape=[jax.ShapeDtypeStruct((T, c), dt) for c, dt in outs],
        compiler_params=_cparams("parallel"),
    )(*arrs)
    return res


def rowwise_bwd(fn, rows, nd_rows, params, cts, grad_dtypes, *, name, tile=512, pre_ct=None, add0=None):
    T = _rows_T(rows[0])
    tl = min(T, tile)
    nr, nn, npar, nc = len(rows), len(nd_rows), len(params), len(cts)
    has_add = add0 is not None

    def body(*refs):
        pos = 0
        r = [ref[...].astype(F32) for ref in refs[pos:pos + nr]]
        pos += nr
        nd = [ref[...].astype(F32) for ref in refs[pos:pos + nn]]
        pos += nn
        p = [ref[...].astype(F32) for ref in refs[pos:pos + npar]]
        pos += npar
        c = [ref[...].astype(F32) for ref in refs[pos:pos + nc]]
        pos += nc
        if has_add:
            addv = refs[pos][...].astype(F32)
            pos += 1
        rg_refs = refs[pos:pos + nr]
        pg_refs = refs[pos + nr:pos + nr + npar]
        if pre_ct is not None:
            c = list(pre_ct(*c))
        _, vjp = jax.vjp(lambda *a: fn(*a[:nr], *nd, *a[nr:]), *r, *p)
        g = vjp(tuple(c))
        for j, ref in enumerate(rg_refs):
            val = g[j]
            if has_add and j == 0:
                val = val + addv
            ref[...] = val.astype(ref.dtype)
        if npar:
            @pl.when(pl.program_id(0) == 0)
            def _():
                for ref in pg_refs:
                    ref[...] = jnp.zeros_like(ref)
            for j, ref in enumerate(pg_refs):
                ref[...] += g[nr + j]

    arrs, specs = [], []
    widths = []
    for e in list(rows) + list(nd_rows):
        a, s = _row_spec(e, tl)
        arrs.append(a)
        specs.append(s)
        widths.append(s.block_shape[1])
    for p in params:
        arrs.append(p)
        specs.append(pl.BlockSpec(p.shape, lambda i: (0, 0)))
    for e in cts:
        a, s = _row_spec(e, tl)
        arrs.append(a)
        specs.append(s)
    if has_add:
        a, s = _row_spec(add0, tl)
        arrs.append(a)
        specs.append(s)
    out_specs = [pl.BlockSpec((tl, widths[j]), lambda i: (i, 0)) for j in range(nr)]
    out_shape = [jax.ShapeDtypeStruct((T, widths[j]), grad_dtypes[j]) for j in range(nr)]
    out_specs += [pl.BlockSpec(p.shape, lambda i: (0, 0)) for p in params]
    out_shape += [jax.ShapeDtypeStruct(p.shape, F32) for p in params]
    res = pl.pallas_call(
        body, name=name, grid=(T // tl,), in_specs=specs, out_specs=out_specs, out_shape=out_shape,
        compiler_params=_cparams("arbitrary"),
    )(*arrs)
    return list(res[:nr]), list(res[nr:])


def rms_fn(h, g):
    return (_rms(h, g),)


def merge_fn(ysb, yssm, z, ymla, g_sb, g_ssm, g_mla):
    ya = _rms(ysb, g_sb)
    yb = _rms(yssm * (z * _sigmoid(z)), g_ssm)
    yc = _rms(ymla, g_mla)
    return ya, yb, yc


def mla_prep_fn(cq, ckv, kra, krb, cos, sin, qn, kvn, wqp, wqr, wkp, wvp):
    cos4 = jnp.concatenate([cos] * MLA_HEADS, axis=1)
    sin4 = jnp.concatenate([sin] * MLA_HEADS, axis=1)
    nq = _rms(cq, qn)
    q = (mm_nn(nq, wqp) * cos4 + mm_nn(nq, wqr) * sin4) * MLA_SCALE
    nkv = _rms(ckv, kvn)
    kpe = kra * cos + krb * sin
    k = mm_nn(nkv, wkp) + jnp.concatenate([kpe] * MLA_HEADS, axis=1)
    v = mm_nn(nkv, wvp)
    return q, k, v


HALO = 8


def _prev_halo_spec(tl, tc, col_of):
    return pl.BlockSpec((HALO, tc), lambda i, j: (jnp.maximum(i * (tl // HALO) - 1, 0), col_of(j)))


def _fill_prev(buf, x_ref, halo_ref, i):
    buf[0:HALO, :] = jnp.where(i > 0, halo_ref[...].astype(F32), 0.0)
    buf[HALO:, :] = x_ref[...].astype(F32)


def _conv_from(buf, w_ref, b_ref, K, tl):
    acc = b_ref[...].astype(F32) + jnp.zeros((tl, buf.shape[1]), F32)
    for k in range(K):
        acc = acc + buf[pl.ds(HALO - (K - 1 - k), tl), :] * w_ref[k:k + 1, :].astype(F32)
    return acc


def ssm_conv_act(proj, w, b, *, name, tile=512, tc=256):
    T = proj.shape[0]
    K, C = w.shape
    tl = min(T, tile)
    c0 = OFF_XBC // tc

    def body(x_ref, halo_ref, w_ref, b_ref, o_ref, buf):
        _fill_prev(buf, x_ref, halo_ref, pl.program_id(0))
        u = _conv_from(buf, w_ref, b_ref, K, tl)
        o_ref[...] = u * _sigmoid(u)

    return pl.pallas_call(
        body, name=name, grid=(T // tl, C // tc),
        in_specs=[pl.BlockSpec((tl, tc), lambda i, j: (i, c0 + j)), _prev_halo_spec(tl, tc, lambda j: c0 + j),
                  pl.BlockSpec((K, tc), lambda i, j: (0, j)), pl.BlockSpec((1, tc), lambda i, j: (0, j))],
        out_specs=pl.BlockSpec((tl, tc), lambda i, j: (i, j)),
        out_shape=jax.ShapeDtypeStruct((T, C), F32),
        scratch_shapes=[pltpu.VMEM((tl + HALO, tc), F32)],
        compiler_params=_cparams("parallel", "parallel"),
    )(proj, proj, w, b)


def ssm_conv_bwd_a(proj, w, b, d_out, *, name, tile=512, tc=256):
    T = proj.shape[0]
    K, C = w.shape
    tl = min(T, tile)
    c0 = OFF_XBC // tc

    def body(x_ref, halo_ref, w_ref, b_ref, d_ref, o_ref, buf):
        _fill_prev(buf, x_ref, halo_ref, pl.program_id(0))
        u = _conv_from(buf, w_ref, b_ref, K, tl)
        s = _sigmoid(u)
        o_ref[...] = d_ref[...].astype(F32) * (s * (1.0 + u * (1.0 - s)))

    return pl.pallas_call(
        body, name=name, grid=(T // tl, C // tc),
        in_specs=[pl.BlockSpec((tl, tc), lambda i, j: (i, c0 + j)), _prev_halo_spec(tl, tc, lambda j: c0 + j),
                  pl.BlockSpec((K, tc), lambda i, j: (0, j)), pl.BlockSpec((1, tc), lambda i, j: (0, j)),
                  pl.BlockSpec((tl, tc), lambda i, j: (i, j))],
        out_specs=pl.BlockSpec((tl, tc), lambda i, j: (i, j)),
        out_shape=jax.ShapeDtypeStruct((T, C), F32),
        scratch_shapes=[pltpu.VMEM((tl + HALO, tc), F32)],
        compiler_params=_cparams("parallel", "parallel"),
    )(proj, proj, w, b, d_out)


def ffn_act(up, w, b, *, name, tile=512, tc=1408):
    T = up.shape[0]
    K = w.shape[0]
    tl = min(T, tile)
    nj = D_FF // tc

    def body(xg_ref, hg_ref, xv_ref, hv_ref, wg_ref, wv_ref, bg_ref, bv_ref, o_ref, bufg, bufv):
        i = pl.program_id(0)
        _fill_prev(bufg, xg_ref, hg_ref, i)
        _fill_prev(bufv, xv_ref, hv_ref, i)
        gate = _conv_from(bufg, wg_ref, bg_ref, K, tl)
        val = _conv_from(bufv, wv_ref, bv_ref, K, tl)
        o_ref[...] = (gate * _sigmoid(gate) * val).astype(o_ref.dtype)

    return pl.pallas_call(
        body, name=name, grid=(T // tl, nj),
        in_specs=[pl.BlockSpec((tl, tc), lambda i, j: (i, j)), _prev_halo_spec(tl, tc, lambda j: j),
                  pl.BlockSpec((tl, tc), lambda i, j: (i, nj + j)), _prev_halo_spec(tl, tc, lambda j: nj + j),
                  pl.BlockSpec((K, tc), lambda i, j: (0, j)), pl.BlockSpec((K, tc), lambda i, j: (0, nj + j)),
                  pl.BlockSpec((1, tc), lambda i, j: (0, j)), pl.BlockSpec((1, tc), lambda i, j: (0, nj + j))],
        out_specs=pl.BlockSpec((tl, tc), lambda i, j: (i, j)),
        out_shape=jax.ShapeDtypeStruct((T, D_FF), BF16),
        scratch_shapes=[pltpu.VMEM((tl + HALO, tc), F32), pltpu.VMEM((tl + HALO, tc), F32)],
        compiler_params=_cparams("parallel", "parallel"),
    )(up, up, up, up, w, w, b, b)


def ffn_bwd_fused(up, w, b, d_act, *, name, tile=512, tc=256):
    T = up.shape[0]
    K = w.shape[0]
    tl = min(T, tile)
    nj = D_FF // tc
    nblk = T // HALO
    ext = tl + HALO

    def body(xg, hgp, hgn, xv, hvp, hvn, wg, wv, bg, bv, d, dn, og, ov, dwg, dwv, dbg, dbv, bufg, bufv, dgb, dvb):
        i = pl.program_id(1)
        last = pl.num_programs(1) - 1

        def fill(buf, x_ref, prev_ref, next_ref):
            buf[0:HALO, :] = jnp.where(i > 0, prev_ref[...].astype(F32), 0.0)
            buf[HALO:HALO + tl, :] = x_ref[...].astype(F32)
            buf[HALO + tl:, :] = jnp.where(i < last, next_ref[...].astype(F32), 0.0)

        def conv_ext(buf, w_ref, b_ref):
            acc = b_ref[...].astype(F32) + jnp.zeros((ext, tc), F32)
            for k in range(K):
                acc = acc + buf[pl.ds(HALO - (K - 1 - k), ext), :] * w_ref[k:k + 1, :].astype(F32)
            return acc

        fill(bufg, xg, hgp, hgn)
        fill(bufv, xv, hvp, hvn)
        gate = conv_ext(bufg, wg, bg)
        val = conv_ext(bufv, wv, bv)
        dd = jnp.concatenate([d[...].astype(F32), jnp.where(i < last, dn[...].astype(F32)[0:HALO], 0.0)], axis=0)
        s = _sigmoid(gate)
        dgb[...] = dd * val * (s * (1.0 + gate * (1.0 - s)))
        dvb[...] = dd * (gate * s)

        @pl.when(i == 0)
        def _():
            for ref in (dwg, dwv, dbg, dbv):
                ref[...] = jnp.zeros_like(ref)

        for dbuf, xbuf, w_ref, o_ref, dw_ref, db_ref in ((dgb, bufg, wg, og, dwg, dbg), (dvb, bufv, wv, ov, dwv, dbv)):
            cur = dbuf[0:tl, :]
            dx = jnp.zeros((tl, tc), F32)
            for k in range(K):
                sft = K - 1 - k
                dx = dx + dbuf[pl.ds(sft, tl), :] * w_ref[k:k + 1, :].astype(F32)
                dw_ref[k:k + 1, :] += jnp.sum(cur * xbuf[pl.ds(HALO - sft, tl), :], axis=0, keepdims=True)
            db_ref[...] += jnp.sum(cur, axis=0, keepdims=True)
            o_ref[...] = dx.astype(o_ref.dtype)

    prev = lambda i: jnp.maximum(i * (tl // HALO) - 1, 0)
    nxt = lambda i: jnp.minimum((i + 1) * (tl // HALO), nblk - 1)

    def x_specs(col):
        return [pl.BlockSpec((tl, tc), lambda j, i: (i, col(j))), pl.BlockSpec((HALO, tc), lambda j, i: (prev(i), col(j))),
                pl.BlockSpec((HALO, tc), lambda j, i: (nxt(i), col(j)))]

    gcol, vcol = (lambda j: j), (lambda j: nj + j)
    in_specs = (x_specs(gcol) + x_specs(vcol)
                + [pl.BlockSpec((K, tc), lambda j, i: (0, j)), pl.BlockSpec((K, tc), lambda j, i: (0, nj + j)),
                   pl.BlockSpec((1, tc), lambda j, i: (0, j)), pl.BlockSpec((1, tc), lambda j, i: (0, nj + j)),
                   pl.BlockSpec((tl, tc), lambda j, i: (i, j)),
                   pl.BlockSpec((2 * HALO, tc), lambda j, i: (jnp.minimum((i + 1) * (tl // (2 * HALO)), nblk // 2 - 1), j))])
    row_out = pl.BlockSpec((tl, tc), lambda j, i: (i, j))
    w_out = pl.BlockSpec((K, tc), lambda j, i: (0, j))
    b_out = pl.BlockSpec((1, tc), lambda j, i: (0, j))
    return pl.pallas_call(
        body, name=name, grid=(nj, T // tl), in_specs=in_specs,
        out_specs=[row_out, row_out, w_out, w_out, b_out, b_out],
        out_shape=[jax.ShapeDtypeStruct((T, D_FF), BF16)] * 2 + [jax.ShapeDtypeStruct((K, D_FF), F32)] * 2
        + [jax.ShapeDtypeStruct((1, D_FF), F32)] * 2,
        scratch_shapes=[pltpu.VMEM((tl + 2 * HALO, tc), F32)] * 2 + [pltpu.VMEM((ext, tc), F32)] * 2,
        compiler_params=_cparams("parallel", "arbitrary"),
    )(up, up, up, up, up, up, w, w, b, b, d_act, d_act)


def conv_bwd_b(du, x, x_off, w, *, name, out_dtype, tile=512, tc=256):
    T, C = du.shape
    K = w.shape[0]
    tl = min(T, tile)
    c0 = x_off // tc
    nblk = T // HALO

    def body(du_ref, nx_ref, x_ref, halo_ref, w_ref, dx_ref, dw_ref, db_ref, dbuf, xbuf):
        i = pl.program_id(1)
        last = pl.num_programs(1) - 1
        d = du_ref[...].astype(F32)
        dbuf[0:tl, :] = d
        dbuf[tl:, :] = jnp.where(i < last, nx_ref[...].astype(F32), 0.0)
        _fill_prev(xbuf, x_ref, halo_ref, i)

        @pl.when(i == 0)
        def _():
            dw_ref[...] = jnp.zeros_like(dw_ref)
            db_ref[...] = jnp.zeros_like(db_ref)

        dx = jnp.zeros((tl, tc), F32)
        for k in range(K):
            s = K - 1 - k
            dx = dx + dbuf[pl.ds(s, tl), :] * w_ref[k:k + 1, :].astype(F32)
            dw_ref[k:k + 1, :] += jnp.sum(d * xbuf[pl.ds(HALO - s, tl), :], axis=0, keepdims=True)
        db_ref[...] += jnp.sum(d, axis=0, keepdims=True)
        dx_ref[...] = dx.astype(dx_ref.dtype)

    return pl.pallas_call(
        body, name=name, grid=(C // tc, T // tl),
        in_specs=[pl.BlockSpec((tl, tc), lambda j, i: (i, j)),
                  pl.BlockSpec((HALO, tc), lambda j, i: (jnp.minimum((i + 1) * (tl // HALO), nblk - 1), j)),
                  pl.BlockSpec((tl, tc), lambda j, i: (i, c0 + j)),
                  pl.BlockSpec((HALO, tc), lambda j, i: (jnp.maximum(i * (tl // HALO) - 1, 0), c0 + j)),
                  pl.BlockSpec((K, tc), lambda j, i: (0, j))],
        out_specs=[pl.BlockSpec((tl, tc), lambda j, i: (i, j)), pl.BlockSpec((K, tc), lambda j, i: (0, j)),
                   pl.BlockSpec((1, tc), lambda j, i: (0, j))],
        out_shape=[jax.ShapeDtypeStruct((T, C), out_dtype), jax.ShapeDtypeStruct((K, C), F32),
                   jax.ShapeDtypeStruct((1, C), F32)],
        scratch_shapes=[pltpu.VMEM((tl + HALO, tc), F32), pltpu.VMEM((tl + HALO, tc), F32)],
        compiler_params=_cparams("parallel", "arbitrary"),
    )(du, du, x, x, w)


def _attn_tiles(T):
    return min(T, 1024), min(T, 256)


def _after_diag(keys, queries, strict):
    d = lax.broadcasted_iota(jnp.int32, (keys, queries), 1) - lax.broadcasted_iota(jnp.int32, (keys, queries), 0)
    return d > 0 if strict else d >= 0


def _log_gates(z):
    l1p = jnp.log(1.0 + jnp.exp(-jnp.abs(z)))
    a = jnp.minimum(z, 0.0) - l1p
    return a, a - z


def _causal_sweep(i, tq, tk, block, descending):
    nb = tq // tk
    n_full = i * nb

    def band():
        order = reversed(range(nb)) if descending else range(nb)
        for bb in order:
            block(pl.multiple_of(i * tq + bb * tk, tk), bb * tk, True)

    def full():
        def step(j, c):
            kb = (n_full - 1 - j) if descending else j
            block(pl.multiple_of(kb * tk, tk), 0, False)
            return c
        lax.fori_loop(0, n_full, step, 0)

    if descending:
        band()
        full()
    else:
        full()
        band()


def sb_fwd(q, k, v, *, name):
    H, dh, T = q.shape
    tq, tk = _attn_tiles(T)

    def body(q_ref, k_ref, v_ref, y_ref, bt_ref, acc, run):
        acc[...] = jnp.zeros_like(acc)
        run[...] = jnp.zeros_like(run)
        u_after = _tri(tk, lambda r, c: r < c)

        def block(k0, r0, masked):
            kb = k_ref[pl.ds(k0, tk), :]
            vb = v_ref[pl.ds(k0, tk), :]
            z = jnp.dot(kb, q_ref[:, r0:], preferred_element_type=F32)
            a, b = _log_gates(z)
            if masked:
                valid = _after_diag(tk, tq - r0, True)
                b = jnp.where(valid, b, 0.0)
            w = jnp.exp(a + _tri_dot(u_after, b, 2) + run[:, r0:])
            if masked:
                w = jnp.where(valid, w, 0.0)
            acc[:, r0:] += lax.dot_general(vb, w.astype(BF16), TN, preferred_element_type=F32)
            run[:, r0:] += jnp.sum(b, axis=0, keepdims=True)

        _causal_sweep(pl.program_id(1), tq, tk, block, descending=True)
        y_ref[...] = acc[...]
        bt_ref[...] = run[...]

    return pl.pallas_call(
        body, name=name, grid=(H, T // tq),
        in_specs=[pl.BlockSpec((None, dh, tq), lambda h, i: (h, 0, i)),
                  pl.BlockSpec((None, T, dh), lambda h, i: (h, 0, 0)),
                  pl.BlockSpec((None, T, dh), lambda h, i: (h, 0, 0))],
        out_specs=[pl.BlockSpec((None, dh, tq), lambda h, i: (h, 0, i)),
                   pl.BlockSpec((None, 1, tq), lambda h, i: (h, 0, i))],
        out_shape=[jax.ShapeDtypeStruct((H, dh, T), F32), jax.ShapeDtypeStruct((H, 1, T), F32)],
        scratch_shapes=[pltpu.VMEM((dh, tq), F32), pltpu.VMEM((1, tq), F32)],
        compiler_params=_cparams("parallel", "parallel"),
    )(q, k, v)


def sb_bwd(q, k, v, dy, btot, *, name, q_scale):
    H, dh, T = q.shape
    tq, tk = _attn_tiles(T)

    def body(q_ref, k_ref, v_ref, dy_ref, bt_ref, dq_ref, dk_ref, dv_ref, dq, pb, pg, dyb):
        @pl.when(pl.program_id(1) == 0)
        def _():
            dk_ref[...] = jnp.zeros_like(dk_ref)
            dv_ref[...] = jnp.zeros_like(dv_ref)

        dq[...] = jnp.zeros_like(dq)
        pb[...] = jnp.zeros_like(pb)
        pg[...] = jnp.zeros_like(pg)
        dyb[...] = dy_ref[...].astype(BF16)
        u_upto = _tri(tk, lambda r, c: r >= c)
        u_before = _tri(tk, lambda r, c: r > c)

        def block(k0, r0, masked):
            kb = k_ref[pl.ds(k0, tk), :]
            vb = v_ref[pl.ds(k0, tk), :]
            qv = q_ref[:, r0:]
            dyv = dyb[:, r0:]
            z = jnp.dot(kb, qv, preferred_element_type=F32)
            a, b = _log_gates(z)
            if masked:
                valid = _after_diag(tk, tq - r0, True)
                b = jnp.where(valid, b, 0.0)
            w = jnp.exp(a + (bt_ref[:, r0:] - pb[:, r0:] - _tri_dot(u_upto, b, 2)))
            if masked:
                w = jnp.where(valid, w, 0.0)
            g = w * jnp.dot(vb, dyv, preferred_element_type=F32)
            dz = g - jnp.exp(a) * (g + pg[:, r0:] + _tri_dot(u_before, g, 2))
            if masked:
                dz = jnp.where(valid, dz, 0.0)
            dz = dz.astype(BF16)
            dq[:, r0:] += lax.dot_general(kb, dz, TN, preferred_element_type=F32)
            dk_ref[pl.ds(k0, tk), :] += lax.dot_general(dz, qv, NT, preferred_element_type=F32)
            dv_ref[pl.ds(k0, tk), :] += lax.dot_general(w.astype(BF16), dyv, NT, preferred_element_type=F32)
            pb[:, r0:] += jnp.sum(b, axis=0, keepdims=True)
            pg[:, r0:] += jnp.sum(g, axis=0, keepdims=True)

        _causal_sweep(pl.program_id(1), tq, tk, block, descending=False)
        dq_ref[...] = dq[...] * q_scale

    return pl.pallas_call(
        body, name=name, grid=(H, T // tq),
        in_specs=[pl.BlockSpec((None, dh, tq), lambda h, i: (h, 0, i)),
                  pl.BlockSpec((None, T, dh), lambda h, i: (h, 0, 0)),
                  pl.BlockSpec((None, T, dh), lambda h, i: (h, 0, 0)),
                  pl.BlockSpec((None, dh, tq), lambda h, i: (h, 0, i)),
                  pl.BlockSpec((None, 1, tq), lambda h, i: (h, 0, i))],
        out_specs=[pl.BlockSpec((None, dh, tq), lambda h, i: (h, 0, i)),
                   pl.BlockSpec((None, T, dh), lambda h, i: (h, 0, 0)),
                   pl.BlockSpec((None, T, dh), lambda h, i: (h, 0, 0))],
        out_shape=[jax.ShapeDtypeStruct((H, dh, T), F32), jax.ShapeDtypeStruct((H, T, dh), F32),
                   jax.ShapeDtypeStruct((H, T, dh), F32)],
        scratch_shapes=[pltpu.VMEM((dh, tq), F32), pltpu.VMEM((1, tq), F32), pltpu.VMEM((1, tq), F32),
                        pltpu.VMEM((dh, tq), BF16)],
        compiler_params=_cparams("parallel", "arbitrary"),
    )(q, k, v, dy, btot)


NEG = -1e30


def mla_fwd(q, k, v, *, name):
    H, dk, T = q.shape
    dv = v.shape[2]
    tq, tk = _attn_tiles(T)

    def body(q_ref, k_ref, v_ref, o_ref, l_ref, acc, m_s, l_s):
        acc[...] = jnp.zeros_like(acc)
        m_s[...] = jnp.full_like(m_s, NEG)
        l_s[...] = jnp.zeros_like(l_s)

        def block(k0, r0, masked):
            kb = k_ref[pl.ds(k0, tk), :]
            vb = v_ref[pl.ds(k0, tk), :]
            s = jnp.dot(kb, q_ref[:, r0:], preferred_element_type=F32)
            if masked:
                s = jnp.where(_after_diag(tk, tq - r0, False), s, NEG)
            m = m_s[:, r0:]
            m_new = jnp.maximum(m, jnp.max(s, axis=0, keepdims=True))
            p = jnp.exp(s - m_new)
            alpha = jnp.exp(m - m_new)
            l_s[:, r0:] = alpha * l_s[:, r0:] + jnp.sum(p, axis=0, keepdims=True)
            acc[:, r0:] = alpha * acc[:, r0:] + lax.dot_general(vb, p.astype(BF16), TN, preferred_element_type=F32)
            m_s[:, r0:] = m_new

        _causal_sweep(pl.program_id(1), tq, tk, block, descending=False)
        o_ref[...] = acc[...] / l_s[...]
        l_ref[...] = m_s[...] + jnp.log(l_s[...])

    return pl.pallas_call(
        body, name=name, grid=(H, T // tq),
        in_specs=[pl.BlockSpec((None, dk, tq), lambda h, i: (h, 0, i)),
                  pl.BlockSpec((None, T, dk), lambda h, i: (h, 0, 0)),
                  pl.BlockSpec((None, T, dv), lambda h, i: (h, 0, 0))],
        out_specs=[pl.BlockSpec((None, dv, tq), lambda h, i: (h, 0, i)),
                   pl.BlockSpec((None, 1, tq), lambda h, i: (h, 0, i))],
        out_shape=[jax.ShapeDtypeStruct((H, dv, T), F32), jax.ShapeDtypeStruct((H, 1, T), F32)],
        scratch_shapes=[pltpu.VMEM((dv, tq), F32), pltpu.VMEM((1, tq), F32), pltpu.VMEM((1, tq), F32)],
        compiler_params=_cparams("parallel", "parallel"),
    )(q, k, v)


def mla_bwd(q, k, v, do, o, lse, *, name):
    H, dk, T = q.shape
    dv = v.shape[2]
    tq, tk = _attn_tiles(T)

    def body(q_ref, k_ref, v_ref, do_ref, o_ref, l_ref, dq_ref, dk_ref, dv_ref, dq, delta, dob):
        @pl.when(pl.program_id(1) == 0)
        def _():
            dk_ref[...] = jnp.zeros_like(dk_ref)
            dv_ref[...] = jnp.zeros_like(dv_ref)

        dq[...] = jnp.zeros_like(dq)
        dov = do_ref[...].astype(F32)
        dob[...] = dov.astype(BF16)
        delta[...] = jnp.sum(dov * o_ref[...], axis=0, keepdims=True)

        def block(k0, r0, masked):
            kb = k_ref[pl.ds(k0, tk), :]
            vb = v_ref[pl.ds(k0, tk), :]
            qv = q_ref[:, r0:]
            dov_b = dob[:, r0:]
            s = jnp.dot(kb, qv, preferred_element_type=F32)
            p = jnp.exp(s - l_ref[:, r0:])
            if masked:
                p = jnp.where(_after_diag(tk, tq - r0, False), p, 0.0)
            dp = jnp.dot(vb, dov_b, preferred_element_type=F32)
            ds = (p * (dp - delta[:, r0:])).astype(BF16)
            dq[:, r0:] += lax.dot_general(kb, ds, TN, preferred_element_type=F32)
            dk_ref[pl.ds(k0, tk), :] += lax.dot_general(ds, qv, NT, preferred_element_type=F32)
            dv_ref[pl.ds(k0, tk), :] += lax.dot_general(p.astype(BF16), dov_b, NT, preferred_element_type=F32)

        _causal_sweep(pl.program_id(1), tq, tk, block, descending=False)
        dq_ref[...] = dq[...]

    return pl.pallas_call(
        body, name=name, grid=(H, T // tq),
        in_specs=[pl.BlockSpec((None, dk, tq), lambda h, i: (h, 0, i)),
                  pl.BlockSpec((None, T, dk), lambda h, i: (h, 0, 0)),
                  pl.BlockSpec((None, T, dv), lambda h, i: (h, 0, 0)),
                  pl.BlockSpec((None, dv, tq), lambda h, i: (h, 0, i)),
                  pl.BlockSpec((None, dv, tq), lambda h, i: (h, 0, i)),
                  pl.BlockSpec((None, 1, tq), lambda h, i: (h, 0, i))],
        out_specs=[pl.BlockSpec((None, dk, tq), lambda h, i: (h, 0, i)),
                   pl.BlockSpec((None, T, dk), lambda h, i: (h, 0, 0)),
                   pl.BlockSpec((None, T, dv), lambda h, i: (h, 0, 0))],
        out_shape=[jax.ShapeDtypeStruct((H, dk, T), F32), jax.ShapeDtypeStruct((H, T, dk), F32),
                   jax.ShapeDtypeStruct((H, T, dv), F32)],
        scratch_shapes=[pltpu.VMEM((dk, tq), F32), pltpu.VMEM((1, tq), F32), pltpu.VMEM((dv, tq), BF16)],
        compiler_params=_cparams("parallel", "arbitrary"),
    )(q, k, v, do, o, lse)


def _lane_pick(x, h):
    lane = lax.broadcasted_iota(jnp.int32, (1, x.shape[1]), 1)
    return jnp.sum(jnp.where(lane == h, x, 0.0), axis=1, keepdims=True)


def _row_pick(x, h):
    sub = lax.broadcasted_iota(jnp.int32, (x.shape[0], 1), 0)
    return jnp.sum(jnp.where(sub == h, x, 0.0), axis=0, keepdims=True)


def ssd_chunk_fn(*args):
    nh, ng = SSM_HEADS, SSM_GROUPS
    xs = args[:nh]
    bs = args[nh:nh + ng]
    cs = args[nh + ng:nh + 2 * ng]
    dt_raw = args[nh + 2 * ng]
    st = args[nh + 2 * ng + 1:nh + 2 * ng + 1 + nh]
    dt_bias, a_log, d_skip = args[nh + 2 * ng + 1 + nh:]
    L = dt_raw.shape[0]
    dt = _softplus(dt_raw + dt_bias)
    da = dt * (-jnp.exp(a_log))
    dcs = csum_rows(da)
    dcs_t = dcs.T
    total = jnp.sum(da, axis=0, keepdims=True)
    causal = lax.broadcasted_iota(jnp.int32, (L, L), 0) >= lax.broadcasted_iota(jnp.int32, (L, L), 1)
    cb = [mm_nt(cs[g], bs[g]) for g in range(ng)]
    ys, new_st = [], []
    for h in range(nh):
        g = h // (nh // ng)
        dcs_h = _lane_pick(dcs, h)
        dt_h = _lane_pick(dt, h)
        tot_h = _lane_pick(total, h)
        dsk_h = _lane_pick(d_skip, h)
        decay = jnp.exp(jnp.where(causal, dcs_h - _row_pick(dcs_t, h), NEG))
        xdt = xs[h] * dt_h
        y = mm_nn(cb[g] * decay, xdt)
        y = y + mm_nn(cs[g] * jnp.exp(dcs_h), st[h])
        ys.append(y + xs[h] * dsk_h)
        new_st.append(st[h] * jnp.exp(tot_h) + mm_tn(bs[g] * jnp.exp(tot_h - dcs_h), xdt))
    return tuple(ys) + tuple(new_st)


def ssd_fwd(x_hm, b_hm, c_hm, proj, dt_bias, a_log, d_skip, *, name):
    nh, T, P = x_hm.shape
    ng, N = b_hm.shape[0], b_hm.shape[2]
    L = SSM_CHUNK
    nc = T // L
    dtb = OFF_DT // LANES

    def body(x_ref, b_ref, c_ref, dt_ref, db_ref, al_ref, ds_ref, y_ref, s_ref, state):
        @pl.when(pl.program_id(0) == 0)
        def _():
            state[...] = jnp.zeros_like(state)

        s_ref[...] = state[...]
        args = ([x_ref[h] for h in range(nh)] + [b_ref[g] for g in range(ng)] + [c_ref[g] for g in range(ng)]
                + [dt_ref[...]] + [state[h] for h in range(nh)] + [db_ref[...], al_ref[...], ds_ref[...]])
        res = ssd_chunk_fn(*args)
        for h in range(nh):
            y_ref[h] = res[h]
            state[h] = res[nh + h]

    par = pl.BlockSpec((1, LANES), lambda i: (0, 0))
    return pl.pallas_call(
        body, name=name, grid=(nc,),
        in_specs=[pl.BlockSpec((nh, L, P), lambda i: (0, i, 0)), pl.BlockSpec((ng, L, N), lambda i: (0, i, 0)),
                  pl.BlockSpec((ng, L, N), lambda i: (0, i, 0)), pl.BlockSpec((L, LANES), lambda i: (i, dtb)),
                  par, par, par],
        out_specs=[pl.BlockSpec((nh, L, P), lambda i: (0, i, 0)),
                   pl.BlockSpec((None, nh, N, P), lambda i: (i, 0, 0, 0))],
        out_shape=[jax.ShapeDtypeStruct((nh, T, P), F32), jax.ShapeDtypeStruct((nc, nh, N, P), F32)],
        scratch_shapes=[pltpu.VMEM((nh, N, P), F32)],
        compiler_params=_cparams("arbitrary"),
    )(x_hm, b_hm, c_hm, proj, dt_bias, a_log, d_skip)


def ssd_bwd(x_hm, b_hm, c_hm, proj, states, dt_bias, a_log, d_skip, dy_hm, *, name):
    nh, T, P = x_hm.shape
    ng, N = b_hm.shape[0], b_hm.shape[2]
    L = SSM_CHUNK
    nc = T // L
    dtb = OFF_DT // LANES

    def body(x_ref, b_ref, c_ref, dt_ref, s_ref, db_ref, al_ref, ds_ref, dy_ref,
             dx_ref, dbm_ref, dcm_ref, ddt_ref, gdb_ref, gal_ref, gds_ref, dstate):
        @pl.when(pl.program_id(0) == 0)
        def _():
            dstate[...] = jnp.zeros_like(dstate)
            gdb_ref[...] = jnp.zeros_like(gdb_ref)
            gal_ref[...] = jnp.zeros_like(gal_ref)
            gds_ref[...] = jnp.zeros_like(gds_ref)

        args = ([x_ref[h] for h in range(nh)] + [b_ref[g] for g in range(ng)] + [c_ref[g] for g in range(ng)]
                + [dt_ref[...]] + [s_ref[h] for h in range(nh)] + [db_ref[...], al_ref[...], ds_ref[...]])
        _, vjp = jax.vjp(ssd_chunk_fn, *args)
        g = vjp(tuple([dy_ref[h] for h in range(nh)] + [dstate[h] for h in range(nh)]))
        for h in range(nh):
            dx_ref[h] = g[h]
        for gi in range(ng):
            dbm_ref[gi] = g[nh + gi]
            dcm_ref[gi] = g[nh + ng + gi]
        ddt_ref[...] = g[nh + 2 * ng]
        for h in range(nh):
            dstate[h] = g[nh + 2 * ng + 1 + h]
        gdb_ref[...] += g[-3]
        gal_ref[...] += g[-2]
        gds_ref[...] += g[-1]

    rev = lambda i: nc - 1 - i
    par = pl.BlockSpec((1, LANES), lambda i: (0, 0))
    return pl.pallas_call(
        body, name=name, grid=(nc,),
        in_specs=[pl.BlockSpec((nh, L, P), lambda i: (0, rev(i), 0)), pl.BlockSpec((ng, L, N), lambda i: (0, rev(i), 0)),
                  pl.BlockSpec((ng, L, N), lambda i: (0, rev(i), 0)), pl.BlockSpec((L, LANES), lambda i: (rev(i), dtb)),
                  pl.BlockSpec((None, nh, N, P), lambda i: (rev(i), 0, 0, 0)), par, par, par,
                  pl.BlockSpec((nh, L, P), lambda i: (0, rev(i), 0))],
        out_specs=[pl.BlockSpec((nh, L, P), lambda i: (0, rev(i), 0)), pl.BlockSpec((ng, L, N), lambda i: (0, rev(i), 0)),
                   pl.BlockSpec((ng, L, N), lambda i: (0, rev(i), 0)), pl.BlockSpec((L, LANES), lambda i: (rev(i), 0)),
                   par, par, par],
        out_shape=[jax.ShapeDtypeStruct((nh, T, P), F32), jax.ShapeDtypeStruct((ng, T, N), F32),
                   jax.ShapeDtypeStruct((ng, T, N), F32), jax.ShapeDtypeStruct((T, LANES), F32),
                   jax.ShapeDtypeStruct((1, LANES), F32), jax.ShapeDtypeStruct((1, LANES), F32),
                   jax.ShapeDtypeStruct((1, LANES), F32)],
        scratch_shapes=[pltpu.VMEM((nh, N, P), F32)],
        compiler_params=_cparams("arbitrary"),
    )(x_hm, b_hm, c_hm, proj, states, dt_bias, a_log, d_skip, dy_hm)


def loss_head(h, target, g, *, name, tile=512):
    T, C = h.shape
    tl = min(T, tile)

    def body(h_ref, t_ref, g_ref, dh_ref, dg_ref, ls_ref):
        @pl.when(pl.program_id(0) == 0)
        def _():
            dg_ref[...] = jnp.zeros_like(dg_ref)
            ls_ref[...] = jnp.zeros_like(ls_ref)

        (y,), vjp = jax.vjp(rms_fn, h_ref[...], g_ref[...])
        err = y - t_ref[...]
        ls_ref[...] += jnp.sum(err * err, axis=0, keepdims=True) * (0.5 / C)
        dh, dg = vjp((err * (1.0 / C),))
        dh_ref[...] = dh
        dg_ref[...] += dg

    row = pl.BlockSpec((tl, C), lambda i: (i, 0))
    par = pl.BlockSpec((1, C), lambda i: (0, 0))
    return pl.pallas_call(
        body, name=name, grid=(T // tl,), in_specs=[row, row, par], out_specs=[row, par, par],
        out_shape=[jax.ShapeDtypeStruct((T, C), F32), jax.ShapeDtypeStruct((1, C), F32),
                   jax.ShapeDtypeStruct((1, C), F32)],
        compiler_params=_cparams("arbitrary"),
    )(h, target, g)


def adamw(w, g, m, v, *, name):
    R, C = w.shape
    tr = R
    for d in range(8, min(R, 512) + 1, 8):
        if R % d == 0:
            tr = d
    c1 = 1.0 - ADAM_B1 ** ADAM_STEP
    c2 = 1.0 - ADAM_B2 ** ADAM_STEP

    def body(w_ref, g_ref, m_ref, v_ref, d_ref, nm_ref, nv_ref):
        gv = g_ref[...]
        nm = ADAM_B1 * m_ref[...] + (1.0 - ADAM_B1) * gv
        nv = ADAM_B2 * v_ref[...] + (1.0 - ADAM_B2) * (gv * gv)
        d_ref[...] = -ADAM_LR * ((nm / c1) / (jnp.sqrt(nv / c2) + ADAM_EPS) + ADAM_WD * w_ref[...])
        nm_ref[...] = nm
        nv_ref[...] = nv

    spec = pl.BlockSpec((tr, C), lambda i: (i, 0))
    return pl.pallas_call(
        body, name=name, grid=(R // tr,), in_specs=[spec] * 4, out_specs=[spec] * 3,
        out_shape=[jax.ShapeDtypeStruct((R, C), F32)] * 3,
        compiler_params=_cparams("parallel"),
    )(w, g, m, v)


MESH = pl.DeviceIdType.MESH
HBM_SPEC = pl.BlockSpec(memory_space=pltpu.HBM)


def _place():
    return lax.axis_index("x"), lax.axis_index("y"), lax.axis_index("c")


def allgather_blocks(mine, *, name):
    R = mine.shape[0]

    def body(x_ref, out_ref, send_sems, recv_sems, local_sem):
        x, y, c = _place()
        me, sibling = (x, y, c), (x, y, 1 - c)
        chips = [(1 - x, y), (x, 1 - y), (1 - x, 1 - y)]

        def slot(px, py, pc):
            return out_ref.at[4 * px + 2 * py + pc]

        def copy(k, block, to, src=None):
            return pltpu.make_async_remote_copy(
                src_ref=slot(*block) if src is None else src, dst_ref=slot(*block),
                send_sem=send_sems.at[k], recv_sem=recv_sems.at[k], device_id=to, device_id_type=MESH)

        own = pltpu.make_async_copy(x_ref, slot(*me), local_sem)
        own.start()
        first = [copy(0, me, sibling, src=x_ref)]
        first += [copy(1 + j, me, (*chip, c), src=x_ref) for j, chip in enumerate(chips)]
        for cp in first:
            cp.start()
        passed = [copy(4 + j, (*chip, c), sibling) for j, chip in enumerate(chips)]
        for j, chip in enumerate(chips):
            copy(1 + j, (*chip, c), me).wait_recv()
            passed[j].start()
        copy(0, sibling, me).wait_recv()
        for j, chip in enumerate(chips):
            copy(4 + j, (*chip, 1 - c), me).wait_recv()
        for cp in first + passed:
            cp.wait_send()
        own.wait()

    return pl.pallas_call(
        body, name=name, out_shape=jax.ShapeDtypeStruct((8, R, LANES), mine.dtype),
        in_specs=[HBM_SPEC], out_specs=HBM_SPEC,
        scratch_shapes=[pltpu.SemaphoreType.DMA((7,)), pltpu.SemaphoreType.DMA((7,)), pltpu.SemaphoreType.DMA],
    )(mine)


def allgather_direct(mine, *, name):
    R = mine.shape[0]

    def body(x_ref, out_ref, send_sems, recv_sems, local_sem):
        x, y, c = _place()
        own = pltpu.make_async_copy(x_ref, out_ref.at[4 * x + 2 * y + c], local_sem)
        own.start()
        sends = []
        for f in range(1, 8):
            fx, fy, fc = (f >> 2) & 1, (f >> 1) & 1, f & 1
            px, py, pc = jnp.where(fx, 1 - x, x), jnp.where(fy, 1 - y, y), jnp.where(fc, 1 - c, c)
            sends.append(pltpu.make_async_remote_copy(
                src_ref=x_ref, dst_ref=out_ref.at[4 * x + 2 * y + c], send_sem=send_sems.at[f - 1],
                recv_sem=recv_sems.at[f - 1], device_id=(px, py, pc), device_id_type=MESH))
        for cp in sends:
            cp.start()
        for f in range(1, 8):
            fx, fy, fc = (f >> 2) & 1, (f >> 1) & 1, f & 1
            px, py, pc = jnp.where(fx, 1 - x, x), jnp.where(fy, 1 - y, y), jnp.where(fc, 1 - c, c)
            pltpu.make_async_remote_copy(
                src_ref=x_ref, dst_ref=out_ref.at[4 * px + 2 * py + pc], send_sem=send_sems.at[f - 1],
                recv_sem=recv_sems.at[f - 1], device_id=(px, py, pc), device_id_type=MESH).wait_recv()
        for cp in sends:
            cp.wait_send()
        own.wait()

    return pl.pallas_call(
        body, name=name, out_shape=jax.ShapeDtypeStruct((8, R, LANES), mine.dtype),
        in_specs=[HBM_SPEC], out_specs=HBM_SPEC,
        scratch_shapes=[pltpu.SemaphoreType.DMA((7,)), pltpu.SemaphoreType.DMA((7,)), pltpu.SemaphoreType.DMA],
    )(mine)


def send_to_sibling(v, *, name):
    def body(v_ref, out_ref, send_sem, recv_sem):
        x, y, c = _place()
        cp = pltpu.make_async_remote_copy(src_ref=v_ref, dst_ref=out_ref, send_sem=send_sem, recv_sem=recv_sem,
                                          device_id=(x, y, 1 - c), device_id_type=MESH)
        cp.start()
        cp.wait()

    return pl.pallas_call(
        body, name=name, out_shape=jax.ShapeDtypeStruct(v.shape, v.dtype), in_specs=[HBM_SPEC], out_specs=HBM_SPEC,
        scratch_shapes=[pltpu.SemaphoreType.DMA, pltpu.SemaphoreType.DMA],
    )(v)


def pair_gather(v, *, name):
    def body(v_ref, out_ref, send_sem, recv_sem, local_sem):
        x, y, c = _place()
        for mine in range(2):
            @pl.when(c == mine)
            def _():
                own = pltpu.make_async_copy(v_ref, out_ref.at[mine], local_sem)
                own.start()
                cp = pltpu.make_async_remote_copy(src_ref=v_ref, dst_ref=out_ref.at[mine], send_sem=send_sem,
                                                  recv_sem=recv_sem, device_id=(x, y, 1 - mine), device_id_type=MESH)
                cp.start()
                pltpu.make_async_remote_copy(src_ref=v_ref, dst_ref=out_ref.at[1 - mine], send_sem=send_sem,
                                             recv_sem=recv_sem, device_id=(x, y, mine), device_id_type=MESH).wait_recv()
                cp.wait_send()
                own.wait()

    return pl.pallas_call(
        body, name=name, out_shape=jax.ShapeDtypeStruct((2,) + v.shape, v.dtype), in_specs=[HBM_SPEC],
        out_specs=HBM_SPEC,
        scratch_shapes=[pltpu.SemaphoreType.DMA, pltpu.SemaphoreType.DMA, pltpu.SemaphoreType.DMA],
    )(v)


def chip_exchange(p, *, name):
    R = p.shape[1]

    def body(p_ref, out_ref, send_sems, recv_sems):
        x, y, c = _place()
        chips = [(1 - x, y), (x, 1 - y), (1 - x, 1 - y)]
        sends = [pltpu.make_async_remote_copy(
            src_ref=p_ref.at[2 * px + py], dst_ref=out_ref.at[j], send_sem=send_sems.at[j], recv_sem=recv_sems.at[j],
            device_id=(px, py, c), device_id_type=MESH) for j, (px, py) in enumerate(chips)]
        for cp in sends:
            cp.start()
        for cp in sends:
            cp.wait()

    return pl.pallas_call(
        body, name=name, out_shape=jax.ShapeDtypeStruct((3, R, LANES), p.dtype), in_specs=[HBM_SPEC],
        out_specs=HBM_SPEC,
        scratch_shapes=[pltpu.SemaphoreType.DMA((3,)), pltpu.SemaphoreType.DMA((3,))],
    )(p)


def add_blocks(terms, out_dtype, *, name, tile=1024):
    R = terms[0].shape[0]
    tr = R
    for d in range(16, min(R, tile) + 1, 16):
        if R % d == 0:
            tr = d

    def body(*refs):
        acc = refs[0][...].astype(F32)
        for ref in refs[1:-1]:
            acc = acc + ref[...].astype(F32)
        refs[-1][...] = acc.astype(out_dtype)

    spec = pl.BlockSpec((tr, LANES), lambda i: (i, 0))
    return pl.pallas_call(
        body, name=name, grid=(R // tr,), in_specs=[spec] * len(terms), out_specs=spec,
        out_shape=jax.ShapeDtypeStruct((R, LANES), out_dtype), compiler_params=_cparams("parallel"),
    )(*terms)


def _half_rows(arr, cc):
    hr = arr.shape[1] // 2
    return lax.dynamic_slice_in_dim(arr, cc * hr, hr, axis=1).reshape(-1)


def _flat_half(shards, cc, dtype):
    flat = jnp.concatenate([_half_rows(shards[n], cc).astype(dtype) for n in BIG])
    return flat.reshape(-1, LANES)


def _unflat_halves(flat_by_c, shapes):
    out, off = {}, 0
    for n in BIG:
        _, R, C = shapes[n]
        sz = 2 * (R // 2) * C
        out[n] = jnp.concatenate([flat_by_c[c][off:off + sz].reshape(2, R // 2, C) for c in range(2)], axis=1)
        off += sz
    return out


def _to_heads(a, nh):
    T = a.shape[0]
    return a.reshape(T, nh, a.shape[1] // nh).transpose(1, 0, 2)


def _from_heads(a):
    nh, T, d = a.shape
    return a.transpose(1, 0, 2).reshape(T, nh * d)


def _to_heads_t(a, nh):
    T = a.shape[0]
    return a.reshape(T, nh, a.shape[1] // nh).transpose(1, 2, 0)


def _from_heads_t(a):
    nh, d, T = a.shape
    return a.transpose(2, 0, 1).reshape(T, nh * d)


def _pad_cols(a, n):
    return jnp.pad(a, ((0, 0), (0, n - a.shape[1])))


def _pack_w_in(w):
    offs = [sum(IN_SPLITS[:i]) for i in range(len(IN_SPLITS) + 1)]
    sb, z, xbc, dt, cq, ckv, kr = [w[:, offs[i]:offs[i + 1]] for i in range(len(IN_SPLITS))]
    zeros = lambda n: jnp.zeros((w.shape[0], n), w.dtype)
    h = MLA_ROPE // 2
    kra = jnp.concatenate([zeros(MLA_NOPE), kr, zeros(LANES - MLA_QK)], axis=1)
    krb = jnp.concatenate([zeros(MLA_NOPE), -kr[:, h:], kr[:, :h], zeros(LANES - MLA_QK)], axis=1)
    return jnp.concatenate([sb, xbc, z, cq, ckv, _pad_cols(dt, LANES), kra, krb], axis=1)


def _unpack_gw_in(g):
    h = MLA_ROPE // 2
    ga, gb = g[:, OFF_KRA:OFF_KRA + LANES], g[:, OFF_KRB:OFF_KRB + LANES]
    gkr = ga[:, MLA_NOPE:MLA_QK] + jnp.concatenate([gb[:, MLA_NOPE + h:MLA_QK], -gb[:, MLA_NOPE:MLA_NOPE + h]], axis=1)
    return jnp.concatenate([g[:, OFF_SB:OFF_SB + 768], g[:, OFF_Z:OFF_Z + 512], g[:, OFF_XBC:OFF_XBC + 768],
                            g[:, OFF_DT:OFF_DT + 8], g[:, OFF_CQ:OFF_CQ + 256], g[:, OFF_CKV:OFF_CKV + 128], gkr], axis=1)


def _pack_w_uq(w):
    zeros = lambda n: jnp.zeros((w.shape[0], n), w.dtype)
    h = MLA_ROPE // 2
    pp, rr = [], []
    for i in range(MLA_HEADS):
        nope = w[:, MLA_QK * i:MLA_QK * i + MLA_NOPE]
        rope = w[:, MLA_QK * i + MLA_NOPE:MLA_QK * (i + 1)]
        pp += [nope, rope, zeros(LANES - MLA_QK)]
        rr += [zeros(MLA_NOPE), -rope[:, h:], rope[:, :h], zeros(LANES - MLA_QK)]
    return jnp.concatenate(pp, axis=1), jnp.concatenate(rr, axis=1)


def _unpack_gw_uq(gp, gr):
    h = MLA_ROPE // 2
    out = []
    for i in range(MLA_HEADS):
        b = LANES * i
        out.append(gp[:, b:b + MLA_NOPE])
        out.append(gp[:, b + MLA_NOPE:b + MLA_NOPE + h] + gr[:, b + MLA_NOPE + h:b + MLA_QK])
        out.append(gp[:, b + MLA_NOPE + h:b + MLA_QK] - gr[:, b + MLA_NOPE:b + MLA_NOPE + h])
    return jnp.concatenate(out, axis=1)


def _pack_w_ukv(w):
    zeros = lambda n: jnp.zeros((w.shape[0], n), w.dtype)
    kk, vv = [], []
    for i in range(MLA_HEADS):
        b = (MLA_NOPE + MLA_V) * i
        kk += [w[:, b:b + MLA_NOPE], zeros(LANES - MLA_NOPE)]
        vv.append(w[:, b + MLA_NOPE:b + MLA_NOPE + MLA_V])
    return jnp.concatenate(kk, axis=1), jnp.concatenate(vv, axis=1)


def _unpack_gw_ukv(gk, gv):
    out = []
    for i in range(MLA_HEADS):
        out += [gk[:, LANES * i:LANES * i + MLA_NOPE], gv[:, MLA_V * i:MLA_V * (i + 1)]]
    return jnp.concatenate(out, axis=1)


def _rope_tables(positions):
    inv_freq = 1.0 / (ROPE_THETA ** (jnp.arange(0, MLA_ROPE, 2, dtype=F32) / MLA_ROPE))
    ang = positions.astype(F32)[:, None] * inv_freq
    cos, sin = jnp.cos(ang), jnp.sin(ang)
    T = positions.shape[0]
    one, zero = jnp.ones((T, MLA_NOPE), F32), jnp.zeros((T, MLA_NOPE), F32)
    pad1, pad0 = jnp.ones((T, LANES - MLA_QK), F32), jnp.zeros((T, LANES - MLA_QK), F32)
    return jnp.concatenate([one, cos, cos, pad1], axis=1), jnp.concatenate([zero, sin, sin, pad0], axis=1)


def _row(v):
    return v.reshape(1, -1)


def _pad_row(v):
    return _pad_cols(v.reshape(1, -1), LANES)


def _layer_weights(full, small, li):
    p = {}
    p["w_in_p"] = _pack_w_in(full["w_in"][li])
    p["w_in_pt"] = p["w_in_p"].T
    p["wqp"], p["wqr"] = _pack_w_uq(full["mla_w_uq"][li])
    p["wkp"], p["wvp"] = _pack_w_ukv(full["mla_w_ukv"][li])
    p["w_out"] = full["w_out"][li]
    p["w_out_t"] = p["w_out"].T
    p["w_up"] = full["ffn_w_up"][li]
    p["w_up_t"] = p["w_up"].T
    p["w_down"] = full["ffn_w_down"][li]
    p["w_down_t"] = p["w_down"].T
    for n in ("mix_norm", "sb_out_norm", "ssm_conv_b", "ssm_out_norm", "mla_q_norm", "mla_kv_norm", "mla_out_norm",
              "ffn_norm", "ffn_conv_b"):
        p[n] = _row(small[n][li])
    for n in ("ssm_dt_bias", "ssm_a_log", "ssm_d"):
        p[n] = _pad_row(small[n][li])
    p["ssm_conv_w"] = small["ssm_conv_w"][li]
    p["ffn_conv_w"] = small["ffn_conv_w"][li]
    return p


def _layer_fwd(h, p, cos, sin, li):
    T = h.shape[0]
    nm = lambda s: "l%d_%s" % (li, s)
    s = {"h": h}
    (n1,) = rowwise(rms_fn, [h], [p["mix_norm"]], [(D_MODEL, BF16)], name=nm("mix_norm"))
    proj = matmul(n1, p["w_in_p"], name=nm("in_proj"))
    s["n1"], s["proj"] = n1, proj
    qkv = proj[:, OFF_SB:OFF_SB + 768]
    s["sb_q"] = _to_heads_t((qkv[:, 0:256] * (SB_DIM ** -0.5)).astype(BF16), SB_HEADS)
    s["sb_k"] = _to_heads(qkv[:, 256:512].astype(BF16), SB_HEADS)
    s["sb_v"] = _to_heads(qkv[:, 512:768].astype(BF16), SB_HEADS)
    y_sb_hm, s["sb_bt"] = sb_fwd(s["sb_q"], s["sb_k"], s["sb_v"], name=nm("sb_fwd"))
    s["y_sb"] = _from_heads_t(y_sb_hm)
    xbc = ssm_conv_act(proj, p["ssm_conv_w"], p["ssm_conv_b"], name=nm("ssm_conv"))
    s["x_hm"] = _to_heads(xbc[:, :SSM_INNER], SSM_HEADS)
    s["b_hm"] = _to_heads(xbc[:, SSM_INNER:SSM_INNER + 128], SSM_GROUPS)
    s["c_hm"] = _to_heads(xbc[:, SSM_INNER + 128:], SSM_GROUPS)
    y_ssm_hm, s["states"] = ssd_fwd(s["x_hm"], s["b_hm"], s["c_hm"], proj, p["ssm_dt_bias"], p["ssm_a_log"],
                                    p["ssm_d"], name=nm("ssd_fwd"))
    s["y_ssm"] = _from_heads(y_ssm_hm)
    rows = [(proj, 256, OFF_CQ // 256), (proj, 128, OFF_CKV // 128), (proj, 128, OFF_KRA // 128),
            (proj, 128, OFF_KRB // 128), cos, sin]
    qp, kp, vv = rowwise(mla_prep_fn, rows, [p["mla_q_norm"], p["mla_kv_norm"], p["wqp"], p["wqr"], p["wkp"], p["wvp"]],
                         [(512, BF16), (512, BF16), (256, BF16)], name=nm("mla_prep"))
    s["mla_q"], s["mla_k"], s["mla_v"] = _to_heads_t(qp, MLA_HEADS), _to_heads(kp, MLA_HEADS), _to_heads(vv, MLA_HEADS)
    s["mla_o"], s["mla_lse"] = mla_fwd(s["mla_q"], s["mla_k"], s["mla_v"], name=nm("mla_fwd"))
    s["y_mla"] = _from_heads_t(s["mla_o"])
    (cat,) = rowwise(merge_fn, [s["y_sb"], s["y_ssm"], (proj, 512, OFF_Z // 512), s["y_mla"]],
                     [p["sb_out_norm"], p["ssm_out_norm"], p["mla_out_norm"]], [(D_MODEL, BF16)], name=nm("merge"),
                     post=lambda a, b, c: (jnp.concatenate([a, b, c], axis=1),))
    s["cat"] = cat
    h1 = matmul(cat, p["w_out"], name=nm("out_proj"), residual=h)
    s["h1"] = h1
    (n2,) = rowwise(rms_fn, [h1], [p["ffn_norm"]], [(D_MODEL, BF16)], name=nm("ffn_norm"))
    up = matmul(n2, p["w_up"], name=nm("ffn_up"))
    act = ffn_act(up, p["ffn_conv_w"], p["ffn_conv_b"], name=nm("ffn_act"))
    s["n2"], s["up"], s["act"] = n2, up, act
    h2 = matmul(act, p["w_down"], name=nm("ffn_down"), residual=h1)
    return h2, s


def _layer_bwd(dh2, s, p, cos, sin, li):
    nm = lambda t: "l%d_%s" % (li, t)
    g = {}
    proj = s["proj"]
    g["ffn_w_down"] = matmul(s["act"], dh2, name=nm("g_w_down"), ta=True)
    d_act = matmul(dh2, p["w_down_t"], name=nm("d_act"), out_dtype=BF16)
    d_up_g, d_up_v, gwg, gwv, gbg, gbv = ffn_bwd_fused(s["up"], p["ffn_conv_w"], p["ffn_conv_b"], d_act,
                                                       name=nm("ffn_act_bwd"))
    g["ffn_conv_w"] = jnp.concatenate([gwg, gwv], axis=1)
    g["ffn_conv_b"] = jnp.concatenate([gbg[0], gbv[0]])
    g["ffn_w_up"] = jnp.concatenate([matmul(s["n2"], d_up_g, name=nm("g_w_up_gate"), ta=True),
                                     matmul(s["n2"], d_up_v, name=nm("g_w_up_val"), ta=True)], axis=1)
    d_n2 = matmul(d_up_g, p["w_up_t"][:D_FF], name=nm("d_n2_gate"))
    d_n2 = matmul(d_up_v, p["w_up_t"][D_FF:], name=nm("d_n2_val"), residual=d_n2)
    (dh1,), (gn,) = rowwise_bwd(rms_fn, [s["h1"]], [], [p["ffn_norm"]], [d_n2], [F32], name=nm("ffn_norm_bwd"),
                                add0=dh2)
    g["ffn_norm"] = gn[0]
    g["w_out"] = matmul(s["cat"], dh1, name=nm("g_w_out"), ta=True)
    d_cat = matmul(dh1, p["w_out_t"], name=nm("d_cat"))
    (d_ysb, d_yssm, d_z, d_ymla), (g1, g2, g3) = rowwise_bwd(
        merge_fn, [s["y_sb"], s["y_ssm"], (proj, 512, OFF_Z // 512), s["y_mla"]], [],
        [p["sb_out_norm"], p["ssm_out_norm"], p["mla_out_norm"]], [d_cat], [F32, F32, BF16, F32], name=nm("merge_bwd"),
        pre_ct=lambda d: (d[:, 0:256], d[:, 256:768], d[:, 768:1024]))
    g["sb_out_norm"], g["ssm_out_norm"], g["mla_out_norm"] = g1[0], g2[0], g3[0]
    dq, dk, dv = sb_bwd(s["sb_q"], s["sb_k"], s["sb_v"], _to_heads_t(d_ysb, SB_HEADS), s["sb_bt"], name=nm("sb_bwd"),
                        q_scale=SB_DIM ** -0.5)
    d_sb = jnp.concatenate([_from_heads_t(dq), _from_heads(dk), _from_heads(dv)], axis=1).astype(BF16)
    dqp, dkp, dvv = mla_bwd(s["mla_q"], s["mla_k"], s["mla_v"], _to_heads_t(d_ymla, MLA_HEADS), s["mla_o"], s["mla_lse"],
                            name=nm("mla_bwd"))
    rows = [(proj, 256, OFF_CQ // 256), (proj, 128, OFF_CKV // 128), (proj, 128, OFF_KRA // 128),
            (proj, 128, OFF_KRB // 128)]
    (d_cq, d_ckv, d_kra, d_krb), (gqn, gkvn, gwqp, gwqr, gwkp, gwvp) = rowwise_bwd(
        mla_prep_fn, rows, [cos, sin], [p["mla_q_norm"], p["mla_kv_norm"], p["wqp"], p["wqr"], p["wkp"], p["wvp"]],
        [_from_heads_t(dqp), _from_heads(dkp), _from_heads(dvv)], [BF16] * 4, name=nm("mla_prep_bwd"), tile=256)
    g["mla_q_norm"], g["mla_kv_norm"] = gqn[0], gkvn[0]
    g["mla_w_uq"] = _unpack_gw_uq(gwqp, gwqr)
    g["mla_w_ukv"] = _unpack_gw_ukv(gwkp, gwvp)
    dx_hm, db_hm, dc_hm, d_dt, gdb, gal, gds = ssd_bwd(
        s["x_hm"], s["b_hm"], s["c_hm"], proj, s["states"], p["ssm_dt_bias"], p["ssm_a_log"], p["ssm_d"],
        _to_heads(d_yssm, SSM_HEADS), name=nm("ssd_bwd"))
    g["ssm_dt_bias"], g["ssm_a_log"], g["ssm_d"] = gdb[0, :8], gal[0, :8], gds[0, :8]
    d_xbc_act = jnp.concatenate([_from_heads(dx_hm), _from_heads(db_hm), _from_heads(dc_hm)], axis=1)
    d_pre = ssm_conv_bwd_a(proj, p["ssm_conv_w"], p["ssm_conv_b"], d_xbc_act, name=nm("ssm_conv_bwd_a"))
    d_xbc, g["ssm_conv_w"], gscb = conv_bwd_b(d_pre, proj, OFF_XBC, p["ssm_conv_w"], name=nm("ssm_conv_bwd_b"),
                                              out_dtype=BF16, tc=256)
    g["ssm_conv_b"] = gscb[0]
    d_proj = jnp.concatenate([d_sb, d_xbc, d_z, d_cq, d_ckv, d_dt.astype(BF16), d_kra, d_krb], axis=1)
    g["w_in"] = _unpack_gw_in(matmul(s["n1"], d_proj, name=nm("g_w_in"), ta=True))
    d_n1 = matmul(d_proj, p["w_in_pt"], name=nm("d_n1"))
    (dh0,), (gm,) = rowwise_bwd(rms_fn, [s["h"]], [], [p["mix_norm"]], [d_n1], [F32], name=nm("mix_norm_bwd"),
                                add0=dh1)
    g["mix_norm"] = gm[0]
    return dh0, g


def kernel(x, positions, mix_norm, w_in, sb_out_norm, ssm_conv_w, ssm_conv_b, ssm_dt_bias, ssm_a_log, ssm_d, ssm_out_norm, mla_q_norm, mla_w_uq, mla_kv_norm, mla_w_ukv, mla_out_norm, w_out, ffn_norm, ffn_w_up, ffn_conv_w, ffn_conv_b, ffn_w_down, final_norm, loss_target, m_mix_norm, m_w_in, m_sb_out_norm, m_ssm_conv_w, m_ssm_conv_b, m_ssm_dt_bias, m_ssm_a_log, m_ssm_d, m_ssm_out_norm, m_mla_q_norm, m_mla_w_uq, m_mla_kv_norm, m_mla_w_ukv, m_mla_out_norm, m_w_out, m_ffn_norm, m_ffn_w_up, m_ffn_conv_w, m_ffn_conv_b, m_ffn_w_down, m_final_norm, v_mix_norm, v_w_in, v_sb_out_norm, v_ssm_conv_w, v_ssm_conv_b, v_ssm_dt_bias, v_ssm_a_log, v_ssm_d, v_ssm_out_norm, v_mla_q_norm, v_mla_w_uq, v_mla_kv_norm, v_mla_w_ukv, v_mla_out_norm, v_w_out, v_ffn_norm, v_ffn_w_up, v_ffn_conv_w, v_ffn_conv_b, v_ffn_w_down, v_final_norm):
    W = dict(mix_norm=mix_norm, w_in=w_in, sb_out_norm=sb_out_norm, ssm_conv_w=ssm_conv_w, ssm_conv_b=ssm_conv_b,
             ssm_dt_bias=ssm_dt_bias, ssm_a_log=ssm_a_log, ssm_d=ssm_d, ssm_out_norm=ssm_out_norm,
             mla_q_norm=mla_q_norm, mla_w_uq=mla_w_uq, mla_kv_norm=mla_kv_norm, mla_w_ukv=mla_w_ukv,
             mla_out_norm=mla_out_norm, w_out=w_out, ffn_norm=ffn_norm, ffn_w_up=ffn_w_up, ffn_conv_w=ffn_conv_w,
             ffn_conv_b=ffn_conv_b, ffn_w_down=ffn_w_down, final_norm=final_norm)
    M = dict(mix_norm=m_mix_norm, w_in=m_w_in, sb_out_norm=m_sb_out_norm, ssm_conv_w=m_ssm_conv_w,
             ssm_conv_b=m_ssm_conv_b, ssm_dt_bias=m_ssm_dt_bias, ssm_a_log=m_ssm_a_log, ssm_d=m_ssm_d,
             ssm_out_norm=m_ssm_out_norm, mla_q_norm=m_mla_q_norm, mla_w_uq=m_mla_w_uq, mla_kv_norm=m_mla_kv_norm,
             mla_w_ukv=m_mla_w_ukv, mla_out_norm=m_mla_out_norm, w_out=m_w_out, ffn_norm=m_ffn_norm,
             ffn_w_up=m_ffn_w_up, ffn_conv_w=m_ffn_conv_w, ffn_conv_b=m_ffn_conv_b, ffn_w_down=m_ffn_w_down,
             final_norm=m_final_norm)
    V = dict(mix_norm=v_mix_norm, w_in=v_w_in, sb_out_norm=v_sb_out_norm, ssm_conv_w=v_ssm_conv_w,
             ssm_conv_b=v_ssm_conv_b, ssm_dt_bias=v_ssm_dt_bias, ssm_a_log=v_ssm_a_log, ssm_d=v_ssm_d,
             ssm_out_norm=v_ssm_out_norm, mla_q_norm=v_mla_q_norm, mla_w_uq=v_mla_w_uq, mla_kv_norm=v_mla_kv_norm,
             mla_w_ukv=v_mla_w_ukv, mla_out_norm=v_mla_out_norm, w_out=v_w_out, ffn_norm=v_ffn_norm,
             ffn_w_up=v_ffn_w_up, ffn_conv_w=v_ffn_conv_w, ffn_conv_b=v_ffn_conv_b, ffn_w_down=v_ffn_w_down,
             final_norm=v_final_norm)
    depth = mix_norm.shape[0]
    cx, cy, cc = _place()
    chip = 2 * cx + cy
    T = x.shape[1]

    shard_shapes = {n: W[n].shape for n in BIG}
    gathered = allgather_blocks(_flat_half(W, cc, BF16), name="gather_weights")
    full = {}
    per_chip = [_unflat_halves([gathered[2 * k + c].reshape(-1) for c in range(2)], shard_shapes) for k in range(4)]
    for n in BIG:
        full[n] = jnp.concatenate([per_chip[k][n] for k in range(4)], axis=BIG_AXIS[n])
    conv_full = {}
    small = {n: W[n] for n in SMALL_REPL}
    cw_flat = jnp.concatenate([W[n].reshape(-1) for n in SMALL_SHARD])
    cw_rows = -(-cw_flat.shape[0] // (8 * LANES)) * 8
    cw_all = allgather_direct(jnp.pad(cw_flat, (0, cw_rows * LANES - cw_flat.shape[0])).reshape(cw_rows, LANES),
                              name="gather_conv_taps")
    off = 0
    for n in SMALL_SHARD:
        sz = W[n].size
        conv_full[n] = jnp.concatenate(
            [cw_all[2 * k].reshape(-1)[off:off + sz].reshape(W[n].shape) for k in range(4)], axis=2)
        off += sz
    small.update(conv_full)

    cos, sin = _rope_tables(positions[0])
    params = [_layer_weights(full, small, li) for li in range(depth)]

    h = x[0]
    saved = []
    for li in range(depth):
        h, s = _layer_fwd(h, params[li], cos, sin, li)
        saved.append(s)
    dh, g_final, loss_lanes = loss_head(h, loss_target[0], _row(final_norm), name="loss_head")

    grads = [None] * depth
    for li in reversed(range(depth)):
        dh, grads[li] = _layer_bwd(dh, saved[li], params[li], cos, sin, li)
    grad_x = dh[None]
    G = {n: jnp.stack([grads[li][n] for li in range(depth)]) for n in WEIGHTS if n != "final_norm"}
    G["final_norm"] = g_final[0]

    def shard_major(n):
        a = G[n]
        ax = BIG_AXIS[n]
        parts = jnp.split(a, 4, axis=ax)
        return parts

    by_chip = {n: shard_major(n) for n in BIG}

    def flat_for(k, which):
        return _flat_half({n: by_chip[n][k] for n in BIG}, which, BF16)

    mine_first = jnp.stack([flat_for(k, cc) for k in range(4)])
    for_sibling = jnp.stack([flat_for(k, 1 - cc) for k in range(4)])
    R = mine_first.shape[1]
    from_sibling = send_to_sibling(for_sibling.reshape(4 * R, LANES), name="grads_to_sibling")
    pair = add_blocks([mine_first.reshape(4 * R, LANES), from_sibling], BF16, name="grads_pair_sum").reshape(4, R, LANES)
    others = chip_exchange(pair, name="grads_chip_exchange")
    own = lax.dynamic_index_in_dim(pair, chip, 0, keepdims=False)
    half = add_blocks([own, others[0], others[1], others[2]], F32, name="grads_chip_sum")
    both = pair_gather(half, name="grads_pair_gather")
    g_big = _unflat_halves([both[c].reshape(-1) for c in range(2)], shard_shapes)

    small_list = [G[n].reshape(-1) for n in SMALL_REPL] + [G[n].reshape(-1) for n in SMALL_SHARD]
    small_list.append(jnp.sum(loss_lanes).reshape(1))
    sm = jnp.concatenate(small_list)
    n_small = sm.shape[0]
    sm_rows = -(-n_small // (16 * LANES)) * 16
    sm_all = allgather_direct(jnp.pad(sm, (0, sm_rows * LANES - n_small)).reshape(sm_rows, LANES), name="gather_small")
    sm_sum = add_blocks([sm_all[d] for d in range(8)], F32, name="small_sum").reshape(-1)
    g_small, off = {}, 0
    for n in SMALL_REPL:
        g_small[n] = sm_sum[off:off + W[n].size].reshape(W[n].shape)
        off += W[n].size
    for n in SMALL_SHARD:
        full_shape = conv_full[n].shape
        sz = conv_full[n].size
        gfull = sm_sum[off:off + sz].reshape(full_shape)
        width = W[n].shape[2]
        g_small[n] = lax.dynamic_slice_in_dim(gfull, chip * width, width, axis=2)
        off += sz
    loss = sm_sum[off]

    grad_out, delta, new_m, new_v = {}, {}, {}, {}
    for n in BIG:
        shp = W[n].shape
        two_d = lambda a: a.reshape(shp[0] * shp[1], shp[2])
        d, nm_, nv_ = adamw(two_d(W[n]), two_d(g_big[n]), two_d(M[n]), two_d(V[n]), name="adamw_" + n)
        grad_out[n], delta[n], new_m[n], new_v[n] = g_big[n], d.reshape(shp), nm_.reshape(shp), nv_.reshape(shp)
    small_names = SMALL_REPL + SMALL_SHARD

    def flat_small(d):
        f = jnp.concatenate([d[n].reshape(-1) for n in small_names])
        rows = -(-f.shape[0] // (8 * LANES)) * 8
        return jnp.pad(f, (0, rows * LANES - f.shape[0])).reshape(rows, LANES)

    vpad = flat_small(V)
    d, nm_, nv_ = adamw(flat_small(W), flat_small(g_small), flat_small(M), vpad, name="adamw_small")
    off = 0
    for n in small_names:
        sz = W[n].size
        grad_out[n] = g_small[n]
        delta[n] = d.reshape(-1)[off:off + sz].reshape(W[n].shape)
        new_m[n] = nm_.reshape(-1)[off:off + sz].reshape(W[n].shape)
        new_v[n] = nv_.reshape(-1)[off:off + sz].reshape(W[n].shape)
        off += sz

    return (loss, grad_x, *[grad_out[n] for n in WEIGHTS], *[delta[n] for n in WEIGHTS],
            *[new_m[n] for n in WEIGHTS], *[new_v[n] for n in WEIGHTS])
```

```python
import functools
import math

import jax
import jax.numpy as jnp
from jax import lax
from jax.experimental import pallas as pl
from jax.experimental.pallas import tpu as pltpu

F32 = jnp.float32
BF16 = jnp.bfloat16

EPS = 1e-6
D_MODEL = 1024
SB_HEADS, SB_DIM = 4, 64
SSM_HEADS, SSM_DIM, SSM_GROUPS, SSM_STATE, SSM_CHUNK = 8, 64, 2, 64, 128
SSM_INNER = SSM_HEADS * SSM_DIM
SSM_CONV_DIM = SSM_INNER + 2 * SSM_GROUPS * SSM_STATE
MLA_HEADS, MLA_NOPE, MLA_ROPE, MLA_V = 4, 64, 32, 64
MLA_QK = MLA_NOPE + MLA_ROPE
MLA_SCALE = MLA_QK ** -0.5
ROPE_THETA = 10000.0
D_FF = 2816
IN_SPLITS = (768, 512, 768, 8, 256, 128, 32)

OFF_SB, OFF_XBC, OFF_Z, OFF_CQ, OFF_CKV, OFF_DT, OFF_KRA, OFF_KRB = 0, 768, 1536, 2048, 2304, 2432, 2560, 2688
D_IN_P = 2816
LANES = 128

ADAM_LR, ADAM_B1, ADAM_B2, ADAM_EPS, ADAM_WD, ADAM_STEP = 0.001, 0.9, 0.999, 1e-08, 0.01, 10

V7X_VMEM_LIMIT = 48 * 1024 * 1024

NT = (((1,), (1,)), ((), ()))
TN = (((0,), (0,)), ((), ()))

BIG = ("w_in", "mla_w_uq", "mla_w_ukv", "w_out", "ffn_w_up", "ffn_w_down")
BIG_AXIS = {"w_in": 2, "mla_w_uq": 2, "mla_w_ukv": 2, "w_out": 1, "ffn_w_up": 2, "ffn_w_down": 1}
SMALL_REPL = ("mix_norm", "sb_out_norm", "ssm_conv_b", "ssm_dt_bias", "ssm_a_log", "ssm_d", "ssm_out_norm",
              "mla_q_norm", "mla_kv_norm", "mla_out_norm", "ffn_norm", "ffn_conv_b", "final_norm")
SMALL_SHARD = ("ssm_conv_w", "ffn_conv_w")
WEIGHTS = ("mix_norm", "w_in", "sb_out_norm", "ssm_conv_w", "ssm_conv_b", "ssm_dt_bias", "ssm_a_log", "ssm_d",
           "ssm_out_norm", "mla_q_norm", "mla_w_uq", "mla_kv_norm", "mla_w_ukv", "mla_out_norm", "w_out", "ffn_norm",
           "ffn_w_up", "ffn_conv_w", "ffn_conv_b", "ffn_w_down", "final_norm")


def _cparams(*sem):
    return pltpu.CompilerParams(dimension_semantics=sem if sem else None, vmem_limit_bytes=V7X_VMEM_LIMIT)


def _pick(n, target, mult=LANES):
    best = None
    for d in range(mult, min(n, target) + 1, mult):
        if n % d == 0:
            best = d
    return best or n


def _sigmoid(x):
    return 1.0 / (1.0 + jnp.exp(-x))


def _softplus(x):
    ax = jnp.where(x > 0, x, -x)
    return jnp.where(x > 0, x, 0.0) + jnp.log(1.0 + jnp.exp(-ax))


def _rms(x, g):
    return x * lax.rsqrt(jnp.mean(x * x, axis=-1, keepdims=True) + EPS) * g


def _raw_nn(a, b):
    return jnp.dot(a.astype(BF16), b.astype(BF16), preferred_element_type=F32)


def _raw_nt(a, b):
    return lax.dot_general(a.astype(BF16), b.astype(BF16), NT, preferred_element_type=F32)


def _raw_tn(a, b):
    return lax.dot_general(a.astype(BF16), b.astype(BF16), TN, preferred_element_type=F32)


@jax.custom_vjp
def mm_nn(a, b):
    return _raw_nn(a, b)


mm_nn.defvjp(lambda a, b: (_raw_nn(a, b), (a, b)),
             lambda r, ct: (_raw_nt(ct, r[1]), _raw_tn(r[0], ct)))


@jax.custom_vjp
def mm_nt(a, b):
    return _raw_nt(a, b)


mm_nt.defvjp(lambda a, b: (_raw_nt(a, b), (a, b)),
             lambda r, ct: (_raw_nn(ct, r[1]), _raw_tn(ct, r[0])))


@jax.custom_vjp
def mm_tn(a, b):
    return _raw_tn(a, b)


mm_tn.defvjp(lambda a, b: (_raw_tn(a, b), (a, b)),
             lambda r, ct: (_raw_nt(r[1], ct), _raw_nn(r[0], ct)))


def _split_dot(x, m, terms):
    acc = None
    r = x
    for t in range(terms):
        xt = r.astype(BF16)
        d = jnp.dot(xt, m, preferred_element_type=F32)
        acc = d if acc is None else acc + d
        if t + 1 < terms:
            r = r - xt.astype(F32)
    return acc


def _tri_dot(tri, x, terms=3):
    parts = []
    r = x
    for t in range(terms):
        xt = r.astype(BF16)
        parts.append(xt)
        if t + 1 < terms:
            r = r - xt.astype(F32)
    return jnp.dot(jnp.concatenate([tri] * terms, axis=1), jnp.concatenate(parts, axis=0),
                   preferred_element_type=F32)


def _tri(n, cmp):
    r = lax.broadcasted_iota(jnp.int32, (n, n), 0)
    c = lax.broadcasted_iota(jnp.int32, (n, n), 1)
    return cmp(r, c).astype(BF16)


@jax.custom_vjp
def csum_rows(x):
    return _tri_dot(_tri(x.shape[0], lambda r, c: r >= c), x)


csum_rows.defvjp(lambda x: (csum_rows(x), None),
                 lambda _, ct: (_tri_dot(_tri(ct.shape[0], lambda r, c: r <= c), ct),))


def matmul(a, b, *, name, out_dtype=F32, ta=False, residual=None):
    if ta:
        K, M = a.shape
    else:
        M, K = a.shape
    N = b.shape[1]
    tm = _pick(M, 1408)
    tn = _pick(N, 1408)
    tk = _pick(K, 1408)
    nk = K // tk
    has_res = residual is not None

    def body(*refs):
        if has_res:
            a_ref, b_ref, r_ref, o_ref, acc = refs
        else:
            a_ref, b_ref, o_ref, acc = refs
        k = pl.program_id(2)

        @pl.when(k == 0)
        def _():
            acc[...] = jnp.zeros_like(acc)

        av = a_ref[...].astype(BF16)
        bv = b_ref[...].astype(BF16)
        if ta:
            acc[...] += lax.dot_general(av, bv, TN, preferred_element_type=F32)
        else:
            acc[...] += jnp.dot(av, bv, preferred_element_type=F32)

        @pl.when(k == nk - 1)
        def _():
            r = acc[...]
            if has_res:
                r = r + r_ref[...].astype(F32)
            o_ref[...] = r.astype(o_ref.dtype)

    a_spec = pl.BlockSpec((tk, tm), lambda i, j, k: (k, i)) if ta else pl.BlockSpec((tm, tk), lambda i, j, k: (i, k))
    in_specs = [a_spec, pl.BlockSpec((tk, tn), lambda i, j, k: (k, j))]
    args = [a, b]
    if has_res:
        in_specs.append(pl.BlockSpec((tm, tn), lambda i, j, k: (i, j)))
        args.append(residual)
    return pl.pallas_call(
        body, name=name, grid=(M // tm, N // tn, nk),
        in_specs=in_specs, out_specs=pl.BlockSpec((tm, tn), lambda i, j, k: (i, j)),
        out_shape=jax.ShapeDtypeStruct((M, N), out_dtype),
        scratch_shapes=[pltpu.VMEM((tm, tn), F32)],
        compiler_params=_cparams("parallel", "parallel", "arbitrary"),
    )(*args)


def _row_spec(entry, tl):
    if isinstance(entry, tuple):
        arr, width, cb = entry
        return arr, pl.BlockSpec((tl, width), lambda i, cb=cb: (i, cb))
    return entry, pl.BlockSpec((tl, entry.shape[1]), lambda i: (i, 0))


def _rows_T(entry):
    return (entry[0] if isinstance(entry, tuple) else entry).shape[0]


def rowwise(fn, rows, params, outs, *, name, tile=512, post=None):
    T = _rows_T(rows[0])
    tl = min(T, tile)
    nr, npar = len(rows), len(params)

    def body(*refs):
        r = [ref[...].astype(F32) for ref in refs[:nr]]
        p = [ref[...].astype(F32) for ref in refs[nr:nr + npar]]
        res = fn(*r, *p)
        if post is not None:
            res = post(*res)
        for o_ref, val in zip(refs[nr + npar:], res):
            o_ref[...] = val.astype(o_ref.dtype)

    arrs, specs = [], []
    for e in rows:
        a, s = _row_spec(e, tl)
        arrs.append(a)
        specs.append(s)
    for p in params:
        arrs.append(p)
        specs.append(pl.BlockSpec(p.shape, lambda i: (0, 0)))
    res = pl.pallas_call(
        body, name=name, grid=(T // tl,), in_specs=specs,
        out_specs=[pl.BlockSpec((tl, c), lambda i: (i, 0)) for c, _ in outs],
        out_shape=[jax.ShapeDtypeStruct((T, c), dt) for c, dt in outs],
        compiler_params=_cparams("parallel"),
    )(*arrs)
    return res


def rowwise_bwd(fn, rows, nd_rows, params, cts, grad_dtypes, *, name, tile=512, pre_ct=None, add0=None):
    T = _rows_T(rows[0])
    tl = min(T, tile)
    nr, nn, npar, nc = len(rows), len(nd_rows), len(params), len(cts)
    has_add = add0 is not None

    def body(*refs):
        pos = 0
        r = [ref[...].astype(F32) for ref in refs[pos:pos + nr]]
        pos += nr
        nd = [ref[...].astype(F32) for ref in refs[pos:pos + nn]]
        pos += nn
        p = [ref[...].astype(F32) for ref in refs[pos:pos + npar]]
        pos += npar
        c = [ref[...].astype(F32) for ref in refs[pos:pos + nc]]
        pos += nc
        if has_add:
            addv = refs[pos][...].astype(F32)
            pos += 1
        rg_refs = refs[pos:pos + nr]
        pg_refs = refs[pos + nr:pos + nr + npar]
        if pre_ct is not None:
            c = list(pre_ct(*c))
        _, vjp = jax.vjp(lambda *a: fn(*a[:nr], *nd, *a[nr:]), *r, *p)
        g = vjp(tuple(c))
        for j, ref in enumerate(rg_refs):
            val = g[j]
            if has_add and j == 0:
                val = val + addv
            ref[...] = val.astype(ref.dtype)
        if npar:
            @pl.when(pl.program_id(0) == 0)
            def _():
                for ref in pg_refs:
                    ref[...] = jnp.zeros_like(ref)
            for j, ref in enumerate(pg_refs):
                ref[...] += g[nr + j]

    arrs, specs = [], []
    widths = []
    for e in list(rows) + list(nd_rows):
        a, s = _row_spec(e, tl)
        arrs.append(a)
        specs.append(s)
        widths.append(s.block_shape[1])
    for p in params:
        arrs.append(p)
        specs.append(pl.BlockSpec(p.shape, lambda i: (0, 0)))
    for e in cts:
        a, s = _row_spec(e, tl)
        arrs.append(a)
        specs.append(s)
    if has_add:
        a, s = _row_spec(add0, tl)
        arrs.append(a)
        specs.append(s)
    out_specs = [pl.BlockSpec((tl, widths[j]), lambda i: (i, 0)) for j in range(nr)]
    out_shape = [jax.ShapeDtypeStruct((T, widths[j]), grad_dtypes[j]) for j in range(nr)]
    out_specs += [pl.BlockSpec(p.shape, lambda i: (0, 0)) for p in params]
    out_shape += [jax.ShapeDtypeStruct(p.shape, F32) for p in params]
    res = pl.pallas_call(
        body, name=name, grid=(T // tl,), in_specs=specs, out_specs=out_specs, out_shape=out_shape,
        compiler_params=_cparams("arbitrary"),
    )(*arrs)
    return list(res[:nr]), list(res[nr:])


def rms_fn(h, g):
    return (_rms(h, g),)


def merge_fn(ysb, yssm, z, ymla, g_sb, g_ssm, g_mla):
    ya = _rms(ysb, g_sb)
    yb = _rms(yssm * (z * _sigmoid(z)), g_ssm)
    yc = _rms(ymla, g_mla)
    return ya, yb, yc


def mla_prep_fn(cq, ckv, kra, krb, cos, sin, qn, kvn, wqp, wqr, wkp, wvp):
    cos4 = jnp.concatenate([cos] * MLA_HEADS, axis=1)
    sin4 = jnp.concatenate([sin] * MLA_HEADS, axis=1)
    nq = _rms(cq, qn)
    q = (mm_nn(nq, wqp) * cos4 + mm_nn(nq, wqr) * sin4) * MLA_SCALE
    nkv = _rms(ckv, kvn)
    kpe = kra * cos + krb * sin
    k = mm_nn(nkv, wkp) + jnp.concatenate([kpe] * MLA_HEADS, axis=1)
    v = mm_nn(nkv, wvp)
    return q, k, v


HALO = 8


def _prev_halo_spec(tl, tc, col_of):
    return pl.BlockSpec((HALO, tc), lambda i, j: (jnp.maximum(i * (tl // HALO) - 1, 0), col_of(j)))


def _fill_prev(buf, x_ref, halo_ref, i):
    buf[0:HALO, :] = jnp.where(i > 0, halo_ref[...].astype(F32), 0.0)
    buf[HALO:, :] = x_ref[...].astype(F32)


def _conv_from(buf, w_ref, b_ref, K, tl):
    acc = b_ref[...].astype(F32) + jnp.zeros((tl, buf.shape[1]), F32)
    for k in range(K):
        acc = acc + buf[pl.ds(HALO - (K - 1 - k), tl), :] * w_ref[k:k + 1, :].astype(F32)
    return acc


def ssm_conv_act(proj, w, b, *, name, tile=512, tc=256):
    T = proj.shape[0]
    K, C = w.shape
    tl = min(T, tile)
    c0 = OFF_XBC // tc

    def body(x_ref, halo_ref, w_ref, b_ref, o_ref, buf):
        _fill_prev(buf, x_ref, halo_ref, pl.program_id(0))
        u = _conv_from(buf, w_ref, b_ref, K, tl)
        o_ref[...] = u * _sigmoid(u)

    return pl.pallas_call(
        body, name=name, grid=(T // tl, C // tc),
        in_specs=[pl.BlockSpec((tl, tc), lambda i, j: (i, c0 + j)), _prev_halo_spec(tl, tc, lambda j: c0 + j),
                  pl.BlockSpec((K, tc), lambda i, j: (0, j)), pl.BlockSpec((1, tc), lambda i, j: (0, j))],
        out_specs=pl.BlockSpec((tl, tc), lambda i, j: (i, j)),
        out_shape=jax.ShapeDtypeStruct((T, C), F32),
        scratch_shapes=[pltpu.VMEM((tl + HALO, tc), F32)],
        compiler_params=_cparams("parallel", "parallel"),
    )(proj, proj, w, b)


def ssm_conv_bwd_a(proj, w, b, d_out, *, name, tile=512, tc=256):
    T = proj.shape[0]
    K, C = w.shape
    tl = min(T, tile)
    c0 = OFF_XBC // tc

    def body(x_ref, halo_ref, w_ref, b_ref, d_ref, o_ref, buf):
        _fill_prev(buf, x_ref, halo_ref, pl.program_id(0))
        u = _conv_from(buf, w_ref, b_ref, K, tl)
        s = _sigmoid(u)
        o_ref[...] = d_ref[...].astype(F32) * (s * (1.0 + u * (1.0 - s)))

    return pl.pallas_call(
        body, name=name, grid=(T // tl, C // tc),
        in_specs=[pl.BlockSpec((tl, tc), lambda i, j: (i, c0 + j)), _prev_halo_spec(tl, tc, lambda j: c0 + j),
                  pl.BlockSpec((K, tc), lambda i, j: (0, j)), pl.BlockSpec((1, tc), lambda i, j: (0, j)),
                  pl.BlockSpec((tl, tc), lambda i, j: (i, j))],
        out_specs=pl.BlockSpec((tl, tc), lambda i, j: (i, j)),
        out_shape=jax.ShapeDtypeStruct((T, C), F32),
        scratch_shapes=[pltpu.VMEM((tl + HALO, tc), F32)],
        compiler_params=_cparams("parallel", "parallel"),
    )(proj, proj, w, b, d_out)


def ffn_act(up, w, b, *, name, tile=512, tc=1408):
    T = up.shape[0]
    K = w.shape[0]
    tl = min(T, tile)
    nj = D_FF // tc

    def body(xg_ref, hg_ref, xv_ref, hv_ref, wg_ref, wv_ref, bg_ref, bv_ref, o_ref, bufg, bufv):
        i = pl.program_id(0)
        _fill_prev(bufg, xg_ref, hg_ref, i)
        _fill_prev(bufv, xv_ref, hv_ref, i)
        gate = _conv_from(bufg, wg_ref, bg_ref, K, tl)
        val = _conv_from(bufv, wv_ref, bv_ref, K, tl)
        o_ref[...] = (gate * _sigmoid(gate) * val).astype(o_ref.dtype)

    return pl.pallas_call(
        body, name=name, grid=(T // tl, nj),
        in_specs=[pl.BlockSpec((tl, tc), lambda i, j: (i, j)), _prev_halo_spec(tl, tc, lambda j: j),
                  pl.BlockSpec((tl, tc), lambda i, j: (i, nj + j)), _prev_halo_spec(tl, tc, lambda j: nj + j),
                  pl.BlockSpec((K, tc), lambda i, j: (0, j)), pl.BlockSpec((K, tc), lambda i, j: (0, nj + j)),
                  pl.BlockSpec((1, tc), lambda i, j: (0, j)), pl.BlockSpec((1, tc), lambda i, j: (0, nj + j))],
        out_specs=pl.BlockSpec((tl, tc), lambda i, j: (i, j)),
        out_shape=jax.ShapeDtypeStruct((T, D_FF), BF16),
        scratch_shapes=[pltpu.VMEM((tl + HALO, tc), F32), pltpu.VMEM((tl + HALO, tc), F32)],
        compiler_params=_cparams("parallel", "parallel"),
    )(up, up, up, up, w, w, b, b)


def ffn_bwd_fused(up, w, b, d_act, *, name, tile=512, tc=256):
    T = up.shape[0]
    K = w.shape[0]
    tl = min(T, tile)
    nj = D_FF // tc
    nblk = T // HALO
    ext = tl + HALO

    def body(xg, hgp, hgn, xv, hvp, hvn, wg, wv, bg, bv, d, dn, og, ov, dwg, dwv, dbg, dbv, bufg, bufv, dgb, dvb):
        i = pl.program_id(1)
        last = pl.num_programs(1) - 1

        def fill(buf, x_ref, prev_ref, next_ref):
            buf[0:HALO, :] = jnp.where(i > 0, prev_ref[...].astype(F32), 0.0)
            buf[HALO:HALO + tl, :] = x_ref[...].astype(F32)
            buf[HALO + tl:, :] = jnp.where(i < last, next_ref[...].astype(F32), 0.0)

        def conv_ext(buf, w_ref, b_ref):
            acc = b_ref[...].astype(F32) + jnp.zeros((ext, tc), F32)
            for k in range(K):
                acc = acc + buf[pl.ds(HALO - (K - 1 - k), ext), :] * w_ref[k:k + 1, :].astype(F32)
            return acc

        fill(bufg, xg, hgp, hgn)
        fill(bufv, xv, hvp, hvn)
        gate = conv_ext(bufg, wg, bg)
        val = conv_ext(bufv, wv, bv)
        dd = jnp.concatenate([d[...].astype(F32), jnp.where(i < last, dn[...].astype(F32)[0:HALO], 0.0)], axis=0)
        s = _sigmoid(gate)
        dgb[...] = dd * val * (s * (1.0 + gate * (1.0 - s)))
        dvb[...] = dd * (gate * s)

        @pl.when(i == 0)
        def _():
            for ref in (dwg, dwv, dbg, dbv):
                ref[...] = jnp.zeros_like(ref)

        for dbuf, xbuf, w_ref, o_ref, dw_ref, db_ref in ((dgb, bufg, wg, og, dwg, dbg), (dvb, bufv, wv, ov, dwv, dbv)):
            cur = dbuf[0:tl, :]
            dx = jnp.zeros((tl, tc), F32)
            for k in range(K):
                sft = K - 1 - k
                dx = dx + dbuf[pl.ds(sft, tl), :] * w_ref[k:k + 1, :].astype(F32)
                dw_ref[k:k + 1, :] += jnp.sum(cur * xbuf[pl.ds(HALO - sft, tl), :], axis=0, keepdims=True)
            db_ref[...] += jnp.sum(cur, axis=0, keepdims=True)
            o_ref[...] = dx.astype(o_ref.dtype)

    prev = lambda i: jnp.maximum(i * (tl // HALO) - 1, 0)
    nxt = lambda i: jnp.minimum((i + 1) * (tl // HALO), nblk - 1)

    def x_specs(col):
        return [pl.BlockSpec((tl, tc), lambda j, i: (i, col(j))), pl.BlockSpec((HALO, tc), lambda j, i: (prev(i), col(j))),
                pl.BlockSpec((HALO, tc), lambda j, i: (nxt(i), col(j)))]

    gcol, vcol = (lambda j: j), (lambda j: nj + j)
    in_specs = (x_specs(gcol) + x_specs(vcol)
                + [pl.BlockSpec((K, tc), lambda j, i: (0, j)), pl.BlockSpec((K, tc), lambda j, i: (0, nj + j)),
                   pl.BlockSpec((1, tc), lambda j, i: (0, j)), pl.BlockSpec((1, tc), lambda j, i: (0, nj + j)),
                   pl.BlockSpec((tl, tc), lambda j, i: (i, j)),
                   pl.BlockSpec((2 * HALO, tc), lambda j, i: (jnp.minimum((i + 1) * (tl // (2 * HALO)), nblk // 2 - 1), j))])
    row_out = pl.BlockSpec((tl, tc), lambda j, i: (i, j))
    w_out = pl.BlockSpec((K, tc), lambda j, i: (0, j))
    b_out = pl.BlockSpec((1, tc), lambda j, i: (0, j))
    return pl.pallas_call(
        body, name=name, grid=(nj, T // tl), in_specs=in_specs,
        out_specs=[row_out, row_out, w_out, w_out, b_out, b_out],
        out_shape=[jax.ShapeDtypeStruct((T, D_FF), BF16)] * 2 + [jax.ShapeDtypeStruct((K, D_FF), F32)] * 2
        + [jax.ShapeDtypeStruct((1, D_FF), F32)] * 2,
        scratch_shapes=[pltpu.VMEM((tl + 2 * HALO, tc), F32)] * 2 + [pltpu.VMEM((ext, tc), F32)] * 2,
        compiler_params=_cparams("parallel", "arbitrary"),
    )(up, up, up, up, up, up, w, w, b, b, d_act, d_act)


def conv_bwd_b(du, x, x_off, w, *, name, out_dtype, tile=512, tc=256):
    T, C = du.shape
    K = w.shape[0]
    tl = min(T, tile)
    c0 = x_off // tc
    nblk = T // HALO

    def body(du_ref, nx_ref, x_ref, halo_ref, w_ref, dx_ref, dw_ref, db_ref, dbuf, xbuf):
        i = pl.program_id(1)
        last = pl.num_programs(1) - 1
        d = du_ref[...].astype(F32)
        dbuf[0:tl, :] = d
        dbuf[tl:, :] = jnp.where(i < last, nx_ref[...].astype(F32), 0.0)
        _fill_prev(xbuf, x_ref, halo_ref, i)

        @pl.when(i == 0)
        def _():
            dw_ref[...] = jnp.zeros_like(dw_ref)
            db_ref[...] = jnp.zeros_like(db_ref)

        dx = jnp.zeros((tl, tc), F32)
        for k in range(K):
            s = K - 1 - k
            dx = dx + dbuf[pl.ds(s, tl), :] * w_ref[k:k + 1, :].astype(F32)
            dw_ref[k:k + 1, :] += jnp.sum(d * xbuf[pl.ds(HALO - s, tl), :], axis=0, keepdims=True)
        db_ref[...] += jnp.sum(d, axis=0, keepdims=True)
        dx_ref[...] = dx.astype(dx_ref.dtype)

    return pl.pallas_call(
        body, name=name, grid=(C // tc, T // tl),
        in_specs=[pl.BlockSpec((tl, tc), lambda j, i: (i, j)),
                  pl.BlockSpec((HALO, tc), lambda j, i: (jnp.minimum((i + 1) * (tl // HALO), nblk - 1), j)),
                  pl.BlockSpec((tl, tc), lambda j, i: (i, c0 + j)),
                  pl.BlockSpec((HALO, tc), lambda j, i: (jnp.maximum(i * (tl // HALO) - 1, 0), c0 + j)),
                  pl.BlockSpec((K, tc), lambda j, i: (0, j))],
        out_specs=[pl.BlockSpec((tl, tc), lambda j, i: (i, j)), pl.BlockSpec((K, tc), lambda j, i: (0, j)),
                   pl.BlockSpec((1, tc), lambda j, i: (0, j))],
        out_shape=[jax.ShapeDtypeStruct((T, C), out_dtype), jax.ShapeDtypeStruct((K, C), F32),
                   jax.ShapeDtypeStruct((1, C), F32)],
        scratch_shapes=[pltpu.VMEM((tl + HALO, tc), F32), pltpu.VMEM((tl + HALO, tc), F32)],
        compiler_params=_cparams("parallel", "arbitrary"),
    )(du, du, x, x, w)


def _attn_tiles(T):
    return min(T, 1024), min(T, 256)


def _after_diag(keys, queries, strict):
    d = lax.broadcasted_iota(jnp.int32, (keys, queries), 1) - lax.broadcasted_iota(jnp.int32, (keys, queries), 0)
    return d > 0 if strict else d >= 0


def _log_gates(z):
    l1p = jnp.log(1.0 + jnp.exp(-jnp.abs(z)))
    a = jnp.minimum(z, 0.0) - l1p
    return a, a - z


def _causal_sweep(i, tq, tk, block, descending, keep_going=None, first_block=None):
    nb = tq // tk
    n_full = i * nb

    def band():
        order = reversed(range(nb)) if descending else range(nb)
        for bb in order:
            block(pl.multiple_of(i * tq + bb * tk, tk), bb * tk, True)

    def full():
        if descending and keep_going is not None:
            def step(j):
                block(pl.multiple_of((n_full - 1 - j) * tk, tk), 0, False)
                return j + 1
            done = lax.while_loop(lambda j: jnp.logical_and(j < n_full, keep_going()), step, jnp.int32(0))
            return n_full - done

        def step(j, c):
            kb = (n_full - 1 - j) if descending else j
            block(pl.multiple_of(kb * tk, tk), 0, False)
            return c
        lax.fori_loop(0 if first_block is None else first_block, n_full, step, 0)
        return None

    if descending:
        band()
        return full()
    full()
    band()
    return None


def sb_fwd(q, k, v, *, name):
    H, dh, T = q.shape
    tq, tk = _attn_tiles(T)

    def body(q_ref, k_ref, v_ref, y_ref, bt_ref, first_ref, acc, run):
        acc[...] = jnp.zeros_like(acc)
        run[...] = jnp.zeros_like(run)
        u_after = _tri(tk, lambda r, c: r < c)

        def block(k0, r0, masked):
            kb = k_ref[pl.ds(k0, tk), :]
            vb = v_ref[pl.ds(k0, tk), :]
            z = jnp.dot(kb, q_ref[:, r0:], preferred_element_type=F32)
            a, b = _log_gates(z)
            if masked:
                valid = _after_diag(tk, tq - r0, True)
                b = jnp.where(valid, b, 0.0)
            w = jnp.exp(a + _tri_dot(u_after, b, 2) + run[:, r0:])
            if masked:
                w = jnp.where(valid, w, 0.0)
            acc[:, r0:] += lax.dot_general(vb, w.astype(BF16), TN, preferred_element_type=F32)
            run[:, r0:] += jnp.sum(b, axis=0, keepdims=True)

        first = _causal_sweep(pl.program_id(1), tq, tk, block, descending=True,
                              keep_going=lambda: jnp.max(run[...]) >= SB_ZERO_BELOW)
        y_ref[...] = acc[...]
        bt_ref[...] = run[...]
        first_ref[...] = jnp.zeros(first_ref.shape, F32) + first.astype(F32)

    return pl.pallas_call(
        body, name=name, grid=(H, T // tq),
        in_specs=[pl.BlockSpec((None, dh, tq), lambda h, i: (h, 0, i)),
                  pl.BlockSpec((None, T, dh), lambda h, i: (h, 0, 0)),
                  pl.BlockSpec((None, T, dh), lambda h, i: (h, 0, 0))],
        out_specs=[pl.BlockSpec((None, dh, tq), lambda h, i: (h, 0, i)),
                   pl.BlockSpec((None, 1, tq), lambda h, i: (h, 0, i)),
                   pl.BlockSpec((None, None, HALO, LANES), lambda h, i: (h, i, 0, 0))],
        out_shape=[jax.ShapeDtypeStruct((H, dh, T), F32), jax.ShapeDtypeStruct((H, 1, T), F32),
                   jax.ShapeDtypeStruct((H, T // tq, HALO, LANES), F32)],
        scratch_shapes=[pltpu.VMEM((dh, tq), F32), pltpu.VMEM((1, tq), F32)],
        compiler_params=_cparams("parallel", "parallel"),
    )(q, k, v)


def sb_bwd(q, k, v, dy, btot, first, *, name, q_scale):
    H, dh, T = q.shape
    tq, tk = _attn_tiles(T)

    def body(q_ref, k_ref, v_ref, dy_ref, bt_ref, first_ref, dq_ref, dk_ref, dv_ref, dq, pb, pg, dyb):
        @pl.when(pl.program_id(1) == 0)
        def _():
            dk_ref[...] = jnp.zeros_like(dk_ref)
            dv_ref[...] = jnp.zeros_like(dv_ref)

        dq[...] = jnp.zeros_like(dq)
        pb[...] = jnp.zeros_like(pb)
        pg[...] = jnp.zeros_like(pg)
        dyb[...] = dy_ref[...].astype(BF16)
        u_upto = _tri(tk, lambda r, c: r >= c)
        u_before = _tri(tk, lambda r, c: r > c)

        def block(k0, r0, masked):
            kb = k_ref[pl.ds(k0, tk), :]
            vb = v_ref[pl.ds(k0, tk), :]
            qv = q_ref[:, r0:]
            dyv = dyb[:, r0:]
            z = jnp.dot(kb, qv, preferred_element_type=F32)
            a, b = _log_gates(z)
            if masked:
                valid = _after_diag(tk, tq - r0, True)
                b = jnp.where(valid, b, 0.0)
            w = jnp.exp(a + (bt_ref[:, r0:] - pb[:, r0:] - _tri_dot(u_upto, b, 2)))
            if masked:
                w = jnp.where(valid, w, 0.0)
            g = w * jnp.dot(vb, dyv, preferred_element_type=F32)
            dz = g - jnp.exp(a) * (g + pg[:, r0:] + _tri_dot(u_before, g, 2))
            if masked:
                dz = jnp.where(valid, dz, 0.0)
            dz = dz.astype(BF16)
            dq[:, r0:] += lax.dot_general(kb, dz, TN, preferred_element_type=F32)
            dk_ref[pl.ds(k0, tk), :] += lax.dot_general(dz, qv, NT, preferred_element_type=F32)
            dv_ref[pl.ds(k0, tk), :] += lax.dot_general(w.astype(BF16), dyv, NT, preferred_element_type=F32)
            pb[:, r0:] += jnp.sum(b, axis=0, keepdims=True)
            pg[:, r0:] += jnp.sum(g, axis=0, keepdims=True)

        i = pl.program_id(1)
        first = jnp.clip(jnp.max(first_ref[...]).astype(jnp.int32), 0, i * (tq // tk))
        _causal_sweep(i, tq, tk, block, descending=False, first_block=first)
        dq_ref[...] = dq[...] * q_scale

    return pl.pallas_call(
        body, name=name, grid=(H, T // tq),
        in_specs=[pl.BlockSpec((None, dh, tq), lambda h, i: (h, 0, i)),
                  pl.BlockSpec((None, T, dh), lambda h, i: (h, 0, 0)),
                  pl.BlockSpec((None, T, dh), lambda h, i: (h, 0, 0)),
                  pl.BlockSpec((None, dh, tq), lambda h, i: (h, 0, i)),
                  pl.BlockSpec((None, 1, tq), lambda h, i: (h, 0, i)),
                  pl.BlockSpec((None, None, HALO, LANES), lambda h, i: (h, i, 0, 0))],
        out_specs=[pl.BlockSpec((None, dh, tq), lambda h, i: (h, 0, i)),
                   pl.BlockSpec((None, T, dh), lambda h, i: (h, 0, 0)),
                   pl.BlockSpec((None, T, dh), lambda h, i: (h, 0, 0))],
        out_shape=[jax.ShapeDtypeStruct((H, dh, T), F32), jax.ShapeDtypeStruct((H, T, dh), F32),
                   jax.ShapeDtypeStruct((H, T, dh), F32)],
        scratch_shapes=[pltpu.VMEM((dh, tq), F32), pltpu.VMEM((1, tq), F32), pltpu.VMEM((1, tq), F32),
                        pltpu.VMEM((dh, tq), BF16)],
        compiler_params=_cparams("parallel", "arbitrary"),
    )(q, k, v, dy, btot, first)


NEG = -1e30
SB_ZERO_BELOW = -105.0


def mla_fwd(q, k, v, *, name):
    H, dk, T = q.shape
    dv = v.shape[2]
    tq, tk = _attn_tiles(T)

    def body(q_ref, k_ref, v_ref, o_ref, l_ref, acc, m_s, l_s):
        acc[...] = jnp.zeros_like(acc)
        m_s[...] = jnp.full_like(m_s, NEG)
        l_s[...] = jnp.zeros_like(l_s)

        def block(k0, r0, masked):
            kb = k_ref[pl.ds(k0, tk), :]
            vb = v_ref[pl.ds(k0, tk), :]
            s = jnp.dot(kb, q_ref[:, r0:], preferred_element_type=F32)
            if masked:
                s = jnp.where(_after_diag(tk, tq - r0, False), s, NEG)
            m = m_s[:, r0:]
            m_new = jnp.maximum(m, jnp.max(s, axis=0, keepdims=True))
            p = jnp.exp(s - m_new)
            alpha = jnp.exp(m - m_new)
            l_s[:, r0:] = alpha * l_s[:, r0:] + jnp.sum(p, axis=0, keepdims=True)
            acc[:, r0:] = alpha * acc[:, r0:] + lax.dot_general(vb, p.astype(BF16), TN, preferred_element_type=F32)
            m_s[:, r0:] = m_new

        _causal_sweep(pl.program_id(1), tq, tk, block, descending=False)
        o_ref[...] = acc[...] / l_s[...]
        l_ref[...] = m_s[...] + jnp.log(l_s[...])

    return pl.pallas_call(
        body, name=name, grid=(H, T // tq),
        in_specs=[pl.BlockSpec((None, dk, tq), lambda h, i: (h, 0, i)),
                  pl.BlockSpec((None, T, dk), lambda h, i: (h, 0, 0)),
                  pl.BlockSpec((None, T, dv), lambda h, i: (h, 0, 0))],
        out_specs=[pl.BlockSpec((None, dv, tq), lambda h, i: (h, 0, i)),
                   pl.BlockSpec((None, 1, tq), lambda h, i: (h, 0, i))],
        out_shape=[jax.ShapeDtypeStruct((H, dv, T), F32), jax.ShapeDtypeStruct((H, 1, T), F32)],
        scratch_shapes=[pltpu.VMEM((dv, tq), F32), pltpu.VMEM((1, tq), F32), pltpu.VMEM((1, tq), F32)],
        compiler_params=_cparams("parallel", "parallel"),
    )(q, k, v)


def mla_bwd(q, k, v, do, o, lse, *, name):
    H, dk, T = q.shape
    dv = v.shape[2]
    tq, tk = _attn_tiles(T)

    def body(q_ref, k_ref, v_ref, do_ref, o_ref, l_ref, dq_ref, dk_ref, dv_ref, dq, delta, dob):
        @pl.when(pl.program_id(1) == 0)
        def _():
            dk_ref[...] = jnp.zeros_like(dk_ref)
            dv_ref[...] = jnp.zeros_like(dv_ref)

        dq[...] = jnp.zeros_like(dq)
        dov = do_ref[...].astype(F32)
        dob[...] = dov.astype(BF16)
        delta[...] = jnp.sum(dov * o_ref[...], axis=0, keepdims=True)

        def block(k0, r0, masked):
            kb = k_ref[pl.ds(k0, tk), :]
            vb = v_ref[pl.ds(k0, tk), :]
            qv = q_ref[:, r0:]
            dov_b = dob[:, r0:]
            s = jnp.dot(kb, qv, preferred_element_type=F32)
            p = jnp.exp(s - l_ref[:, r0:])
            if masked:
                p = jnp.where(_after_diag(tk, tq - r0, False), p, 0.0)
            dp = jnp.dot(vb, dov_b, preferred_element_type=F32)
            ds = (p * (dp - delta[:, r0:])).astype(BF16)
            dq[:, r0:] += lax.dot_general(kb, ds, TN, preferred_element_type=F32)
            dk_ref[pl.ds(k0, tk), :] += lax.dot_general(ds, qv, NT, preferred_element_type=F32)
            dv_ref[pl.ds(k0, tk), :] += lax.dot_general(p.astype(BF16), dov_b, NT, preferred_element_type=F32)

        _causal_sweep(pl.program_id(1), tq, tk, block, descending=False)
        dq_ref[...] = dq[...]

    return pl.pallas_call(
        body, name=name, grid=(H, T // tq),
        in_specs=[pl.BlockSpec((None, dk, tq), lambda h, i: (h, 0, i)),
                  pl.BlockSpec((None, T, dk), lambda h, i: (h, 0, 0)),
                  pl.BlockSpec((None, T, dv), lambda h, i: (h, 0, 0)),
                  pl.BlockSpec((None, dv, tq), lambda h, i: (h, 0, i)),
                  pl.BlockSpec((None, dv, tq), lambda h, i: (h, 0, i)),
                  pl.BlockSpec((None, 1, tq), lambda h, i: (h, 0, i))],
        out_specs=[pl.BlockSpec((None, dk, tq), lambda h, i: (h, 0, i)),
                   pl.BlockSpec((None, T, dk), lambda h, i: (h, 0, 0)),
                   pl.BlockSpec((None, T, dv), lambda h, i: (h, 0, 0))],
        out_shape=[jax.ShapeDtypeStruct((H, dk, T), F32), jax.ShapeDtypeStruct((H, T, dk), F32),
                   jax.ShapeDtypeStruct((H, T, dv), F32)],
        scratch_shapes=[pltpu.VMEM((dk, tq), F32), pltpu.VMEM((1, tq), F32), pltpu.VMEM((dv, tq), BF16)],
        compiler_params=_cparams("parallel", "arbitrary"),
    )(q, k, v, do, o, lse)


def _lane_pick(x, h):
    lane = lax.broadcasted_iota(jnp.int32, (1, x.shape[1]), 1)
    return jnp.sum(jnp.where(lane == h, x, 0.0), axis=1, keepdims=True)


def _row_pick(x, h):
    sub = lax.broadcasted_iota(jnp.int32, (x.shape[0], 1), 0)
    return jnp.sum(jnp.where(sub == h, x, 0.0), axis=0, keepdims=True)


def ssd_chunk_fn(*args):
    nh, ng = SSM_HEADS, SSM_GROUPS
    xs = args[:nh]
    bs = args[nh:nh + ng]
    cs = args[nh + ng:nh + 2 * ng]
    dt_raw = args[nh + 2 * ng]
    st = args[nh + 2 * ng + 1:nh + 2 * ng + 1 + nh]
    dt_bias, a_log, d_skip = args[nh + 2 * ng + 1 + nh:]
    L = dt_raw.shape[0]
    dt = _softplus(dt_raw + dt_bias)
    da = dt * (-jnp.exp(a_log))
    dcs = csum_rows(da)
    dcs_t = dcs.T
    total = jnp.sum(da, axis=0, keepdims=True)
    causal = lax.broadcasted_iota(jnp.int32, (L, L), 0) >= lax.broadcasted_iota(jnp.int32, (L, L), 1)
    cb = [mm_nt(cs[g], bs[g]) for g in range(ng)]
    ys, new_st = [], []
    for h in range(nh):
        g = h // (nh // ng)
        dcs_h = _lane_pick(dcs, h)
        dt_h = _lane_pick(dt, h)
        tot_h = _lane_pick(total, h)
        dsk_h = _lane_pick(d_skip, h)
        decay = jnp.exp(jnp.where(causal, dcs_h - _row_pick(dcs_t, h), NEG))
        xdt = xs[h] * dt_h
        y = mm_nn(cb[g] * decay, xdt)
        y = y + mm_nn(cs[g] * jnp.exp(dcs_h), st[h])
        ys.append(y + xs[h] * dsk_h)
        new_st.append(st[h] * jnp.exp(tot_h) + mm_tn(bs[g] * jnp.exp(tot_h - dcs_h), xdt))
    return tuple(ys) + tuple(new_st)


def ssd_fwd(x_hm, b_hm, c_hm, proj, dt_bias, a_log, d_skip, *, name):
    nh, T, P = x_hm.shape
    ng, N = b_hm.shape[0], b_hm.shape[2]
    L = SSM_CHUNK
    nc = T // L
    dtb = OFF_DT // LANES

    def body(x_ref, b_ref, c_ref, dt_ref, db_ref, al_ref, ds_ref, y_ref, s_ref, state):
        @pl.when(pl.program_id(0) == 0)
        def _():
            state[...] = jnp.zeros_like(state)

        s_ref[...] = state[...]
        args = ([x_ref[h] for h in range(nh)] + [b_ref[g] for g in range(ng)] + [c_ref[g] for g in range(ng)]
                + [dt_ref[...]] + [state[h] for h in range(nh)] + [db_ref[...], al_ref[...], ds_ref[...]])
        res = ssd_chunk_fn(*args)
        for h in range(nh):
            y_ref[h] = res[h]
            state[h] = res[nh + h]

    par = pl.BlockSpec((1, LANES), lambda i: (0, 0))
    return pl.pallas_call(
        body, name=name, grid=(nc,),
        in_specs=[pl.BlockSpec((nh, L, P), lambda i: (0, i, 0)), pl.BlockSpec((ng, L, N), lambda i: (0, i, 0)),
                  pl.BlockSpec((ng, L, N), lambda i: (0, i, 0)), pl.BlockSpec((L, LANES), lambda i: (i, dtb)),
                  par, par, par],
        out_specs=[pl.BlockSpec((nh, L, P), lambda i: (0, i, 0)),
                   pl.BlockSpec((None, nh, N, P), lambda i: (i, 0, 0, 0))],
        out_shape=[jax.ShapeDtypeStruct((nh, T, P), F32), jax.ShapeDtypeStruct((nc, nh, N, P), F32)],
        scratch_shapes=[pltpu.VMEM((nh, N, P), F32)],
        compiler_params=_cparams("arbitrary"),
    )(x_hm, b_hm, c_hm, proj, dt_bias, a_log, d_skip)


def ssd_bwd(x_hm, b_hm, c_hm, proj, states, dt_bias, a_log, d_skip, dy_hm, *, name):
    nh, T, P = x_hm.shape
    ng, N = b_hm.shape[0], b_hm.shape[2]
    L = SSM_CHUNK
    nc = T // L
    dtb = OFF_DT // LANES

    def body(x_ref, b_ref, c_ref, dt_ref, s_ref, db_ref, al_ref, ds_ref, dy_ref,
             dx_ref, dbm_ref, dcm_ref, ddt_ref, gdb_ref, gal_ref, gds_ref, dstate):
        @pl.when(pl.program_id(0) == 0)
        def _():
            dstate[...] = jnp.zeros_like(dstate)
            gdb_ref[...] = jnp.zeros_like(gdb_ref)
            gal_ref[...] = jnp.zeros_like(gal_ref)
            gds_ref[...] = jnp.zeros_like(gds_ref)

        args = ([x_ref[h] for h in range(nh)] + [b_ref[g] for g in range(ng)] + [c_ref[g] for g in range(ng)]
                + [dt_ref[...]] + [s_ref[h] for h in range(nh)] + [db_ref[...], al_ref[...], ds_ref[...]])
        _, vjp = jax.vjp(ssd_chunk_fn, *args)
        g = vjp(tuple([dy_ref[h] for h in range(nh)] + [dstate[h] for h in range(nh)]))
        for h in range(nh):
            dx_ref[h] = g[h]
        for gi in range(ng):
            dbm_ref[gi] = g[nh + gi]
            dcm_ref[gi] = g[nh + ng + gi]
        ddt_ref[...] = g[nh + 2 * ng]
        for h in range(nh):
            dstate[h] = g[nh + 2 * ng + 1 + h]
        gdb_ref[...] += g[-3]
        gal_ref[...] += g[-2]
        gds_ref[...] += g[-1]

    rev = lambda i: nc - 1 - i
    par = pl.BlockSpec((1, LANES), lambda i: (0, 0))
    return pl.pallas_call(
        body, name=name, grid=(nc,),
        in_specs=[pl.BlockSpec((nh, L, P), lambda i: (0, rev(i), 0)), pl.BlockSpec((ng, L, N), lambda i: (0, rev(i), 0)),
                  pl.BlockSpec((ng, L, N), lambda i: (0, rev(i), 0)), pl.BlockSpec((L, LANES), lambda i: (rev(i), dtb)),
                  pl.BlockSpec((None, nh, N, P), lambda i: (rev(i), 0, 0, 0)), par, par, par,
                  pl.BlockSpec((nh, L, P), lambda i: (0, rev(i), 0))],
        out_specs=[pl.BlockSpec((nh, L, P), lambda i: (0, rev(i), 0)), pl.BlockSpec((ng, L, N), lambda i: (0, rev(i), 0)),
                   pl.BlockSpec((ng, L, N), lambda i: (0, rev(i), 0)), pl.BlockSpec((L, LANES), lambda i: (rev(i), 0)),
                   par, par, par],
        out_shape=[jax.ShapeDtypeStruct((nh, T, P), F32), jax.ShapeDtypeStruct((ng, T, N), F32),
                   jax.ShapeDtypeStruct((ng, T, N), F32), jax.ShapeDtypeStruct((T, LANES), F32),
                   jax.ShapeDtypeStruct((1, LANES), F32), jax.ShapeDtypeStruct((1, LANES), F32),
                   jax.ShapeDtypeStruct((1, LANES), F32)],
        scratch_shapes=[pltpu.VMEM((nh, N, P), F32)],
        compiler_params=_cparams("arbitrary"),
    )(x_hm, b_hm, c_hm, proj, states, dt_bias, a_log, d_skip, dy_hm)


def loss_head(h, target, g, *, name, tile=512):
    T, C = h.shape
    tl = min(T, tile)

    def body(h_ref, t_ref, g_ref, dh_ref, dg_ref, ls_ref):
        @pl.when(pl.program_id(0) == 0)
        def _():
            dg_ref[...] = jnp.zeros_like(dg_ref)
            ls_ref[...] = jnp.zeros_like(ls_ref)

        (y,), vjp = jax.vjp(rms_fn, h_ref[...], g_ref[...])
        err = y - t_ref[...]
        ls_ref[...] += jnp.sum(err * err, axis=0, keepdims=True) * (0.5 / C)
        dh, dg = vjp((err * (1.0 / C),))
        dh_ref[...] = dh
        dg_ref[...] += dg

    row = pl.BlockSpec((tl, C), lambda i: (i, 0))
    par = pl.BlockSpec((1, C), lambda i: (0, 0))
    return pl.pallas_call(
        body, name=name, grid=(T // tl,), in_specs=[row, row, par], out_specs=[row, par, par],
        out_shape=[jax.ShapeDtypeStruct((T, C), F32), jax.ShapeDtypeStruct((1, C), F32),
                   jax.ShapeDtypeStruct((1, C), F32)],
        compiler_params=_cparams("arbitrary"),
    )(h, target, g)


def adamw(w, g, m, v, *, name):
    R, C = w.shape
    tr = R
    for d in range(8, min(R, 512) + 1, 8):
        if R % d == 0:
            tr = d
    c1 = 1.0 - ADAM_B1 ** ADAM_STEP
    c2 = 1.0 - ADAM_B2 ** ADAM_STEP

    def body(w_ref, g_ref, m_ref, v_ref, d_ref, nm_ref, nv_ref):
        gv = g_ref[...]
        nm = ADAM_B1 * m_ref[...] + (1.0 - ADAM_B1) * gv
        nv = ADAM_B2 * v_ref[...] + (1.0 - ADAM_B2) * (gv * gv)
        d_ref[...] = -ADAM_LR * ((nm / c1) / (jnp.sqrt(nv / c2) + ADAM_EPS) + ADAM_WD * w_ref[...])
        nm_ref[...] = nm
        nv_ref[...] = nv

    spec = pl.BlockSpec((tr, C), lambda i: (i, 0))
    return pl.pallas_call(
        body, name=name, grid=(R // tr,), in_specs=[spec] * 4, out_specs=[spec] * 3,
        out_shape=[jax.ShapeDtypeStruct((R, C), F32)] * 3,
        compiler_params=_cparams("parallel"),
    )(w, g, m, v)


MESH = pl.DeviceIdType.MESH
HBM_SPEC = pl.BlockSpec(memory_space=pltpu.HBM)


def _place():
    return lax.axis_index("x"), lax.axis_index("y"), lax.axis_index("c")


def allgather_blocks(mine, *, name):
    R = mine.shape[0]

    def body(x_ref, out_ref, send_sems, recv_sems, local_sem):
        x, y, c = _place()
        me, sibling = (x, y, c), (x, y, 1 - c)
        chips = [(1 - x, y), (x, 1 - y), (1 - x, 1 - y)]

        def slot(px, py, pc):
            return out_ref.at[4 * px + 2 * py + pc]

        def copy(k, block, to, src=None):
            return pltpu.make_async_remote_copy(
                src_ref=slot(*block) if src is None else src, dst_ref=slot(*block),
                send_sem=send_sems.at[k], recv_sem=recv_sems.at[k], device_id=to, device_id_type=MESH)

        own = pltpu.make_async_copy(x_ref, slot(*me), local_sem)
        own.start()
        first = [copy(0, me, sibling, src=x_ref)]
        first += [copy(1 + j, me, (*chip, c), src=x_ref) for j, chip in enumerate(chips)]
        for cp in first:
            cp.start()
        passed = [copy(4 + j, (*chip, c), sibling) for j, chip in enumerate(chips)]
        for j, chip in enumerate(chips):
            copy(1 + j, (*chip, c), me).wait_recv()
            passed[j].start()
        copy(0, sibling, me).wait_recv()
        for j, chip in enumerate(chips):
            copy(4 + j, (*chip, 1 - c), me).wait_recv()
        for cp in first + passed:
            cp.wait_send()
        own.wait()

    return pl.pallas_call(
        body, name=name, out_shape=jax.ShapeDtypeStruct((8, R, LANES), mine.dtype),
        in_specs=[HBM_SPEC], out_specs=HBM_SPEC,
        scratch_shapes=[pltpu.SemaphoreType.DMA((7,)), pltpu.SemaphoreType.DMA((7,)), pltpu.SemaphoreType.DMA],
    )(mine)


def allgather_direct(mine, *, name):
    R = mine.shape[0]

    def body(x_ref, out_ref, send_sems, recv_sems, local_sem):
        x, y, c = _place()
        own = pltpu.make_async_copy(x_ref, out_ref.at[4 * x + 2 * y + c], local_sem)
        own.start()
        sends = []
        for f in range(1, 8):
            fx, fy, fc = (f >> 2) & 1, (f >> 1) & 1, f & 1
            px, py, pc = jnp.where(fx, 1 - x, x), jnp.where(fy, 1 - y, y), jnp.where(fc, 1 - c, c)
            sends.append(pltpu.make_async_remote_copy(
                src_ref=x_ref, dst_ref=out_ref.at[4 * x + 2 * y + c], send_sem=send_sems.at[f - 1],
                recv_sem=recv_sems.at[f - 1], device_id=(px, py, pc), device_id_type=MESH))
        for cp in sends:
            cp.start()
        for f in range(1, 8):
            fx, fy, fc = (f >> 2) & 1, (f >> 1) & 1, f & 1
            px, py, pc = jnp.where(fx, 1 - x, x), jnp.where(fy, 1 - y, y), jnp.where(fc, 1 - c, c)
            pltpu.make_async_remote_copy(
                src_ref=x_ref, dst_ref=out_ref.at[4 * px + 2 * py + pc], send_sem=send_sems.at[f - 1],
                recv_sem=recv_sems.at[f - 1], device_id=(px, py, pc), device_id_type=MESH).wait_recv()
        for cp in sends:
            cp.wait_send()
        own.wait()

    return pl.pallas_call(
        body, name=name, out_shape=jax.ShapeDtypeStruct((8, R, LANES), mine.dtype),
        in_specs=[HBM_SPEC], out_specs=HBM_SPEC,
        scratch_shapes=[pltpu.SemaphoreType.DMA((7,)), pltpu.SemaphoreType.DMA((7,)), pltpu.SemaphoreType.DMA],
    )(mine)


def send_to_sibling(v, *, name):
    def body(v_ref, out_ref, send_sem, recv_sem):
        x, y, c = _place()
        cp = pltpu.make_async_remote_copy(src_ref=v_ref, dst_ref=out_ref, send_sem=send_sem, recv_sem=recv_sem,
                                          device_id=(x, y, 1 - c), device_id_type=MESH)
        cp.start()
        cp.wait()

    return pl.pallas_call(
        body, name=name, out_shape=jax.ShapeDtypeStruct(v.shape, v.dtype), in_specs=[HBM_SPEC], out_specs=HBM_SPEC,
        scratch_shapes=[pltpu.SemaphoreType.DMA, pltpu.SemaphoreType.DMA],
    )(v)


def chip_exchange(p, *, name):
    R = p.shape[1]

    def body(p_ref, out_ref, send_sems, recv_sems):
        x, y, c = _place()
        chips = [(1 - x, y), (x, 1 - y), (1 - x, 1 - y)]
        sends = [pltpu.make_async_remote_copy(
            src_ref=p_ref.at[2 * px + py], dst_ref=out_ref.at[j], send_sem=send_sems.at[j], recv_sem=recv_sems.at[j],
            device_id=(px, py, c), device_id_type=MESH) for j, (px, py) in enumerate(chips)]
        for cp in sends:
            cp.start()
        for cp in sends:
            cp.wait()

    return pl.pallas_call(
        body, name=name, out_shape=jax.ShapeDtypeStruct((3, R, LANES), p.dtype), in_specs=[HBM_SPEC],
        out_specs=HBM_SPEC,
        scratch_shapes=[pltpu.SemaphoreType.DMA((3,)), pltpu.SemaphoreType.DMA((3,))],
    )(p)


def add_blocks(terms, out_dtype, *, name, tile=1024):
    R = terms[0].shape[0]
    tr = R
    for d in range(16, min(R, tile) + 1, 16):
        if R % d == 0:
            tr = d

    def body(*refs):
        acc = refs[0][...].astype(F32)
        for ref in refs[1:-1]:
            acc = acc + ref[...].astype(F32)
        refs[-1][...] = acc.astype(out_dtype)

    spec = pl.BlockSpec((tr, LANES), lambda i: (i, 0))
    return pl.pallas_call(
        body, name=name, grid=(R // tr,), in_specs=[spec] * len(terms), out_specs=spec,
        out_shape=jax.ShapeDtypeStruct((R, LANES), out_dtype), compiler_params=_cparams("parallel"),
    )(*terms)


def _half_rows(arr, cc):
    hr = arr.shape[1] // 2
    return lax.dynamic_slice_in_dim(arr, cc * hr, hr, axis=1).reshape(-1)


def _flat_half(shards, cc, dtype):
    flat = jnp.concatenate([_half_rows(shards[n], cc).astype(dtype) for n in BIG])
    return flat.reshape(-1, LANES)


def _unflat_halves(flat_by_c, shapes):
    out, off = {}, 0
    for n in BIG:
        _, R, C = shapes[n]
        sz = 2 * (R // 2) * C
        out[n] = jnp.concatenate([flat_by_c[c][off:off + sz].reshape(2, R // 2, C) for c in range(2)], axis=1)
        off += sz
    return out


def _to_heads(a, nh):
    T = a.shape[0]
    return a.reshape(T, nh, a.shape[1] // nh).transpose(1, 0, 2)


def _from_heads(a):
    nh, T, d = a.shape
    return a.transpose(1, 0, 2).reshape(T, nh * d)


def _to_heads_t(a, nh):
    T = a.shape[0]
    return a.reshape(T, nh, a.shape[1] // nh).transpose(1, 2, 0)


def _from_heads_t(a):
    nh, d, T = a.shape
    return a.transpose(2, 0, 1).reshape(T, nh * d)


def _pad_cols(a, n):
    return jnp.pad(a, ((0, 0), (0, n - a.shape[1])))


def _pack_w_in(w):
    offs = [sum(IN_SPLITS[:i]) for i in range(len(IN_SPLITS) + 1)]
    sb, z, xbc, dt, cq, ckv, kr = [w[:, offs[i]:offs[i + 1]] for i in range(len(IN_SPLITS))]
    zeros = lambda n: jnp.zeros((w.shape[0], n), w.dtype)
    h = MLA_ROPE // 2
    kra = jnp.concatenate([zeros(MLA_NOPE), kr, zeros(LANES - MLA_QK)], axis=1)
    krb = jnp.concatenate([zeros(MLA_NOPE), -kr[:, h:], kr[:, :h], zeros(LANES - MLA_QK)], axis=1)
    return jnp.concatenate([sb, xbc, z, cq, ckv, _pad_cols(dt, LANES), kra, krb], axis=1)


def _unpack_gw_in(g):
    h = MLA_ROPE // 2
    ga, gb = g[:, OFF_KRA:OFF_KRA + LANES], g[:, OFF_KRB:OFF_KRB + LANES]
    gkr = ga[:, MLA_NOPE:MLA_QK] + jnp.concatenate([gb[:, MLA_NOPE + h:MLA_QK], -gb[:, MLA_NOPE:MLA_NOPE + h]], axis=1)
    return jnp.concatenate([g[:, OFF_SB:OFF_SB + 768], g[:, OFF_Z:OFF_Z + 512], g[:, OFF_XBC:OFF_XBC + 768],
                            g[:, OFF_DT:OFF_DT + 8], g[:, OFF_CQ:OFF_CQ + 256], g[:, OFF_CKV:OFF_CKV + 128], gkr], axis=1)


def _pack_w_uq(w):
    zeros = lambda n: jnp.zeros((w.shape[0], n), w.dtype)
    h = MLA_ROPE // 2
    pp, rr = [], []
    for i in range(MLA_HEADS):
        nope = w[:, MLA_QK * i:MLA_QK * i + MLA_NOPE]
        rope = w[:, MLA_QK * i + MLA_NOPE:MLA_QK * (i + 1)]
        pp += [nope, rope, zeros(LANES - MLA_QK)]
        rr += [zeros(MLA_NOPE), -rope[:, h:], rope[:, :h], zeros(LANES - MLA_QK)]
    return jnp.concatenate(pp, axis=1), jnp.concatenate(rr, axis=1)


def _unpack_gw_uq(gp, gr):
    h = MLA_ROPE // 2
    out = []
    for i in range(MLA_HEADS):
        b = LANES * i
        out.append(gp[:, b:b + MLA_NOPE])
        out.append(gp[:, b + MLA_NOPE:b + MLA_NOPE + h] + gr[:, b + MLA_NOPE + h:b + MLA_QK])
        out.append(gp[:, b + MLA_NOPE + h:b + MLA_QK] - gr[:, b + MLA_NOPE:b + MLA_NOPE + h])
    return jnp.concatenate(out, axis=1)


def _pack_w_ukv(w):
    zeros = lambda n: jnp.zeros((w.shape[0], n), w.dtype)
    kk, vv = [], []
    for i in range(MLA_HEADS):
        b = (MLA_NOPE + MLA_V) * i
        kk += [w[:, b:b + MLA_NOPE], zeros(LANES - MLA_NOPE)]
        vv.append(w[:, b + MLA_NOPE:b + MLA_NOPE + MLA_V])
    return jnp.concatenate(kk, axis=1), jnp.concatenate(vv, axis=1)


def _unpack_gw_ukv(gk, gv):
    out = []
    for i in range(MLA_HEADS):
        out += [gk[:, LANES * i:LANES * i + MLA_NOPE], gv[:, MLA_V * i:MLA_V * (i + 1)]]
    return jnp.concatenate(out, axis=1)


def _rope_tables(positions):
    inv_freq = 1.0 / (ROPE_THETA ** (jnp.arange(0, MLA_ROPE, 2, dtype=F32) / MLA_ROPE))
    ang = positions.astype(F32)[:, None] * inv_freq
    cos, sin = jnp.cos(ang), jnp.sin(ang)
    T = positions.shape[0]
    one, zero = jnp.ones((T, MLA_NOPE), F32), jnp.zeros((T, MLA_NOPE), F32)
    pad1, pad0 = jnp.ones((T, LANES - MLA_QK), F32), jnp.zeros((T, LANES - MLA_QK), F32)
    return jnp.concatenate([one, cos, cos, pad1], axis=1), jnp.concatenate([zero, sin, sin, pad0], axis=1)


def _row(v):
    return v.reshape(1, -1)


def _pad_row(v):
    return _pad_cols(v.reshape(1, -1), LANES)


def _layer_weights(full, small, li):
    p = {}
    p["w_in_p"] = _pack_w_in(full["w_in"][li])
    p["w_in_pt"] = p["w_in_p"].T
    p["wqp"], p["wqr"] = _pack_w_uq(full["mla_w_uq"][li])
    p["wkp"], p["wvp"] = _pack_w_ukv(full["mla_w_ukv"][li])
    p["w_out"] = full["w_out"][li]
    p["w_out_t"] = p["w_out"].T
    p["w_up"] = full["ffn_w_up"][li]
    p["w_up_t"] = p["w_up"].T
    p["w_down"] = full["ffn_w_down"][li]
    p["w_down_t"] = p["w_down"].T
    for n in ("mix_norm", "sb_out_norm", "ssm_conv_b", "ssm_out_norm", "mla_q_norm", "mla_kv_norm", "mla_out_norm",
              "ffn_norm", "ffn_conv_b"):
        p[n] = _row(small[n][li])
    for n in ("ssm_dt_bias", "ssm_a_log", "ssm_d"):
        p[n] = _pad_row(small[n][li])
    p["ssm_conv_w"] = small["ssm_conv_w"][li]
    p["ffn_conv_w"] = small["ffn_conv_w"][li]
    return p


def _layer_fwd(h, p, cos, sin, li):
    T = h.shape[0]
    nm = lambda s: "l%d_%s" % (li, s)
    s = {"h": h}
    (n1,) = rowwise(rms_fn, [h], [p["mix_norm"]], [(D_MODEL, BF16)], name=nm("mix_norm"))
    proj = matmul(n1, p["w_in_p"], name=nm("in_proj"))
    s["n1"], s["proj"] = n1, proj
    qkv = proj[:, OFF_SB:OFF_SB + 768]
    s["sb_q"] = _to_heads_t((qkv[:, 0:256] * (SB_DIM ** -0.5)).astype(BF16), SB_HEADS)
    s["sb_k"] = _to_heads(qkv[:, 256:512].astype(BF16), SB_HEADS)
    s["sb_v"] = _to_heads(qkv[:, 512:768].astype(BF16), SB_HEADS)
    y_sb_hm, s["sb_bt"], s["sb_first"] = sb_fwd(s["sb_q"], s["sb_k"], s["sb_v"], name=nm("sb_fwd"))
    s["y_sb"] = _from_heads_t(y_sb_hm)
    xbc = ssm_conv_act(proj, p["ssm_conv_w"], p["ssm_conv_b"], name=nm("ssm_conv"))
    s["x_hm"] = _to_heads(xbc[:, :SSM_INNER], SSM_HEADS)
    s["b_hm"] = _to_heads(xbc[:, SSM_INNER:SSM_INNER + 128], SSM_GROUPS)
    s["c_hm"] = _to_heads(xbc[:, SSM_INNER + 128:], SSM_GROUPS)
    y_ssm_hm, s["states"] = ssd_fwd(s["x_hm"], s["b_hm"], s["c_hm"], proj, p["ssm_dt_bias"], p["ssm_a_log"],
                                    p["ssm_d"], name=nm("ssd_fwd"))
    s["y_ssm"] = _from_heads(y_ssm_hm)
    rows = [(proj, 256, OFF_CQ // 256), (proj, 128, OFF_CKV // 128), (proj, 128, OFF_KRA // 128),
            (proj, 128, OFF_KRB // 128), cos, sin]
    qp, kp, vv = rowwise(mla_prep_fn, rows, [p["mla_q_norm"], p["mla_kv_norm"], p["wqp"], p["wqr"], p["wkp"], p["wvp"]],
                         [(512, BF16), (512, BF16), (256, BF16)], name=nm("mla_prep"))
    s["mla_q"], s["mla_k"], s["mla_v"] = _to_heads_t(qp, MLA_HEADS), _to_heads(kp, MLA_HEADS), _to_heads(vv, MLA_HEADS)
    s["mla_o"], s["mla_lse"] = mla_fwd(s["mla_q"], s["mla_k"], s["mla_v"], name=nm("mla_fwd"))
    s["y_mla"] = _from_heads_t(s["mla_o"])
    (cat,) = rowwise(merge_fn, [s["y_sb"], s["y_ssm"], (proj, 512, OFF_Z // 512), s["y_mla"]],
                     [p["sb_out_norm"], p["ssm_out_norm"], p["mla_out_norm"]], [(D_MODEL, BF16)], name=nm("merge"),
                     post=lambda a, b, c: (jnp.concatenate([a, b, c], axis=1),))
    s["cat"] = cat
    h1 = matmul(cat, p["w_out"], name=nm("out_proj"), residual=h)
    s["h1"] = h1
    (n2,) = rowwise(rms_fn, [h1], [p["ffn_norm"]], [(D_MODEL, BF16)], name=nm("ffn_norm"))
    up = matmul(n2, p["w_up"], name=nm("ffn_up"))
    act = ffn_act(up, p["ffn_conv_w"], p["ffn_conv_b"], name=nm("ffn_act"))
    s["n2"], s["up"], s["act"] = n2, up, act
    h2 = matmul(act, p["w_down"], name=nm("ffn_down"), residual=h1)
    return h2, s


def _layer_bwd(dh2, s, p, cos, sin, li):
    nm = lambda t: "l%d_%s" % (li, t)
    g = {}
    proj = s["proj"]
    g["ffn_w_down"] = matmul(s["act"], dh2, name=nm("g_w_down"), ta=True)
    d_act = matmul(dh2, p["w_down_t"], name=nm("d_act"), out_dtype=BF16)
    d_up_g, d_up_v, gwg, gwv, gbg, gbv = ffn_bwd_fused(s["up"], p["ffn_conv_w"], p["ffn_conv_b"], d_act,
                                                       name=nm("ffn_act_bwd"))
    g["ffn_conv_w"] = jnp.concatenate([gwg, gwv], axis=1)
    g["ffn_conv_b"] = jnp.concatenate([gbg[0], gbv[0]])
    g["ffn_w_up"] = jnp.concatenate([matmul(s["n2"], d_up_g, name=nm("g_w_up_gate"), ta=True),
                                     matmul(s["n2"], d_up_v, name=nm("g_w_up_val"), ta=True)], axis=1)
    d_n2 = matmul(d_up_g, p["w_up_t"][:D_FF], name=nm("d_n2_gate"))
    d_n2 = matmul(d_up_v, p["w_up_t"][D_FF:], name=nm("d_n2_val"), residual=d_n2)
    (dh1,), (gn,) = rowwise_bwd(rms_fn, [s["h1"]], [], [p["ffn_norm"]], [d_n2], [F32], name=nm("ffn_norm_bwd"),
                                add0=dh2)
    g["ffn_norm"] = gn[0]
    g["w_out"] = matmul(s["cat"], dh1, name=nm("g_w_out"), ta=True)
    d_cat = matmul(dh1, p["w_out_t"], name=nm("d_cat"))
    (d_ysb, d_yssm, d_z, d_ymla), (g1, g2, g3) = rowwise_bwd(
        merge_fn, [s["y_sb"], s["y_ssm"], (proj, 512, OFF_Z // 512), s["y_mla"]], [],
        [p["sb_out_norm"], p["ssm_out_norm"], p["mla_out_norm"]], [d_cat], [F32, F32, BF16, F32], name=nm("merge_bwd"),
        pre_ct=lambda d: (d[:, 0:256], d[:, 256:768], d[:, 768:1024]))
    g["sb_out_norm"], g["ssm_out_norm"], g["mla_out_norm"] = g1[0], g2[0], g3[0]
    dq, dk, dv = sb_bwd(s["sb_q"], s["sb_k"], s["sb_v"], _to_heads_t(d_ysb, SB_HEADS), s["sb_bt"], s["sb_first"], name=nm("sb_bwd"),
                        q_scale=SB_DIM ** -0.5)
    d_sb = jnp.concatenate([_from_heads_t(dq), _from_heads(dk), _from_heads(dv)], axis=1).astype(BF16)
    dqp, dkp, dvv = mla_bwd(s["mla_q"], s["mla_k"], s["mla_v"], _to_heads_t(d_ymla, MLA_HEADS), s["mla_o"], s["mla_lse"],
                            name=nm("mla_bwd"))
    rows = [(proj, 256, OFF_CQ // 256), (proj, 128, OFF_CKV // 128), (proj, 128, OFF_KRA // 128),
            (proj, 128, OFF_KRB // 128)]
    (d_cq, d_ckv, d_kra, d_krb), (gqn, gkvn, gwqp, gwqr, gwkp, gwvp) = rowwise_bwd(
        mla_prep_fn, rows, [cos, sin], [p["mla_q_norm"], p["mla_kv_norm"], p["wqp"], p["wqr"], p["wkp"], p["wvp"]],
        [_from_heads_t(dqp), _from_heads(dkp), _from_heads(dvv)], [BF16] * 4, name=nm("mla_prep_bwd"), tile=256)
    g["mla_q_norm"], g["mla_kv_norm"] = gqn[0], gkvn[0]
    g["mla_w_uq"] = _unpack_gw_uq(gwqp, gwqr)
    g["mla_w_ukv"] = _unpack_gw_ukv(gwkp, gwvp)
    dx_hm, db_hm, dc_hm, d_dt, gdb, gal, gds = ssd_bwd(
        s["x_hm"], s["b_hm"], s["c_hm"], proj, s["states"], p["ssm_dt_bias"], p["ssm_a_log"], p["ssm_d"],
        _to_heads(d_yssm, SSM_HEADS), name=nm("ssd_bwd"))
    g["ssm_dt_bias"], g["ssm_a_log"], g["ssm_d"] = gdb[0, :8], gal[0, :8], gds[0, :8]
    d_xbc_act = jnp.concatenate([_from_heads(dx_hm), _from_heads(db_hm), _from_heads(dc_hm)], axis=1)
    d_pre = ssm_conv_bwd_a(proj, p["ssm_conv_w"], p["ssm_conv_b"], d_xbc_act, name=nm("ssm_conv_bwd_a"))
    d_xbc, g["ssm_conv_w"], gscb = conv_bwd_b(d_pre, proj, OFF_XBC, p["ssm_conv_w"], name=nm("ssm_conv_bwd_b"),
                                              out_dtype=BF16, tc=256)
    g["ssm_conv_b"] = gscb[0]
    d_proj = jnp.concatenate([d_sb, d_xbc, d_z, d_cq, d_ckv, d_dt.astype(BF16), d_kra, d_krb], axis=1)
    g["w_in"] = _unpack_gw_in(matmul(s["n1"], d_proj, name=nm("g_w_in"), ta=True))
    d_n1 = matmul(d_proj, p["w_in_pt"], name=nm("d_n1"))
    (dh0,), (gm,) = rowwise_bwd(rms_fn, [s["h"]], [], [p["mix_norm"]], [d_n1], [F32], name=nm("mix_norm_bwd"),
                                add0=dh1)
    g["mix_norm"] = gm[0]
    return dh0, g


def kernel(x, positions, mix_norm, w_in, sb_out_norm, ssm_conv_w, ssm_conv_b, ssm_dt_bias, ssm_a_log, ssm_d, ssm_out_norm, mla_q_norm, mla_w_uq, mla_kv_norm, mla_w_ukv, mla_out_norm, w_out, ffn_norm, ffn_w_up, ffn_conv_w, ffn_conv_b, ffn_w_down, final_norm, loss_target, m_mix_norm, m_w_in, m_sb_out_norm, m_ssm_conv_w, m_ssm_conv_b, m_ssm_dt_bias, m_ssm_a_log, m_ssm_d, m_ssm_out_norm, m_mla_q_norm, m_mla_w_uq, m_mla_kv_norm, m_mla_w_ukv, m_mla_out_norm, m_w_out, m_ffn_norm, m_ffn_w_up, m_ffn_conv_w, m_ffn_conv_b, m_ffn_w_down, m_final_norm, v_mix_norm, v_w_in, v_sb_out_norm, v_ssm_conv_w, v_ssm_conv_b, v_ssm_dt_bias, v_ssm_a_log, v_ssm_d, v_ssm_out_norm, v_mla_q_norm, v_mla_w_uq, v_mla_kv_norm, v_mla_w_ukv, v_mla_out_norm, v_w_out, v_ffn_norm, v_ffn_w_up, v_ffn_conv_w, v_ffn_conv_b, v_ffn_w_down, v_final_norm):
    W = dict(mix_norm=mix_norm, w_in=w_in, sb_out_norm=sb_out_norm, ssm_conv_w=ssm_conv_w, ssm_conv_b=ssm_conv_b,
             ssm_dt_bias=ssm_dt_bias, ssm_a_log=ssm_a_log, ssm_d=ssm_d, ssm_out_norm=ssm_out_norm,
             mla_q_norm=mla_q_norm, mla_w_uq=mla_w_uq, mla_kv_norm=mla_kv_norm, mla_w_ukv=mla_w_ukv,
             mla_out_norm=mla_out_norm, w_out=w_out, ffn_norm=ffn_norm, ffn_w_up=ffn_w_up, ffn_conv_w=ffn_conv_w,
             ffn_conv_b=ffn_conv_b, ffn_w_down=ffn_w_down, final_norm=final_norm)
    M = dict(mix_norm=m_mix_norm, w_in=m_w_in, sb_out_norm=m_sb_out_norm, ssm_conv_w=m_ssm_conv_w,
             ssm_conv_b=m_ssm_conv_b, ssm_dt_bias=m_ssm_dt_bias, ssm_a_log=m_ssm_a_log, ssm_d=m_ssm_d,
             ssm_out_norm=m_ssm_out_norm, mla_q_norm=m_mla_q_norm, mla_w_uq=m_mla_w_uq, mla_kv_norm=m_mla_kv_norm,
             mla_w_ukv=m_mla_w_ukv, mla_out_norm=m_mla_out_norm, w_out=m_w_out, ffn_norm=m_ffn_norm,
             ffn_w_up=m_ffn_w_up, ffn_conv_w=m_ffn_conv_w, ffn_conv_b=m_ffn_conv_b, ffn_w_down=m_ffn_w_down,
             final_norm=m_final_norm)
    V = dict(mix_norm=v_mix_norm, w_in=v_w_in, sb_out_norm=v_sb_out_norm, ssm_conv_w=v_ssm_conv_w,
             ssm_conv_b=v_ssm_conv_b, ssm_dt_bias=v_ssm_dt_bias, ssm_a_log=v_ssm_a_log, ssm_d=v_ssm_d,
             ssm_out_norm=v_ssm_out_norm, mla_q_norm=v_mla_q_norm, mla_w_uq=v_mla_w_uq, mla_kv_norm=v_mla_kv_norm,
             mla_w_ukv=v_mla_w_ukv, mla_out_norm=v_mla_out_norm, w_out=v_w_out, ffn_norm=v_ffn_norm,
             ffn_w_up=v_ffn_w_up, ffn_conv_w=v_ffn_conv_w, ffn_conv_b=v_ffn_conv_b, ffn_w_down=v_ffn_w_down,
             final_norm=v_final_norm)
    depth = mix_norm.shape[0]
    cx, cy, cc = _place()
    chip = 2 * cx + cy
    T = x.shape[1]

    shard_shapes = {n: W[n].shape for n in BIG}
    gathered = allgather_blocks(_flat_half(W, cc, BF16), name="gather_weights")
    full = {}
    per_chip = [_unflat_halves([gathered[2 * k + c].reshape(-1) for c in range(2)], shard_shapes) for k in range(4)]
    for n in BIG:
        full[n] = jnp.concatenate([per_chip[k][n] for k in range(4)], axis=BIG_AXIS[n])
    conv_full = {}
    small = {n: W[n] for n in SMALL_REPL}
    cw_flat = jnp.concatenate([W[n].reshape(-1) for n in SMALL_SHARD])
    cw_rows = -(-cw_flat.shape[0] // (8 * LANES)) * 8
    cw_all = allgather_direct(jnp.pad(cw_flat, (0, cw_rows * LANES - cw_flat.shape[0])).reshape(cw_rows, LANES),
                              name="gather_conv_taps")
    off = 0
    for n in SMALL_SHARD:
        sz = W[n].size
        conv_full[n] = jnp.concatenate(
            [cw_all[2 * k].reshape(-1)[off:off + sz].reshape(W[n].shape) for k in range(4)], axis=2)
        off += sz
    small.update(conv_full)

    cos, sin = _rope_tables(positions[0])
    params = [_layer_weights(full, small, li) for li in range(depth)]

    h = x[0]
    saved = []
    for li in range(depth):
        h, s = _layer_fwd(h, params[li], cos, sin, li)
        saved.append(s)
    dh, g_final, loss_lanes = loss_head(h, loss_target[0], _row(final_norm), name="loss_head")

    grads = [None] * depth
    for li in reversed(range(depth)):
        dh, grads[li] = _layer_bwd(dh, saved[li], params[li], cos, sin, li)
    grad_x = dh[None]
    G = {n: jnp.stack([grads[li][n] for li in range(depth)]) for n in WEIGHTS if n != "final_norm"}
    G["final_norm"] = g_final[0]

    def shard_major(n):
        a = G[n]
        ax = BIG_AXIS[n]
        parts = jnp.split(a, 4, axis=ax)
        return parts

    by_chip = {n: shard_major(n) for n in BIG}

    def flat_for(k, which):
        return _flat_half({n: by_chip[n][k] for n in BIG}, which, BF16)

    mine_first = jnp.stack([flat_for(k, cc) for k in range(4)])
    for_sibling = jnp.stack([flat_for(k, 1 - cc) for k in range(4)])
    R = mine_first.shape[1]
    from_sibling = send_to_sibling(for_sibling.reshape(4 * R, LANES), name="grads_to_sibling")
    pair = add_blocks([mine_first.reshape(4 * R, LANES), from_sibling], BF16, name="grads_pair_sum").reshape(4, R, LANES)
    others = chip_exchange(pair, name="grads_chip_exchange")
    own = lax.dynamic_index_in_dim(pair, chip, 0, keepdims=False)
    half = add_blocks([own, others[0], others[1], others[2]], F32, name="grads_chip_sum")
    other = send_to_sibling(half, name="grads_pair_swap")
    by_core = [jnp.where(cc == 0, half, other), jnp.where(cc == 0, other, half)]
    g_big = _unflat_halves([a.reshape(-1) for a in by_core], shard_shapes)

    small_list = [G[n].reshape(-1) for n in SMALL_REPL] + [G[n].reshape(-1) for n in SMALL_SHARD]
    small_list.append(jnp.sum(loss_lanes).reshape(1))
    sm = jnp.concatenate(small_list)
    n_small = sm.shape[0]
    sm_rows = -(-n_small // (16 * LANES)) * 16
    sm_all = allgather_direct(jnp.pad(sm, (0, sm_rows * LANES - n_small)).reshape(sm_rows, LANES), name="gather_small")
    sm_sum = add_blocks([sm_all[d] for d in range(8)], F32, name="small_sum").reshape(-1)
    g_small, off = {}, 0
    for n in SMALL_REPL:
        g_small[n] = sm_sum[off:off + W[n].size].reshape(W[n].shape)
        off += W[n].size
    for n in SMALL_SHARD:
        full_shape = conv_full[n].shape
        sz = conv_full[n].size
        gfull = sm_sum[off:off + sz].reshape(full_shape)
        width = W[n].shape[2]
        g_small[n] = lax.dynamic_slice_in_dim(gfull, chip * width, width, axis=2)
        off += sz
    loss = sm_sum[off]

    grad_out, delta, new_m, new_v = {}, {}, {}, {}
    for n in BIG:
        shp = W[n].shape
        two_d = lambda a: a.reshape(shp[0] * shp[1], shp[2])
        d, nm_, nv_ = adamw(two_d(W[n]), two_d(g_big[n]), two_d(M[n]), two_d(V[n]), name="adamw_" + n)
        grad_out[n], delta[n], new_m[n], new_v[n] = g_big[n], d.reshape(shp), nm_.reshape(shp), nv_.reshape(shp)
    small_names = SMALL_REPL + SMALL_SHARD

    def flat_small(d):
        f = jnp.concatenate([d[n].reshape(-1) for n in small_names])
        rows = -(-f.shape[0] // (8 * LANES)) * 8
        return jnp.pad(f, (0, rows * LANES - f.shape[0])).reshape(rows, LANES)

    vpad = flat_small(V)
    d, nm_, nv_ = adamw(flat_small(W), flat_small(g_small), flat_small(M), vpad, name="adamw_small")
    off = 0
    for n in small_names:
        sz = W[n].size
        grad_out[n] = g_small[n]
        delta[n] = d.reshape(-1)[off:off + sz].reshape(W[n].shape)
        new_m[n] = nm_.reshape(-1)[off:off + sz].reshape(W[n].shape)
        new_v[n] = nv_.reshape(-1)[off:off + sz].reshape(W[n].shape)
        off += sz

    return (loss, grad_x, *[grad_out[n] for n in WEIGHTS], *[delta[n] for n in WEIGHTS],
            *[new_m[n] for n in WEIGHTS], *[new_v[n] for n in WEIGHTS])
```

```python
import functools
import math

import jax
import jax.numpy as jnp
from jax import lax
from jax.experimental import pallas as pl
from jax.experimental.pallas import tpu as pltpu

F32 = jnp.float32
BF16 = jnp.bfloat16

EPS = 1e-6
D_MODEL = 1024
SB_HEADS, SB_DIM = 4, 64
SSM_HEADS, SSM_DIM, SSM_GROUPS, SSM_STATE, SSM_CHUNK = 8, 64, 2, 64, 128
SSM_INNER = SSM_HEADS * SSM_DIM
SSM_CONV_DIM = SSM_INNER + 2 * SSM_GROUPS * SSM_STATE
MLA_HEADS, MLA_NOPE, MLA_ROPE, MLA_V = 4, 64, 32, 64
MLA_QK = MLA_NOPE + MLA_ROPE
MLA_SCALE = MLA_QK ** -0.5
ROPE_THETA = 10000.0
D_FF = 2816
IN_SPLITS = (768, 512, 768, 8, 256, 128, 32)

OFF_Z, OFF_XBC, OFF_CQ, OFF_CKV, OFF_DT, OFF_KRA, OFF_KRB = 0, 512, 1280, 1536, 1664, 1792, 1920
D_REST = 2048
LANES = 128

ADAM_LR, ADAM_B1, ADAM_B2, ADAM_EPS, ADAM_WD, ADAM_STEP = 0.001, 0.9, 0.999, 1e-08, 0.01, 10

V7X_VMEM_LIMIT = 48 * 1024 * 1024

NT = (((1,), (1,)), ((), ()))
TN = (((0,), (0,)), ((), ()))

BIG = ("w_in", "mla_w_uq", "mla_w_ukv", "w_out", "ffn_w_up", "ffn_w_down")
BIG_AXIS = {"w_in": 2, "mla_w_uq": 2, "mla_w_ukv": 2, "w_out": 1, "ffn_w_up": 2, "ffn_w_down": 1}
SMALL_REPL = ("mix_norm", "sb_out_norm", "ssm_conv_b", "ssm_dt_bias", "ssm_a_log", "ssm_d", "ssm_out_norm",
              "mla_q_norm", "mla_kv_norm", "mla_out_norm", "ffn_norm", "ffn_conv_b", "final_norm")
SMALL_SHARD = ("ssm_conv_w", "ffn_conv_w")
WEIGHTS = ("mix_norm", "w_in", "sb_out_norm", "ssm_conv_w", "ssm_conv_b", "ssm_dt_bias", "ssm_a_log", "ssm_d",
           "ssm_out_norm", "mla_q_norm", "mla_w_uq", "mla_kv_norm", "mla_w_ukv", "mla_out_norm", "w_out", "ffn_norm",
           "ffn_w_up", "ffn_conv_w", "ffn_conv_b", "ffn_w_down", "final_norm")


def _cparams(*sem):
    return pltpu.CompilerParams(dimension_semantics=sem if sem else None, vmem_limit_bytes=V7X_VMEM_LIMIT)


def _pick(n, target, mult=LANES):
    best = None
    for d in range(mult, min(n, target) + 1, mult):
        if n % d == 0:
            best = d
    return best or n


def _sigmoid(x):
    return 1.0 / (1.0 + jnp.exp(-x))


def _softplus(x):
    ax = jnp.where(x > 0, x, -x)
    return jnp.where(x > 0, x, 0.0) + jnp.log(1.0 + jnp.exp(-ax))


def _rms(x, g):
    return x * lax.rsqrt(jnp.mean(x * x, axis=-1, keepdims=True) + EPS) * g


def _raw_nn(a, b):
    return jnp.dot(a.astype(BF16), b.astype(BF16), preferred_element_type=F32)


def _raw_nt(a, b):
    return lax.dot_general(a.astype(BF16), b.astype(BF16), NT, preferred_element_type=F32)


def _raw_tn(a, b):
    return lax.dot_general(a.astype(BF16), b.astype(BF16), TN, preferred_element_type=F32)


@jax.custom_vjp
def mm_nn(a, b):
    return _raw_nn(a, b)


mm_nn.defvjp(lambda a, b: (_raw_nn(a, b), (a, b)),
             lambda r, ct: (_raw_nt(ct, r[1]), _raw_tn(r[0], ct)))


@jax.custom_vjp
def mm_nt(a, b):
    return _raw_nt(a, b)


mm_nt.defvjp(lambda a, b: (_raw_nt(a, b), (a, b)),
             lambda r, ct: (_raw_nn(ct, r[1]), _raw_tn(ct, r[0])))


@jax.custom_vjp
def mm_tn(a, b):
    return _raw_tn(a, b)


mm_tn.defvjp(lambda a, b: (_raw_tn(a, b), (a, b)),
             lambda r, ct: (_raw_nt(r[1], ct), _raw_nn(r[0], ct)))


def _split_dot(x, m, terms):
    acc = None
    r = x
    for t in range(terms):
        xt = r.astype(BF16)
        d = jnp.dot(xt, m, preferred_element_type=F32)
        acc = d if acc is None else acc + d
        if t + 1 < terms:
            r = r - xt.astype(F32)
    return acc


def _tri_dot(tri, x, terms=3):
    parts = []
    r = x
    for t in range(terms):
        xt = r.astype(BF16)
        parts.append(xt)
        if t + 1 < terms:
            r = r - xt.astype(F32)
    return jnp.dot(jnp.concatenate([tri] * terms, axis=1), jnp.concatenate(parts, axis=0),
                   preferred_element_type=F32)


def _tri(n, cmp):
    r = lax.broadcasted_iota(jnp.int32, (n, n), 0)
    c = lax.broadcasted_iota(jnp.int32, (n, n), 1)
    return cmp(r, c).astype(BF16)


@jax.custom_vjp
def csum_rows(x):
    return _tri_dot(_tri(x.shape[0], lambda r, c: r >= c), x)


csum_rows.defvjp(lambda x: (csum_rows(x), None),
                 lambda _, ct: (_tri_dot(_tri(ct.shape[0], lambda r, c: r <= c), ct),))


def matmul(a, b, *, name, out_dtype=F32, ta=False, tb=False, b_k0=0, residual=None):
    if ta:
        K, M = a.shape
    else:
        M, K = a.shape
    N = b.shape[0] if tb else b.shape[1]
    tm = _pick(M, 1408)
    tn = _pick(N, 1408)
    tk = _pick(K, 1408)
    nk = K // tk
    kb0 = b_k0 // tk
    assert b_k0 % tk == 0 and (tb or b_k0 == 0)
    has_res = residual is not None

    def body(*refs):
        if has_res:
            a_ref, b_ref, r_ref, o_ref, acc = refs
        else:
            a_ref, b_ref, o_ref, acc = refs
        k = pl.program_id(2)

        @pl.when(k == 0)
        def _():
            acc[...] = jnp.zeros_like(acc)

        av = a_ref[...].astype(BF16)
        bv = b_ref[...].astype(BF16)
        if ta:
            acc[...] += lax.dot_general(av, bv, TN, preferred_element_type=F32)
        elif tb:
            acc[...] += lax.dot_general(av, bv, NT, preferred_element_type=F32)
        else:
            acc[...] += jnp.dot(av, bv, preferred_element_type=F32)

        @pl.when(k == nk - 1)
        def _():
            r = acc[...]
            if has_res:
                r = r + r_ref[...].astype(F32)
            o_ref[...] = r.astype(o_ref.dtype)

    a_spec = pl.BlockSpec((tk, tm), lambda i, j, k: (k, i)) if ta else pl.BlockSpec((tm, tk), lambda i, j, k: (i, k))
    b_spec = pl.BlockSpec((tn, tk), lambda i, j, k: (j, kb0 + k)) if tb else pl.BlockSpec((tk, tn), lambda i, j, k: (k, j))
    in_specs = [a_spec, b_spec]
    args = [a, b]
    if has_res:
        in_specs.append(pl.BlockSpec((tm, tn), lambda i, j, k: (i, j)))
        args.append(residual)
    return pl.pallas_call(
        body, name=name, grid=(M // tm, N // tn, nk),
        in_specs=in_specs, out_specs=pl.BlockSpec((tm, tn), lambda i, j, k: (i, j)),
        out_shape=jax.ShapeDtypeStruct((M, N), out_dtype),
        scratch_shapes=[pltpu.VMEM((tm, tn), F32)],
        compiler_params=_cparams("parallel", "parallel", "arbitrary"),
    )(*args)


def _row_spec(entry, tl):
    if isinstance(entry, tuple):
        arr, width, cb = entry
        return arr, pl.BlockSpec((tl, width), lambda i, cb=cb: (i, cb))
    return entry, pl.BlockSpec((tl, entry.shape[1]), lambda i: (i, 0))


def _rows_T(entry):
    return (entry[0] if isinstance(entry, tuple) else entry).shape[0]


def rowwise(fn, rows, params, outs, *, name, tile=512, post=None):
    T = _rows_T(rows[0])
    tl = min(T, tile)
    nr, npar = len(rows), len(params)

    def body(*refs):
        r = [ref[...].astype(F32) for ref in refs[:nr]]
        p = [ref[...].astype(F32) for ref in refs[nr:nr + npar]]
        res = fn(*r, *p)
        if post is not None:
            res = post(*res)
        for o_ref, val in zip(refs[nr + npar:], res):
            o_ref[...] = val.astype(o_ref.dtype)

    arrs, specs = [], []
    for e in rows:
        a, s = _row_spec(e, tl)
        arrs.append(a)
        specs.append(s)
    for p in params:
        arrs.append(p)
        specs.append(pl.BlockSpec(p.shape, lambda i: (0, 0)))
    res = pl.pallas_call(
        body, name=name, grid=(T // tl,), in_specs=specs,
        out_specs=[pl.BlockSpec((tl, c), lambda i: (i, 0)) for c, _ in outs],
        out_shape=[jax.ShapeDtypeStruct((T, c), dt) for c, dt in outs],
        compiler_params=_cparams("parallel"),
    )(*arrs)
    return res


def rowwise_bwd(fn, rows, nd_rows, params, cts, grad_dtypes, *, name, tile=512, pre_ct=None, add0=None):
    T = _rows_T(rows[0])
    tl = min(T, tile)
    nr, nn, npar, nc = len(rows), len(nd_rows), len(params), len(cts)
    has_add = add0 is not None

    def body(*refs):
        pos = 0
        r = [ref[...].astype(F32) for ref in refs[pos:pos + nr]]
        pos += nr
        nd = [ref[...].astype(F32) for ref in refs[pos:pos + nn]]
        pos += nn
        p = [ref[...].astype(F32) for ref in refs[pos:pos + npar]]
        pos += npar
        c = [ref[...].astype(F32) for ref in refs[pos:pos + nc]]
        pos += nc
        if has_add:
            addv = refs[pos][...].astype(F32)
            pos += 1
        rg_refs = refs[pos:pos + nr]
        pg_refs = refs[pos + nr:pos + nr + npar]
        if pre_ct is not None:
            c = list(pre_ct(*c))
        _, vjp = jax.vjp(lambda *a: fn(*a[:nr], *nd, *a[nr:]), *r, *p)
        g = vjp(tuple(c))
        for j, ref in enumerate(rg_refs):
            val = g[j]
            if has_add and j == 0:
                val = val + addv
            ref[...] = val.astype(ref.dtype)
        if npar:
            @pl.when(pl.program_id(0) == 0)
            def _():
                for ref in pg_refs:
                    ref[...] = jnp.zeros_like(ref)
            for j, ref in enumerate(pg_refs):
                ref[...] += g[nr + j]

    arrs, specs = [], []
    widths = []
    for e in list(rows) + list(nd_rows):
        a, s = _row_spec(e, tl)
        arrs.append(a)
        specs.append(s)
        widths.append(s.block_shape[1])
    for p in params:
        arrs.append(p)
        specs.append(pl.BlockSpec(p.shape, lambda i: (0, 0)))
    for e in cts:
        a, s = _row_spec(e, tl)
        arrs.append(a)
        specs.append(s)
    if has_add:
        a, s = _row_spec(add0, tl)
        arrs.append(a)
        specs.append(s)
    out_specs = [pl.BlockSpec((tl, widths[j]), lambda i: (i, 0)) for j in range(nr)]
    out_shape = [jax.ShapeDtypeStruct((T, widths[j]), grad_dtypes[j]) for j in range(nr)]
    out_specs += [pl.BlockSpec(p.shape, lambda i: (0, 0)) for p in params]
    out_shape += [jax.ShapeDtypeStruct(p.shape, F32) for p in params]
    res = pl.pallas_call(
        body, name=name, grid=(T // tl,), in_specs=specs, out_specs=out_specs, out_shape=out_shape,
        compiler_params=_cparams("arbitrary"),
    )(*arrs)
    return list(res[:nr]), list(res[nr:])


def rms_fn(h, g):
    return (_rms(h, g),)


def merge_fn(ysb, yssm, z, ymla, g_sb, g_ssm, g_mla):
    ya = _rms(ysb, g_sb)
    yb = _rms(yssm * (z * _sigmoid(z)), g_ssm)
    yc = _rms(ymla, g_mla)
    return ya, yb, yc


def mla_prep_fn(cq, ckv, kra, krb, cos, sin, qn, kvn, wqp, wqr, wkp, wvp):
    cos4 = jnp.concatenate([cos] * MLA_HEADS, axis=1)
    sin4 = jnp.concatenate([sin] * MLA_HEADS, axis=1)
    nq = _rms(cq, qn)
    q = (mm_nn(nq, wqp) * cos4 + mm_nn(nq, wqr) * sin4) * MLA_SCALE
    nkv = _rms(ckv, kvn)
    kpe = kra * cos + krb * sin
    k = mm_nn(nkv, wkp) + jnp.concatenate([kpe] * MLA_HEADS, axis=1)
    v = mm_nn(nkv, wvp)
    return q, k, v


HALO = 8


def _prev_halo_spec(tl, tc, col_of):
    return pl.BlockSpec((HALO, tc), lambda i, j: (jnp.maximum(i * (tl // HALO) - 1, 0), col_of(j)))


def _fill_prev(buf, x_ref, halo_ref, i):
    buf[0:HALO, :] = jnp.where(i > 0, halo_ref[...].astype(F32), 0.0)
    buf[HALO:, :] = x_ref[...].astype(F32)


def _conv_from(buf, w_ref, b_ref, K, tl):
    acc = b_ref[...].astype(F32) + jnp.zeros((tl, buf.shape[1]), F32)
    for k in range(K):
        acc = acc + buf[pl.ds(HALO - (K - 1 - k), tl), :] * w_ref[k:k + 1, :].astype(F32)
    return acc


def ssm_conv_act(proj, w, b, *, name, tile=512, tc=256):
    T = proj.shape[0]
    K, C = w.shape
    tl = min(T, tile)
    c0 = OFF_XBC // tc

    def body(x_ref, halo_ref, w_ref, b_ref, o_ref, buf):
        _fill_prev(buf, x_ref, halo_ref, pl.program_id(0))
        u = _conv_from(buf, w_ref, b_ref, K, tl)
        o_ref[...] = u * _sigmoid(u)

    return pl.pallas_call(
        body, name=name, grid=(T // tl, C // tc),
        in_specs=[pl.BlockSpec((tl, tc), lambda i, j: (i, c0 + j)), _prev_halo_spec(tl, tc, lambda j: c0 + j),
                  pl.BlockSpec((K, tc), lambda i, j: (0, j)), pl.BlockSpec((1, tc), lambda i, j: (0, j))],
        out_specs=pl.BlockSpec((tl, tc), lambda i, j: (i, j)),
        out_shape=jax.ShapeDtypeStruct((T, C), F32),
        scratch_shapes=[pltpu.VMEM((tl + HALO, tc), F32)],
        compiler_params=_cparams("parallel", "parallel"),
    )(proj, proj, w, b)


def ssm_conv_bwd_a(proj, w, b, d_out, *, name, tile=512, tc=256):
    T = proj.shape[0]
    K, C = w.shape
    tl = min(T, tile)
    c0 = OFF_XBC // tc

    def body(x_ref, halo_ref, w_ref, b_ref, d_ref, o_ref, buf):
        _fill_prev(buf, x_ref, halo_ref, pl.program_id(0))
        u = _conv_from(buf, w_ref, b_ref, K, tl)
        s = _sigmoid(u)
        o_ref[...] = d_ref[...].astype(F32) * (s * (1.0 + u * (1.0 - s)))

    return pl.pallas_call(
        body, name=name, grid=(T // tl, C // tc),
        in_specs=[pl.BlockSpec((tl, tc), lambda i, j: (i, c0 + j)), _prev_halo_spec(tl, tc, lambda j: c0 + j),
                  pl.BlockSpec((K, tc), lambda i, j: (0, j)), pl.BlockSpec((1, tc), lambda i, j: (0, j)),
                  pl.BlockSpec((tl, tc), lambda i, j: (i, j))],
        out_specs=pl.BlockSpec((tl, tc), lambda i, j: (i, j)),
        out_shape=jax.ShapeDtypeStruct((T, C), F32),
        scratch_shapes=[pltpu.VMEM((tl + HALO, tc), F32)],
        compiler_params=_cparams("parallel", "parallel"),
    )(proj, proj, w, b, d_out)


def ffn_act(up, w, b, *, name, tile=512, tc=1408):
    T = up.shape[0]
    K = w.shape[0]
    tl = min(T, tile)
    nj = D_FF // tc

    def body(xg_ref, hg_ref, xv_ref, hv_ref, wg_ref, wv_ref, bg_ref, bv_ref, o_ref, bufg, bufv):
        i = pl.program_id(0)
        _fill_prev(bufg, xg_ref, hg_ref, i)
        _fill_prev(bufv, xv_ref, hv_ref, i)
        gate = _conv_from(bufg, wg_ref, bg_ref, K, tl)
        val = _conv_from(bufv, wv_ref, bv_ref, K, tl)
        o_ref[...] = (gate * _sigmoid(gate) * val).astype(o_ref.dtype)

    return pl.pallas_call(
        body, name=name, grid=(T // tl, nj),
        in_specs=[pl.BlockSpec((tl, tc), lambda i, j: (i, j)), _prev_halo_spec(tl, tc, lambda j: j),
                  pl.BlockSpec((tl, tc), lambda i, j: (i, nj + j)), _prev_halo_spec(tl, tc, lambda j: nj + j),
                  pl.BlockSpec((K, tc), lambda i, j: (0, j)), pl.BlockSpec((K, tc), lambda i, j: (0, nj + j)),
                  pl.BlockSpec((1, tc), lambda i, j: (0, j)), pl.BlockSpec((1, tc), lambda i, j: (0, nj + j))],
        out_specs=pl.BlockSpec((tl, tc), lambda i, j: (i, j)),
        out_shape=jax.ShapeDtypeStruct((T, D_FF), BF16),
        scratch_shapes=[pltpu.VMEM((tl + HALO, tc), F32), pltpu.VMEM((tl + HALO, tc), F32)],
        compiler_params=_cparams("parallel", "parallel"),
    )(up, up, up, up, w, w, b, b)


def ffn_bwd_fused(up, w, b, d_act, *, name, tile=512, tc=256):
    T = up.shape[0]
    K = w.shape[0]
    tl = min(T, tile)
    nj = D_FF // tc
    nblk = T // HALO
    ext = tl + HALO

    def body(xg, hgp, hgn, xv, hvp, hvn, wg, wv, bg, bv, d, dn, og, ov, dwg, dwv, dbg, dbv, bufg, bufv, dgb, dvb):
        i = pl.program_id(1)
        last = pl.num_programs(1) - 1

        def fill(buf, x_ref, prev_ref, next_ref):
            buf[0:HALO, :] = jnp.where(i > 0, prev_ref[...].astype(F32), 0.0)
            buf[HALO:HALO + tl, :] = x_ref[...].astype(F32)
            buf[HALO + tl:, :] = jnp.where(i < last, next_ref[...].astype(F32), 0.0)

        def conv_ext(buf, w_ref, b_ref):
            acc = b_ref[...].astype(F32) + jnp.zeros((ext, tc), F32)
            for k in range(K):
                acc = acc + buf[pl.ds(HALO - (K - 1 - k), ext), :] * w_ref[k:k + 1, :].astype(F32)
            return acc

        fill(bufg, xg, hgp, hgn)
        fill(bufv, xv, hvp, hvn)
        gate = conv_ext(bufg, wg, bg)
        val = conv_ext(bufv, wv, bv)
        dd = jnp.concatenate([d[...].astype(F32), jnp.where(i < last, dn[...].astype(F32)[0:HALO], 0.0)], axis=0)
        s = _sigmoid(gate)
        dgb[...] = dd * val * (s * (1.0 + gate * (1.0 - s)))
        dvb[...] = dd * (gate * s)

        @pl.when(i == 0)
        def _():
            for ref in (dwg, dwv, dbg, dbv):
                ref[...] = jnp.zeros_like(ref)

        for dbuf, xbuf, w_ref, o_ref, dw_ref, db_ref in ((dgb, bufg, wg, og, dwg, dbg), (dvb, bufv, wv, ov, dwv, dbv)):
            xin = xbuf[HALO:HALO + tl, :]
            dx = jnp.zeros((tl, tc), F32)
            for k in range(K):
                sft = K - 1 - k
                shifted = dbuf[pl.ds(sft, tl), :]
                dx = dx + shifted * w_ref[k:k + 1, :].astype(F32)
                dw_ref[k:k + 1, :] += jnp.sum(shifted * xin, axis=0, keepdims=True)
            db_ref[...] += jnp.sum(dbuf[0:tl, :], axis=0, keepdims=True)
            o_ref[...] = dx.astype(o_ref.dtype)

    prev = lambda i: jnp.maximum(i * (tl // HALO) - 1, 0)
    nxt = lambda i: jnp.minimum((i + 1) * (tl // HALO), nblk - 1)

    def x_specs(col):
        return [pl.BlockSpec((tl, tc), lambda j, i: (i, col(j))), pl.BlockSpec((HALO, tc), lambda j, i: (prev(i), col(j))),
                pl.BlockSpec((HALO, tc), lambda j, i: (nxt(i), col(j)))]

    gcol, vcol = (lambda j: j), (lambda j: nj + j)
    in_specs = (x_specs(gcol) + x_specs(vcol)
                + [pl.BlockSpec((K, tc), lambda j, i: (0, j)), pl.BlockSpec((K, tc), lambda j, i: (0, nj + j)),
                   pl.BlockSpec((1, tc), lambda j, i: (0, j)), pl.BlockSpec((1, tc), lambda j, i: (0, nj + j)),
                   pl.BlockSpec((tl, tc), lambda j, i: (i, j)),
                   pl.BlockSpec((2 * HALO, tc), lambda j, i: (jnp.minimum((i + 1) * (tl // (2 * HALO)), nblk // 2 - 1), j))])
    row_out = pl.BlockSpec((tl, tc), lambda j, i: (i, j))
    w_out = pl.BlockSpec((K, tc), lambda j, i: (0, j))
    b_out = pl.BlockSpec((1, tc), lambda j, i: (0, j))
    return pl.pallas_call(
        body, name=name, grid=(nj, T // tl), in_specs=in_specs,
        out_specs=[row_out, row_out, w_out, w_out, b_out, b_out],
        out_shape=[jax.ShapeDtypeStruct((T, D_FF), BF16)] * 2 + [jax.ShapeDtypeStruct((K, D_FF), F32)] * 2
        + [jax.ShapeDtypeStruct((1, D_FF), F32)] * 2,
        scratch_shapes=[pltpu.VMEM((tl + 2 * HALO, tc), F32)] * 2 + [pltpu.VMEM((ext, tc), F32)] * 2,
        compiler_params=_cparams("parallel", "arbitrary"),
    )(up, up, up, up, up, up, w, w, b, b, d_act, d_act)


def conv_bwd_b(du, x, x_off, w, *, name, out_dtype, tile=512, tc=256):
    T, C = du.shape
    K = w.shape[0]
    tl = min(T, tile)
    c0 = x_off // tc
    nblk = T // HALO

    def body(du_ref, nx_ref, x_ref, w_ref, dx_ref, dw_ref, db_ref, dbuf):
        i = pl.program_id(1)
        last = pl.num_programs(1) - 1
        d = du_ref[...].astype(F32)
        dbuf[0:tl, :] = d
        dbuf[tl:, :] = jnp.where(i < last, nx_ref[...].astype(F32), 0.0)

        @pl.when(i == 0)
        def _():
            dw_ref[...] = jnp.zeros_like(dw_ref)
            db_ref[...] = jnp.zeros_like(db_ref)

        xin = x_ref[...].astype(F32)
        dx = jnp.zeros((tl, tc), F32)
        for k in range(K):
            s = K - 1 - k
            shifted = dbuf[pl.ds(s, tl), :]
            dx = dx + shifted * w_ref[k:k + 1, :].astype(F32)
            dw_ref[k:k + 1, :] += jnp.sum(shifted * xin, axis=0, keepdims=True)
        db_ref[...] += jnp.sum(d, axis=0, keepdims=True)
        dx_ref[...] = dx.astype(dx_ref.dtype)

    return pl.pallas_call(
        body, name=name, grid=(C // tc, T // tl),
        in_specs=[pl.BlockSpec((tl, tc), lambda j, i: (i, j)),
                  pl.BlockSpec((HALO, tc), lambda j, i: (jnp.minimum((i + 1) * (tl // HALO), nblk - 1), j)),
                  pl.BlockSpec((tl, tc), lambda j, i: (i, c0 + j)),
                  pl.BlockSpec((K, tc), lambda j, i: (0, j))],
        out_specs=[pl.BlockSpec((tl, tc), lambda j, i: (i, j)), pl.BlockSpec((K, tc), lambda j, i: (0, j)),
                   pl.BlockSpec((1, tc), lambda j, i: (0, j))],
        out_shape=[jax.ShapeDtypeStruct((T, C), out_dtype), jax.ShapeDtypeStruct((K, C), F32),
                   jax.ShapeDtypeStruct((1, C), F32)],
        scratch_shapes=[pltpu.VMEM((tl + HALO, tc), F32)],
        compiler_params=_cparams("parallel", "arbitrary"),
    )(du, du, x, w)


def _attn_tiles(T):
    return min(T, 1024), min(T, 256)


def _after_diag(keys, queries, strict):
    d = lax.broadcasted_iota(jnp.int32, (keys, queries), 1) - lax.broadcasted_iota(jnp.int32, (keys, queries), 0)
    return d > 0 if strict else d >= 0


def _log_gates(z):
    l1p = jnp.log(1.0 + jnp.exp(-jnp.abs(z)))
    a = jnp.minimum(z, 0.0) - l1p
    return a, a - z


def _causal_sweep(i, tq, tk, block, descending, keep_going=None, first_block=None):
    nb = tq // tk
    n_full = i * nb

    def band():
        order = reversed(range(nb)) if descending else range(nb)
        for bb in order:
            block(pl.multiple_of(i * tq + bb * tk, tk), bb * tk, True)

    def full():
        if descending and keep_going is not None:
            def step(j):
                block(pl.multiple_of((n_full - 1 - j) * tk, tk), 0, False)
                return j + 1
            done = lax.while_loop(lambda j: jnp.logical_and(j < n_full, keep_going()), step, jnp.int32(0))
            return n_full - done

        def step(j, c):
            kb = (n_full - 1 - j) if descending else j
            block(pl.multiple_of(kb * tk, tk), 0, False)
            return c
        lax.fori_loop(0 if first_block is None else first_block, n_full, step, 0)
        return None

    if descending:
        band()
        return full()
    full()
    band()
    return None


def sb_fwd(q, k, v, *, name):
    H, dh, T = q.shape
    tq, tk = _attn_tiles(T)

    def body(q_ref, k_ref, v_ref, y_ref, bt_ref, first_ref, acc, run):
        acc[...] = jnp.zeros_like(acc)
        run[...] = jnp.zeros_like(run)
        u_after = _tri(tk, lambda r, c: r < c)

        def block(k0, r0, masked):
            kb = k_ref[pl.ds(k0, tk), :]
            vb = v_ref[pl.ds(k0, tk), :]
            z = jnp.dot(kb, q_ref[:, r0:], preferred_element_type=F32)
            a, b = _log_gates(z)
            if masked:
                valid = _after_diag(tk, tq - r0, True)
                b = jnp.where(valid, b, 0.0)
            w = jnp.exp(a + _tri_dot(u_after, b, 2) + run[:, r0:])
            if masked:
                w = jnp.where(valid, w, 0.0)
            acc[:, r0:] += lax.dot_general(vb, w.astype(BF16), TN, preferred_element_type=F32)
            run[:, r0:] += jnp.sum(b, axis=0, keepdims=True)

        first = _causal_sweep(pl.program_id(1), tq, tk, block, descending=True,
                              keep_going=lambda: jnp.max(run[...]) >= SB_ZERO_BELOW)
        y_ref[...] = acc[...]
        bt_ref[...] = run[...]
        first_ref[...] = jnp.zeros(first_ref.shape, F32) + first.astype(F32)

    return pl.pallas_call(
        body, name=name, grid=(H, T // tq),
        in_specs=[pl.BlockSpec((None, dh, tq), lambda h, i: (h, 0, i)),
                  pl.BlockSpec((None, T, dh), lambda h, i: (h, 0, 0)),
                  pl.BlockSpec((None, T, dh), lambda h, i: (h, 0, 0))],
        out_specs=[pl.BlockSpec((None, dh, tq), lambda h, i: (h, 0, i)),
                   pl.BlockSpec((None, 1, tq), lambda h, i: (h, 0, i)),
                   pl.BlockSpec((None, None, HALO, LANES), lambda h, i: (h, i, 0, 0))],
        out_shape=[jax.ShapeDtypeStruct((H, dh, T), F32), jax.ShapeDtypeStruct((H, 1, T), F32),
                   jax.ShapeDtypeStruct((H, T // tq, HALO, LANES), F32)],
        scratch_shapes=[pltpu.VMEM((dh, tq), F32), pltpu.VMEM((1, tq), F32)],
        compiler_params=_cparams("parallel", "parallel"),
    )(q, k, v)


def sb_bwd(q, k, v, dy, btot, first, *, name, q_scale):
    H, dh, T = q.shape
    tq, tk = _attn_tiles(T)

    def body(q_ref, k_ref, v_ref, dy_ref, bt_ref, first_ref, dq_ref, dk_ref, dv_ref, dq, pb, pg, dyb):
        @pl.when(pl.program_id(1) == 0)
        def _():
            dk_ref[...] = jnp.zeros_like(dk_ref)
            dv_ref[...] = jnp.zeros_like(dv_ref)

        dq[...] = jnp.zeros_like(dq)
        pb[...] = jnp.zeros_like(pb)
        pg[...] = jnp.zeros_like(pg)
        dyb[...] = dy_ref[...].astype(BF16)
        u_upto = _tri(tk, lambda r, c: r >= c)
        u_before = _tri(tk, lambda r, c: r > c)

        def block(k0, r0, masked):
            kb = k_ref[pl.ds(k0, tk), :]
            vb = v_ref[pl.ds(k0, tk), :]
            qv = q_ref[:, r0:]
            dyv = dyb[:, r0:]
            z = jnp.dot(kb, qv, preferred_element_type=F32)
            a, b = _log_gates(z)
            if masked:
                valid = _after_diag(tk, tq - r0, True)
                b = jnp.where(valid, b, 0.0)
            w = jnp.exp(a + (bt_ref[:, r0:] - pb[:, r0:] - _tri_dot(u_upto, b, 2)))
            if masked:
                w = jnp.where(valid, w, 0.0)
            g = w * jnp.dot(vb, dyv, preferred_element_type=F32)
            dz = g - jnp.exp(a) * (g + pg[:, r0:] + _tri_dot(u_before, g, 2))
            if masked:
                dz = jnp.where(valid, dz, 0.0)
            dz = dz.astype(BF16)
            dq[:, r0:] += lax.dot_general(kb, dz, TN, preferred_element_type=F32)
            dk_ref[pl.ds(k0, tk), :] += lax.dot_general(dz, qv, NT, preferred_element_type=F32)
            dv_ref[pl.ds(k0, tk), :] += lax.dot_general(w.astype(BF16), dyv, NT, preferred_element_type=F32)
            pb[:, r0:] += jnp.sum(b, axis=0, keepdims=True)
            pg[:, r0:] += jnp.sum(g, axis=0, keepdims=True)

        i = pl.program_id(1)
        first = jnp.clip(jnp.max(first_ref[...]).astype(jnp.int32), 0, i * (tq // tk))
        _causal_sweep(i, tq, tk, block, descending=False, first_block=first)
        dq_ref[...] = dq[...] * q_scale

    return pl.pallas_call(
        body, name=name, grid=(H, T // tq),
        in_specs=[pl.BlockSpec((None, dh, tq), lambda h, i: (h, 0, i)),
                  pl.BlockSpec((None, T, dh), lambda h, i: (h, 0, 0)),
                  pl.BlockSpec((None, T, dh), lambda h, i: (h, 0, 0)),
                  pl.BlockSpec((None, dh, tq), lambda h, i: (h, 0, i)),
                  pl.BlockSpec((None, 1, tq), lambda h, i: (h, 0, i)),
                  pl.BlockSpec((None, None, HALO, LANES), lambda h, i: (h, i, 0, 0))],
        out_specs=[pl.BlockSpec((None, dh, tq), lambda h, i: (h, 0, i)),
                   pl.BlockSpec((None, T, dh), lambda h, i: (h, 0, 0)),
                   pl.BlockSpec((None, T, dh), lambda h, i: (h, 0, 0))],
        out_shape=[jax.ShapeDtypeStruct((H, dh, T), F32), jax.ShapeDtypeStruct((H, T, dh), F32),
                   jax.ShapeDtypeStruct((H, T, dh), F32)],
        scratch_shapes=[pltpu.VMEM((dh, tq), F32), pltpu.VMEM((1, tq), F32), pltpu.VMEM((1, tq), F32),
                        pltpu.VMEM((dh, tq), BF16)],
        compiler_params=_cparams("parallel", "arbitrary"),
    )(q, k, v, dy, btot, first)


NEG = -1e30
SB_ZERO_BELOW = -105.0


def mla_fwd(q, k, v, *, name):
    H, dk, T = q.shape
    dv = v.shape[1] // H
    tq, tk = _attn_tiles(T)

    def body(q_ref, k_ref, v_ref, o_ref, l_ref, acc, m_s, l_s):
        acc[...] = jnp.zeros_like(acc)
        m_s[...] = jnp.full_like(m_s, NEG)
        l_s[...] = jnp.zeros_like(l_s)

        def block(k0, r0, masked):
            kb = k_ref[pl.ds(k0, tk), :]
            vb = v_ref[pl.ds(k0, tk), :]
            s = jnp.dot(kb, q_ref[:, r0:], preferred_element_type=F32)
            if masked:
                s = jnp.where(_after_diag(tk, tq - r0, False), s, NEG)
            m = m_s[:, r0:]
            m_new = jnp.maximum(m, jnp.max(s, axis=0, keepdims=True))
            p = jnp.exp(s - m_new)
            alpha = jnp.exp(m - m_new)
            l_s[:, r0:] = alpha * l_s[:, r0:] + jnp.sum(p, axis=0, keepdims=True)
            acc[:, r0:] = alpha * acc[:, r0:] + lax.dot_general(vb, p.astype(BF16), TN, preferred_element_type=F32)
            m_s[:, r0:] = m_new

        _causal_sweep(pl.program_id(1), tq, tk, block, descending=False)
        o_ref[...] = acc[...] / l_s[...]
        l_ref[...] = m_s[...] + jnp.log(l_s[...])

    return pl.pallas_call(
        body, name=name, grid=(H, T // tq),
        in_specs=[pl.BlockSpec((None, dk, tq), lambda h, i: (h, 0, i)),
                  pl.BlockSpec((T, dk), lambda h, i: (0, h)),
                  pl.BlockSpec((T, dv), lambda h, i: (0, h))],
        out_specs=[pl.BlockSpec((None, dv, tq), lambda h, i: (h, 0, i)),
                   pl.BlockSpec((None, 1, tq), lambda h, i: (h, 0, i))],
        out_shape=[jax.ShapeDtypeStruct((H, dv, T), F32), jax.ShapeDtypeStruct((H, 1, T), F32)],
        scratch_shapes=[pltpu.VMEM((dv, tq), F32), pltpu.VMEM((1, tq), F32), pltpu.VMEM((1, tq), F32)],
        compiler_params=_cparams("parallel", "parallel"),
    )(q, k, v)


def mla_bwd(q, k, v, do, o, lse, *, name):
    H, dk, T = q.shape
    dv = v.shape[1] // H
    tq, tk = _attn_tiles(T)

    def body(q_ref, k_ref, v_ref, do_ref, o_ref, l_ref, dq_ref, dk_ref, dv_ref, dq, delta, dob):
        @pl.when(pl.program_id(1) == 0)
        def _():
            dk_ref[...] = jnp.zeros_like(dk_ref)
            dv_ref[...] = jnp.zeros_like(dv_ref)

        dq[...] = jnp.zeros_like(dq)
        dov = do_ref[...].astype(F32)
        dob[...] = dov.astype(BF16)
        delta[...] = jnp.sum(dov * o_ref[...], axis=0, keepdims=True)

        def block(k0, r0, masked):
            kb = k_ref[pl.ds(k0, tk), :]
            vb = v_ref[pl.ds(k0, tk), :]
            qv = q_ref[:, r0:]
            dov_b = dob[:, r0:]
            s = jnp.dot(kb, qv, preferred_element_type=F32)
            p = jnp.exp(s - l_ref[:, r0:])
            if masked:
                p = jnp.where(_after_diag(tk, tq - r0, False), p, 0.0)
            dp = jnp.dot(vb, dov_b, preferred_element_type=F32)
            ds = (p * (dp - delta[:, r0:])).astype(BF16)
            dq[:, r0:] += lax.dot_general(kb, ds, TN, preferred_element_type=F32)
            dk_ref[pl.ds(k0, tk), :] += lax.dot_general(ds, qv, NT, preferred_element_type=F32)
            dv_ref[pl.ds(k0, tk), :] += lax.dot_general(p.astype(BF16), dov_b, NT, preferred_element_type=F32)

        _causal_sweep(pl.program_id(1), tq, tk, block, descending=False)
        dq_ref[...] = dq[...]

    return pl.pallas_call(
        body, name=name, grid=(H, T // tq),
        in_specs=[pl.BlockSpec((None, dk, tq), lambda h, i: (h, 0, i)),
                  pl.BlockSpec((T, dk), lambda h, i: (0, h)),
                  pl.BlockSpec((T, dv), lambda h, i: (0, h)),
                  pl.BlockSpec((None, dv, tq), lambda h, i: (h, 0, i)),
                  pl.BlockSpec((None, dv, tq), lambda h, i: (h, 0, i)),
                  pl.BlockSpec((None, 1, tq), lambda h, i: (h, 0, i))],
        out_specs=[pl.BlockSpec((None, dk, tq), lambda h, i: (h, 0, i)),
                   pl.BlockSpec((T, dk), lambda h, i: (0, h)),
                   pl.BlockSpec((T, dv), lambda h, i: (0, h))],
        out_shape=[jax.ShapeDtypeStruct((H, dk, T), F32), jax.ShapeDtypeStruct((T, H * dk), F32),
                   jax.ShapeDtypeStruct((T, H * dv), F32)],
        scratch_shapes=[pltpu.VMEM((dk, tq), F32), pltpu.VMEM((1, tq), F32), pltpu.VMEM((dv, tq), BF16)],
        compiler_params=_cparams("parallel", "arbitrary"),
    )(q, k, v, do, o, lse)


def _lane_pick(x, h):
    lane = lax.broadcasted_iota(jnp.int32, (1, x.shape[1]), 1)
    return jnp.sum(jnp.where(lane == h, x, 0.0), axis=1, keepdims=True)


def _row_pick(x, h):
    sub = lax.broadcasted_iota(jnp.int32, (x.shape[0], 1), 0)
    return jnp.sum(jnp.where(sub == h, x, 0.0), axis=0, keepdims=True)


def ssd_chunk_fn(*args):
    nh, ng = SSM_HEADS, SSM_GROUPS
    xs = args[:nh]
    bs = args[nh:nh + ng]
    cs = args[nh + ng:nh + 2 * ng]
    dt_raw = args[nh + 2 * ng]
    st = args[nh + 2 * ng + 1:nh + 2 * ng + 1 + nh]
    dt_bias, a_log, d_skip = args[nh + 2 * ng + 1 + nh:]
    L = dt_raw.shape[0]
    dt = _softplus(dt_raw + dt_bias)
    da = dt * (-jnp.exp(a_log))
    dcs = csum_rows(da)
    dcs_t = dcs.T
    total = jnp.sum(da, axis=0, keepdims=True)
    causal = lax.broadcasted_iota(jnp.int32, (L, L), 0) >= lax.broadcasted_iota(jnp.int32, (L, L), 1)
    cb = [mm_nt(cs[g], bs[g]) for g in range(ng)]
    ys, new_st = [], []
    for h in range(nh):
        g = h // (nh // ng)
        dcs_h = _lane_pick(dcs, h)
        dt_h = _lane_pick(dt, h)
        tot_h = _lane_pick(total, h)
        dsk_h = _lane_pick(d_skip, h)
        decay = jnp.exp(jnp.where(causal, dcs_h - _row_pick(dcs_t, h), NEG))
        xdt = xs[h] * dt_h
        y = mm_nn(cb[g] * decay, xdt)
        y = y + mm_nn(cs[g] * jnp.exp(dcs_h), st[h])
        ys.append(y + xs[h] * dsk_h)
        new_st.append(st[h] * jnp.exp(tot_h) + mm_tn(bs[g] * jnp.exp(tot_h - dcs_h), xdt))
    return tuple(ys) + tuple(new_st)


def ssd_fwd(x_hm, b_hm, c_hm, proj, dt_bias, a_log, d_skip, *, name):
    nh, T, P = x_hm.shape
    ng, N = b_hm.shape[0], b_hm.shape[2]
    L = SSM_CHUNK
    nc = T // L
    dtb = OFF_DT // LANES

    def body(x_ref, b_ref, c_ref, dt_ref, db_ref, al_ref, ds_ref, y_ref, s_ref, state):
        @pl.when(pl.program_id(0) == 0)
        def _():
            state[...] = jnp.zeros_like(state)

        s_ref[...] = state[...]
        args = ([x_ref[h] for h in range(nh)] + [b_ref[g] for g in range(ng)] + [c_ref[g] for g in range(ng)]
                + [dt_ref[...]] + [state[h] for h in range(nh)] + [db_ref[...], al_ref[...], ds_ref[...]])
        res = ssd_chunk_fn(*args)
        for h in range(nh):
            y_ref[h] = res[h]
            state[h] = res[nh + h]

    par = pl.BlockSpec((1, LANES), lambda i: (0, 0))
    return pl.pallas_call(
        body, name=name, grid=(nc,),
        in_specs=[pl.BlockSpec((nh, L, P), lambda i: (0, i, 0)), pl.BlockSpec((ng, L, N), lambda i: (0, i, 0)),
                  pl.BlockSpec((ng, L, N), lambda i: (0, i, 0)), pl.BlockSpec((L, LANES), lambda i: (i, dtb)),
                  par, par, par],
        out_specs=[pl.BlockSpec((nh, L, P), lambda i: (0, i, 0)),
                   pl.BlockSpec((None, nh, N, P), lambda i: (i, 0, 0, 0))],
        out_shape=[jax.ShapeDtypeStruct((nh, T, P), F32), jax.ShapeDtypeStruct((nc, nh, N, P), F32)],
        scratch_shapes=[pltpu.VMEM((nh, N, P), F32)],
        compiler_params=_cparams("arbitrary"),
    )(x_hm, b_hm, c_hm, proj, dt_bias, a_log, d_skip)


def ssd_bwd(x_hm, b_hm, c_hm, proj, states, dt_bias, a_log, d_skip, dy_hm, *, name):
    nh, T, P = x_hm.shape
    ng, N = b_hm.shape[0], b_hm.shape[2]
    L = SSM_CHUNK
    nc = T // L
    dtb = OFF_DT // LANES

    def body(x_ref, b_ref, c_ref, dt_ref, s_ref, db_ref, al_ref, ds_ref, dy_ref,
             dx_ref, dbm_ref, dcm_ref, ddt_ref, gdb_ref, gal_ref, gds_ref, dstate):
        @pl.when(pl.program_id(0) == 0)
        def _():
            dstate[...] = jnp.zeros_like(dstate)
            gdb_ref[...] = jnp.zeros_like(gdb_ref)
            gal_ref[...] = jnp.zeros_like(gal_ref)
            gds_ref[...] = jnp.zeros_like(gds_ref)

        args = ([x_ref[h] for h in range(nh)] + [b_ref[g] for g in range(ng)] + [c_ref[g] for g in range(ng)]
                + [dt_ref[...]] + [s_ref[h] for h in range(nh)] + [db_ref[...], al_ref[...], ds_ref[...]])
        _, vjp = jax.vjp(ssd_chunk_fn, *args)
        g = vjp(tuple([dy_ref[h] for h in range(nh)] + [dstate[h] for h in range(nh)]))
        for h in range(nh):
            dx_ref[h] = g[h]
        for gi in range(ng):
            dbm_ref[gi] = g[nh + gi]
            dcm_ref[gi] = g[nh + ng + gi]
        ddt_ref[...] = g[nh + 2 * ng]
        for h in range(nh):
            dstate[h] = g[nh + 2 * ng + 1 + h]
        gdb_ref[...] += g[-3]
        gal_ref[...] += g[-2]
        gds_ref[...] += g[-1]

    rev = lambda i: nc - 1 - i
    par = pl.BlockSpec((1, LANES), lambda i: (0, 0))
    return pl.pallas_call(
        body, name=name, grid=(nc,),
        in_specs=[pl.BlockSpec((nh, L, P), lambda i: (0, rev(i), 0)), pl.BlockSpec((ng, L, N), lambda i: (0, rev(i), 0)),
                  pl.BlockSpec((ng, L, N), lambda i: (0, rev(i), 0)), pl.BlockSpec((L, LANES), lambda i: (rev(i), dtb)),
                  pl.BlockSpec((None, nh, N, P), lambda i: (rev(i), 0, 0, 0)), par, par, par,
                  pl.BlockSpec((nh, L, P), lambda i: (0, rev(i), 0))],
        out_specs=[pl.BlockSpec((nh, L, P), lambda i: (0, rev(i), 0)), pl.BlockSpec((ng, L, N), lambda i: (0, rev(i), 0)),
                   pl.BlockSpec((ng, L, N), lambda i: (0, rev(i), 0)), pl.BlockSpec((L, LANES), lambda i: (rev(i), 0)),
                   par, par, par],
        out_shape=[jax.ShapeDtypeStruct((nh, T, P), F32), jax.ShapeDtypeStruct((ng, T, N), F32),
                   jax.ShapeDtypeStruct((ng, T, N), F32), jax.ShapeDtypeStruct((T, LANES), F32),
                   jax.ShapeDtypeStruct((1, LANES), F32), jax.ShapeDtypeStruct((1, LANES), F32),
                   jax.ShapeDtypeStruct((1, LANES), F32)],
        scratch_shapes=[pltpu.VMEM((nh, N, P), F32)],
        compiler_params=_cparams("arbitrary"),
    )(x_hm, b_hm, c_hm, proj, states, dt_bias, a_log, d_skip, dy_hm)


def loss_head(h, target, g, *, name, tile=512):
    T, C = h.shape
    tl = min(T, tile)

    def body(h_ref, t_ref, g_ref, dh_ref, dg_ref, ls_ref):
        @pl.when(pl.program_id(0) == 0)
        def _():
            dg_ref[...] = jnp.zeros_like(dg_ref)
            ls_ref[...] = jnp.zeros_like(ls_ref)

        (y,), vjp = jax.vjp(rms_fn, h_ref[...], g_ref[...])
        err = y - t_ref[...]
        ls_ref[...] += jnp.sum(err * err, axis=0, keepdims=True) * (0.5 / C)
        dh, dg = vjp((err * (1.0 / C),))
        dh_ref[...] = dh
        dg_ref[...] += dg

    row = pl.BlockSpec((tl, C), lambda i: (i, 0))
    par = pl.BlockSpec((1, C), lambda i: (0, 0))
    return pl.pallas_call(
        body, name=name, grid=(T // tl,), in_specs=[row, row, par], out_specs=[row, par, par],
        out_shape=[jax.ShapeDtypeStruct((T, C), F32), jax.ShapeDtypeStruct((1, C), F32),
                   jax.ShapeDtypeStruct((1, C), F32)],
        compiler_params=_cparams("arbitrary"),
    )(h, target, g)


def adamw(w, g, m, v, *, name):
    R, C = w.shape
    tr = R
    for d in range(8, min(R, 512) + 1, 8):
        if R % d == 0:
            tr = d
    c1 = 1.0 - ADAM_B1 ** ADAM_STEP
    c2 = 1.0 - ADAM_B2 ** ADAM_STEP

    def body(w_ref, g_ref, m_ref, v_ref, d_ref, nm_ref, nv_ref):
        gv = g_ref[...]
        nm = ADAM_B1 * m_ref[...] + (1.0 - ADAM_B1) * gv
        nv = ADAM_B2 * v_ref[...] + (1.0 - ADAM_B2) * (gv * gv)
        d_ref[...] = -ADAM_LR * ((nm / c1) / (jnp.sqrt(nv / c2) + ADAM_EPS) + ADAM_WD * w_ref[...])
        nm_ref[...] = nm
        nv_ref[...] = nv

    spec = pl.BlockSpec((tr, C), lambda i: (i, 0))
    return pl.pallas_call(
        body, name=name, grid=(R // tr,), in_specs=[spec] * 4, out_specs=[spec] * 3,
        out_shape=[jax.ShapeDtypeStruct((R, C), F32)] * 3,
        compiler_params=_cparams("parallel"),
    )(w, g, m, v)


MESH = pl.DeviceIdType.MESH
HBM_SPEC = pl.BlockSpec(memory_space=pltpu.HBM)


def _place():
    return lax.axis_index("x"), lax.axis_index("y"), lax.axis_index("c")


def allgather_blocks(mine, *, name):
    R = mine.shape[0]

    def body(x_ref, out_ref, send_sems, recv_sems, local_sem):
        x, y, c = _place()
        me, sibling = (x, y, c), (x, y, 1 - c)
        chips = [(1 - x, y), (x, 1 - y), (1 - x, 1 - y)]

        def slot(px, py, pc):
            return out_ref.at[4 * px + 2 * py + pc]

        def copy(k, block, to, src=None):
            return pltpu.make_async_remote_copy(
                src_ref=slot(*block) if src is None else src, dst_ref=slot(*block),
                send_sem=send_sems.at[k], recv_sem=recv_sems.at[k], device_id=to, device_id_type=MESH)

        own = pltpu.make_async_copy(x_ref, slot(*me), local_sem)
        own.start()
        first = [copy(0, me, sibling, src=x_ref)]
        first += [copy(1 + j, me, (*chip, c), src=x_ref) for j, chip in enumerate(chips)]
        for cp in first:
            cp.start()
        passed = [copy(4 + j, (*chip, c), sibling) for j, chip in enumerate(chips)]
        for j, chip in enumerate(chips):
            copy(1 + j, (*chip, c), me).wait_recv()
            passed[j].start()
        copy(0, sibling, me).wait_recv()
        for j, chip in enumerate(chips):
            copy(4 + j, (*chip, 1 - c), me).wait_recv()
        for cp in first + passed:
            cp.wait_send()
        own.wait()

    return pl.pallas_call(
        body, name=name, out_shape=jax.ShapeDtypeStruct((8, R, LANES), mine.dtype),
        in_specs=[HBM_SPEC], out_specs=HBM_SPEC,
        scratch_shapes=[pltpu.SemaphoreType.DMA((7,)), pltpu.SemaphoreType.DMA((7,)), pltpu.SemaphoreType.DMA],
    )(mine)


def allgather_direct(mine, *, name):
    R = mine.shape[0]

    def body(x_ref, out_ref, send_sems, recv_sems, local_sem):
        x, y, c = _place()
        own = pltpu.make_async_copy(x_ref, out_ref.at[4 * x + 2 * y + c], local_sem)
        own.start()
        sends = []
        for f in range(1, 8):
            fx, fy, fc = (f >> 2) & 1, (f >> 1) & 1, f & 1
            px, py, pc = jnp.where(fx, 1 - x, x), jnp.where(fy, 1 - y, y), jnp.where(fc, 1 - c, c)
            sends.append(pltpu.make_async_remote_copy(
                src_ref=x_ref, dst_ref=out_ref.at[4 * x + 2 * y + c], send_sem=send_sems.at[f - 1],
                recv_sem=recv_sems.at[f - 1], device_id=(px, py, pc), device_id_type=MESH))
        for cp in sends:
            cp.start()
        for f in range(1, 8):
            fx, fy, fc = (f >> 2) & 1, (f >> 1) & 1, f & 1
            px, py, pc = jnp.where(fx, 1 - x, x), jnp.where(fy, 1 - y, y), jnp.where(fc, 1 - c, c)
            pltpu.make_async_remote_copy(
                src_ref=x_ref, dst_ref=out_ref.at[4 * px + 2 * py + pc], send_sem=send_sems.at[f - 1],
                recv_sem=recv_sems.at[f - 1], device_id=(px, py, pc), device_id_type=MESH).wait_recv()
        for cp in sends:
            cp.wait_send()
        own.wait()

    return pl.pallas_call(
        body, name=name, out_shape=jax.ShapeDtypeStruct((8, R, LANES), mine.dtype),
        in_specs=[HBM_SPEC], out_specs=HBM_SPEC,
        scratch_shapes=[pltpu.SemaphoreType.DMA((7,)), pltpu.SemaphoreType.DMA((7,)), pltpu.SemaphoreType.DMA],
    )(mine)


def send_to_sibling(v, *, name):
    def body(v_ref, out_ref, send_sem, recv_sem):
        x, y, c = _place()
        cp = pltpu.make_async_remote_copy(src_ref=v_ref, dst_ref=out_ref, send_sem=send_sem, recv_sem=recv_sem,
                                          device_id=(x, y, 1 - c), device_id_type=MESH)
        cp.start()
        cp.wait()

    return pl.pallas_call(
        body, name=name, out_shape=jax.ShapeDtypeStruct(v.shape, v.dtype), in_specs=[HBM_SPEC], out_specs=HBM_SPEC,
        scratch_shapes=[pltpu.SemaphoreType.DMA, pltpu.SemaphoreType.DMA],
    )(v)


def chip_exchange(p, *, name):
    R = p.shape[1]

    def body(p_ref, out_ref, send_sems, recv_sems):
        x, y, c = _place()
        chips = [(1 - x, y), (x, 1 - y), (1 - x, 1 - y)]
        sends = [pltpu.make_async_remote_copy(
            src_ref=p_ref.at[2 * px + py], dst_ref=out_ref.at[j], send_sem=send_sems.at[j], recv_sem=recv_sems.at[j],
            device_id=(px, py, c), device_id_type=MESH) for j, (px, py) in enumerate(chips)]
        for cp in sends:
            cp.start()
        for cp in sends:
            cp.wait()

    return pl.pallas_call(
        body, name=name, out_shape=jax.ShapeDtypeStruct((3, R, LANES), p.dtype), in_specs=[HBM_SPEC],
        out_specs=HBM_SPEC,
        scratch_shapes=[pltpu.SemaphoreType.DMA((3,)), pltpu.SemaphoreType.DMA((3,))],
    )(p)


def add_blocks(terms, out_dtype, *, name, tile=1024):
    R = terms[0].shape[0]
    tr = R
    for d in range(16, min(R, tile) + 1, 16):
        if R % d == 0:
            tr = d

    def body(*refs):
        acc = refs[0][...].astype(F32)
        for ref in refs[1:-1]:
            acc = acc + ref[...].astype(F32)
        refs[-1][...] = acc.astype(out_dtype)

    spec = pl.BlockSpec((tr, LANES), lambda i: (i, 0))
    return pl.pallas_call(
        body, name=name, grid=(R // tr,), in_specs=[spec] * len(terms), out_specs=spec,
        out_shape=jax.ShapeDtypeStruct((R, LANES), out_dtype), compiler_params=_cparams("parallel"),
    )(*terms)


def _half_rows(arr, cc):
    hr = arr.shape[1] // 2
    return lax.dynamic_slice_in_dim(arr, cc * hr, hr, axis=1).reshape(-1)


def _flat_half(shards, cc, dtype):
    flat = jnp.concatenate([_half_rows(shards[n], cc).astype(dtype) for n in BIG])
    return flat.reshape(-1, LANES)


def _unflat_halves(flat_by_c, shapes):
    out, off = {}, 0
    for n in BIG:
        _, R, C = shapes[n]
        sz = 2 * (R // 2) * C
        out[n] = jnp.concatenate([flat_by_c[c][off:off + sz].reshape(2, R // 2, C) for c in range(2)], axis=1)
        off += sz
    return out


def _to_heads(a, nh):
    T = a.shape[0]
    return a.reshape(T, nh, a.shape[1] // nh).transpose(1, 0, 2)


def _from_heads(a):
    nh, T, d = a.shape
    return a.transpose(1, 0, 2).reshape(T, nh * d)


def _to_heads_t(a, nh):
    T = a.shape[0]
    return a.reshape(T, nh, a.shape[1] // nh).transpose(1, 2, 0)


def _from_heads_t(a):
    nh, d, T = a.shape
    return a.transpose(2, 0, 1).reshape(T, nh * d)


def _pad_cols(a, n):
    return jnp.pad(a, ((0, 0), (0, n - a.shape[1])))


def _pack_w_in(w):
    offs = [sum(IN_SPLITS[:i]) for i in range(len(IN_SPLITS) + 1)]
    sb, z, xbc, dt, cq, ckv, kr = [w[:, offs[i]:offs[i + 1]] for i in range(len(IN_SPLITS))]
    zeros = lambda n: jnp.zeros((w.shape[0], n), w.dtype)
    h = MLA_ROPE // 2
    kra = jnp.concatenate([zeros(MLA_NOPE), kr, zeros(LANES - MLA_QK)], axis=1)
    krb = jnp.concatenate([zeros(MLA_NOPE), -kr[:, h:], kr[:, :h], zeros(LANES - MLA_QK)], axis=1)
    return sb, jnp.concatenate([z, xbc, cq, ckv, _pad_cols(dt, LANES), kra, krb], axis=1)


def _unpack_gw_in(g_sb, g):
    h = MLA_ROPE // 2
    ga, gb = g[:, OFF_KRA:OFF_KRA + LANES], g[:, OFF_KRB:OFF_KRB + LANES]
    gkr = ga[:, MLA_NOPE:MLA_QK] + jnp.concatenate([gb[:, MLA_NOPE + h:MLA_QK], -gb[:, MLA_NOPE:MLA_NOPE + h]], axis=1)
    return jnp.concatenate([g_sb, g[:, OFF_Z:OFF_Z + 512], g[:, OFF_XBC:OFF_XBC + 768],
                            g[:, OFF_DT:OFF_DT + 8], g[:, OFF_CQ:OFF_CQ + 256], g[:, OFF_CKV:OFF_CKV + 128], gkr], axis=1)


def _pack_w_uq(w):
    zeros = lambda n: jnp.zeros((w.shape[0], n), w.dtype)
    h = MLA_ROPE // 2
    pp, rr = [], []
    for i in range(MLA_HEADS):
        nope = w[:, MLA_QK * i:MLA_QK * i + MLA_NOPE]
        rope = w[:, MLA_QK * i + MLA_NOPE:MLA_QK * (i + 1)]
        pp += [nope, rope, zeros(LANES - MLA_QK)]
        rr += [zeros(MLA_NOPE), -rope[:, h:], rope[:, :h], zeros(LANES - MLA_QK)]
    return jnp.concatenate(pp, axis=1), jnp.concatenate(rr, axis=1)


def _unpack_gw_uq(gp, gr):
    h = MLA_ROPE // 2
    out = []
    for i in range(MLA_HEADS):
        b = LANES * i
        out.append(gp[:, b:b + MLA_NOPE])
        out.append(gp[:, b + MLA_NOPE:b + MLA_NOPE + h] + gr[:, b + MLA_NOPE + h:b + MLA_QK])
        out.append(gp[:, b + MLA_NOPE + h:b + MLA_QK] - gr[:, b + MLA_NOPE:b + MLA_NOPE + h])
    return jnp.concatenate(out, axis=1)


def _pack_w_ukv(w):
    zeros = lambda n: jnp.zeros((w.shape[0], n), w.dtype)
    kk, vv = [], []
    for i in range(MLA_HEADS):
        b = (MLA_NOPE + MLA_V) * i
        kk += [w[:, b:b + MLA_NOPE], zeros(LANES - MLA_NOPE)]
        vv += [w[:, b + MLA_NOPE:b + MLA_NOPE + MLA_V], zeros(LANES - MLA_V)]
    return jnp.concatenate(kk, axis=1), jnp.concatenate(vv, axis=1)


def _unpack_gw_ukv(gk, gv):
    out = []
    for i in range(MLA_HEADS):
        out += [gk[:, LANES * i:LANES * i + MLA_NOPE], gv[:, LANES * i:LANES * i + MLA_V]]
    return jnp.concatenate(out, axis=1)


def _rope_tables(positions):
    inv_freq = 1.0 / (ROPE_THETA ** (jnp.arange(0, MLA_ROPE, 2, dtype=F32) / MLA_ROPE))
    ang = positions.astype(F32)[:, None] * inv_freq
    cos, sin = jnp.cos(ang), jnp.sin(ang)
    T = positions.shape[0]
    one, zero = jnp.ones((T, MLA_NOPE), F32), jnp.zeros((T, MLA_NOPE), F32)
    pad1, pad0 = jnp.ones((T, LANES - MLA_QK), F32), jnp.zeros((T, LANES - MLA_QK), F32)
    return jnp.concatenate([one, cos, cos, pad1], axis=1), jnp.concatenate([zero, sin, sin, pad0], axis=1)


def _row(v):
    return v.reshape(1, -1)


def _pad_row(v):
    return _pad_cols(v.reshape(1, -1), LANES)


def _layer_weights(full, small, li):
    p = {}
    p["w_sb"], p["w_rest"] = _pack_w_in(full["w_in"][li])
    q_scale = jnp.concatenate([jnp.full((1, SB_HEADS * SB_DIM), SB_DIM ** -0.5, BF16),
                               jnp.ones((1, 2 * SB_HEADS * SB_DIM), BF16)], axis=1)
    p["w_sb_fwd"] = p["w_sb"] * q_scale
    p["wqp"], p["wqr"] = _pack_w_uq(full["mla_w_uq"][li])
    p["wkp"], p["wvp"] = _pack_w_ukv(full["mla_w_ukv"][li])
    p["w_out"] = full["w_out"][li]
    p["w_up"] = full["ffn_w_up"][li]
    p["w_down"] = full["ffn_w_down"][li]
    for n in ("mix_norm", "sb_out_norm", "ssm_conv_b", "ssm_out_norm", "mla_q_norm", "mla_kv_norm", "mla_out_norm",
              "ffn_norm", "ffn_conv_b"):
        p[n] = _row(small[n][li])
    for n in ("ssm_dt_bias", "ssm_a_log", "ssm_d"):
        p[n] = _pad_row(small[n][li])
    p["ssm_conv_w"] = small["ssm_conv_w"][li]
    p["ffn_conv_w"] = small["ffn_conv_w"][li]
    return p


def _layer_fwd(h, p, cos, sin, li):
    T = h.shape[0]
    nm = lambda s: "l%d_%s" % (li, s)
    s = {"h": h}
    (n1,) = rowwise(rms_fn, [h], [p["mix_norm"]], [(D_MODEL, BF16)], name=nm("mix_norm"))
    proj = matmul(n1, p["w_rest"], name=nm("in_proj"))
    qkv = matmul(n1, p["w_sb_fwd"], name=nm("in_proj_sb"), out_dtype=BF16)
    s["n1"], s["proj"] = n1, proj
    s["sb_q"] = _to_heads_t(qkv[:, 0:256], SB_HEADS)
    s["sb_k"] = _to_heads(qkv[:, 256:512], SB_HEADS)
    s["sb_v"] = _to_heads(qkv[:, 512:768], SB_HEADS)
    y_sb_hm, s["sb_bt"], s["sb_first"] = sb_fwd(s["sb_q"], s["sb_k"], s["sb_v"], name=nm("sb_fwd"))
    s["y_sb"] = _from_heads_t(y_sb_hm)
    xbc = ssm_conv_act(proj, p["ssm_conv_w"], p["ssm_conv_b"], name=nm("ssm_conv"))
    s["x_hm"] = _to_heads(xbc[:, :SSM_INNER], SSM_HEADS)
    s["b_hm"] = _to_heads(xbc[:, SSM_INNER:SSM_INNER + 128], SSM_GROUPS)
    s["c_hm"] = _to_heads(xbc[:, SSM_INNER + 128:], SSM_GROUPS)
    y_ssm_hm, s["states"] = ssd_fwd(s["x_hm"], s["b_hm"], s["c_hm"], proj, p["ssm_dt_bias"], p["ssm_a_log"],
                                    p["ssm_d"], name=nm("ssd_fwd"))
    s["y_ssm"] = _from_heads(y_ssm_hm)
    rows = [(proj, 256, OFF_CQ // 256), (proj, 128, OFF_CKV // 128), (proj, 128, OFF_KRA // 128),
            (proj, 128, OFF_KRB // 128), cos, sin]
    qp, kp, vv = rowwise(mla_prep_fn, rows, [p["mla_q_norm"], p["mla_kv_norm"], p["wqp"], p["wqr"], p["wkp"], p["wvp"]],
                         [(512, BF16), (512, BF16), (512, BF16)], name=nm("mla_prep"))
    s["mla_q"], s["mla_k"], s["mla_v"] = _to_heads_t(qp, MLA_HEADS), kp, vv
    s["mla_o"], s["mla_lse"] = mla_fwd(s["mla_q"], kp, vv, name=nm("mla_fwd"))
    s["y_mla"] = _from_heads_t(s["mla_o"][:, :MLA_V, :])
    (cat,) = rowwise(merge_fn, [s["y_sb"], s["y_ssm"], (proj, 512, OFF_Z // 512), s["y_mla"]],
                     [p["sb_out_norm"], p["ssm_out_norm"], p["mla_out_norm"]], [(D_MODEL, BF16)], name=nm("merge"),
                     post=lambda a, b, c: (jnp.concatenate([a, b, c], axis=1),))
    s["cat"] = cat
    h1 = matmul(cat, p["w_out"], name=nm("out_proj"), residual=h)
    s["h1"] = h1
    (n2,) = rowwise(rms_fn, [h1], [p["ffn_norm"]], [(D_MODEL, BF16)], name=nm("ffn_norm"))
    up = matmul(n2, p["w_up"], name=nm("ffn_up"))
    act = ffn_act(up, p["ffn_conv_w"], p["ffn_conv_b"], name=nm("ffn_act"))
    s["n2"], s["up"], s["act"] = n2, up, act
    h2 = matmul(act, p["w_down"], name=nm("ffn_down"), residual=h1)
    return h2, s


def _layer_bwd(dh2, s, p, cos, sin, li):
    nm = lambda t: "l%d_%s" % (li, t)
    g = {}
    proj = s["proj"]
    g["ffn_w_down"] = matmul(s["act"], dh2, name=nm("g_w_down"), ta=True)
    d_act = matmul(dh2, p["w_down"], name=nm("d_act"), out_dtype=BF16, tb=True)
    d_up_g, d_up_v, gwg, gwv, gbg, gbv = ffn_bwd_fused(s["up"], p["ffn_conv_w"], p["ffn_conv_b"], d_act,
                                                       name=nm("ffn_act_bwd"))
    g["ffn_conv_w"] = jnp.concatenate([gwg, gwv], axis=1)
    g["ffn_conv_b"] = jnp.concatenate([gbg[0], gbv[0]])
    g["ffn_w_up"] = jnp.concatenate([matmul(s["n2"], d_up_g, name=nm("g_w_up_gate"), ta=True),
                                     matmul(s["n2"], d_up_v, name=nm("g_w_up_val"), ta=True)], axis=1)
    d_n2 = matmul(d_up_g, p["w_up"], name=nm("d_n2_gate"), tb=True)
    d_n2 = matmul(d_up_v, p["w_up"], name=nm("d_n2_val"), tb=True, b_k0=D_FF, residual=d_n2)
    (dh1,), (gn,) = rowwise_bwd(rms_fn, [s["h1"]], [], [p["ffn_norm"]], [d_n2], [F32], name=nm("ffn_norm_bwd"),
                                add0=dh2)
    g["ffn_norm"] = gn[0]
    g["w_out"] = matmul(s["cat"], dh1, name=nm("g_w_out"), ta=True)
    d_cat = matmul(dh1, p["w_out"], name=nm("d_cat"), tb=True)
    (d_ysb, d_yssm, d_z, d_ymla), (g1, g2, g3) = rowwise_bwd(
        merge_fn, [s["y_sb"], s["y_ssm"], (proj, 512, OFF_Z // 512), s["y_mla"]], [],
        [p["sb_out_norm"], p["ssm_out_norm"], p["mla_out_norm"]], [d_cat], [F32, F32, BF16, F32], name=nm("merge_bwd"),
        pre_ct=lambda d: (d[:, 0:256], d[:, 256:768], d[:, 768:1024]))
    g["sb_out_norm"], g["ssm_out_norm"], g["mla_out_norm"] = g1[0], g2[0], g3[0]
    dq, dk, dv = sb_bwd(s["sb_q"], s["sb_k"], s["sb_v"], _to_heads_t(d_ysb, SB_HEADS), s["sb_bt"], s["sb_first"], name=nm("sb_bwd"),
                        q_scale=SB_DIM ** -0.5)
    d_sb = jnp.concatenate([_from_heads_t(dq), _from_heads(dk), _from_heads(dv)], axis=1).astype(BF16)
    do_t = jnp.pad(_to_heads_t(d_ymla, MLA_HEADS), ((0, 0), (0, LANES - MLA_V), (0, 0)))
    dqp, dkp, dvv = mla_bwd(s["mla_q"], s["mla_k"], s["mla_v"], do_t, s["mla_o"], s["mla_lse"], name=nm("mla_bwd"))
    rows = [(proj, 256, OFF_CQ // 256), (proj, 128, OFF_CKV // 128), (proj, 128, OFF_KRA // 128),
            (proj, 128, OFF_KRB // 128)]
    (d_cq, d_ckv, d_kra, d_krb), (gqn, gkvn, gwqp, gwqr, gwkp, gwvp) = rowwise_bwd(
        mla_prep_fn, rows, [cos, sin], [p["mla_q_norm"], p["mla_kv_norm"], p["wqp"], p["wqr"], p["wkp"], p["wvp"]],
        [_from_heads_t(dqp), dkp, dvv], [BF16] * 4, name=nm("mla_prep_bwd"), tile=256)
    g["mla_q_norm"], g["mla_kv_norm"] = gqn[0], gkvn[0]
    g["mla_w_uq"] = _unpack_gw_uq(gwqp, gwqr)
    g["mla_w_ukv"] = _unpack_gw_ukv(gwkp, gwvp)
    dx_hm, db_hm, dc_hm, d_dt, gdb, gal, gds = ssd_bwd(
        s["x_hm"], s["b_hm"], s["c_hm"], proj, s["states"], p["ssm_dt_bias"], p["ssm_a_log"], p["ssm_d"],
        _to_heads(d_yssm, SSM_HEADS), name=nm("ssd_bwd"))
    g["ssm_dt_bias"], g["ssm_a_log"], g["ssm_d"] = gdb[0, :8], gal[0, :8], gds[0, :8]
    d_xbc_act = jnp.concatenate([_from_heads(dx_hm), _from_heads(db_hm), _from_heads(dc_hm)], axis=1)
    d_pre = ssm_conv_bwd_a(proj, p["ssm_conv_w"], p["ssm_conv_b"], d_xbc_act, name=nm("ssm_conv_bwd_a"))
    d_xbc, g["ssm_conv_w"], gscb = conv_bwd_b(d_pre, proj, OFF_XBC, p["ssm_conv_w"], name=nm("ssm_conv_bwd_b"),
                                              out_dtype=BF16, tc=256)
    g["ssm_conv_b"] = gscb[0]
    d_proj = jnp.concatenate([d_z, d_xbc, d_cq, d_ckv, d_dt.astype(BF16), d_kra, d_krb], axis=1)
    g["w_in"] = _unpack_gw_in(matmul(s["n1"], d_sb, name=nm("g_w_in_sb"), ta=True),
                              matmul(s["n1"], d_proj, name=nm("g_w_in"), ta=True))
    d_n1 = matmul(d_sb, p["w_sb"], name=nm("d_n1_sb"), tb=True)
    d_n1 = matmul(d_proj, p["w_rest"], name=nm("d_n1"), tb=True, residual=d_n1)
    (dh0,), (gm,) = rowwise_bwd(rms_fn, [s["h"]], [], [p["mix_norm"]], [d_n1], [F32], name=nm("mix_norm_bwd"),
                                add0=dh1)
    g["mix_norm"] = gm[0]
    return dh0, g


def kernel(x, positions, mix_norm, w_in, sb_out_norm, ssm_conv_w, ssm_conv_b, ssm_dt_bias, ssm_a_log, ssm_d, ssm_out_norm, mla_q_norm, mla_w_uq, mla_kv_norm, mla_w_ukv, mla_out_norm, w_out, ffn_norm, ffn_w_up, ffn_conv_w, ffn_conv_b, ffn_w_down, final_norm, loss_target, m_mix_norm, m_w_in, m_sb_out_norm, m_ssm_conv_w, m_ssm_conv_b, m_ssm_dt_bias, m_ssm_a_log, m_ssm_d, m_ssm_out_norm, m_mla_q_norm, m_mla_w_uq, m_mla_kv_norm, m_mla_w_ukv, m_mla_out_norm, m_w_out, m_ffn_norm, m_ffn_w_up, m_ffn_conv_w, m_ffn_conv_b, m_ffn_w_down, m_final_norm, v_mix_norm, v_w_in, v_sb_out_norm, v_ssm_conv_w, v_ssm_conv_b, v_ssm_dt_bias, v_ssm_a_log, v_ssm_d, v_ssm_out_norm, v_mla_q_norm, v_mla_w_uq, v_mla_kv_norm, v_mla_w_ukv, v_mla_out_norm, v_w_out, v_ffn_norm, v_ffn_w_up, v_ffn_conv_w, v_ffn_conv_b, v_ffn_w_down, v_final_norm):
    W = dict(mix_norm=mix_norm, w_in=w_in, sb_out_norm=sb_out_norm, ssm_conv_w=ssm_conv_w, ssm_conv_b=ssm_conv_b,
             ssm_dt_bias=ssm_dt_bias, ssm_a_log=ssm_a_log, ssm_d=ssm_d, ssm_out_norm=ssm_out_norm,
             mla_q_norm=mla_q_norm, mla_w_uq=mla_w_uq, mla_kv_norm=mla_kv_norm, mla_w_ukv=mla_w_ukv,
             mla_out_norm=mla_out_norm, w_out=w_out, ffn_norm=ffn_norm, ffn_w_up=ffn_w_up, ffn_conv_w=ffn_conv_w,
             ffn_conv_b=ffn_conv_b, ffn_w_down=ffn_w_down, final_norm=final_norm)
    M = dict(mix_norm=m_mix_norm, w_in=m_w_in, sb_out_norm=m_sb_out_norm, ssm_conv_w=m_ssm_conv_w,
             ssm_conv_b=m_ssm_conv_b, ssm_dt_bias=m_ssm_dt_bias, ssm_a_log=m_ssm_a_log, ssm_d=m_ssm_d,
             ssm_out_norm=m_ssm_out_norm, mla_q_norm=m_mla_q_norm, mla_w_uq=m_mla_w_uq, mla_kv_norm=m_mla_kv_norm,
             mla_w_ukv=m_mla_w_ukv, mla_out_norm=m_mla_out_norm, w_out=m_w_out, ffn_norm=m_ffn_norm,
             ffn_w_up=m_ffn_w_up, ffn_conv_w=m_ffn_conv_w, ffn_conv_b=m_ffn_conv_b, ffn_w_down=m_ffn_w_down,
             final_norm=m_final_norm)
    V = dict(mix_norm=v_mix_norm, w_in=v_w_in, sb_out_norm=v_sb_out_norm, ssm_conv_w=v_ssm_conv_w,
             ssm_conv_b=v_ssm_conv_b, ssm_dt_bias=v_ssm_dt_bias, ssm_a_log=v_ssm_a_log, ssm_d=v_ssm_d,
             ssm_out_norm=v_ssm_out_norm, mla_q_norm=v_mla_q_norm, mla_w_uq=v_mla_w_uq, mla_kv_norm=v_mla_kv_norm,
             mla_w_ukv=v_mla_w_ukv, mla_out_norm=v_mla_out_norm, w_out=v_w_out, ffn_norm=v_ffn_norm,
             ffn_w_up=v_ffn_w_up, ffn_conv_w=v_ffn_conv_w, ffn_conv_b=v_ffn_conv_b, ffn_w_down=v_ffn_w_down,
             final_norm=v_final_norm)
    depth = mix_norm.shape[0]
    cx, cy, cc = _place()
    chip = 2 * cx + cy
    T = x.shape[1]

    shard_shapes = {n: W[n].shape for n in BIG}
    gathered = allgather_blocks(_flat_half(W, cc, BF16), name="gather_weights")
    full = {}
    per_chip = [_unflat_halves([gathered[2 * k + c].reshape(-1) for c in range(2)], shard_shapes) for k in range(4)]
    for n in BIG:
        full[n] = jnp.concatenate([per_chip[k][n] for k in range(4)], axis=BIG_AXIS[n])
    conv_full = {}
    small = {n: W[n] for n in SMALL_REPL}
    cw_flat = jnp.concatenate([W[n].reshape(-1) for n in SMALL_SHARD])
    cw_rows = -(-cw_flat.shape[0] // (8 * LANES)) * 8
    cw_all = allgather_direct(jnp.pad(cw_flat, (0, cw_rows * LANES - cw_flat.shape[0])).reshape(cw_rows, LANES),
                              name="gather_conv_taps")
    off = 0
    for n in SMALL_SHARD:
        sz = W[n].size
        conv_full[n] = jnp.concatenate(
            [cw_all[2 * k].reshape(-1)[off:off + sz].reshape(W[n].shape) for k in range(4)], axis=2)
        off += sz
    small.update(conv_full)

    cos, sin = _rope_tables(positions[0])
    params = [_layer_weights(full, small, li) for li in range(depth)]

    h = x[0]
    saved = []
    for li in range(depth):
        h, s = _layer_fwd(h, params[li], cos, sin, li)
        saved.append(s)
    dh, g_final, loss_lanes = loss_head(h, loss_target[0], _row(final_norm), name="loss_head")

    grads = [None] * depth
    for li in reversed(range(depth)):
        dh, grads[li] = _layer_bwd(dh, saved[li], params[li], cos, sin, li)
    grad_x = dh[None]
    G = {n: jnp.stack([grads[li][n] for li in range(depth)]) for n in WEIGHTS if n != "final_norm"}
    G["final_norm"] = g_final[0]

    def shard_major(n):
        a = G[n]
        ax = BIG_AXIS[n]
        parts = jnp.split(a, 4, axis=ax)
        return parts

    by_chip = {n: shard_major(n) for n in BIG}

    def flat_for(k, which):
        return _flat_half({n: by_chip[n][k] for n in BIG}, which, BF16)

    mine_first = jnp.stack([flat_for(k, cc) for k in range(4)])
    for_sibling = jnp.stack([flat_for(k, 1 - cc) for k in range(4)])
    R = mine_first.shape[1]
    from_sibling = send_to_sibling(for_sibling.reshape(4 * R, LANES), name="grads_to_sibling")
    pair = add_blocks([mine_first.reshape(4 * R, LANES), from_sibling], BF16, name="grads_pair_sum").reshape(4, R, LANES)
    others = chip_exchange(pair, name="grads_chip_exchange")
    own = lax.dynamic_index_in_dim(pair, chip, 0, keepdims=False)
    half = add_blocks([own, others[0], others[1], others[2]], F32, name="grads_chip_sum")
    other = send_to_sibling(half, name="grads_pair_swap")
    by_core = [jnp.where(cc == 0, half, other), jnp.where(cc == 0, other, half)]
    g_big = _unflat_halves([a.reshape(-1) for a in by_core], shard_shapes)

    small_list = [G[n].reshape(-1) for n in SMALL_REPL] + [G[n].reshape(-1) for n in SMALL_SHARD]
    small_list.append(jnp.sum(loss_lanes).reshape(1))
    sm = jnp.concatenate(small_list)
    n_small = sm.shape[0]
    sm_rows = -(-n_small // (16 * LANES)) * 16
    sm_all = allgather_direct(jnp.pad(sm, (0, sm_rows * LANES - n_small)).reshape(sm_rows, LANES), name="gather_small")
    sm_sum = add_blocks([sm_all[d] for d in range(8)], F32, name="small_sum").reshape(-1)
    g_small, off = {}, 0
    for n in SMALL_REPL:
        g_small[n] = sm_sum[off:off + W[n].size].reshape(W[n].shape)
        off += W[n].size
    for n in SMALL_SHARD:
        full_shape = conv_full[n].shape
        sz = conv_full[n].size
        gfull = sm_sum[off:off + sz].reshape(full_shape)
        width = W[n].shape[2]
        g_small[n] = lax.dynamic_slice_in_dim(gfull, chip * width, width, axis=2)
        off += sz
    loss = sm_sum[off]

    grad_out, delta, new_m, new_v = {}, {}, {}, {}
    for n in BIG:
        shp = W[n].shape
        two_d = lambda a: a.reshape(shp[0] * shp[1], shp[2])
        d, nm_, nv_ = adamw(two_d(W[n]), two_d(g_big[n]), two_d(M[n]), two_d(V[n]), name="adamw_" + n)
        grad_out[n], delta[n], new_m[n], new_v[n] = g_big[n], d.reshape(shp), nm_.reshape(shp), nv_.reshape(shp)
    small_names = SMALL_REPL + SMALL_SHARD

    def flat_small(d):
        f = jnp.concatenate([d[n].reshape(-1) for n in small_names])
        rows = -(-f.shape[0] // (8 * LANES)) * 8
        return jnp.pad(f, (0, rows * LANES - f.shape[0])).reshape(rows, LANES)

    vpad = flat_small(V)
    d, nm_, nv_ = adamw(flat_small(W), flat_small(g_small), flat_small(M), vpad, name="adamw_small")
    off = 0
    for n in small_names:
        sz = W[n].size
        grad_out[n] = g_small[n]
        delta[n] = d.reshape(-1)[off:off + sz].reshape(W[n].shape)
        new_m[n] = nm_.reshape(-1)[off:off + sz].reshape(W[n].shape)
        new_v[n] = nv_.reshape(-1)[off:off + sz].reshape(W[n].shape)
        off += sz

    return (loss, grad_x, *[grad_out[n] for n in WEIGHTS], *[delta[n] for n in WEIGHTS],
            *[new_m[n] for n in WEIGHTS], *[new_v[n] for n in WEIGHTS])
```

```python
import functools
import math

import jax
import jax.numpy as jnp
from jax import lax
from jax.experimental import pallas as pl
from jax.experimental.pallas import tpu as pltpu

F32 = jnp.float32
BF16 = jnp.bfloat16

EPS = 1e-6
D_MODEL = 1024
SB_HEADS, SB_DIM = 4, 64
SSM_HEADS, SSM_DIM, SSM_GROUPS, SSM_STATE, SSM_CHUNK = 8, 64, 2, 64, 128
SSM_INNER = SSM_HEADS * SSM_DIM
SSM_CONV_DIM = SSM_INNER + 2 * SSM_GROUPS * SSM_STATE
MLA_HEADS, MLA_NOPE, MLA_ROPE, MLA_V = 4, 64, 32, 64
MLA_QK = MLA_NOPE + MLA_ROPE
MLA_SCALE = MLA_QK ** -0.5
ROPE_THETA = 10000.0
D_FF = 2816
IN_SPLITS = (768, 512, 768, 8, 256, 128, 32)

OFF_Z, OFF_XBC, OFF_CQ, OFF_CKV, OFF_DT, OFF_KRA, OFF_KRB = 0, 512, 1280, 1536, 1664, 1792, 1920
D_REST = 2048
LANES = 128

ADAM_LR, ADAM_B1, ADAM_B2, ADAM_EPS, ADAM_WD, ADAM_STEP = 0.001, 0.9, 0.999, 1e-08, 0.01, 10

V7X_VMEM_LIMIT = 48 * 1024 * 1024

NT = (((1,), (1,)), ((), ()))
TN = (((0,), (0,)), ((), ()))

BIG = ("w_in", "mla_w_uq", "mla_w_ukv", "w_out", "ffn_w_up", "ffn_w_down")
BIG_AXIS = {"w_in": 2, "mla_w_uq": 2, "mla_w_ukv": 2, "w_out": 1, "ffn_w_up": 2, "ffn_w_down": 1}
SMALL_REPL = ("mix_norm", "sb_out_norm", "ssm_conv_b", "ssm_dt_bias", "ssm_a_log", "ssm_d", "ssm_out_norm",
              "mla_q_norm", "mla_kv_norm", "mla_out_norm", "ffn_norm", "ffn_conv_b", "final_norm")
SMALL_SHARD = ("ssm_conv_w", "ffn_conv_w")
WEIGHTS = ("mix_norm", "w_in", "sb_out_norm", "ssm_conv_w", "ssm_conv_b", "ssm_dt_bias", "ssm_a_log", "ssm_d",
           "ssm_out_norm", "mla_q_norm", "mla_w_uq", "mla_kv_norm", "mla_w_ukv", "mla_out_norm", "w_out", "ffn_norm",
           "ffn_w_up", "ffn_conv_w", "ffn_conv_b", "ffn_w_down", "final_norm")


def _cparams(*sem):
    return pltpu.CompilerParams(dimension_semantics=sem if sem else None, vmem_limit_bytes=V7X_VMEM_LIMIT)


def _pick(n, target, mult=LANES):
    best = None
    for d in range(mult, min(n, target) + 1, mult):
        if n % d == 0:
            best = d
    return best or n


def _sigmoid(x):
    return 1.0 / (1.0 + jnp.exp(-x))


def _softplus(x):
    ax = jnp.where(x > 0, x, -x)
    return jnp.where(x > 0, x, 0.0) + jnp.log(1.0 + jnp.exp(-ax))


def _rms(x, g):
    return x * lax.rsqrt(jnp.mean(x * x, axis=-1, keepdims=True) + EPS) * g


def _raw_nn(a, b):
    return jnp.dot(a.astype(BF16), b.astype(BF16), preferred_element_type=F32)


def _raw_nt(a, b):
    return lax.dot_general(a.astype(BF16), b.astype(BF16), NT, preferred_element_type=F32)


def _raw_tn(a, b):
    return lax.dot_general(a.astype(BF16), b.astype(BF16), TN, preferred_element_type=F32)


@jax.custom_vjp
def mm_nn(a, b):
    return _raw_nn(a, b)


mm_nn.defvjp(lambda a, b: (_raw_nn(a, b), (a, b)),
             lambda r, ct: (_raw_nt(ct, r[1]), _raw_tn(r[0], ct)))


@jax.custom_vjp
def mm_nt(a, b):
    return _raw_nt(a, b)


mm_nt.defvjp(lambda a, b: (_raw_nt(a, b), (a, b)),
             lambda r, ct: (_raw_nn(ct, r[1]), _raw_tn(ct, r[0])))


@jax.custom_vjp
def mm_tn(a, b):
    return _raw_tn(a, b)


mm_tn.defvjp(lambda a, b: (_raw_tn(a, b), (a, b)),
             lambda r, ct: (_raw_nt(r[1], ct), _raw_nn(r[0], ct)))


def _split_dot(x, m, terms):
    acc = None
    r = x
    for t in range(terms):
        xt = r.astype(BF16)
        d = jnp.dot(xt, m, preferred_element_type=F32)
        acc = d if acc is None else acc + d
        if t + 1 < terms:
            r = r - xt.astype(F32)
    return acc


def _tri_dot(tri, x, terms=3):
    parts = []
    r = x
    for t in range(terms):
        xt = r.astype(BF16)
        parts.append(xt)
        if t + 1 < terms:
            r = r - xt.astype(F32)
    return jnp.dot(jnp.concatenate([tri] * terms, axis=1), jnp.concatenate(parts, axis=0),
                   preferred_element_type=F32)


def _tri(n, cmp):
    r = lax.broadcasted_iota(jnp.int32, (n, n), 0)
    c = lax.broadcasted_iota(jnp.int32, (n, n), 1)
    return cmp(r, c).astype(BF16)


@jax.custom_vjp
def csum_rows(x):
    return _tri_dot(_tri(x.shape[0], lambda r, c: r >= c), x)


csum_rows.defvjp(lambda x: (csum_rows(x), None),
                 lambda _, ct: (_tri_dot(_tri(ct.shape[0], lambda r, c: r <= c), ct),))


def matmul(a, b, *, name, out_dtype=F32, ta=False, tb=False, b_k0=0, residual=None):
    if ta:
        K, M = a.shape
    else:
        M, K = a.shape
    N = b.shape[0] if tb else b.shape[1]
    tm = _pick(M, 1408)
    tn = _pick(N, 1408)
    tk = _pick(K, 1408)
    nk = K // tk
    kb0 = b_k0 // tk
    assert b_k0 % tk == 0 and (tb or b_k0 == 0)
    has_res = residual is not None

    def body(*refs):
        if has_res:
            a_ref, b_ref, r_ref, o_ref, acc = refs
        else:
            a_ref, b_ref, o_ref, acc = refs
        k = pl.program_id(2)

        @pl.when(k == 0)
        def _():
            acc[...] = jnp.zeros_like(acc)

        av = a_ref[...].astype(BF16)
        bv = b_ref[...].astype(BF16)
        if ta:
            acc[...] += lax.dot_general(av, bv, TN, preferred_element_type=F32)
        elif tb:
            acc[...] += lax.dot_general(av, bv, NT, preferred_element_type=F32)
        else:
            acc[...] += jnp.dot(av, bv, preferred_element_type=F32)

        @pl.when(k == nk - 1)
        def _():
            r = acc[...]
            if has_res:
                r = r + r_ref[...].astype(F32)
            o_ref[...] = r.astype(o_ref.dtype)

    a_spec = pl.BlockSpec((tk, tm), lambda i, j, k: (k, i)) if ta else pl.BlockSpec((tm, tk), lambda i, j, k: (i, k))
    b_spec = pl.BlockSpec((tn, tk), lambda i, j, k: (j, kb0 + k)) if tb else pl.BlockSpec((tk, tn), lambda i, j, k: (k, j))
    in_specs = [a_spec, b_spec]
    args = [a, b]
    if has_res:
        in_specs.append(pl.BlockSpec((tm, tn), lambda i, j, k: (i, j)))
        args.append(residual)
    return pl.pallas_call(
        body, name=name, grid=(M // tm, N // tn, nk),
        in_specs=in_specs, out_specs=pl.BlockSpec((tm, tn), lambda i, j, k: (i, j)),
        out_shape=jax.ShapeDtypeStruct((M, N), out_dtype),
        scratch_shapes=[pltpu.VMEM((tm, tn), F32)],
        compiler_params=_cparams("parallel", "parallel", "arbitrary"),
    )(*args)


def _row_spec(entry, tl):
    if isinstance(entry, tuple):
        arr, width, cb = entry
        return arr, pl.BlockSpec((tl, width), lambda i, cb=cb: (i, cb))
    return entry, pl.BlockSpec((tl, entry.shape[1]), lambda i: (i, 0))


def _rows_T(entry):
    return (entry[0] if isinstance(entry, tuple) else entry).shape[0]


def rowwise(fn, rows, params, outs, *, name, tile=512, post=None):
    T = _rows_T(rows[0])
    tl = min(T, tile)
    nr, npar = len(rows), len(params)

    def body(*refs):
        r = [ref[...].astype(F32) for ref in refs[:nr]]
        p = [ref[...].astype(F32) for ref in refs[nr:nr + npar]]
        res = fn(*r, *p)
        if post is not None:
            res = post(*res)
        for o_ref, val in zip(refs[nr + npar:], res):
            o_ref[...] = val.astype(o_ref.dtype)

    arrs, specs = [], []
    for e in rows:
        a, s = _row_spec(e, tl)
        arrs.append(a)
        specs.append(s)
    for p in params:
        arrs.append(p)
        specs.append(pl.BlockSpec(p.shape, lambda i: (0, 0)))
    res = pl.pallas_call(
        body, name=name, grid=(T // tl,), in_specs=specs,
        out_specs=[pl.BlockSpec((tl, c), lambda i: (i, 0)) for c, _ in outs],
        out_shape=[jax.ShapeDtypeStruct((T, c), dt) for c, dt in outs],
        compiler_params=_cparams("parallel"),
    )(*arrs)
    return res


def rowwise_bwd(fn, rows, nd_rows, params, cts, grad_dtypes, *, name, tile=512, pre_ct=None, add0=None):
    T = _rows_T(rows[0])
    tl = min(T, tile)
    nr, nn, npar, nc = len(rows), len(nd_rows), len(params), len(cts)
    has_add = add0 is not None

    def body(*refs):
        pos = 0
        r = [ref[...].astype(F32) for ref in refs[pos:pos + nr]]
        pos += nr
        nd = [ref[...].astype(F32) for ref in refs[pos:pos + nn]]
        pos += nn
        p = [ref[...].astype(F32) for ref in refs[pos:pos + npar]]
        pos += npar
        c = [ref[...].astype(F32) for ref in refs[pos:pos + nc]]
        pos += nc
        if has_add:
            addv = refs[pos][...].astype(F32)
            pos += 1
        rg_refs = refs[pos:pos + nr]
        pg_refs = refs[pos + nr:pos + nr + npar]
        if pre_ct is not None:
            c = list(pre_ct(*c))
        _, vjp = jax.vjp(lambda *a: fn(*a[:nr], *nd, *a[nr:]), *r, *p)
        g = vjp(tuple(c))
        for j, ref in enumerate(rg_refs):
            val = g[j]
            if has_add and j == 0:
                val = val + addv
            ref[...] = val.astype(ref.dtype)
        if npar:
            @pl.when(pl.program_id(0) == 0)
            def _():
                for ref in pg_refs:
                    ref[...] = jnp.zeros_like(ref)
            for j, ref in enumerate(pg_refs):
                ref[...] += g[nr + j]

    arrs, specs = [], []
    widths = []
    for e in list(rows) + list(nd_rows):
        a, s = _row_spec(e, tl)
        arrs.append(a)
        specs.append(s)
        widths.append(s.block_shape[1])
    for p in params:
        arrs.append(p)
        specs.append(pl.BlockSpec(p.shape, lambda i: (0, 0)))
    for e in cts:
        a, s = _row_spec(e, tl)
        arrs.append(a)
        specs.append(s)
    if has_add:
        a, s = _row_spec(add0, tl)
        arrs.append(a)
        specs.append(s)
    out_specs = [pl.BlockSpec((tl, widths[j]), lambda i: (i, 0)) for j in range(nr)]
    out_shape = [jax.ShapeDtypeStruct((T, widths[j]), grad_dtypes[j]) for j in range(nr)]
    out_specs += [pl.BlockSpec(p.shape, lambda i: (0, 0)) for p in params]
    out_shape += [jax.ShapeDtypeStruct(p.shape, F32) for p in params]
    res = pl.pallas_call(
        body, name=name, grid=(T // tl,), in_specs=specs, out_specs=out_specs, out_shape=out_shape,
        compiler_params=_cparams("arbitrary"),
    )(*arrs)
    return list(res[:nr]), list(res[nr:])


def rms_fn(h, g):
    return (_rms(h, g),)


def merge_fn(ysb, yssm, z, ymla, g_sb, g_ssm, g_mla):
    ya = _rms(ysb, g_sb)
    yb = _rms(yssm * (z * _sigmoid(z)), g_ssm)
    yc = _rms(ymla, g_mla)
    return ya, yb, yc


def mla_prep_fn(cq, ckv, kra, krb, cos, sin, qn, kvn, wqp, wqr, wkp, wvp):
    cos4 = jnp.concatenate([cos] * MLA_HEADS, axis=1)
    sin4 = jnp.concatenate([sin] * MLA_HEADS, axis=1)
    nq = _rms(cq, qn)
    q = (mm_nn(nq, wqp) * cos4 + mm_nn(nq, wqr) * sin4) * MLA_SCALE
    nkv = _rms(ckv, kvn)
    kpe = kra * cos + krb * sin
    k = mm_nn(nkv, wkp) + jnp.concatenate([kpe] * MLA_HEADS, axis=1)
    v = mm_nn(nkv, wvp)
    return q, k, v


HALO = 8


def _prev_halo_spec(tl, tc, col_of):
    return pl.BlockSpec((HALO, tc), lambda i, j: (jnp.maximum(i * (tl // HALO) - 1, 0), col_of(j)))


def _fill_prev(buf, x_ref, halo_ref, i):
    buf[0:HALO, :] = jnp.where(i > 0, halo_ref[...].astype(F32), 0.0)
    buf[HALO:, :] = x_ref[...].astype(F32)


def _conv_from(buf, w_ref, b_ref, K, tl):
    acc = b_ref[...].astype(F32) + jnp.zeros((tl, buf.shape[1]), F32)
    for k in range(K):
        acc = acc + buf[pl.ds(HALO - (K - 1 - k), tl), :] * w_ref[k:k + 1, :].astype(F32)
    return acc


def ssm_conv_act(proj, w, b, *, name, tile=512, tc=256):
    T = proj.shape[0]
    K, C = w.shape
    tl = min(T, tile)
    c0 = OFF_XBC // tc

    def body(x_ref, halo_ref, w_ref, b_ref, o_ref, buf):
        _fill_prev(buf, x_ref, halo_ref, pl.program_id(0))
        u = _conv_from(buf, w_ref, b_ref, K, tl)
        o_ref[...] = u * _sigmoid(u)

    return pl.pallas_call(
        body, name=name, grid=(T // tl, C // tc),
        in_specs=[pl.BlockSpec((tl, tc), lambda i, j: (i, c0 + j)), _prev_halo_spec(tl, tc, lambda j: c0 + j),
                  pl.BlockSpec((K, tc), lambda i, j: (0, j)), pl.BlockSpec((1, tc), lambda i, j: (0, j))],
        out_specs=pl.BlockSpec((tl, tc), lambda i, j: (i, j)),
        out_shape=jax.ShapeDtypeStruct((T, C), F32),
        scratch_shapes=[pltpu.VMEM((tl + HALO, tc), F32)],
        compiler_params=_cparams("parallel", "parallel"),
    )(proj, proj, w, b)


def ssm_conv_bwd_a(proj, w, b, d_out, *, name, tile=512, tc=256):
    T = proj.shape[0]
    K, C = w.shape
    tl = min(T, tile)
    c0 = OFF_XBC // tc

    def body(x_ref, halo_ref, w_ref, b_ref, d_ref, o_ref, buf):
        _fill_prev(buf, x_ref, halo_ref, pl.program_id(0))
        u = _conv_from(buf, w_ref, b_ref, K, tl)
        s = _sigmoid(u)
        o_ref[...] = d_ref[...].astype(F32) * (s * (1.0 + u * (1.0 - s)))

    return pl.pallas_call(
        body, name=name, grid=(T // tl, C // tc),
        in_specs=[pl.BlockSpec((tl, tc), lambda i, j: (i, c0 + j)), _prev_halo_spec(tl, tc, lambda j: c0 + j),
                  pl.BlockSpec((K, tc), lambda i, j: (0, j)), pl.BlockSpec((1, tc), lambda i, j: (0, j)),
                  pl.BlockSpec((tl, tc), lambda i, j: (i, j))],
        out_specs=pl.BlockSpec((tl, tc), lambda i, j: (i, j)),
        out_shape=jax.ShapeDtypeStruct((T, C), F32),
        scratch_shapes=[pltpu.VMEM((tl + HALO, tc), F32)],
        compiler_params=_cparams("parallel", "parallel"),
    )(proj, proj, w, b, d_out)


def ffn_act(up, w, b, *, name, tile=512, tc=1408):
    T = up.shape[0]
    K = w.shape[0]
    tl = min(T, tile)
    nj = D_FF // tc

    def body(xg_ref, hg_ref, xv_ref, hv_ref, wg_ref, wv_ref, bg_ref, bv_ref, o_ref, bufg, bufv):
        i = pl.program_id(0)
        _fill_prev(bufg, xg_ref, hg_ref, i)
        _fill_prev(bufv, xv_ref, hv_ref, i)
        gate = _conv_from(bufg, wg_ref, bg_ref, K, tl)
        val = _conv_from(bufv, wv_ref, bv_ref, K, tl)
        o_ref[...] = (gate * _sigmoid(gate) * val).astype(o_ref.dtype)

    return pl.pallas_call(
        body, name=name, grid=(T // tl, nj),
        in_specs=[pl.BlockSpec((tl, tc), lambda i, j: (i, j)), _prev_halo_spec(tl, tc, lambda j: j),
                  pl.BlockSpec((tl, tc), lambda i, j: (i, nj + j)), _prev_halo_spec(tl, tc, lambda j: nj + j),
                  pl.BlockSpec((K, tc), lambda i, j: (0, j)), pl.BlockSpec((K, tc), lambda i, j: (0, nj + j)),
                  pl.BlockSpec((1, tc), lambda i, j: (0, j)), pl.BlockSpec((1, tc), lambda i, j: (0, nj + j))],
        out_specs=pl.BlockSpec((tl, tc), lambda i, j: (i, j)),
        out_shape=jax.ShapeDtypeStruct((T, D_FF), BF16),
        scratch_shapes=[pltpu.VMEM((tl + HALO, tc), F32), pltpu.VMEM((tl + HALO, tc), F32)],
        compiler_params=_cparams("parallel", "parallel"),
    )(up, up, up, up, w, w, b, b)


def ffn_bwd_fused(up, w, b, d_act, *, name, tile=512, tc=256):
    T = up.shape[0]
    K = w.shape[0]
    tl = min(T, tile)
    nj = D_FF // tc
    nblk = T // HALO
    ext = tl + HALO

    def body(xg, hgp, hgn, xv, hvp, hvn, wg, wv, bg, bv, d, dn, og, ov, dwg, dwv, dbg, dbv, bufg, bufv, dgb, dvb):
        i = pl.program_id(1)
        last = pl.num_programs(1) - 1

        def fill(buf, x_ref, prev_ref, next_ref):
            buf[0:HALO, :] = jnp.where(i > 0, prev_ref[...].astype(F32), 0.0)
            buf[HALO:HALO + tl, :] = x_ref[...].astype(F32)
            buf[HALO + tl:, :] = jnp.where(i < last, next_ref[...].astype(F32), 0.0)

        def conv_ext(buf, w_ref, b_ref):
            acc = b_ref[...].astype(F32) + jnp.zeros((ext, tc), F32)
            for k in range(K):
                acc = acc + buf[pl.ds(HALO - (K - 1 - k), ext), :] * w_ref[k:k + 1, :].astype(F32)
            return acc

        fill(bufg, xg, hgp, hgn)
        fill(bufv, xv, hvp, hvn)
        gate = conv_ext(bufg, wg, bg)
        val = conv_ext(bufv, wv, bv)
        dd = jnp.concatenate([d[...].astype(F32), jnp.where(i < last, dn[...].astype(F32)[0:HALO], 0.0)], axis=0)
        s = _sigmoid(gate)
        dgb[...] = dd * val * (s * (1.0 + gate * (1.0 - s)))
        dvb[...] = dd * (gate * s)

        @pl.when(i == 0)
        def _():
            for ref in (dwg, dwv, dbg, dbv):
                ref[...] = jnp.zeros_like(ref)

        for dbuf, xbuf, w_ref, o_ref, dw_ref, db_ref in ((dgb, bufg, wg, og, dwg, dbg), (dvb, bufv, wv, ov, dwv, dbv)):
            xin = xbuf[HALO:HALO + tl, :]
            dx = jnp.zeros((tl, tc), F32)
            for k in range(K):
                sft = K - 1 - k
                shifted = dbuf[pl.ds(sft, tl), :]
                dx = dx + shifted * w_ref[k:k + 1, :].astype(F32)
                dw_ref[k:k + 1, :] += jnp.sum(shifted * xin, axis=0, keepdims=True)
            db_ref[...] += jnp.sum(dbuf[0:tl, :], axis=0, keepdims=True)
            o_ref[...] = dx.astype(o_ref.dtype)

    prev = lambda i: jnp.maximum(i * (tl // HALO) - 1, 0)
    nxt = lambda i: jnp.minimum((i + 1) * (tl // HALO), nblk - 1)

    def x_specs(col):
        return [pl.BlockSpec((tl, tc), lambda j, i: (i, col(j))), pl.BlockSpec((HALO, tc), lambda j, i: (prev(i), col(j))),
                pl.BlockSpec((HALO, tc), lambda j, i: (nxt(i), col(j)))]

    gcol, vcol = (lambda j: j), (lambda j: nj + j)
    in_specs = (x_specs(gcol) + x_specs(vcol)
                + [pl.BlockSpec((K, tc), lambda j, i: (0, j)), pl.BlockSpec((K, tc), lambda j, i: (0, nj + j)),
                   pl.BlockSpec((1, tc), lambda j, i: (0, j)), pl.BlockSpec((1, tc), lambda j, i: (0, nj + j)),
                   pl.BlockSpec((tl, tc), lambda j, i: (i, j)),
                   pl.BlockSpec((2 * HALO, tc), lambda j, i: (jnp.minimum((i + 1) * (tl // (2 * HALO)), nblk // 2 - 1), j))])
    row_out = pl.BlockSpec((tl, tc), lambda j, i: (i, j))
    w_out = pl.BlockSpec((K, tc), lambda j, i: (0, j))
    b_out = pl.BlockSpec((1, tc), lambda j, i: (0, j))
    return pl.pallas_call(
        body, name=name, grid=(nj, T // tl), in_specs=in_specs,
        out_specs=[row_out, row_out, w_out, w_out, b_out, b_out],
        out_shape=[jax.ShapeDtypeStruct((T, D_FF), BF16)] * 2 + [jax.ShapeDtypeStruct((K, D_FF), F32)] * 2
        + [jax.ShapeDtypeStruct((1, D_FF), F32)] * 2,
        scratch_shapes=[pltpu.VMEM((tl + 2 * HALO, tc), F32)] * 2 + [pltpu.VMEM((ext, tc), F32)] * 2,
        compiler_params=_cparams("parallel", "arbitrary"),
    )(up, up, up, up, up, up, w, w, b, b, d_act, d_act)


def conv_bwd_b(du, x, x_off, w, *, name, out_dtype, tile=512, tc=256):
    T, C = du.shape
    K = w.shape[0]
    tl = min(T, tile)
    c0 = x_off // tc
    nblk = T // HALO

    def body(du_ref, nx_ref, x_ref, w_ref, dx_ref, dw_ref, db_ref, dbuf):
        i = pl.program_id(1)
        last = pl.num_programs(1) - 1
        d = du_ref[...].astype(F32)
        dbuf[0:tl, :] = d
        dbuf[tl:, :] = jnp.where(i < last, nx_ref[...].astype(F32), 0.0)

        @pl.when(i == 0)
        def _():
            dw_ref[...] = jnp.zeros_like(dw_ref)
            db_ref[...] = jnp.zeros_like(db_ref)

        xin = x_ref[...].astype(F32)
        dx = jnp.zeros((tl, tc), F32)
        for k in range(K):
            s = K - 1 - k
            shifted = dbuf[pl.ds(s, tl), :]
            dx = dx + shifted * w_ref[k:k + 1, :].astype(F32)
            dw_ref[k:k + 1, :] += jnp.sum(shifted * xin, axis=0, keepdims=True)
        db_ref[...] += jnp.sum(d, axis=0, keepdims=True)
        dx_ref[...] = dx.astype(dx_ref.dtype)

    return pl.pallas_call(
        body, name=name, grid=(C // tc, T // tl),
        in_specs=[pl.BlockSpec((tl, tc), lambda j, i: (i, j)),
                  pl.BlockSpec((HALO, tc), lambda j, i: (jnp.minimum((i + 1) * (tl // HALO), nblk - 1), j)),
                  pl.BlockSpec((tl, tc), lambda j, i: (i, c0 + j)),
                  pl.BlockSpec((K, tc), lambda j, i: (0, j))],
        out_specs=[pl.BlockSpec((tl, tc), lambda j, i: (i, j)), pl.BlockSpec((K, tc), lambda j, i: (0, j)),
                   pl.BlockSpec((1, tc), lambda j, i: (0, j))],
        out_shape=[jax.ShapeDtypeStruct((T, C), out_dtype), jax.ShapeDtypeStruct((K, C), F32),
                   jax.ShapeDtypeStruct((1, C), F32)],
        scratch_shapes=[pltpu.VMEM((tl + HALO, tc), F32)],
        compiler_params=_cparams("parallel", "arbitrary"),
    )(du, du, x, w)


def _attn_tiles(T):
    return min(T, 1024), min(T, 256)


def _after_diag(keys, queries, strict):
    d = lax.broadcasted_iota(jnp.int32, (keys, queries), 1) - lax.broadcasted_iota(jnp.int32, (keys, queries), 0)
    return d > 0 if strict else d >= 0


def _log_gates(z):
    l1p = jnp.log(1.0 + jnp.exp(-jnp.abs(z)))
    a = jnp.minimum(z, 0.0) - l1p
    return a, a - z


def _causal_sweep(i, tq, tk, block, descending, keep_going=None, first_block=None):
    nb = tq // tk
    n_full = i * nb

    def band():
        order = reversed(range(nb)) if descending else range(nb)
        for bb in order:
            block(pl.multiple_of(i * tq + bb * tk, tk), bb * tk, True)

    def full():
        if descending and keep_going is not None:
            def step(j):
                block(pl.multiple_of((n_full - 1 - j) * tk, tk), 0, False)
                return j + 1
            done = lax.while_loop(lambda j: jnp.logical_and(j < n_full, keep_going()), step, jnp.int32(0))
            return n_full - done

        def step(j, c):
            kb = (n_full - 1 - j) if descending else j
            block(pl.multiple_of(kb * tk, tk), 0, False)
            return c
        lax.fori_loop(0 if first_block is None else first_block, n_full, step, 0)
        return None

    if descending:
        band()
        return full()
    full()
    band()
    return None


def sb_fwd(q, k, v, *, name):
    H, dh, T = q.shape
    tq, tk = _attn_tiles(T)

    def body(q_ref, k_ref, v_ref, y_ref, bt_ref, first_ref, acc, run):
        acc[...] = jnp.zeros_like(acc)
        run[...] = jnp.zeros_like(run)
        u_after = _tri(tk, lambda r, c: r < c)

        def block(k0, r0, masked):
            kb = k_ref[pl.ds(k0, tk), :]
            vb = v_ref[pl.ds(k0, tk), :]
            z = jnp.dot(kb, q_ref[:, r0:], preferred_element_type=F32)
            a, b = _log_gates(z)
            if masked:
                valid = _after_diag(tk, tq - r0, True)
                b = jnp.where(valid, b, 0.0)
            w = jnp.exp(a + _tri_dot(u_after, b, 2) + run[:, r0:])
            if masked:
                w = jnp.where(valid, w, 0.0)
            acc[:, r0:] += lax.dot_general(vb, w.astype(BF16), TN, preferred_element_type=F32)
            run[:, r0:] += jnp.sum(b, axis=0, keepdims=True)

        first = _causal_sweep(pl.program_id(1), tq, tk, block, descending=True,
                              keep_going=lambda: jnp.max(run[...]) >= SB_ZERO_BELOW)
        y_ref[...] = acc[...]
        bt_ref[...] = run[...]
        first_ref[...] = jnp.zeros(first_ref.shape, F32) + first.astype(F32)

    return pl.pallas_call(
        body, name=name, grid=(H, T // tq),
        in_specs=[pl.BlockSpec((None, dh, tq), lambda h, i: (h, 0, i)),
                  pl.BlockSpec((None, T, dh), lambda h, i: (h, 0, 0)),
                  pl.BlockSpec((None, T, dh), lambda h, i: (h, 0, 0))],
        out_specs=[pl.BlockSpec((None, dh, tq), lambda h, i: (h, 0, i)),
                   pl.BlockSpec((None, 1, tq), lambda h, i: (h, 0, i)),
                   pl.BlockSpec((None, None, HALO, LANES), lambda h, i: (h, i, 0, 0))],
        out_shape=[jax.ShapeDtypeStruct((H, dh, T), F32), jax.ShapeDtypeStruct((H, 1, T), F32),
                   jax.ShapeDtypeStruct((H, T // tq, HALO, LANES), F32)],
        scratch_shapes=[pltpu.VMEM((dh, tq), F32), pltpu.VMEM((1, tq), F32)],
        compiler_params=_cparams("parallel", "parallel"),
    )(q, k, v)


def sb_bwd(q, k, v, dy, btot, first, *, name, q_scale):
    H, dh, T = q.shape
    tq, tk = _attn_tiles(T)

    def body(q_ref, k_ref, v_ref, dy_ref, bt_ref, first_ref, dq_ref, dk_ref, dv_ref, dq, pb, pg, dyb):
        @pl.when(pl.program_id(1) == 0)
        def _():
            dk_ref[...] = jnp.zeros_like(dk_ref)
            dv_ref[...] = jnp.zeros_like(dv_ref)

        dq[...] = jnp.zeros_like(dq)
        pb[...] = jnp.zeros_like(pb)
        pg[...] = jnp.zeros_like(pg)
        dyb[...] = dy_ref[...].astype(BF16)
        u_upto = _tri(tk, lambda r, c: r >= c)
        u_before = _tri(tk, lambda r, c: r > c)

        def block(k0, r0, masked):
            kb = k_ref[pl.ds(k0, tk), :]
            vb = v_ref[pl.ds(k0, tk), :]
            qv = q_ref[:, r0:]
            dyv = dyb[:, r0:]
            z = jnp.dot(kb, qv, preferred_element_type=F32)
            a, b = _log_gates(z)
            if masked:
                valid = _after_diag(tk, tq - r0, True)
                b = jnp.where(valid, b, 0.0)
            w = jnp.exp(a + (bt_ref[:, r0:] - pb[:, r0:] - _tri_dot(u_upto, b, 2)))
            if masked:
                w = jnp.where(valid, w, 0.0)
            g = w * jnp.dot(vb, dyv, preferred_element_type=F32)
            dz = g - jnp.exp(a) * (g + pg[:, r0:] + _tri_dot(u_before, g, 2))
            if masked:
                dz = jnp.where(valid, dz, 0.0)
            dz = dz.astype(BF16)
            dq[:, r0:] += lax.dot_general(kb, dz, TN, preferred_element_type=F32)
            dk_ref[pl.ds(k0, tk), :] += lax.dot_general(dz, qv, NT, preferred_element_type=F32)
            dv_ref[pl.ds(k0, tk), :] += lax.dot_general(w.astype(BF16), dyv, NT, preferred_element_type=F32)
            pb[:, r0:] += jnp.sum(b, axis=0, keepdims=True)
            pg[:, r0:] += jnp.sum(g, axis=0, keepdims=True)

        i = pl.program_id(1)
        first = jnp.clip(jnp.max(first_ref[...]).astype(jnp.int32), 0, i * (tq // tk))
        _causal_sweep(i, tq, tk, block, descending=False, first_block=first)
        dq_ref[...] = dq[...] * q_scale

    return pl.pallas_call(
        body, name=name, grid=(H, T // tq),
        in_specs=[pl.BlockSpec((None, dh, tq), lambda h, i: (h, 0, i)),
                  pl.BlockSpec((None, T, dh), lambda h, i: (h, 0, 0)),
                  pl.BlockSpec((None, T, dh), lambda h, i: (h, 0, 0)),
                  pl.BlockSpec((None, dh, tq), lambda h, i: (h, 0, i)),
                  pl.BlockSpec((None, 1, tq), lambda h, i: (h, 0, i)),
                  pl.BlockSpec((None, None, HALO, LANES), lambda h, i: (h, i, 0, 0))],
        out_specs=[pl.BlockSpec((None, dh, tq), lambda h, i: (h, 0, i)),
                   pl.BlockSpec((None, T, dh), lambda h, i: (h, 0, 0)),
                   pl.BlockSpec((None, T, dh), lambda h, i: (h, 0, 0))],
        out_shape=[jax.ShapeDtypeStruct((H, dh, T), F32), jax.ShapeDtypeStruct((H, T, dh), F32),
                   jax.ShapeDtypeStruct((H, T, dh), F32)],
        scratch_shapes=[pltpu.VMEM((dh, tq), F32), pltpu.VMEM((1, tq), F32), pltpu.VMEM((1, tq), F32),
                        pltpu.VMEM((dh, tq), BF16)],
        compiler_params=_cparams("parallel", "arbitrary"),
    )(q, k, v, dy, btot, first)


NEG = -1e30
SB_ZERO_BELOW = -105.0


def _call_with_exchange(body, exchange, *, name, grid, in_specs, out_specs, out_shape, scratch_shapes, args):
    if exchange is None:
        return pl.pallas_call(body, name=name, grid=grid, in_specs=in_specs, out_specs=out_specs, out_shape=out_shape,
                              scratch_shapes=scratch_shapes, compiler_params=_cparams("parallel", "arbitrary"))(*args)
    kind, src = exchange
    n_in, n_out, n_scr = len(in_specs), len(out_specs), len(scratch_shapes)
    R = src.shape[-2]

    def wrapped(*refs):
        ins, src_ref = refs[:n_in], refs[n_in]
        outs, xout = refs[n_in + 1:n_in + 1 + n_out], refs[n_in + 1 + n_out]
        scr = refs[n_in + 2 + n_out:n_in + 2 + n_out + n_scr]
        start, finish = _direct_exchange(kind, src_ref, xout, *refs[-3:])
        step = pl.program_id(0) * pl.num_programs(1) + pl.program_id(1)
        pl.when(step == 0)(start)
        body(*ins, *outs, *scr)
        pl.when(step == pl.num_programs(0) * pl.num_programs(1) - 1)(finish)

    return pl.pallas_call(
        wrapped, name=name, grid=grid, in_specs=list(in_specs) + [HBM_SPEC], out_specs=list(out_specs) + [HBM_SPEC],
        out_shape=list(out_shape) + [jax.ShapeDtypeStruct((8, R, LANES), src.dtype)],
        scratch_shapes=list(scratch_shapes) + [pltpu.SemaphoreType.DMA((7,)), pltpu.SemaphoreType.DMA((7,)),
                                               pltpu.SemaphoreType.DMA],
        compiler_params=_cparams("arbitrary", "arbitrary"))(*args, src)


def mla_fwd(q, k, v, *, name, exchange=None):
    H, dk, T = q.shape
    dv = v.shape[1] // H
    tq, tk = _attn_tiles(T)

    def body(q_ref, k_ref, v_ref, o_ref, l_ref, acc, m_s, l_s):
        acc[...] = jnp.zeros_like(acc)
        m_s[...] = jnp.full_like(m_s, NEG)
        l_s[...] = jnp.zeros_like(l_s)

        def block(k0, r0, masked):
            kb = k_ref[pl.ds(k0, tk), :]
            vb = v_ref[pl.ds(k0, tk), :]
            s = jnp.dot(kb, q_ref[:, r0:], preferred_element_type=F32)
            if masked:
                s = jnp.where(_after_diag(tk, tq - r0, False), s, NEG)
            m = m_s[:, r0:]
            m_new = jnp.maximum(m, jnp.max(s, axis=0, keepdims=True))
            p = jnp.exp(s - m_new)
            alpha = jnp.exp(m - m_new)
            l_s[:, r0:] = alpha * l_s[:, r0:] + jnp.sum(p, axis=0, keepdims=True)
            acc[:, r0:] = alpha * acc[:, r0:] + lax.dot_general(vb, p.astype(BF16), TN, preferred_element_type=F32)
            m_s[:, r0:] = m_new

        _causal_sweep(pl.program_id(1), tq, tk, block, descending=False)
        o_ref[...] = acc[...] / l_s[...]
        l_ref[...] = m_s[...] + jnp.log(l_s[...])

    return _call_with_exchange(
        body, exchange, name=name, grid=(H, T // tq),
        in_specs=[pl.BlockSpec((None, dk, tq), lambda h, i: (h, 0, i)),
                  pl.BlockSpec((T, dk), lambda h, i: (0, h)),
                  pl.BlockSpec((T, dv), lambda h, i: (0, h))],
        out_specs=[pl.BlockSpec((None, dv, tq), lambda h, i: (h, 0, i)),
                   pl.BlockSpec((None, 1, tq), lambda h, i: (h, 0, i))],
        out_shape=[jax.ShapeDtypeStruct((H, dv, T), F32), jax.ShapeDtypeStruct((H, 1, T), F32)],
        scratch_shapes=[pltpu.VMEM((dv, tq), F32), pltpu.VMEM((1, tq), F32), pltpu.VMEM((1, tq), F32)],
        args=(q, k, v))


def mla_bwd(q, k, v, do, o, lse, *, name, exchange=None):
    H, dk, T = q.shape
    dv = v.shape[1] // H
    tq, tk = _attn_tiles(T)

    def body(q_ref, k_ref, v_ref, do_ref, o_ref, l_ref, dq_ref, dk_ref, dv_ref, dq, delta, dob):
        @pl.when(pl.program_id(1) == 0)
        def _():
            dk_ref[...] = jnp.zeros_like(dk_ref)
            dv_ref[...] = jnp.zeros_like(dv_ref)

        dq[...] = jnp.zeros_like(dq)
        dov = do_ref[...].astype(F32)
        dob[...] = dov.astype(BF16)
        delta[...] = jnp.sum(dov * o_ref[...], axis=0, keepdims=True)

        def block(k0, r0, masked):
            kb = k_ref[pl.ds(k0, tk), :]
            vb = v_ref[pl.ds(k0, tk), :]
            qv = q_ref[:, r0:]
            dov_b = dob[:, r0:]
            s = jnp.dot(kb, qv, preferred_element_type=F32)
            p = jnp.exp(s - l_ref[:, r0:])
            if masked:
                p = jnp.where(_after_diag(tk, tq - r0, False), p, 0.0)
            dp = jnp.dot(vb, dov_b, preferred_element_type=F32)
            ds = (p * (dp - delta[:, r0:])).astype(BF16)
            dq[:, r0:] += lax.dot_general(kb, ds, TN, preferred_element_type=F32)
            dk_ref[pl.ds(k0, tk), :] += lax.dot_general(ds, qv, NT, preferred_element_type=F32)
            dv_ref[pl.ds(k0, tk), :] += lax.dot_general(p.astype(BF16), dov_b, NT, preferred_element_type=F32)

        _causal_sweep(pl.program_id(1), tq, tk, block, descending=False)
        dq_ref[...] = dq[...]

    return _call_with_exchange(
        body, exchange, name=name, grid=(H, T // tq),
        in_specs=[pl.BlockSpec((None, dk, tq), lambda h, i: (h, 0, i)),
                  pl.BlockSpec((T, dk), lambda h, i: (0, h)),
                  pl.BlockSpec((T, dv), lambda h, i: (0, h)),
                  pl.BlockSpec((None, dv, tq), lambda h, i: (h, 0, i)),
                  pl.BlockSpec((None, dv, tq), lambda h, i: (h, 0, i)),
                  pl.BlockSpec((None, 1, tq), lambda h, i: (h, 0, i))],
        out_specs=[pl.BlockSpec((None, dk, tq), lambda h, i: (h, 0, i)),
                   pl.BlockSpec((T, dk), lambda h, i: (0, h)),
                   pl.BlockSpec((T, dv), lambda h, i: (0, h))],
        out_shape=[jax.ShapeDtypeStruct((H, dk, T), F32), jax.ShapeDtypeStruct((T, H * dk), F32),
                   jax.ShapeDtypeStruct((T, H * dv), F32)],
        scratch_shapes=[pltpu.VMEM((dk, tq), F32), pltpu.VMEM((1, tq), F32), pltpu.VMEM((dv, tq), BF16)],
        args=(q, k, v, do, o, lse))


def _lane_pick(x, h):
    lane = lax.broadcasted_iota(jnp.int32, (1, x.shape[1]), 1)
    return jnp.sum(jnp.where(lane == h, x, 0.0), axis=1, keepdims=True)


def _row_pick(x, h):
    sub = lax.broadcasted_iota(jnp.int32, (x.shape[0], 1), 0)
    return jnp.sum(jnp.where(sub == h, x, 0.0), axis=0, keepdims=True)


def ssd_chunk_fn(*args):
    nh, ng = SSM_HEADS, SSM_GROUPS
    xs = args[:nh]
    bs = args[nh:nh + ng]
    cs = args[nh + ng:nh + 2 * ng]
    dt_raw = args[nh + 2 * ng]
    st = args[nh + 2 * ng + 1:nh + 2 * ng + 1 + nh]
    dt_bias, a_log, d_skip = args[nh + 2 * ng + 1 + nh:]
    L = dt_raw.shape[0]
    dt = _softplus(dt_raw + dt_bias)
    da = dt * (-jnp.exp(a_log))
    dcs = csum_rows(da)
    dcs_t = dcs.T
    total = jnp.sum(da, axis=0, keepdims=True)
    causal = lax.broadcasted_iota(jnp.int32, (L, L), 0) >= lax.broadcasted_iota(jnp.int32, (L, L), 1)
    cb = [mm_nt(cs[g], bs[g]) for g in range(ng)]
    ys, new_st = [], []
    for h in range(nh):
        g = h // (nh // ng)
        dcs_h = _lane_pick(dcs, h)
        dt_h = _lane_pick(dt, h)
        tot_h = _lane_pick(total, h)
        dsk_h = _lane_pick(d_skip, h)
        decay = jnp.exp(jnp.where(causal, dcs_h - _row_pick(dcs_t, h), NEG))
        xdt = xs[h] * dt_h
        y = mm_nn(cb[g] * decay, xdt)
        y = y + mm_nn(cs[g] * jnp.exp(dcs_h), st[h])
        ys.append(y + xs[h] * dsk_h)
        new_st.append(st[h] * jnp.exp(tot_h) + mm_tn(bs[g] * jnp.exp(tot_h - dcs_h), xdt))
    return tuple(ys) + tuple(new_st)


def ssd_fwd(x_hm, b_hm, c_hm, proj, dt_bias, a_log, d_skip, *, name):
    nh, T, P = x_hm.shape
    ng, N = b_hm.shape[0], b_hm.shape[2]
    L = SSM_CHUNK
    nc = T // L
    dtb = OFF_DT // LANES

    def body(x_ref, b_ref, c_ref, dt_ref, db_ref, al_ref, ds_ref, y_ref, s_ref, state):
        @pl.when(pl.program_id(0) == 0)
        def _():
            state[...] = jnp.zeros_like(state)

        s_ref[...] = state[...]
        args = ([x_ref[h] for h in range(nh)] + [b_ref[g] for g in range(ng)] + [c_ref[g] for g in range(ng)]
                + [dt_ref[...]] + [state[h] for h in range(nh)] + [db_ref[...], al_ref[...], ds_ref[...]])
        res = ssd_chunk_fn(*args)
        for h in range(nh):
            y_ref[h] = res[h]
            state[h] = res[nh + h]

    par = pl.BlockSpec((1, LANES), lambda i: (0, 0))
    return pl.pallas_call(
        body, name=name, grid=(nc,),
        in_specs=[pl.BlockSpec((nh, L, P), lambda i: (0, i, 0)), pl.BlockSpec((ng, L, N), lambda i: (0, i, 0)),
                  pl.BlockSpec((ng, L, N), lambda i: (0, i, 0)), pl.BlockSpec((L, LANES), lambda i: (i, dtb)),
                  par, par, par],
        out_specs=[pl.BlockSpec((nh, L, P), lambda i: (0, i, 0)),
                   pl.BlockSpec((None, nh, N, P), lambda i: (i, 0, 0, 0))],
        out_shape=[jax.ShapeDtypeStruct((nh, T, P), F32), jax.ShapeDtypeStruct((nc, nh, N, P), F32)],
        scratch_shapes=[pltpu.VMEM((nh, N, P), F32)],
        compiler_params=_cparams("arbitrary"),
    )(x_hm, b_hm, c_hm, proj, dt_bias, a_log, d_skip)


def ssd_bwd(x_hm, b_hm, c_hm, proj, states, dt_bias, a_log, d_skip, dy_hm, *, name):
    nh, T, P = x_hm.shape
    ng, N = b_hm.shape[0], b_hm.shape[2]
    L = SSM_CHUNK
    nc = T // L
    dtb = OFF_DT // LANES

    def body(x_ref, b_ref, c_ref, dt_ref, s_ref, db_ref, al_ref, ds_ref, dy_ref,
             dx_ref, dbm_ref, dcm_ref, ddt_ref, gdb_ref, gal_ref, gds_ref, dstate):
        @pl.when(pl.program_id(0) == 0)
        def _():
            dstate[...] = jnp.zeros_like(dstate)
            gdb_ref[...] = jnp.zeros_like(gdb_ref)
            gal_ref[...] = jnp.zeros_like(gal_ref)
            gds_ref[...] = jnp.zeros_like(gds_ref)

        args = ([x_ref[h] for h in range(nh)] + [b_ref[g] for g in range(ng)] + [c_ref[g] for g in range(ng)]
                + [dt_ref[...]] + [s_ref[h] for h in range(nh)] + [db_ref[...], al_ref[...], ds_ref[...]])
        _, vjp = jax.vjp(ssd_chunk_fn, *args)
        g = vjp(tuple([dy_ref[h] for h in range(nh)] + [dstate[h] for h in range(nh)]))
        for h in range(nh):
            dx_ref[h] = g[h]
        for gi in range(ng):
            dbm_ref[gi] = g[nh + gi]
            dcm_ref[gi] = g[nh + ng + gi]
        ddt_ref[...] = g[nh + 2 * ng]
        for h in range(nh):
            dstate[h] = g[nh + 2 * ng + 1 + h]
        gdb_ref[...] += g[-3]
        gal_ref[...] += g[-2]
        gds_ref[...] += g[-1]

    rev = lambda i: nc - 1 - i
    par = pl.BlockSpec((1, LANES), lambda i: (0, 0))
    return pl.pallas_call(
        body, name=name, grid=(nc,),
        in_specs=[pl.BlockSpec((nh, L, P), lambda i: (0, rev(i), 0)), pl.BlockSpec((ng, L, N), lambda i: (0, rev(i), 0)),
                  pl.BlockSpec((ng, L, N), lambda i: (0, rev(i), 0)), pl.BlockSpec((L, LANES), lambda i: (rev(i), dtb)),
                  pl.BlockSpec((None, nh, N, P), lambda i: (rev(i), 0, 0, 0)), par, par, par,
                  pl.BlockSpec((nh, L, P), lambda i: (0, rev(i), 0))],
        out_specs=[pl.BlockSpec((nh, L, P), lambda i: (0, rev(i), 0)), pl.BlockSpec((ng, L, N), lambda i: (0, rev(i), 0)),
                   pl.BlockSpec((ng, L, N), lambda i: (0, rev(i), 0)), pl.BlockSpec((L, LANES), lambda i: (rev(i), 0)),
                   par, par, par],
        out_shape=[jax.ShapeDtypeStruct((nh, T, P), F32), jax.ShapeDtypeStruct((ng, T, N), F32),
                   jax.ShapeDtypeStruct((ng, T, N), F32), jax.ShapeDtypeStruct((T, LANES), F32),
                   jax.ShapeDtypeStruct((1, LANES), F32), jax.ShapeDtypeStruct((1, LANES), F32),
                   jax.ShapeDtypeStruct((1, LANES), F32)],
        scratch_shapes=[pltpu.VMEM((nh, N, P), F32)],
        compiler_params=_cparams("arbitrary"),
    )(x_hm, b_hm, c_hm, proj, states, dt_bias, a_log, d_skip, dy_hm)


def loss_head(h, target, g, *, name, tile=512):
    T, C = h.shape
    tl = min(T, tile)

    def body(h_ref, t_ref, g_ref, dh_ref, dg_ref, ls_ref):
        @pl.when(pl.program_id(0) == 0)
        def _():
            dg_ref[...] = jnp.zeros_like(dg_ref)
            ls_ref[...] = jnp.zeros_like(ls_ref)

        (y,), vjp = jax.vjp(rms_fn, h_ref[...], g_ref[...])
        err = y - t_ref[...]
        ls_ref[...] += jnp.sum(err * err, axis=0, keepdims=True) * (0.5 / C)
        dh, dg = vjp((err * (1.0 / C),))
        dh_ref[...] = dh
        dg_ref[...] += dg

    row = pl.BlockSpec((tl, C), lambda i: (i, 0))
    par = pl.BlockSpec((1, C), lambda i: (0, 0))
    return pl.pallas_call(
        body, name=name, grid=(T // tl,), in_specs=[row, row, par], out_specs=[row, par, par],
        out_shape=[jax.ShapeDtypeStruct((T, C), F32), jax.ShapeDtypeStruct((1, C), F32),
                   jax.ShapeDtypeStruct((1, C), F32)],
        compiler_params=_cparams("arbitrary"),
    )(h, target, g)


def adamw(w, g, m, v, *, name):
    R, C = w.shape
    tr = R
    for d in range(8, min(R, 512) + 1, 8):
        if R % d == 0:
            tr = d
    c1 = 1.0 - ADAM_B1 ** ADAM_STEP
    c2 = 1.0 - ADAM_B2 ** ADAM_STEP

    def body(w_ref, g_ref, m_ref, v_ref, d_ref, nm_ref, nv_ref):
        gv = g_ref[...]
        nm = ADAM_B1 * m_ref[...] + (1.0 - ADAM_B1) * gv
        nv = ADAM_B2 * v_ref[...] + (1.0 - ADAM_B2) * (gv * gv)
        d_ref[...] = -ADAM_LR * ((nm / c1) / (jnp.sqrt(nv / c2) + ADAM_EPS) + ADAM_WD * w_ref[...])
        nm_ref[...] = nm
        nv_ref[...] = nv

    spec = pl.BlockSpec((tr, C), lambda i: (i, 0))
    return pl.pallas_call(
        body, name=name, grid=(R // tr,), in_specs=[spec] * 4, out_specs=[spec] * 3,
        out_shape=[jax.ShapeDtypeStruct((R, C), F32)] * 3,
        compiler_params=_cparams("parallel"),
    )(w, g, m, v)


MESH = pl.DeviceIdType.MESH
HBM_SPEC = pl.BlockSpec(memory_space=pltpu.HBM)


def _place():
    return lax.axis_index("x"), lax.axis_index("y"), lax.axis_index("c")


def _direct_exchange(kind, src_ref, out_ref, send_sems, recv_sems, local_sem):
    x, y, c = _place()
    me = 4 * x + 2 * y + c

    def block_for(dest):
        return src_ref if kind == "gather" else src_ref.at[dest]

    own = pltpu.make_async_copy(block_for(me), out_ref.at[me], local_sem)
    sends, arrivals = [], []
    for f in range(1, 8):
        px = jnp.where((f >> 2) & 1, 1 - x, x)
        py = jnp.where((f >> 1) & 1, 1 - y, y)
        pc = jnp.where(f & 1, 1 - c, c)
        peer = 4 * px + 2 * py + pc
        for dst, group in ((out_ref.at[me], sends), (out_ref.at[peer], arrivals)):
            group.append(pltpu.make_async_remote_copy(
                src_ref=block_for(peer), dst_ref=dst, send_sem=send_sems.at[f - 1], recv_sem=recv_sems.at[f - 1],
                device_id=(px, py, pc), device_id_type=MESH))

    def start():
        own.start()
        for cp in sends:
            cp.start()

    def finish():
        for cp in arrivals:
            cp.wait_recv()
        for cp in sends:
            cp.wait_send()
        own.wait()

    return start, finish


def allgather_blocks(mine, *, name):
    R = mine.shape[0]

    def body(x_ref, out_ref, send_sems, recv_sems, local_sem):
        x, y, c = _place()
        me, sibling = (x, y, c), (x, y, 1 - c)
        chips = [(1 - x, y), (x, 1 - y), (1 - x, 1 - y)]

        def slot(px, py, pc):
            return out_ref.at[4 * px + 2 * py + pc]

        def copy(k, block, to, src=None):
            return pltpu.make_async_remote_copy(
                src_ref=slot(*block) if src is None else src, dst_ref=slot(*block),
                send_sem=send_sems.at[k], recv_sem=recv_sems.at[k], device_id=to, device_id_type=MESH)

        own = pltpu.make_async_copy(x_ref, slot(*me), local_sem)
        own.start()
        first = [copy(0, me, sibling, src=x_ref)]
        first += [copy(1 + j, me, (*chip, c), src=x_ref) for j, chip in enumerate(chips)]
        for cp in first:
            cp.start()
        passed = [copy(4 + j, (*chip, c), sibling) for j, chip in enumerate(chips)]
        for j, chip in enumerate(chips):
            copy(1 + j, (*chip, c), me).wait_recv()
            passed[j].start()
        copy(0, sibling, me).wait_recv()
        for j, chip in enumerate(chips):
            copy(4 + j, (*chip, 1 - c), me).wait_recv()
        for cp in first + passed:
            cp.wait_send()
        own.wait()

    return pl.pallas_call(
        body, name=name, out_shape=jax.ShapeDtypeStruct((8, R, LANES), mine.dtype),
        in_specs=[HBM_SPEC], out_specs=HBM_SPEC,
        scratch_shapes=[pltpu.SemaphoreType.DMA((7,)), pltpu.SemaphoreType.DMA((7,)), pltpu.SemaphoreType.DMA],
    )(mine)


def allgather_direct(mine, *, name):
    R = mine.shape[0]

    def body(x_ref, out_ref, send_sems, recv_sems, local_sem):
        start, finish = _direct_exchange("gather", x_ref, out_ref, send_sems, recv_sems, local_sem)
        start()
        finish()

    return pl.pallas_call(
        body, name=name, out_shape=jax.ShapeDtypeStruct((8, R, LANES), mine.dtype),
        in_specs=[HBM_SPEC], out_specs=HBM_SPEC,
        scratch_shapes=[pltpu.SemaphoreType.DMA((7,)), pltpu.SemaphoreType.DMA((7,)), pltpu.SemaphoreType.DMA],
    )(mine)


def send_to_sibling(v, *, name):
    def body(v_ref, out_ref, send_sem, recv_sem):
        x, y, c = _place()
        cp = pltpu.make_async_remote_copy(src_ref=v_ref, dst_ref=out_ref, send_sem=send_sem, recv_sem=recv_sem,
                                          device_id=(x, y, 1 - c), device_id_type=MESH)
        cp.start()
        cp.wait()

    return pl.pallas_call(
        body, name=name, out_shape=jax.ShapeDtypeStruct(v.shape, v.dtype), in_specs=[HBM_SPEC], out_specs=HBM_SPEC,
        scratch_shapes=[pltpu.SemaphoreType.DMA, pltpu.SemaphoreType.DMA],
    )(v)


def chip_exchange(p, *, name):
    R = p.shape[1]

    def body(p_ref, out_ref, send_sems, recv_sems):
        x, y, c = _place()
        chips = [(1 - x, y), (x, 1 - y), (1 - x, 1 - y)]
        sends = [pltpu.make_async_remote_copy(
            src_ref=p_ref.at[2 * px + py], dst_ref=out_ref.at[j], send_sem=send_sems.at[j], recv_sem=recv_sems.at[j],
            device_id=(px, py, c), device_id_type=MESH) for j, (px, py) in enumerate(chips)]
        for cp in sends:
            cp.start()
        for cp in sends:
            cp.wait()

    return pl.pallas_call(
        body, name=name, out_shape=jax.ShapeDtypeStruct((3, R, LANES), p.dtype), in_specs=[HBM_SPEC],
        out_specs=HBM_SPEC,
        scratch_shapes=[pltpu.SemaphoreType.DMA((3,)), pltpu.SemaphoreType.DMA((3,))],
    )(p)


def add_blocks(terms, out_dtype, *, name, tile=1024):
    R = terms[0].shape[0]
    tr = R
    for d in range(16, min(R, tile) + 1, 16):
        if R % d == 0:
            tr = d

    def body(*refs):
        acc = refs[0][...].astype(F32)
        for ref in refs[1:-1]:
            acc = acc + ref[...].astype(F32)
        refs[-1][...] = acc.astype(out_dtype)

    spec = pl.BlockSpec((tr, LANES), lambda i: (i, 0))
    return pl.pallas_call(
        body, name=name, grid=(R // tr,), in_specs=[spec] * len(terms), out_specs=spec,
        out_shape=jax.ShapeDtypeStruct((R, LANES), out_dtype), compiler_params=_cparams("parallel"),
    )(*terms)


FLAT_ROW_STEP = 640


def _half_rows(arr, cc):
    hr = arr.shape[0] // 2
    return lax.dynamic_slice_in_dim(arr, cc * hr, hr, axis=0).reshape(-1)


def _flat_half(shards, cc, dtype):
    flat = jnp.concatenate([_half_rows(shards[n], cc).astype(dtype) for n in BIG])
    rows = -(-flat.shape[0] // (FLAT_ROW_STEP * LANES)) * FLAT_ROW_STEP
    return jnp.pad(flat, (0, rows * LANES - flat.shape[0])).reshape(rows, LANES)


def _unflat_halves(flat_by_c, shapes):
    out, off = {}, 0
    for n in BIG:
        R, C = shapes[n]
        sz = (R // 2) * C
        out[n] = jnp.concatenate([flat_by_c[c][off:off + sz].reshape(R // 2, C) for c in range(2)], axis=0)
        off += sz
    return out


def _to_heads(a, nh):
    T = a.shape[0]
    return a.reshape(T, nh, a.shape[1] // nh).transpose(1, 0, 2)


def _from_heads(a):
    nh, T, d = a.shape
    return a.transpose(1, 0, 2).reshape(T, nh * d)


def _to_heads_t(a, nh):
    T = a.shape[0]
    return a.reshape(T, nh, a.shape[1] // nh).transpose(1, 2, 0)


def _from_heads_t(a):
    nh, d, T = a.shape
    return a.transpose(2, 0, 1).reshape(T, nh * d)


def _pad_cols(a, n):
    return jnp.pad(a, ((0, 0), (0, n - a.shape[1])))


def _pack_w_in(w):
    offs = [sum(IN_SPLITS[:i]) for i in range(len(IN_SPLITS) + 1)]
    sb, z, xbc, dt, cq, ckv, kr = [w[:, offs[i]:offs[i + 1]] for i in range(len(IN_SPLITS))]
    zeros = lambda n: jnp.zeros((w.shape[0], n), w.dtype)
    h = MLA_ROPE // 2
    kra = jnp.concatenate([zeros(MLA_NOPE), kr, zeros(LANES - MLA_QK)], axis=1)
    krb = jnp.concatenate([zeros(MLA_NOPE), -kr[:, h:], kr[:, :h], zeros(LANES - MLA_QK)], axis=1)
    return sb, jnp.concatenate([z, xbc, cq, ckv, _pad_cols(dt, LANES), kra, krb], axis=1)


def _unpack_gw_in(g_sb, g):
    h = MLA_ROPE // 2
    ga, gb = g[:, OFF_KRA:OFF_KRA + LANES], g[:, OFF_KRB:OFF_KRB + LANES]
    gkr = ga[:, MLA_NOPE:MLA_QK] + jnp.concatenate([gb[:, MLA_NOPE + h:MLA_QK], -gb[:, MLA_NOPE:MLA_NOPE + h]], axis=1)
    return jnp.concatenate([g_sb, g[:, OFF_Z:OFF_Z + 512], g[:, OFF_XBC:OFF_XBC + 768],
                            g[:, OFF_DT:OFF_DT + 8], g[:, OFF_CQ:OFF_CQ + 256], g[:, OFF_CKV:OFF_CKV + 128], gkr], axis=1)


def _pack_w_uq(w):
    zeros = lambda n: jnp.zeros((w.shape[0], n), w.dtype)
    h = MLA_ROPE // 2
    pp, rr = [], []
    for i in range(MLA_HEADS):
        nope = w[:, MLA_QK * i:MLA_QK * i + MLA_NOPE]
        rope = w[:, MLA_QK * i + MLA_NOPE:MLA_QK * (i + 1)]
        pp += [nope, rope, zeros(LANES - MLA_QK)]
        rr += [zeros(MLA_NOPE), -rope[:, h:], rope[:, :h], zeros(LANES - MLA_QK)]
    return jnp.concatenate(pp, axis=1), jnp.concatenate(rr, axis=1)


def _unpack_gw_uq(gp, gr):
    h = MLA_ROPE // 2
    out = []
    for i in range(MLA_HEADS):
        b = LANES * i
        out.append(gp[:, b:b + MLA_NOPE])
        out.append(gp[:, b + MLA_NOPE:b + MLA_NOPE + h] + gr[:, b + MLA_NOPE + h:b + MLA_QK])
        out.append(gp[:, b + MLA_NOPE + h:b + MLA_QK] - gr[:, b + MLA_NOPE:b + MLA_NOPE + h])
    return jnp.concatenate(out, axis=1)


def _pack_w_ukv(w):
    zeros = lambda n: jnp.zeros((w.shape[0], n), w.dtype)
    kk, vv = [], []
    for i in range(MLA_HEADS):
        b = (MLA_NOPE + MLA_V) * i
        kk += [w[:, b:b + MLA_NOPE], zeros(LANES - MLA_NOPE)]
        vv += [w[:, b + MLA_NOPE:b + MLA_NOPE + MLA_V], zeros(LANES - MLA_V)]
    return jnp.concatenate(kk, axis=1), jnp.concatenate(vv, axis=1)


def _unpack_gw_ukv(gk, gv):
    out = []
    for i in range(MLA_HEADS):
        out += [gk[:, LANES * i:LANES * i + MLA_NOPE], gv[:, LANES * i:LANES * i + MLA_V]]
    return jnp.concatenate(out, axis=1)


def _rope_tables(positions):
    inv_freq = 1.0 / (ROPE_THETA ** (jnp.arange(0, MLA_ROPE, 2, dtype=F32) / MLA_ROPE))
    ang = positions.astype(F32)[:, None] * inv_freq
    cos, sin = jnp.cos(ang), jnp.sin(ang)
    T = positions.shape[0]
    one, zero = jnp.ones((T, MLA_NOPE), F32), jnp.zeros((T, MLA_NOPE), F32)
    pad1, pad0 = jnp.ones((T, LANES - MLA_QK), F32), jnp.zeros((T, LANES - MLA_QK), F32)
    return jnp.concatenate([one, cos, cos, pad1], axis=1), jnp.concatenate([zero, sin, sin, pad0], axis=1)


def _row(v):
    return v.reshape(1, -1)


def _pad_row(v):
    return _pad_cols(v.reshape(1, -1), LANES)


def _layer_weights(full, small, li):
    p = {}
    p["w_sb"], p["w_rest"] = _pack_w_in(full["w_in"])
    q_scale = jnp.concatenate([jnp.full((1, SB_HEADS * SB_DIM), SB_DIM ** -0.5, BF16),
                               jnp.ones((1, 2 * SB_HEADS * SB_DIM), BF16)], axis=1)
    p["w_sb_fwd"] = p["w_sb"] * q_scale
    p["wqp"], p["wqr"] = _pack_w_uq(full["mla_w_uq"])
    p["wkp"], p["wvp"] = _pack_w_ukv(full["mla_w_ukv"])
    p["w_out"] = full["w_out"]
    p["w_up"] = full["ffn_w_up"]
    p["w_down"] = full["ffn_w_down"]
    for n in ("mix_norm", "sb_out_norm", "ssm_conv_b", "ssm_out_norm", "mla_q_norm", "mla_kv_norm", "mla_out_norm",
              "ffn_norm", "ffn_conv_b"):
        p[n] = _row(small[n][li])
    for n in ("ssm_dt_bias", "ssm_a_log", "ssm_d"):
        p[n] = _pad_row(small[n][li])
    p["ssm_conv_w"] = small["ssm_conv_w"][li]
    p["ffn_conv_w"] = small["ffn_conv_w"][li]
    return p


def _layer_fwd(h, p, cos, sin, li, exchange=None):
    T = h.shape[0]
    nm = lambda s: "l%d_%s" % (li, s)
    s = {"h": h}
    (n1,) = rowwise(rms_fn, [h], [p["mix_norm"]], [(D_MODEL, BF16)], name=nm("mix_norm"))
    proj = matmul(n1, p["w_rest"], name=nm("in_proj"))
    qkv = matmul(n1, p["w_sb_fwd"], name=nm("in_proj_sb"), out_dtype=BF16)
    s["n1"], s["proj"] = n1, proj
    s["sb_q"] = _to_heads_t(qkv[:, 0:256], SB_HEADS)
    s["sb_k"] = _to_heads(qkv[:, 256:512], SB_HEADS)
    s["sb_v"] = _to_heads(qkv[:, 512:768], SB_HEADS)
    y_sb_hm, s["sb_bt"], s["sb_first"] = sb_fwd(s["sb_q"], s["sb_k"], s["sb_v"], name=nm("sb_fwd"))
    s["y_sb"] = _from_heads_t(y_sb_hm)
    xbc = ssm_conv_act(proj, p["ssm_conv_w"], p["ssm_conv_b"], name=nm("ssm_conv"))
    s["x_hm"] = _to_heads(xbc[:, :SSM_INNER], SSM_HEADS)
    s["b_hm"] = _to_heads(xbc[:, SSM_INNER:SSM_INNER + 128], SSM_GROUPS)
    s["c_hm"] = _to_heads(xbc[:, SSM_INNER + 128:], SSM_GROUPS)
    y_ssm_hm, s["states"] = ssd_fwd(s["x_hm"], s["b_hm"], s["c_hm"], proj, p["ssm_dt_bias"], p["ssm_a_log"],
                                    p["ssm_d"], name=nm("ssd_fwd"))
    s["y_ssm"] = _from_heads(y_ssm_hm)
    rows = [(proj, 256, OFF_CQ // 256), (proj, 128, OFF_CKV // 128), (proj, 128, OFF_KRA // 128),
            (proj, 128, OFF_KRB // 128), cos, sin]
    qp, kp, vv = rowwise(mla_prep_fn, rows, [p["mla_q_norm"], p["mla_kv_norm"], p["wqp"], p["wqr"], p["wkp"], p["wvp"]],
                         [(512, BF16), (512, BF16), (512, BF16)], name=nm("mla_prep"))
    s["mla_q"], s["mla_k"], s["mla_v"] = _to_heads_t(qp, MLA_HEADS), kp, vv
    s["mla_o"], s["mla_lse"], *rode = mla_fwd(s["mla_q"], kp, vv, name=nm("mla_fwd"), exchange=exchange)
    s["y_mla"] = _from_heads_t(s["mla_o"][:, :MLA_V, :])
    (cat,) = rowwise(merge_fn, [s["y_sb"], s["y_ssm"], (proj, 512, OFF_Z // 512), s["y_mla"]],
                     [p["sb_out_norm"], p["ssm_out_norm"], p["mla_out_norm"]], [(D_MODEL, BF16)], name=nm("merge"),
                     post=lambda a, b, c: (jnp.concatenate([a, b, c], axis=1),))
    s["cat"] = cat
    h1 = matmul(cat, p["w_out"], name=nm("out_proj"), residual=h)
    s["h1"] = h1
    (n2,) = rowwise(rms_fn, [h1], [p["ffn_norm"]], [(D_MODEL, BF16)], name=nm("ffn_norm"))
    up = matmul(n2, p["w_up"], name=nm("ffn_up"))
    act = ffn_act(up, p["ffn_conv_w"], p["ffn_conv_b"], name=nm("ffn_act"))
    s["n2"], s["up"], s["act"] = n2, up, act
    h2 = matmul(act, p["w_down"], name=nm("ffn_down"), residual=h1)
    return h2, s, (rode[0] if rode else None)


def _layer_bwd(dh2, s, p, cos, sin, li, exchange=None):
    nm = lambda t: "l%d_%s" % (li, t)
    g = {}
    proj = s["proj"]
    g["ffn_w_down"] = matmul(s["act"], dh2, name=nm("g_w_down"), ta=True)
    d_act = matmul(dh2, p["w_down"], name=nm("d_act"), out_dtype=BF16, tb=True)
    d_up_g, d_up_v, gwg, gwv, gbg, gbv = ffn_bwd_fused(s["up"], p["ffn_conv_w"], p["ffn_conv_b"], d_act,
                                                       name=nm("ffn_act_bwd"))
    g["ffn_conv_w"] = jnp.concatenate([gwg, gwv], axis=1)
    g["ffn_conv_b"] = jnp.concatenate([gbg[0], gbv[0]])
    g["ffn_w_up"] = jnp.concatenate([matmul(s["n2"], d_up_g, name=nm("g_w_up_gate"), ta=True),
                                     matmul(s["n2"], d_up_v, name=nm("g_w_up_val"), ta=True)], axis=1)
    d_n2 = matmul(d_up_g, p["w_up"], name=nm("d_n2_gate"), tb=True)
    d_n2 = matmul(d_up_v, p["w_up"], name=nm("d_n2_val"), tb=True, b_k0=D_FF, residual=d_n2)
    (dh1,), (gn,) = rowwise_bwd(rms_fn, [s["h1"]], [], [p["ffn_norm"]], [d_n2], [F32], name=nm("ffn_norm_bwd"),
                                add0=dh2)
    g["ffn_norm"] = gn[0]
    g["w_out"] = matmul(s["cat"], dh1, name=nm("g_w_out"), ta=True)
    d_cat = matmul(dh1, p["w_out"], name=nm("d_cat"), tb=True)
    (d_ysb, d_yssm, d_z, d_ymla), (g1, g2, g3) = rowwise_bwd(
        merge_fn, [s["y_sb"], s["y_ssm"], (proj, 512, OFF_Z // 512), s["y_mla"]], [],
        [p["sb_out_norm"], p["ssm_out_norm"], p["mla_out_norm"]], [d_cat], [F32, F32, BF16, F32], name=nm("merge_bwd"),
        pre_ct=lambda d: (d[:, 0:256], d[:, 256:768], d[:, 768:1024]))
    g["sb_out_norm"], g["ssm_out_norm"], g["mla_out_norm"] = g1[0], g2[0], g3[0]
    dq, dk, dv = sb_bwd(s["sb_q"], s["sb_k"], s["sb_v"], _to_heads_t(d_ysb, SB_HEADS), s["sb_bt"], s["sb_first"], name=nm("sb_bwd"),
                        q_scale=SB_DIM ** -0.5)
    d_sb = jnp.concatenate([_from_heads_t(dq), _from_heads(dk), _from_heads(dv)], axis=1).astype(BF16)
    do_t = jnp.pad(_to_heads_t(d_ymla, MLA_HEADS), ((0, 0), (0, LANES - MLA_V), (0, 0)))
    dqp, dkp, dvv, *rode = mla_bwd(s["mla_q"], s["mla_k"], s["mla_v"], do_t, s["mla_o"], s["mla_lse"],
                                   name=nm("mla_bwd"), exchange=exchange)
    rows = [(proj, 256, OFF_CQ // 256), (proj, 128, OFF_CKV // 128), (proj, 128, OFF_KRA // 128),
            (proj, 128, OFF_KRB // 128)]
    (d_cq, d_ckv, d_kra, d_krb), (gqn, gkvn, gwqp, gwqr, gwkp, gwvp) = rowwise_bwd(
        mla_prep_fn, rows, [cos, sin], [p["mla_q_norm"], p["mla_kv_norm"], p["wqp"], p["wqr"], p["wkp"], p["wvp"]],
        [_from_heads_t(dqp), dkp, dvv], [BF16] * 4, name=nm("mla_prep_bwd"), tile=256)
    g["mla_q_norm"], g["mla_kv_norm"] = gqn[0], gkvn[0]
    g["mla_w_uq"] = _unpack_gw_uq(gwqp, gwqr)
    g["mla_w_ukv"] = _unpack_gw_ukv(gwkp, gwvp)
    dx_hm, db_hm, dc_hm, d_dt, gdb, gal, gds = ssd_bwd(
        s["x_hm"], s["b_hm"], s["c_hm"], proj, s["states"], p["ssm_dt_bias"], p["ssm_a_log"], p["ssm_d"],
        _to_heads(d_yssm, SSM_HEADS), name=nm("ssd_bwd"))
    g["ssm_dt_bias"], g["ssm_a_log"], g["ssm_d"] = gdb[0, :8], gal[0, :8], gds[0, :8]
    d_xbc_act = jnp.concatenate([_from_heads(dx_hm), _from_heads(db_hm), _from_heads(dc_hm)], axis=1)
    d_pre = ssm_conv_bwd_a(proj, p["ssm_conv_w"], p["ssm_conv_b"], d_xbc_act, name=nm("ssm_conv_bwd_a"))
    d_xbc, g["ssm_conv_w"], gscb = conv_bwd_b(d_pre, proj, OFF_XBC, p["ssm_conv_w"], name=nm("ssm_conv_bwd_b"),
                                              out_dtype=BF16, tc=256)
    g["ssm_conv_b"] = gscb[0]
    d_proj = jnp.concatenate([d_z, d_xbc, d_cq, d_ckv, d_dt.astype(BF16), d_kra, d_krb], axis=1)
    g["w_in"] = _unpack_gw_in(matmul(s["n1"], d_sb, name=nm("g_w_in_sb"), ta=True),
                              matmul(s["n1"], d_proj, name=nm("g_w_in"), ta=True))
    d_n1 = matmul(d_sb, p["w_sb"], name=nm("d_n1_sb"), tb=True)
    d_n1 = matmul(d_proj, p["w_rest"], name=nm("d_n1"), tb=True, residual=d_n1)
    (dh0,), (gm,) = rowwise_bwd(rms_fn, [s["h"]], [], [p["mix_norm"]], [d_n1], [F32], name=nm("mix_norm_bwd"),
                                add0=dh1)
    g["mix_norm"] = gm[0]
    return dh0, g, (rode[0] if rode else None)


def kernel(x, positions, mix_norm, w_in, sb_out_norm, ssm_conv_w, ssm_conv_b, ssm_dt_bias, ssm_a_log, ssm_d, ssm_out_norm, mla_q_norm, mla_w_uq, mla_kv_norm, mla_w_ukv, mla_out_norm, w_out, ffn_norm, ffn_w_up, ffn_conv_w, ffn_conv_b, ffn_w_down, final_norm, loss_target, m_mix_norm, m_w_in, m_sb_out_norm, m_ssm_conv_w, m_ssm_conv_b, m_ssm_dt_bias, m_ssm_a_log, m_ssm_d, m_ssm_out_norm, m_mla_q_norm, m_mla_w_uq, m_mla_kv_norm, m_mla_w_ukv, m_mla_out_norm, m_w_out, m_ffn_norm, m_ffn_w_up, m_ffn_conv_w, m_ffn_conv_b, m_ffn_w_down, m_final_norm, v_mix_norm, v_w_in, v_sb_out_norm, v_ssm_conv_w, v_ssm_conv_b, v_ssm_dt_bias, v_ssm_a_log, v_ssm_d, v_ssm_out_norm, v_mla_q_norm, v_mla_w_uq, v_mla_kv_norm, v_mla_w_ukv, v_mla_out_norm, v_w_out, v_ffn_norm, v_ffn_w_up, v_ffn_conv_w, v_ffn_conv_b, v_ffn_w_down, v_final_norm):
    W = dict(mix_norm=mix_norm, w_in=w_in, sb_out_norm=sb_out_norm, ssm_conv_w=ssm_conv_w, ssm_conv_b=ssm_conv_b,
             ssm_dt_bias=ssm_dt_bias, ssm_a_log=ssm_a_log, ssm_d=ssm_d, ssm_out_norm=ssm_out_norm,
             mla_q_norm=mla_q_norm, mla_w_uq=mla_w_uq, mla_kv_norm=mla_kv_norm, mla_w_ukv=mla_w_ukv,
             mla_out_norm=mla_out_norm, w_out=w_out, ffn_norm=ffn_norm, ffn_w_up=ffn_w_up, ffn_conv_w=ffn_conv_w,
             ffn_conv_b=ffn_conv_b, ffn_w_down=ffn_w_down, final_norm=final_norm)
    M = dict(mix_norm=m_mix_norm, w_in=m_w_in, sb_out_norm=m_sb_out_norm, ssm_conv_w=m_ssm_conv_w,
             ssm_conv_b=m_ssm_conv_b, ssm_dt_bias=m_ssm_dt_bias, ssm_a_log=m_ssm_a_log, ssm_d=m_ssm_d,
             ssm_out_norm=m_ssm_out_norm, mla_q_norm=m_mla_q_norm, mla_w_uq=m_mla_w_uq, mla_kv_norm=m_mla_kv_norm,
             mla_w_ukv=m_mla_w_ukv, mla_out_norm=m_mla_out_norm, w_out=m_w_out, ffn_norm=m_ffn_norm,
             ffn_w_up=m_ffn_w_up, ffn_conv_w=m_ffn_conv_w, ffn_conv_b=m_ffn_conv_b, ffn_w_down=m_ffn_w_down,
             final_norm=m_final_norm)
    V = dict(mix_norm=v_mix_norm, w_in=v_w_in, sb_out_norm=v_sb_out_norm, ssm_conv_w=v_ssm_conv_w,
             ssm_conv_b=v_ssm_conv_b, ssm_dt_bias=v_ssm_dt_bias, ssm_a_log=v_ssm_a_log, ssm_d=v_ssm_d,
             ssm_out_norm=v_ssm_out_norm, mla_q_norm=v_mla_q_norm, mla_w_uq=v_mla_w_uq, mla_kv_norm=v_mla_kv_norm,
             mla_w_ukv=v_mla_w_ukv, mla_out_norm=v_mla_out_norm, w_out=v_w_out, ffn_norm=v_ffn_norm,
             ffn_w_up=v_ffn_w_up, ffn_conv_w=v_ffn_conv_w, ffn_conv_b=v_ffn_conv_b, ffn_w_down=v_ffn_w_down,
             final_norm=v_final_norm)
    depth = mix_norm.shape[0]
    cx, cy, cc = _place()
    chip = 2 * cx + cy
    T = x.shape[1]

    assert depth == 2
    shard_shapes = {n: W[n].shape[1:] for n in BIG}

    def layer_of(d, li):
        return {n: d[n][li] for n in BIG}

    def assemble(g8):
        per_chip = [_unflat_halves([g8[2 * k + c].reshape(-1) for c in range(2)], shard_shapes) for k in range(4)]
        return {n: jnp.concatenate([per_chip[k][n] for k in range(4)], axis=BIG_AXIS[n] - 1) for n in BIG}

    full0 = assemble(allgather_blocks(_flat_half(layer_of(W, 0), cc, BF16), name="gather_weights_l0"))
    conv_full = {}
    small = {n: W[n] for n in SMALL_REPL}
    cw_flat = jnp.concatenate([W[n].reshape(-1) for n in SMALL_SHARD])
    cw_rows = -(-cw_flat.shape[0] // (8 * LANES)) * 8
    cw_all = allgather_direct(jnp.pad(cw_flat, (0, cw_rows * LANES - cw_flat.shape[0])).reshape(cw_rows, LANES),
                              name="gather_conv_taps")
    off = 0
    for n in SMALL_SHARD:
        sz = W[n].size
        conv_full[n] = jnp.concatenate(
            [cw_all[2 * k].reshape(-1)[off:off + sz].reshape(W[n].shape) for k in range(4)], axis=2)
        off += sz
    small.update(conv_full)

    cos, sin = _rope_tables(positions[0])
    params0 = _layer_weights(full0, small, 0)
    h, s0, g8 = _layer_fwd(x[0], params0, cos, sin, 0, exchange=("gather", _flat_half(layer_of(W, 1), cc, BF16)))
    params1 = _layer_weights(assemble(g8), small, 1)
    h, s1, _ = _layer_fwd(h, params1, cos, sin, 1)
    dh, g_final, loss_lanes = loss_head(h, loss_target[0], _row(final_norm), name="loss_head")

    def chip_parts(gl):
        return {n: jnp.split(gl[n], 4, axis=BIG_AXIS[n] - 1) for n in BIG}

    dh, g1, _ = _layer_bwd(dh, s1, params1, cos, sin, 1)
    parts1 = chip_parts(g1)
    by_dest = jnp.stack([_flat_half({n: parts1[n][k] for n in BIG}, c, BF16) for k in range(4) for c in range(2)])
    dh, g0, from_all = _layer_bwd(dh, s0, params0, cos, sin, 0, exchange=("all_to_all", by_dest))
    grad_x = dh[None]
    grads = [g0, g1]
    G = {n: jnp.stack([grads[li][n] for li in range(depth)]) for n in WEIGHTS if n != "final_norm" and n not in BIG}
    G["final_norm"] = g_final[0]
    half1 = add_blocks([from_all[d] for d in range(8)], F32, name="grads_l1_sum")

    parts0 = chip_parts(g0)

    def flat_for(k, which):
        return _flat_half({n: parts0[n][k] for n in BIG}, which, BF16)

    mine_first = jnp.stack([flat_for(k, cc) for k in range(4)])
    for_sibling = jnp.stack([flat_for(k, 1 - cc) for k in range(4)])
    R = mine_first.shape[1]
    from_sibling = send_to_sibling(for_sibling.reshape(4 * R, LANES), name="grads_to_sibling")
    pair = add_blocks([mine_first.reshape(4 * R, LANES), from_sibling], BF16, name="grads_pair_sum").reshape(4, R, LANES)
    others = chip_exchange(pair, name="grads_chip_exchange")
    own = lax.dynamic_index_in_dim(pair, chip, 0, keepdims=False)
    half0 = add_blocks([own, others[0], others[1], others[2]], F32, name="grads_chip_sum")
    half = jnp.concatenate([half0, half1])
    other = send_to_sibling(half, name="grads_pair_swap")
    by_core = [jnp.where(cc == 0, half, other), jnp.where(cc == 0, other, half)]
    g_big_l = [_unflat_halves([a[li * R:(li + 1) * R].reshape(-1) for a in by_core], shard_shapes) for li in range(depth)]
    g_big = {n: jnp.stack([g_big_l[li][n] for li in range(depth)]) for n in BIG}

    small_list = [G[n].reshape(-1) for n in SMALL_REPL] + [G[n].reshape(-1) for n in SMALL_SHARD]
    small_list.append(jnp.sum(loss_lanes).reshape(1))
    sm = jnp.concatenate(small_list)
    n_small = sm.shape[0]
    sm_rows = -(-n_small // (16 * LANES)) * 16
    sm_all = allgather_direct(jnp.pad(sm, (0, sm_rows * LANES - n_small)).reshape(sm_rows, LANES), name="gather_small")
    sm_sum = add_blocks([sm_all[d] for d in range(8)], F32, name="small_sum").reshape(-1)
    g_small, off = {}, 0
    for n in SMALL_REPL:
        g_small[n] = sm_sum[off:off + W[n].size].reshape(W[n].shape)
        off += W[n].size
    for n in SMALL_SHARD:
        full_shape = conv_full[n].shape
        sz = conv_full[n].size
        gfull = sm_sum[off:off + sz].reshape(full_shape)
        width = W[n].shape[2]
        g_small[n] = lax.dynamic_slice_in_dim(gfull, chip * width, width, axis=2)
        off += sz
    loss = sm_sum[off]

    grad_out, delta, new_m, new_v = {}, {}, {}, {}
    for n in BIG:
        shp = W[n].shape
        two_d = lambda a: a.reshape(shp[0] * shp[1], shp[2])
        d, nm_, nv_ = adamw(two_d(W[n]), two_d(g_big[n]), two_d(M[n]), two_d(V[n]), name="adamw_" + n)
        grad_out[n], delta[n], new_m[n], new_v[n] = g_big[n], d.reshape(shp), nm_.reshape(shp), nv_.reshape(shp)
    small_names = SMALL_REPL + SMALL_SHARD

    def flat_small(d):
        f = jnp.concatenate([d[n].reshape(-1) for n in small_names])
        rows = -(-f.shape[0] // (8 * LANES)) * 8
        return jnp.pad(f, (0, rows * LANES - f.shape[0])).reshape(rows, LANES)

    vpad = flat_small(V)
    d, nm_, nv_ = adamw(flat_small(W), flat_small(g_small), flat_small(M), vpad, name="adamw_small")
    off = 0
    for n in small_names:
        sz = W[n].size
        grad_out[n] = g_small[n]
        delta[n] = d.reshape(-1)[off:off + sz].reshape(W[n].shape)
        new_m[n] = nm_.reshape(-1)[off:off + sz].reshape(W[n].shape)
        new_v[n] = nv_.reshape(-1)[off:off + sz].reshape(W[n].shape)
        off += sz

    return (loss, grad_x, *[grad_out[n] for n in WEIGHTS], *[delta[n] for n in WEIGHTS],
            *[new_m[n] for n in WEIGHTS], *[new_v[n] for n in WEIGHTS])
```

```python
import functools
import math

import jax
import jax.numpy as jnp
from jax import lax
from jax.experimental import pallas as pl
from jax.experimental.pallas import tpu as pltpu

F32 = jnp.float32
BF16 = jnp.bfloat16

EPS = 1e-6
D_MODEL = 1024
SB_HEADS, SB_DIM = 4, 64
SSM_HEADS, SSM_DIM, SSM_GROUPS, SSM_STATE, SSM_CHUNK = 8, 64, 2, 64, 128
SSM_INNER = SSM_HEADS * SSM_DIM
SSM_CONV_DIM = SSM_INNER + 2 * SSM_GROUPS * SSM_STATE
MLA_HEADS, MLA_NOPE, MLA_ROPE, MLA_V = 4, 64, 32, 64
MLA_QK = MLA_NOPE + MLA_ROPE
MLA_SCALE = MLA_QK ** -0.5
ROPE_THETA = 10000.0
D_FF = 2816
IN_SPLITS = (768, 512, 768, 8, 256, 128, 32)

OFF_Z, OFF_XBC, OFF_CQ, OFF_CKV, OFF_DT, OFF_KRA, OFF_KRB = 0, 512, 1280, 1536, 1664, 1792, 1920
D_REST = 2048
LANES = 128

ADAM_LR, ADAM_B1, ADAM_B2, ADAM_EPS, ADAM_WD, ADAM_STEP = 0.001, 0.9, 0.999, 1e-08, 0.01, 10

V7X_VMEM_LIMIT = 48 * 1024 * 1024

NT = (((1,), (1,)), ((), ()))
TN = (((0,), (0,)), ((), ()))

BIG = ("w_in", "mla_w_uq", "mla_w_ukv", "w_out", "ffn_w_up", "ffn_w_down")
BIG_AXIS = {"w_in": 2, "mla_w_uq": 2, "mla_w_ukv": 2, "w_out": 1, "ffn_w_up": 2, "ffn_w_down": 1}
SMALL_REPL = ("mix_norm", "sb_out_norm", "ssm_conv_b", "ssm_dt_bias", "ssm_a_log", "ssm_d", "ssm_out_norm",
              "mla_q_norm", "mla_kv_norm", "mla_out_norm", "ffn_norm", "ffn_conv_b", "final_norm")
SMALL_SHARD = ("ssm_conv_w", "ffn_conv_w")
WEIGHTS = ("mix_norm", "w_in", "sb_out_norm", "ssm_conv_w", "ssm_conv_b", "ssm_dt_bias", "ssm_a_log", "ssm_d",
           "ssm_out_norm", "mla_q_norm", "mla_w_uq", "mla_kv_norm", "mla_w_ukv", "mla_out_norm", "w_out", "ffn_norm",
           "ffn_w_up", "ffn_conv_w", "ffn_conv_b", "ffn_w_down", "final_norm")


def _cparams(*sem):
    return pltpu.CompilerParams(dimension_semantics=sem if sem else None, vmem_limit_bytes=V7X_VMEM_LIMIT)


def _pick(n, target, mult=LANES):
    best = None
    for d in range(mult, min(n, target) + 1, mult):
        if n % d == 0:
            best = d
    return best or n


def _sigmoid(x):
    return 1.0 / (1.0 + jnp.exp(-x))


def _softplus(x):
    ax = jnp.where(x > 0, x, -x)
    return jnp.where(x > 0, x, 0.0) + jnp.log(1.0 + jnp.exp(-ax))


def _rms(x, g):
    return x * lax.rsqrt(jnp.mean(x * x, axis=-1, keepdims=True) + EPS) * g


def _raw_nn(a, b):
    return jnp.dot(a.astype(BF16), b.astype(BF16), preferred_element_type=F32)


def _raw_nt(a, b):
    return lax.dot_general(a.astype(BF16), b.astype(BF16), NT, preferred_element_type=F32)


def _raw_tn(a, b):
    return lax.dot_general(a.astype(BF16), b.astype(BF16), TN, preferred_element_type=F32)


@jax.custom_vjp
def mm_nn(a, b):
    return _raw_nn(a, b)


mm_nn.defvjp(lambda a, b: (_raw_nn(a, b), (a, b)),
             lambda r, ct: (_raw_nt(ct, r[1]), _raw_tn(r[0], ct)))


@jax.custom_vjp
def mm_nt(a, b):
    return _raw_nt(a, b)


mm_nt.defvjp(lambda a, b: (_raw_nt(a, b), (a, b)),
             lambda r, ct: (_raw_nn(ct, r[1]), _raw_tn(ct, r[0])))


@jax.custom_vjp
def mm_tn(a, b):
    return _raw_tn(a, b)


mm_tn.defvjp(lambda a, b: (_raw_tn(a, b), (a, b)),
             lambda r, ct: (_raw_nt(r[1], ct), _raw_nn(r[0], ct)))


def _split_dot(x, m, terms):
    acc = None
    r = x
    for t in range(terms):
        xt = r.astype(BF16)
        d = jnp.dot(xt, m, preferred_element_type=F32)
        acc = d if acc is None else acc + d
        if t + 1 < terms:
            r = r - xt.astype(F32)
    return acc


def _tri_dot(tri, x, terms=3):
    parts = []
    r = x
    for t in range(terms):
        xt = r.astype(BF16)
        parts.append(xt)
        if t + 1 < terms:
            r = r - xt.astype(F32)
    return jnp.dot(jnp.concatenate([tri] * terms, axis=1), jnp.concatenate(parts, axis=0),
                   preferred_element_type=F32)


def _tri(n, cmp):
    r = lax.broadcasted_iota(jnp.int32, (n, n), 0)
    c = lax.broadcasted_iota(jnp.int32, (n, n), 1)
    return cmp(r, c).astype(BF16)


@jax.custom_vjp
def csum_rows(x):
    return _tri_dot(_tri(x.shape[0], lambda r, c: r >= c), x)


csum_rows.defvjp(lambda x: (csum_rows(x), None),
                 lambda _, ct: (_tri_dot(_tri(ct.shape[0], lambda r, c: r <= c), ct),))


def matmul(a, b, *, name, out_dtype=F32, ta=False, tb=False, b_k0=0, residual=None):
    if ta:
        K, M = a.shape
    else:
        M, K = a.shape
    N = b.shape[0] if tb else b.shape[1]
    tm = _pick(M, 1408)
    tn = _pick(N, 1408)
    tk = _pick(K, 1408)
    nk = K // tk
    kb0 = b_k0 // tk
    assert b_k0 % tk == 0 and (tb or b_k0 == 0)
    has_res = residual is not None

    def body(*refs):
        if has_res:
            a_ref, b_ref, r_ref, o_ref, acc = refs
        else:
            a_ref, b_ref, o_ref, acc = refs
        k = pl.program_id(2)

        @pl.when(k == 0)
        def _():
            acc[...] = jnp.zeros_like(acc)

        av = a_ref[...].astype(BF16)
        bv = b_ref[...].astype(BF16)
        if ta:
            acc[...] += lax.dot_general(av, bv, TN, preferred_element_type=F32)
        elif tb:
            acc[...] += lax.dot_general(av, bv, NT, preferred_element_type=F32)
        else:
            acc[...] += jnp.dot(av, bv, preferred_element_type=F32)

        @pl.when(k == nk - 1)
        def _():
            r = acc[...]
            if has_res:
                r = r + r_ref[...].astype(F32)
            o_ref[...] = r.astype(o_ref.dtype)

    a_spec = pl.BlockSpec((tk, tm), lambda i, j, k: (k, i)) if ta else pl.BlockSpec((tm, tk), lambda i, j, k: (i, k))
    b_spec = pl.BlockSpec((tn, tk), lambda i, j, k: (j, kb0 + k)) if tb else pl.BlockSpec((tk, tn), lambda i, j, k: (k, j))
    in_specs = [a_spec, b_spec]
    args = [a, b]
    if has_res:
        in_specs.append(pl.BlockSpec((tm, tn), lambda i, j, k: (i, j)))
        args.append(residual)
    return pl.pallas_call(
        body, name=name, grid=(M // tm, N // tn, nk),
        in_specs=in_specs, out_specs=pl.BlockSpec((tm, tn), lambda i, j, k: (i, j)),
        out_shape=jax.ShapeDtypeStruct((M, N), out_dtype),
        scratch_shapes=[pltpu.VMEM((tm, tn), F32)],
        compiler_params=_cparams("parallel", "parallel", "arbitrary"),
    )(*args)


def _row_spec(entry, tl):
    if isinstance(entry, tuple):
        arr, width, cb = entry
        return arr, pl.BlockSpec((tl, width), lambda i, cb=cb: (i, cb))
    return entry, pl.BlockSpec((tl, entry.shape[1]), lambda i: (i, 0))


def _rows_T(entry):
    return (entry[0] if isinstance(entry, tuple) else entry).shape[0]


def rowwise(fn, rows, params, outs, *, name, tile=512, post=None):
    T = _rows_T(rows[0])
    tl = min(T, tile)
    nr, npar = len(rows), len(params)

    def body(*refs):
        r = [ref[...].astype(F32) for ref in refs[:nr]]
        p = [ref[...].astype(F32) for ref in refs[nr:nr + npar]]
        res = fn(*r, *p)
        if post is not None:
            res = post(*res)
        for o_ref, val in zip(refs[nr + npar:], res):
            o_ref[...] = val.astype(o_ref.dtype)

    arrs, specs = [], []
    for e in rows:
        a, s = _row_spec(e, tl)
        arrs.append(a)
        specs.append(s)
    for p in params:
        arrs.append(p)
        specs.append(pl.BlockSpec(p.shape, lambda i: (0, 0)))
    res = pl.pallas_call(
        body, name=name, grid=(T // tl,), in_specs=specs,
        out_specs=[pl.BlockSpec((tl, c), lambda i: (i, 0)) for c, _ in outs],
        out_shape=[jax.ShapeDtypeStruct((T, c), dt) for c, dt in outs],
        compiler_params=_cparams("parallel"),
    )(*arrs)
    return res


def rowwise_bwd(fn, rows, nd_rows, params, cts, grad_dtypes, *, name, tile=512, pre_ct=None, add0=None):
    T = _rows_T(rows[0])
    tl = min(T, tile)
    nr, nn, npar, nc = len(rows), len(nd_rows), len(params), len(cts)
    has_add = add0 is not None

    def body(*refs):
        pos = 0
        r = [ref[...].astype(F32) for ref in refs[pos:pos + nr]]
        pos += nr
        nd = [ref[...].astype(F32) for ref in refs[pos:pos + nn]]
        pos += nn
        p = [ref[...].astype(F32) for ref in refs[pos:pos + npar]]
        pos += npar
        c = [ref[...].astype(F32) for ref in refs[pos:pos + nc]]
        pos += nc
        if has_add:
            addv = refs[pos][...].astype(F32)
            pos += 1
        rg_refs = refs[pos:pos + nr]
        pg_refs = refs[pos + nr:pos + nr + npar]
        if pre_ct is not None:
            c = list(pre_ct(*c))
        _, vjp = jax.vjp(lambda *a: fn(*a[:nr], *nd, *a[nr:]), *r, *p)
        g = vjp(tuple(c))
        for j, ref in enumerate(rg_refs):
            val = g[j]
            if has_add and j == 0:
                val = val + addv
            ref[...] = val.astype(ref.dtype)
        if npar:
            @pl.when(pl.program_id(0) == 0)
            def _():
                for ref in pg_refs:
                    ref[...] = jnp.zeros_like(ref)
            for j, ref in enumerate(pg_refs):
                ref[...] += g[nr + j]

    arrs, specs = [], []
    widths = []
    for e in list(rows) + list(nd_rows):
        a, s = _row_spec(e, tl)
        arrs.append(a)
        specs.append(s)
        widths.append(s.block_shape[1])
    for p in params:
        arrs.append(p)
        specs.append(pl.BlockSpec(p.shape, lambda i: (0, 0)))
    for e in cts:
        a, s = _row_spec(e, tl)
        arrs.append(a)
        specs.append(s)
    if has_add:
        a, s = _row_spec(add0, tl)
        arrs.append(a)
        specs.append(s)
    out_specs = [pl.BlockSpec((tl, widths[j]), lambda i: (i, 0)) for j in range(nr)]
    out_shape = [jax.ShapeDtypeStruct((T, widths[j]), grad_dtypes[j]) for j in range(nr)]
    out_specs += [pl.BlockSpec(p.shape, lambda i: (0, 0)) for p in params]
    out_shape += [jax.ShapeDtypeStruct(p.shape, F32) for p in params]
    res = pl.pallas_call(
        body, name=name, grid=(T // tl,), in_specs=specs, out_specs=out_specs, out_shape=out_shape,
        compiler_params=_cparams("arbitrary"),
    )(*arrs)
    return list(res[:nr]), list(res[nr:])


def rms_fn(h, g):
    return (_rms(h, g),)


def merge_fn(ysb, yssm, z, ymla, g_sb, g_ssm, g_mla):
    ya = _rms(ysb, g_sb)
    yb = _rms(yssm * (z * _sigmoid(z)), g_ssm)
    yc = _rms(ymla, g_mla)
    return ya, yb, yc


def mla_prep_fn(cq, ckv, kra, krb, cos, sin, qn, kvn, wqp, wqr, wkp, wvp):
    cos4 = jnp.concatenate([cos] * MLA_HEADS, axis=1)
    sin4 = jnp.concatenate([sin] * MLA_HEADS, axis=1)
    nq = _rms(cq, qn)
    q = (mm_nn(nq, wqp) * cos4 + mm_nn(nq, wqr) * sin4) * MLA_SCALE
    nkv = _rms(ckv, kvn)
    kpe = kra * cos + krb * sin
    k = mm_nn(nkv, wkp) + jnp.concatenate([kpe] * MLA_HEADS, axis=1)
    v = mm_nn(nkv, wvp)
    return q, k, v


HALO = 8


def _prev_halo_spec(tl, tc, col_of):
    return pl.BlockSpec((HALO, tc), lambda i, j: (jnp.maximum(i * (tl // HALO) - 1, 0), col_of(j)))


def _fill_prev(buf, x_ref, halo_ref, i):
    buf[0:HALO, :] = jnp.where(i > 0, halo_ref[...].astype(F32), 0.0)
    buf[HALO:, :] = x_ref[...].astype(F32)


def _conv_from(buf, w_ref, b_ref, K, tl):
    acc = b_ref[...].astype(F32) + jnp.zeros((tl, buf.shape[1]), F32)
    for k in range(K):
        acc = acc + buf[pl.ds(HALO - (K - 1 - k), tl), :] * w_ref[k:k + 1, :].astype(F32)
    return acc


def ssm_conv_act(proj, w, b, *, name, tile=512, tc=256):
    T = proj.shape[0]
    K, C = w.shape
    tl = min(T, tile)
    c0 = OFF_XBC // tc

    def body(x_ref, halo_ref, w_ref, b_ref, o_ref, buf):
        _fill_prev(buf, x_ref, halo_ref, pl.program_id(0))
        u = _conv_from(buf, w_ref, b_ref, K, tl)
        o_ref[...] = u * _sigmoid(u)

    return pl.pallas_call(
        body, name=name, grid=(T // tl, C // tc),
        in_specs=[pl.BlockSpec((tl, tc), lambda i, j: (i, c0 + j)), _prev_halo_spec(tl, tc, lambda j: c0 + j),
                  pl.BlockSpec((K, tc), lambda i, j: (0, j)), pl.BlockSpec((1, tc), lambda i, j: (0, j))],
        out_specs=pl.BlockSpec((tl, tc), lambda i, j: (i, j)),
        out_shape=jax.ShapeDtypeStruct((T, C), F32),
        scratch_shapes=[pltpu.VMEM((tl + HALO, tc), F32)],
        compiler_params=_cparams("parallel", "parallel"),
    )(proj, proj, w, b)


def ssm_conv_bwd_a(proj, w, b, d_out, *, name, tile=512, tc=256):
    T = proj.shape[0]
    K, C = w.shape
    tl = min(T, tile)
    c0 = OFF_XBC // tc

    def body(x_ref, halo_ref, w_ref, b_ref, d_ref, o_ref, buf):
        _fill_prev(buf, x_ref, halo_ref, pl.program_id(0))
        u = _conv_from(buf, w_ref, b_ref, K, tl)
        s = _sigmoid(u)
        o_ref[...] = d_ref[...].astype(F32) * (s * (1.0 + u * (1.0 - s)))

    return pl.pallas_call(
        body, name=name, grid=(T // tl, C // tc),
        in_specs=[pl.BlockSpec((tl, tc), lambda i, j: (i, c0 + j)), _prev_halo_spec(tl, tc, lambda j: c0 + j),
                  pl.BlockSpec((K, tc), lambda i, j: (0, j)), pl.BlockSpec((1, tc), lambda i, j: (0, j)),
                  pl.BlockSpec((tl, tc), lambda i, j: (i, j))],
        out_specs=pl.BlockSpec((tl, tc), lambda i, j: (i, j)),
        out_shape=jax.ShapeDtypeStruct((T, C), F32),
        scratch_shapes=[pltpu.VMEM((tl + HALO, tc), F32)],
        compiler_params=_cparams("parallel", "parallel"),
    )(proj, proj, w, b, d_out)


def ffn_act(up, w, b, *, name, tile=512, tc=1408):
    T = up.shape[0]
    K = w.shape[0]
    tl = min(T, tile)
    nj = D_FF // tc

    def body(xg_ref, hg_ref, xv_ref, hv_ref, wg_ref, wv_ref, bg_ref, bv_ref, o_ref, bufg, bufv):
        i = pl.program_id(0)
        _fill_prev(bufg, xg_ref, hg_ref, i)
        _fill_prev(bufv, xv_ref, hv_ref, i)
        gate = _conv_from(bufg, wg_ref, bg_ref, K, tl)
        val = _conv_from(bufv, wv_ref, bv_ref, K, tl)
        o_ref[...] = (gate * _sigmoid(gate) * val).astype(o_ref.dtype)

    return pl.pallas_call(
        body, name=name, grid=(T // tl, nj),
        in_specs=[pl.BlockSpec((tl, tc), lambda i, j: (i, j)), _prev_halo_spec(tl, tc, lambda j: j),
                  pl.BlockSpec((tl, tc), lambda i, j: (i, nj + j)), _prev_halo_spec(tl, tc, lambda j: nj + j),
                  pl.BlockSpec((K, tc), lambda i, j: (0, j)), pl.BlockSpec((K, tc), lambda i, j: (0, nj + j)),
                  pl.BlockSpec((1, tc), lambda i, j: (0, j)), pl.BlockSpec((1, tc), lambda i, j: (0, nj + j))],
        out_specs=pl.BlockSpec((tl, tc), lambda i, j: (i, j)),
        out_shape=jax.ShapeDtypeStruct((T, D_FF), BF16),
        scratch_shapes=[pltpu.VMEM((tl + HALO, tc), F32), pltpu.VMEM((tl + HALO, tc), F32)],
        compiler_params=_cparams("parallel", "parallel"),
    )(up, up, up, up, w, w, b, b)


def ffn_bwd_fused(up, w, b, d_act, *, name, tile=512, tc=256):
    T = up.shape[0]
    K = w.shape[0]
    tl = min(T, tile)
    nj = D_FF // tc
    nblk = T // HALO
    ext = tl + HALO

    def body(xg, hgp, hgn, xv, hvp, hvn, wg, wv, bg, bv, d, dn, og, ov, dwg, dwv, dbg, dbv, bufg, bufv, dgb, dvb):
        i = pl.program_id(1)
        last = pl.num_programs(1) - 1

        def fill(buf, x_ref, prev_ref, next_ref):
            buf[0:HALO, :] = jnp.where(i > 0, prev_ref[...].astype(F32), 0.0)
            buf[HALO:HALO + tl, :] = x_ref[...].astype(F32)
            buf[HALO + tl:, :] = jnp.where(i < last, next_ref[...].astype(F32), 0.0)

        def conv_ext(buf, w_ref, b_ref):
            acc = b_ref[...].astype(F32) + jnp.zeros((ext, tc), F32)
            for k in range(K):
                acc = acc + buf[pl.ds(HALO - (K - 1 - k), ext), :] * w_ref[k:k + 1, :].astype(F32)
            return acc

        fill(bufg, xg, hgp, hgn)
        fill(bufv, xv, hvp, hvn)
        gate = conv_ext(bufg, wg, bg)
        val = conv_ext(bufv, wv, bv)
        dd = jnp.concatenate([d[...].astype(F32), jnp.where(i < last, dn[...].astype(F32)[0:HALO], 0.0)], axis=0)
        s = _sigmoid(gate)
        dgb[...] = dd * val * (s * (1.0 + gate * (1.0 - s)))
        dvb[...] = dd * (gate * s)

        @pl.when(i == 0)
        def _():
            for ref in (dwg, dwv, dbg, dbv):
                ref[...] = jnp.zeros_like(ref)

        for dbuf, xbuf, w_ref, o_ref, dw_ref, db_ref in ((dgb, bufg, wg, og, dwg, dbg), (dvb, bufv, wv, ov, dwv, dbv)):
            cur = dbuf[0:tl, :]
            dx = jnp.zeros((tl, tc), F32)
            for k in range(K):
                sft = K - 1 - k
                dx = dx + dbuf[pl.ds(sft, tl), :] * w_ref[k:k + 1, :].astype(F32)
                dw_ref[k:k + 1, :] += jnp.sum(cur * xbuf[pl.ds(HALO - sft, tl), :], axis=0, keepdims=True)
            db_ref[...] += jnp.sum(cur, axis=0, keepdims=True)
            o_ref[...] = dx.astype(o_ref.dtype)

    prev = lambda i: jnp.maximum(i * (tl // HALO) - 1, 0)
    nxt = lambda i: jnp.minimum((i + 1) * (tl // HALO), nblk - 1)

    def x_specs(col):
        return [pl.BlockSpec((tl, tc), lambda j, i: (i, col(j))), pl.BlockSpec((HALO, tc), lambda j, i: (prev(i), col(j))),
                pl.BlockSpec((HALO, tc), lambda j, i: (nxt(i), col(j)))]

    gcol, vcol = (lambda j: j), (lambda j: nj + j)
    in_specs = (x_specs(gcol) + x_specs(vcol)
                + [pl.BlockSpec((K, tc), lambda j, i: (0, j)), pl.BlockSpec((K, tc), lambda j, i: (0, nj + j)),
                   pl.BlockSpec((1, tc), lambda j, i: (0, j)), pl.BlockSpec((1, tc), lambda j, i: (0, nj + j)),
                   pl.BlockSpec((tl, tc), lambda j, i: (i, j)),
                   pl.BlockSpec((2 * HALO, tc), lambda j, i: (jnp.minimum((i + 1) * (tl // (2 * HALO)), nblk // 2 - 1), j))])
    row_out = pl.BlockSpec((tl, tc), lambda j, i: (i, j))
    w_out = pl.BlockSpec((K, tc), lambda j, i: (0, j))
    b_out = pl.BlockSpec((1, tc), lambda j, i: (0, j))
    return pl.pallas_call(
        body, name=name, grid=(nj, T // tl), in_specs=in_specs,
        out_specs=[row_out, row_out, w_out, w_out, b_out, b_out],
        out_shape=[jax.ShapeDtypeStruct((T, D_FF), BF16)] * 2 + [jax.ShapeDtypeStruct((K, D_FF), F32)] * 2
        + [jax.ShapeDtypeStruct((1, D_FF), F32)] * 2,
        scratch_shapes=[pltpu.VMEM((tl + 2 * HALO, tc), F32)] * 2 + [pltpu.VMEM((ext, tc), F32)] * 2,
        compiler_params=_cparams("parallel", "arbitrary"),
    )(up, up, up, up, up, up, w, w, b, b, d_act, d_act)


def conv_bwd_b(du, x, x_off, w, *, name, out_dtype, tile=512, tc=256):
    T, C = du.shape
    K = w.shape[0]
    tl = min(T, tile)
    c0 = x_off // tc
    nblk = T // HALO

    def body(du_ref, nx_ref, x_ref, w_ref, dx_ref, dw_ref, db_ref, dbuf):
        i = pl.program_id(1)
        last = pl.num_programs(1) - 1
        d = du_ref[...].astype(F32)
        dbuf[0:tl, :] = d
        dbuf[tl:, :] = jnp.where(i < last, nx_ref[...].astype(F32), 0.0)

        @pl.when(i == 0)
        def _():
            dw_ref[...] = jnp.zeros_like(dw_ref)
            db_ref[...] = jnp.zeros_like(db_ref)

        xin = x_ref[...].astype(F32)
        dx = jnp.zeros((tl, tc), F32)
        for k in range(K):
            s = K - 1 - k
            shifted = dbuf[pl.ds(s, tl), :]
            dx = dx + shifted * w_ref[k:k + 1, :].astype(F32)
            dw_ref[k:k + 1, :] += jnp.sum(shifted * xin, axis=0, keepdims=True)
        db_ref[...] += jnp.sum(d, axis=0, keepdims=True)
        dx_ref[...] = dx.astype(dx_ref.dtype)

    return pl.pallas_call(
        body, name=name, grid=(C // tc, T // tl),
        in_specs=[pl.BlockSpec((tl, tc), lambda j, i: (i, j)),
                  pl.BlockSpec((HALO, tc), lambda j, i: (jnp.minimum((i + 1) * (tl // HALO), nblk - 1), j)),
                  pl.BlockSpec((tl, tc), lambda j, i: (i, c0 + j)),
                  pl.BlockSpec((K, tc), lambda j, i: (0, j))],
        out_specs=[pl.BlockSpec((tl, tc), lambda j, i: (i, j)), pl.BlockSpec((K, tc), lambda j, i: (0, j)),
                   pl.BlockSpec((1, tc), lambda j, i: (0, j))],
        out_shape=[jax.ShapeDtypeStruct((T, C), out_dtype), jax.ShapeDtypeStruct((K, C), F32),
                   jax.ShapeDtypeStruct((1, C), F32)],
        scratch_shapes=[pltpu.VMEM((tl + HALO, tc), F32)],
        compiler_params=_cparams("parallel", "arbitrary"),
    )(du, du, x, w)


def _attn_tiles(T, keys=256):
    return min(T, 1024), min(T, keys)


def _after_diag(keys, queries, strict):
    d = lax.broadcasted_iota(jnp.int32, (keys, queries), 1) - lax.broadcasted_iota(jnp.int32, (keys, queries), 0)
    return d > 0 if strict else d >= 0


def _log_gates(z):
    l1p = jnp.log(1.0 + jnp.exp(-jnp.abs(z)))
    a = jnp.minimum(z, 0.0) - l1p
    return a, a - z


def _causal_sweep(i, tq, tk, block, descending, keep_going=None, first_block=None):
    nb = tq // tk
    n_full = i * nb

    def band():
        order = reversed(range(nb)) if descending else range(nb)
        for bb in order:
            block(pl.multiple_of(i * tq + bb * tk, tk), bb * tk, True)

    def full():
        if descending and keep_going is not None:
            def step(j):
                block(pl.multiple_of((n_full - 1 - j) * tk, tk), 0, False)
                return j + 1
            done = lax.while_loop(lambda j: jnp.logical_and(j < n_full, keep_going()), step, jnp.int32(0))
            return n_full - done

        def step(j, c):
            kb = (n_full - 1 - j) if descending else j
            block(pl.multiple_of(kb * tk, tk), 0, False)
            return c
        lax.fori_loop(0 if first_block is None else first_block, n_full, step, 0)
        return None

    if descending:
        band()
        return full()
    full()
    band()
    return None


def sb_fwd(q, k, v, *, name):
    H, dh, T = q.shape
    tq, tk = _attn_tiles(T)

    def body(q_ref, k_ref, v_ref, y_ref, bt_ref, first_ref, acc, run):
        acc[...] = jnp.zeros_like(acc)
        run[...] = jnp.zeros_like(run)
        u_after = _tri(tk, lambda r, c: r < c)

        def block(k0, r0, masked):
            kb = k_ref[pl.ds(k0, tk), :]
            vb = v_ref[pl.ds(k0, tk), :]
            z = jnp.dot(kb, q_ref[:, r0:], preferred_element_type=F32)
            a, b = _log_gates(z)
            if masked:
                valid = _after_diag(tk, tq - r0, True)
                b = jnp.where(valid, b, 0.0)
            w = jnp.exp(a + _tri_dot(u_after, b, 2) + run[:, r0:])
            if masked:
                w = jnp.where(valid, w, 0.0)
            acc[:, r0:] += lax.dot_general(vb, w.astype(BF16), TN, preferred_element_type=F32)
            run[:, r0:] += jnp.sum(b, axis=0, keepdims=True)

        first = _causal_sweep(pl.program_id(1), tq, tk, block, descending=True,
                              keep_going=lambda: jnp.max(run[...]) >= SB_ZERO_BELOW)
        y_ref[...] = acc[...]
        bt_ref[...] = run[...]
        first_ref[...] = jnp.zeros(first_ref.shape, F32) + first.astype(F32)

    return pl.pallas_call(
        body, name=name, grid=(H, T // tq),
        in_specs=[pl.BlockSpec((None, dh, tq), lambda h, i: (h, 0, i)),
                  pl.BlockSpec((None, T, dh), lambda h, i: (h, 0, 0)),
                  pl.BlockSpec((None, T, dh), lambda h, i: (h, 0, 0))],
        out_specs=[pl.BlockSpec((None, dh, tq), lambda h, i: (h, 0, i)),
                   pl.BlockSpec((None, 1, tq), lambda h, i: (h, 0, i)),
                   pl.BlockSpec((None, None, HALO, LANES), lambda h, i: (h, i, 0, 0))],
        out_shape=[jax.ShapeDtypeStruct((H, dh, T), F32), jax.ShapeDtypeStruct((H, 1, T), F32),
                   jax.ShapeDtypeStruct((H, T // tq, HALO, LANES), F32)],
        scratch_shapes=[pltpu.VMEM((dh, tq), F32), pltpu.VMEM((1, tq), F32)],
        compiler_params=_cparams("parallel", "parallel"),
    )(q, k, v)


def sb_bwd(q, k, v, dy, btot, first, *, name, q_scale):
    H, dh, T = q.shape
    tq, tk = _attn_tiles(T)

    def body(q_ref, k_ref, v_ref, dy_ref, bt_ref, first_ref, dq_ref, dk_ref, dv_ref, dq, pb, pg, dyb):
        @pl.when(pl.program_id(1) == 0)
        def _():
            dk_ref[...] = jnp.zeros_like(dk_ref)
            dv_ref[...] = jnp.zeros_like(dv_ref)

        dq[...] = jnp.zeros_like(dq)
        pb[...] = jnp.zeros_like(pb)
        pg[...] = jnp.zeros_like(pg)
        dyb[...] = dy_ref[...].astype(BF16)
        u_upto = _tri(tk, lambda r, c: r >= c)
        u_before = _tri(tk, lambda r, c: r > c)

        def block(k0, r0, masked):
            kb = k_ref[pl.ds(k0, tk), :]
            vb = v_ref[pl.ds(k0, tk), :]
            qv = q_ref[:, r0:]
            dyv = dyb[:, r0:]
            z = jnp.dot(kb, qv, preferred_element_type=F32)
            a, b = _log_gates(z)
            if masked:
                valid = _after_diag(tk, tq - r0, True)
                b = jnp.where(valid, b, 0.0)
            w = jnp.exp(a + (bt_ref[:, r0:] - pb[:, r0:] - _tri_dot(u_upto, b, 2)))
            if masked:
                w = jnp.where(valid, w, 0.0)
            g = w * jnp.dot(vb, dyv, preferred_element_type=F32)
            dz = g - jnp.exp(a) * (g + pg[:, r0:] + _tri_dot(u_before, g, 2))
            if masked:
                dz = jnp.where(valid, dz, 0.0)
            dz = dz.astype(BF16)
            dq[:, r0:] += lax.dot_general(kb, dz, TN, preferred_element_type=F32)
            dk_ref[pl.ds(k0, tk), :] += lax.dot_general(dz, qv, NT, preferred_element_type=F32)
            dv_ref[pl.ds(k0, tk), :] += lax.dot_general(w.astype(BF16), dyv, NT, preferred_element_type=F32)
            pb[:, r0:] += jnp.sum(b, axis=0, keepdims=True)
            pg[:, r0:] += jnp.sum(g, axis=0, keepdims=True)

        i = pl.program_id(1)
        first = jnp.clip(jnp.max(first_ref[...]).astype(jnp.int32), 0, i * (tq // tk))
        _causal_sweep(i, tq, tk, block, descending=False, first_block=first)
        dq_ref[...] = dq[...] * q_scale

    return pl.pallas_call(
        body, name=name, grid=(H, T // tq),
        in_specs=[pl.BlockSpec((None, dh, tq), lambda h, i: (h, 0, i)),
                  pl.BlockSpec((None, T, dh), lambda h, i: (h, 0, 0)),
                  pl.BlockSpec((None, T, dh), lambda h, i: (h, 0, 0)),
                  pl.BlockSpec((None, dh, tq), lambda h, i: (h, 0, i)),
                  pl.BlockSpec((None, 1, tq), lambda h, i: (h, 0, i)),
                  pl.BlockSpec((None, None, HALO, LANES), lambda h, i: (h, i, 0, 0))],
        out_specs=[pl.BlockSpec((None, dh, tq), lambda h, i: (h, 0, i)),
                   pl.BlockSpec((None, T, dh), lambda h, i: (h, 0, 0)),
                   pl.BlockSpec((None, T, dh), lambda h, i: (h, 0, 0))],
        out_shape=[jax.ShapeDtypeStruct((H, dh, T), F32), jax.ShapeDtypeStruct((H, T, dh), F32),
                   jax.ShapeDtypeStruct((H, T, dh), F32)],
        scratch_shapes=[pltpu.VMEM((dh, tq), F32), pltpu.VMEM((1, tq), F32), pltpu.VMEM((1, tq), F32),
                        pltpu.VMEM((dh, tq), BF16)],
        compiler_params=_cparams("parallel", "arbitrary"),
    )(q, k, v, dy, btot, first)


NEG = -1e30
SB_ZERO_BELOW = -105.0
MLA_KEYS = 512


def _call_with_exchange(body, exchange, *, name, grid, in_specs, out_specs, out_shape, scratch_shapes, args):
    if exchange is None:
        return pl.pallas_call(body, name=name, grid=grid, in_specs=in_specs, out_specs=out_specs, out_shape=out_shape,
                              scratch_shapes=scratch_shapes, compiler_params=_cparams("parallel", "arbitrary"))(*args)
    kind, src = exchange
    n_in, n_out, n_scr = len(in_specs), len(out_specs), len(scratch_shapes)
    R = src.shape[-2]

    def wrapped(*refs):
        ins, src_ref = refs[:n_in], refs[n_in]
        outs, xout = refs[n_in + 1:n_in + 1 + n_out], refs[n_in + 1 + n_out]
        scr = refs[n_in + 2 + n_out:n_in + 2 + n_out + n_scr]
        start, finish = _direct_exchange(kind, src_ref, xout, *refs[-3:])
        step = pl.program_id(0) * pl.num_programs(1) + pl.program_id(1)
        pl.when(step == 0)(start)
        body(*ins, *outs, *scr)
        pl.when(step == pl.num_programs(0) * pl.num_programs(1) - 1)(finish)

    return pl.pallas_call(
        wrapped, name=name, grid=grid, in_specs=list(in_specs) + [HBM_SPEC], out_specs=list(out_specs) + [HBM_SPEC],
        out_shape=list(out_shape) + [jax.ShapeDtypeStruct((8, R, LANES), src.dtype)],
        scratch_shapes=list(scratch_shapes) + [pltpu.SemaphoreType.DMA((7,)), pltpu.SemaphoreType.DMA((7,)),
                                               pltpu.SemaphoreType.DMA],
        compiler_params=_cparams("arbitrary", "arbitrary"))(*args, src)


def mla_fwd(q, k, v, *, name, exchange=None):
    H, dk, T = q.shape
    dv = v.shape[1] // H
    tq, tk = _attn_tiles(T, MLA_KEYS)

    def body(q_ref, k_ref, v_ref, o_ref, l_ref, acc, m_s, l_s):
        acc[...] = jnp.zeros_like(acc)
        m_s[...] = jnp.full_like(m_s, NEG)
        l_s[...] = jnp.zeros_like(l_s)

        def block(k0, r0, masked):
            kb = k_ref[pl.ds(k0, tk), :]
            vb = v_ref[pl.ds(k0, tk), :]
            s = jnp.dot(kb, q_ref[:, r0:], preferred_element_type=F32)
            if masked:
                s = jnp.where(_after_diag(tk, tq - r0, False), s, NEG)
            m = m_s[:, r0:]
            m_new = jnp.maximum(m, jnp.max(s, axis=0, keepdims=True))
            p = jnp.exp(s - m_new)
            alpha = jnp.exp(m - m_new)
            l_s[:, r0:] = alpha * l_s[:, r0:] + jnp.sum(p, axis=0, keepdims=True)
            acc[:, r0:] = alpha * acc[:, r0:] + lax.dot_general(vb, p.astype(BF16), TN, preferred_element_type=F32)
            m_s[:, r0:] = m_new

        _causal_sweep(pl.program_id(1), tq, tk, block, descending=False)
        o_ref[...] = acc[...] / l_s[...]
        l_ref[...] = m_s[...] + jnp.log(l_s[...])

    return _call_with_exchange(
        body, exchange, name=name, grid=(H, T // tq),
        in_specs=[pl.BlockSpec((None, dk, tq), lambda h, i: (h, 0, i)),
                  pl.BlockSpec((T, dk), lambda h, i: (0, h)),
                  pl.BlockSpec((T, dv), lambda h, i: (0, h))],
        out_specs=[pl.BlockSpec((None, dv, tq), lambda h, i: (h, 0, i)),
                   pl.BlockSpec((None, 1, tq), lambda h, i: (h, 0, i))],
        out_shape=[jax.ShapeDtypeStruct((H, dv, T), F32), jax.ShapeDtypeStruct((H, 1, T), F32)],
        scratch_shapes=[pltpu.VMEM((dv, tq), F32), pltpu.VMEM((1, tq), F32), pltpu.VMEM((1, tq), F32)],
        args=(q, k, v))


def mla_bwd(q, k, v, do, o, lse, *, name, exchange=None):
    H, dk, T = q.shape
    dv = v.shape[1] // H
    tq, tk = _attn_tiles(T, MLA_KEYS)

    def body(q_ref, k_ref, v_ref, do_ref, o_ref, l_ref, dq_ref, dk_ref, dv_ref, dq, delta, dob):
        @pl.when(pl.program_id(1) == 0)
        def _():
            dk_ref[...] = jnp.zeros_like(dk_ref)
            dv_ref[...] = jnp.zeros_like(dv_ref)

        dq[...] = jnp.zeros_like(dq)
        dov = do_ref[...].astype(F32)
        dob[...] = dov.astype(BF16)
        delta[...] = jnp.sum(dov * o_ref[...], axis=0, keepdims=True)

        def block(k0, r0, masked):
            kb = k_ref[pl.ds(k0, tk), :]
            vb = v_ref[pl.ds(k0, tk), :]
            qv = q_ref[:, r0:]
            dov_b = dob[:, r0:]
            s = jnp.dot(kb, qv, preferred_element_type=F32)
            p = jnp.exp(s - l_ref[:, r0:])
            if masked:
                p = jnp.where(_after_diag(tk, tq - r0, False), p, 0.0)
            dp = jnp.dot(vb, dov_b, preferred_element_type=F32)
            ds = (p * (dp - delta[:, r0:])).astype(BF16)
            dq[:, r0:] += lax.dot_general(kb, ds, TN, preferred_element_type=F32)
            dk_ref[pl.ds(k0, tk), :] += lax.dot_general(ds, qv, NT, preferred_element_type=F32)
            dv_ref[pl.ds(k0, tk), :] += lax.dot_general(p.astype(BF16), dov_b, NT, preferred_element_type=F32)

        _causal_sweep(pl.program_id(1), tq, tk, block, descending=False)
        dq_ref[...] = dq[...]

    return _call_with_exchange(
        body, exchange, name=name, grid=(H, T // tq),
        in_specs=[pl.BlockSpec((None, dk, tq), lambda h, i: (h, 0, i)),
                  pl.BlockSpec((T, dk), lambda h, i: (0, h)),
                  pl.BlockSpec((T, dv), lambda h, i: (0, h)),
                  pl.BlockSpec((None, dv, tq), lambda h, i: (h, 0, i)),
                  pl.BlockSpec((None, dv, tq), lambda h, i: (h, 0, i)),
                  pl.BlockSpec((None, 1, tq), lambda h, i: (h, 0, i))],
        out_specs=[pl.BlockSpec((None, dk, tq), lambda h, i: (h, 0, i)),
                   pl.BlockSpec((T, dk), lambda h, i: (0, h)),
                   pl.BlockSpec((T, dv), lambda h, i: (0, h))],
        out_shape=[jax.ShapeDtypeStruct((H, dk, T), F32), jax.ShapeDtypeStruct((T, H * dk), F32),
                   jax.ShapeDtypeStruct((T, H * dv), F32)],
        scratch_shapes=[pltpu.VMEM((dk, tq), F32), pltpu.VMEM((1, tq), F32), pltpu.VMEM((dv, tq), BF16)],
        args=(q, k, v, do, o, lse))


def _lane_pick(x, h):
    lane = lax.broadcasted_iota(jnp.int32, (1, x.shape[1]), 1)
    return jnp.sum(jnp.where(lane == h, x, 0.0), axis=1, keepdims=True)


def _row_pick(x, h):
    sub = lax.broadcasted_iota(jnp.int32, (x.shape[0], 1), 0)
    return jnp.sum(jnp.where(sub == h, x, 0.0), axis=0, keepdims=True)


def ssd_chunk_fn(*args):
    nh, ng = SSM_HEADS, SSM_GROUPS
    xs = args[:nh]
    bs = args[nh:nh + ng]
    cs = args[nh + ng:nh + 2 * ng]
    dt_raw = args[nh + 2 * ng]
    st = args[nh + 2 * ng + 1:nh + 2 * ng + 1 + nh]
    dt_bias, a_log, d_skip = args[nh + 2 * ng + 1 + nh:]
    L = dt_raw.shape[0]
    dt = _softplus(dt_raw + dt_bias)
    da = dt * (-jnp.exp(a_log))
    dcs = csum_rows(da)
    dcs_t = dcs.T
    total = jnp.sum(da, axis=0, keepdims=True)
    causal = lax.broadcasted_iota(jnp.int32, (L, L), 0) >= lax.broadcasted_iota(jnp.int32, (L, L), 1)
    cb = [mm_nt(cs[g], bs[g]) for g in range(ng)]
    ys, new_st = [], []
    for h in range(nh):
        g = h // (nh // ng)
        dcs_h = _lane_pick(dcs, h)
        dt_h = _lane_pick(dt, h)
        tot_h = _lane_pick(total, h)
        dsk_h = _lane_pick(d_skip, h)
        decay = jnp.exp(jnp.where(causal, dcs_h - _row_pick(dcs_t, h), NEG))
        xdt = xs[h] * dt_h
        y = mm_nn(cb[g] * decay, xdt)
        y = y + mm_nn(cs[g] * jnp.exp(dcs_h), st[h])
        ys.append(y + xs[h] * dsk_h)
        new_st.append(st[h] * jnp.exp(tot_h) + mm_tn(bs[g] * jnp.exp(tot_h - dcs_h), xdt))
    return tuple(ys) + tuple(new_st)


def ssd_fwd(x_hm, b_hm, c_hm, proj, dt_bias, a_log, d_skip, *, name):
    nh, T, P = x_hm.shape
    ng, N = b_hm.shape[0], b_hm.shape[2]
    L = SSM_CHUNK
    nc = T // L
    dtb = OFF_DT // LANES

    def body(x_ref, b_ref, c_ref, dt_ref, db_ref, al_ref, ds_ref, y_ref, s_ref, state):
        @pl.when(pl.program_id(0) == 0)
        def _():
            state[...] = jnp.zeros_like(state)

        s_ref[...] = state[...]
        args = ([x_ref[h] for h in range(nh)] + [b_ref[g] for g in range(ng)] + [c_ref[g] for g in range(ng)]
                + [dt_ref[...]] + [state[h] for h in range(nh)] + [db_ref[...], al_ref[...], ds_ref[...]])
        res = ssd_chunk_fn(*args)
        for h in range(nh):
            y_ref[h] = res[h]
            state[h] = res[nh + h]

    par = pl.BlockSpec((1, LANES), lambda i: (0, 0))
    return pl.pallas_call(
        body, name=name, grid=(nc,),
        in_specs=[pl.BlockSpec((nh, L, P), lambda i: (0, i, 0)), pl.BlockSpec((ng, L, N), lambda i: (0, i, 0)),
                  pl.BlockSpec((ng, L, N), lambda i: (0, i, 0)), pl.BlockSpec((L, LANES), lambda i: (i, dtb)),
                  par, par, par],
        out_specs=[pl.BlockSpec((nh, L, P), lambda i: (0, i, 0)),
                   pl.BlockSpec((None, nh, N, P), lambda i: (i, 0, 0, 0))],
        out_shape=[jax.ShapeDtypeStruct((nh, T, P), F32), jax.ShapeDtypeStruct((nc, nh, N, P), F32)],
        scratch_shapes=[pltpu.VMEM((nh, N, P), F32)],
        compiler_params=_cparams("arbitrary"),
    )(x_hm, b_hm, c_hm, proj, dt_bias, a_log, d_skip)


def ssd_bwd(x_hm, b_hm, c_hm, proj, states, dt_bias, a_log, d_skip, dy_hm, *, name):
    nh, T, P = x_hm.shape
    ng, N = b_hm.shape[0], b_hm.shape[2]
    L = SSM_CHUNK
    nc = T // L
    dtb = OFF_DT // LANES

    def body(x_ref, b_ref, c_ref, dt_ref, s_ref, db_ref, al_ref, ds_ref, dy_ref,
             dx_ref, dbm_ref, dcm_ref, ddt_ref, gdb_ref, gal_ref, gds_ref, dstate):
        @pl.when(pl.program_id(0) == 0)
        def _():
            dstate[...] = jnp.zeros_like(dstate)
            gdb_ref[...] = jnp.zeros_like(gdb_ref)
            gal_ref[...] = jnp.zeros_like(gal_ref)
            gds_ref[...] = jnp.zeros_like(gds_ref)

        args = ([x_ref[h] for h in range(nh)] + [b_ref[g] for g in range(ng)] + [c_ref[g] for g in range(ng)]
                + [dt_ref[...]] + [s_ref[h] for h in range(nh)] + [db_ref[...], al_ref[...], ds_ref[...]])
        _, vjp = jax.vjp(ssd_chunk_fn, *args)
        g = vjp(tuple([dy_ref[h] for h in range(nh)] + [dstate[h] for h in range(nh)]))
        for h in range(nh):
            dx_ref[h] = g[h]
        for gi in range(ng):
            dbm_ref[gi] = g[nh + gi]
            dcm_ref[gi] = g[nh + ng + gi]
        ddt_ref[...] = g[nh + 2 * ng]
        for h in range(nh):
            dstate[h] = g[nh + 2 * ng + 1 + h]
        gdb_ref[...] += g[-3]
        gal_ref[...] += g[-2]
        gds_ref[...] += g[-1]

    rev = lambda i: nc - 1 - i
    par = pl.BlockSpec((1, LANES), lambda i: (0, 0))
    return pl.pallas_call(
        body, name=name, grid=(nc,),
        in_specs=[pl.BlockSpec((nh, L, P), lambda i: (0, rev(i), 0)), pl.BlockSpec((ng, L, N), lambda i: (0, rev(i), 0)),
                  pl.BlockSpec((ng, L, N), lambda i: (0, rev(i), 0)), pl.BlockSpec((L, LANES), lambda i: (rev(i), dtb)),
                  pl.BlockSpec((None, nh, N, P), lambda i: (rev(i), 0, 0, 0)), par, par, par,
                  pl.BlockSpec((nh, L, P), lambda i: (0, rev(i), 0))],
        out_specs=[pl.BlockSpec((nh, L, P), lambda i: (0, rev(i), 0)), pl.BlockSpec((ng, L, N), lambda i: (0, rev(i), 0)),
                   pl.BlockSpec((ng, L, N), lambda i: (0, rev(i), 0)), pl.BlockSpec((L, LANES), lambda i: (rev(i), 0)),
                   par, par, par],
        out_shape=[jax.ShapeDtypeStruct((nh, T, P), F32), jax.ShapeDtypeStruct((ng, T, N), F32),
                   jax.ShapeDtypeStruct((ng, T, N), F32), jax.ShapeDtypeStruct((T, LANES), F32),
                   jax.ShapeDtypeStruct((1, LANES), F32), jax.ShapeDtypeStruct((1, LANES), F32),
                   jax.ShapeDtypeStruct((1, LANES), F32)],
        scratch_shapes=[pltpu.VMEM((nh, N, P), F32)],
        compiler_params=_cparams("arbitrary"),
    )(x_hm, b_hm, c_hm, proj, states, dt_bias, a_log, d_skip, dy_hm)


def loss_head(h, target, g, *, name, tile=512):
    T, C = h.shape
    tl = min(T, tile)

    def body(h_ref, t_ref, g_ref, dh_ref, dg_ref, ls_ref):
        @pl.when(pl.program_id(0) == 0)
        def _():
            dg_ref[...] = jnp.zeros_like(dg_ref)
            ls_ref[...] = jnp.zeros_like(ls_ref)

        (y,), vjp = jax.vjp(rms_fn, h_ref[...], g_ref[...])
        err = y - t_ref[...]
        ls_ref[...] += jnp.sum(err * err, axis=0, keepdims=True) * (0.5 / C)
        dh, dg = vjp((err * (1.0 / C),))
        dh_ref[...] = dh
        dg_ref[...] += dg

    row = pl.BlockSpec((tl, C), lambda i: (i, 0))
    par = pl.BlockSpec((1, C), lambda i: (0, 0))
    return pl.pallas_call(
        body, name=name, grid=(T // tl,), in_specs=[row, row, par], out_specs=[row, par, par],
        out_shape=[jax.ShapeDtypeStruct((T, C), F32), jax.ShapeDtypeStruct((1, C), F32),
                   jax.ShapeDtypeStruct((1, C), F32)],
        compiler_params=_cparams("arbitrary"),
    )(h, target, g)


def adamw(w, g, m, v, *, name):
    R, C = w.shape
    tr = R
    for d in range(8, min(R, 512) + 1, 8):
        if R % d == 0:
            tr = d
    c1 = 1.0 - ADAM_B1 ** ADAM_STEP
    c2 = 1.0 - ADAM_B2 ** ADAM_STEP

    def body(w_ref, g_ref, m_ref, v_ref, d_ref, nm_ref, nv_ref):
        gv = g_ref[...]
        nm = ADAM_B1 * m_ref[...] + (1.0 - ADAM_B1) * gv
        nv = ADAM_B2 * v_ref[...] + (1.0 - ADAM_B2) * (gv * gv)
        d_ref[...] = -ADAM_LR * ((nm / c1) / (jnp.sqrt(nv / c2) + ADAM_EPS) + ADAM_WD * w_ref[...])
        nm_ref[...] = nm
        nv_ref[...] = nv

    spec = pl.BlockSpec((tr, C), lambda i: (i, 0))
    return pl.pallas_call(
        body, name=name, grid=(R // tr,), in_specs=[spec] * 4, out_specs=[spec] * 3,
        out_shape=[jax.ShapeDtypeStruct((R, C), F32)] * 3,
        compiler_params=_cparams("parallel"),
    )(w, g, m, v)


MESH = pl.DeviceIdType.MESH
HBM_SPEC = pl.BlockSpec(memory_space=pltpu.HBM)


def _place():
    return lax.axis_index("x"), lax.axis_index("y"), lax.axis_index("c")


def _direct_exchange(kind, src_ref, out_ref, send_sems, recv_sems, local_sem):
    x, y, c = _place()
    me = 4 * x + 2 * y + c

    def block_for(dest):
        return src_ref if kind == "gather" else src_ref.at[dest]

    own = pltpu.make_async_copy(block_for(me), out_ref.at[me], local_sem)
    sends, arrivals = [], []
    for f in range(1, 8):
        px = jnp.where((f >> 2) & 1, 1 - x, x)
        py = jnp.where((f >> 1) & 1, 1 - y, y)
        pc = jnp.where(f & 1, 1 - c, c)
        peer = 4 * px + 2 * py + pc
        for dst, group in ((out_ref.at[me], sends), (out_ref.at[peer], arrivals)):
            group.append(pltpu.make_async_remote_copy(
                src_ref=block_for(peer), dst_ref=dst, send_sem=send_sems.at[f - 1], recv_sem=recv_sems.at[f - 1],
                device_id=(px, py, pc), device_id_type=MESH))

    def start():
        own.start()
        for cp in sends:
            cp.start()

    def finish():
        for cp in arrivals:
            cp.wait_recv()
        for cp in sends:
            cp.wait_send()
        own.wait()

    return start, finish


def allgather_blocks(mine, *, name):
    R = mine.shape[0]

    def body(x_ref, out_ref, send_sems, recv_sems, local_sem):
        x, y, c = _place()
        me, sibling = (x, y, c), (x, y, 1 - c)
        chips = [(1 - x, y), (x, 1 - y), (1 - x, 1 - y)]

        def slot(px, py, pc):
            return out_ref.at[4 * px + 2 * py + pc]

        def copy(k, block, to, src=None):
            return pltpu.make_async_remote_copy(
                src_ref=slot(*block) if src is None else src, dst_ref=slot(*block),
                send_sem=send_sems.at[k], recv_sem=recv_sems.at[k], device_id=to, device_id_type=MESH)

        own = pltpu.make_async_copy(x_ref, slot(*me), local_sem)
        own.start()
        first = [copy(0, me, sibling, src=x_ref)]
        first += [copy(1 + j, me, (*chip, c), src=x_ref) for j, chip in enumerate(chips)]
        for cp in first:
            cp.start()
        passed = [copy(4 + j, (*chip, c), sibling) for j, chip in enumerate(chips)]
        for j, chip in enumerate(chips):
            copy(1 + j, (*chip, c), me).wait_recv()
            passed[j].start()
        copy(0, sibling, me).wait_recv()
        for j, chip in enumerate(chips):
            copy(4 + j, (*chip, 1 - c), me).wait_recv()
        for cp in first + passed:
            cp.wait_send()
        own.wait()

    return pl.pallas_call(
        body, name=name, out_shape=jax.ShapeDtypeStruct((8, R, LANES), mine.dtype),
        in_specs=[HBM_SPEC], out_specs=HBM_SPEC,
        scratch_shapes=[pltpu.SemaphoreType.DMA((7,)), pltpu.SemaphoreType.DMA((7,)), pltpu.SemaphoreType.DMA],
    )(mine)


def allgather_direct(mine, *, name):
    R = mine.shape[0]

    def body(x_ref, out_ref, send_sems, recv_sems, local_sem):
        start, finish = _direct_exchange("gather", x_ref, out_ref, send_sems, recv_sems, local_sem)
        start()
        finish()

    return pl.pallas_call(
        body, name=name, out_shape=jax.ShapeDtypeStruct((8, R, LANES), mine.dtype),
        in_specs=[HBM_SPEC], out_specs=HBM_SPEC,
        scratch_shapes=[pltpu.SemaphoreType.DMA((7,)), pltpu.SemaphoreType.DMA((7,)), pltpu.SemaphoreType.DMA],
    )(mine)


def send_to_sibling(v, *, name):
    def body(v_ref, out_ref, send_sem, recv_sem):
        x, y, c = _place()
        cp = pltpu.make_async_remote_copy(src_ref=v_ref, dst_ref=out_ref, send_sem=send_sem, recv_sem=recv_sem,
                                          device_id=(x, y, 1 - c), device_id_type=MESH)
        cp.start()
        cp.wait()

    return pl.pallas_call(
        body, name=name, out_shape=jax.ShapeDtypeStruct(v.shape, v.dtype), in_specs=[HBM_SPEC], out_specs=HBM_SPEC,
        scratch_shapes=[pltpu.SemaphoreType.DMA, pltpu.SemaphoreType.DMA],
    )(v)


def chip_exchange(p, *, name):
    R = p.shape[1]

    def body(p_ref, out_ref, send_sems, recv_sems):
        x, y, c = _place()
        chips = [(1 - x, y), (x, 1 - y), (1 - x, 1 - y)]
        sends = [pltpu.make_async_remote_copy(
            src_ref=p_ref.at[2 * px + py], dst_ref=out_ref.at[j], send_sem=send_sems.at[j], recv_sem=recv_sems.at[j],
            device_id=(px, py, c), device_id_type=MESH) for j, (px, py) in enumerate(chips)]
        for cp in sends:
            cp.start()
        for cp in sends:
            cp.wait()

    return pl.pallas_call(
        body, name=name, out_shape=jax.ShapeDtypeStruct((3, R, LANES), p.dtype), in_specs=[HBM_SPEC],
        out_specs=HBM_SPEC,
        scratch_shapes=[pltpu.SemaphoreType.DMA((3,)), pltpu.SemaphoreType.DMA((3,))],
    )(p)


def add_blocks(terms, out_dtype, *, name, tile=1024):
    terms = [t if isinstance(t, tuple) else (t, None) for t in terms]
    R = terms[0][0].shape[-2]
    tr = R
    for d in range(16, min(R, tile) + 1, 16):
        if R % d == 0:
            tr = d

    def body(*refs):
        acc = refs[0][...].astype(F32)
        for ref in refs[1:-1]:
            acc = acc + ref[...].astype(F32)
        refs[-1][...] = acc.astype(out_dtype)

    spec = pl.BlockSpec((tr, LANES), lambda i: (i, 0))
    in_specs = [spec if slot is None else pl.BlockSpec((None, tr, LANES), lambda i, slot=slot: (slot, i, 0))
                for _, slot in terms]
    return pl.pallas_call(
        body, name=name, grid=(R // tr,), in_specs=in_specs, out_specs=spec,
        out_shape=jax.ShapeDtypeStruct((R, LANES), out_dtype), compiler_params=_cparams("parallel"),
    )(*[a for a, _ in terms])


FLAT_ROW_STEP = 640


def _half_rows(arr, cc):
    hr = arr.shape[0] // 2
    return lax.dynamic_slice_in_dim(arr, cc * hr, hr, axis=0).reshape(-1)


def _flat_half(shards, cc, dtype):
    flat = jnp.concatenate([_half_rows(shards[n], cc).astype(dtype) for n in BIG])
    rows = -(-flat.shape[0] // (FLAT_ROW_STEP * LANES)) * FLAT_ROW_STEP
    return jnp.pad(flat, (0, rows * LANES - flat.shape[0])).reshape(rows, LANES)


def _flat_rows(shapes):
    n = sum((R // 2) * C for R, C in shapes.values()) // LANES
    return -(-n // FLAT_ROW_STEP) * FLAT_ROW_STEP


def _to_blocks(full, shapes, dtype):
    pieces = []
    for n in BIG:
        R, C = shapes[n]
        a = full[n].astype(dtype)
        if BIG_AXIS[n] == 2:
            a = a.reshape(2, R // 2, 4, C).transpose(2, 0, 1, 3)
        pieces.append(a.reshape(8, (R // 2) * C // LANES, LANES))
    flat = jnp.concatenate(pieces, axis=1)
    return jnp.pad(flat, ((0, 0), (0, _flat_rows(shapes) - flat.shape[1]), (0, 0)))


def _from_blocks(g8, shapes):
    out, off = {}, 0
    for n in BIG:
        R, C = shapes[n]
        rows = (R // 2) * C // LANES
        a = g8[:, off:off + rows, :].reshape(4, 2, R // 2, C)
        out[n] = a.transpose(1, 2, 0, 3).reshape(R, 4 * C) if BIG_AXIS[n] == 2 else a.reshape(4 * R, C)
        off += rows
    return out


def _unflat_halves(flat_by_c, shapes):
    out, off = {}, 0
    for n in BIG:
        R, C = shapes[n]
        sz = (R // 2) * C
        out[n] = jnp.concatenate([flat_by_c[c][off:off + sz].reshape(R // 2, C) for c in range(2)], axis=0)
        off += sz
    return out


def _to_heads(a, nh):
    T = a.shape[0]
    return a.reshape(T, nh, a.shape[1] // nh).transpose(1, 0, 2)


def _from_heads(a):
    nh, T, d = a.shape
    return a.transpose(1, 0, 2).reshape(T, nh * d)


def _to_heads_t(a, nh):
    T = a.shape[0]
    return a.reshape(T, nh, a.shape[1] // nh).transpose(1, 2, 0)


def _from_heads_t(a):
    nh, d, T = a.shape
    return a.transpose(2, 0, 1).reshape(T, nh * d)


def _pad_cols(a, n):
    return jnp.pad(a, ((0, 0), (0, n - a.shape[1])))


def _pack_w_in(w):
    offs = [sum(IN_SPLITS[:i]) for i in range(len(IN_SPLITS) + 1)]
    sb, z, xbc, dt, cq, ckv, kr = [w[:, offs[i]:offs[i + 1]] for i in range(len(IN_SPLITS))]
    zeros = lambda n: jnp.zeros((w.shape[0], n), w.dtype)
    h = MLA_ROPE // 2
    kra = jnp.concatenate([zeros(MLA_NOPE), kr, zeros(LANES - MLA_QK)], axis=1)
    krb = jnp.concatenate([zeros(MLA_NOPE), -kr[:, h:], kr[:, :h], zeros(LANES - MLA_QK)], axis=1)
    return sb, jnp.concatenate([z, xbc, cq, ckv, _pad_cols(dt, LANES), kra, krb], axis=1)


def _unpack_gw_in(g_sb, g):
    h = MLA_ROPE // 2
    ga, gb = g[:, OFF_KRA:OFF_KRA + LANES], g[:, OFF_KRB:OFF_KRB + LANES]
    gkr = ga[:, MLA_NOPE:MLA_QK] + jnp.concatenate([gb[:, MLA_NOPE + h:MLA_QK], -gb[:, MLA_NOPE:MLA_NOPE + h]], axis=1)
    return jnp.concatenate([g_sb, g[:, OFF_Z:OFF_Z + 512], g[:, OFF_XBC:OFF_XBC + 768],
                            g[:, OFF_DT:OFF_DT + 8], g[:, OFF_CQ:OFF_CQ + 256], g[:, OFF_CKV:OFF_CKV + 128], gkr], axis=1)


def _pack_w_uq(w):
    zeros = lambda n: jnp.zeros((w.shape[0], n), w.dtype)
    h = MLA_ROPE // 2
    pp, rr = [], []
    for i in range(MLA_HEADS):
        nope = w[:, MLA_QK * i:MLA_QK * i + MLA_NOPE]
        rope = w[:, MLA_QK * i + MLA_NOPE:MLA_QK * (i + 1)]
        pp += [nope, rope, zeros(LANES - MLA_QK)]
        rr += [zeros(MLA_NOPE), -rope[:, h:], rope[:, :h], zeros(LANES - MLA_QK)]
    return jnp.concatenate(pp, axis=1), jnp.concatenate(rr, axis=1)


def _unpack_gw_uq(gp, gr):
    h = MLA_ROPE // 2
    out = []
    for i in range(MLA_HEADS):
        b = LANES * i
        out.append(gp[:, b:b + MLA_NOPE])
        out.append(gp[:, b + MLA_NOPE:b + MLA_NOPE + h] + gr[:, b + MLA_NOPE + h:b + MLA_QK])
        out.append(gp[:, b + MLA_NOPE + h:b + MLA_QK] - gr[:, b + MLA_NOPE:b + MLA_NOPE + h])
    return jnp.concatenate(out, axis=1)


def _pack_w_ukv(w):
    zeros = lambda n: jnp.zeros((w.shape[0], n), w.dtype)
    kk, vv = [], []
    for i in range(MLA_HEADS):
        b = (MLA_NOPE + MLA_V) * i
        kk += [w[:, b:b + MLA_NOPE], zeros(LANES - MLA_NOPE)]
        vv += [w[:, b + MLA_NOPE:b + MLA_NOPE + MLA_V], zeros(LANES - MLA_V)]
    return jnp.concatenate(kk, axis=1), jnp.concatenate(vv, axis=1)


def _unpack_gw_ukv(gk, gv):
    out = []
    for i in range(MLA_HEADS):
        out += [gk[:, LANES * i:LANES * i + MLA_NOPE], gv[:, LANES * i:LANES * i + MLA_V]]
    return jnp.concatenate(out, axis=1)


def _rope_tables(positions):
    inv_freq = 1.0 / (ROPE_THETA ** (jnp.arange(0, MLA_ROPE, 2, dtype=F32) / MLA_ROPE))
    ang = positions.astype(F32)[:, None] * inv_freq
    cos, sin = jnp.cos(ang), jnp.sin(ang)
    T = positions.shape[0]
    one, zero = jnp.ones((T, MLA_NOPE), F32), jnp.zeros((T, MLA_NOPE), F32)
    pad1, pad0 = jnp.ones((T, LANES - MLA_QK), F32), jnp.zeros((T, LANES - MLA_QK), F32)
    return jnp.concatenate([one, cos, cos, pad1], axis=1), jnp.concatenate([zero, sin, sin, pad0], axis=1)


def _row(v):
    return v.reshape(1, -1)


def _pad_row(v):
    return _pad_cols(v.reshape(1, -1), LANES)


def _layer_weights(full, small, li):
    p = {}
    p["w_sb"], p["w_rest"] = _pack_w_in(full["w_in"])
    q_scale = jnp.concatenate([jnp.full((1, SB_HEADS * SB_DIM), SB_DIM ** -0.5, BF16),
                               jnp.ones((1, 2 * SB_HEADS * SB_DIM), BF16)], axis=1)
    p["w_sb_fwd"] = p["w_sb"] * q_scale
    p["wqp"], p["wqr"] = _pack_w_uq(full["mla_w_uq"])
    p["wkp"], p["wvp"] = _pack_w_ukv(full["mla_w_ukv"])
    p["w_out"] = full["w_out"]
    p["w_up"] = full["ffn_w_up"]
    p["w_down"] = full["ffn_w_down"]
    for n in ("mix_norm", "sb_out_norm", "ssm_conv_b", "ssm_out_norm", "mla_q_norm", "mla_kv_norm", "mla_out_norm",
              "ffn_norm", "ffn_conv_b"):
        p[n] = _row(small[n][li])
    for n in ("ssm_dt_bias", "ssm_a_log", "ssm_d"):
        p[n] = _pad_row(small[n][li])
    p["ssm_conv_w"] = small["ssm_conv_w"][li]
    p["ffn_conv_w"] = small["ffn_conv_w"][li]
    return p


def _layer_fwd(h, p, cos, sin, li, exchange=None):
    T = h.shape[0]
    nm = lambda s: "l%d_%s" % (li, s)
    s = {"h": h}
    (n1,) = rowwise(rms_fn, [h], [p["mix_norm"]], [(D_MODEL, BF16)], name=nm("mix_norm"))
    proj = matmul(n1, p["w_rest"], name=nm("in_proj"))
    qkv = matmul(n1, p["w_sb_fwd"], name=nm("in_proj_sb"), out_dtype=BF16)
    s["n1"], s["proj"] = n1, proj
    s["sb_q"] = _to_heads_t(qkv[:, 0:256], SB_HEADS)
    s["sb_k"] = _to_heads(qkv[:, 256:512], SB_HEADS)
    s["sb_v"] = _to_heads(qkv[:, 512:768], SB_HEADS)
    y_sb_hm, s["sb_bt"], s["sb_first"] = sb_fwd(s["sb_q"], s["sb_k"], s["sb_v"], name=nm("sb_fwd"))
    s["y_sb"] = _from_heads_t(y_sb_hm)
    xbc = ssm_conv_act(proj, p["ssm_conv_w"], p["ssm_conv_b"], name=nm("ssm_conv"))
    s["x_hm"] = _to_heads(xbc[:, :SSM_INNER], SSM_HEADS)
    s["b_hm"] = _to_heads(xbc[:, SSM_INNER:SSM_INNER + 128], SSM_GROUPS)
    s["c_hm"] = _to_heads(xbc[:, SSM_INNER + 128:], SSM_GROUPS)
    y_ssm_hm, s["states"] = ssd_fwd(s["x_hm"], s["b_hm"], s["c_hm"], proj, p["ssm_dt_bias"], p["ssm_a_log"],
                                    p["ssm_d"], name=nm("ssd_fwd"))
    s["y_ssm"] = _from_heads(y_ssm_hm)
    rows = [(proj, 256, OFF_CQ // 256), (proj, 128, OFF_CKV // 128), (proj, 128, OFF_KRA // 128),
            (proj, 128, OFF_KRB // 128), cos, sin]
    qp, kp, vv = rowwise(mla_prep_fn, rows, [p["mla_q_norm"], p["mla_kv_norm"], p["wqp"], p["wqr"], p["wkp"], p["wvp"]],
                         [(512, BF16), (512, BF16), (512, BF16)], name=nm("mla_prep"))
    s["mla_q"], s["mla_k"], s["mla_v"] = _to_heads_t(qp, MLA_HEADS), kp, vv
    s["mla_o"], s["mla_lse"], *rode = mla_fwd(s["mla_q"], kp, vv, name=nm("mla_fwd"), exchange=exchange)
    s["y_mla"] = _from_heads_t(s["mla_o"][:, :MLA_V, :])
    (cat,) = rowwise(merge_fn, [s["y_sb"], s["y_ssm"], (proj, 512, OFF_Z // 512), s["y_mla"]],
                     [p["sb_out_norm"], p["ssm_out_norm"], p["mla_out_norm"]], [(D_MODEL, BF16)], name=nm("merge"),
                     post=lambda a, b, c: (jnp.concatenate([a, b, c], axis=1),))
    s["cat"] = cat
    h1 = matmul(cat, p["w_out"], name=nm("out_proj"), residual=h)
    s["h1"] = h1
    (n2,) = rowwise(rms_fn, [h1], [p["ffn_norm"]], [(D_MODEL, BF16)], name=nm("ffn_norm"))
    up = matmul(n2, p["w_up"], name=nm("ffn_up"))
    act = ffn_act(up, p["ffn_conv_w"], p["ffn_conv_b"], name=nm("ffn_act"))
    s["n2"], s["up"], s["act"] = n2, up, act
    h2 = matmul(act, p["w_down"], name=nm("ffn_down"), residual=h1)
    return h2, s, (rode[0] if rode else None)


def _layer_bwd(dh2, s, p, cos, sin, li, exchange=None):
    nm = lambda t: "l%d_%s" % (li, t)
    g = {}
    proj = s["proj"]
    g["ffn_w_down"] = matmul(s["act"], dh2, name=nm("g_w_down"), ta=True)
    d_act = matmul(dh2, p["w_down"], name=nm("d_act"), out_dtype=BF16, tb=True)
    d_up_g, d_up_v, gwg, gwv, gbg, gbv = ffn_bwd_fused(s["up"], p["ffn_conv_w"], p["ffn_conv_b"], d_act,
                                                       name=nm("ffn_act_bwd"))
    g["ffn_conv_w"] = jnp.concatenate([gwg, gwv], axis=1)
    g["ffn_conv_b"] = jnp.concatenate([gbg[0], gbv[0]])
    g["ffn_w_up"] = jnp.concatenate([matmul(s["n2"], d_up_g, name=nm("g_w_up_gate"), ta=True),
                                     matmul(s["n2"], d_up_v, name=nm("g_w_up_val"), ta=True)], axis=1)
    d_n2 = matmul(d_up_g, p["w_up"], name=nm("d_n2_gate"), tb=True)
    d_n2 = matmul(d_up_v, p["w_up"], name=nm("d_n2_val"), tb=True, b_k0=D_FF, residual=d_n2)
    (dh1,), (gn,) = rowwise_bwd(rms_fn, [s["h1"]], [], [p["ffn_norm"]], [d_n2], [F32], name=nm("ffn_norm_bwd"),
                                add0=dh2)
    g["ffn_norm"] = gn[0]
    g["w_out"] = matmul(s["cat"], dh1, name=nm("g_w_out"), ta=True)
    d_cat = matmul(dh1, p["w_out"], name=nm("d_cat"), tb=True)
    (d_ysb, d_yssm, d_z, d_ymla), (g1, g2, g3) = rowwise_bwd(
        merge_fn, [s["y_sb"], s["y_ssm"], (proj, 512, OFF_Z // 512), s["y_mla"]], [],
        [p["sb_out_norm"], p["ssm_out_norm"], p["mla_out_norm"]], [d_cat], [F32, F32, BF16, F32], name=nm("merge_bwd"),
        pre_ct=lambda d: (d[:, 0:256], d[:, 256:768], d[:, 768:1024]))
    g["sb_out_norm"], g["ssm_out_norm"], g["mla_out_norm"] = g1[0], g2[0], g3[0]
    dq, dk, dv = sb_bwd(s["sb_q"], s["sb_k"], s["sb_v"], _to_heads_t(d_ysb, SB_HEADS), s["sb_bt"], s["sb_first"], name=nm("sb_bwd"),
                        q_scale=SB_DIM ** -0.5)
    d_sb = jnp.concatenate([_from_heads_t(dq), _from_heads(dk), _from_heads(dv)], axis=1).astype(BF16)
    do_t = jnp.pad(_to_heads_t(d_ymla, MLA_HEADS), ((0, 0), (0, LANES - MLA_V), (0, 0)))
    dqp, dkp, dvv, *rode = mla_bwd(s["mla_q"], s["mla_k"], s["mla_v"], do_t, s["mla_o"], s["mla_lse"],
                                   name=nm("mla_bwd"), exchange=exchange)
    rows = [(proj, 256, OFF_CQ // 256), (proj, 128, OFF_CKV // 128), (proj, 128, OFF_KRA // 128),
            (proj, 128, OFF_KRB // 128)]
    (d_cq, d_ckv, d_kra, d_krb), (gqn, gkvn, gwqp, gwqr, gwkp, gwvp) = rowwise_bwd(
        mla_prep_fn, rows, [cos, sin], [p["mla_q_norm"], p["mla_kv_norm"], p["wqp"], p["wqr"], p["wkp"], p["wvp"]],
        [_from_heads_t(dqp), dkp, dvv], [BF16] * 4, name=nm("mla_prep_bwd"), tile=256)
    g["mla_q_norm"], g["mla_kv_norm"] = gqn[0], gkvn[0]
    g["mla_w_uq"] = _unpack_gw_uq(gwqp, gwqr)
    g["mla_w_ukv"] = _unpack_gw_ukv(gwkp, gwvp)
    dx_hm, db_hm, dc_hm, d_dt, gdb, gal, gds = ssd_bwd(
        s["x_hm"], s["b_hm"], s["c_hm"], proj, s["states"], p["ssm_dt_bias"], p["ssm_a_log"], p["ssm_d"],
        _to_heads(d_yssm, SSM_HEADS), name=nm("ssd_bwd"))
    g["ssm_dt_bias"], g["ssm_a_log"], g["ssm_d"] = gdb[0, :8], gal[0, :8], gds[0, :8]
    d_xbc_act = jnp.concatenate([_from_heads(dx_hm), _from_heads(db_hm), _from_heads(dc_hm)], axis=1)
    d_pre = ssm_conv_bwd_a(proj, p["ssm_conv_w"], p["ssm_conv_b"], d_xbc_act, name=nm("ssm_conv_bwd_a"))
    d_xbc, g["ssm_conv_w"], gscb = conv_bwd_b(d_pre, proj, OFF_XBC, p["ssm_conv_w"], name=nm("ssm_conv_bwd_b"),
                                              out_dtype=BF16, tc=256)
    g["ssm_conv_b"] = gscb[0]
    d_proj = jnp.concatenate([d_z, d_xbc, d_cq, d_ckv, d_dt.astype(BF16), d_kra, d_krb], axis=1)
    g["w_in"] = _unpack_gw_in(matmul(s["n1"], d_sb, name=nm("g_w_in_sb"), ta=True),
                              matmul(s["n1"], d_proj, name=nm("g_w_in"), ta=True))
    d_n1 = matmul(d_sb, p["w_sb"], name=nm("d_n1_sb"), tb=True)
    d_n1 = matmul(d_proj, p["w_rest"], name=nm("d_n1"), tb=True, residual=d_n1)
    (dh0,), (gm,) = rowwise_bwd(rms_fn, [s["h"]], [], [p["mix_norm"]], [d_n1], [F32], name=nm("mix_norm_bwd"),
                                add0=dh1)
    g["mix_norm"] = gm[0]
    return dh0, g, (rode[0] if rode else None)


def kernel(x, positions, mix_norm, w_in, sb_out_norm, ssm_conv_w, ssm_conv_b, ssm_dt_bias, ssm_a_log, ssm_d, ssm_out_norm, mla_q_norm, mla_w_uq, mla_kv_norm, mla_w_ukv, mla_out_norm, w_out, ffn_norm, ffn_w_up, ffn_conv_w, ffn_conv_b, ffn_w_down, final_norm, loss_target, m_mix_norm, m_w_in, m_sb_out_norm, m_ssm_conv_w, m_ssm_conv_b, m_ssm_dt_bias, m_ssm_a_log, m_ssm_d, m_ssm_out_norm, m_mla_q_norm, m_mla_w_uq, m_mla_kv_norm, m_mla_w_ukv, m_mla_out_norm, m_w_out, m_ffn_norm, m_ffn_w_up, m_ffn_conv_w, m_ffn_conv_b, m_ffn_w_down, m_final_norm, v_mix_norm, v_w_in, v_sb_out_norm, v_ssm_conv_w, v_ssm_conv_b, v_ssm_dt_bias, v_ssm_a_log, v_ssm_d, v_ssm_out_norm, v_mla_q_norm, v_mla_w_uq, v_mla_kv_norm, v_mla_w_ukv, v_mla_out_norm, v_w_out, v_ffn_norm, v_ffn_w_up, v_ffn_conv_w, v_ffn_conv_b, v_ffn_w_down, v_final_norm):
    W = dict(mix_norm=mix_norm, w_in=w_in, sb_out_norm=sb_out_norm, ssm_conv_w=ssm_conv_w, ssm_conv_b=ssm_conv_b,
             ssm_dt_bias=ssm_dt_bias, ssm_a_log=ssm_a_log, ssm_d=ssm_d, ssm_out_norm=ssm_out_norm,
             mla_q_norm=mla_q_norm, mla_w_uq=mla_w_uq, mla_kv_norm=mla_kv_norm, mla_w_ukv=mla_w_ukv,
             mla_out_norm=mla_out_norm, w_out=w_out, ffn_norm=ffn_norm, ffn_w_up=ffn_w_up, ffn_conv_w=ffn_conv_w,
             ffn_conv_b=ffn_conv_b, ffn_w_down=ffn_w_down, final_norm=final_norm)
    M = dict(mix_norm=m_mix_norm, w_in=m_w_in, sb_out_norm=m_sb_out_norm, ssm_conv_w=m_ssm_conv_w,
             ssm_conv_b=m_ssm_conv_b, ssm_dt_bias=m_ssm_dt_bias, ssm_a_log=m_ssm_a_log, ssm_d=m_ssm_d,
             ssm_out_norm=m_ssm_out_norm, mla_q_norm=m_mla_q_norm, mla_w_uq=m_mla_w_uq, mla_kv_norm=m_mla_kv_norm,
             mla_w_ukv=m_mla_w_ukv, mla_out_norm=m_mla_out_norm, w_out=m_w_out, ffn_norm=m_ffn_norm,
             ffn_w_up=m_ffn_w_up, ffn_conv_w=m_ffn_conv_w, ffn_conv_b=m_ffn_conv_b, ffn_w_down=m_ffn_w_down,
             final_norm=m_final_norm)
    V = dict(mix_norm=v_mix_norm, w_in=v_w_in, sb_out_norm=v_sb_out_norm, ssm_conv_w=v_ssm_conv_w,
             ssm_conv_b=v_ssm_conv_b, ssm_dt_bias=v_ssm_dt_bias, ssm_a_log=v_ssm_a_log, ssm_d=v_ssm_d,
             ssm_out_norm=v_ssm_out_norm, mla_q_norm=v_mla_q_norm, mla_w_uq=v_mla_w_uq, mla_kv_norm=v_mla_kv_norm,
             mla_w_ukv=v_mla_w_ukv, mla_out_norm=v_mla_out_norm, w_out=v_w_out, ffn_norm=v_ffn_norm,
             ffn_w_up=v_ffn_w_up, ffn_conv_w=v_ffn_conv_w, ffn_conv_b=v_ffn_conv_b, ffn_w_down=v_ffn_w_down,
             final_norm=v_final_norm)
    depth = mix_norm.shape[0]
    cx, cy, cc = _place()
    chip = 2 * cx + cy
    T = x.shape[1]

    assert depth == 2
    shard_shapes = {n: W[n].shape[1:] for n in BIG}

    def layer_of(d, li):
        return {n: d[n][li] for n in BIG}

    def assemble(g8):
        return _from_blocks(g8, shard_shapes)

    full0 = assemble(allgather_blocks(_flat_half(layer_of(W, 0), cc, BF16), name="gather_weights_l0"))
    conv_full = {}
    small = {n: W[n] for n in SMALL_REPL}
    cw_flat = jnp.concatenate([W[n].reshape(-1) for n in SMALL_SHARD])
    cw_rows = -(-cw_flat.shape[0] // (8 * LANES)) * 8
    cw_all = allgather_direct(jnp.pad(cw_flat, (0, cw_rows * LANES - cw_flat.shape[0])).reshape(cw_rows, LANES),
                              name="gather_conv_taps")
    off = 0
    for n in SMALL_SHARD:
        sz = W[n].size
        conv_full[n] = jnp.concatenate(
            [cw_all[2 * k].reshape(-1)[off:off + sz].reshape(W[n].shape) for k in range(4)], axis=2)
        off += sz
    small.update(conv_full)

    cos, sin = _rope_tables(positions[0])
    params0 = _layer_weights(full0, small, 0)
    h, s0, g8 = _layer_fwd(x[0], params0, cos, sin, 0, exchange=("gather", _flat_half(layer_of(W, 1), cc, BF16)))
    params1 = _layer_weights(assemble(g8), small, 1)
    h, s1, _ = _layer_fwd(h, params1, cos, sin, 1)
    dh, g_final, loss_lanes = loss_head(h, loss_target[0], _row(final_norm), name="loss_head")

    dh, g1, _ = _layer_bwd(dh, s1, params1, cos, sin, 1)
    by_dest = _to_blocks(g1, shard_shapes, BF16)
    dh, g0, from_all = _layer_bwd(dh, s0, params0, cos, sin, 0, exchange=("all_to_all", by_dest))
    grad_x = dh[None]
    grads = [g0, g1]
    G = {n: jnp.stack([grads[li][n] for li in range(depth)]) for n in WEIGHTS if n != "final_norm" and n not in BIG}
    G["final_norm"] = g_final[0]
    half1 = add_blocks([(from_all, d) for d in range(8)], F32, name="grads_l1_sum")

    blocks0 = _to_blocks(g0, shard_shapes, BF16)
    R = blocks0.shape[1]
    blocks0 = blocks0.reshape(4, 2, R, LANES)
    mine_first = lax.dynamic_index_in_dim(blocks0, cc, 1, keepdims=False)
    for_sibling = lax.dynamic_index_in_dim(blocks0, 1 - cc, 1, keepdims=False)
    from_sibling = send_to_sibling(for_sibling.reshape(4 * R, LANES), name="grads_to_sibling")
    pair = add_blocks([mine_first.reshape(4 * R, LANES), from_sibling], BF16, name="grads_pair_sum").reshape(4, R, LANES)
    others = chip_exchange(pair, name="grads_chip_exchange")
    own = lax.dynamic_index_in_dim(pair, chip, 0, keepdims=False)
    half0 = add_blocks([own, (others, 0), (others, 1), (others, 2)], F32, name="grads_chip_sum")
    half = jnp.concatenate([half0, half1])
    other = send_to_sibling(half, name="grads_pair_swap")
    by_core = [jnp.where(cc == 0, half, other), jnp.where(cc == 0, other, half)]
    g_big_l = [_unflat_halves([a[li * R:(li + 1) * R].reshape(-1) for a in by_core], shard_shapes) for li in range(depth)]
    g_big = {n: jnp.stack([g_big_l[li][n] for li in range(depth)]) for n in BIG}

    small_list = [G[n].reshape(-1) for n in SMALL_REPL] + [G[n].reshape(-1) for n in SMALL_SHARD]
    small_list.append(jnp.sum(loss_lanes).reshape(1))
    sm = jnp.concatenate(small_list)
    n_small = sm.shape[0]
    sm_rows = -(-n_small // (16 * LANES)) * 16
    sm_all = allgather_direct(jnp.pad(sm, (0, sm_rows * LANES - n_small)).reshape(sm_rows, LANES), name="gather_small")
    sm_sum = add_blocks([(sm_all, d) for d in range(8)], F32, name="small_sum").reshape(-1)
    g_small, off = {}, 0
    for n in SMALL_REPL:
        g_small[n] = sm_sum[off:off + W[n].size].reshape(W[n].shape)
        off += W[n].size
    for n in SMALL_SHARD:
        full_shape = conv_full[n].shape
        sz = conv_full[n].size
        gfull = sm_sum[off:off + sz].reshape(full_shape)
        width = W[n].shape[2]
        g_small[n] = lax.dynamic_slice_in_dim(gfull, chip * width, width, axis=2)
        off += sz
    loss = sm_sum[off]

    grad_out, delta, new_m, new_v = {}, {}, {}, {}
    for n in BIG:
        shp = W[n].shape
        two_d = lambda a: a.reshape(shp[0] * shp[1], shp[2])
        d, nm_, nv_ = adamw(two_d(W[n]), two_d(g_big[n]), two_d(M[n]), two_d(V[n]), name="adamw_" + n)
        grad_out[n], delta[n], new_m[n], new_v[n] = g_big[n], d.reshape(shp), nm_.reshape(shp), nv_.reshape(shp)
    small_names = SMALL_REPL + SMALL_SHARD

    def flat_small(d):
        f = jnp.concatenate([d[n].reshape(-1) for n in small_names])
        rows = -(-f.shape[0] // (8 * LANES)) * 8
        return jnp.pad(f, (0, rows * LANES - f.shape[0])).reshape(rows, LANES)

    vpad = flat_small(V)
    d, nm_, nv_ = adamw(flat_small(W), flat_small(g_small), flat_small(M), vpad, name="adamw_small")
    off = 0
    for n in small_names:
        sz = W[n].size
        grad_out[n] = g_small[n]
        delta[n] = d.reshape(-1)[off:off + sz].reshape(W[n].shape)
        new_m[n] = nm_.reshape(-1)[off:off + sz].reshape(W[n].shape)
        new_v[n] = nv_.reshape(-1)[off:off + sz].reshape(W[n].shape)
        off += sz

    return (loss, grad_x, *[grad_out[n] for n in WEIGHTS], *[delta[n] for n in WEIGHTS],
            *[new_m[n] for n in WEIGHTS], *[new_v[n] for n in WEIGHTS])
```

```python
import functools
import math

import jax
import jax.numpy as jnp
from jax import lax
from jax.experimental import pallas as pl
from jax.experimental.pallas import tpu as pltpu

F32 = jnp.float32
BF16 = jnp.bfloat16

EPS = 1e-6
D_MODEL = 1024
SB_HEADS, SB_DIM = 4, 64
SSM_HEADS, SSM_DIM, SSM_GROUPS, SSM_STATE, SSM_CHUNK = 8, 64, 2, 64, 128
SSM_INNER = SSM_HEADS * SSM_DIM
SSM_CONV_DIM = SSM_INNER + 2 * SSM_GROUPS * SSM_STATE
MLA_HEADS, MLA_NOPE, MLA_ROPE, MLA_V = 4, 64, 32, 64
MLA_QK = MLA_NOPE + MLA_ROPE
MLA_SCALE = MLA_QK ** -0.5
ROPE_THETA = 10000.0
D_FF = 2816
IN_SPLITS = (768, 512, 768, 8, 256, 128, 32)

OFF_Z, OFF_XBC, OFF_CQ, OFF_CKV, OFF_DT, OFF_KRA, OFF_KRB = 0, 512, 1280, 1536, 1664, 1792, 1920
D_REST = 2048
LANES = 128

ADAM_LR, ADAM_B1, ADAM_B2, ADAM_EPS, ADAM_WD, ADAM_STEP = 0.001, 0.9, 0.999, 1e-08, 0.01, 10

V7X_VMEM_LIMIT = 48 * 1024 * 1024

NT = (((1,), (1,)), ((), ()))
TN = (((0,), (0,)), ((), ()))

BIG = ("w_in", "mla_w_uq", "mla_w_ukv", "w_out", "ffn_w_up", "ffn_w_down")
BIG_AXIS = {"w_in": 2, "mla_w_uq": 2, "mla_w_ukv": 2, "w_out": 1, "ffn_w_up": 2, "ffn_w_down": 1}
SMALL_REPL = ("mix_norm", "sb_out_norm", "ssm_conv_b", "ssm_dt_bias", "ssm_a_log", "ssm_d", "ssm_out_norm",
              "mla_q_norm", "mla_kv_norm", "mla_out_norm", "ffn_norm", "ffn_conv_b", "final_norm")
SMALL_SHARD = ("ssm_conv_w", "ffn_conv_w")
WEIGHTS = ("mix_norm", "w_in", "sb_out_norm", "ssm_conv_w", "ssm_conv_b", "ssm_dt_bias", "ssm_a_log", "ssm_d",
           "ssm_out_norm", "mla_q_norm", "mla_w_uq", "mla_kv_norm", "mla_w_ukv", "mla_out_norm", "w_out", "ffn_norm",
           "ffn_w_up", "ffn_conv_w", "ffn_conv_b", "ffn_w_down", "final_norm")


def _cparams(*sem):
    return pltpu.CompilerParams(dimension_semantics=sem if sem else None, vmem_limit_bytes=V7X_VMEM_LIMIT)


def _pick(n, target, mult=LANES):
    best = None
    for d in range(mult, min(n, target) + 1, mult):
        if n % d == 0:
            best = d
    return best or n


def _sigmoid(x):
    return 1.0 / (1.0 + jnp.exp(-x))


def _softplus(x):
    ax = jnp.where(x > 0, x, -x)
    return jnp.where(x > 0, x, 0.0) + jnp.log(1.0 + jnp.exp(-ax))


def _rms(x, g):
    return x * lax.rsqrt(jnp.mean(x * x, axis=-1, keepdims=True) + EPS) * g


def _raw_nn(a, b):
    return jnp.dot(a.astype(BF16), b.astype(BF16), preferred_element_type=F32)


def _raw_nt(a, b):
    return lax.dot_general(a.astype(BF16), b.astype(BF16), NT, preferred_element_type=F32)


def _raw_tn(a, b):
    return lax.dot_general(a.astype(BF16), b.astype(BF16), TN, preferred_element_type=F32)


@jax.custom_vjp
def mm_nn(a, b):
    return _raw_nn(a, b)


mm_nn.defvjp(lambda a, b: (_raw_nn(a, b), (a, b)),
             lambda r, ct: (_raw_nt(ct, r[1]), _raw_tn(r[0], ct)))


@jax.custom_vjp
def mm_nt(a, b):
    return _raw_nt(a, b)


mm_nt.defvjp(lambda a, b: (_raw_nt(a, b), (a, b)),
             lambda r, ct: (_raw_nn(ct, r[1]), _raw_tn(ct, r[0])))


@jax.custom_vjp
def mm_tn(a, b):
    return _raw_tn(a, b)


mm_tn.defvjp(lambda a, b: (_raw_tn(a, b), (a, b)),
             lambda r, ct: (_raw_nt(r[1], ct), _raw_nn(r[0], ct)))


def _split_dot(x, m, terms):
    acc = None
    r = x
    for t in range(terms):
        xt = r.astype(BF16)
        d = jnp.dot(xt, m, preferred_element_type=F32)
        acc = d if acc is None else acc + d
        if t + 1 < terms:
            r = r - xt.astype(F32)
    return acc


def _tri_dot(tri, x, terms=3):
    parts = []
    r = x
    for t in range(terms):
        xt = r.astype(BF16)
        parts.append(xt)
        if t + 1 < terms:
            r = r - xt.astype(F32)
    return jnp.dot(jnp.concatenate([tri] * terms, axis=1), jnp.concatenate(parts, axis=0),
                   preferred_element_type=F32)


def _tri(n, cmp):
    r = lax.broadcasted_iota(jnp.int32, (n, n), 0)
    c = lax.broadcasted_iota(jnp.int32, (n, n), 1)
    return cmp(r, c).astype(BF16)


@jax.custom_vjp
def csum_rows(x):
    return _tri_dot(_tri(x.shape[0], lambda r, c: r >= c), x)


csum_rows.defvjp(lambda x: (csum_rows(x), None),
                 lambda _, ct: (_tri_dot(_tri(ct.shape[0], lambda r, c: r <= c), ct),))


def matmul(a, b, *, name, out_dtype=F32, ta=False, tb=False, b_k0=0, residual=None):
    if ta:
        K, M = a.shape
    else:
        M, K = a.shape
    N = b.shape[0] if tb else b.shape[1]
    tm = _pick(M, 1408)
    tn = _pick(N, 1408)
    tk = _pick(K, 1408)
    nk = K // tk
    kb0 = b_k0 // tk
    assert b_k0 % tk == 0 and (tb or b_k0 == 0)
    has_res = residual is not None

    def body(*refs):
        if has_res:
            a_ref, b_ref, r_ref, o_ref, acc = refs
        else:
            a_ref, b_ref, o_ref, acc = refs
        k = pl.program_id(2)

        @pl.when(k == 0)
        def _():
            acc[...] = jnp.zeros_like(acc)

        av = a_ref[...].astype(BF16)
        bv = b_ref[...].astype(BF16)
        if ta:
            acc[...] += lax.dot_general(av, bv, TN, preferred_element_type=F32)
        elif tb:
            acc[...] += lax.dot_general(av, bv, NT, preferred_element_type=F32)
        else:
            acc[...] += jnp.dot(av, bv, preferred_element_type=F32)

        @pl.when(k == nk - 1)
        def _():
            r = acc[...]
            if has_res:
                r = r + r_ref[...].astype(F32)
            o_ref[...] = r.astype(o_ref.dtype)

    a_spec = pl.BlockSpec((tk, tm), lambda i, j, k: (k, i)) if ta else pl.BlockSpec((tm, tk), lambda i, j, k: (i, k))
    b_spec = pl.BlockSpec((tn, tk), lambda i, j, k: (j, kb0 + k)) if tb else pl.BlockSpec((tk, tn), lambda i, j, k: (k, j))
    in_specs = [a_spec, b_spec]
    args = [a, b]
    if has_res:
        in_specs.append(pl.BlockSpec((tm, tn), lambda i, j, k: (i, j)))
        args.append(residual)
    return pl.pallas_call(
        body, name=name, grid=(M // tm, N // tn, nk),
        in_specs=in_specs, out_specs=pl.BlockSpec((tm, tn), lambda i, j, k: (i, j)),
        out_shape=jax.ShapeDtypeStruct((M, N), out_dtype),
        scratch_shapes=[pltpu.VMEM((tm, tn), F32)],
        compiler_params=_cparams("parallel", "parallel", "arbitrary"),
    )(*args)


def _row_spec(entry, tl):
    if isinstance(entry, tuple):
        arr, width, cb = entry
        return arr, pl.BlockSpec((tl, width), lambda i, cb=cb: (i, cb))
    return entry, pl.BlockSpec((tl, entry.shape[1]), lambda i: (i, 0))


def _rows_T(entry):
    return (entry[0] if isinstance(entry, tuple) else entry).shape[0]


def rowwise(fn, rows, params, outs, *, name, tile=512, post=None):
    T = _rows_T(rows[0])
    tl = min(T, tile)
    nr, npar = len(rows), len(params)

    def body(*refs):
        r = [ref[...].astype(F32) for ref in refs[:nr]]
        p = [ref[...].astype(F32) for ref in refs[nr:nr + npar]]
        res = fn(*r, *p)
        if post is not None:
            res = post(*res)
        for o_ref, val in zip(refs[nr + npar:], res):
            o_ref[...] = val.astype(o_ref.dtype)

    arrs, specs = [], []
    for e in rows:
        a, s = _row_spec(e, tl)
        arrs.append(a)
        specs.append(s)
    for p in params:
        arrs.append(p)
        specs.append(pl.BlockSpec(p.shape, lambda i: (0, 0)))
    res = pl.pallas_call(
        body, name=name, grid=(T // tl,), in_specs=specs,
        out_specs=[pl.BlockSpec((tl, c), lambda i: (i, 0)) for c, _ in outs],
        out_shape=[jax.ShapeDtypeStruct((T, c), dt) for c, dt in outs],
        compiler_params=_cparams("parallel"),
    )(*arrs)
    return res


def rowwise_bwd(fn, rows, nd_rows, params, cts, grad_dtypes, *, name, tile=512, pre_ct=None, add0=None):
    T = _rows_T(rows[0])
    tl = min(T, tile)
    nr, nn, npar, nc = len(rows), len(nd_rows), len(params), len(cts)
    has_add = add0 is not None

    def body(*refs):
        pos = 0
        r = [ref[...].astype(F32) for ref in refs[pos:pos + nr]]
        pos += nr
        nd = [ref[...].astype(F32) for ref in refs[pos:pos + nn]]
        pos += nn
        p = [ref[...].astype(F32) for ref in refs[pos:pos + npar]]
        pos += npar
        c = [ref[...].astype(F32) for ref in refs[pos:pos + nc]]
        pos += nc
        if has_add:
            addv = refs[pos][...].astype(F32)
            pos += 1
        rg_refs = refs[pos:pos + nr]
        pg_refs = refs[pos + nr:pos + nr + npar]
        if pre_ct is not None:
            c = list(pre_ct(*c))
        _, vjp = jax.vjp(lambda *a: fn(*a[:nr], *nd, *a[nr:]), *r, *p)
        g = vjp(tuple(c))
        for j, ref in enumerate(rg_refs):
            val = g[j]
            if has_add and j == 0:
                val = val + addv
            ref[...] = val.astype(ref.dtype)
        if npar:
            @pl.when(pl.program_id(0) == 0)
            def _():
                for ref in pg_refs:
                    ref[...] = jnp.zeros_like(ref)
            for j, ref in enumerate(pg_refs):
                ref[...] += g[nr + j]

    arrs, specs = [], []
    widths = []
    for e in list(rows) + list(nd_rows):
        a, s = _row_spec(e, tl)
        arrs.append(a)
        specs.append(s)
        widths.append(s.block_shape[1])
    for p in params:
        arrs.append(p)
        specs.append(pl.BlockSpec(p.shape, lambda i: (0, 0)))
    for e in cts:
        a, s = _row_spec(e, tl)
        arrs.append(a)
        specs.append(s)
    if has_add:
        a, s = _row_spec(add0, tl)
        arrs.append(a)
        specs.append(s)
    out_specs = [pl.BlockSpec((tl, widths[j]), lambda i: (i, 0)) for j in range(nr)]
    out_shape = [jax.ShapeDtypeStruct((T, widths[j]), grad_dtypes[j]) for j in range(nr)]
    out_specs += [pl.BlockSpec(p.shape, lambda i: (0, 0)) for p in params]
    out_shape += [jax.ShapeDtypeStruct(p.shape, F32) for p in params]
    res = pl.pallas_call(
        body, name=name, grid=(T // tl,), in_specs=specs, out_specs=out_specs, out_shape=out_shape,
        compiler_params=_cparams("arbitrary"),
    )(*arrs)
    return list(res[:nr]), list(res[nr:])


def rms_fn(h, g):
    return (_rms(h, g),)


def merge_fn(ysb, yssm, z, ymla, g_sb, g_ssm, g_mla):
    ya = _rms(ysb, g_sb)
    yb = _rms(yssm * (z * _sigmoid(z)), g_ssm)
    yc = _rms(ymla, g_mla)
    return ya, yb, yc


def mla_prep_fn(cq, ckv, kra, krb, cos, sin, qn, kvn, wqp, wqr, wkp, wvp):
    cos4 = jnp.concatenate([cos] * MLA_HEADS, axis=1)
    sin4 = jnp.concatenate([sin] * MLA_HEADS, axis=1)
    nq = _rms(cq, qn)
    q = (mm_nn(nq, wqp) * cos4 + mm_nn(nq, wqr) * sin4) * MLA_SCALE
    nkv = _rms(ckv, kvn)
    kpe = kra * cos + krb * sin
    k = mm_nn(nkv, wkp) + jnp.concatenate([kpe] * MLA_HEADS, axis=1)
    v = mm_nn(nkv, wvp)
    return q, k, v


HALO = 8


def _prev_halo_spec(tl, tc, col_of):
    return pl.BlockSpec((HALO, tc), lambda i, j: (jnp.maximum(i * (tl // HALO) - 1, 0), col_of(j)))


def _fill_prev(buf, x_ref, halo_ref, i):
    buf[0:HALO, :] = jnp.where(i > 0, halo_ref[...].astype(F32), 0.0)
    buf[HALO:, :] = x_ref[...].astype(F32)


def _conv_from(buf, w_ref, b_ref, K, tl):
    acc = b_ref[...].astype(F32) + jnp.zeros((tl, buf.shape[1]), F32)
    for k in range(K):
        acc = acc + buf[pl.ds(HALO - (K - 1 - k), tl), :] * w_ref[k:k + 1, :].astype(F32)
    return acc


def ssm_conv_act(proj, w, b, *, name, tile=512, tc=256):
    T = proj.shape[0]
    K, C = w.shape
    tl = min(T, tile)
    c0 = OFF_XBC // tc
    nb = C // tc
    hd = SSM_DIM
    per = tc // hd

    def body(*refs):
        xs, halos = refs[0:nb], refs[nb:2 * nb]
        w_ref, b_ref = refs[2 * nb:2 * nb + 2]
        x_out, b_out, c_out = refs[2 * nb + 2:2 * nb + 5]
        bufs = refs[2 * nb + 5:]
        for j in range(nb):
            cols = slice(j * tc, (j + 1) * tc)
            _fill_prev(bufs[j], xs[j], halos[j], pl.program_id(0))
            u = b_ref[:, cols].astype(F32) + jnp.zeros((tl, tc), F32)
            for k in range(K):
                u = u + bufs[j][pl.ds(HALO - (K - 1 - k), tl), :] * w_ref[k:k + 1, cols].astype(F32)
            act = u * _sigmoid(u)
            for hh in range(per):
                piece = act[:, hh * hd:(hh + 1) * hd]
                head = j * per + hh
                if head < SSM_HEADS:
                    x_out[head] = piece
                elif head < SSM_HEADS + SSM_GROUPS:
                    b_out[head - SSM_HEADS] = piece
                else:
                    c_out[head - SSM_HEADS - SSM_GROUPS] = piece

    in_specs = ([pl.BlockSpec((tl, tc), lambda i, j=j: (i, c0 + j)) for j in range(nb)]
                + [pl.BlockSpec((HALO, tc), lambda i, j=j: (jnp.maximum(i * (tl // HALO) - 1, 0), c0 + j)) for j in range(nb)]
                + [pl.BlockSpec((K, C), lambda i: (0, 0)), pl.BlockSpec((1, C), lambda i: (0, 0))])
    return pl.pallas_call(
        body, name=name, grid=(T // tl,), in_specs=in_specs,
        out_specs=[pl.BlockSpec((SSM_HEADS, tl, hd), lambda i: (0, i, 0)),
                   pl.BlockSpec((SSM_GROUPS, tl, hd), lambda i: (0, i, 0)),
                   pl.BlockSpec((SSM_GROUPS, tl, hd), lambda i: (0, i, 0))],
        out_shape=[jax.ShapeDtypeStruct((SSM_HEADS, T, hd), F32), jax.ShapeDtypeStruct((SSM_GROUPS, T, hd), F32),
                   jax.ShapeDtypeStruct((SSM_GROUPS, T, hd), F32)],
        scratch_shapes=[pltpu.VMEM((tl + HALO, tc), F32)] * nb,
        compiler_params=_cparams("parallel"),
    )(*([proj] * (2 * nb)), w, b)


def ssm_conv_bwd_a(proj, w, b, d_out, *, name, tile=512, tc=256):
    T = proj.shape[0]
    K, C = w.shape
    tl = min(T, tile)
    c0 = OFF_XBC // tc

    def body(x_ref, halo_ref, w_ref, b_ref, d_ref, o_ref, buf):
        _fill_prev(buf, x_ref, halo_ref, pl.program_id(0))
        u = _conv_from(buf, w_ref, b_ref, K, tl)
        s = _sigmoid(u)
        o_ref[...] = d_ref[...].astype(F32) * (s * (1.0 + u * (1.0 - s)))

    return pl.pallas_call(
        body, name=name, grid=(T // tl, C // tc),
        in_specs=[pl.BlockSpec((tl, tc), lambda i, j: (i, c0 + j)), _prev_halo_spec(tl, tc, lambda j: c0 + j),
                  pl.BlockSpec((K, tc), lambda i, j: (0, j)), pl.BlockSpec((1, tc), lambda i, j: (0, j)),
                  pl.BlockSpec((tl, tc), lambda i, j: (i, j))],
        out_specs=pl.BlockSpec((tl, tc), lambda i, j: (i, j)),
        out_shape=jax.ShapeDtypeStruct((T, C), F32),
        scratch_shapes=[pltpu.VMEM((tl + HALO, tc), F32)],
        compiler_params=_cparams("parallel", "parallel"),
    )(proj, proj, w, b, d_out)


def ffn_act(up, w, b, *, name, tile=512, tc=1408):
    T = up.shape[0]
    K = w.shape[0]
    tl = min(T, tile)
    nj = D_FF // tc

    def body(xg_ref, hg_ref, xv_ref, hv_ref, wg_ref, wv_ref, bg_ref, bv_ref, o_ref, bufg, bufv):
        i = pl.program_id(0)
        _fill_prev(bufg, xg_ref, hg_ref, i)
        _fill_prev(bufv, xv_ref, hv_ref, i)
        gate = _conv_from(bufg, wg_ref, bg_ref, K, tl)
        val = _conv_from(bufv, wv_ref, bv_ref, K, tl)
        o_ref[...] = (gate * _sigmoid(gate) * val).astype(o_ref.dtype)

    return pl.pallas_call(
        body, name=name, grid=(T // tl, nj),
        in_specs=[pl.BlockSpec((tl, tc), lambda i, j: (i, j)), _prev_halo_spec(tl, tc, lambda j: j),
                  pl.BlockSpec((tl, tc), lambda i, j: (i, nj + j)), _prev_halo_spec(tl, tc, lambda j: nj + j),
                  pl.BlockSpec((K, tc), lambda i, j: (0, j)), pl.BlockSpec((K, tc), lambda i, j: (0, nj + j)),
                  pl.BlockSpec((1, tc), lambda i, j: (0, j)), pl.BlockSpec((1, tc), lambda i, j: (0, nj + j))],
        out_specs=pl.BlockSpec((tl, tc), lambda i, j: (i, j)),
        out_shape=jax.ShapeDtypeStruct((T, D_FF), BF16),
        scratch_shapes=[pltpu.VMEM((tl + HALO, tc), F32), pltpu.VMEM((tl + HALO, tc), F32)],
        compiler_params=_cparams("parallel", "parallel"),
    )(up, up, up, up, w, w, b, b)


def ffn_bwd_fused(up, w, b, d_act, *, name, tile=512, tc=256):
    T = up.shape[0]
    K = w.shape[0]
    tl = min(T, tile)
    nj = D_FF // tc
    nblk = T // HALO
    ext = tl + HALO

    def body(xg, hgp, hgn, xv, hvp, hvn, wg, wv, bg, bv, d, dn, og, ov, dwg, dwv, dbg, dbv, bufg, bufv, dgb, dvb):
        i = pl.program_id(1)
        last = pl.num_programs(1) - 1

        def fill(buf, x_ref, prev_ref, next_ref):
            buf[0:HALO, :] = jnp.where(i > 0, prev_ref[...].astype(F32), 0.0)
            buf[HALO:HALO + tl, :] = x_ref[...].astype(F32)
            buf[HALO + tl:, :] = jnp.where(i < last, next_ref[...].astype(F32), 0.0)

        def conv_ext(buf, w_ref, b_ref):
            acc = b_ref[...].astype(F32) + jnp.zeros((ext, tc), F32)
            for k in range(K):
                acc = acc + buf[pl.ds(HALO - (K - 1 - k), ext), :] * w_ref[k:k + 1, :].astype(F32)
            return acc

        fill(bufg, xg, hgp, hgn)
        fill(bufv, xv, hvp, hvn)
        gate = conv_ext(bufg, wg, bg)
        val = conv_ext(bufv, wv, bv)
        dd = jnp.concatenate([d[...].astype(F32), jnp.where(i < last, dn[...].astype(F32)[0:HALO], 0.0)], axis=0)
        s = _sigmoid(gate)
        dgb[...] = dd * val * (s * (1.0 + gate * (1.0 - s)))
        dvb[...] = dd * (gate * s)

        @pl.when(i == 0)
        def _():
            for ref in (dwg, dwv, dbg, dbv):
                ref[...] = jnp.zeros_like(ref)

        for dbuf, xbuf, w_ref, o_ref, dw_ref, db_ref in ((dgb, bufg, wg, og, dwg, dbg), (dvb, bufv, wv, ov, dwv, dbv)):
            cur = dbuf[0:tl, :]
            dx = jnp.zeros((tl, tc), F32)
            for k in range(K):
                sft = K - 1 - k
                dx = dx + dbuf[pl.ds(sft, tl), :] * w_ref[k:k + 1, :].astype(F32)
                dw_ref[k:k + 1, :] += jnp.sum(cur * xbuf[pl.ds(HALO - sft, tl), :], axis=0, keepdims=True)
            db_ref[...] += jnp.sum(cur, axis=0, keepdims=True)
            o_ref[...] = dx.astype(o_ref.dtype)

    prev = lambda i: jnp.maximum(i * (tl // HALO) - 1, 0)
    nxt = lambda i: jnp.minimum((i + 1) * (tl // HALO), nblk - 1)

    def x_specs(col):
        return [pl.BlockSpec((tl, tc), lambda j, i: (i, col(j))), pl.BlockSpec((HALO, tc), lambda j, i: (prev(i), col(j))),
                pl.BlockSpec((HALO, tc), lambda j, i: (nxt(i), col(j)))]

    gcol, vcol = (lambda j: j), (lambda j: nj + j)
    in_specs = (x_specs(gcol) + x_specs(vcol)
                + [pl.BlockSpec((K, tc), lambda j, i: (0, j)), pl.BlockSpec((K, tc), lambda j, i: (0, nj + j)),
                   pl.BlockSpec((1, tc), lambda j, i: (0, j)), pl.BlockSpec((1, tc), lambda j, i: (0, nj + j)),
                   pl.BlockSpec((tl, tc), lambda j, i: (i, j)),
                   pl.BlockSpec((2 * HALO, tc), lambda j, i: (jnp.minimum((i + 1) * (tl // (2 * HALO)), nblk // 2 - 1), j))])
    row_out = pl.BlockSpec((tl, tc), lambda j, i: (i, j))
    w_out = pl.BlockSpec((K, tc), lambda j, i: (0, j))
    b_out = pl.BlockSpec((1, tc), lambda j, i: (0, j))
    return pl.pallas_call(
        body, name=name, grid=(nj, T // tl), in_specs=in_specs,
        out_specs=[row_out, row_out, w_out, w_out, b_out, b_out],
        out_shape=[jax.ShapeDtypeStruct((T, D_FF), BF16)] * 2 + [jax.ShapeDtypeStruct((K, D_FF), F32)] * 2
        + [jax.ShapeDtypeStruct((1, D_FF), F32)] * 2,
        scratch_shapes=[pltpu.VMEM((tl + 2 * HALO, tc), F32)] * 2 + [pltpu.VMEM((ext, tc), F32)] * 2,
        compiler_params=_cparams("parallel", "arbitrary"),
    )(up, up, up, up, up, up, w, w, b, b, d_act, d_act)


def conv_bwd_b(du, x, x_off, w, *, name, out_dtype, tile=512, tc=256):
    T, C = du.shape
    K = w.shape[0]
    tl = min(T, tile)
    c0 = x_off // tc
    nblk = T // HALO

    def body(du_ref, nx_ref, x_ref, w_ref, dx_ref, dw_ref, db_ref, dbuf):
        i = pl.program_id(1)
        last = pl.num_programs(1) - 1
        d = du_ref[...].astype(F32)
        dbuf[0:tl, :] = d
        dbuf[tl:, :] = jnp.where(i < last, nx_ref[...].astype(F32), 0.0)

        @pl.when(i == 0)
        def _():
            dw_ref[...] = jnp.zeros_like(dw_ref)
            db_ref[...] = jnp.zeros_like(db_ref)

        xin = x_ref[...].astype(F32)
        dx = jnp.zeros((tl, tc), F32)
        for k in range(K):
            s = K - 1 - k
            shifted = dbuf[pl.ds(s, tl), :]
            dx = dx + shifted * w_ref[k:k + 1, :].astype(F32)
            dw_ref[k:k + 1, :] += jnp.sum(shifted * xin, axis=0, keepdims=True)
        db_ref[...] += jnp.sum(d, axis=0, keepdims=True)
        dx_ref[...] = dx.astype(dx_ref.dtype)

    return pl.pallas_call(
        body, name=name, grid=(C // tc, T // tl),
        in_specs=[pl.BlockSpec((tl, tc), lambda j, i: (i, j)),
                  pl.BlockSpec((HALO, tc), lambda j, i: (jnp.minimum((i + 1) * (tl // HALO), nblk - 1), j)),
                  pl.BlockSpec((tl, tc), lambda j, i: (i, c0 + j)),
                  pl.BlockSpec((K, tc), lambda j, i: (0, j))],
        out_specs=[pl.BlockSpec((tl, tc), lambda j, i: (i, j)), pl.BlockSpec((K, tc), lambda j, i: (0, j)),
                   pl.BlockSpec((1, tc), lambda j, i: (0, j))],
        out_shape=[jax.ShapeDtypeStruct((T, C), out_dtype), jax.ShapeDtypeStruct((K, C), F32),
                   jax.ShapeDtypeStruct((1, C), F32)],
        scratch_shapes=[pltpu.VMEM((tl + HALO, tc), F32)],
        compiler_params=_cparams("parallel", "arbitrary"),
    )(du, du, x, w)


def _attn_tiles(T, keys=256):
    return min(T, 1024), min(T, keys)


def _after_diag(keys, queries, strict):
    d = lax.broadcasted_iota(jnp.int32, (keys, queries), 1) - lax.broadcasted_iota(jnp.int32, (keys, queries), 0)
    return d > 0 if strict else d >= 0


def _log_gates(z):
    l1p = jnp.log(1.0 + jnp.exp(-jnp.abs(z)))
    a = jnp.minimum(z, 0.0) - l1p
    return a, a - z


def _causal_sweep(i, tq, tk, block, descending, keep_going=None, first_block=None):
    nb = tq // tk
    n_full = i * nb

    def band():
        order = reversed(range(nb)) if descending else range(nb)
        for bb in order:
            block(pl.multiple_of(i * tq + bb * tk, tk), bb * tk, True)

    def full():
        if descending and keep_going is not None:
            def step(j):
                block(pl.multiple_of((n_full - 1 - j) * tk, tk), 0, False)
                return j + 1
            done = lax.while_loop(lambda j: jnp.logical_and(j < n_full, keep_going()), step, jnp.int32(0))
            return n_full - done

        def step(j, c):
            kb = (n_full - 1 - j) if descending else j
            block(pl.multiple_of(kb * tk, tk), 0, False)
            return c
        lax.fori_loop(0 if first_block is None else first_block, n_full, step, 0)
        return None

    if descending:
        band()
        return full()
    full()
    band()
    return None


def sb_fwd(q, k, v, *, name):
    H, dh, T = q.shape
    tq, tk = _attn_tiles(T)

    def body(q_ref, k_ref, v_ref, y_ref, bt_ref, first_ref, acc, run):
        acc[...] = jnp.zeros_like(acc)
        run[...] = jnp.zeros_like(run)
        u_after = _tri(tk, lambda r, c: r < c)

        def block(k0, r0, masked):
            kb = k_ref[pl.ds(k0, tk), :]
            vb = v_ref[pl.ds(k0, tk), :]
            z = jnp.dot(kb, q_ref[:, r0:], preferred_element_type=F32)
            a, b = _log_gates(z)
            if masked:
                valid = _after_diag(tk, tq - r0, True)
                b = jnp.where(valid, b, 0.0)
            w = jnp.exp(a + _tri_dot(u_after, b, 2) + run[:, r0:])
            if masked:
                w = jnp.where(valid, w, 0.0)
            acc[:, r0:] += lax.dot_general(vb, w.astype(BF16), TN, preferred_element_type=F32)
            run[:, r0:] += jnp.sum(b, axis=0, keepdims=True)

        first = _causal_sweep(pl.program_id(1), tq, tk, block, descending=True,
                              keep_going=lambda: jnp.max(run[...]) >= SB_ZERO_BELOW)
        y_ref[...] = acc[...]
        bt_ref[...] = run[...]
        first_ref[...] = jnp.zeros(first_ref.shape, F32) + first.astype(F32)

    return pl.pallas_call(
        body, name=name, grid=(H, T // tq),
        in_specs=[pl.BlockSpec((None, dh, tq), lambda h, i: (h, 0, i)),
                  pl.BlockSpec((None, T, dh), lambda h, i: (h, 0, 0)),
                  pl.BlockSpec((None, T, dh), lambda h, i: (h, 0, 0))],
        out_specs=[pl.BlockSpec((None, dh, tq), lambda h, i: (h, 0, i)),
                   pl.BlockSpec((None, 1, tq), lambda h, i: (h, 0, i)),
                   pl.BlockSpec((None, None, HALO, LANES), lambda h, i: (h, i, 0, 0))],
        out_shape=[jax.ShapeDtypeStruct((H, dh, T), F32), jax.ShapeDtypeStruct((H, 1, T), F32),
                   jax.ShapeDtypeStruct((H, T // tq, HALO, LANES), F32)],
        scratch_shapes=[pltpu.VMEM((dh, tq), F32), pltpu.VMEM((1, tq), F32)],
        compiler_params=_cparams("parallel", "parallel"),
    )(q, k, v)


def sb_bwd(q, k, v, dy, btot, first, *, name, q_scale):
    H, dh, T = q.shape
    tq, tk = _attn_tiles(T)

    def body(q_ref, k_ref, v_ref, dy_ref, bt_ref, first_ref, dq_ref, dk_ref, dv_ref, dq, pb, pg, dyb):
        @pl.when(pl.program_id(1) == 0)
        def _():
            dk_ref[...] = jnp.zeros_like(dk_ref)
            dv_ref[...] = jnp.zeros_like(dv_ref)

        dq[...] = jnp.zeros_like(dq)
        pb[...] = jnp.zeros_like(pb)
        pg[...] = jnp.zeros_like(pg)
        dyb[...] = dy_ref[...].astype(BF16)
        u_upto = _tri(tk, lambda r, c: r >= c)
        u_before = _tri(tk, lambda r, c: r > c)

        def block(k0, r0, masked):
            kb = k_ref[pl.ds(k0, tk), :]
            vb = v_ref[pl.ds(k0, tk), :]
            qv = q_ref[:, r0:]
            dyv = dyb[:, r0:]
            z = jnp.dot(kb, qv, preferred_element_type=F32)
            a, b = _log_gates(z)
            if masked:
                valid = _after_diag(tk, tq - r0, True)
                b = jnp.where(valid, b, 0.0)
            w = jnp.exp(a + (bt_ref[:, r0:] - pb[:, r0:] - _tri_dot(u_upto, b, 2)))
            if masked:
                w = jnp.where(valid, w, 0.0)
            g = w * jnp.dot(vb, dyv, preferred_element_type=F32)
            dz = g - jnp.exp(a) * (g + pg[:, r0:] + _tri_dot(u_before, g, 2))
            if masked:
                dz = jnp.where(valid, dz, 0.0)
            dz = dz.astype(BF16)
            dq[:, r0:] += lax.dot_general(kb, dz, TN, preferred_element_type=F32)
            dk_ref[pl.ds(k0, tk), :] += lax.dot_general(dz, qv, NT, preferred_element_type=F32)
            dv_ref[pl.ds(k0, tk), :] += lax.dot_general(w.astype(BF16), dyv, NT, preferred_element_type=F32)
            pb[:, r0:] += jnp.sum(b, axis=0, keepdims=True)
            pg[:, r0:] += jnp.sum(g, axis=0, keepdims=True)

        i = pl.program_id(1)
        first = jnp.clip(jnp.max(first_ref[...]).astype(jnp.int32), 0, i * (tq // tk))
        _causal_sweep(i, tq, tk, block, descending=False, first_block=first)
        dq_ref[...] = dq[...] * q_scale

    return pl.pallas_call(
        body, name=name, grid=(H, T // tq),
        in_specs=[pl.BlockSpec((None, dh, tq), lambda h, i: (h, 0, i)),
                  pl.BlockSpec((None, T, dh), lambda h, i: (h, 0, 0)),
                  pl.BlockSpec((None, T, dh), lambda h, i: (h, 0, 0)),
                  pl.BlockSpec((None, dh, tq), lambda h, i: (h, 0, i)),
                  pl.BlockSpec((None, 1, tq), lambda h, i: (h, 0, i)),
                  pl.BlockSpec((None, None, HALO, LANES), lambda h, i: (h, i, 0, 0))],
        out_specs=[pl.BlockSpec((None, dh, tq), lambda h, i: (h, 0, i)),
                   pl.BlockSpec((None, T, dh), lambda h, i: (h, 0, 0)),
                   pl.BlockSpec((None, T, dh), lambda h, i: (h, 0, 0))],
        out_shape=[jax.ShapeDtypeStruct((H, dh, T), F32), jax.ShapeDtypeStruct((H, T, dh), F32),
                   jax.ShapeDtypeStruct((H, T, dh), F32)],
        scratch_shapes=[pltpu.VMEM((dh, tq), F32), pltpu.VMEM((1, tq), F32), pltpu.VMEM((1, tq), F32),
                        pltpu.VMEM((dh, tq), BF16)],
        compiler_params=_cparams("parallel", "arbitrary"),
    )(q, k, v, dy, btot, first)


NEG = -1e30
SB_ZERO_BELOW = -105.0
MLA_KEYS = 512


def _call_with_exchange(body, exchange, *, name, grid, in_specs, out_specs, out_shape, scratch_shapes, args):
    if exchange is None:
        return pl.pallas_call(body, name=name, grid=grid, in_specs=in_specs, out_specs=out_specs, out_shape=out_shape,
                              scratch_shapes=scratch_shapes, compiler_params=_cparams("parallel", "arbitrary"))(*args)
    kind, src = exchange
    n_in, n_out, n_scr = len(in_specs), len(out_specs), len(scratch_shapes)
    R = src.shape[-2]

    def wrapped(*refs):
        ins, src_ref = refs[:n_in], refs[n_in]
        outs, xout = refs[n_in + 1:n_in + 1 + n_out], refs[n_in + 1 + n_out]
        scr = refs[n_in + 2 + n_out:n_in + 2 + n_out + n_scr]
        start, finish = _direct_exchange(kind, src_ref, xout, *refs[-3:])
        step = pl.program_id(0) * pl.num_programs(1) + pl.program_id(1)
        pl.when(step == 0)(start)
        body(*ins, *outs, *scr)
        pl.when(step == pl.num_programs(0) * pl.num_programs(1) - 1)(finish)

    return pl.pallas_call(
        wrapped, name=name, grid=grid, in_specs=list(in_specs) + [HBM_SPEC], out_specs=list(out_specs) + [HBM_SPEC],
        out_shape=list(out_shape) + [jax.ShapeDtypeStruct((8, R, LANES), src.dtype)],
        scratch_shapes=list(scratch_shapes) + [pltpu.SemaphoreType.DMA((7,)), pltpu.SemaphoreType.DMA((7,)),
                                               pltpu.SemaphoreType.DMA],
        compiler_params=_cparams("arbitrary", "arbitrary"))(*args, src)


def mla_fwd(q, k, v, *, name, exchange=None):
    H, dk, T = q.shape
    dv = v.shape[1] // H
    tq, tk = _attn_tiles(T, MLA_KEYS)

    def body(q_ref, k_ref, v_ref, o_ref, l_ref, acc, m_s, l_s):
        acc[...] = jnp.zeros_like(acc)
        m_s[...] = jnp.full_like(m_s, NEG)
        l_s[...] = jnp.zeros_like(l_s)

        def block(k0, r0, masked):
            kb = k_ref[pl.ds(k0, tk), :]
            vb = v_ref[pl.ds(k0, tk), :]
            s = jnp.dot(kb, q_ref[:, r0:], preferred_element_type=F32)
            if masked:
                s = jnp.where(_after_diag(tk, tq - r0, False), s, NEG)
            m = m_s[:, r0:]
            m_new = jnp.maximum(m, jnp.max(s, axis=0, keepdims=True))
            p = jnp.exp(s - m_new)
            alpha = jnp.exp(m - m_new)
            l_s[:, r0:] = alpha * l_s[:, r0:] + jnp.sum(p, axis=0, keepdims=True)
            acc[:, r0:] = alpha * acc[:, r0:] + lax.dot_general(vb, p.astype(BF16), TN, preferred_element_type=F32)
            m_s[:, r0:] = m_new

        _causal_sweep(pl.program_id(1), tq, tk, block, descending=False)
        o_ref[...] = acc[...] / l_s[...]
        l_ref[...] = m_s[...] + jnp.log(l_s[...])

    return _call_with_exchange(
        body, exchange, name=name, grid=(H, T // tq),
        in_specs=[pl.BlockSpec((None, dk, tq), lambda h, i: (h, 0, i)),
                  pl.BlockSpec((T, dk), lambda h, i: (0, h)),
                  pl.BlockSpec((T, dv), lambda h, i: (0, h))],
        out_specs=[pl.BlockSpec((None, dv, tq), lambda h, i: (h, 0, i)),
                   pl.BlockSpec((None, 1, tq), lambda h, i: (h, 0, i))],
        out_shape=[jax.ShapeDtypeStruct((H, dv, T), F32), jax.ShapeDtypeStruct((H, 1, T), F32)],
        scratch_shapes=[pltpu.VMEM((dv, tq), F32), pltpu.VMEM((1, tq), F32), pltpu.VMEM((1, tq), F32)],
        args=(q, k, v))


def mla_bwd(q, k, v, do, o, lse, *, name, exchange=None):
    H, dk, T = q.shape
    dv = v.shape[1] // H
    tq, tk = _attn_tiles(T, MLA_KEYS)

    def body(q_ref, k_ref, v_ref, do_ref, o_ref, l_ref, dq_ref, dk_ref, dv_ref, dq, delta, dob):
        @pl.when(pl.program_id(1) == 0)
        def _():
            dk_ref[...] = jnp.zeros_like(dk_ref)
            dv_ref[...] = jnp.zeros_like(dv_ref)

        dq[...] = jnp.zeros_like(dq)
        dov = do_ref[...].astype(F32)
        dob[...] = dov.astype(BF16)
        delta[...] = jnp.sum(dov * o_ref[...], axis=0, keepdims=True)

        def block(k0, r0, masked):
            kb = k_ref[pl.ds(k0, tk), :]
            vb = v_ref[pl.ds(k0, tk), :]
            qv = q_ref[:, r0:]
            dov_b = dob[:, r0:]
            s = jnp.dot(kb, qv, preferred_element_type=F32)
            p = jnp.exp(s - l_ref[:, r0:])
            if masked:
                p = jnp.where(_after_diag(tk, tq - r0, False), p, 0.0)
            dp = jnp.dot(vb, dov_b, preferred_element_type=F32)
            ds = (p * (dp - delta[:, r0:])).astype(BF16)
            dq[:, r0:] += lax.dot_general(kb, ds, TN, preferred_element_type=F32)
            dk_ref[pl.ds(k0, tk), :] += lax.dot_general(ds, qv, NT, preferred_element_type=F32)
            dv_ref[pl.ds(k0, tk), :] += lax.dot_general(p.astype(BF16), dov_b, NT, preferred_element_type=F32)

        _causal_sweep(pl.program_id(1), tq, tk, block, descending=False)
        dq_ref[...] = dq[...]

    return _call_with_exchange(
        body, exchange, name=name, grid=(H, T // tq),
        in_specs=[pl.BlockSpec((None, dk, tq), lambda h, i: (h, 0, i)),
                  pl.BlockSpec((T, dk), lambda h, i: (0, h)),
                  pl.BlockSpec((T, dv), lambda h, i: (0, h)),
                  pl.BlockSpec((None, dv, tq), lambda h, i: (h, 0, i)),
                  pl.BlockSpec((None, dv, tq), lambda h, i: (h, 0, i)),
                  pl.BlockSpec((None, 1, tq), lambda h, i: (h, 0, i))],
        out_specs=[pl.BlockSpec((None, dk, tq), lambda h, i: (h, 0, i)),
                   pl.BlockSpec((T, dk), lambda h, i: (0, h)),
                   pl.BlockSpec((T, dv), lambda h, i: (0, h))],
        out_shape=[jax.ShapeDtypeStruct((H, dk, T), F32), jax.ShapeDtypeStruct((T, H * dk), F32),
                   jax.ShapeDtypeStruct((T, H * dv), F32)],
        scratch_shapes=[pltpu.VMEM((dk, tq), F32), pltpu.VMEM((1, tq), F32), pltpu.VMEM((dv, tq), BF16)],
        args=(q, k, v, do, o, lse))


def _lane_pick(x, h):
    lane = lax.broadcasted_iota(jnp.int32, (1, x.shape[1]), 1)
    return jnp.sum(jnp.where(lane == h, x, 0.0), axis=1, keepdims=True)


def _row_pick(x, h):
    sub = lax.broadcasted_iota(jnp.int32, (x.shape[0], 1), 0)
    return jnp.sum(jnp.where(sub == h, x, 0.0), axis=0, keepdims=True)


def ssd_chunk_fn(*args):
    nh, ng = SSM_HEADS, SSM_GROUPS
    xs = args[:nh]
    bs = args[nh:nh + ng]
    cs = args[nh + ng:nh + 2 * ng]
    dt_raw = args[nh + 2 * ng]
    st = args[nh + 2 * ng + 1:nh + 2 * ng + 1 + nh]
    dt_bias, a_log, d_skip = args[nh + 2 * ng + 1 + nh:]
    L = dt_raw.shape[0]
    dt = _softplus(dt_raw + dt_bias)
    da = dt * (-jnp.exp(a_log))
    dcs = csum_rows(da)
    dcs_t = dcs.T
    total = jnp.sum(da, axis=0, keepdims=True)
    causal = lax.broadcasted_iota(jnp.int32, (L, L), 0) >= lax.broadcasted_iota(jnp.int32, (L, L), 1)
    cb = [mm_nt(cs[g], bs[g]) for g in range(ng)]
    ys, new_st = [], []
    for h in range(nh):
        g = h // (nh // ng)
        dcs_h = _lane_pick(dcs, h)
        dt_h = _lane_pick(dt, h)
        tot_h = _lane_pick(total, h)
        dsk_h = _lane_pick(d_skip, h)
        decay = jnp.exp(jnp.where(causal, dcs_h - _row_pick(dcs_t, h), NEG))
        xdt = xs[h] * dt_h
        y = mm_nn(cb[g] * decay, xdt)
        y = y + mm_nn(cs[g] * jnp.exp(dcs_h), st[h])
        ys.append(y + xs[h] * dsk_h)
        new_st.append(st[h] * jnp.exp(tot_h) + mm_tn(bs[g] * jnp.exp(tot_h - dcs_h), xdt))
    return tuple(ys) + tuple(new_st)


def ssd_fwd(x_hm, b_hm, c_hm, proj, dt_bias, a_log, d_skip, *, name):
    nh, T, P = x_hm.shape
    ng, N = b_hm.shape[0], b_hm.shape[2]
    L = SSM_CHUNK
    nc = T // L
    dtb = OFF_DT // LANES

    def body(x_ref, b_ref, c_ref, dt_ref, db_ref, al_ref, ds_ref, y_ref, s_ref, state):
        @pl.when(pl.program_id(0) == 0)
        def _():
            state[...] = jnp.zeros_like(state)

        s_ref[...] = state[...]
        args = ([x_ref[h] for h in range(nh)] + [b_ref[g] for g in range(ng)] + [c_ref[g] for g in range(ng)]
                + [dt_ref[...]] + [state[h] for h in range(nh)] + [db_ref[...], al_ref[...], ds_ref[...]])
        res = ssd_chunk_fn(*args)
        for h in range(nh):
            y_ref[:, h * P:(h + 1) * P] = res[h]
            state[h] = res[nh + h]

    par = pl.BlockSpec((1, LANES), lambda i: (0, 0))
    return pl.pallas_call(
        body, name=name, grid=(nc,),
        in_specs=[pl.BlockSpec((nh, L, P), lambda i: (0, i, 0)), pl.BlockSpec((ng, L, N), lambda i: (0, i, 0)),
                  pl.BlockSpec((ng, L, N), lambda i: (0, i, 0)), pl.BlockSpec((L, LANES), lambda i: (i, dtb)),
                  par, par, par],
        out_specs=[pl.BlockSpec((L, nh * P), lambda i: (i, 0)),
                   pl.BlockSpec((None, nh, N, P), lambda i: (i, 0, 0, 0))],
        out_shape=[jax.ShapeDtypeStruct((T, nh * P), F32), jax.ShapeDtypeStruct((nc, nh, N, P), F32)],
        scratch_shapes=[pltpu.VMEM((nh, N, P), F32)],
        compiler_params=_cparams("arbitrary"),
    )(x_hm, b_hm, c_hm, proj, dt_bias, a_log, d_skip)


def ssd_bwd(x_hm, b_hm, c_hm, proj, states, dt_bias, a_log, d_skip, dy, *, name):
    nh, T, P = x_hm.shape
    ng, N = b_hm.shape[0], b_hm.shape[2]
    L = SSM_CHUNK
    nc = T // L
    dtb = OFF_DT // LANES

    def body(x_ref, b_ref, c_ref, dt_ref, s_ref, db_ref, al_ref, ds_ref, dy_ref,
             dxbc_ref, ddt_ref, gdb_ref, gal_ref, gds_ref, dstate):
        @pl.when(pl.program_id(0) == 0)
        def _():
            dstate[...] = jnp.zeros_like(dstate)
            gdb_ref[...] = jnp.zeros_like(gdb_ref)
            gal_ref[...] = jnp.zeros_like(gal_ref)
            gds_ref[...] = jnp.zeros_like(gds_ref)

        args = ([x_ref[h] for h in range(nh)] + [b_ref[g] for g in range(ng)] + [c_ref[g] for g in range(ng)]
                + [dt_ref[...]] + [s_ref[h] for h in range(nh)] + [db_ref[...], al_ref[...], ds_ref[...]])
        _, vjp = jax.vjp(ssd_chunk_fn, *args)
        dy = dy_ref[...]
        g = vjp(tuple([dy[:, h * P:(h + 1) * P] for h in range(nh)] + [dstate[h] for h in range(nh)]))
        for j in range(nh + 2 * ng):
            dxbc_ref[:, j * P:(j + 1) * P] = g[j]
        ddt_ref[...] = g[nh + 2 * ng]
        for h in range(nh):
            dstate[h] = g[nh + 2 * ng + 1 + h]
        gdb_ref[...] += g[-3]
        gal_ref[...] += g[-2]
        gds_ref[...] += g[-1]

    rev = lambda i: nc - 1 - i
    par = pl.BlockSpec((1, LANES), lambda i: (0, 0))
    return pl.pallas_call(
        body, name=name, grid=(nc,),
        in_specs=[pl.BlockSpec((nh, L, P), lambda i: (0, rev(i), 0)), pl.BlockSpec((ng, L, N), lambda i: (0, rev(i), 0)),
                  pl.BlockSpec((ng, L, N), lambda i: (0, rev(i), 0)), pl.BlockSpec((L, LANES), lambda i: (rev(i), dtb)),
                  pl.BlockSpec((None, nh, N, P), lambda i: (rev(i), 0, 0, 0)), par, par, par,
                  pl.BlockSpec((L, nh * P), lambda i: (rev(i), 0))],
        out_specs=[pl.BlockSpec((L, (nh + 2 * ng) * P), lambda i: (rev(i), 0)),
                   pl.BlockSpec((L, LANES), lambda i: (rev(i), 0)), par, par, par],
        out_shape=[jax.ShapeDtypeStruct((T, (nh + 2 * ng) * P), F32), jax.ShapeDtypeStruct((T, LANES), F32),
                   jax.ShapeDtypeStruct((1, LANES), F32), jax.ShapeDtypeStruct((1, LANES), F32),
                   jax.ShapeDtypeStruct((1, LANES), F32)],
        scratch_shapes=[pltpu.VMEM((nh, N, P), F32)],
        compiler_params=_cparams("arbitrary"),
    )(x_hm, b_hm, c_hm, proj, states, dt_bias, a_log, d_skip, dy)


def loss_head(h, target, g, *, name, tile=512):
    T, C = h.shape
    tl = min(T, tile)

    def body(h_ref, t_ref, g_ref, dh_ref, dg_ref, ls_ref):
        @pl.when(pl.program_id(0) == 0)
        def _():
            dg_ref[...] = jnp.zeros_like(dg_ref)
            ls_ref[...] = jnp.zeros_like(ls_ref)

        (y,), vjp = jax.vjp(rms_fn, h_ref[...], g_ref[...])
        err = y - t_ref[...]
        ls_ref[...] += jnp.sum(err * err, axis=0, keepdims=True) * (0.5 / C)
        dh, dg = vjp((err * (1.0 / C),))
        dh_ref[...] = dh
        dg_ref[...] += dg

    row = pl.BlockSpec((tl, C), lambda i: (i, 0))
    par = pl.BlockSpec((1, C), lambda i: (0, 0))
    return pl.pallas_call(
        body, name=name, grid=(T // tl,), in_specs=[row, row, par], out_specs=[row, par, par],
        out_shape=[jax.ShapeDtypeStruct((T, C), F32), jax.ShapeDtypeStruct((1, C), F32),
                   jax.ShapeDtypeStruct((1, C), F32)],
        compiler_params=_cparams("arbitrary"),
    )(h, target, g)


def adamw(w, g, m, v, *, name):
    R, C = w.shape
    tr = R
    for d in range(8, min(R, 512) + 1, 8):
        if R % d == 0:
            tr = d
    c1 = 1.0 - ADAM_B1 ** ADAM_STEP
    c2 = 1.0 - ADAM_B2 ** ADAM_STEP

    def body(w_ref, g_ref, m_ref, v_ref, d_ref, nm_ref, nv_ref):
        gv = g_ref[...]
        nm = ADAM_B1 * m_ref[...] + (1.0 - ADAM_B1) * gv
        nv = ADAM_B2 * v_ref[...] + (1.0 - ADAM_B2) * (gv * gv)
        d_ref[...] = -ADAM_LR * ((nm / c1) / (jnp.sqrt(nv / c2) + ADAM_EPS) + ADAM_WD * w_ref[...])
        nm_ref[...] = nm
        nv_ref[...] = nv

    spec = pl.BlockSpec((tr, C), lambda i: (i, 0))
    return pl.pallas_call(
        body, name=name, grid=(R // tr,), in_specs=[spec] * 4, out_specs=[spec] * 3,
        out_shape=[jax.ShapeDtypeStruct((R, C), F32)] * 3,
        compiler_params=_cparams("parallel"),
    )(w, g, m, v)


MESH = pl.DeviceIdType.MESH
HBM_SPEC = pl.BlockSpec(memory_space=pltpu.HBM)


def _place():
    return lax.axis_index("x"), lax.axis_index("y"), lax.axis_index("c")


def _direct_exchange(kind, src_ref, out_ref, send_sems, recv_sems, local_sem):
    x, y, c = _place()
    me = 4 * x + 2 * y + c

    def block_for(dest):
        return src_ref if kind == "gather" else src_ref.at[dest]

    own = pltpu.make_async_copy(block_for(me), out_ref.at[me], local_sem)
    sends, arrivals = [], []
    for f in range(1, 8):
        px = jnp.where((f >> 2) & 1, 1 - x, x)
        py = jnp.where((f >> 1) & 1, 1 - y, y)
        pc = jnp.where(f & 1, 1 - c, c)
        peer = 4 * px + 2 * py + pc
        for dst, group in ((out_ref.at[me], sends), (out_ref.at[peer], arrivals)):
            group.append(pltpu.make_async_remote_copy(
                src_ref=block_for(peer), dst_ref=dst, send_sem=send_sems.at[f - 1], recv_sem=recv_sems.at[f - 1],
                device_id=(px, py, pc), device_id_type=MESH))

    def start():
        own.start()
        for cp in sends:
            cp.start()

    def finish():
        for cp in arrivals:
            cp.wait_recv()
        for cp in sends:
            cp.wait_send()
        own.wait()

    return start, finish


def allgather_blocks(mine, *, name):
    R = mine.shape[0]

    def body(x_ref, out_ref, send_sems, recv_sems, local_sem):
        x, y, c = _place()
        me, sibling = (x, y, c), (x, y, 1 - c)
        chips = [(1 - x, y), (x, 1 - y), (1 - x, 1 - y)]

        def slot(px, py, pc):
            return out_ref.at[4 * px + 2 * py + pc]

        def copy(k, block, to, src=None):
            return pltpu.make_async_remote_copy(
                src_ref=slot(*block) if src is None else src, dst_ref=slot(*block),
                send_sem=send_sems.at[k], recv_sem=recv_sems.at[k], device_id=to, device_id_type=MESH)

        own = pltpu.make_async_copy(x_ref, slot(*me), local_sem)
        own.start()
        first = [copy(0, me, sibling, src=x_ref)]
        first += [copy(1 + j, me, (*chip, c), src=x_ref) for j, chip in enumerate(chips)]
        for cp in first:
            cp.start()
        passed = [copy(4 + j, (*chip, c), sibling) for j, chip in enumerate(chips)]
        for j, chip in enumerate(chips):
            copy(1 + j, (*chip, c), me).wait_recv()
            passed[j].start()
        copy(0, sibling, me).wait_recv()
        for j, chip in enumerate(chips):
            copy(4 + j, (*chip, 1 - c), me).wait_recv()
        for cp in first + passed:
            cp.wait_send()
        own.wait()

    return pl.pallas_call(
        body, name=name, out_shape=jax.ShapeDtypeStruct((8, R, LANES), mine.dtype),
        in_specs=[HBM_SPEC], out_specs=HBM_SPEC,
        scratch_shapes=[pltpu.SemaphoreType.DMA((7,)), pltpu.SemaphoreType.DMA((7,)), pltpu.SemaphoreType.DMA],
    )(mine)


def allgather_direct(mine, *, name):
    R = mine.shape[0]

    def body(x_ref, out_ref, send_sems, recv_sems, local_sem):
        start, finish = _direct_exchange("gather", x_ref, out_ref, send_sems, recv_sems, local_sem)
        start()
        finish()

    return pl.pallas_call(
        body, name=name, out_shape=jax.ShapeDtypeStruct((8, R, LANES), mine.dtype),
        in_specs=[HBM_SPEC], out_specs=HBM_SPEC,
        scratch_shapes=[pltpu.SemaphoreType.DMA((7,)), pltpu.SemaphoreType.DMA((7,)), pltpu.SemaphoreType.DMA],
    )(mine)


def send_to_sibling(v, *, name):
    def body(v_ref, out_ref, send_sem, recv_sem):
        x, y, c = _place()
        cp = pltpu.make_async_remote_copy(src_ref=v_ref, dst_ref=out_ref, send_sem=send_sem, recv_sem=recv_sem,
                                          device_id=(x, y, 1 - c), device_id_type=MESH)
        cp.start()
        cp.wait()

    return pl.pallas_call(
        body, name=name, out_shape=jax.ShapeDtypeStruct(v.shape, v.dtype), in_specs=[HBM_SPEC], out_specs=HBM_SPEC,
        scratch_shapes=[pltpu.SemaphoreType.DMA, pltpu.SemaphoreType.DMA],
    )(v)


def chip_exchange(p, *, name):
    R = p.shape[1]

    def body(p_ref, out_ref, send_sems, recv_sems):
        x, y, c = _place()
        chips = [(1 - x, y), (x, 1 - y), (1 - x, 1 - y)]
        sends = [pltpu.make_async_remote_copy(
            src_ref=p_ref.at[2 * px + py], dst_ref=out_ref.at[j], send_sem=send_sems.at[j], recv_sem=recv_sems.at[j],
            device_id=(px, py, c), device_id_type=MESH) for j, (px, py) in enumerate(chips)]
        for cp in sends:
            cp.start()
        for cp in sends:
            cp.wait()

    return pl.pallas_call(
        body, name=name, out_shape=jax.ShapeDtypeStruct((3, R, LANES), p.dtype), in_specs=[HBM_SPEC],
        out_specs=HBM_SPEC,
        scratch_shapes=[pltpu.SemaphoreType.DMA((3,)), pltpu.SemaphoreType.DMA((3,))],
    )(p)


def add_blocks(terms, out_dtype, *, name, tile=1024):
    terms = [t if isinstance(t, tuple) else (t, None) for t in terms]
    R = terms[0][0].shape[-2]
    tr = R
    for d in range(16, min(R, tile) + 1, 16):
        if R % d == 0:
            tr = d

    def body(*refs):
        acc = refs[0][...].astype(F32)
        for ref in refs[1:-1]:
            acc = acc + ref[...].astype(F32)
        refs[-1][...] = acc.astype(out_dtype)

    spec = pl.BlockSpec((tr, LANES), lambda i: (i, 0))
    in_specs = [spec if slot is None else pl.BlockSpec((None, tr, LANES), lambda i, slot=slot: (slot, i, 0))
                for _, slot in terms]
    return pl.pallas_call(
        body, name=name, grid=(R // tr,), in_specs=in_specs, out_specs=spec,
        out_shape=jax.ShapeDtypeStruct((R, LANES), out_dtype), compiler_params=_cparams("parallel"),
    )(*[a for a, _ in terms])


FLAT_ROW_STEP = 640


def _half_rows(arr, cc):
    hr = arr.shape[0] // 2
    return lax.dynamic_slice_in_dim(arr, cc * hr, hr, axis=0).reshape(-1)


def _flat_half(shards, cc, dtype):
    flat = jnp.concatenate([_half_rows(shards[n], cc).astype(dtype) for n in BIG])
    rows = -(-flat.shape[0] // (FLAT_ROW_STEP * LANES)) * FLAT_ROW_STEP
    return jnp.pad(flat, (0, rows * LANES - flat.shape[0])).reshape(rows, LANES)


def _flat_rows(shapes):
    n = sum((R // 2) * C for R, C in shapes.values()) // LANES
    return -(-n // FLAT_ROW_STEP) * FLAT_ROW_STEP


def _to_blocks(full, shapes, dtype):
    pieces = []
    for n in BIG:
        R, C = shapes[n]
        a = full[n].astype(dtype)
        if BIG_AXIS[n] == 2:
            a = a.reshape(2, R // 2, 4, C).transpose(2, 0, 1, 3)
        pieces.append(a.reshape(8, (R // 2) * C // LANES, LANES))
    flat = jnp.concatenate(pieces, axis=1)
    return jnp.pad(flat, ((0, 0), (0, _flat_rows(shapes) - flat.shape[1]), (0, 0)))


def _from_blocks(g8, shapes):
    out, off = {}, 0
    for n in BIG:
        R, C = shapes[n]
        rows = (R // 2) * C // LANES
        a = g8[:, off:off + rows, :].reshape(4, 2, R // 2, C)
        out[n] = a.transpose(1, 2, 0, 3).reshape(R, 4 * C) if BIG_AXIS[n] == 2 else a.reshape(4 * R, C)
        off += rows
    return out


def _unflat_halves(flat_by_c, shapes):
    out, off = {}, 0
    for n in BIG:
        R, C = shapes[n]
        sz = (R // 2) * C
        out[n] = jnp.concatenate([flat_by_c[c][off:off + sz].reshape(R // 2, C) for c in range(2)], axis=0)
        off += sz
    return out


def _to_heads(a, nh):
    T = a.shape[0]
    return a.reshape(T, nh, a.shape[1] // nh).transpose(1, 0, 2)


def _from_heads(a):
    nh, T, d = a.shape
    return a.transpose(1, 0, 2).reshape(T, nh * d)


def _to_heads_t(a, nh):
    T = a.shape[0]
    return a.reshape(T, nh, a.shape[1] // nh).transpose(1, 2, 0)


def _from_heads_t(a):
    nh, d, T = a.shape
    return a.transpose(2, 0, 1).reshape(T, nh * d)


def _pad_cols(a, n):
    return jnp.pad(a, ((0, 0), (0, n - a.shape[1])))


def _pack_w_in(w):
    offs = [sum(IN_SPLITS[:i]) for i in range(len(IN_SPLITS) + 1)]
    sb, z, xbc, dt, cq, ckv, kr = [w[:, offs[i]:offs[i + 1]] for i in range(len(IN_SPLITS))]
    zeros = lambda n: jnp.zeros((w.shape[0], n), w.dtype)
    h = MLA_ROPE // 2
    kra = jnp.concatenate([zeros(MLA_NOPE), kr, zeros(LANES - MLA_QK)], axis=1)
    krb = jnp.concatenate([zeros(MLA_NOPE), -kr[:, h:], kr[:, :h], zeros(LANES - MLA_QK)], axis=1)
    return sb, jnp.concatenate([z, xbc, cq, ckv, _pad_cols(dt, LANES), kra, krb], axis=1)


def _unpack_gw_in(g_sb, g):
    h = MLA_ROPE // 2
    ga, gb = g[:, OFF_KRA:OFF_KRA + LANES], g[:, OFF_KRB:OFF_KRB + LANES]
    gkr = ga[:, MLA_NOPE:MLA_QK] + jnp.concatenate([gb[:, MLA_NOPE + h:MLA_QK], -gb[:, MLA_NOPE:MLA_NOPE + h]], axis=1)
    return jnp.concatenate([g_sb, g[:, OFF_Z:OFF_Z + 512], g[:, OFF_XBC:OFF_XBC + 768],
                            g[:, OFF_DT:OFF_DT + 8], g[:, OFF_CQ:OFF_CQ + 256], g[:, OFF_CKV:OFF_CKV + 128], gkr], axis=1)


def _pack_w_uq(w):
    zeros = lambda n: jnp.zeros((w.shape[0], n), w.dtype)
    h = MLA_ROPE // 2
    pp, rr = [], []
    for i in range(MLA_HEADS):
        nope = w[:, MLA_QK * i:MLA_QK * i + MLA_NOPE]
        rope = w[:, MLA_QK * i + MLA_NOPE:MLA_QK * (i + 1)]
        pp += [nope, rope, zeros(LANES - MLA_QK)]
        rr += [zeros(MLA_NOPE), -rope[:, h:], rope[:, :h], zeros(LANES - MLA_QK)]
    return jnp.concatenate(pp, axis=1), jnp.concatenate(rr, axis=1)


def _unpack_gw_uq(gp, gr):
    h = MLA_ROPE // 2
    out = []
    for i in range(MLA_HEADS):
        b = LANES * i
        out.append(gp[:, b:b + MLA_NOPE])
        out.append(gp[:, b + MLA_NOPE:b + MLA_NOPE + h] + gr[:, b + MLA_NOPE + h:b + MLA_QK])
        out.append(gp[:, b + MLA_NOPE + h:b + MLA_QK] - gr[:, b + MLA_NOPE:b + MLA_NOPE + h])
    return jnp.concatenate(out, axis=1)


def _pack_w_ukv(w):
    zeros = lambda n: jnp.zeros((w.shape[0], n), w.dtype)
    kk, vv = [], []
    for i in range(MLA_HEADS):
        b = (MLA_NOPE + MLA_V) * i
        kk += [w[:, b:b + MLA_NOPE], zeros(LANES - MLA_NOPE)]
        vv += [w[:, b + MLA_NOPE:b + MLA_NOPE + MLA_V], zeros(LANES - MLA_V)]
    return jnp.concatenate(kk, axis=1), jnp.concatenate(vv, axis=1)


def _unpack_gw_ukv(gk, gv):
    out = []
    for i in range(MLA_HEADS):
        out += [gk[:, LANES * i:LANES * i + MLA_NOPE], gv[:, LANES * i:LANES * i + MLA_V]]
    return jnp.concatenate(out, axis=1)


def _rope_tables(positions):
    inv_freq = 1.0 / (ROPE_THETA ** (jnp.arange(0, MLA_ROPE, 2, dtype=F32) / MLA_ROPE))
    ang = positions.astype(F32)[:, None] * inv_freq
    cos, sin = jnp.cos(ang), jnp.sin(ang)
    T = positions.shape[0]
    one, zero = jnp.ones((T, MLA_NOPE), F32), jnp.zeros((T, MLA_NOPE), F32)
    pad1, pad0 = jnp.ones((T, LANES - MLA_QK), F32), jnp.zeros((T, LANES - MLA_QK), F32)
    return jnp.concatenate([one, cos, cos, pad1], axis=1), jnp.concatenate([zero, sin, sin, pad0], axis=1)


def _row(v):
    return v.reshape(1, -1)


def _pad_row(v):
    return _pad_cols(v.reshape(1, -1), LANES)


def _layer_weights(full, small, li):
    p = {}
    p["w_sb"], p["w_rest"] = _pack_w_in(full["w_in"])
    q_scale = jnp.concatenate([jnp.full((1, SB_HEADS * SB_DIM), SB_DIM ** -0.5, BF16),
                               jnp.ones((1, 2 * SB_HEADS * SB_DIM), BF16)], axis=1)
    p["w_sb_fwd"] = p["w_sb"] * q_scale
    p["wqp"], p["wqr"] = _pack_w_uq(full["mla_w_uq"])
    p["wkp"], p["wvp"] = _pack_w_ukv(full["mla_w_ukv"])
    p["w_out"] = full["w_out"]
    p["w_up"] = full["ffn_w_up"]
    p["w_down"] = full["ffn_w_down"]
    for n in ("mix_norm", "sb_out_norm", "ssm_conv_b", "ssm_out_norm", "mla_q_norm", "mla_kv_norm", "mla_out_norm",
              "ffn_norm", "ffn_conv_b"):
        p[n] = _row(small[n][li])
    for n in ("ssm_dt_bias", "ssm_a_log", "ssm_d"):
        p[n] = _pad_row(small[n][li])
    p["ssm_conv_w"] = small["ssm_conv_w"][li]
    p["ffn_conv_w"] = small["ffn_conv_w"][li]
    return p


def _layer_fwd(h, p, cos, sin, li, exchange=None):
    T = h.shape[0]
    nm = lambda s: "l%d_%s" % (li, s)
    s = {"h": h}
    (n1,) = rowwise(rms_fn, [h], [p["mix_norm"]], [(D_MODEL, BF16)], name=nm("mix_norm"))
    proj = matmul(n1, p["w_rest"], name=nm("in_proj"))
    qkv = matmul(n1, p["w_sb_fwd"], name=nm("in_proj_sb"), out_dtype=BF16)
    s["n1"], s["proj"] = n1, proj
    s["sb_q"] = _to_heads_t(qkv[:, 0:256], SB_HEADS)
    s["sb_k"] = _to_heads(qkv[:, 256:512], SB_HEADS)
    s["sb_v"] = _to_heads(qkv[:, 512:768], SB_HEADS)
    y_sb_hm, s["sb_bt"], s["sb_first"] = sb_fwd(s["sb_q"], s["sb_k"], s["sb_v"], name=nm("sb_fwd"))
    s["y_sb"] = _from_heads_t(y_sb_hm)
    s["x_hm"], s["b_hm"], s["c_hm"] = ssm_conv_act(proj, p["ssm_conv_w"], p["ssm_conv_b"], name=nm("ssm_conv"))
    s["y_ssm"], s["states"] = ssd_fwd(s["x_hm"], s["b_hm"], s["c_hm"], proj, p["ssm_dt_bias"], p["ssm_a_log"],
                                      p["ssm_d"], name=nm("ssd_fwd"))
    rows = [(proj, 256, OFF_CQ // 256), (proj, 128, OFF_CKV // 128), (proj, 128, OFF_KRA // 128),
            (proj, 128, OFF_KRB // 128), cos, sin]
    qp, kp, vv = rowwise(mla_prep_fn, rows, [p["mla_q_norm"], p["mla_kv_norm"], p["wqp"], p["wqr"], p["wkp"], p["wvp"]],
                         [(512, BF16), (512, BF16), (512, BF16)], name=nm("mla_prep"))
    s["mla_q"], s["mla_k"], s["mla_v"] = _to_heads_t(qp, MLA_HEADS), kp, vv
    s["mla_o"], s["mla_lse"], *rode = mla_fwd(s["mla_q"], kp, vv, name=nm("mla_fwd"), exchange=exchange)
    s["y_mla"] = _from_heads_t(s["mla_o"][:, :MLA_V, :])
    (cat,) = rowwise(merge_fn, [s["y_sb"], s["y_ssm"], (proj, 512, OFF_Z // 512), s["y_mla"]],
                     [p["sb_out_norm"], p["ssm_out_norm"], p["mla_out_norm"]], [(D_MODEL, BF16)], name=nm("merge"),
                     post=lambda a, b, c: (jnp.concatenate([a, b, c], axis=1),))
    s["cat"] = cat
    h1 = matmul(cat, p["w_out"], name=nm("out_proj"), residual=h)
    s["h1"] = h1
    (n2,) = rowwise(rms_fn, [h1], [p["ffn_norm"]], [(D_MODEL, BF16)], name=nm("ffn_norm"))
    up = matmul(n2, p["w_up"], name=nm("ffn_up"))
    act = ffn_act(up, p["ffn_conv_w"], p["ffn_conv_b"], name=nm("ffn_act"))
    s["n2"], s["up"], s["act"] = n2, up, act
    h2 = matmul(act, p["w_down"], name=nm("ffn_down"), residual=h1)
    return h2, s, (rode[0] if rode else None)


def _layer_bwd(dh2, s, p, cos, sin, li, exchange=None):
    nm = lambda t: "l%d_%s" % (li, t)
    g = {}
    proj = s["proj"]
    g["ffn_w_down"] = matmul(s["act"], dh2, name=nm("g_w_down"), ta=True)
    d_act = matmul(dh2, p["w_down"], name=nm("d_act"), out_dtype=BF16, tb=True)
    d_up_g, d_up_v, gwg, gwv, gbg, gbv = ffn_bwd_fused(s["up"], p["ffn_conv_w"], p["ffn_conv_b"], d_act,
                                                       name=nm("ffn_act_bwd"))
    g["ffn_conv_w"] = jnp.concatenate([gwg, gwv], axis=1)
    g["ffn_conv_b"] = jnp.concatenate([gbg[0], gbv[0]])
    g["ffn_w_up"] = jnp.concatenate([matmul(s["n2"], d_up_g, name=nm("g_w_up_gate"), ta=True),
                                     matmul(s["n2"], d_up_v, name=nm("g_w_up_val"), ta=True)], axis=1)
    d_n2 = matmul(d_up_g, p["w_up"], name=nm("d_n2_gate"), tb=True)
    d_n2 = matmul(d_up_v, p["w_up"], name=nm("d_n2_val"), tb=True, b_k0=D_FF, residual=d_n2)
    (dh1,), (gn,) = rowwise_bwd(rms_fn, [s["h1"]], [], [p["ffn_norm"]], [d_n2], [F32], name=nm("ffn_norm_bwd"),
                                add0=dh2)
    g["ffn_norm"] = gn[0]
    g["w_out"] = matmul(s["cat"], dh1, name=nm("g_w_out"), ta=True)
    d_cat = matmul(dh1, p["w_out"], name=nm("d_cat"), tb=True)
    (d_ysb, d_yssm, d_z, d_ymla), (g1, g2, g3) = rowwise_bwd(
        merge_fn, [s["y_sb"], s["y_ssm"], (proj, 512, OFF_Z // 512), s["y_mla"]], [],
        [p["sb_out_norm"], p["ssm_out_norm"], p["mla_out_norm"]], [d_cat], [F32, F32, BF16, F32], name=nm("merge_bwd"),
        pre_ct=lambda d: (d[:, 0:256], d[:, 256:768], d[:, 768:1024]))
    g["sb_out_norm"], g["ssm_out_norm"], g["mla_out_norm"] = g1[0], g2[0], g3[0]
    dq, dk, dv = sb_bwd(s["sb_q"], s["sb_k"], s["sb_v"], _to_heads_t(d_ysb, SB_HEADS), s["sb_bt"], s["sb_first"], name=nm("sb_bwd"),
                        q_scale=SB_DIM ** -0.5)
    d_sb = jnp.concatenate([_from_heads_t(dq), _from_heads(dk), _from_heads(dv)], axis=1).astype(BF16)
    do_t = jnp.pad(_to_heads_t(d_ymla, MLA_HEADS), ((0, 0), (0, LANES - MLA_V), (0, 0)))
    dqp, dkp, dvv, *rode = mla_bwd(s["mla_q"], s["mla_k"], s["mla_v"], do_t, s["mla_o"], s["mla_lse"],
                                   name=nm("mla_bwd"), exchange=exchange)
    rows = [(proj, 256, OFF_CQ // 256), (proj, 128, OFF_CKV // 128), (proj, 128, OFF_KRA // 128),
            (proj, 128, OFF_KRB // 128)]
    (d_cq, d_ckv, d_kra, d_krb), (gqn, gkvn, gwqp, gwqr, gwkp, gwvp) = rowwise_bwd(
        mla_prep_fn, rows, [cos, sin], [p["mla_q_norm"], p["mla_kv_norm"], p["wqp"], p["wqr"], p["wkp"], p["wvp"]],
        [_from_heads_t(dqp), dkp, dvv], [BF16] * 4, name=nm("mla_prep_bwd"), tile=256)
    g["mla_q_norm"], g["mla_kv_norm"] = gqn[0], gkvn[0]
    g["mla_w_uq"] = _unpack_gw_uq(gwqp, gwqr)
    g["mla_w_ukv"] = _unpack_gw_ukv(gwkp, gwvp)
    d_xbc_act, d_dt, gdb, gal, gds = ssd_bwd(
        s["x_hm"], s["b_hm"], s["c_hm"], proj, s["states"], p["ssm_dt_bias"], p["ssm_a_log"], p["ssm_d"],
        d_yssm, name=nm("ssd_bwd"))
    g["ssm_dt_bias"], g["ssm_a_log"], g["ssm_d"] = gdb[0, :8], gal[0, :8], gds[0, :8]
    d_pre = ssm_conv_bwd_a(proj, p["ssm_conv_w"], p["ssm_conv_b"], d_xbc_act, name=nm("ssm_conv_bwd_a"))
    d_xbc, g["ssm_conv_w"], gscb = conv_bwd_b(d_pre, proj, OFF_XBC, p["ssm_conv_w"], name=nm("ssm_conv_bwd_b"),
                                              out_dtype=BF16, tc=256)
    g["ssm_conv_b"] = gscb[0]
    d_proj = jnp.concatenate([d_z, d_xbc, d_cq, d_ckv, d_dt.astype(BF16), d_kra, d_krb], axis=1)
    g["w_in"] = _unpack_gw_in(matmul(s["n1"], d_sb, name=nm("g_w_in_sb"), ta=True),
                              matmul(s["n1"], d_proj, name=nm("g_w_in"), ta=True))
    d_n1 = matmul(d_sb, p["w_sb"], name=nm("d_n1_sb"), tb=True)
    d_n1 = matmul(d_proj, p["w_rest"], name=nm("d_n1"), tb=True, residual=d_n1)
    (dh0,), (gm,) = rowwise_bwd(rms_fn, [s["h"]], [], [p["mix_norm"]], [d_n1], [F32], name=nm("mix_norm_bwd"),
                                add0=dh1)
    g["mix_norm"] = gm[0]
    return dh0, g, (rode[0] if rode else None)


def kernel(x, positions, mix_norm, w_in, sb_out_norm, ssm_conv_w, ssm_conv_b, ssm_dt_bias, ssm_a_log, ssm_d, ssm_out_norm, mla_q_norm, mla_w_uq, mla_kv_norm, mla_w_ukv, mla_out_norm, w_out, ffn_norm, ffn_w_up, ffn_conv_w, ffn_conv_b, ffn_w_down, final_norm, loss_target, m_mix_norm, m_w_in, m_sb_out_norm, m_ssm_conv_w, m_ssm_conv_b, m_ssm_dt_bias, m_ssm_a_log, m_ssm_d, m_ssm_out_norm, m_mla_q_norm, m_mla_w_uq, m_mla_kv_norm, m_mla_w_ukv, m_mla_out_norm, m_w_out, m_ffn_norm, m_ffn_w_up, m_ffn_conv_w, m_ffn_conv_b, m_ffn_w_down, m_final_norm, v_mix_norm, v_w_in, v_sb_out_norm, v_ssm_conv_w, v_ssm_conv_b, v_ssm_dt_bias, v_ssm_a_log, v_ssm_d, v_ssm_out_norm, v_mla_q_norm, v_mla_w_uq, v_mla_kv_norm, v_mla_w_ukv, v_mla_out_norm, v_w_out, v_ffn_norm, v_ffn_w_up, v_ffn_conv_w, v_ffn_conv_b, v_ffn_w_down, v_final_norm):
    W = dict(mix_norm=mix_norm, w_in=w_in, sb_out_norm=sb_out_norm, ssm_conv_w=ssm_conv_w, ssm_conv_b=ssm_conv_b,
             ssm_dt_bias=ssm_dt_bias, ssm_a_log=ssm_a_log, ssm_d=ssm_d, ssm_out_norm=ssm_out_norm,
             mla_q_norm=mla_q_norm, mla_w_uq=mla_w_uq, mla_kv_norm=mla_kv_norm, mla_w_ukv=mla_w_ukv,
             mla_out_norm=mla_out_norm, w_out=w_out, ffn_norm=ffn_norm, ffn_w_up=ffn_w_up, ffn_conv_w=ffn_conv_w,
             ffn_conv_b=ffn_conv_b, ffn_w_down=ffn_w_down, final_norm=final_norm)
    M = dict(mix_norm=m_mix_norm, w_in=m_w_in, sb_out_norm=m_sb_out_norm, ssm_conv_w=m_ssm_conv_w,
             ssm_conv_b=m_ssm_conv_b, ssm_dt_bias=m_ssm_dt_bias, ssm_a_log=m_ssm_a_log, ssm_d=m_ssm_d,
             ssm_out_norm=m_ssm_out_norm, mla_q_norm=m_mla_q_norm, mla_w_uq=m_mla_w_uq, mla_kv_norm=m_mla_kv_norm,
             mla_w_ukv=m_mla_w_ukv, mla_out_norm=m_mla_out_norm, w_out=m_w_out, ffn_norm=m_ffn_norm,
             ffn_w_up=m_ffn_w_up, ffn_conv_w=m_ffn_conv_w, ffn_conv_b=m_ffn_conv_b, ffn_w_down=m_ffn_w_down,
             final_norm=m_final_norm)
    V = dict(mix_norm=v_mix_norm, w_in=v_w_in, sb_out_norm=v_sb_out_norm, ssm_conv_w=v_ssm_conv_w,
             ssm_conv_b=v_ssm_conv_b, ssm_dt_bias=v_ssm_dt_bias, ssm_a_log=v_ssm_a_log, ssm_d=v_ssm_d,
             ssm_out_norm=v_ssm_out_norm, mla_q_norm=v_mla_q_norm, mla_w_uq=v_mla_w_uq, mla_kv_norm=v_mla_kv_norm,
             mla_w_ukv=v_mla_w_ukv, mla_out_norm=v_mla_out_norm, w_out=v_w_out, ffn_norm=v_ffn_norm,
             ffn_w_up=v_ffn_w_up, ffn_conv_w=v_ffn_conv_w, ffn_conv_b=v_ffn_conv_b, ffn_w_down=v_ffn_w_down,
             final_norm=v_final_norm)
    depth = mix_norm.shape[0]
    cx, cy, cc = _place()
    chip = 2 * cx + cy
    T = x.shape[1]

    assert depth == 2
    shard_shapes = {n: W[n].shape[1:] for n in BIG}

    def layer_of(d, li):
        return {n: d[n][li] for n in BIG}

    def assemble(g8):
        return _from_blocks(g8, shard_shapes)

    full0 = assemble(allgather_blocks(_flat_half(layer_of(W, 0), cc, BF16), name="gather_weights_l0"))
    conv_full = {}
    small = {n: W[n] for n in SMALL_REPL}
    cw_flat = jnp.concatenate([W[n].reshape(-1) for n in SMALL_SHARD])
    cw_rows = -(-cw_flat.shape[0] // (8 * LANES)) * 8
    cw_all = allgather_direct(jnp.pad(cw_flat, (0, cw_rows * LANES - cw_flat.shape[0])).reshape(cw_rows, LANES),
                              name="gather_conv_taps")
    off = 0
    for n in SMALL_SHARD:
        sz = W[n].size
        conv_full[n] = jnp.concatenate(
            [cw_all[2 * k].reshape(-1)[off:off + sz].reshape(W[n].shape) for k in range(4)], axis=2)
        off += sz
    small.update(conv_full)

    cos, sin = _rope_tables(positions[0])
    params0 = _layer_weights(full0, small, 0)
    h, s0, g8 = _layer_fwd(x[0], params0, cos, sin, 0, exchange=("gather", _flat_half(layer_of(W, 1), cc, BF16)))
    params1 = _layer_weights(assemble(g8), small, 1)
    h, s1, _ = _layer_fwd(h, params1, cos, sin, 1)
    dh, g_final, loss_lanes = loss_head(h, loss_target[0], _row(final_norm), name="loss_head")

    dh, g1, _ = _layer_bwd(dh, s1, params1, cos, sin, 1)
    by_dest = _to_blocks(g1, shard_shapes, BF16)
    dh, g0, from_all = _layer_bwd(dh, s0, params0, cos, sin, 0, exchange=("all_to_all", by_dest))
    grad_x = dh[None]
    grads = [g0, g1]
    G = {n: jnp.stack([grads[li][n] for li in range(depth)]) for n in WEIGHTS if n != "final_norm" and n not in BIG}
    G["final_norm"] = g_final[0]
    half1 = add_blocks([(from_all, d) for d in range(8)], F32, name="grads_l1_sum")

    blocks0 = _to_blocks(g0, shard_shapes, BF16)
    R = blocks0.shape[1]
    blocks0 = blocks0.reshape(4, 2, R, LANES)
    mine_first = lax.dynamic_index_in_dim(blocks0, cc, 1, keepdims=False)
    for_sibling = lax.dynamic_index_in_dim(blocks0, 1 - cc, 1, keepdims=False)
    from_sibling = send_to_sibling(for_sibling.reshape(4 * R, LANES), name="grads_to_sibling")
    pair = add_blocks([mine_first.reshape(4 * R, LANES), from_sibling], BF16, name="grads_pair_sum").reshape(4, R, LANES)
    others = chip_exchange(pair, name="grads_chip_exchange")
    own = lax.dynamic_index_in_dim(pair, chip, 0, keepdims=False)
    half0 = add_blocks([own, (others, 0), (others, 1), (others, 2)], F32, name="grads_chip_sum")
    half = jnp.concatenate([half0, half1])
    other = send_to_sibling(half, name="grads_pair_swap")
    by_core = [jnp.where(cc == 0, half, other), jnp.where(cc == 0, other, half)]
    g_big_l = [_unflat_halves([a[li * R:(li + 1) * R].reshape(-1) for a in by_core], shard_shapes) for li in range(depth)]
    g_big = {n: jnp.stack([g_big_l[li][n] for li in range(depth)]) for n in BIG}

    small_list = [G[n].reshape(-1) for n in SMALL_REPL] + [G[n].reshape(-1) for n in SMALL_SHARD]
    small_list.append(jnp.sum(loss_lanes).reshape(1))
    sm = jnp.concatenate(small_list)
    n_small = sm.shape[0]
    sm_rows = -(-n_small // (16 * LANES)) * 16
    sm_all = allgather_direct(jnp.pad(sm, (0, sm_rows * LANES - n_small)).reshape(sm_rows, LANES), name="gather_small")
    sm_sum = add_blocks([(sm_all, d) for d in range(8)], F32, name="small_sum").reshape(-1)
    g_small, off = {}, 0
    for n in SMALL_REPL:
        g_small[n] = sm_sum[off:off + W[n].size].reshape(W[n].shape)
        off += W[n].size
    for n in SMALL_SHARD:
        full_shape = conv_full[n].shape
        sz = conv_full[n].size
        gfull = sm_sum[off:off + sz].reshape(full_shape)
        width = W[n].shape[2]
        g_small[n] = lax.dynamic_slice_in_dim(gfull, chip * width, width, axis=2)
        off += sz
    loss = sm_sum[off]

    grad_out, delta, new_m, new_v = {}, {}, {}, {}
    for n in BIG:
        shp = W[n].shape
        two_d = lambda a: a.reshape(shp[0] * shp[1], shp[2])
        d, nm_, nv_ = adamw(two_d(W[n]), two_d(g_big[n]), two_d(M[n]), two_d(V[n]), name="adamw_" + n)
        grad_out[n], delta[n], new_m[n], new_v[n] = g_big[n], d.reshape(shp), nm_.reshape(shp), nv_.reshape(shp)
    small_names = SMALL_REPL + SMALL_SHARD

    def flat_small(d):
        f = jnp.concatenate([d[n].reshape(-1) for n in small_names])
        rows = -(-f.shape[0] // (8 * LANES)) * 8
        return jnp.pad(f, (0, rows * LANES - f.shape[0])).reshape(rows, LANES)

    vpad = flat_small(V)
    d, nm_, nv_ = adamw(flat_small(W), flat_small(g_small), flat_small(M), vpad, name="adamw_small")
    off = 0
    for n in small_names:
        sz = W[n].size
        grad_out[n] = g_small[n]
        delta[n] = d.reshape(-1)[off:off + sz].reshape(W[n].shape)
        new_m[n] = nm_.reshape(-1)[off:off + sz].reshape(W[n].shape)
        new_v[n] = nv_.reshape(-1)[off:off + sz].reshape(W[n].shape)
        off += sz

    return (loss, grad_x, *[grad_out[n] for n in WEIGHTS], *[delta[n] for n in WEIGHTS],
            *[new_m[n] for n in WEIGHTS], *[new_v[n] for n in WEIGHTS])
```

```python
import functools
import math

import jax
import jax.numpy as jnp
from jax import lax
from jax.experimental import pallas as pl
from jax.experimental.pallas import tpu as pltpu

F32 = jnp.float32
BF16 = jnp.bfloat16

EPS = 1e-6
D_MODEL = 1024
SB_HEADS, SB_DIM = 4, 64
SSM_HEADS, SSM_DIM, SSM_GROUPS, SSM_STATE, SSM_CHUNK = 8, 64, 2, 64, 128
SSM_INNER = SSM_HEADS * SSM_DIM
SSM_CONV_DIM = SSM_INNER + 2 * SSM_GROUPS * SSM_STATE
MLA_HEADS, MLA_NOPE, MLA_ROPE, MLA_V = 4, 64, 32, 64
MLA_QK = MLA_NOPE + MLA_ROPE
MLA_SCALE = MLA_QK ** -0.5
ROPE_THETA = 10000.0
D_FF = 2816
IN_SPLITS = (768, 512, 768, 8, 256, 128, 32)

OFF_Z, OFF_XBC, OFF_CQ, OFF_CKV, OFF_DT, OFF_KRA, OFF_KRB = 0, 512, 1280, 1536, 1664, 1792, 1920
D_REST = 2048
LANES = 128

ADAM_LR, ADAM_B1, ADAM_B2, ADAM_EPS, ADAM_WD, ADAM_STEP = 0.001, 0.9, 0.999, 1e-08, 0.01, 10

V7X_VMEM_LIMIT = 48 * 1024 * 1024

NT = (((1,), (1,)), ((), ()))
TN = (((0,), (0,)), ((), ()))

BIG = ("w_in", "mla_w_uq", "mla_w_ukv", "w_out", "ffn_w_up", "ffn_w_down")
BIG_AXIS = {"w_in": 2, "mla_w_uq": 2, "mla_w_ukv": 2, "w_out": 1, "ffn_w_up": 2, "ffn_w_down": 1}
SMALL_REPL = ("mix_norm", "sb_out_norm", "ssm_conv_b", "ssm_dt_bias", "ssm_a_log", "ssm_d", "ssm_out_norm",
              "mla_q_norm", "mla_kv_norm", "mla_out_norm", "ffn_norm", "ffn_conv_b", "final_norm")
SMALL_SHARD = ("ssm_conv_w", "ffn_conv_w")
WEIGHTS = ("mix_norm", "w_in", "sb_out_norm", "ssm_conv_w", "ssm_conv_b", "ssm_dt_bias", "ssm_a_log", "ssm_d",
           "ssm_out_norm", "mla_q_norm", "mla_w_uq", "mla_kv_norm", "mla_w_ukv", "mla_out_norm", "w_out", "ffn_norm",
           "ffn_w_up", "ffn_conv_w", "ffn_conv_b", "ffn_w_down", "final_norm")


def _cparams(*sem):
    return pltpu.CompilerParams(dimension_semantics=sem if sem else None, vmem_limit_bytes=V7X_VMEM_LIMIT)


def _pick(n, target, mult=LANES):
    best = None
    for d in range(mult, min(n, target) + 1, mult):
        if n % d == 0:
            best = d
    return best or n


def _sigmoid(x):
    return 1.0 / (1.0 + jnp.exp(-x))


def _softplus(x):
    ax = jnp.where(x > 0, x, -x)
    return jnp.where(x > 0, x, 0.0) + jnp.log(1.0 + jnp.exp(-ax))


def _rms(x, g):
    return x * lax.rsqrt(jnp.mean(x * x, axis=-1, keepdims=True) + EPS) * g


def _raw_nn(a, b):
    return jnp.dot(a.astype(BF16), b.astype(BF16), preferred_element_type=F32)


def _raw_nt(a, b):
    return lax.dot_general(a.astype(BF16), b.astype(BF16), NT, preferred_element_type=F32)


def _raw_tn(a, b):
    return lax.dot_general(a.astype(BF16), b.astype(BF16), TN, preferred_element_type=F32)


@jax.custom_vjp
def mm_nn(a, b):
    return _raw_nn(a, b)


mm_nn.defvjp(lambda a, b: (_raw_nn(a, b), (a, b)),
             lambda r, ct: (_raw_nt(ct, r[1]), _raw_tn(r[0], ct)))


@jax.custom_vjp
def mm_nt(a, b):
    return _raw_nt(a, b)


mm_nt.defvjp(lambda a, b: (_raw_nt(a, b), (a, b)),
             lambda r, ct: (_raw_nn(ct, r[1]), _raw_tn(ct, r[0])))


@jax.custom_vjp
def mm_tn(a, b):
    return _raw_tn(a, b)


mm_tn.defvjp(lambda a, b: (_raw_tn(a, b), (a, b)),
             lambda r, ct: (_raw_nt(r[1], ct), _raw_nn(r[0], ct)))


def _split_dot(x, m, terms):
    acc = None
    r = x
    for t in range(terms):
        xt = r.astype(BF16)
        d = jnp.dot(xt, m, preferred_element_type=F32)
        acc = d if acc is None else acc + d
        if t + 1 < terms:
            r = r - xt.astype(F32)
    return acc


def _tri_dot(tri, x, terms=3):
    parts = []
    r = x
    for t in range(terms):
        xt = r.astype(BF16)
        parts.append(xt)
        if t + 1 < terms:
            r = r - xt.astype(F32)
    return jnp.dot(jnp.concatenate([tri] * terms, axis=1), jnp.concatenate(parts, axis=0),
                   preferred_element_type=F32)


def _tri(n, cmp):
    r = lax.broadcasted_iota(jnp.int32, (n, n), 0)
    c = lax.broadcasted_iota(jnp.int32, (n, n), 1)
    return cmp(r, c).astype(BF16)


@jax.custom_vjp
def csum_rows(x):
    return _tri_dot(_tri(x.shape[0], lambda r, c: r >= c), x)


csum_rows.defvjp(lambda x: (csum_rows(x), None),
                 lambda _, ct: (_tri_dot(_tri(ct.shape[0], lambda r, c: r <= c), ct),))


def matmul(a, b, *, name, out_dtype=F32, ta=False, tb=False, b_k0=0, residual=None):
    if ta:
        K, M = a.shape
    else:
        M, K = a.shape
    N = b.shape[0] if tb else b.shape[1]
    tm = _pick(M, 1408)
    tn = _pick(N, 1408)
    tk = _pick(K, 1408)
    nk = K // tk
    kb0 = b_k0 // tk
    assert b_k0 % tk == 0 and (tb or b_k0 == 0)
    has_res = residual is not None

    def body(*refs):
        if has_res:
            a_ref, b_ref, r_ref, o_ref, acc = refs
        else:
            a_ref, b_ref, o_ref, acc = refs
        k = pl.program_id(2)

        @pl.when(k == 0)
        def _():
            acc[...] = jnp.zeros_like(acc)

        av = a_ref[...].astype(BF16)
        bv = b_ref[...].astype(BF16)
        if ta:
            acc[...] += lax.dot_general(av, bv, TN, preferred_element_type=F32)
        elif tb:
            acc[...] += lax.dot_general(av, bv, NT, preferred_element_type=F32)
        else:
            acc[...] += jnp.dot(av, bv, preferred_element_type=F32)

        @pl.when(k == nk - 1)
        def _():
            r = acc[...]
            if has_res:
                r = r + r_ref[...].astype(F32)
            o_ref[...] = r.astype(o_ref.dtype)

    a_spec = pl.BlockSpec((tk, tm), lambda i, j, k: (k, i)) if ta else pl.BlockSpec((tm, tk), lambda i, j, k: (i, k))
    b_spec = pl.BlockSpec((tn, tk), lambda i, j, k: (j, kb0 + k)) if tb else pl.BlockSpec((tk, tn), lambda i, j, k: (k, j))
    in_specs = [a_spec, b_spec]
    args = [a, b]
    if has_res:
        in_specs.append(pl.BlockSpec((tm, tn), lambda i, j, k: (i, j)))
        args.append(residual)
    return pl.pallas_call(
        body, name=name, grid=(M // tm, N // tn, nk),
        in_specs=in_specs, out_specs=pl.BlockSpec((tm, tn), lambda i, j, k: (i, j)),
        out_shape=jax.ShapeDtypeStruct((M, N), out_dtype),
        scratch_shapes=[pltpu.VMEM((tm, tn), F32)],
        compiler_params=_cparams("parallel", "parallel", "arbitrary"),
    )(*args)


def _row_spec(entry, tl):
    if isinstance(entry, tuple):
        arr, width, cb = entry
        return arr, pl.BlockSpec((tl, width), lambda i, cb=cb: (i, cb))
    return entry, pl.BlockSpec((tl, entry.shape[1]), lambda i: (i, 0))


def _rows_T(entry):
    return (entry[0] if isinstance(entry, tuple) else entry).shape[0]


def rowwise(fn, rows, params, outs, *, name, tile=512, post=None):
    T = _rows_T(rows[0])
    tl = min(T, tile)
    nr, npar = len(rows), len(params)

    def body(*refs):
        r = [ref[...].astype(F32) for ref in refs[:nr]]
        p = [ref[...].astype(F32) for ref in refs[nr:nr + npar]]
        res = fn(*r, *p)
        if post is not None:
            res = post(*res)
        for o_ref, val in zip(refs[nr + npar:], res):
            o_ref[...] = val.astype(o_ref.dtype)

    arrs, specs = [], []
    for e in rows:
        a, s = _row_spec(e, tl)
        arrs.append(a)
        specs.append(s)
    for p in params:
        arrs.append(p)
        specs.append(pl.BlockSpec(p.shape, lambda i: (0, 0)))
    res = pl.pallas_call(
        body, name=name, grid=(T // tl,), in_specs=specs,
        out_specs=[pl.BlockSpec((tl, c), lambda i: (i, 0)) for c, _ in outs],
        out_shape=[jax.ShapeDtypeStruct((T, c), dt) for c, dt in outs],
        compiler_params=_cparams("parallel"),
    )(*arrs)
    return res


def rowwise_bwd(fn, rows, nd_rows, params, cts, grad_dtypes, *, name, tile=512, pre_ct=None, add0=None):
    T = _rows_T(rows[0])
    tl = min(T, tile)
    nr, nn, npar, nc = len(rows), len(nd_rows), len(params), len(cts)
    has_add = add0 is not None

    def body(*refs):
        pos = 0
        r = [ref[...].astype(F32) for ref in refs[pos:pos + nr]]
        pos += nr
        nd = [ref[...].astype(F32) for ref in refs[pos:pos + nn]]
        pos += nn
        p = [ref[...].astype(F32) for ref in refs[pos:pos + npar]]
        pos += npar
        c = [ref[...].astype(F32) for ref in refs[pos:pos + nc]]
        pos += nc
        if has_add:
            addv = refs[pos][...].astype(F32)
            pos += 1
        rg_refs = refs[pos:pos + nr]
        pg_refs = refs[pos + nr:pos + nr + npar]
        if pre_ct is not None:
            c = list(pre_ct(*c))
        _, vjp = jax.vjp(lambda *a: fn(*a[:nr], *nd, *a[nr:]), *r, *p)
        g = vjp(tuple(c))
        for j, ref in enumerate(rg_refs):
            val = g[j]
            if has_add and j == 0:
                val = val + addv
            ref[...] = val.astype(ref.dtype)
        if npar:
            @pl.when(pl.program_id(0) == 0)
            def _():
                for ref in pg_refs:
                    ref[...] = jnp.zeros_like(ref)
            for j, ref in enumerate(pg_refs):
                ref[...] += g[nr + j]

    arrs, specs = [], []
    widths = []
    for e in list(rows) + list(nd_rows):
        a, s = _row_spec(e, tl)
        arrs.append(a)
        specs.append(s)
        widths.append(s.block_shape[1])
    for p in params:
        arrs.append(p)
        specs.append(pl.BlockSpec(p.shape, lambda i: (0, 0)))
    for e in cts:
        a, s = _row_spec(e, tl)
        arrs.append(a)
        specs.append(s)
    if has_add:
        a, s = _row_spec(add0, tl)
        arrs.append(a)
        specs.append(s)
    out_specs = [pl.BlockSpec((tl, widths[j]), lambda i: (i, 0)) for j in range(nr)]
    out_shape = [jax.ShapeDtypeStruct((T, widths[j]), grad_dtypes[j]) for j in range(nr)]
    out_specs += [pl.BlockSpec(p.shape, lambda i: (0, 0)) for p in params]
    out_shape += [jax.ShapeDtypeStruct(p.shape, F32) for p in params]
    res = pl.pallas_call(
        body, name=name, grid=(T // tl,), in_specs=specs, out_specs=out_specs, out_shape=out_shape,
        compiler_params=_cparams("arbitrary"),
    )(*arrs)
    return list(res[:nr]), list(res[nr:])


def rms_fn(h, g):
    return (_rms(h, g),)


def merge_fn(ysb, yssm, z, ymla, g_sb, g_ssm, g_mla):
    ya = _rms(ysb, g_sb)
    yb = _rms(yssm * (z * _sigmoid(z)), g_ssm)
    yc = _rms(ymla, g_mla)
    return ya, yb, yc


def mla_prep_fn(cq, ckv, kra, krb, cos, sin, qn, kvn, wqp, wqr, wkp, wvp):
    cos4 = jnp.concatenate([cos] * MLA_HEADS, axis=1)
    sin4 = jnp.concatenate([sin] * MLA_HEADS, axis=1)
    nq = _rms(cq, qn)
    q = (mm_nn(nq, wqp) * cos4 + mm_nn(nq, wqr) * sin4) * MLA_SCALE
    nkv = _rms(ckv, kvn)
    kpe = kra * cos + krb * sin
    k = mm_nn(nkv, wkp) + jnp.concatenate([kpe] * MLA_HEADS, axis=1)
    v = mm_nn(nkv, wvp)
    return q, k, v


HALO = 8


def _prev_halo_spec(tl, tc, col_of):
    return pl.BlockSpec((HALO, tc), lambda i, j: (jnp.maximum(i * (tl // HALO) - 1, 0), col_of(j)))


def _fill_prev(buf, x_ref, halo_ref, i):
    buf[0:HALO, :] = jnp.where(i > 0, halo_ref[...].astype(F32), 0.0)
    buf[HALO:, :] = x_ref[...].astype(F32)


def _conv_from(buf, w_ref, b_ref, K, tl):
    acc = b_ref[...].astype(F32) + jnp.zeros((tl, buf.shape[1]), F32)
    for k in range(K):
        acc = acc + buf[pl.ds(HALO - (K - 1 - k), tl), :] * w_ref[k:k + 1, :].astype(F32)
    return acc


def ssm_conv_act(proj, w, b, *, name, tile=512, tc=256):
    T = proj.shape[0]
    K, C = w.shape
    tl = min(T, tile)
    c0 = OFF_XBC // tc
    nb = C // tc
    hd = SSM_DIM
    per = tc // hd

    def body(*refs):
        xs, halos = refs[0:nb], refs[nb:2 * nb]
        w_ref, b_ref = refs[2 * nb:2 * nb + 2]
        x_out, b_out, c_out = refs[2 * nb + 2:2 * nb + 5]
        bufs = refs[2 * nb + 5:]
        for j in range(nb):
            cols = slice(j * tc, (j + 1) * tc)
            _fill_prev(bufs[j], xs[j], halos[j], pl.program_id(0))
            u = b_ref[:, cols].astype(F32) + jnp.zeros((tl, tc), F32)
            for k in range(K):
                u = u + bufs[j][pl.ds(HALO - (K - 1 - k), tl), :] * w_ref[k:k + 1, cols].astype(F32)
            act = u * _sigmoid(u)
            for hh in range(per):
                piece = act[:, hh * hd:(hh + 1) * hd]
                head = j * per + hh
                if head < SSM_HEADS:
                    x_out[head] = piece
                elif head < SSM_HEADS + SSM_GROUPS:
                    b_out[head - SSM_HEADS] = piece
                else:
                    c_out[head - SSM_HEADS - SSM_GROUPS] = piece

    in_specs = ([pl.BlockSpec((tl, tc), lambda i, j=j: (i, c0 + j)) for j in range(nb)]
                + [pl.BlockSpec((HALO, tc), lambda i, j=j: (jnp.maximum(i * (tl // HALO) - 1, 0), c0 + j)) for j in range(nb)]
                + [pl.BlockSpec((K, C), lambda i: (0, 0)), pl.BlockSpec((1, C), lambda i: (0, 0))])
    return pl.pallas_call(
        body, name=name, grid=(T // tl,), in_specs=in_specs,
        out_specs=[pl.BlockSpec((SSM_HEADS, tl, hd), lambda i: (0, i, 0)),
                   pl.BlockSpec((SSM_GROUPS, tl, hd), lambda i: (0, i, 0)),
                   pl.BlockSpec((SSM_GROUPS, tl, hd), lambda i: (0, i, 0))],
        out_shape=[jax.ShapeDtypeStruct((SSM_HEADS, T, hd), F32), jax.ShapeDtypeStruct((SSM_GROUPS, T, hd), F32),
                   jax.ShapeDtypeStruct((SSM_GROUPS, T, hd), F32)],
        scratch_shapes=[pltpu.VMEM((tl + HALO, tc), F32)] * nb,
        compiler_params=_cparams("parallel"),
    )(*([proj] * (2 * nb)), w, b)


def ssm_conv_bwd_a(proj, w, b, d_out, *, name, tile=512, tc=256):
    T = proj.shape[0]
    K, C = w.shape
    tl = min(T, tile)
    c0 = OFF_XBC // tc

    def body(x_ref, halo_ref, w_ref, b_ref, d_ref, o_ref, buf):
        _fill_prev(buf, x_ref, halo_ref, pl.program_id(0))
        u = _conv_from(buf, w_ref, b_ref, K, tl)
        s = _sigmoid(u)
        o_ref[...] = d_ref[...].astype(F32) * (s * (1.0 + u * (1.0 - s)))

    return pl.pallas_call(
        body, name=name, grid=(T // tl, C // tc),
        in_specs=[pl.BlockSpec((tl, tc), lambda i, j: (i, c0 + j)), _prev_halo_spec(tl, tc, lambda j: c0 + j),
                  pl.BlockSpec((K, tc), lambda i, j: (0, j)), pl.BlockSpec((1, tc), lambda i, j: (0, j)),
                  pl.BlockSpec((tl, tc), lambda i, j: (i, j))],
        out_specs=pl.BlockSpec((tl, tc), lambda i, j: (i, j)),
        out_shape=jax.ShapeDtypeStruct((T, C), F32),
        scratch_shapes=[pltpu.VMEM((tl + HALO, tc), F32)],
        compiler_params=_cparams("parallel", "parallel"),
    )(proj, proj, w, b, d_out)


def ffn_act(up, w, b, *, name, tile=512, tc=1408):
    T = up.shape[0]
    K = w.shape[0]
    tl = min(T, tile)
    nj = D_FF // tc

    def body(xg_ref, hg_ref, xv_ref, hv_ref, wg_ref, wv_ref, bg_ref, bv_ref, o_ref, bufg, bufv):
        i = pl.program_id(0)
        _fill_prev(bufg, xg_ref, hg_ref, i)
        _fill_prev(bufv, xv_ref, hv_ref, i)
        gate = _conv_from(bufg, wg_ref, bg_ref, K, tl)
        val = _conv_from(bufv, wv_ref, bv_ref, K, tl)
        o_ref[...] = (gate * _sigmoid(gate) * val).astype(o_ref.dtype)

    return pl.pallas_call(
        body, name=name, grid=(T // tl, nj),
        in_specs=[pl.BlockSpec((tl, tc), lambda i, j: (i, j)), _prev_halo_spec(tl, tc, lambda j: j),
                  pl.BlockSpec((tl, tc), lambda i, j: (i, nj + j)), _prev_halo_spec(tl, tc, lambda j: nj + j),
                  pl.BlockSpec((K, tc), lambda i, j: (0, j)), pl.BlockSpec((K, tc), lambda i, j: (0, nj + j)),
                  pl.BlockSpec((1, tc), lambda i, j: (0, j)), pl.BlockSpec((1, tc), lambda i, j: (0, nj + j))],
        out_specs=pl.BlockSpec((tl, tc), lambda i, j: (i, j)),
        out_shape=jax.ShapeDtypeStruct((T, D_FF), BF16),
        scratch_shapes=[pltpu.VMEM((tl + HALO, tc), F32), pltpu.VMEM((tl + HALO, tc), F32)],
        compiler_params=_cparams("parallel", "parallel"),
    )(up, up, up, up, w, w, b, b)


def ffn_bwd_fused(up, w, b, d_act, *, name, tile=512, tc=256):
    T = up.shape[0]
    K = w.shape[0]
    tl = min(T, tile)
    nj = D_FF // tc
    nblk = T // HALO
    ext = tl + HALO

    def body(xg, hgp, hgn, xv, hvp, hvn, wg, wv, bg, bv, d, dn, og, ov, dwg, dwv, dbg, dbv, bufg, bufv, dgb, dvb):
        i = pl.program_id(1)
        last = pl.num_programs(1) - 1

        def fill(buf, x_ref, prev_ref, next_ref):
            buf[0:HALO, :] = jnp.where(i > 0, prev_ref[...].astype(F32), 0.0)
            buf[HALO:HALO + tl, :] = x_ref[...].astype(F32)
            buf[HALO + tl:, :] = jnp.where(i < last, next_ref[...].astype(F32), 0.0)

        def conv_ext(buf, w_ref, b_ref):
            acc = b_ref[...].astype(F32) + jnp.zeros((ext, tc), F32)
            for k in range(K):
                acc = acc + buf[pl.ds(HALO - (K - 1 - k), ext), :] * w_ref[k:k + 1, :].astype(F32)
            return acc

        fill(bufg, xg, hgp, hgn)
        fill(bufv, xv, hvp, hvn)
        gate = conv_ext(bufg, wg, bg)
        val = conv_ext(bufv, wv, bv)
        dd = jnp.concatenate([d[...].astype(F32), jnp.where(i < last, dn[...].astype(F32)[0:HALO], 0.0)], axis=0)
        s = _sigmoid(gate)
        dgb[...] = dd * val * (s * (1.0 + gate * (1.0 - s)))
        dvb[...] = dd * (gate * s)

        @pl.when(i == 0)
        def _():
            for ref in (dwg, dwv, dbg, dbv):
                ref[...] = jnp.zeros_like(ref)

        for dbuf, xbuf, w_ref, o_ref, dw_ref, db_ref in ((dgb, bufg, wg, og, dwg, dbg), (dvb, bufv, wv, ov, dwv, dbv)):
            cur = dbuf[0:tl, :]
            dx = jnp.zeros((tl, tc), F32)
            for k in range(K):
                sft = K - 1 - k
                dx = dx + dbuf[pl.ds(sft, tl), :] * w_ref[k:k + 1, :].astype(F32)
                dw_ref[k:k + 1, :] += jnp.sum(cur * xbuf[pl.ds(HALO - sft, tl), :], axis=0, keepdims=True)
            db_ref[...] += jnp.sum(cur, axis=0, keepdims=True)
            o_ref[...] = dx.astype(o_ref.dtype)

    prev = lambda i: jnp.maximum(i * (tl // HALO) - 1, 0)
    nxt = lambda i: jnp.minimum((i + 1) * (tl // HALO), nblk - 1)

    def x_specs(col):
        return [pl.BlockSpec((tl, tc), lambda j, i: (i, col(j))), pl.BlockSpec((HALO, tc), lambda j, i: (prev(i), col(j))),
                pl.BlockSpec((HALO, tc), lambda j, i: (nxt(i), col(j)))]

    gcol, vcol = (lambda j: j), (lambda j: nj + j)
    in_specs = (x_specs(gcol) + x_specs(vcol)
                + [pl.BlockSpec((K, tc), lambda j, i: (0, j)), pl.BlockSpec((K, tc), lambda j, i: (0, nj + j)),
                   pl.BlockSpec((1, tc), lambda j, i: (0, j)), pl.BlockSpec((1, tc), lambda j, i: (0, nj + j)),
                   pl.BlockSpec((tl, tc), lambda j, i: (i, j)),
                   pl.BlockSpec((2 * HALO, tc), lambda j, i: (jnp.minimum((i + 1) * (tl // (2 * HALO)), nblk // 2 - 1), j))])
    row_out = pl.BlockSpec((tl, tc), lambda j, i: (i, j))
    w_out = pl.BlockSpec((K, tc), lambda j, i: (0, j))
    b_out = pl.BlockSpec((1, tc), lambda j, i: (0, j))
    return pl.pallas_call(
        body, name=name, grid=(nj, T // tl), in_specs=in_specs,
        out_specs=[row_out, row_out, w_out, w_out, b_out, b_out],
        out_shape=[jax.ShapeDtypeStruct((T, D_FF), BF16)] * 2 + [jax.ShapeDtypeStruct((K, D_FF), F32)] * 2
        + [jax.ShapeDtypeStruct((1, D_FF), F32)] * 2,
        scratch_shapes=[pltpu.VMEM((tl + 2 * HALO, tc), F32)] * 2 + [pltpu.VMEM((ext, tc), F32)] * 2,
        compiler_params=_cparams("parallel", "arbitrary"),
    )(up, up, up, up, up, up, w, w, b, b, d_act, d_act)


def conv_bwd_b(du, x, x_off, w, *, name, out_dtype, tile=512, tc=256):
    T, C = du.shape
    K = w.shape[0]
    tl = min(T, tile)
    c0 = x_off // tc
    nblk = T // HALO

    def body(du_ref, nx_ref, x_ref, w_ref, dx_ref, dw_ref, db_ref, dbuf):
        i = pl.program_id(1)
        last = pl.num_programs(1) - 1
        d = du_ref[...].astype(F32)
        dbuf[0:tl, :] = d
        dbuf[tl:, :] = jnp.where(i < last, nx_ref[...].astype(F32), 0.0)

        @pl.when(i == 0)
        def _():
            dw_ref[...] = jnp.zeros_like(dw_ref)
            db_ref[...] = jnp.zeros_like(db_ref)

        xin = x_ref[...].astype(F32)
        dx = jnp.zeros((tl, tc), F32)
        for k in range(K):
            s = K - 1 - k
            shifted = dbuf[pl.ds(s, tl), :]
            dx = dx + shifted * w_ref[k:k + 1, :].astype(F32)
            dw_ref[k:k + 1, :] += jnp.sum(shifted * xin, axis=0, keepdims=True)
        db_ref[...] += jnp.sum(d, axis=0, keepdims=True)
        dx_ref[...] = dx.astype(dx_ref.dtype)

    return pl.pallas_call(
        body, name=name, grid=(C // tc, T // tl),
        in_specs=[pl.BlockSpec((tl, tc), lambda j, i: (i, j)),
                  pl.BlockSpec((HALO, tc), lambda j, i: (jnp.minimum((i + 1) * (tl // HALO), nblk - 1), j)),
                  pl.BlockSpec((tl, tc), lambda j, i: (i, c0 + j)),
                  pl.BlockSpec((K, tc), lambda j, i: (0, j))],
        out_specs=[pl.BlockSpec((tl, tc), lambda j, i: (i, j)), pl.BlockSpec((K, tc), lambda j, i: (0, j)),
                   pl.BlockSpec((1, tc), lambda j, i: (0, j))],
        out_shape=[jax.ShapeDtypeStruct((T, C), out_dtype), jax.ShapeDtypeStruct((K, C), F32),
                   jax.ShapeDtypeStruct((1, C), F32)],
        scratch_shapes=[pltpu.VMEM((tl + HALO, tc), F32)],
        compiler_params=_cparams("parallel", "arbitrary"),
    )(du, du, x, w)


def _attn_tiles(T, keys=256):
    return min(T, 1024), min(T, keys)


def _after_diag(keys, queries, strict):
    d = lax.broadcasted_iota(jnp.int32, (keys, queries), 1) - lax.broadcasted_iota(jnp.int32, (keys, queries), 0)
    return d > 0 if strict else d >= 0


def _log_gates(z):
    l1p = jnp.log(1.0 + jnp.exp(-jnp.abs(z)))
    a = jnp.minimum(z, 0.0) - l1p
    return a, a - z


def _causal_sweep(i, tq, tk, block, descending, keep_going=None, first_block=None):
    nb = tq // tk
    n_full = i * nb

    def band():
        order = reversed(range(nb)) if descending else range(nb)
        for bb in order:
            block(pl.multiple_of(i * tq + bb * tk, tk), bb * tk, True)

    def full():
        if descending and keep_going is not None:
            def step(j):
                block(pl.multiple_of((n_full - 1 - j) * tk, tk), 0, False)
                return j + 1
            done = lax.while_loop(lambda j: jnp.logical_and(j < n_full, keep_going()), step, jnp.int32(0))
            return n_full - done

        def step(j, c):
            kb = (n_full - 1 - j) if descending else j
            block(pl.multiple_of(kb * tk, tk), 0, False)
            return c
        lax.fori_loop(0 if first_block is None else first_block, n_full, step, 0)
        return None

    if descending:
        band()
        return full()
    full()
    band()
    return None


def sb_fwd(q, k, v, *, name):
    H, dh, T = q.shape
    tq, tk = _attn_tiles(T)

    def body(q_ref, k_ref, v_ref, y_ref, bt_ref, first_ref, acc, run):
        acc[...] = jnp.zeros_like(acc)
        run[...] = jnp.zeros_like(run)
        u_after = _tri(tk, lambda r, c: r < c)

        def block(k0, r0, masked):
            kb = k_ref[pl.ds(k0, tk), :]
            vb = v_ref[pl.ds(k0, tk), :]
            z = jnp.dot(kb, q_ref[:, r0:], preferred_element_type=F32)
            a, b = _log_gates(z)
            if masked:
                valid = _after_diag(tk, tq - r0, True)
                b = jnp.where(valid, b, 0.0)
            w = jnp.exp(a + _tri_dot(u_after, b, 2) + run[:, r0:])
            if masked:
                w = jnp.where(valid, w, 0.0)
            acc[:, r0:] += lax.dot_general(vb, w.astype(BF16), TN, preferred_element_type=F32)
            run[:, r0:] += jnp.sum(b, axis=0, keepdims=True)

        first = _causal_sweep(pl.program_id(1), tq, tk, block, descending=True,
                              keep_going=lambda: jnp.max(run[...]) >= SB_ZERO_BELOW)
        y_ref[...] = acc[...]
        bt_ref[...] = run[...]
        first_ref[...] = jnp.zeros(first_ref.shape, F32) + first.astype(F32)

    return pl.pallas_call(
        body, name=name, grid=(H, T // tq),
        in_specs=[pl.BlockSpec((None, dh, tq), lambda h, i: (h, 0, i)),
                  pl.BlockSpec((None, T, dh), lambda h, i: (h, 0, 0)),
                  pl.BlockSpec((None, T, dh), lambda h, i: (h, 0, 0))],
        out_specs=[pl.BlockSpec((None, dh, tq), lambda h, i: (h, 0, i)),
                   pl.BlockSpec((None, 1, tq), lambda h, i: (h, 0, i)),
                   pl.BlockSpec((None, None, HALO, LANES), lambda h, i: (h, i, 0, 0))],
        out_shape=[jax.ShapeDtypeStruct((H, dh, T), F32), jax.ShapeDtypeStruct((H, 1, T), F32),
                   jax.ShapeDtypeStruct((H, T // tq, HALO, LANES), F32)],
        scratch_shapes=[pltpu.VMEM((dh, tq), F32), pltpu.VMEM((1, tq), F32)],
        compiler_params=_cparams("parallel", "parallel"),
    )(q, k, v)


def sb_bwd(q, k, v, dy, btot, first, *, name, q_scale):
    H, dh, T = q.shape
    tq, tk = _attn_tiles(T)

    def body(q_ref, k_ref, v_ref, dy_ref, bt_ref, first_ref, dq_ref, dk_ref, dv_ref, dq, pb, pg, dyb):
        @pl.when(pl.program_id(1) == 0)
        def _():
            dk_ref[...] = jnp.zeros_like(dk_ref)
            dv_ref[...] = jnp.zeros_like(dv_ref)

        dq[...] = jnp.zeros_like(dq)
        pb[...] = jnp.zeros_like(pb)
        pg[...] = jnp.zeros_like(pg)
        dyb[...] = dy_ref[...].astype(BF16)
        u_upto = _tri(tk, lambda r, c: r >= c)
        u_before = _tri(tk, lambda r, c: r > c)

        def block(k0, r0, masked):
            kb = k_ref[pl.ds(k0, tk), :]
            vb = v_ref[pl.ds(k0, tk), :]
            qv = q_ref[:, r0:]
            dyv = dyb[:, r0:]
            z = jnp.dot(kb, qv, preferred_element_type=F32)
            a, b = _log_gates(z)
            if masked:
                valid = _after_diag(tk, tq - r0, True)
                b = jnp.where(valid, b, 0.0)
            w = jnp.exp(a + (bt_ref[:, r0:] - pb[:, r0:] - _tri_dot(u_upto, b, 2)))
            if masked:
                w = jnp.where(valid, w, 0.0)
            g = w * jnp.dot(vb, dyv, preferred_element_type=F32)
            dz = g - jnp.exp(a) * (g + pg[:, r0:] + _tri_dot(u_before, g, 2))
            if masked:
                dz = jnp.where(valid, dz, 0.0)
            dz = dz.astype(BF16)
            dq[:, r0:] += lax.dot_general(kb, dz, TN, preferred_element_type=F32)
            dk_ref[pl.ds(k0, tk), :] += lax.dot_general(dz, qv, NT, preferred_element_type=F32)
            dv_ref[pl.ds(k0, tk), :] += lax.dot_general(w.astype(BF16), dyv, NT, preferred_element_type=F32)
            pb[:, r0:] += jnp.sum(b, axis=0, keepdims=True)
            pg[:, r0:] += jnp.sum(g, axis=0, keepdims=True)

        i = pl.program_id(1)
        first = jnp.clip(jnp.max(first_ref[...]).astype(jnp.int32), 0, i * (tq // tk))
        _causal_sweep(i, tq, tk, block, descending=False, first_block=first)
        dq_ref[...] = dq[...] * q_scale

    return pl.pallas_call(
        body, name=name, grid=(H, T // tq),
        in_specs=[pl.BlockSpec((None, dh, tq), lambda h, i: (h, 0, i)),
                  pl.BlockSpec((None, T, dh), lambda h, i: (h, 0, 0)),
                  pl.BlockSpec((None, T, dh), lambda h, i: (h, 0, 0)),
                  pl.BlockSpec((None, dh, tq), lambda h, i: (h, 0, i)),
                  pl.BlockSpec((None, 1, tq), lambda h, i: (h, 0, i)),
                  pl.BlockSpec((None, None, HALO, LANES), lambda h, i: (h, i, 0, 0))],
        out_specs=[pl.BlockSpec((None, dh, tq), lambda h, i: (h, 0, i)),
                   pl.BlockSpec((None, T, dh), lambda h, i: (h, 0, 0)),
                   pl.BlockSpec((None, T, dh), lambda h, i: (h, 0, 0))],
        out_shape=[jax.ShapeDtypeStruct((H, dh, T), F32), jax.ShapeDtypeStruct((H, T, dh), F32),
                   jax.ShapeDtypeStruct((H, T, dh), F32)],
        scratch_shapes=[pltpu.VMEM((dh, tq), F32), pltpu.VMEM((1, tq), F32), pltpu.VMEM((1, tq), F32),
                        pltpu.VMEM((dh, tq), BF16)],
        compiler_params=_cparams("parallel", "arbitrary"),
    )(q, k, v, dy, btot, first)


NEG = -1e30
SB_ZERO_BELOW = -105.0
MLA_KEYS = 512


def _call_with_exchange(body, exchange, *, name, grid, in_specs, out_specs, out_shape, scratch_shapes, args):
    if exchange is None:
        return pl.pallas_call(body, name=name, grid=grid, in_specs=in_specs, out_specs=out_specs, out_shape=out_shape,
                              scratch_shapes=scratch_shapes, compiler_params=_cparams("parallel", "arbitrary"))(*args)
    kind, src = exchange
    n_in, n_out, n_scr = len(in_specs), len(out_specs), len(scratch_shapes)
    R = src.shape[-2]

    def wrapped(*refs):
        ins, src_ref = refs[:n_in], refs[n_in]
        outs, xout = refs[n_in + 1:n_in + 1 + n_out], refs[n_in + 1 + n_out]
        scr = refs[n_in + 2 + n_out:n_in + 2 + n_out + n_scr]
        start, finish = _direct_exchange(kind, src_ref, xout, *refs[-3:])
        step = pl.program_id(0) * pl.num_programs(1) + pl.program_id(1)
        pl.when(step == 0)(start)
        body(*ins, *outs, *scr)
        pl.when(step == pl.num_programs(0) * pl.num_programs(1) - 1)(finish)

    return pl.pallas_call(
        wrapped, name=name, grid=grid, in_specs=list(in_specs) + [HBM_SPEC], out_specs=list(out_specs) + [HBM_SPEC],
        out_shape=list(out_shape) + [jax.ShapeDtypeStruct((8, R, LANES), src.dtype)],
        scratch_shapes=list(scratch_shapes) + [pltpu.SemaphoreType.DMA((7,)), pltpu.SemaphoreType.DMA((7,)),
                                               pltpu.SemaphoreType.DMA],
        compiler_params=_cparams("arbitrary", "arbitrary"))(*args, src)


def mla_fwd(q, k, v, *, name, exchange=None):
    H, dk, T = q.shape
    dv = v.shape[1] // H
    tq, tk = _attn_tiles(T, MLA_KEYS)

    def body(q_ref, k_ref, v_ref, o_ref, l_ref, acc, m_s, l_s):
        acc[...] = jnp.zeros_like(acc)
        m_s[...] = jnp.full_like(m_s, NEG)
        l_s[...] = jnp.zeros_like(l_s)

        def block(k0, r0, masked):
            kb = k_ref[pl.ds(k0, tk), :]
            vb = v_ref[pl.ds(k0, tk), :]
            s = jnp.dot(kb, q_ref[:, r0:], preferred_element_type=F32)
            if masked:
                s = jnp.where(_after_diag(tk, tq - r0, False), s, NEG)
            m = m_s[:, r0:]
            m_new = jnp.maximum(m, jnp.max(s, axis=0, keepdims=True))
            p = jnp.exp(s - m_new)
            alpha = jnp.exp(m - m_new)
            l_s[:, r0:] = alpha * l_s[:, r0:] + jnp.sum(p, axis=0, keepdims=True)
            acc[:, r0:] = alpha * acc[:, r0:] + lax.dot_general(vb, p.astype(BF16), TN, preferred_element_type=F32)
            m_s[:, r0:] = m_new

        _causal_sweep(pl.program_id(1), tq, tk, block, descending=False)
        o_ref[...] = acc[...] / l_s[...]
        l_ref[...] = m_s[...] + jnp.log(l_s[...])

    return _call_with_exchange(
        body, exchange, name=name, grid=(H, T // tq),
        in_specs=[pl.BlockSpec((None, dk, tq), lambda h, i: (h, 0, i)),
                  pl.BlockSpec((T, dk), lambda h, i: (0, h)),
                  pl.BlockSpec((T, dv), lambda h, i: (0, h))],
        out_specs=[pl.BlockSpec((None, dv, tq), lambda h, i: (h, 0, i)),
                   pl.BlockSpec((None, 1, tq), lambda h, i: (h, 0, i))],
        out_shape=[jax.ShapeDtypeStruct((H, dv, T), F32), jax.ShapeDtypeStruct((H, 1, T), F32)],
        scratch_shapes=[pltpu.VMEM((dv, tq), F32), pltpu.VMEM((1, tq), F32), pltpu.VMEM((1, tq), F32)],
        args=(q, k, v))


def mla_bwd(q, k, v, do, o, lse, *, name, exchange=None):
    H, dk, T = q.shape
    dv = v.shape[1] // H
    tq, tk = _attn_tiles(T, MLA_KEYS)

    def body(q_ref, k_ref, v_ref, do_ref, o_ref, l_ref, dq_ref, dk_ref, dv_ref, dq, delta, dob):
        @pl.when(pl.program_id(1) == 0)
        def _():
            dk_ref[...] = jnp.zeros_like(dk_ref)
            dv_ref[...] = jnp.zeros_like(dv_ref)

        dq[...] = jnp.zeros_like(dq)
        dov = do_ref[...].astype(F32)
        dob[...] = dov.astype(BF16)
        delta[...] = jnp.sum(dov * o_ref[...], axis=0, keepdims=True)

        def block(k0, r0, masked):
            kb = k_ref[pl.ds(k0, tk), :]
            vb = v_ref[pl.ds(k0, tk), :]
            qv = q_ref[:, r0:]
            dov_b = dob[:, r0:]
            s = jnp.dot(kb, qv, preferred_element_type=F32)
            p = jnp.exp(s - l_ref[:, r0:])
            if masked:
                p = jnp.where(_after_diag(tk, tq - r0, False), p, 0.0)
            dp = jnp.dot(vb, dov_b, preferred_element_type=F32)
            ds = (p * (dp - delta[:, r0:])).astype(BF16)
            dq[:, r0:] += lax.dot_general(kb, ds, TN, preferred_element_type=F32)
            dk_ref[pl.ds(k0, tk), :] += lax.dot_general(ds, qv, NT, preferred_element_type=F32)
            dv_ref[pl.ds(k0, tk), :] += lax.dot_general(p.astype(BF16), dov_b, NT, preferred_element_type=F32)

        _causal_sweep(pl.program_id(1), tq, tk, block, descending=False)
        dq_ref[...] = dq[...]

    return _call_with_exchange(
        body, exchange, name=name, grid=(H, T // tq),
        in_specs=[pl.BlockSpec((None, dk, tq), lambda h, i: (h, 0, i)),
                  pl.BlockSpec((T, dk), lambda h, i: (0, h)),
                  pl.BlockSpec((T, dv), lambda h, i: (0, h)),
                  pl.BlockSpec((None, dv, tq), lambda h, i: (h, 0, i)),
                  pl.BlockSpec((None, dv, tq), lambda h, i: (h, 0, i)),
                  pl.BlockSpec((None, 1, tq), lambda h, i: (h, 0, i))],
        out_specs=[pl.BlockSpec((None, dk, tq), lambda h, i: (h, 0, i)),
                   pl.BlockSpec((T, dk), lambda h, i: (0, h)),
                   pl.BlockSpec((T, dv), lambda h, i: (0, h))],
        out_shape=[jax.ShapeDtypeStruct((H, dk, T), F32), jax.ShapeDtypeStruct((T, H * dk), F32),
                   jax.ShapeDtypeStruct((T, H * dv), F32)],
        scratch_shapes=[pltpu.VMEM((dk, tq), F32), pltpu.VMEM((1, tq), F32), pltpu.VMEM((dv, tq), BF16)],
        args=(q, k, v, do, o, lse))


def _lane_pick(x, h):
    lane = lax.broadcasted_iota(jnp.int32, (1, x.shape[1]), 1)
    return jnp.sum(jnp.where(lane == h, x, 0.0), axis=1, keepdims=True)


def _row_pick(x, h):
    sub = lax.broadcasted_iota(jnp.int32, (x.shape[0], 1), 0)
    return jnp.sum(jnp.where(sub == h, x, 0.0), axis=0, keepdims=True)


def ssd_chunk_fn(*args):
    nh, ng = SSM_HEADS, SSM_GROUPS
    xs = args[:nh]
    bs = args[nh:nh + ng]
    cs = args[nh + ng:nh + 2 * ng]
    dt_raw = args[nh + 2 * ng]
    st = args[nh + 2 * ng + 1:nh + 2 * ng + 1 + nh]
    dt_bias, a_log, d_skip = args[nh + 2 * ng + 1 + nh:]
    L = dt_raw.shape[0]
    dt = _softplus(dt_raw + dt_bias)
    da = dt * (-jnp.exp(a_log))
    dcs = csum_rows(da)
    dcs_t = dcs.T
    total = jnp.sum(da, axis=0, keepdims=True)
    causal = lax.broadcasted_iota(jnp.int32, (L, L), 0) >= lax.broadcasted_iota(jnp.int32, (L, L), 1)
    cb = [mm_nt(cs[g], bs[g]) for g in range(ng)]
    ys, new_st = [], []
    for h in range(nh):
        g = h // (nh // ng)
        dcs_h = _lane_pick(dcs, h)
        dt_h = _lane_pick(dt, h)
        tot_h = _lane_pick(total, h)
        dsk_h = _lane_pick(d_skip, h)
        decay = jnp.exp(jnp.where(causal, dcs_h - _row_pick(dcs_t, h), NEG))
        xdt = xs[h] * dt_h
        y = mm_nn(cb[g] * decay, xdt)
        y = y + mm_nn(cs[g] * jnp.exp(dcs_h), st[h])
        ys.append(y + xs[h] * dsk_h)
        new_st.append(st[h] * jnp.exp(tot_h) + mm_tn(bs[g] * jnp.exp(tot_h - dcs_h), xdt))
    return tuple(ys) + tuple(new_st)


def ssd_fwd(x_hm, b_hm, c_hm, proj, dt_bias, a_log, d_skip, *, name):
    nh, T, P = x_hm.shape
    ng, N = b_hm.shape[0], b_hm.shape[2]
    L = SSM_CHUNK
    nc = T // L
    dtb = OFF_DT // LANES

    def body(x_ref, b_ref, c_ref, dt_ref, db_ref, al_ref, ds_ref, y_ref, s_ref, state):
        @pl.when(pl.program_id(0) == 0)
        def _():
            state[...] = jnp.zeros_like(state)

        s_ref[...] = state[...]
        args = ([x_ref[h] for h in range(nh)] + [b_ref[g] for g in range(ng)] + [c_ref[g] for g in range(ng)]
                + [dt_ref[...]] + [state[h] for h in range(nh)] + [db_ref[...], al_ref[...], ds_ref[...]])
        res = ssd_chunk_fn(*args)
        for h in range(nh):
            y_ref[:, h * P:(h + 1) * P] = res[h]
            state[h] = res[nh + h]

    par = pl.BlockSpec((1, LANES), lambda i: (0, 0))
    return pl.pallas_call(
        body, name=name, grid=(nc,),
        in_specs=[pl.BlockSpec((nh, L, P), lambda i: (0, i, 0)), pl.BlockSpec((ng, L, N), lambda i: (0, i, 0)),
                  pl.BlockSpec((ng, L, N), lambda i: (0, i, 0)), pl.BlockSpec((L, LANES), lambda i: (i, dtb)),
                  par, par, par],
        out_specs=[pl.BlockSpec((L, nh * P), lambda i: (i, 0)),
                   pl.BlockSpec((None, nh, N, P), lambda i: (i, 0, 0, 0))],
        out_shape=[jax.ShapeDtypeStruct((T, nh * P), F32), jax.ShapeDtypeStruct((nc, nh, N, P), F32)],
        scratch_shapes=[pltpu.VMEM((nh, N, P), F32)],
        compiler_params=_cparams("arbitrary"),
    )(x_hm, b_hm, c_hm, proj, dt_bias, a_log, d_skip)


def ssd_bwd(x_hm, b_hm, c_hm, proj, states, dt_bias, a_log, d_skip, dy, *, name):
    nh, T, P = x_hm.shape
    ng, N = b_hm.shape[0], b_hm.shape[2]
    L = SSM_CHUNK
    nc = T // L
    dtb = OFF_DT // LANES

    def body(x_ref, b_ref, c_ref, dt_ref, s_ref, db_ref, al_ref, ds_ref, dy_ref,
             dxbc_ref, ddt_ref, gdb_ref, gal_ref, gds_ref, dstate):
        @pl.when(pl.program_id(0) == 0)
        def _():
            dstate[...] = jnp.zeros_like(dstate)
            gdb_ref[...] = jnp.zeros_like(gdb_ref)
            gal_ref[...] = jnp.zeros_like(gal_ref)
            gds_ref[...] = jnp.zeros_like(gds_ref)

        args = ([x_ref[h] for h in range(nh)] + [b_ref[g] for g in range(ng)] + [c_ref[g] for g in range(ng)]
                + [dt_ref[...]] + [s_ref[h] for h in range(nh)] + [db_ref[...], al_ref[...], ds_ref[...]])
        _, vjp = jax.vjp(ssd_chunk_fn, *args)
        g = vjp(tuple([dy_ref[h] for h in range(nh)] + [dstate[h] for h in range(nh)]))
        for j in range(nh + 2 * ng):
            dxbc_ref[:, j * P:(j + 1) * P] = g[j]
        ddt_ref[...] = g[nh + 2 * ng]
        for h in range(nh):
            dstate[h] = g[nh + 2 * ng + 1 + h]
        gdb_ref[...] += g[-3]
        gal_ref[...] += g[-2]
        gds_ref[...] += g[-1]

    rev = lambda i: nc - 1 - i
    par = pl.BlockSpec((1, LANES), lambda i: (0, 0))
    return pl.pallas_call(
        body, name=name, grid=(nc,),
        in_specs=[pl.BlockSpec((nh, L, P), lambda i: (0, rev(i), 0)), pl.BlockSpec((ng, L, N), lambda i: (0, rev(i), 0)),
                  pl.BlockSpec((ng, L, N), lambda i: (0, rev(i), 0)), pl.BlockSpec((L, LANES), lambda i: (rev(i), dtb)),
                  pl.BlockSpec((None, nh, N, P), lambda i: (rev(i), 0, 0, 0)), par, par, par,
                  pl.BlockSpec((nh, L, P), lambda i: (0, rev(i), 0))],
        out_specs=[pl.BlockSpec((L, (nh + 2 * ng) * P), lambda i: (rev(i), 0)),
                   pl.BlockSpec((L, LANES), lambda i: (rev(i), 0)), par, par, par],
        out_shape=[jax.ShapeDtypeStruct((T, (nh + 2 * ng) * P), F32), jax.ShapeDtypeStruct((T, LANES), F32),
                   jax.ShapeDtypeStruct((1, LANES), F32), jax.ShapeDtypeStruct((1, LANES), F32),
                   jax.ShapeDtypeStruct((1, LANES), F32)],
        scratch_shapes=[pltpu.VMEM((nh, N, P), F32)],
        compiler_params=_cparams("arbitrary"),
    )(x_hm, b_hm, c_hm, proj, states, dt_bias, a_log, d_skip, dy)


def loss_head(h, target, g, *, name, tile=512):
    T, C = h.shape
    tl = min(T, tile)

    def body(h_ref, t_ref, g_ref, dh_ref, dg_ref, ls_ref):
        @pl.when(pl.program_id(0) == 0)
        def _():
            dg_ref[...] = jnp.zeros_like(dg_ref)
            ls_ref[...] = jnp.zeros_like(ls_ref)

        (y,), vjp = jax.vjp(rms_fn, h_ref[...], g_ref[...])
        err = y - t_ref[...]
        ls_ref[...] += jnp.sum(err * err, axis=0, keepdims=True) * (0.5 / C)
        dh, dg = vjp((err * (1.0 / C),))
        dh_ref[...] = dh
        dg_ref[...] += dg

    row = pl.BlockSpec((tl, C), lambda i: (i, 0))
    par = pl.BlockSpec((1, C), lambda i: (0, 0))
    return pl.pallas_call(
        body, name=name, grid=(T // tl,), in_specs=[row, row, par], out_specs=[row, par, par],
        out_shape=[jax.ShapeDtypeStruct((T, C), F32), jax.ShapeDtypeStruct((1, C), F32),
                   jax.ShapeDtypeStruct((1, C), F32)],
        compiler_params=_cparams("arbitrary"),
    )(h, target, g)


def adamw(w, g, m, v, *, name):
    R, C = w.shape
    tr = R
    for d in range(8, min(R, 512) + 1, 8):
        if R % d == 0:
            tr = d
    c1 = 1.0 - ADAM_B1 ** ADAM_STEP
    c2 = 1.0 - ADAM_B2 ** ADAM_STEP

    def body(w_ref, g_ref, m_ref, v_ref, d_ref, nm_ref, nv_ref):
        gv = g_ref[...]
        nm = ADAM_B1 * m_ref[...] + (1.0 - ADAM_B1) * gv
        nv = ADAM_B2 * v_ref[...] + (1.0 - ADAM_B2) * (gv * gv)
        d_ref[...] = -ADAM_LR * ((nm / c1) / (jnp.sqrt(nv / c2) + ADAM_EPS) + ADAM_WD * w_ref[...])
        nm_ref[...] = nm
        nv_ref[...] = nv

    spec = pl.BlockSpec((tr, C), lambda i: (i, 0))
    return pl.pallas_call(
        body, name=name, grid=(R // tr,), in_specs=[spec] * 4, out_specs=[spec] * 3,
        out_shape=[jax.ShapeDtypeStruct((R, C), F32)] * 3,
        compiler_params=_cparams("parallel"),
    )(w, g, m, v)


MESH = pl.DeviceIdType.MESH
HBM_SPEC = pl.BlockSpec(memory_space=pltpu.HBM)


def _place():
    return lax.axis_index("x"), lax.axis_index("y"), lax.axis_index("c")


def _direct_exchange(kind, src_ref, out_ref, send_sems, recv_sems, local_sem):
    x, y, c = _place()
    me = 4 * x + 2 * y + c

    def block_for(dest):
        return src_ref if kind == "gather" else src_ref.at[dest]

    own = pltpu.make_async_copy(block_for(me), out_ref.at[me], local_sem)
    sends, arrivals = [], []
    for f in range(1, 8):
        px = jnp.where((f >> 2) & 1, 1 - x, x)
        py = jnp.where((f >> 1) & 1, 1 - y, y)
        pc = jnp.where(f & 1, 1 - c, c)
        peer = 4 * px + 2 * py + pc
        for dst, group in ((out_ref.at[me], sends), (out_ref.at[peer], arrivals)):
            group.append(pltpu.make_async_remote_copy(
                src_ref=block_for(peer), dst_ref=dst, send_sem=send_sems.at[f - 1], recv_sem=recv_sems.at[f - 1],
                device_id=(px, py, pc), device_id_type=MESH))

    def start():
        own.start()
        for cp in sends:
            cp.start()

    def finish():
        for cp in arrivals:
            cp.wait_recv()
        for cp in sends:
            cp.wait_send()
        own.wait()

    return start, finish


def allgather_blocks(mine, *, name):
    R = mine.shape[0]

    def body(x_ref, out_ref, send_sems, recv_sems, local_sem):
        x, y, c = _place()
        me, sibling = (x, y, c), (x, y, 1 - c)
        chips = [(1 - x, y), (x, 1 - y), (1 - x, 1 - y)]

        def slot(px, py, pc):
            return out_ref.at[4 * px + 2 * py + pc]

        def copy(k, block, to, src=None):
            return pltpu.make_async_remote_copy(
                src_ref=slot(*block) if src is None else src, dst_ref=slot(*block),
                send_sem=send_sems.at[k], recv_sem=recv_sems.at[k], device_id=to, device_id_type=MESH)

        own = pltpu.make_async_copy(x_ref, slot(*me), local_sem)
        own.start()
        first = [copy(0, me, sibling, src=x_ref)]
        first += [copy(1 + j, me, (*chip, c), src=x_ref) for j, chip in enumerate(chips)]
        for cp in first:
            cp.start()
        passed = [copy(4 + j, (*chip, c), sibling) for j, chip in enumerate(chips)]
        for j, chip in enumerate(chips):
            copy(1 + j, (*chip, c), me).wait_recv()
            passed[j].start()
        copy(0, sibling, me).wait_recv()
        for j, chip in enumerate(chips):
            copy(4 + j, (*chip, 1 - c), me).wait_recv()
        for cp in first + passed:
            cp.wait_send()
        own.wait()

    return pl.pallas_call(
        body, name=name, out_shape=jax.ShapeDtypeStruct((8, R, LANES), mine.dtype),
        in_specs=[HBM_SPEC], out_specs=HBM_SPEC,
        scratch_shapes=[pltpu.SemaphoreType.DMA((7,)), pltpu.SemaphoreType.DMA((7,)), pltpu.SemaphoreType.DMA],
    )(mine)


def allgather_direct(mine, *, name):
    R = mine.shape[0]

    def body(x_ref, out_ref, send_sems, recv_sems, local_sem):
        start, finish = _direct_exchange("gather", x_ref, out_ref, send_sems, recv_sems, local_sem)
        start()
        finish()

    return pl.pallas_call(
        body, name=name, out_shape=jax.ShapeDtypeStruct((8, R, LANES), mine.dtype),
        in_specs=[HBM_SPEC], out_specs=HBM_SPEC,
        scratch_shapes=[pltpu.SemaphoreType.DMA((7,)), pltpu.SemaphoreType.DMA((7,)), pltpu.SemaphoreType.DMA],
    )(mine)


def send_to_sibling(v, *, name):
    def body(v_ref, out_ref, send_sem, recv_sem):
        x, y, c = _place()
        cp = pltpu.make_async_remote_copy(src_ref=v_ref, dst_ref=out_ref, send_sem=send_sem, recv_sem=recv_sem,
                                          device_id=(x, y, 1 - c), device_id_type=MESH)
        cp.start()
        cp.wait()

    return pl.pallas_call(
        body, name=name, out_shape=jax.ShapeDtypeStruct(v.shape, v.dtype), in_specs=[HBM_SPEC], out_specs=HBM_SPEC,
        scratch_shapes=[pltpu.SemaphoreType.DMA, pltpu.SemaphoreType.DMA],
    )(v)


def chip_exchange(p, *, name):
    R = p.shape[1]

    def body(p_ref, out_ref, send_sems, recv_sems):
        x, y, c = _place()
        chips = [(1 - x, y), (x, 1 - y), (1 - x, 1 - y)]
        sends = [pltpu.make_async_remote_copy(
            src_ref=p_ref.at[2 * px + py], dst_ref=out_ref.at[j], send_sem=send_sems.at[j], recv_sem=recv_sems.at[j],
            device_id=(px, py, c), device_id_type=MESH) for j, (px, py) in enumerate(chips)]
        for cp in sends:
            cp.start()
        for cp in sends:
            cp.wait()

    return pl.pallas_call(
        body, name=name, out_shape=jax.ShapeDtypeStruct((3, R, LANES), p.dtype), in_specs=[HBM_SPEC],
        out_specs=HBM_SPEC,
        scratch_shapes=[pltpu.SemaphoreType.DMA((3,)), pltpu.SemaphoreType.DMA((3,))],
    )(p)


def add_blocks(terms, out_dtype, *, name, tile=1024):
    terms = [t if isinstance(t, tuple) else (t, None) for t in terms]
    R = terms[0][0].shape[-2]
    tr = R
    for d in range(16, min(R, tile) + 1, 16):
        if R % d == 0:
            tr = d

    def body(*refs):
        acc = refs[0][...].astype(F32)
        for ref in refs[1:-1]:
            acc = acc + ref[...].astype(F32)
        refs[-1][...] = acc.astype(out_dtype)

    spec = pl.BlockSpec((tr, LANES), lambda i: (i, 0))
    in_specs = [spec if slot is None else pl.BlockSpec((None, tr, LANES), lambda i, slot=slot: (slot, i, 0))
                for _, slot in terms]
    return pl.pallas_call(
        body, name=name, grid=(R // tr,), in_specs=in_specs, out_specs=spec,
        out_shape=jax.ShapeDtypeStruct((R, LANES), out_dtype), compiler_params=_cparams("parallel"),
    )(*[a for a, _ in terms])


FLAT_ROW_STEP = 640


def _half_rows(arr, cc):
    hr = arr.shape[0] // 2
    return lax.dynamic_slice_in_dim(arr, cc * hr, hr, axis=0).reshape(-1)


def _flat_half(shards, cc, dtype):
    flat = jnp.concatenate([_half_rows(shards[n], cc).astype(dtype) for n in BIG])
    rows = -(-flat.shape[0] // (FLAT_ROW_STEP * LANES)) * FLAT_ROW_STEP
    return jnp.pad(flat, (0, rows * LANES - flat.shape[0])).reshape(rows, LANES)


def _flat_rows(shapes):
    n = sum((R // 2) * C for R, C in shapes.values()) // LANES
    return -(-n // FLAT_ROW_STEP) * FLAT_ROW_STEP


def _to_blocks(full, shapes, dtype):
    pieces = []
    for n in BIG:
        R, C = shapes[n]
        a = full[n].astype(dtype)
        if BIG_AXIS[n] == 2:
            a = a.reshape(2, R // 2, 4, C).transpose(2, 0, 1, 3)
        pieces.append(a.reshape(8, (R // 2) * C // LANES, LANES))
    flat = jnp.concatenate(pieces, axis=1)
    return jnp.pad(flat, ((0, 0), (0, _flat_rows(shapes) - flat.shape[1]), (0, 0)))


def _from_blocks(g8, shapes):
    out, off = {}, 0
    for n in BIG:
        R, C = shapes[n]
        rows = (R // 2) * C // LANES
        a = g8[:, off:off + rows, :].reshape(4, 2, R // 2, C)
        out[n] = a.transpose(1, 2, 0, 3).reshape(R, 4 * C) if BIG_AXIS[n] == 2 else a.reshape(4 * R, C)
        off += rows
    return out


def _unflat_halves(flat_by_c, shapes):
    out, off = {}, 0
    for n in BIG:
        R, C = shapes[n]
        sz = (R // 2) * C
        out[n] = jnp.concatenate([flat_by_c[c][off:off + sz].reshape(R // 2, C) for c in range(2)], axis=0)
        off += sz
    return out


def _to_heads(a, nh):
    T = a.shape[0]
    return a.reshape(T, nh, a.shape[1] // nh).transpose(1, 0, 2)


def _from_heads(a):
    nh, T, d = a.shape
    return a.transpose(1, 0, 2).reshape(T, nh * d)


def _to_heads_t(a, nh):
    T = a.shape[0]
    return a.reshape(T, nh, a.shape[1] // nh).transpose(1, 2, 0)


def _from_heads_t(a):
    nh, d, T = a.shape
    return a.transpose(2, 0, 1).reshape(T, nh * d)


def _pad_cols(a, n):
    return jnp.pad(a, ((0, 0), (0, n - a.shape[1])))


def _pack_w_in(w):
    offs = [sum(IN_SPLITS[:i]) for i in range(len(IN_SPLITS) + 1)]
    sb, z, xbc, dt, cq, ckv, kr = [w[:, offs[i]:offs[i + 1]] for i in range(len(IN_SPLITS))]
    zeros = lambda n: jnp.zeros((w.shape[0], n), w.dtype)
    h = MLA_ROPE // 2
    kra = jnp.concatenate([zeros(MLA_NOPE), kr, zeros(LANES - MLA_QK)], axis=1)
    krb = jnp.concatenate([zeros(MLA_NOPE), -kr[:, h:], kr[:, :h], zeros(LANES - MLA_QK)], axis=1)
    return sb, jnp.concatenate([z, xbc, cq, ckv, _pad_cols(dt, LANES), kra, krb], axis=1)


def _unpack_gw_in(g_sb, g):
    h = MLA_ROPE // 2
    ga, gb = g[:, OFF_KRA:OFF_KRA + LANES], g[:, OFF_KRB:OFF_KRB + LANES]
    gkr = ga[:, MLA_NOPE:MLA_QK] + jnp.concatenate([gb[:, MLA_NOPE + h:MLA_QK], -gb[:, MLA_NOPE:MLA_NOPE + h]], axis=1)
    return jnp.concatenate([g_sb, g[:, OFF_Z:OFF_Z + 512], g[:, OFF_XBC:OFF_XBC + 768],
                            g[:, OFF_DT:OFF_DT + 8], g[:, OFF_CQ:OFF_CQ + 256], g[:, OFF_CKV:OFF_CKV + 128], gkr], axis=1)


def _pack_w_uq(w):
    zeros = lambda n: jnp.zeros((w.shape[0], n), w.dtype)
    h = MLA_ROPE // 2
    pp, rr = [], []
    for i in range(MLA_HEADS):
        nope = w[:, MLA_QK * i:MLA_QK * i + MLA_NOPE]
        rope = w[:, MLA_QK * i + MLA_NOPE:MLA_QK * (i + 1)]
        pp += [nope, rope, zeros(LANES - MLA_QK)]
        rr += [zeros(MLA_NOPE), -rope[:, h:], rope[:, :h], zeros(LANES - MLA_QK)]
    return jnp.concatenate(pp, axis=1), jnp.concatenate(rr, axis=1)


def _unpack_gw_uq(gp, gr):
    h = MLA_ROPE // 2
    out = []
    for i in range(MLA_HEADS):
        b = LANES * i
        out.append(gp[:, b:b + MLA_NOPE])
        out.append(gp[:, b + MLA_NOPE:b + MLA_NOPE + h] + gr[:, b + MLA_NOPE + h:b + MLA_QK])
        out.append(gp[:, b + MLA_NOPE + h:b + MLA_QK] - gr[:, b + MLA_NOPE:b + MLA_NOPE + h])
    return jnp.concatenate(out, axis=1)


def _pack_w_ukv(w):
    zeros = lambda n: jnp.zeros((w.shape[0], n), w.dtype)
    kk, vv = [], []
    for i in range(MLA_HEADS):
        b = (MLA_NOPE + MLA_V) * i
        kk += [w[:, b:b + MLA_NOPE], zeros(LANES - MLA_NOPE)]
        vv += [w[:, b + MLA_NOPE:b + MLA_NOPE + MLA_V], zeros(LANES - MLA_V)]
    return jnp.concatenate(kk, axis=1), jnp.concatenate(vv, axis=1)


def _unpack_gw_ukv(gk, gv):
    out = []
    for i in range(MLA_HEADS):
        out += [gk[:, LANES * i:LANES * i + MLA_NOPE], gv[:, LANES * i:LANES * i + MLA_V]]
    return jnp.concatenate(out, axis=1)


def _rope_tables(positions):
    inv_freq = 1.0 / (ROPE_THETA ** (jnp.arange(0, MLA_ROPE, 2, dtype=F32) / MLA_ROPE))
    ang = positions.astype(F32)[:, None] * inv_freq
    cos, sin = jnp.cos(ang), jnp.sin(ang)
    T = positions.shape[0]
    one, zero = jnp.ones((T, MLA_NOPE), F32), jnp.zeros((T, MLA_NOPE), F32)
    pad1, pad0 = jnp.ones((T, LANES - MLA_QK), F32), jnp.zeros((T, LANES - MLA_QK), F32)
    return jnp.concatenate([one, cos, cos, pad1], axis=1), jnp.concatenate([zero, sin, sin, pad0], axis=1)


def _row(v):
    return v.reshape(1, -1)


def _pad_row(v):
    return _pad_cols(v.reshape(1, -1), LANES)


def _layer_weights(full, small, li):
    p = {}
    p["w_sb"], p["w_rest"] = _pack_w_in(full["w_in"])
    q_scale = jnp.concatenate([jnp.full((1, SB_HEADS * SB_DIM), SB_DIM ** -0.5, BF16),
                               jnp.ones((1, 2 * SB_HEADS * SB_DIM), BF16)], axis=1)
    p["w_sb_fwd"] = p["w_sb"] * q_scale
    p["wqp"], p["wqr"] = _pack_w_uq(full["mla_w_uq"])
    p["wkp"], p["wvp"] = _pack_w_ukv(full["mla_w_ukv"])
    p["w_out"] = full["w_out"]
    p["w_up"] = full["ffn_w_up"]
    p["w_down"] = full["ffn_w_down"]
    for n in ("mix_norm", "sb_out_norm", "ssm_conv_b", "ssm_out_norm", "mla_q_norm", "mla_kv_norm", "mla_out_norm",
              "ffn_norm", "ffn_conv_b"):
        p[n] = _row(small[n][li])
    for n in ("ssm_dt_bias", "ssm_a_log", "ssm_d"):
        p[n] = _pad_row(small[n][li])
    p["ssm_conv_w"] = small["ssm_conv_w"][li]
    p["ffn_conv_w"] = small["ffn_conv_w"][li]
    return p


def _layer_fwd(h, p, cos, sin, li, exchange=None):
    T = h.shape[0]
    nm = lambda s: "l%d_%s" % (li, s)
    s = {"h": h}
    (n1,) = rowwise(rms_fn, [h], [p["mix_norm"]], [(D_MODEL, BF16)], name=nm("mix_norm"))
    proj = matmul(n1, p["w_rest"], name=nm("in_proj"))
    qkv = matmul(n1, p["w_sb_fwd"], name=nm("in_proj_sb"), out_dtype=BF16)
    s["n1"], s["proj"] = n1, proj
    s["sb_q"] = _to_heads_t(qkv[:, 0:256], SB_HEADS)
    s["sb_k"] = _to_heads(qkv[:, 256:512], SB_HEADS)
    s["sb_v"] = _to_heads(qkv[:, 512:768], SB_HEADS)
    y_sb_hm, s["sb_bt"], s["sb_first"] = sb_fwd(s["sb_q"], s["sb_k"], s["sb_v"], name=nm("sb_fwd"))
    s["y_sb"] = _from_heads_t(y_sb_hm)
    s["x_hm"], s["b_hm"], s["c_hm"] = ssm_conv_act(proj, p["ssm_conv_w"], p["ssm_conv_b"], name=nm("ssm_conv"))
    s["y_ssm"], s["states"] = ssd_fwd(s["x_hm"], s["b_hm"], s["c_hm"], proj, p["ssm_dt_bias"], p["ssm_a_log"],
                                      p["ssm_d"], name=nm("ssd_fwd"))
    rows = [(proj, 256, OFF_CQ // 256), (proj, 128, OFF_CKV // 128), (proj, 128, OFF_KRA // 128),
            (proj, 128, OFF_KRB // 128), cos, sin]
    qp, kp, vv = rowwise(mla_prep_fn, rows, [p["mla_q_norm"], p["mla_kv_norm"], p["wqp"], p["wqr"], p["wkp"], p["wvp"]],
                         [(512, BF16), (512, BF16), (512, BF16)], name=nm("mla_prep"))
    s["mla_q"], s["mla_k"], s["mla_v"] = _to_heads_t(qp, MLA_HEADS), kp, vv
    s["mla_o"], s["mla_lse"], *rode = mla_fwd(s["mla_q"], kp, vv, name=nm("mla_fwd"), exchange=exchange)
    s["y_mla"] = _from_heads_t(s["mla_o"][:, :MLA_V, :])
    (cat,) = rowwise(merge_fn, [s["y_sb"], s["y_ssm"], (proj, 512, OFF_Z // 512), s["y_mla"]],
                     [p["sb_out_norm"], p["ssm_out_norm"], p["mla_out_norm"]], [(D_MODEL, BF16)], name=nm("merge"),
                     post=lambda a, b, c: (jnp.concatenate([a, b, c], axis=1),))
    s["cat"] = cat
    h1 = matmul(cat, p["w_out"], name=nm("out_proj"), residual=h)
    s["h1"] = h1
    (n2,) = rowwise(rms_fn, [h1], [p["ffn_norm"]], [(D_MODEL, BF16)], name=nm("ffn_norm"))
    up = matmul(n2, p["w_up"], name=nm("ffn_up"))
    act = ffn_act(up, p["ffn_conv_w"], p["ffn_conv_b"], name=nm("ffn_act"))
    s["n2"], s["up"], s["act"] = n2, up, act
    h2 = matmul(act, p["w_down"], name=nm("ffn_down"), residual=h1)
    return h2, s, (rode[0] if rode else None)


def _layer_bwd(dh2, s, p, cos, sin, li, exchange=None):
    nm = lambda t: "l%d_%s" % (li, t)
    g = {}
    proj = s["proj"]
    g["ffn_w_down"] = matmul(s["act"], dh2, name=nm("g_w_down"), ta=True)
    d_act = matmul(dh2, p["w_down"], name=nm("d_act"), out_dtype=BF16, tb=True)
    d_up_g, d_up_v, gwg, gwv, gbg, gbv = ffn_bwd_fused(s["up"], p["ffn_conv_w"], p["ffn_conv_b"], d_act,
                                                       name=nm("ffn_act_bwd"))
    g["ffn_conv_w"] = jnp.concatenate([gwg, gwv], axis=1)
    g["ffn_conv_b"] = jnp.concatenate([gbg[0], gbv[0]])
    g["ffn_w_up"] = jnp.concatenate([matmul(s["n2"], d_up_g, name=nm("g_w_up_gate"), ta=True),
                                     matmul(s["n2"], d_up_v, name=nm("g_w_up_val"), ta=True)], axis=1)
    d_n2 = matmul(d_up_g, p["w_up"], name=nm("d_n2_gate"), tb=True)
    d_n2 = matmul(d_up_v, p["w_up"], name=nm("d_n2_val"), tb=True, b_k0=D_FF, residual=d_n2)
    (dh1,), (gn,) = rowwise_bwd(rms_fn, [s["h1"]], [], [p["ffn_norm"]], [d_n2], [F32], name=nm("ffn_norm_bwd"),
                                add0=dh2)
    g["ffn_norm"] = gn[0]
    g["w_out"] = matmul(s["cat"], dh1, name=nm("g_w_out"), ta=True)
    d_cat = matmul(dh1, p["w_out"], name=nm("d_cat"), tb=True)
    (d_ysb, d_yssm, d_z, d_ymla), (g1, g2, g3) = rowwise_bwd(
        merge_fn, [s["y_sb"], s["y_ssm"], (proj, 512, OFF_Z // 512), s["y_mla"]], [],
        [p["sb_out_norm"], p["ssm_out_norm"], p["mla_out_norm"]], [d_cat], [F32, F32, BF16, F32], name=nm("merge_bwd"),
        pre_ct=lambda d: (d[:, 0:256], d[:, 256:768], d[:, 768:1024]))
    g["sb_out_norm"], g["ssm_out_norm"], g["mla_out_norm"] = g1[0], g2[0], g3[0]
    dq, dk, dv = sb_bwd(s["sb_q"], s["sb_k"], s["sb_v"], _to_heads_t(d_ysb, SB_HEADS), s["sb_bt"], s["sb_first"], name=nm("sb_bwd"),
                        q_scale=SB_DIM ** -0.5)
    d_sb = jnp.concatenate([_from_heads_t(dq), _from_heads(dk), _from_heads(dv)], axis=1).astype(BF16)
    do_t = jnp.pad(_to_heads_t(d_ymla, MLA_HEADS), ((0, 0), (0, LANES - MLA_V), (0, 0)))
    dqp, dkp, dvv, *rode = mla_bwd(s["mla_q"], s["mla_k"], s["mla_v"], do_t, s["mla_o"], s["mla_lse"],
                                   name=nm("mla_bwd"), exchange=exchange)
    rows = [(proj, 256, OFF_CQ // 256), (proj, 128, OFF_CKV // 128), (proj, 128, OFF_KRA // 128),
            (proj, 128, OFF_KRB // 128)]
    (d_cq, d_ckv, d_kra, d_krb), (gqn, gkvn, gwqp, gwqr, gwkp, gwvp) = rowwise_bwd(
        mla_prep_fn, rows, [cos, sin], [p["mla_q_norm"], p["mla_kv_norm"], p["wqp"], p["wqr"], p["wkp"], p["wvp"]],
        [_from_heads_t(dqp), dkp, dvv], [BF16] * 4, name=nm("mla_prep_bwd"), tile=256)
    g["mla_q_norm"], g["mla_kv_norm"] = gqn[0], gkvn[0]
    g["mla_w_uq"] = _unpack_gw_uq(gwqp, gwqr)
    g["mla_w_ukv"] = _unpack_gw_ukv(gwkp, gwvp)
    d_xbc_act, d_dt, gdb, gal, gds = ssd_bwd(
        s["x_hm"], s["b_hm"], s["c_hm"], proj, s["states"], p["ssm_dt_bias"], p["ssm_a_log"], p["ssm_d"],
        _to_heads(d_yssm, SSM_HEADS), name=nm("ssd_bwd"))
    g["ssm_dt_bias"], g["ssm_a_log"], g["ssm_d"] = gdb[0, :8], gal[0, :8], gds[0, :8]
    d_pre = ssm_conv_bwd_a(proj, p["ssm_conv_w"], p["ssm_conv_b"], d_xbc_act, name=nm("ssm_conv_bwd_a"))
    d_xbc, g["ssm_conv_w"], gscb = conv_bwd_b(d_pre, proj, OFF_XBC, p["ssm_conv_w"], name=nm("ssm_conv_bwd_b"),
                                              out_dtype=BF16, tc=256)
    g["ssm_conv_b"] = gscb[0]
    d_proj = jnp.concatenate([d_z, d_xbc, d_cq, d_ckv, d_dt.astype(BF16), d_kra, d_krb], axis=1)
    g["w_in"] = _unpack_gw_in(matmul(s["n1"], d_sb, name=nm("g_w_in_sb"), ta=True),
                              matmul(s["n1"], d_proj, name=nm("g_w_in"), ta=True))
    d_n1 = matmul(d_sb, p["w_sb"], name=nm("d_n1_sb"), tb=True)
    d_n1 = matmul(d_proj, p["w_rest"], name=nm("d_n1"), tb=True, residual=d_n1)
    (dh0,), (gm,) = rowwise_bwd(rms_fn, [s["h"]], [], [p["mix_norm"]], [d_n1], [F32], name=nm("mix_norm_bwd"),
                                add0=dh1)
    g["mix_norm"] = gm[0]
    return dh0, g, (rode[0] if rode else None)


def kernel(x, positions, mix_norm, w_in, sb_out_norm, ssm_conv_w, ssm_conv_b, ssm_dt_bias, ssm_a_log, ssm_d, ssm_out_norm, mla_q_norm, mla_w_uq, mla_kv_norm, mla_w_ukv, mla_out_norm, w_out, ffn_norm, ffn_w_up, ffn_conv_w, ffn_conv_b, ffn_w_down, final_norm, loss_target, m_mix_norm, m_w_in, m_sb_out_norm, m_ssm_conv_w, m_ssm_conv_b, m_ssm_dt_bias, m_ssm_a_log, m_ssm_d, m_ssm_out_norm, m_mla_q_norm, m_mla_w_uq, m_mla_kv_norm, m_mla_w_ukv, m_mla_out_norm, m_w_out, m_ffn_norm, m_ffn_w_up, m_ffn_conv_w, m_ffn_conv_b, m_ffn_w_down, m_final_norm, v_mix_norm, v_w_in, v_sb_out_norm, v_ssm_conv_w, v_ssm_conv_b, v_ssm_dt_bias, v_ssm_a_log, v_ssm_d, v_ssm_out_norm, v_mla_q_norm, v_mla_w_uq, v_mla_kv_norm, v_mla_w_ukv, v_mla_out_norm, v_w_out, v_ffn_norm, v_ffn_w_up, v_ffn_conv_w, v_ffn_conv_b, v_ffn_w_down, v_final_norm):
    W = dict(mix_norm=mix_norm, w_in=w_in, sb_out_norm=sb_out_norm, ssm_conv_w=ssm_conv_w, ssm_conv_b=ssm_conv_b,
             ssm_dt_bias=ssm_dt_bias, ssm_a_log=ssm_a_log, ssm_d=ssm_d, ssm_out_norm=ssm_out_norm,
             mla_q_norm=mla_q_norm, mla_w_uq=mla_w_uq, mla_kv_norm=mla_kv_norm, mla_w_ukv=mla_w_ukv,
             mla_out_norm=mla_out_norm, w_out=w_out, ffn_norm=ffn_norm, ffn_w_up=ffn_w_up, ffn_conv_w=ffn_conv_w,
             ffn_conv_b=ffn_conv_b, ffn_w_down=ffn_w_down, final_norm=final_norm)
    M = dict(mix_norm=m_mix_norm, w_in=m_w_in, sb_out_norm=m_sb_out_norm, ssm_conv_w=m_ssm_conv_w,
             ssm_conv_b=m_ssm_conv_b, ssm_dt_bias=m_ssm_dt_bias, ssm_a_log=m_ssm_a_log, ssm_d=m_ssm_d,
             ssm_out_norm=m_ssm_out_norm, mla_q_norm=m_mla_q_norm, mla_w_uq=m_mla_w_uq, mla_kv_norm=m_mla_kv_norm,
             mla_w_ukv=m_mla_w_ukv, mla_out_norm=m_mla_out_norm, w_out=m_w_out, ffn_norm=m_ffn_norm,
             ffn_w_up=m_ffn_w_up, ffn_conv_w=m_ffn_conv_w, ffn_conv_b=m_ffn_conv_b, ffn_w_down=m_ffn_w_down,
             final_norm=m_final_norm)
    V = dict(mix_norm=v_mix_norm, w_in=v_w_in, sb_out_norm=v_sb_out_norm, ssm_conv_w=v_ssm_conv_w,
             ssm_conv_b=v_ssm_conv_b, ssm_dt_bias=v_ssm_dt_bias, ssm_a_log=v_ssm_a_log, ssm_d=v_ssm_d,
             ssm_out_norm=v_ssm_out_norm, mla_q_norm=v_mla_q_norm, mla_w_uq=v_mla_w_uq, mla_kv_norm=v_mla_kv_norm,
             mla_w_ukv=v_mla_w_ukv, mla_out_norm=v_mla_out_norm, w_out=v_w_out, ffn_norm=v_ffn_norm,
             ffn_w_up=v_ffn_w_up, ffn_conv_w=v_ffn_conv_w, ffn_conv_b=v_ffn_conv_b, ffn_w_down=v_ffn_w_down,
             final_norm=v_final_norm)
    depth = mix_norm.shape[0]
    cx, cy, cc = _place()
    chip = 2 * cx + cy
    T = x.shape[1]

    assert depth == 2
    shard_shapes = {n: W[n].shape[1:] for n in BIG}

    def layer_of(d, li):
        return {n: d[n][li] for n in BIG}

    def assemble(g8):
        return _from_blocks(g8, shard_shapes)

    full0 = assemble(allgather_blocks(_flat_half(layer_of(W, 0), cc, BF16), name="gather_weights_l0"))
    conv_full = {}
    small = {n: W[n] for n in SMALL_REPL}
    cw_flat = jnp.concatenate([W[n].reshape(-1) for n in SMALL_SHARD])
    cw_rows = -(-cw_flat.shape[0] // (8 * LANES)) * 8
    cw_all = allgather_direct(jnp.pad(cw_flat, (0, cw_rows * LANES - cw_flat.shape[0])).reshape(cw_rows, LANES),
                              name="gather_conv_taps")
    off = 0
    for n in SMALL_SHARD:
        sz = W[n].size
        conv_full[n] = jnp.concatenate(
            [cw_all[2 * k].reshape(-1)[off:off + sz].reshape(W[n].shape) for k in range(4)], axis=2)
        off += sz
    small.update(conv_full)

    cos, sin = _rope_tables(positions[0])
    params0 = _layer_weights(full0, small, 0)
    h, s0, g8 = _layer_fwd(x[0], params0, cos, sin, 0, exchange=("gather", _flat_half(layer_of(W, 1), cc, BF16)))
    params1 = _layer_weights(assemble(g8), small, 1)
    h, s1, _ = _layer_fwd(h, params1, cos, sin, 1)
    dh, g_final, loss_lanes = loss_head(h, loss_target[0], _row(final_norm), name="loss_head")

    dh, g1, _ = _layer_bwd(dh, s1, params1, cos, sin, 1)
    by_dest = _to_blocks(g1, shard_shapes, BF16)
    dh, g0, from_all = _layer_bwd(dh, s0, params0, cos, sin, 0, exchange=("all_to_all", by_dest))
    grad_x = dh[None]
    grads = [g0, g1]
    G = {n: jnp.stack([grads[li][n] for li in range(depth)]) for n in WEIGHTS if n != "final_norm" and n not in BIG}
    G["final_norm"] = g_final[0]
    half1 = add_blocks([(from_all, d) for d in range(8)], F32, name="grads_l1_sum")

    blocks0 = _to_blocks(g0, shard_shapes, BF16)
    R = blocks0.shape[1]
    blocks0 = blocks0.reshape(4, 2, R, LANES)
    mine_first = lax.dynamic_index_in_dim(blocks0, cc, 1, keepdims=False)
    for_sibling = lax.dynamic_index_in_dim(blocks0, 1 - cc, 1, keepdims=False)
    from_sibling = send_to_sibling(for_sibling.reshape(4 * R, LANES), name="grads_to_sibling")
    pair = add_blocks([mine_first.reshape(4 * R, LANES), from_sibling], BF16, name="grads_pair_sum").reshape(4, R, LANES)
    others = chip_exchange(pair, name="grads_chip_exchange")
    own = lax.dynamic_index_in_dim(pair, chip, 0, keepdims=False)
    half0 = add_blocks([own, (others, 0), (others, 1), (others, 2)], F32, name="grads_chip_sum")
    half = jnp.concatenate([half0, half1])
    other = send_to_sibling(half, name="grads_pair_swap")
    by_core = [jnp.where(cc == 0, half, other), jnp.where(cc == 0, other, half)]
    g_big_l = [_unflat_halves([a[li * R:(li + 1) * R].reshape(-1) for a in by_core], shard_shapes) for li in range(depth)]
    g_big = {n: jnp.stack([g_big_l[li][n] for li in range(depth)]) for n in BIG}

    small_list = [G[n].reshape(-1) for n in SMALL_REPL] + [G[n].reshape(-1) for n in SMALL_SHARD]
    small_list.append(jnp.sum(loss_lanes).reshape(1))
    sm = jnp.concatenate(small_list)
    n_small = sm.shape[0]
    sm_rows = -(-n_small // (16 * LANES)) * 16
    sm_all = allgather_direct(jnp.pad(sm, (0, sm_rows * LANES - n_small)).reshape(sm_rows, LANES), name="gather_small")
    sm_sum = add_blocks([(sm_all, d) for d in range(8)], F32, name="small_sum").reshape(-1)
    g_small, off = {}, 0
    for n in SMALL_REPL:
        g_small[n] = sm_sum[off:off + W[n].size].reshape(W[n].shape)
        off += W[n].size
    for n in SMALL_SHARD:
        full_shape = conv_full[n].shape
        sz = conv_full[n].size
        gfull = sm_sum[off:off + sz].reshape(full_shape)
        width = W[n].shape[2]
        g_small[n] = lax.dynamic_slice_in_dim(gfull, chip * width, width, axis=2)
        off += sz
    loss = sm_sum[off]

    grad_out, delta, new_m, new_v = {}, {}, {}, {}
    for n in BIG:
        shp = W[n].shape
        two_d = lambda a: a.reshape(shp[0] * shp[1], shp[2])
        d, nm_, nv_ = adamw(two_d(W[n]), two_d(g_big[n]), two_d(M[n]), two_d(V[n]), name="adamw_" + n)
        grad_out[n], delta[n], new_m[n], new_v[n] = g_big[n], d.reshape(shp), nm_.reshape(shp), nv_.reshape(shp)
    small_names = SMALL_REPL + SMALL_SHARD

    def flat_small(d):
        f = jnp.concatenate([d[n].reshape(-1) for n in small_names])
        rows = -(-f.shape[0] // (8 * LANES)) * 8
        return jnp.pad(f, (0, rows * LANES - f.shape[0])).reshape(rows, LANES)

    vpad = flat_small(V)
    d, nm_, nv_ = adamw(flat_small(W), flat_small(g_small), flat_small(M), vpad, name="adamw_small")
    off = 0
    for n in small_names:
        sz = W[n].size
        grad_out[n] = g_small[n]
        delta[n] = d.reshape(-1)[off:off + sz].reshape(W[n].shape)
        new_m[n] = nm_.reshape(-1)[off:off + sz].reshape(W[n].shape)
        new_v[n] = nv_.reshape(-1)[off:off + sz].reshape(W[n].shape)
        off += sz

    return (loss, grad_x, *[grad_out[n] for n in WEIGHTS], *[delta[n] for n in WEIGHTS],
            *[new_m[n] for n in WEIGHTS], *[new_v[n] for n in WEIGHTS])
```

```python
import functools
import math

import jax
import jax.numpy as jnp
from jax import lax
from jax.experimental import pallas as pl
from jax.experimental.pallas import tpu as pltpu

F32 = jnp.float32
BF16 = jnp.bfloat16

EPS = 1e-6
D_MODEL = 1024
SB_HEADS, SB_DIM = 4, 64
SSM_HEADS, SSM_DIM, SSM_GROUPS, SSM_STATE, SSM_CHUNK = 8, 64, 2, 64, 128
SSM_INNER = SSM_HEADS * SSM_DIM
SSM_CONV_DIM = SSM_INNER + 2 * SSM_GROUPS * SSM_STATE
MLA_HEADS, MLA_NOPE, MLA_ROPE, MLA_V = 4, 64, 32, 64
MLA_QK = MLA_NOPE + MLA_ROPE
MLA_SCALE = MLA_QK ** -0.5
ROPE_THETA = 10000.0
D_FF = 2816
IN_SPLITS = (768, 512, 768, 8, 256, 128, 32)

OFF_Z, OFF_XBC, OFF_CQ, OFF_CKV, OFF_DT, OFF_KRA, OFF_KRB = 0, 512, 1280, 1536, 1664, 1792, 1920
D_REST = 2048
LANES = 128

ADAM_LR, ADAM_B1, ADAM_B2, ADAM_EPS, ADAM_WD, ADAM_STEP = 0.001, 0.9, 0.999, 1e-08, 0.01, 10

V7X_VMEM_LIMIT = 48 * 1024 * 1024

NT = (((1,), (1,)), ((), ()))
TN = (((0,), (0,)), ((), ()))

BIG = ("w_in", "mla_w_uq", "mla_w_ukv", "w_out", "ffn_w_up", "ffn_w_down")
BIG_AXIS = {"w_in": 2, "mla_w_uq": 2, "mla_w_ukv": 2, "w_out": 1, "ffn_w_up": 2, "ffn_w_down": 1}
SMALL_REPL = ("mix_norm", "sb_out_norm", "ssm_conv_b", "ssm_dt_bias", "ssm_a_log", "ssm_d", "ssm_out_norm",
              "mla_q_norm", "mla_kv_norm", "mla_out_norm", "ffn_norm", "ffn_conv_b", "final_norm")
SMALL_SHARD = ("ssm_conv_w", "ffn_conv_w")
WEIGHTS = ("mix_norm", "w_in", "sb_out_norm", "ssm_conv_w", "ssm_conv_b", "ssm_dt_bias", "ssm_a_log", "ssm_d",
           "ssm_out_norm", "mla_q_norm", "mla_w_uq", "mla_kv_norm", "mla_w_ukv", "mla_out_norm", "w_out", "ffn_norm",
           "ffn_w_up", "ffn_conv_w", "ffn_conv_b", "ffn_w_down", "final_norm")


def _cparams(*sem):
    return pltpu.CompilerParams(dimension_semantics=sem if sem else None, vmem_limit_bytes=V7X_VMEM_LIMIT)


def _pick(n, target, mult=LANES):
    best = None
    for d in range(mult, min(n, target) + 1, mult):
        if n % d == 0:
            best = d
    return best or n


def _sigmoid(x):
    return 1.0 / (1.0 + jnp.exp(-x))


def _softplus(x):
    ax = jnp.where(x > 0, x, -x)
    return jnp.where(x > 0, x, 0.0) + jnp.log(1.0 + jnp.exp(-ax))


def _rms(x, g):
    return x * lax.rsqrt(jnp.mean(x * x, axis=-1, keepdims=True) + EPS) * g


def _raw_nn(a, b):
    return jnp.dot(a.astype(BF16), b.astype(BF16), preferred_element_type=F32)


def _raw_nt(a, b):
    return lax.dot_general(a.astype(BF16), b.astype(BF16), NT, preferred_element_type=F32)


def _raw_tn(a, b):
    return lax.dot_general(a.astype(BF16), b.astype(BF16), TN, preferred_element_type=F32)


@jax.custom_vjp
def mm_nn(a, b):
    return _raw_nn(a, b)


mm_nn.defvjp(lambda a, b: (_raw_nn(a, b), (a, b)),
             lambda r, ct: (_raw_nt(ct, r[1]), _raw_tn(r[0], ct)))


@jax.custom_vjp
def mm_nt(a, b):
    return _raw_nt(a, b)


mm_nt.defvjp(lambda a, b: (_raw_nt(a, b), (a, b)),
             lambda r, ct: (_raw_nn(ct, r[1]), _raw_tn(ct, r[0])))


@jax.custom_vjp
def mm_tn(a, b):
    return _raw_tn(a, b)


mm_tn.defvjp(lambda a, b: (_raw_tn(a, b), (a, b)),
             lambda r, ct: (_raw_nt(r[1], ct), _raw_nn(r[0], ct)))


def _split_dot(x, m, terms):
    acc = None
    r = x
    for t in range(terms):
        xt = r.astype(BF16)
        d = jnp.dot(xt, m, preferred_element_type=F32)
        acc = d if acc is None else acc + d
        if t + 1 < terms:
            r = r - xt.astype(F32)
    return acc


def _tri_dot(tri, x, terms=3):
    parts = []
    r = x
    for t in range(terms):
        xt = r.astype(BF16)
        parts.append(xt)
        if t + 1 < terms:
            r = r - xt.astype(F32)
    return jnp.dot(jnp.concatenate([tri] * terms, axis=1), jnp.concatenate(parts, axis=0),
                   preferred_element_type=F32)


def _tri(n, cmp):
    r = lax.broadcasted_iota(jnp.int32, (n, n), 0)
    c = lax.broadcasted_iota(jnp.int32, (n, n), 1)
    return cmp(r, c).astype(BF16)


@jax.custom_vjp
def csum_rows(x):
    return _tri_dot(_tri(x.shape[0], lambda r, c: r >= c), x)


csum_rows.defvjp(lambda x: (csum_rows(x), None),
                 lambda _, ct: (_tri_dot(_tri(ct.shape[0], lambda r, c: r <= c), ct),))


def matmul(a, b, *, name, out_dtype=F32, ta=False, tb=False, b_k0=0, residual=None):
    if ta:
        K, M = a.shape
    else:
        M, K = a.shape
    N = b.shape[0] if tb else b.shape[1]
    tm = _pick(M, 1408)
    tn = _pick(N, 1408)
    tk = _pick(K, 1408)
    nk = K // tk
    kb0 = b_k0 // tk
    assert b_k0 % tk == 0 and (tb or b_k0 == 0)
    has_res = residual is not None

    def body(*refs):
        a_ref, b_ref = refs[:2]
        r_ref = refs[2] if has_res else None
        o_ref = refs[3] if has_res else refs[2]
        k = pl.program_id(2)
        av = a_ref[...].astype(BF16)
        bv = b_ref[...].astype(BF16)
        if ta:
            prod = lax.dot_general(av, bv, TN, preferred_element_type=F32)
        elif tb:
            prod = lax.dot_general(av, bv, NT, preferred_element_type=F32)
        else:
            prod = jnp.dot(av, bv, preferred_element_type=F32)

        def finish(total):
            if has_res:
                total = total + r_ref[...].astype(F32)
            o_ref[...] = total.astype(o_ref.dtype)

        if nk == 1:
            finish(prod)
            return
        acc = refs[-1]

        @pl.when(k == 0)
        def _():
            acc[...] = prod

        @pl.when(jnp.logical_and(k > 0, k < nk - 1))
        def _():
            acc[...] += prod

        @pl.when(k == nk - 1)
        def _():
            finish(acc[...] + prod)

    a_spec = pl.BlockSpec((tk, tm), lambda i, j, k: (k, i)) if ta else pl.BlockSpec((tm, tk), lambda i, j, k: (i, k))
    b_spec = pl.BlockSpec((tn, tk), lambda i, j, k: (j, kb0 + k)) if tb else pl.BlockSpec((tk, tn), lambda i, j, k: (k, j))
    in_specs = [a_spec, b_spec]
    args = [a, b]
    if has_res:
        in_specs.append(pl.BlockSpec((tm, tn), lambda i, j, k: (i, j)))
        args.append(residual)
    return pl.pallas_call(
        body, name=name, grid=(M // tm, N // tn, nk),
        in_specs=in_specs, out_specs=pl.BlockSpec((tm, tn), lambda i, j, k: (i, j)),
        out_shape=jax.ShapeDtypeStruct((M, N), out_dtype),
        scratch_shapes=[pltpu.VMEM((tm, tn), F32)] if nk > 1 else [],
        compiler_params=_cparams("parallel", "parallel", "arbitrary"),
    )(*args)


def _row_spec(entry, tl):
    if isinstance(entry, tuple):
        arr, width, cb = entry
        return arr, pl.BlockSpec((tl, width), lambda i, cb=cb: (i, cb))
    return entry, pl.BlockSpec((tl, entry.shape[1]), lambda i: (i, 0))


def _rows_T(entry):
    return (entry[0] if isinstance(entry, tuple) else entry).shape[0]


def rowwise(fn, rows, params, outs, *, name, tile=512, post=None):
    T = _rows_T(rows[0])
    tl = min(T, tile)
    nr, npar = len(rows), len(params)

    def body(*refs):
        r = [ref[...].astype(F32) for ref in refs[:nr]]
        p = [ref[...].astype(F32) for ref in refs[nr:nr + npar]]
        res = fn(*r, *p)
        if post is not None:
            res = post(*res)
        for o_ref, val in zip(refs[nr + npar:], res):
            o_ref[...] = val.astype(o_ref.dtype)

    arrs, specs = [], []
    for e in rows:
        a, s = _row_spec(e, tl)
        arrs.append(a)
        specs.append(s)
    for p in params:
        arrs.append(p)
        specs.append(pl.BlockSpec(p.shape, lambda i: (0, 0)))
    res = pl.pallas_call(
        body, name=name, grid=(T // tl,), in_specs=specs,
        out_specs=[pl.BlockSpec((tl, c), lambda i: (i, 0)) for c, _ in outs],
        out_shape=[jax.ShapeDtypeStruct((T, c), dt) for c, dt in outs],
        compiler_params=_cparams("parallel"),
    )(*arrs)
    return res


def rowwise_bwd(fn, rows, nd_rows, params, cts, grad_dtypes, *, name, tile=512, pre_ct=None, add0=None):
    T = _rows_T(rows[0])
    tl = min(T, tile)
    nr, nn, npar, nc = len(rows), len(nd_rows), len(params), len(cts)
    has_add = add0 is not None

    def body(*refs):
        pos = 0
        r = [ref[...].astype(F32) for ref in refs[pos:pos + nr]]
        pos += nr
        nd = [ref[...].astype(F32) for ref in refs[pos:pos + nn]]
        pos += nn
        p = [ref[...].astype(F32) for ref in refs[pos:pos + npar]]
        pos += npar
        c = [ref[...].astype(F32) for ref in refs[pos:pos + nc]]
        pos += nc
        if has_add:
            addv = refs[pos][...].astype(F32)
            pos += 1
        rg_refs = refs[pos:pos + nr]
        pg_refs = refs[pos + nr:pos + nr + npar]
        if pre_ct is not None:
            c = list(pre_ct(*c))
        _, vjp = jax.vjp(lambda *a: fn(*a[:nr], *nd, *a[nr:]), *r, *p)
        g = vjp(tuple(c))
        for j, ref in enumerate(rg_refs):
            val = g[j]
            if has_add and j == 0:
                val = val + addv
            ref[...] = val.astype(ref.dtype)
        if npar:
            @pl.when(pl.program_id(0) == 0)
            def _():
                for ref in pg_refs:
                    ref[...] = jnp.zeros_like(ref)
            for j, ref in enumerate(pg_refs):
                ref[...] += g[nr + j]

    arrs, specs = [], []
    widths = []
    for e in list(rows) + list(nd_rows):
        a, s = _row_spec(e, tl)
        arrs.append(a)
        specs.append(s)
        widths.append(s.block_shape[1])
    for p in params:
        arrs.append(p)
        specs.append(pl.BlockSpec(p.shape, lambda i: (0, 0)))
    for e in cts:
        a, s = _row_spec(e, tl)
        arrs.append(a)
        specs.append(s)
    if has_add:
        a, s = _row_spec(add0, tl)
        arrs.append(a)
        specs.append(s)
    out_specs = [pl.BlockSpec((tl, widths[j]), lambda i: (i, 0)) for j in range(nr)]
    out_shape = [jax.ShapeDtypeStruct((T, widths[j]), grad_dtypes[j]) for j in range(nr)]
    out_specs += [pl.BlockSpec(p.shape, lambda i: (0, 0)) for p in params]
    out_shape += [jax.ShapeDtypeStruct(p.shape, F32) for p in params]
    res = pl.pallas_call(
        body, name=name, grid=(T // tl,), in_specs=specs, out_specs=out_specs, out_shape=out_shape,
        compiler_params=_cparams("arbitrary"),
    )(*arrs)
    return list(res[:nr]), list(res[nr:])


def rms_fn(h, g):
    return (_rms(h, g),)


def merge_fn(ysb, yssm, z, ymla, g_sb, g_ssm, g_mla):
    ya = _rms(ysb, g_sb)
    yb = _rms(yssm * (z * _sigmoid(z)), g_ssm)
    yc = _rms(ymla, g_mla)
    return ya, yb, yc


def mla_prep_fn(cq, ckv, kra, krb, cos, sin, qn, kvn, wqp, wqr, wkp, wvp):
    cos4 = jnp.concatenate([cos] * MLA_HEADS, axis=1)
    sin4 = jnp.concatenate([sin] * MLA_HEADS, axis=1)
    nq = _rms(cq, qn)
    q = (mm_nn(nq, wqp) * cos4 + mm_nn(nq, wqr) * sin4) * MLA_SCALE
    nkv = _rms(ckv, kvn)
    kpe = kra * cos + krb * sin
    k = mm_nn(nkv, wkp) + jnp.concatenate([kpe] * MLA_HEADS, axis=1)
    v = mm_nn(nkv, wvp)
    return q, k, v


HALO = 8


def _prev_halo_spec(tl, tc, col_of):
    return pl.BlockSpec((HALO, tc), lambda i, j: (jnp.maximum(i * (tl // HALO) - 1, 0), col_of(j)))


def _fill_prev(buf, x_ref, halo_ref, i):
    buf[0:HALO, :] = jnp.where(i > 0, halo_ref[...].astype(F32), 0.0)
    buf[HALO:, :] = x_ref[...].astype(F32)


def _conv_from(buf, w_ref, b_ref, K, tl):
    acc = b_ref[...].astype(F32) + jnp.zeros((tl, buf.shape[1]), F32)
    for k in range(K):
        acc = acc + buf[pl.ds(HALO - (K - 1 - k), tl), :] * w_ref[k:k + 1, :].astype(F32)
    return acc


def ssm_conv_act(proj, w, b, *, name, tile=512, tc=256):
    T = proj.shape[0]
    K, C = w.shape
    tl = min(T, tile)
    c0 = OFF_XBC // tc
    nb = C // tc
    hd = SSM_DIM
    per = tc // hd

    def body(*refs):
        xs, halos = refs[0:nb], refs[nb:2 * nb]
        w_ref, b_ref = refs[2 * nb:2 * nb + 2]
        x_out, b_out, c_out = refs[2 * nb + 2:2 * nb + 5]
        bufs = refs[2 * nb + 5:]
        for j in range(nb):
            cols = slice(j * tc, (j + 1) * tc)
            _fill_prev(bufs[j], xs[j], halos[j], pl.program_id(0))
            u = b_ref[:, cols].astype(F32) + jnp.zeros((tl, tc), F32)
            for k in range(K):
                u = u + bufs[j][pl.ds(HALO - (K - 1 - k), tl), :] * w_ref[k:k + 1, cols].astype(F32)
            act = u * _sigmoid(u)
            for hh in range(per):
                piece = act[:, hh * hd:(hh + 1) * hd]
                head = j * per + hh
                if head < SSM_HEADS:
                    x_out[head] = piece
                elif head < SSM_HEADS + SSM_GROUPS:
                    b_out[head - SSM_HEADS] = piece
                else:
                    c_out[head - SSM_HEADS - SSM_GROUPS] = piece

    in_specs = ([pl.BlockSpec((tl, tc), lambda i, j=j: (i, c0 + j)) for j in range(nb)]
                + [pl.BlockSpec((HALO, tc), lambda i, j=j: (jnp.maximum(i * (tl // HALO) - 1, 0), c0 + j)) for j in range(nb)]
                + [pl.BlockSpec((K, C), lambda i: (0, 0)), pl.BlockSpec((1, C), lambda i: (0, 0))])
    return pl.pallas_call(
        body, name=name, grid=(T // tl,), in_specs=in_specs,
        out_specs=[pl.BlockSpec((SSM_HEADS, tl, hd), lambda i: (0, i, 0)),
                   pl.BlockSpec((SSM_GROUPS, tl, hd), lambda i: (0, i, 0)),
                   pl.BlockSpec((SSM_GROUPS, tl, hd), lambda i: (0, i, 0))],
        out_shape=[jax.ShapeDtypeStruct((SSM_HEADS, T, hd), F32), jax.ShapeDtypeStruct((SSM_GROUPS, T, hd), F32),
                   jax.ShapeDtypeStruct((SSM_GROUPS, T, hd), F32)],
        scratch_shapes=[pltpu.VMEM((tl + HALO, tc), F32)] * nb,
        compiler_params=_cparams("parallel"),
    )(*([proj] * (2 * nb)), w, b)


def ssm_conv_bwd_a(proj, w, b, d_out, *, name, tile=512, tc=256):
    T = proj.shape[0]
    K, C = w.shape
    tl = min(T, tile)
    c0 = OFF_XBC // tc

    def body(x_ref, halo_ref, w_ref, b_ref, d_ref, o_ref, buf):
        _fill_prev(buf, x_ref, halo_ref, pl.program_id(0))
        u = _conv_from(buf, w_ref, b_ref, K, tl)
        s = _sigmoid(u)
        o_ref[...] = d_ref[...].astype(F32) * (s * (1.0 + u * (1.0 - s)))

    return pl.pallas_call(
        body, name=name, grid=(T // tl, C // tc),
        in_specs=[pl.BlockSpec((tl, tc), lambda i, j: (i, c0 + j)), _prev_halo_spec(tl, tc, lambda j: c0 + j),
                  pl.BlockSpec((K, tc), lambda i, j: (0, j)), pl.BlockSpec((1, tc), lambda i, j: (0, j)),
                  pl.BlockSpec((tl, tc), lambda i, j: (i, j))],
        out_specs=pl.BlockSpec((tl, tc), lambda i, j: (i, j)),
        out_shape=jax.ShapeDtypeStruct((T, C), F32),
        scratch_shapes=[pltpu.VMEM((tl + HALO, tc), F32)],
        compiler_params=_cparams("parallel", "parallel"),
    )(proj, proj, w, b, d_out)


def ffn_act(up, w, b, *, name, tile=512, tc=1408):
    T = up.shape[0]
    K = w.shape[0]
    tl = min(T, tile)
    nj = D_FF // tc

    def body(xg_ref, hg_ref, xv_ref, hv_ref, wg_ref, wv_ref, bg_ref, bv_ref, o_ref, bufg, bufv):
        i = pl.program_id(0)
        _fill_prev(bufg, xg_ref, hg_ref, i)
        _fill_prev(bufv, xv_ref, hv_ref, i)
        gate = _conv_from(bufg, wg_ref, bg_ref, K, tl)
        val = _conv_from(bufv, wv_ref, bv_ref, K, tl)
        o_ref[...] = (gate * _sigmoid(gate) * val).astype(o_ref.dtype)

    return pl.pallas_call(
        body, name=name, grid=(T // tl, nj),
        in_specs=[pl.BlockSpec((tl, tc), lambda i, j: (i, j)), _prev_halo_spec(tl, tc, lambda j: j),
                  pl.BlockSpec((tl, tc), lambda i, j: (i, nj + j)), _prev_halo_spec(tl, tc, lambda j: nj + j),
                  pl.BlockSpec((K, tc), lambda i, j: (0, j)), pl.BlockSpec((K, tc), lambda i, j: (0, nj + j)),
                  pl.BlockSpec((1, tc), lambda i, j: (0, j)), pl.BlockSpec((1, tc), lambda i, j: (0, nj + j))],
        out_specs=pl.BlockSpec((tl, tc), lambda i, j: (i, j)),
        out_shape=jax.ShapeDtypeStruct((T, D_FF), BF16),
        scratch_shapes=[pltpu.VMEM((tl + HALO, tc), F32), pltpu.VMEM((tl + HALO, tc), F32)],
        compiler_params=_cparams("parallel", "parallel"),
    )(up, up, up, up, w, w, b, b)


def ffn_bwd_fused(up, w, b, d_act, *, name, tile=1024, tc=256):
    T = up.shape[0]
    K = w.shape[0]
    tl = min(T, tile)
    nj = D_FF // tc
    nblk = T // HALO
    ext = tl + HALO

    def body(xg, hgp, hgn, xv, hvp, hvn, wg, wv, bg, bv, d, dn, og, ov, dwg, dwv, dbg, dbv, bufg, bufv, dgb, dvb):
        i = pl.program_id(1)
        last = pl.num_programs(1) - 1

        def fill(buf, x_ref, prev_ref, next_ref):
            buf[0:HALO, :] = jnp.where(i > 0, prev_ref[...].astype(F32), 0.0)
            buf[HALO:HALO + tl, :] = x_ref[...].astype(F32)
            buf[HALO + tl:, :] = jnp.where(i < last, next_ref[...].astype(F32), 0.0)

        def conv_ext(buf, w_ref, b_ref):
            acc = b_ref[...].astype(F32) + jnp.zeros((ext, tc), F32)
            for k in range(K):
                acc = acc + buf[pl.ds(HALO - (K - 1 - k), ext), :] * w_ref[k:k + 1, :].astype(F32)
            return acc

        fill(bufg, xg, hgp, hgn)
        fill(bufv, xv, hvp, hvn)
        gate = conv_ext(bufg, wg, bg)
        val = conv_ext(bufv, wv, bv)
        dd = jnp.concatenate([d[...].astype(F32), jnp.where(i < last, dn[...].astype(F32)[0:HALO], 0.0)], axis=0)
        s = _sigmoid(gate)
        dgb[...] = dd * val * (s * (1.0 + gate * (1.0 - s)))
        dvb[...] = dd * (gate * s)

        @pl.when(i == 0)
        def _():
            for ref in (dwg, dwv, dbg, dbv):
                ref[...] = jnp.zeros_like(ref)

        for dbuf, xbuf, w_ref, o_ref, dw_ref, db_ref in ((dgb, bufg, wg, og, dwg, dbg), (dvb, bufv, wv, ov, dwv, dbv)):
            cur = dbuf[0:tl, :]
            dx = jnp.zeros((tl, tc), F32)
            for k in range(K):
                sft = K - 1 - k
                dx = dx + dbuf[pl.ds(sft, tl), :] * w_ref[k:k + 1, :].astype(F32)
                dw_ref[k:k + 1, :] += jnp.sum(cur * xbuf[pl.ds(HALO - sft, tl), :], axis=0, keepdims=True)
            db_ref[...] += jnp.sum(cur, axis=0, keepdims=True)
            o_ref[...] = dx.astype(o_ref.dtype)

    prev = lambda i: jnp.maximum(i * (tl // HALO) - 1, 0)
    nxt = lambda i: jnp.minimum((i + 1) * (tl // HALO), nblk - 1)

    def x_specs(col):
        return [pl.BlockSpec((tl, tc), lambda j, i: (i, col(j))), pl.BlockSpec((HALO, tc), lambda j, i: (prev(i), col(j))),
                pl.BlockSpec((HALO, tc), lambda j, i: (nxt(i), col(j)))]

    gcol, vcol = (lambda j: j), (lambda j: nj + j)
    in_specs = (x_specs(gcol) + x_specs(vcol)
                + [pl.BlockSpec((K, tc), lambda j, i: (0, j)), pl.BlockSpec((K, tc), lambda j, i: (0, nj + j)),
                   pl.BlockSpec((1, tc), lambda j, i: (0, j)), pl.BlockSpec((1, tc), lambda j, i: (0, nj + j)),
                   pl.BlockSpec((tl, tc), lambda j, i: (i, j)),
                   pl.BlockSpec((2 * HALO, tc), lambda j, i: (jnp.minimum((i + 1) * (tl // (2 * HALO)), nblk // 2 - 1), j))])
    row_out = pl.BlockSpec((tl, tc), lambda j, i: (i, j))
    w_out = pl.BlockSpec((K, tc), lambda j, i: (0, j))
    b_out = pl.BlockSpec((1, tc), lambda j, i: (0, j))
    return pl.pallas_call(
        body, name=name, grid=(nj, T // tl), in_specs=in_specs,
        out_specs=[row_out, row_out, w_out, w_out, b_out, b_out],
        out_shape=[jax.ShapeDtypeStruct((T, D_FF), BF16)] * 2 + [jax.ShapeDtypeStruct((K, D_FF), F32)] * 2
        + [jax.ShapeDtypeStruct((1, D_FF), F32)] * 2,
        scratch_shapes=[pltpu.VMEM((tl + 2 * HALO, tc), F32)] * 2 + [pltpu.VMEM((ext, tc), F32)] * 2,
        compiler_params=_cparams("parallel", "arbitrary"),
    )(up, up, up, up, up, up, w, w, b, b, d_act, d_act)


def conv_bwd_b(du, x, x_off, w, *, name, out_dtype, tile=512, tc=256):
    T, C = du.shape
    K = w.shape[0]
    tl = min(T, tile)
    c0 = x_off // tc
    nblk = T // HALO

    def body(du_ref, nx_ref, x_ref, w_ref, dx_ref, dw_ref, db_ref, dbuf):
        i = pl.program_id(1)
        last = pl.num_programs(1) - 1
        d = du_ref[...].astype(F32)
        dbuf[0:tl, :] = d
        dbuf[tl:, :] = jnp.where(i < last, nx_ref[...].astype(F32), 0.0)

        @pl.when(i == 0)
        def _():
            dw_ref[...] = jnp.zeros_like(dw_ref)
            db_ref[...] = jnp.zeros_like(db_ref)

        xin = x_ref[...].astype(F32)
        dx = jnp.zeros((tl, tc), F32)
        for k in range(K):
            s = K - 1 - k
            shifted = dbuf[pl.ds(s, tl), :]
            dx = dx + shifted * w_ref[k:k + 1, :].astype(F32)
            dw_ref[k:k + 1, :] += jnp.sum(shifted * xin, axis=0, keepdims=True)
        db_ref[...] += jnp.sum(d, axis=0, keepdims=True)
        dx_ref[...] = dx.astype(dx_ref.dtype)

    return pl.pallas_call(
        body, name=name, grid=(C // tc, T // tl),
        in_specs=[pl.BlockSpec((tl, tc), lambda j, i: (i, j)),
                  pl.BlockSpec((HALO, tc), lambda j, i: (jnp.minimum((i + 1) * (tl // HALO), nblk - 1), j)),
                  pl.BlockSpec((tl, tc), lambda j, i: (i, c0 + j)),
                  pl.BlockSpec((K, tc), lambda j, i: (0, j))],
        out_specs=[pl.BlockSpec((tl, tc), lambda j, i: (i, j)), pl.BlockSpec((K, tc), lambda j, i: (0, j)),
                   pl.BlockSpec((1, tc), lambda j, i: (0, j))],
        out_shape=[jax.ShapeDtypeStruct((T, C), out_dtype), jax.ShapeDtypeStruct((K, C), F32),
                   jax.ShapeDtypeStruct((1, C), F32)],
        scratch_shapes=[pltpu.VMEM((tl + HALO, tc), F32)],
        compiler_params=_cparams("parallel", "arbitrary"),
    )(du, du, x, w)


SB_QUERIES = 1024


def _attn_tiles(T, keys=256, queries=1024):
    return min(T, queries), min(T, keys)


def _after_diag(keys, queries, strict):
    d = lax.broadcasted_iota(jnp.int32, (keys, queries), 1) - lax.broadcasted_iota(jnp.int32, (keys, queries), 0)
    return d > 0 if strict else d >= 0


def _log_gates(z):
    l1p = jnp.log(1.0 + jnp.exp(-jnp.abs(z)))
    a = jnp.minimum(z, 0.0) - l1p
    return a, a - z


def _causal_sweep(i, tq, tk, block, descending, keep_going=None, first_block=None):
    nb = tq // tk
    n_full = i * nb

    def band():
        order = reversed(range(nb)) if descending else range(nb)
        for bb in order:
            block(pl.multiple_of(i * tq + bb * tk, tk), bb * tk, True)

    def full():
        if descending and keep_going is not None:
            def step(j):
                block(pl.multiple_of((n_full - 1 - j) * tk, tk), 0, False)
                return j + 1
            done = lax.while_loop(lambda j: jnp.logical_and(j < n_full, keep_going()), step, jnp.int32(0))
            return n_full - done

        def step(j, c):
            kb = (n_full - 1 - j) if descending else j
            block(pl.multiple_of(kb * tk, tk), 0, False)
            return c
        lax.fori_loop(0 if first_block is None else first_block, n_full, step, 0)
        return None

    if descending:
        band()
        return full()
    full()
    band()
    return None


def sb_fwd(q, k, v, *, name):
    H, dh, T = q.shape
    tq, tk = _attn_tiles(T, queries=SB_QUERIES)

    def body(q_ref, k_ref, v_ref, y_ref, bt_ref, first_ref, acc, run):
        acc[...] = jnp.zeros_like(acc)
        run[...] = jnp.zeros_like(run)
        u_after = _tri(tk, lambda r, c: r < c)

        def block(k0, r0, masked):
            kb = k_ref[pl.ds(k0, tk), :]
            vb = v_ref[pl.ds(k0, tk), :]
            z = jnp.dot(kb, q_ref[:, r0:], preferred_element_type=F32)
            a, b = _log_gates(z)
            if masked:
                valid = _after_diag(tk, tq - r0, True)
                b = jnp.where(valid, b, 0.0)
            w = jnp.exp(a + _tri_dot(u_after, b, 2) + run[:, r0:])
            if masked:
                w = jnp.where(valid, w, 0.0)
            acc[:, r0:] += lax.dot_general(vb, w.astype(BF16), TN, preferred_element_type=F32)
            run[:, r0:] += jnp.sum(b, axis=0, keepdims=True)

        first = _causal_sweep(pl.program_id(1), tq, tk, block, descending=True,
                              keep_going=lambda: jnp.max(run[...]) >= SB_ZERO_BELOW)
        y_ref[...] = acc[...]
        bt_ref[...] = run[...]
        first_ref[...] = jnp.zeros(first_ref.shape, F32) + first.astype(F32)

    return pl.pallas_call(
        body, name=name, grid=(H, T // tq),
        in_specs=[pl.BlockSpec((None, dh, tq), lambda h, i: (h, 0, i)),
                  pl.BlockSpec((None, T, dh), lambda h, i: (h, 0, 0)),
                  pl.BlockSpec((None, T, dh), lambda h, i: (h, 0, 0))],
        out_specs=[pl.BlockSpec((None, dh, tq), lambda h, i: (h, 0, i)),
                   pl.BlockSpec((None, 1, tq), lambda h, i: (h, 0, i)),
                   pl.BlockSpec((None, None, HALO, LANES), lambda h, i: (h, i, 0, 0))],
        out_shape=[jax.ShapeDtypeStruct((H, dh, T), F32), jax.ShapeDtypeStruct((H, 1, T), F32),
                   jax.ShapeDtypeStruct((H, T // tq, HALO, LANES), F32)],
        scratch_shapes=[pltpu.VMEM((dh, tq), F32), pltpu.VMEM((1, tq), F32)],
        compiler_params=_cparams("parallel", "parallel"),
    )(q, k, v)


def sb_bwd(q, k, v, dy, btot, first, *, name, q_scale):
    H, dh, T = q.shape
    tq, tk = _attn_tiles(T, queries=SB_QUERIES)

    def body(q_ref, k_ref, v_ref, dy_ref, bt_ref, first_ref, dq_ref, dk_ref, dv_ref, dq, pb, pg, dyb):
        @pl.when(pl.program_id(1) == 0)
        def _():
            dk_ref[...] = jnp.zeros_like(dk_ref)
            dv_ref[...] = jnp.zeros_like(dv_ref)

        dq[...] = jnp.zeros_like(dq)
        pb[...] = jnp.zeros_like(pb)
        pg[...] = jnp.zeros_like(pg)
        dyb[...] = dy_ref[...].astype(BF16)
        u_upto = _tri(tk, lambda r, c: r >= c)
        u_before = _tri(tk, lambda r, c: r > c)

        def block(k0, r0, masked):
            kb = k_ref[pl.ds(k0, tk), :]
            vb = v_ref[pl.ds(k0, tk), :]
            qv = q_ref[:, r0:]
            dyv = dyb[:, r0:]
            z = jnp.dot(kb, qv, preferred_element_type=F32)
            a, b = _log_gates(z)
            if masked:
                valid = _after_diag(tk, tq - r0, True)
                b = jnp.where(valid, b, 0.0)
            w = jnp.exp(a + (bt_ref[:, r0:] - pb[:, r0:] - _tri_dot(u_upto, b, 2)))
            if masked:
                w = jnp.where(valid, w, 0.0)
            g = w * jnp.dot(vb, dyv, preferred_element_type=F32)
            dz = g - jnp.exp(a) * (g + pg[:, r0:] + _tri_dot(u_before, g, 2))
            if masked:
                dz = jnp.where(valid, dz, 0.0)
            dz = dz.astype(BF16)
            dq[:, r0:] += lax.dot_general(kb, dz, TN, preferred_element_type=F32)
            dk_ref[pl.ds(k0, tk), :] += lax.dot_general(dz, qv, NT, preferred_element_type=F32)
            dv_ref[pl.ds(k0, tk), :] += lax.dot_general(w.astype(BF16), dyv, NT, preferred_element_type=F32)
            pb[:, r0:] += jnp.sum(b, axis=0, keepdims=True)
            pg[:, r0:] += jnp.sum(g, axis=0, keepdims=True)

        i = pl.program_id(1)
        first = jnp.clip(jnp.max(first_ref[...]).astype(jnp.int32), 0, i * (tq // tk))
        _causal_sweep(i, tq, tk, block, descending=False, first_block=first)
        dq_ref[...] = dq[...] * q_scale

    return pl.pallas_call(
        body, name=name, grid=(H, T // tq),
        in_specs=[pl.BlockSpec((None, dh, tq), lambda h, i: (h, 0, i)),
                  pl.BlockSpec((None, T, dh), lambda h, i: (h, 0, 0)),
                  pl.BlockSpec((None, T, dh), lambda h, i: (h, 0, 0)),
                  pl.BlockSpec((None, dh, tq), lambda h, i: (h, 0, i)),
                  pl.BlockSpec((None, 1, tq), lambda h, i: (h, 0, i)),
                  pl.BlockSpec((None, None, HALO, LANES), lambda h, i: (h, i, 0, 0))],
        out_specs=[pl.BlockSpec((None, dh, tq), lambda h, i: (h, 0, i)),
                   pl.BlockSpec((None, T, dh), lambda h, i: (h, 0, 0)),
                   pl.BlockSpec((None, T, dh), lambda h, i: (h, 0, 0))],
        out_shape=[jax.ShapeDtypeStruct((H, dh, T), F32), jax.ShapeDtypeStruct((H, T, dh), F32),
                   jax.ShapeDtypeStruct((H, T, dh), F32)],
        scratch_shapes=[pltpu.VMEM((dh, tq), F32), pltpu.VMEM((1, tq), F32), pltpu.VMEM((1, tq), F32),
                        pltpu.VMEM((dh, tq), BF16)],
        compiler_params=_cparams("parallel", "arbitrary"),
    )(q, k, v, dy, btot, first)


NEG = -1e30
SB_ZERO_BELOW = -105.0
MLA_KEYS = 512


def _call_with_exchange(body, exchange, *, name, grid, in_specs, out_specs, out_shape, scratch_shapes, args):
    if exchange is None:
        return pl.pallas_call(body, name=name, grid=grid, in_specs=in_specs, out_specs=out_specs, out_shape=out_shape,
                              scratch_shapes=scratch_shapes, compiler_params=_cparams("parallel", "arbitrary"))(*args)
    kind, src = exchange
    n_in, n_out, n_scr = len(in_specs), len(out_specs), len(scratch_shapes)
    R = src.shape[-2]

    def wrapped(*refs):
        ins, src_ref = refs[:n_in], refs[n_in]
        outs, xout = refs[n_in + 1:n_in + 1 + n_out], refs[n_in + 1 + n_out]
        scr = refs[n_in + 2 + n_out:n_in + 2 + n_out + n_scr]
        start, finish = _direct_exchange(kind, src_ref, xout, *refs[-3:])
        step = pl.program_id(0) * pl.num_programs(1) + pl.program_id(1)
        pl.when(step == 0)(start)
        body(*ins, *outs, *scr)
        pl.when(step == pl.num_programs(0) * pl.num_programs(1) - 1)(finish)

    return pl.pallas_call(
        wrapped, name=name, grid=grid, in_specs=list(in_specs) + [HBM_SPEC], out_specs=list(out_specs) + [HBM_SPEC],
        out_shape=list(out_shape) + [jax.ShapeDtypeStruct((8, R, LANES), src.dtype)],
        scratch_shapes=list(scratch_shapes) + [pltpu.SemaphoreType.DMA((7,)), pltpu.SemaphoreType.DMA((7,)),
                                               pltpu.SemaphoreType.DMA],
        compiler_params=_cparams("arbitrary", "arbitrary"))(*args, src)


def mla_fwd(q, k, v, *, name, exchange=None):
    H, dk, T = q.shape
    dv = v.shape[1] // H
    tq, tk = _attn_tiles(T, MLA_KEYS)

    def body(q_ref, k_ref, v_ref, o_ref, l_ref, acc, m_s, l_s):
        acc[...] = jnp.zeros_like(acc)
        m_s[...] = jnp.full_like(m_s, NEG)
        l_s[...] = jnp.zeros_like(l_s)

        def block(k0, r0, masked):
            kb = k_ref[pl.ds(k0, tk), :]
            vb = v_ref[pl.ds(k0, tk), :]
            s = jnp.dot(kb, q_ref[:, r0:], preferred_element_type=F32)
            if masked:
                s = jnp.where(_after_diag(tk, tq - r0, False), s, NEG)
            m = m_s[:, r0:]
            m_new = jnp.maximum(m, jnp.max(s, axis=0, keepdims=True))
            p = jnp.exp(s - m_new)
            alpha = jnp.exp(m - m_new)
            l_s[:, r0:] = alpha * l_s[:, r0:] + jnp.sum(p, axis=0, keepdims=True)
            acc[:, r0:] = alpha * acc[:, r0:] + lax.dot_general(vb, p.astype(BF16), TN, preferred_element_type=F32)
            m_s[:, r0:] = m_new

        _causal_sweep(pl.program_id(1), tq, tk, block, descending=False)
        o_ref[...] = acc[...] / l_s[...]
        l_ref[...] = m_s[...] + jnp.log(l_s[...])

    return _call_with_exchange(
        body, exchange, name=name, grid=(H, T // tq),
        in_specs=[pl.BlockSpec((None, dk, tq), lambda h, i: (h, 0, i)),
                  pl.BlockSpec((T, dk), lambda h, i: (0, h)),
                  pl.BlockSpec((T, dv), lambda h, i: (0, h))],
        out_specs=[pl.BlockSpec((None, dv, tq), lambda h, i: (h, 0, i)),
                   pl.BlockSpec((None, 1, tq), lambda h, i: (h, 0, i))],
        out_shape=[jax.ShapeDtypeStruct((H, dv, T), F32), jax.ShapeDtypeStruct((H, 1, T), F32)],
        scratch_shapes=[pltpu.VMEM((dv, tq), F32), pltpu.VMEM((1, tq), F32), pltpu.VMEM((1, tq), F32)],
        args=(q, k, v))


def mla_bwd(q, k, v, do, o, lse, *, name, exchange=None):
    H, dk, T = q.shape
    dv = v.shape[1] // H
    tq, tk = _attn_tiles(T, MLA_KEYS)

    def body(q_ref, k_ref, v_ref, do_ref, o_ref, l_ref, dq_ref, dk_ref, dv_ref, dq, delta, dob):
        @pl.when(pl.program_id(1) == 0)
        def _():
            dk_ref[...] = jnp.zeros_like(dk_ref)
            dv_ref[...] = jnp.zeros_like(dv_ref)

        dq[...] = jnp.zeros_like(dq)
        dov = do_ref[...].astype(F32)
        dob[...] = dov.astype(BF16)
        delta[...] = jnp.sum(dov * o_ref[...], axis=0, keepdims=True)

        def block(k0, r0, masked):
            kb = k_ref[pl.ds(k0, tk), :]
            vb = v_ref[pl.ds(k0, tk), :]
            qv = q_ref[:, r0:]
            dov_b = dob[:, r0:]
            s = jnp.dot(kb, qv, preferred_element_type=F32)
            p = jnp.exp(s - l_ref[:, r0:])
            if masked:
                p = jnp.where(_after_diag(tk, tq - r0, False), p, 0.0)
            dp = jnp.dot(vb, dov_b, preferred_element_type=F32)
            ds = (p * (dp - delta[:, r0:])).astype(BF16)
            dq[:, r0:] += lax.dot_general(kb, ds, TN, preferred_element_type=F32)
            dk_ref[pl.ds(k0, tk), :] += lax.dot_general(ds, qv, NT, preferred_element_type=F32)
            dv_ref[pl.ds(k0, tk), :] += lax.dot_general(p.astype(BF16), dov_b, NT, preferred_element_type=F32)

        _causal_sweep(pl.program_id(1), tq, tk, block, descending=False)
        dq_ref[...] = dq[...]

    return _call_with_exchange(
        body, exchange, name=name, grid=(H, T // tq),
        in_specs=[pl.BlockSpec((None, dk, tq), lambda h, i: (h, 0, i)),
                  pl.BlockSpec((T, dk), lambda h, i: (0, h)),
                  pl.BlockSpec((T, dv), lambda h, i: (0, h)),
                  pl.BlockSpec((None, dv, tq), lambda h, i: (h, 0, i)),
                  pl.BlockSpec((None, dv, tq), lambda h, i: (h, 0, i)),
                  pl.BlockSpec((None, 1, tq), lambda h, i: (h, 0, i))],
        out_specs=[pl.BlockSpec((None, dk, tq), lambda h, i: (h, 0, i)),
                   pl.BlockSpec((T, dk), lambda h, i: (0, h)),
                   pl.BlockSpec((T, dv), lambda h, i: (0, h))],
        out_shape=[jax.ShapeDtypeStruct((H, dk, T), F32), jax.ShapeDtypeStruct((T, H * dk), F32),
                   jax.ShapeDtypeStruct((T, H * dv), F32)],
        scratch_shapes=[pltpu.VMEM((dk, tq), F32), pltpu.VMEM((1, tq), F32), pltpu.VMEM((dv, tq), BF16)],
        args=(q, k, v, do, o, lse))


def _lane_pick(x, h):
    lane = lax.broadcasted_iota(jnp.int32, (1, x.shape[1]), 1)
    return jnp.sum(jnp.where(lane == h, x, 0.0), axis=1, keepdims=True)


def _row_pick(x, h):
    sub = lax.broadcasted_iota(jnp.int32, (x.shape[0], 1), 0)
    return jnp.sum(jnp.where(sub == h, x, 0.0), axis=0, keepdims=True)


def ssd_chunk_fn(*args):
    nh, ng = SSM_HEADS, SSM_GROUPS
    xs = args[:nh]
    bs = args[nh:nh + ng]
    cs = args[nh + ng:nh + 2 * ng]
    dt_raw = args[nh + 2 * ng]
    st = args[nh + 2 * ng + 1:nh + 2 * ng + 1 + nh]
    dt_bias, a_log, d_skip = args[nh + 2 * ng + 1 + nh:]
    L = dt_raw.shape[0]
    dt = _softplus(dt_raw + dt_bias)
    da = dt * (-jnp.exp(a_log))
    dcs = csum_rows(da)
    dcs_t = dcs.T
    total = jnp.sum(da, axis=0, keepdims=True)
    causal = lax.broadcasted_iota(jnp.int32, (L, L), 0) >= lax.broadcasted_iota(jnp.int32, (L, L), 1)
    cb = [mm_nt(cs[g], bs[g]) for g in range(ng)]
    ys, new_st = [], []
    for h in range(nh):
        g = h // (nh // ng)
        dcs_h = _lane_pick(dcs, h)
        dt_h = _lane_pick(dt, h)
        tot_h = _lane_pick(total, h)
        dsk_h = _lane_pick(d_skip, h)
        decay = jnp.exp(jnp.where(causal, dcs_h - _row_pick(dcs_t, h), NEG))
        xdt = xs[h] * dt_h
        y = mm_nn(cb[g] * decay, xdt)
        y = y + mm_nn(cs[g] * jnp.exp(dcs_h), st[h])
        ys.append(y + xs[h] * dsk_h)
        new_st.append(st[h] * jnp.exp(tot_h) + mm_tn(bs[g] * jnp.exp(tot_h - dcs_h), xdt))
    return tuple(ys) + tuple(new_st)


def ssd_fwd(x_hm, b_hm, c_hm, proj, dt_bias, a_log, d_skip, *, name):
    nh, T, P = x_hm.shape
    ng, N = b_hm.shape[0], b_hm.shape[2]
    L = SSM_CHUNK
    nc = T // L
    dtb = OFF_DT // LANES

    def body(x_ref, b_ref, c_ref, dt_ref, db_ref, al_ref, ds_ref, y_ref, s_ref, state):
        @pl.when(pl.program_id(0) == 0)
        def _():
            state[...] = jnp.zeros_like(state)

        s_ref[...] = state[...]
        args = ([x_ref[h] for h in range(nh)] + [b_ref[g] for g in range(ng)] + [c_ref[g] for g in range(ng)]
                + [dt_ref[...]] + [state[h] for h in range(nh)] + [db_ref[...], al_ref[...], ds_ref[...]])
        res = ssd_chunk_fn(*args)
        for h in range(nh):
            y_ref[:, h * P:(h + 1) * P] = res[h]
            state[h] = res[nh + h]

    par = pl.BlockSpec((1, LANES), lambda i: (0, 0))
    return pl.pallas_call(
        body, name=name, grid=(nc,),
        in_specs=[pl.BlockSpec((nh, L, P), lambda i: (0, i, 0)), pl.BlockSpec((ng, L, N), lambda i: (0, i, 0)),
                  pl.BlockSpec((ng, L, N), lambda i: (0, i, 0)), pl.BlockSpec((L, LANES), lambda i: (i, dtb)),
                  par, par, par],
        out_specs=[pl.BlockSpec((L, nh * P), lambda i: (i, 0)),
                   pl.BlockSpec((None, nh, N, P), lambda i: (i, 0, 0, 0))],
        out_shape=[jax.ShapeDtypeStruct((T, nh * P), F32), jax.ShapeDtypeStruct((nc, nh, N, P), F32)],
        scratch_shapes=[pltpu.VMEM((nh, N, P), F32)],
        compiler_params=_cparams("arbitrary"),
    )(x_hm, b_hm, c_hm, proj, dt_bias, a_log, d_skip)


def ssd_bwd(x_hm, b_hm, c_hm, proj, states, dt_bias, a_log, d_skip, dy, *, name):
    nh, T, P = x_hm.shape
    ng, N = b_hm.shape[0], b_hm.shape[2]
    L = SSM_CHUNK
    nc = T // L
    dtb = OFF_DT // LANES

    def body(x_ref, b_ref, c_ref, dt_ref, s_ref, db_ref, al_ref, ds_ref, dy_ref,
             dxbc_ref, ddt_ref, gdb_ref, gal_ref, gds_ref, dstate):
        @pl.when(pl.program_id(0) == 0)
        def _():
            dstate[...] = jnp.zeros_like(dstate)
            gdb_ref[...] = jnp.zeros_like(gdb_ref)
            gal_ref[...] = jnp.zeros_like(gal_ref)
            gds_ref[...] = jnp.zeros_like(gds_ref)

        args = ([x_ref[h] for h in range(nh)] + [b_ref[g] for g in range(ng)] + [c_ref[g] for g in range(ng)]
                + [dt_ref[...]] + [s_ref[h] for h in range(nh)] + [db_ref[...], al_ref[...], ds_ref[...]])
        _, vjp = jax.vjp(ssd_chunk_fn, *args)
        g = vjp(tuple([dy_ref[h] for h in range(nh)] + [dstate[h] for h in range(nh)]))
        for j in range(nh + 2 * ng):
            dxbc_ref[:, j * P:(j + 1) * P] = g[j]
        ddt_ref[...] = g[nh + 2 * ng]
        for h in range(nh):
            dstate[h] = g[nh + 2 * ng + 1 + h]
        gdb_ref[...] += g[-3]
        gal_ref[...] += g[-2]
        gds_ref[...] += g[-1]

    rev = lambda i: nc - 1 - i
    par = pl.BlockSpec((1, LANES), lambda i: (0, 0))
    return pl.pallas_call(
        body, name=name, grid=(nc,),
        in_specs=[pl.BlockSpec((nh, L, P), lambda i: (0, rev(i), 0)), pl.BlockSpec((ng, L, N), lambda i: (0, rev(i), 0)),
                  pl.BlockSpec((ng, L, N), lambda i: (0, rev(i), 0)), pl.BlockSpec((L, LANES), lambda i: (rev(i), dtb)),
                  pl.BlockSpec((None, nh, N, P), lambda i: (rev(i), 0, 0, 0)), par, par, par,
                  pl.BlockSpec((nh, L, P), lambda i: (0, rev(i), 0))],
        out_specs=[pl.BlockSpec((L, (nh + 2 * ng) * P), lambda i: (rev(i), 0)),
                   pl.BlockSpec((L, LANES), lambda i: (rev(i), 0)), par, par, par],
        out_shape=[jax.ShapeDtypeStruct((T, (nh + 2 * ng) * P), F32), jax.ShapeDtypeStruct((T, LANES), F32),
                   jax.ShapeDtypeStruct((1, LANES), F32), jax.ShapeDtypeStruct((1, LANES), F32),
                   jax.ShapeDtypeStruct((1, LANES), F32)],
        scratch_shapes=[pltpu.VMEM((nh, N, P), F32)],
        compiler_params=_cparams("arbitrary"),
    )(x_hm, b_hm, c_hm, proj, states, dt_bias, a_log, d_skip, dy)


def loss_head(h, target, g, *, name, tile=512):
    T, C = h.shape
    tl = min(T, tile)

    def body(h_ref, t_ref, g_ref, dh_ref, dg_ref, ls_ref):
        @pl.when(pl.program_id(0) == 0)
        def _():
            dg_ref[...] = jnp.zeros_like(dg_ref)
            ls_ref[...] = jnp.zeros_like(ls_ref)

        (y,), vjp = jax.vjp(rms_fn, h_ref[...], g_ref[...])
        err = y - t_ref[...]
        ls_ref[...] += jnp.sum(err * err, axis=0, keepdims=True) * (0.5 / C)
        dh, dg = vjp((err * (1.0 / C),))
        dh_ref[...] = dh
        dg_ref[...] += dg

    row = pl.BlockSpec((tl, C), lambda i: (i, 0))
    par = pl.BlockSpec((1, C), lambda i: (0, 0))
    return pl.pallas_call(
        body, name=name, grid=(T // tl,), in_specs=[row, row, par], out_specs=[row, par, par],
        out_shape=[jax.ShapeDtypeStruct((T, C), F32), jax.ShapeDtypeStruct((1, C), F32),
                   jax.ShapeDtypeStruct((1, C), F32)],
        compiler_params=_cparams("arbitrary"),
    )(h, target, g)


def adamw(w, g, m, v, *, name):
    R, C = w.shape
    tr = R
    for d in range(8, min(R, 512) + 1, 8):
        if R % d == 0:
            tr = d
    c1 = 1.0 - ADAM_B1 ** ADAM_STEP
    c2 = 1.0 - ADAM_B2 ** ADAM_STEP

    def body(w_ref, g_ref, m_ref, v_ref, d_ref, nm_ref, nv_ref):
        gv = g_ref[...]
        nm = ADAM_B1 * m_ref[...] + (1.0 - ADAM_B1) * gv
        nv = ADAM_B2 * v_ref[...] + (1.0 - ADAM_B2) * (gv * gv)
        d_ref[...] = -ADAM_LR * ((nm / c1) / (jnp.sqrt(nv / c2) + ADAM_EPS) + ADAM_WD * w_ref[...])
        nm_ref[...] = nm
        nv_ref[...] = nv

    spec = pl.BlockSpec((tr, C), lambda i: (i, 0))
    return pl.pallas_call(
        body, name=name, grid=(R // tr,), in_specs=[spec] * 4, out_specs=[spec] * 3,
        out_shape=[jax.ShapeDtypeStruct((R, C), F32)] * 3,
        compiler_params=_cparams("parallel"),
    )(w, g, m, v)


MESH = pl.DeviceIdType.MESH
HBM_SPEC = pl.BlockSpec(memory_space=pltpu.HBM)


def _place():
    return lax.axis_index("x"), lax.axis_index("y"), lax.axis_index("c")


def _direct_exchange(kind, src_ref, out_ref, send_sems, recv_sems, local_sem):
    x, y, c = _place()
    me = 4 * x + 2 * y + c

    def block_for(dest):
        return src_ref if kind == "gather" else src_ref.at[dest]

    own = pltpu.make_async_copy(block_for(me), out_ref.at[me], local_sem)
    sends, arrivals = [], []
    for f in range(1, 8):
        px = jnp.where((f >> 2) & 1, 1 - x, x)
        py = jnp.where((f >> 1) & 1, 1 - y, y)
        pc = jnp.where(f & 1, 1 - c, c)
        peer = 4 * px + 2 * py + pc
        for dst, group in ((out_ref.at[me], sends), (out_ref.at[peer], arrivals)):
            group.append(pltpu.make_async_remote_copy(
                src_ref=block_for(peer), dst_ref=dst, send_sem=send_sems.at[f - 1], recv_sem=recv_sems.at[f - 1],
                device_id=(px, py, pc), device_id_type=MESH))

    def start():
        own.start()
        for cp in sends:
            cp.start()

    def finish():
        for cp in arrivals:
            cp.wait_recv()
        for cp in sends:
            cp.wait_send()
        own.wait()

    return start, finish


def allgather_blocks(mine, *, name):
    R = mine.shape[0]

    def body(x_ref, out_ref, send_sems, recv_sems, local_sem):
        x, y, c = _place()
        me, sibling = (x, y, c), (x, y, 1 - c)
        chips = [(1 - x, y), (x, 1 - y), (1 - x, 1 - y)]

        def slot(px, py, pc):
            return out_ref.at[4 * px + 2 * py + pc]

        def copy(k, block, to, src=None):
            return pltpu.make_async_remote_copy(
                src_ref=slot(*block) if src is None else src, dst_ref=slot(*block),
                send_sem=send_sems.at[k], recv_sem=recv_sems.at[k], device_id=to, device_id_type=MESH)

        own = pltpu.make_async_copy(x_ref, slot(*me), local_sem)
        own.start()
        first = [copy(0, me, sibling, src=x_ref)]
        first += [copy(1 + j, me, (*chip, c), src=x_ref) for j, chip in enumerate(chips)]
        for cp in first:
            cp.start()
        passed = [copy(4 + j, (*chip, c), sibling) for j, chip in enumerate(chips)]
        for j, chip in enumerate(chips):
            copy(1 + j, (*chip, c), me).wait_recv()
            passed[j].start()
        copy(0, sibling, me).wait_recv()
        for j, chip in enumerate(chips):
            copy(4 + j, (*chip, 1 - c), me).wait_recv()
        for cp in first + passed:
            cp.wait_send()
        own.wait()

    return pl.pallas_call(
        body, name=name, out_shape=jax.ShapeDtypeStruct((8, R, LANES), mine.dtype),
        in_specs=[HBM_SPEC], out_specs=HBM_SPEC,
        scratch_shapes=[pltpu.SemaphoreType.DMA((7,)), pltpu.SemaphoreType.DMA((7,)), pltpu.SemaphoreType.DMA],
    )(mine)


def allgather_direct(mine, *, name):
    R = mine.shape[0]

    def body(x_ref, out_ref, send_sems, recv_sems, local_sem):
        start, finish = _direct_exchange("gather", x_ref, out_ref, send_sems, recv_sems, local_sem)
        start()
        finish()

    return pl.pallas_call(
        body, name=name, out_shape=jax.ShapeDtypeStruct((8, R, LANES), mine.dtype),
        in_specs=[HBM_SPEC], out_specs=HBM_SPEC,
        scratch_shapes=[pltpu.SemaphoreType.DMA((7,)), pltpu.SemaphoreType.DMA((7,)), pltpu.SemaphoreType.DMA],
    )(mine)


def send_to_sibling(v, *, name):
    def body(v_ref, out_ref, send_sem, recv_sem):
        x, y, c = _place()
        cp = pltpu.make_async_remote_copy(src_ref=v_ref, dst_ref=out_ref, send_sem=send_sem, recv_sem=recv_sem,
                                          device_id=(x, y, 1 - c), device_id_type=MESH)
        cp.start()
        cp.wait()

    return pl.pallas_call(
        body, name=name, out_shape=jax.ShapeDtypeStruct(v.shape, v.dtype), in_specs=[HBM_SPEC], out_specs=HBM_SPEC,
        scratch_shapes=[pltpu.SemaphoreType.DMA, pltpu.SemaphoreType.DMA],
    )(v)


def chip_exchange(p, *, name):
    R = p.shape[1]

    def body(p_ref, out_ref, send_sems, recv_sems):
        x, y, c = _place()
        chips = [(1 - x, y), (x, 1 - y), (1 - x, 1 - y)]
        sends = [pltpu.make_async_remote_copy(
            src_ref=p_ref.at[2 * px + py], dst_ref=out_ref.at[j], send_sem=send_sems.at[j], recv_sem=recv_sems.at[j],
            device_id=(px, py, c), device_id_type=MESH) for j, (px, py) in enumerate(chips)]
        for cp in sends:
            cp.start()
        for cp in sends:
            cp.wait()

    return pl.pallas_call(
        body, name=name, out_shape=jax.ShapeDtypeStruct((3, R, LANES), p.dtype), in_specs=[HBM_SPEC],
        out_specs=HBM_SPEC,
        scratch_shapes=[pltpu.SemaphoreType.DMA((3,)), pltpu.SemaphoreType.DMA((3,))],
    )(p)


def add_blocks(terms, out_dtype, *, name, tile=1024):
    terms = [t if isinstance(t, tuple) else (t, None) for t in terms]
    R = terms[0][0].shape[-2]
    tr = R
    for d in range(16, min(R, tile) + 1, 16):
        if R % d == 0:
            tr = d

    def body(*refs):
        acc = refs[0][...].astype(F32)
        for ref in refs[1:-1]:
            acc = acc + ref[...].astype(F32)
        refs[-1][...] = acc.astype(out_dtype)

    spec = pl.BlockSpec((tr, LANES), lambda i: (i, 0))
    in_specs = [spec if slot is None else pl.BlockSpec((None, tr, LANES), lambda i, slot=slot: (slot, i, 0))
                for _, slot in terms]
    return pl.pallas_call(
        body, name=name, grid=(R // tr,), in_specs=in_specs, out_specs=spec,
        out_shape=jax.ShapeDtypeStruct((R, LANES), out_dtype), compiler_params=_cparams("parallel"),
    )(*[a for a, _ in terms])


FLAT_ROW_STEP = 640


def _half_rows(arr, cc):
    hr = arr.shape[0] // 2
    return lax.dynamic_slice_in_dim(arr, cc * hr, hr, axis=0).reshape(-1)


def _flat_half(shards, cc, dtype):
    flat = jnp.concatenate([_half_rows(shards[n], cc).astype(dtype) for n in BIG])
    rows = -(-flat.shape[0] // (FLAT_ROW_STEP * LANES)) * FLAT_ROW_STEP
    return jnp.pad(flat, (0, rows * LANES - flat.shape[0])).reshape(rows, LANES)


def _flat_rows(shapes):
    n = sum((R // 2) * C for R, C in shapes.values()) // LANES
    return -(-n // FLAT_ROW_STEP) * FLAT_ROW_STEP


def _to_blocks(full, shapes, dtype):
    pieces = []
    for n in BIG:
        R, C = shapes[n]
        a = full[n].astype(dtype)
        if BIG_AXIS[n] == 2:
            a = a.reshape(2, R // 2, 4, C).transpose(2, 0, 1, 3)
        pieces.append(a.reshape(8, (R // 2) * C // LANES, LANES))
    flat = jnp.concatenate(pieces, axis=1)
    return jnp.pad(flat, ((0, 0), (0, _flat_rows(shapes) - flat.shape[1]), (0, 0)))


def _from_blocks(g8, shapes):
    out, off = {}, 0
    for n in BIG:
        R, C = shapes[n]
        rows = (R // 2) * C // LANES
        a = g8[:, off:off + rows, :].reshape(4, 2, R // 2, C)
        out[n] = a.transpose(1, 2, 0, 3).reshape(R, 4 * C) if BIG_AXIS[n] == 2 else a.reshape(4 * R, C)
        off += rows
    return out


def _unflat_halves(flat_by_c, shapes):
    out, off = {}, 0
    for n in BIG:
        R, C = shapes[n]
        sz = (R // 2) * C
        out[n] = jnp.concatenate([flat_by_c[c][off:off + sz].reshape(R // 2, C) for c in range(2)], axis=0)
        off += sz
    return out


def _to_heads(a, nh):
    T = a.shape[0]
    return a.reshape(T, nh, a.shape[1] // nh).transpose(1, 0, 2)


def _from_heads(a):
    nh, T, d = a.shape
    return a.transpose(1, 0, 2).reshape(T, nh * d)


def _to_heads_t(a, nh):
    T = a.shape[0]
    return a.reshape(T, nh, a.shape[1] // nh).transpose(1, 2, 0)


def _from_heads_t(a):
    nh, d, T = a.shape
    return a.transpose(2, 0, 1).reshape(T, nh * d)


def _pad_cols(a, n):
    return jnp.pad(a, ((0, 0), (0, n - a.shape[1])))


def _pack_w_in(w):
    offs = [sum(IN_SPLITS[:i]) for i in range(len(IN_SPLITS) + 1)]
    sb, z, xbc, dt, cq, ckv, kr = [w[:, offs[i]:offs[i + 1]] for i in range(len(IN_SPLITS))]
    zeros = lambda n: jnp.zeros((w.shape[0], n), w.dtype)
    h = MLA_ROPE // 2
    kra = jnp.concatenate([zeros(MLA_NOPE), kr, zeros(LANES - MLA_QK)], axis=1)
    krb = jnp.concatenate([zeros(MLA_NOPE), -kr[:, h:], kr[:, :h], zeros(LANES - MLA_QK)], axis=1)
    return sb, jnp.concatenate([z, xbc, cq, ckv, _pad_cols(dt, LANES), kra, krb], axis=1)


def _unpack_gw_in(g_sb, g):
    h = MLA_ROPE // 2
    ga, gb = g[:, OFF_KRA:OFF_KRA + LANES], g[:, OFF_KRB:OFF_KRB + LANES]
    gkr = ga[:, MLA_NOPE:MLA_QK] + jnp.concatenate([gb[:, MLA_NOPE + h:MLA_QK], -gb[:, MLA_NOPE:MLA_NOPE + h]], axis=1)
    return jnp.concatenate([g_sb, g[:, OFF_Z:OFF_Z + 512], g[:, OFF_XBC:OFF_XBC + 768],
                            g[:, OFF_DT:OFF_DT + 8], g[:, OFF_CQ:OFF_CQ + 256], g[:, OFF_CKV:OFF_CKV + 128], gkr], axis=1)


def _pack_w_uq(w):
    zeros = lambda n: jnp.zeros((w.shape[0], n), w.dtype)
    h = MLA_ROPE // 2
    pp, rr = [], []
    for i in range(MLA_HEADS):
        nope = w[:, MLA_QK * i:MLA_QK * i + MLA_NOPE]
        rope = w[:, MLA_QK * i + MLA_NOPE:MLA_QK * (i + 1)]
        pp += [nope, rope, zeros(LANES - MLA_QK)]
        rr += [zeros(MLA_NOPE), -rope[:, h:], rope[:, :h], zeros(LANES - MLA_QK)]
    return jnp.concatenate(pp, axis=1), jnp.concatenate(rr, axis=1)


def _unpack_gw_uq(gp, gr):
    h = MLA_ROPE // 2
    out = []
    for i in range(MLA_HEADS):
        b = LANES * i
        out.append(gp[:, b:b + MLA_NOPE])
        out.append(gp[:, b + MLA_NOPE:b + MLA_NOPE + h] + gr[:, b + MLA_NOPE + h:b + MLA_QK])
        out.append(gp[:, b + MLA_NOPE + h:b + MLA_QK] - gr[:, b + MLA_NOPE:b + MLA_NOPE + h])
    return jnp.concatenate(out, axis=1)


def _pack_w_ukv(w):
    zeros = lambda n: jnp.zeros((w.shape[0], n), w.dtype)
    kk, vv = [], []
    for i in range(MLA_HEADS):
        b = (MLA_NOPE + MLA_V) * i
        kk += [w[:, b:b + MLA_NOPE], zeros(LANES - MLA_NOPE)]
        vv += [w[:, b + MLA_NOPE:b + MLA_NOPE + MLA_V], zeros(LANES - MLA_V)]
    return jnp.concatenate(kk, axis=1), jnp.concatenate(vv, axis=1)


def _unpack_gw_ukv(gk, gv):
    out = []
    for i in range(MLA_HEADS):
        out += [gk[:, LANES * i:LANES * i + MLA_NOPE], gv[:, LANES * i:LANES * i + MLA_V]]
    return jnp.concatenate(out, axis=1)


def _rope_tables(positions):
    inv_freq = 1.0 / (ROPE_THETA ** (jnp.arange(0, MLA_ROPE, 2, dtype=F32) / MLA_ROPE))
    ang = positions.astype(F32)[:, None] * inv_freq
    cos, sin = jnp.cos(ang), jnp.sin(ang)
    T = positions.shape[0]
    one, zero = jnp.ones((T, MLA_NOPE), F32), jnp.zeros((T, MLA_NOPE), F32)
    pad1, pad0 = jnp.ones((T, LANES - MLA_QK), F32), jnp.zeros((T, LANES - MLA_QK), F32)
    return jnp.concatenate([one, cos, cos, pad1], axis=1), jnp.concatenate([zero, sin, sin, pad0], axis=1)


def _row(v):
    return v.reshape(1, -1)


def _pad_row(v):
    return _pad_cols(v.reshape(1, -1), LANES)


def _layer_weights(full, small, li):
    p = {}
    p["w_sb"], p["w_rest"] = _pack_w_in(full["w_in"])
    q_scale = jnp.concatenate([jnp.full((1, SB_HEADS * SB_DIM), SB_DIM ** -0.5, BF16),
                               jnp.ones((1, 2 * SB_HEADS * SB_DIM), BF16)], axis=1)
    p["w_sb_fwd"] = p["w_sb"] * q_scale
    p["wqp"], p["wqr"] = _pack_w_uq(full["mla_w_uq"])
    p["wkp"], p["wvp"] = _pack_w_ukv(full["mla_w_ukv"])
    p["w_out"] = full["w_out"]
    p["w_up"] = full["ffn_w_up"]
    p["w_down"] = full["ffn_w_down"]
    for n in ("mix_norm", "sb_out_norm", "ssm_conv_b", "ssm_out_norm", "mla_q_norm", "mla_kv_norm", "mla_out_norm",
              "ffn_norm", "ffn_conv_b"):
        p[n] = _row(small[n][li])
    for n in ("ssm_dt_bias", "ssm_a_log", "ssm_d"):
        p[n] = _pad_row(small[n][li])
    p["ssm_conv_w"] = small["ssm_conv_w"][li]
    p["ffn_conv_w"] = small["ffn_conv_w"][li]
    return p


def _layer_fwd(h, p, cos, sin, li, exchange=None):
    T = h.shape[0]
    nm = lambda s: "l%d_%s" % (li, s)
    s = {"h": h}
    (n1,) = rowwise(rms_fn, [h], [p["mix_norm"]], [(D_MODEL, BF16)], name=nm("mix_norm"))
    proj = matmul(n1, p["w_rest"], name=nm("in_proj"))
    qkv = matmul(n1, p["w_sb_fwd"], name=nm("in_proj_sb"), out_dtype=BF16)
    s["n1"], s["proj"] = n1, proj
    s["sb_q"] = _to_heads_t(qkv[:, 0:256], SB_HEADS)
    s["sb_k"] = _to_heads(qkv[:, 256:512], SB_HEADS)
    s["sb_v"] = _to_heads(qkv[:, 512:768], SB_HEADS)
    y_sb_hm, s["sb_bt"], s["sb_first"] = sb_fwd(s["sb_q"], s["sb_k"], s["sb_v"], name=nm("sb_fwd"))
    s["y_sb"] = _from_heads_t(y_sb_hm)
    s["x_hm"], s["b_hm"], s["c_hm"] = ssm_conv_act(proj, p["ssm_conv_w"], p["ssm_conv_b"], name=nm("ssm_conv"))
    s["y_ssm"], s["states"] = ssd_fwd(s["x_hm"], s["b_hm"], s["c_hm"], proj, p["ssm_dt_bias"], p["ssm_a_log"],
                                      p["ssm_d"], name=nm("ssd_fwd"))
    rows = [(proj, 256, OFF_CQ // 256), (proj, 128, OFF_CKV // 128), (proj, 128, OFF_KRA // 128),
            (proj, 128, OFF_KRB // 128), cos, sin]
    qp, kp, vv = rowwise(mla_prep_fn, rows, [p["mla_q_norm"], p["mla_kv_norm"], p["wqp"], p["wqr"], p["wkp"], p["wvp"]],
                         [(512, BF16), (512, BF16), (512, BF16)], name=nm("mla_prep"))
    s["mla_q"], s["mla_k"], s["mla_v"] = _to_heads_t(qp, MLA_HEADS), kp, vv
    s["mla_o"], s["mla_lse"], *rode = mla_fwd(s["mla_q"], kp, vv, name=nm("mla_fwd"), exchange=exchange)
    s["y_mla"] = _from_heads_t(s["mla_o"][:, :MLA_V, :])
    (cat,) = rowwise(merge_fn, [s["y_sb"], s["y_ssm"], (proj, 512, OFF_Z // 512), s["y_mla"]],
                     [p["sb_out_norm"], p["ssm_out_norm"], p["mla_out_norm"]], [(D_MODEL, BF16)], name=nm("merge"),
                     post=lambda a, b, c: (jnp.concatenate([a, b, c], axis=1),))
    s["cat"] = cat
    h1 = matmul(cat, p["w_out"], name=nm("out_proj"), residual=h)
    s["h1"] = h1
    (n2,) = rowwise(rms_fn, [h1], [p["ffn_norm"]], [(D_MODEL, BF16)], name=nm("ffn_norm"))
    up = matmul(n2, p["w_up"], name=nm("ffn_up"))
    act = ffn_act(up, p["ffn_conv_w"], p["ffn_conv_b"], name=nm("ffn_act"))
    s["n2"], s["up"], s["act"] = n2, up, act
    h2 = matmul(act, p["w_down"], name=nm("ffn_down"), residual=h1)
    return h2, s, (rode[0] if rode else None)


def _layer_bwd(dh2, s, p, cos, sin, li, exchange=None):
    nm = lambda t: "l%d_%s" % (li, t)
    g = {}
    proj = s["proj"]
    g["ffn_w_down"] = matmul(s["act"], dh2, name=nm("g_w_down"), ta=True)
    d_act = matmul(dh2, p["w_down"], name=nm("d_act"), out_dtype=BF16, tb=True)
    d_up_g, d_up_v, gwg, gwv, gbg, gbv = ffn_bwd_fused(s["up"], p["ffn_conv_w"], p["ffn_conv_b"], d_act,
                                                       name=nm("ffn_act_bwd"))
    g["ffn_conv_w"] = jnp.concatenate([gwg, gwv], axis=1)
    g["ffn_conv_b"] = jnp.concatenate([gbg[0], gbv[0]])
    g["ffn_w_up"] = jnp.concatenate([matmul(s["n2"], d_up_g, name=nm("g_w_up_gate"), ta=True),
                                     matmul(s["n2"], d_up_v, name=nm("g_w_up_val"), ta=True)], axis=1)
    d_n2 = matmul(d_up_g, p["w_up"], name=nm("d_n2_gate"), tb=True)
    d_n2 = matmul(d_up_v, p["w_up"], name=nm("d_n2_val"), tb=True, b_k0=D_FF, residual=d_n2)
    (dh1,), (gn,) = rowwise_bwd(rms_fn, [s["h1"]], [], [p["ffn_norm"]], [d_n2], [F32], name=nm("ffn_norm_bwd"),
                                add0=dh2)
    g["ffn_norm"] = gn[0]
    g["w_out"] = matmul(s["cat"], dh1, name=nm("g_w_out"), ta=True)
    d_cat = matmul(dh1, p["w_out"], name=nm("d_cat"), tb=True)
    (d_ysb, d_yssm, d_z, d_ymla), (g1, g2, g3) = rowwise_bwd(
        merge_fn, [s["y_sb"], s["y_ssm"], (proj, 512, OFF_Z // 512), s["y_mla"]], [],
        [p["sb_out_norm"], p["ssm_out_norm"], p["mla_out_norm"]], [d_cat], [F32, F32, BF16, F32], name=nm("merge_bwd"),
        pre_ct=lambda d: (d[:, 0:256], d[:, 256:768], d[:, 768:1024]))
    g["sb_out_norm"], g["ssm_out_norm"], g["mla_out_norm"] = g1[0], g2[0], g3[0]
    dq, dk, dv = sb_bwd(s["sb_q"], s["sb_k"], s["sb_v"], _to_heads_t(d_ysb, SB_HEADS), s["sb_bt"], s["sb_first"], name=nm("sb_bwd"),
                        q_scale=SB_DIM ** -0.5)
    d_sb = jnp.concatenate([_from_heads_t(dq), _from_heads(dk), _from_heads(dv)], axis=1).astype(BF16)
    do_t = jnp.pad(_to_heads_t(d_ymla, MLA_HEADS), ((0, 0), (0, LANES - MLA_V), (0, 0)))
    dqp, dkp, dvv, *rode = mla_bwd(s["mla_q"], s["mla_k"], s["mla_v"], do_t, s["mla_o"], s["mla_lse"],
                                   name=nm("mla_bwd"), exchange=exchange)
    rows = [(proj, 256, OFF_CQ // 256), (proj, 128, OFF_CKV // 128), (proj, 128, OFF_KRA // 128),
            (proj, 128, OFF_KRB // 128)]
    (d_cq, d_ckv, d_kra, d_krb), (gqn, gkvn, gwqp, gwqr, gwkp, gwvp) = rowwise_bwd(
        mla_prep_fn, rows, [cos, sin], [p["mla_q_norm"], p["mla_kv_norm"], p["wqp"], p["wqr"], p["wkp"], p["wvp"]],
        [_from_heads_t(dqp), dkp, dvv], [BF16] * 4, name=nm("mla_prep_bwd"), tile=256)
    g["mla_q_norm"], g["mla_kv_norm"] = gqn[0], gkvn[0]
    g["mla_w_uq"] = _unpack_gw_uq(gwqp, gwqr)
    g["mla_w_ukv"] = _unpack_gw_ukv(gwkp, gwvp)
    d_xbc_act, d_dt, gdb, gal, gds = ssd_bwd(
        s["x_hm"], s["b_hm"], s["c_hm"], proj, s["states"], p["ssm_dt_bias"], p["ssm_a_log"], p["ssm_d"],
        _to_heads(d_yssm, SSM_HEADS), name=nm("ssd_bwd"))
    g["ssm_dt_bias"], g["ssm_a_log"], g["ssm_d"] = gdb[0, :8], gal[0, :8], gds[0, :8]
    d_pre = ssm_conv_bwd_a(proj, p["ssm_conv_w"], p["ssm_conv_b"], d_xbc_act, name=nm("ssm_conv_bwd_a"))
    d_xbc, g["ssm_conv_w"], gscb = conv_bwd_b(d_pre, proj, OFF_XBC, p["ssm_conv_w"], name=nm("ssm_conv_bwd_b"),
                                              out_dtype=BF16, tc=256)
    g["ssm_conv_b"] = gscb[0]
    d_proj = jnp.concatenate([d_z, d_xbc, d_cq, d_ckv, d_dt.astype(BF16), d_kra, d_krb], axis=1)
    g["w_in"] = _unpack_gw_in(matmul(s["n1"], d_sb, name=nm("g_w_in_sb"), ta=True),
                              matmul(s["n1"], d_proj, name=nm("g_w_in"), ta=True))
    d_n1 = matmul(d_sb, p["w_sb"], name=nm("d_n1_sb"), tb=True)
    d_n1 = matmul(d_proj, p["w_rest"], name=nm("d_n1"), tb=True, residual=d_n1)
    (dh0,), (gm,) = rowwise_bwd(rms_fn, [s["h"]], [], [p["mix_norm"]], [d_n1], [F32], name=nm("mix_norm_bwd"),
                                add0=dh1)
    g["mix_norm"] = gm[0]
    return dh0, g, (rode[0] if rode else None)


def kernel(x, positions, mix_norm, w_in, sb_out_norm, ssm_conv_w, ssm_conv_b, ssm_dt_bias, ssm_a_log, ssm_d, ssm_out_norm, mla_q_norm, mla_w_uq, mla_kv_norm, mla_w_ukv, mla_out_norm, w_out, ffn_norm, ffn_w_up, ffn_conv_w, ffn_conv_b, ffn_w_down, final_norm, loss_target, m_mix_norm, m_w_in, m_sb_out_norm, m_ssm_conv_w, m_ssm_conv_b, m_ssm_dt_bias, m_ssm_a_log, m_ssm_d, m_ssm_out_norm, m_mla_q_norm, m_mla_w_uq, m_mla_kv_norm, m_mla_w_ukv, m_mla_out_norm, m_w_out, m_ffn_norm, m_ffn_w_up, m_ffn_conv_w, m_ffn_conv_b, m_ffn_w_down, m_final_norm, v_mix_norm, v_w_in, v_sb_out_norm, v_ssm_conv_w, v_ssm_conv_b, v_ssm_dt_bias, v_ssm_a_log, v_ssm_d, v_ssm_out_norm, v_mla_q_norm, v_mla_w_uq, v_mla_kv_norm, v_mla_w_ukv, v_mla_out_norm, v_w_out, v_ffn_norm, v_ffn_w_up, v_ffn_conv_w, v_ffn_conv_b, v_ffn_w_down, v_final_norm):
    W = dict(mix_norm=mix_norm, w_in=w_in, sb_out_norm=sb_out_norm, ssm_conv_w=ssm_conv_w, ssm_conv_b=ssm_conv_b,
             ssm_dt_bias=ssm_dt_bias, ssm_a_log=ssm_a_log, ssm_d=ssm_d, ssm_out_norm=ssm_out_norm,
             mla_q_norm=mla_q_norm, mla_w_uq=mla_w_uq, mla_kv_norm=mla_kv_norm, mla_w_ukv=mla_w_ukv,
             mla_out_norm=mla_out_norm, w_out=w_out, ffn_norm=ffn_norm, ffn_w_up=ffn_w_up, ffn_conv_w=ffn_conv_w,
             ffn_conv_b=ffn_conv_b, ffn_w_down=ffn_w_down, final_norm=final_norm)
    M = dict(mix_norm=m_mix_norm, w_in=m_w_in, sb_out_norm=m_sb_out_norm, ssm_conv_w=m_ssm_conv_w,
             ssm_conv_b=m_ssm_conv_b, ssm_dt_bias=m_ssm_dt_bias, ssm_a_log=m_ssm_a_log, ssm_d=m_ssm_d,
             ssm_out_norm=m_ssm_out_norm, mla_q_norm=m_mla_q_norm, mla_w_uq=m_mla_w_uq, mla_kv_norm=m_mla_kv_norm,
             mla_w_ukv=m_mla_w_ukv, mla_out_norm=m_mla_out_norm, w_out=m_w_out, ffn_norm=m_ffn_norm,
             ffn_w_up=m_ffn_w_up, ffn_conv_w=m_ffn_conv_w, ffn_conv_b=m_ffn_conv_b, ffn_w_down=m_ffn_w_down,
             final_norm=m_final_norm)
    V = dict(mix_norm=v_mix_norm, w_in=v_w_in, sb_out_norm=v_sb_out_norm, ssm_conv_w=v_ssm_conv_w,
             ssm_conv_b=v_ssm_conv_b, ssm_dt_bias=v_ssm_dt_bias, ssm_a_log=v_ssm_a_log, ssm_d=v_ssm_d,
             ssm_out_norm=v_ssm_out_norm, mla_q_norm=v_mla_q_norm, mla_w_uq=v_mla_w_uq, mla_kv_norm=v_mla_kv_norm,
             mla_w_ukv=v_mla_w_ukv, mla_out_norm=v_mla_out_norm, w_out=v_w_out, ffn_norm=v_ffn_norm,
             ffn_w_up=v_ffn_w_up, ffn_conv_w=v_ffn_conv_w, ffn_conv_b=v_ffn_conv_b, ffn_w_down=v_ffn_w_down,
             final_norm=v_final_norm)
    depth = mix_norm.shape[0]
    cx, cy, cc = _place()
    chip = 2 * cx + cy
    T = x.shape[1]

    assert depth == 2
    shard_shapes = {n: W[n].shape[1:] for n in BIG}

    def layer_of(d, li):
        return {n: d[n][li] for n in BIG}

    def assemble(g8):
        return _from_blocks(g8, shard_shapes)

    full0 = assemble(allgather_blocks(_flat_half(layer_of(W, 0), cc, BF16), name="gather_weights_l0"))
    conv_full = {}
    small = {n: W[n] for n in SMALL_REPL}
    cw_flat = jnp.concatenate([W[n].reshape(-1) for n in SMALL_SHARD])
    cw_rows = -(-cw_flat.shape[0] // (8 * LANES)) * 8
    cw_all = allgather_direct(jnp.pad(cw_flat, (0, cw_rows * LANES - cw_flat.shape[0])).reshape(cw_rows, LANES),
                              name="gather_conv_taps")
    off = 0
    for n in SMALL_SHARD:
        sz = W[n].size
        conv_full[n] = jnp.concatenate(
            [cw_all[2 * k].reshape(-1)[off:off + sz].reshape(W[n].shape) for k in range(4)], axis=2)
        off += sz
    small.update(conv_full)

    cos, sin = _rope_tables(positions[0])
    params0 = _layer_weights(full0, small, 0)
    h, s0, g8 = _layer_fwd(x[0], params0, cos, sin, 0, exchange=("gather", _flat_half(layer_of(W, 1), cc, BF16)))
    params1 = _layer_weights(assemble(g8), small, 1)
    h, s1, _ = _layer_fwd(h, params1, cos, sin, 1)
    dh, g_final, loss_lanes = loss_head(h, loss_target[0], _row(final_norm), name="loss_head")

    dh, g1, _ = _layer_bwd(dh, s1, params1, cos, sin, 1)
    by_dest = _to_blocks(g1, shard_shapes, BF16)
    dh, g0, from_all = _layer_bwd(dh, s0, params0, cos, sin, 0, exchange=("all_to_all", by_dest))
    grad_x = dh[None]
    grads = [g0, g1]
    G = {n: jnp.stack([grads[li][n] for li in range(depth)]) for n in WEIGHTS if n != "final_norm" and n not in BIG}
    G["final_norm"] = g_final[0]
    half1 = add_blocks([(from_all, d) for d in range(8)], F32, name="grads_l1_sum")

    blocks0 = _to_blocks(g0, shard_shapes, BF16)
    R = blocks0.shape[1]
    blocks0 = blocks0.reshape(4, 2, R, LANES)
    mine_first = lax.dynamic_index_in_dim(blocks0, cc, 1, keepdims=False)
    for_sibling = lax.dynamic_index_in_dim(blocks0, 1 - cc, 1, keepdims=False)
    from_sibling = send_to_sibling(for_sibling.reshape(4 * R, LANES), name="grads_to_sibling")
    pair = add_blocks([mine_first.reshape(4 * R, LANES), from_sibling], BF16, name="grads_pair_sum").reshape(4, R, LANES)
    others = chip_exchange(pair, name="grads_chip_exchange")
    own = lax.dynamic_index_in_dim(pair, chip, 0, keepdims=False)
    half0 = add_blocks([own, (others, 0), (others, 1), (others, 2)], F32, name="grads_chip_sum")
    half = jnp.concatenate([half0, half1])
    other = send_to_sibling(half, name="grads_pair_swap")
    by_core = [jnp.where(cc == 0, half, other), jnp.where(cc == 0, other, half)]
    g_big_l = [_unflat_halves([a[li * R:(li + 1) * R].reshape(-1) for a in by_core], shard_shapes) for li in range(depth)]
    g_big = {n: jnp.stack([g_big_l[li][n] for li in range(depth)]) for n in BIG}

    small_list = [G[n].reshape(-1) for n in SMALL_REPL] + [G[n].reshape(-1) for n in SMALL_SHARD]
    small_list.append(jnp.sum(loss_lanes).reshape(1))
    sm = jnp.concatenate(small_list)
    n_small = sm.shape[0]
    sm_rows = -(-n_small // (16 * LANES)) * 16
    sm_all = allgather_direct(jnp.pad(sm, (0, sm_rows * LANES - n_small)).reshape(sm_rows, LANES), name="gather_small")
    sm_sum = add_blocks([(sm_all, d) for d in range(8)], F32, name="small_sum").reshape(-1)
    g_small, off = {}, 0
    for n in SMALL_REPL:
        g_small[n] = sm_sum[off:off + W[n].size].reshape(W[n].shape)
        off += W[n].size
    for n in SMALL_SHARD:
        full_shape = conv_full[n].shape
        sz = conv_full[n].size
        gfull = sm_sum[off:off + sz].reshape(full_shape)
        width = W[n].shape[2]
        g_small[n] = lax.dynamic_slice_in_dim(gfull, chip * width, width, axis=2)
        off += sz
    loss = sm_sum[off]

    grad_out, delta, new_m, new_v = {}, {}, {}, {}
    for n in BIG:
        shp = W[n].shape
        two_d = lambda a: a.reshape(shp[0] * shp[1], shp[2])
        d, nm_, nv_ = adamw(two_d(W[n]), two_d(g_big[n]), two_d(M[n]), two_d(V[n]), name="adamw_" + n)
        grad_out[n], delta[n], new_m[n], new_v[n] = g_big[n], d.reshape(shp), nm_.reshape(shp), nv_.reshape(shp)
    small_names = SMALL_REPL + SMALL_SHARD

    def flat_small(d):
        f = jnp.concatenate([d[n].reshape(-1) for n in small_names])
        rows = -(-f.shape[0] // (8 * LANES)) * 8
        return jnp.pad(f, (0, rows * LANES - f.shape[0])).reshape(rows, LANES)

    vpad = flat_small(V)
    d, nm_, nv_ = adamw(flat_small(W), flat_small(g_small), flat_small(M), vpad, name="adamw_small")
    off = 0
    for n in small_names:
        sz = W[n].size
        grad_out[n] = g_small[n]
        delta[n] = d.reshape(-1)[off:off + sz].reshape(W[n].shape)
        new_m[n] = nm_.reshape(-1)[off:off + sz].reshape(W[n].shape)
        new_v[n] = nv_.reshape(-1)[off:off + sz].reshape(W[n].shape)
        off += sz

    return (loss, grad_x, *[grad_out[n] for n in WEIGHTS], *[delta[n] for n in WEIGHTS],
            *[new_m[n] for n in WEIGHTS], *[new_v[n] for n in WEIGHTS])
```

```python
import functools
import math

import jax
import jax.numpy as jnp
from jax import lax
from jax.experimental import pallas as pl
from jax.experimental.pallas import tpu as pltpu

F32 = jnp.float32
BF16 = jnp.bfloat16

EPS = 1e-6
D_MODEL = 1024
SB_HEADS, SB_DIM = 4, 64
SSM_HEADS, SSM_DIM, SSM_GROUPS, SSM_STATE, SSM_CHUNK = 8, 64, 2, 64, 128
SSM_INNER = SSM_HEADS * SSM_DIM
SSM_CONV_DIM = SSM_INNER + 2 * SSM_GROUPS * SSM_STATE
MLA_HEADS, MLA_NOPE, MLA_ROPE, MLA_V = 4, 64, 32, 64
MLA_QK = MLA_NOPE + MLA_ROPE
MLA_SCALE = MLA_QK ** -0.5
ROPE_THETA = 10000.0
D_FF = 2816
IN_SPLITS = (768, 512, 768, 8, 256, 128, 32)

OFF_Z, OFF_XBC, OFF_CQ, OFF_CKV, OFF_DT, OFF_KRA, OFF_KRB = 0, 512, 1280, 1536, 1664, 1792, 1920
D_REST = 2048
LANES = 128

ADAM_LR, ADAM_B1, ADAM_B2, ADAM_EPS, ADAM_WD, ADAM_STEP = 0.001, 0.9, 0.999, 1e-08, 0.01, 10

V7X_VMEM_LIMIT = 48 * 1024 * 1024

NT = (((1,), (1,)), ((), ()))
TN = (((0,), (0,)), ((), ()))

BIG = ("w_in", "mla_w_uq", "mla_w_ukv", "w_out", "ffn_w_up", "ffn_w_down")
BIG_AXIS = {"w_in": 2, "mla_w_uq": 2, "mla_w_ukv": 2, "w_out": 1, "ffn_w_up": 2, "ffn_w_down": 1}
SMALL_REPL = ("mix_norm", "sb_out_norm", "ssm_conv_b", "ssm_dt_bias", "ssm_a_log", "ssm_d", "ssm_out_norm",
              "mla_q_norm", "mla_kv_norm", "mla_out_norm", "ffn_norm", "ffn_conv_b", "final_norm")
SMALL_SHARD = ("ssm_conv_w", "ffn_conv_w")
WEIGHTS = ("mix_norm", "w_in", "sb_out_norm", "ssm_conv_w", "ssm_conv_b", "ssm_dt_bias", "ssm_a_log", "ssm_d",
           "ssm_out_norm", "mla_q_norm", "mla_w_uq", "mla_kv_norm", "mla_w_ukv", "mla_out_norm", "w_out", "ffn_norm",
           "ffn_w_up", "ffn_conv_w", "ffn_conv_b", "ffn_w_down", "final_norm")


def _cparams(*sem):
    return pltpu.CompilerParams(dimension_semantics=sem if sem else None, vmem_limit_bytes=V7X_VMEM_LIMIT)


def _pick(n, target, mult=LANES):
    best = None
    for d in range(mult, min(n, target) + 1, mult):
        if n % d == 0:
            best = d
    return best or n


def _sigmoid(x):
    return 1.0 / (1.0 + jnp.exp(-x))


def _softplus(x):
    ax = jnp.where(x > 0, x, -x)
    return jnp.where(x > 0, x, 0.0) + jnp.log(1.0 + jnp.exp(-ax))


def _rms(x, g):
    return x * lax.rsqrt(jnp.mean(x * x, axis=-1, keepdims=True) + EPS) * g


def _raw_nn(a, b):
    return jnp.dot(a.astype(BF16), b.astype(BF16), preferred_element_type=F32)


def _raw_nt(a, b):
    return lax.dot_general(a.astype(BF16), b.astype(BF16), NT, preferred_element_type=F32)


def _raw_tn(a, b):
    return lax.dot_general(a.astype(BF16), b.astype(BF16), TN, preferred_element_type=F32)


@jax.custom_vjp
def mm_nn(a, b):
    return _raw_nn(a, b)


mm_nn.defvjp(lambda a, b: (_raw_nn(a, b), (a, b)),
             lambda r, ct: (_raw_nt(ct, r[1]), _raw_tn(r[0], ct)))


@jax.custom_vjp
def mm_nt(a, b):
    return _raw_nt(a, b)


mm_nt.defvjp(lambda a, b: (_raw_nt(a, b), (a, b)),
             lambda r, ct: (_raw_nn(ct, r[1]), _raw_tn(ct, r[0])))


@jax.custom_vjp
def mm_tn(a, b):
    return _raw_tn(a, b)


mm_tn.defvjp(lambda a, b: (_raw_tn(a, b), (a, b)),
             lambda r, ct: (_raw_nt(r[1], ct), _raw_nn(r[0], ct)))


def _tri_dot(tri, x, terms=3):
    parts = []
    r = x
    for t in range(terms):
        xt = r.astype(BF16)
        parts.append(xt)
        if t + 1 < terms:
            r = r - xt.astype(F32)
    return jnp.dot(jnp.concatenate([tri] * terms, axis=1), jnp.concatenate(parts, axis=0),
                   preferred_element_type=F32)


def _tri(n, cmp):
    r = lax.broadcasted_iota(jnp.int32, (n, n), 0)
    c = lax.broadcasted_iota(jnp.int32, (n, n), 1)
    return cmp(r, c).astype(BF16)


@jax.custom_vjp
def csum_rows(x):
    return _tri_dot(_tri(x.shape[0], lambda r, c: r >= c), x)


csum_rows.defvjp(lambda x: (csum_rows(x), None),
                 lambda _, ct: (_tri_dot(_tri(ct.shape[0], lambda r, c: r <= c), ct),))


def matmul(a, b, *, name, out_dtype=F32, ta=False, tb=False, b_k0=0, residual=None):
    if ta:
        K, M = a.shape
    else:
        M, K = a.shape
    N = b.shape[0] if tb else b.shape[1]
    tm = _pick(M, 1408)
    tn = _pick(N, 1408)
    tk = _pick(K, 1408)
    nk = K // tk
    kb0 = b_k0 // tk
    assert b_k0 % tk == 0 and (tb or b_k0 == 0)
    has_res = residual is not None

    def body(*refs):
        if has_res:
            a_ref, b_ref, r_ref, o_ref, acc = refs
        else:
            a_ref, b_ref, o_ref, acc = refs
        k = pl.program_id(2)

        @pl.when(k == 0)
        def _():
            acc[...] = jnp.zeros_like(acc)

        av = a_ref[...].astype(BF16)
        bv = b_ref[...].astype(BF16)
        if ta:
            acc[...] += lax.dot_general(av, bv, TN, preferred_element_type=F32)
        elif tb:
            acc[...] += lax.dot_general(av, bv, NT, preferred_element_type=F32)
        else:
            acc[...] += jnp.dot(av, bv, preferred_element_type=F32)

        @pl.when(k == nk - 1)
        def _():
            r = acc[...]
            if has_res:
                r = r + r_ref[...].astype(F32)
            o_ref[...] = r.astype(o_ref.dtype)

    a_spec = pl.BlockSpec((tk, tm), lambda i, j, k: (k, i)) if ta else pl.BlockSpec((tm, tk), lambda i, j, k: (i, k))
    b_spec = pl.BlockSpec((tn, tk), lambda i, j, k: (j, kb0 + k)) if tb else pl.BlockSpec((tk, tn), lambda i, j, k: (k, j))
    in_specs = [a_spec, b_spec]
    args = [a, b]
    if has_res:
        in_specs.append(pl.BlockSpec((tm, tn), lambda i, j, k: (i, j)))
        args.append(residual)
    return pl.pallas_call(
        body, name=name, grid=(M // tm, N // tn, nk),
        in_specs=in_specs, out_specs=pl.BlockSpec((tm, tn), lambda i, j, k: (i, j)),
        out_shape=jax.ShapeDtypeStruct((M, N), out_dtype),
        scratch_shapes=[pltpu.VMEM((tm, tn), F32)],
        compiler_params=_cparams("parallel", "parallel", "arbitrary"),
    )(*args)


def _row_spec(entry, tl):
    if isinstance(entry, tuple):
        arr, width, cb = entry
        return arr, pl.BlockSpec((tl, width), lambda i, cb=cb: (i, cb))
    return entry, pl.BlockSpec((tl, entry.shape[1]), lambda i: (i, 0))


def _rows_T(entry):
    return (entry[0] if isinstance(entry, tuple) else entry).shape[0]


def rowwise(fn, rows, params, outs, *, name, tile=512, post=None):
    T = _rows_T(rows[0])
    tl = min(T, tile)
    nr, npar = len(rows), len(params)

    def body(*refs):
        r = [ref[...].astype(F32) for ref in refs[:nr]]
        p = [ref[...].astype(F32) for ref in refs[nr:nr + npar]]
        res = fn(*r, *p)
        if post is not None:
            res = post(*res)
        for o_ref, val in zip(refs[nr + npar:], res):
            o_ref[...] = val.astype(o_ref.dtype)

    arrs, specs = [], []
    for e in rows:
        a, s = _row_spec(e, tl)
        arrs.append(a)
        specs.append(s)
    for p in params:
        arrs.append(p)
        specs.append(pl.BlockSpec(p.shape, lambda i: (0, 0)))
    res = pl.pallas_call(
        body, name=name, grid=(T // tl,), in_specs=specs,
        out_specs=[pl.BlockSpec((tl, c), lambda i: (i, 0)) for c, _ in outs],
        out_shape=[jax.ShapeDtypeStruct((T, c), dt) for c, dt in outs],
        compiler_params=_cparams("parallel"),
    )(*arrs)
    return res


def rowwise_bwd(fn, rows, nd_rows, params, cts, grad_dtypes, *, name, tile=512, pre_ct=None, add0=None):
    T = _rows_T(rows[0])
    tl = min(T, tile)
    nr, nn, npar, nc = len(rows), len(nd_rows), len(params), len(cts)
    has_add = add0 is not None

    def body(*refs):
        pos = 0
        r = [ref[...].astype(F32) for ref in refs[pos:pos + nr]]
        pos += nr
        nd = [ref[...].astype(F32) for ref in refs[pos:pos + nn]]
        pos += nn
        p = [ref[...].astype(F32) for ref in refs[pos:pos + npar]]
        pos += npar
        c = [ref[...].astype(F32) for ref in refs[pos:pos + nc]]
        pos += nc
        if has_add:
            addv = refs[pos][...].astype(F32)
            pos += 1
        rg_refs = refs[pos:pos + nr]
        pg_refs = refs[pos + nr:pos + nr + npar]
        if pre_ct is not None:
            c = list(pre_ct(*c))
        _, vjp = jax.vjp(lambda *a: fn(*a[:nr], *nd, *a[nr:]), *r, *p)
        g = vjp(tuple(c))
        for j, ref in enumerate(rg_refs):
            val = g[j]
            if has_add and j == 0:
                val = val + addv
            ref[...] = val.astype(ref.dtype)
        if npar:
            @pl.when(pl.program_id(0) == 0)
            def _():
                for ref in pg_refs:
                    ref[...] = jnp.zeros_like(ref)
            for j, ref in enumerate(pg_refs):
                ref[...] += g[nr + j]

    arrs, specs = [], []
    widths = []
    for e in list(rows) + list(nd_rows):
        a, s = _row_spec(e, tl)
        arrs.append(a)
        specs.append(s)
        widths.append(s.block_shape[1])
    for p in params:
        arrs.append(p)
        specs.append(pl.BlockSpec(p.shape, lambda i: (0, 0)))
    for e in cts:
        a, s = _row_spec(e, tl)
        arrs.append(a)
        specs.append(s)
    if has_add:
        a, s = _row_spec(add0, tl)
        arrs.append(a)
        specs.append(s)
    out_specs = [pl.BlockSpec((tl, widths[j]), lambda i: (i, 0)) for j in range(nr)]
    out_shape = [jax.ShapeDtypeStruct((T, widths[j]), grad_dtypes[j]) for j in range(nr)]
    out_specs += [pl.BlockSpec(p.shape, lambda i: (0, 0)) for p in params]
    out_shape += [jax.ShapeDtypeStruct(p.shape, F32) for p in params]
    res = pl.pallas_call(
        body, name=name, grid=(T // tl,), in_specs=specs, out_specs=out_specs, out_shape=out_shape,
        compiler_params=_cparams("arbitrary"),
    )(*arrs)
    return list(res[:nr]), list(res[nr:])


def rms_fn(h, g):
    return (_rms(h, g),)


def merge_fn(ysb, yssm, z, ymla, g_sb, g_ssm, g_mla):
    ya = _rms(ysb, g_sb)
    yb = _rms(yssm * (z * _sigmoid(z)), g_ssm)
    yc = _rms(ymla, g_mla)
    return ya, yb, yc


def mla_prep_fn(cq, ckv, kra, krb, cos, sin, qn, kvn, wqp, wqr, wkp, wvp):
    cos4 = jnp.concatenate([cos] * MLA_HEADS, axis=1)
    sin4 = jnp.concatenate([sin] * MLA_HEADS, axis=1)
    nq = _rms(cq, qn)
    q = (mm_nn(nq, wqp) * cos4 + mm_nn(nq, wqr) * sin4) * MLA_SCALE
    nkv = _rms(ckv, kvn)
    kpe = kra * cos + krb * sin
    k = mm_nn(nkv, wkp) + jnp.concatenate([kpe] * MLA_HEADS, axis=1)
    v = mm_nn(nkv, wvp)
    return q, k, v


HALO = 8


def _prev_halo_spec(tl, tc, col_of):
    return pl.BlockSpec((HALO, tc), lambda i, j: (jnp.maximum(i * (tl // HALO) - 1, 0), col_of(j)))


def _fill_prev(buf, x_ref, halo_ref, i):
    buf[0:HALO, :] = jnp.where(i > 0, halo_ref[...].astype(F32), 0.0)
    buf[HALO:, :] = x_ref[...].astype(F32)


def _conv_from(buf, w_ref, b_ref, K, tl):
    acc = b_ref[...].astype(F32) + jnp.zeros((tl, buf.shape[1]), F32)
    for k in range(K):
        acc = acc + buf[pl.ds(HALO - (K - 1 - k), tl), :] * w_ref[k:k + 1, :].astype(F32)
    return acc


def ssm_conv_act(proj, w, b, *, name, tile=512, tc=256):
    T = proj.shape[0]
    K, C = w.shape
    tl = min(T, tile)
    c0 = OFF_XBC // tc
    nb = C // tc
    hd = SSM_DIM
    per = tc // hd

    def body(*refs):
        xs, halos = refs[0:nb], refs[nb:2 * nb]
        w_ref, b_ref = refs[2 * nb:2 * nb + 2]
        x_out, b_out, c_out = refs[2 * nb + 2:2 * nb + 5]
        bufs = refs[2 * nb + 5:]
        for j in range(nb):
            cols = slice(j * tc, (j + 1) * tc)
            _fill_prev(bufs[j], xs[j], halos[j], pl.program_id(0))
            u = b_ref[:, cols].astype(F32) + jnp.zeros((tl, tc), F32)
            for k in range(K):
                u = u + bufs[j][pl.ds(HALO - (K - 1 - k), tl), :] * w_ref[k:k + 1, cols].astype(F32)
            act = u * _sigmoid(u)
            for hh in range(per):
                piece = act[:, hh * hd:(hh + 1) * hd]
                head = j * per + hh
                if head < SSM_HEADS:
                    x_out[head] = piece
                elif head < SSM_HEADS + SSM_GROUPS:
                    b_out[head - SSM_HEADS] = piece
                else:
                    c_out[head - SSM_HEADS - SSM_GROUPS] = piece

    in_specs = ([pl.BlockSpec((tl, tc), lambda i, j=j: (i, c0 + j)) for j in range(nb)]
                + [pl.BlockSpec((HALO, tc), lambda i, j=j: (jnp.maximum(i * (tl // HALO) - 1, 0), c0 + j)) for j in range(nb)]
                + [pl.BlockSpec((K, C), lambda i: (0, 0)), pl.BlockSpec((1, C), lambda i: (0, 0))])
    return pl.pallas_call(
        body, name=name, grid=(T // tl,), in_specs=in_specs,
        out_specs=[pl.BlockSpec((SSM_HEADS, tl, hd), lambda i: (0, i, 0)),
                   pl.BlockSpec((SSM_GROUPS, tl, hd), lambda i: (0, i, 0)),
                   pl.BlockSpec((SSM_GROUPS, tl, hd), lambda i: (0, i, 0))],
        out_shape=[jax.ShapeDtypeStruct((SSM_HEADS, T, hd), F32), jax.ShapeDtypeStruct((SSM_GROUPS, T, hd), F32),
                   jax.ShapeDtypeStruct((SSM_GROUPS, T, hd), F32)],
        scratch_shapes=[pltpu.VMEM((tl + HALO, tc), F32)] * nb,
        compiler_params=_cparams("parallel"),
    )(*([proj] * (2 * nb)), w, b)


def ssm_conv_bwd_a(proj, w, b, d_out, *, name, tile=512, tc=256):
    T = proj.shape[0]
    K, C = w.shape
    tl = min(T, tile)
    c0 = OFF_XBC // tc

    def body(x_ref, halo_ref, w_ref, b_ref, d_ref, o_ref, buf):
        _fill_prev(buf, x_ref, halo_ref, pl.program_id(0))
        u = _conv_from(buf, w_ref, b_ref, K, tl)
        s = _sigmoid(u)
        o_ref[...] = d_ref[...].astype(F32) * (s * (1.0 + u * (1.0 - s)))

    return pl.pallas_call(
        body, name=name, grid=(T // tl, C // tc),
        in_specs=[pl.BlockSpec((tl, tc), lambda i, j: (i, c0 + j)), _prev_halo_spec(tl, tc, lambda j: c0 + j),
                  pl.BlockSpec((K, tc), lambda i, j: (0, j)), pl.BlockSpec((1, tc), lambda i, j: (0, j)),
                  pl.BlockSpec((tl, tc), lambda i, j: (i, j))],
        out_specs=pl.BlockSpec((tl, tc), lambda i, j: (i, j)),
        out_shape=jax.ShapeDtypeStruct((T, C), F32),
        scratch_shapes=[pltpu.VMEM((tl + HALO, tc), F32)],
        compiler_params=_cparams("parallel", "parallel"),
    )(proj, proj, w, b, d_out)


def ffn_act(up, w, b, *, name, tile=512, tc=1408):
    T = up.shape[0]
    K = w.shape[0]
    tl = min(T, tile)
    nj = D_FF // tc

    def body(xg_ref, hg_ref, xv_ref, hv_ref, wg_ref, wv_ref, bg_ref, bv_ref, o_ref, bufg, bufv):
        i = pl.program_id(0)
        _fill_prev(bufg, xg_ref, hg_ref, i)
        _fill_prev(bufv, xv_ref, hv_ref, i)
        gate = _conv_from(bufg, wg_ref, bg_ref, K, tl)
        val = _conv_from(bufv, wv_ref, bv_ref, K, tl)
        o_ref[...] = (gate * _sigmoid(gate) * val).astype(o_ref.dtype)

    return pl.pallas_call(
        body, name=name, grid=(T // tl, nj),
        in_specs=[pl.BlockSpec((tl, tc), lambda i, j: (i, j)), _prev_halo_spec(tl, tc, lambda j: j),
                  pl.BlockSpec((tl, tc), lambda i, j: (i, nj + j)), _prev_halo_spec(tl, tc, lambda j: nj + j),
                  pl.BlockSpec((K, tc), lambda i, j: (0, j)), pl.BlockSpec((K, tc), lambda i, j: (0, nj + j)),
                  pl.BlockSpec((1, tc), lambda i, j: (0, j)), pl.BlockSpec((1, tc), lambda i, j: (0, nj + j))],
        out_specs=pl.BlockSpec((tl, tc), lambda i, j: (i, j)),
        out_shape=jax.ShapeDtypeStruct((T, D_FF), BF16),
        scratch_shapes=[pltpu.VMEM((tl + HALO, tc), F32), pltpu.VMEM((tl + HALO, tc), F32)],
        compiler_params=_cparams("parallel", "parallel"),
    )(up, up, up, up, w, w, b, b)


def ffn_bwd_fused(up, w, b, d_act, *, name, tile=1024, tc=256):
    T = up.shape[0]
    K = w.shape[0]
    tl = min(T, tile)
    nj = D_FF // tc
    nblk = T // HALO
    ext = tl + HALO

    def body(xg, hgp, hgn, xv, hvp, hvn, wg, wv, bg, bv, d, dn, og, ov, dwg, dwv, dbg, dbv, bufg, bufv, dgb, dvb):
        i = pl.program_id(1)
        last = pl.num_programs(1) - 1

        def fill(buf, x_ref, prev_ref, next_ref):
            buf[0:HALO, :] = jnp.where(i > 0, prev_ref[...].astype(F32), 0.0)
            buf[HALO:HALO + tl, :] = x_ref[...].astype(F32)
            buf[HALO + tl:, :] = jnp.where(i < last, next_ref[...].astype(F32), 0.0)

        def conv_ext(buf, w_ref, b_ref):
            acc = b_ref[...].astype(F32) + jnp.zeros((ext, tc), F32)
            for k in range(K):
                acc = acc + buf[pl.ds(HALO - (K - 1 - k), ext), :] * w_ref[k:k + 1, :].astype(F32)
            return acc

        fill(bufg, xg, hgp, hgn)
        fill(bufv, xv, hvp, hvn)
        gate = conv_ext(bufg, wg, bg)
        val = conv_ext(bufv, wv, bv)
        dd = jnp.concatenate([d[...].astype(F32), jnp.where(i < last, dn[...].astype(F32)[0:HALO], 0.0)], axis=0)
        s = _sigmoid(gate)
        dgb[...] = dd * val * (s * (1.0 + gate * (1.0 - s)))
        dvb[...] = dd * (gate * s)

        @pl.when(i == 0)
        def _():
            for ref in (dwg, dwv, dbg, dbv):
                ref[...] = jnp.zeros_like(ref)

        for dbuf, xbuf, w_ref, o_ref, dw_ref, db_ref in ((dgb, bufg, wg, og, dwg, dbg), (dvb, bufv, wv, ov, dwv, dbv)):
            cur = dbuf[0:tl, :]
            dx = jnp.zeros((tl, tc), F32)
            for k in range(K):
                sft = K - 1 - k
                dx = dx + dbuf[pl.ds(sft, tl), :] * w_ref[k:k + 1, :].astype(F32)
                dw_ref[k:k + 1, :] += jnp.sum(cur * xbuf[pl.ds(HALO - sft, tl), :], axis=0, keepdims=True)
            db_ref[...] += jnp.sum(cur, axis=0, keepdims=True)
            o_ref[...] = dx.astype(o_ref.dtype)

    prev = lambda i: jnp.maximum(i * (tl // HALO) - 1, 0)
    nxt = lambda i: jnp.minimum((i + 1) * (tl // HALO), nblk - 1)

    def x_specs(col):
        return [pl.BlockSpec((tl, tc), lambda j, i: (i, col(j))), pl.BlockSpec((HALO, tc), lambda j, i: (prev(i), col(j))),
                pl.BlockSpec((HALO, tc), lambda j, i: (nxt(i), col(j)))]

    gcol, vcol = (lambda j: j), (lambda j: nj + j)
    in_specs = (x_specs(gcol) + x_specs(vcol)
                + [pl.BlockSpec((K, tc), lambda j, i: (0, j)), pl.BlockSpec((K, tc), lambda j, i: (0, nj + j)),
                   pl.BlockSpec((1, tc), lambda j, i: (0, j)), pl.BlockSpec((1, tc), lambda j, i: (0, nj + j)),
                   pl.BlockSpec((tl, tc), lambda j, i: (i, j)),
                   pl.BlockSpec((2 * HALO, tc), lambda j, i: (jnp.minimum((i + 1) * (tl // (2 * HALO)), nblk // 2 - 1), j))])
    row_out = pl.BlockSpec((tl, tc), lambda j, i: (i, j))
    w_out = pl.BlockSpec((K, tc), lambda j, i: (0, j))
    b_out = pl.BlockSpec((1, tc), lambda j, i: (0, j))
    return pl.pallas_call(
        body, name=name, grid=(nj, T // tl), in_specs=in_specs,
        out_specs=[row_out, row_out, w_out, w_out, b_out, b_out],
        out_shape=[jax.ShapeDtypeStruct((T, D_FF), BF16)] * 2 + [jax.ShapeDtypeStruct((K, D_FF), F32)] * 2
        + [jax.ShapeDtypeStruct((1, D_FF), F32)] * 2,
        scratch_shapes=[pltpu.VMEM((tl + 2 * HALO, tc), F32)] * 2 + [pltpu.VMEM((ext, tc), F32)] * 2,
        compiler_params=_cparams("parallel", "arbitrary"),
    )(up, up, up, up, up, up, w, w, b, b, d_act, d_act)


def conv_bwd_b(du, x, x_off, w, *, name, out_dtype, tile=512, tc=256):
    T, C = du.shape
    K = w.shape[0]
    tl = min(T, tile)
    c0 = x_off // tc
    nblk = T // HALO

    def body(du_ref, nx_ref, x_ref, w_ref, dx_ref, dw_ref, db_ref, dbuf):
        i = pl.program_id(1)
        last = pl.num_programs(1) - 1
        d = du_ref[...].astype(F32)
        dbuf[0:tl, :] = d
        dbuf[tl:, :] = jnp.where(i < last, nx_ref[...].astype(F32), 0.0)

        @pl.when(i == 0)
        def _():
            dw_ref[...] = jnp.zeros_like(dw_ref)
            db_ref[...] = jnp.zeros_like(db_ref)

        xin = x_ref[...].astype(F32)
        dx = jnp.zeros((tl, tc), F32)
        for k in range(K):
            s = K - 1 - k
            shifted = dbuf[pl.ds(s, tl), :]
            dx = dx + shifted * w_ref[k:k + 1, :].astype(F32)
            dw_ref[k:k + 1, :] += jnp.sum(shifted * xin, axis=0, keepdims=True)
        db_ref[...] += jnp.sum(d, axis=0, keepdims=True)
        dx_ref[...] = dx.astype(dx_ref.dtype)

    return pl.pallas_call(
        body, name=name, grid=(C // tc, T // tl),
        in_specs=[pl.BlockSpec((tl, tc), lambda j, i: (i, j)),
                  pl.BlockSpec((HALO, tc), lambda j, i: (jnp.minimum((i + 1) * (tl // HALO), nblk - 1), j)),
                  pl.BlockSpec((tl, tc), lambda j, i: (i, c0 + j)),
                  pl.BlockSpec((K, tc), lambda j, i: (0, j))],
        out_specs=[pl.BlockSpec((tl, tc), lambda j, i: (i, j)), pl.BlockSpec((K, tc), lambda j, i: (0, j)),
                   pl.BlockSpec((1, tc), lambda j, i: (0, j))],
        out_shape=[jax.ShapeDtypeStruct((T, C), out_dtype), jax.ShapeDtypeStruct((K, C), F32),
                   jax.ShapeDtypeStruct((1, C), F32)],
        scratch_shapes=[pltpu.VMEM((tl + HALO, tc), F32)],
        compiler_params=_cparams("parallel", "arbitrary"),
    )(du, du, x, w)


SB_QUERIES = 1024


def _attn_tiles(T, keys=256, queries=1024):
    return min(T, queries), min(T, keys)


def _after_diag(keys, queries, strict):
    d = lax.broadcasted_iota(jnp.int32, (keys, queries), 1) - lax.broadcasted_iota(jnp.int32, (keys, queries), 0)
    return d > 0 if strict else d >= 0


def _log_gates(z):
    l1p = jnp.log(1.0 + jnp.exp(-jnp.abs(z)))
    a = jnp.minimum(z, 0.0) - l1p
    return a, a - z


def _causal_sweep(i, tq, tk, block, descending, keep_going=None, first_block=None):
    nb = tq // tk
    n_full = i * nb

    def band():
        order = reversed(range(nb)) if descending else range(nb)
        for bb in order:
            block(pl.multiple_of(i * tq + bb * tk, tk), bb * tk, True)

    def full():
        if descending and keep_going is not None:
            def step(j):
                block(pl.multiple_of((n_full - 1 - j) * tk, tk), 0, False)
                return j + 1
            done = lax.while_loop(lambda j: jnp.logical_and(j < n_full, keep_going()), step, jnp.int32(0))
            return n_full - done

        def step(j, c):
            kb = (n_full - 1 - j) if descending else j
            block(pl.multiple_of(kb * tk, tk), 0, False)
            return c
        lax.fori_loop(0 if first_block is None else first_block, n_full, step, 0)
        return None

    if descending:
        band()
        return full()
    full()
    band()
    return None


def sb_fwd(q, k, v, *, name):
    H, dh, T = q.shape
    tq, tk = _attn_tiles(T, queries=SB_QUERIES)

    def body(q_ref, k_ref, v_ref, y_ref, bt_ref, first_ref, acc, run):
        acc[...] = jnp.zeros_like(acc)
        run[...] = jnp.zeros_like(run)
        u_after = _tri(tk, lambda r, c: r < c)

        def block(k0, r0, masked):
            kb = k_ref[pl.ds(k0, tk), :]
            vb = v_ref[pl.ds(k0, tk), :]
            z = jnp.dot(kb, q_ref[:, r0:], preferred_element_type=F32)
            a, b = _log_gates(z)
            if masked:
                valid = _after_diag(tk, tq - r0, True)
                b = jnp.where(valid, b, 0.0)
            w = jnp.exp(a + _tri_dot(u_after, b, 2) + run[:, r0:])
            if masked:
                w = jnp.where(valid, w, 0.0)
            acc[:, r0:] += lax.dot_general(vb, w.astype(BF16), TN, preferred_element_type=F32)
            run[:, r0:] += jnp.sum(b, axis=0, keepdims=True)

        first = _causal_sweep(pl.program_id(1), tq, tk, block, descending=True,
                              keep_going=lambda: jnp.max(run[...]) >= SB_ZERO_BELOW)
        y_ref[...] = acc[...]
        bt_ref[...] = run[...]
        first_ref[...] = jnp.zeros(first_ref.shape, F32) + first.astype(F32)

    return pl.pallas_call(
        body, name=name, grid=(H, T // tq),
        in_specs=[pl.BlockSpec((None, dh, tq), lambda h, i: (h, 0, i)),
                  pl.BlockSpec((None, T, dh), lambda h, i: (h, 0, 0)),
                  pl.BlockSpec((None, T, dh), lambda h, i: (h, 0, 0))],
        out_specs=[pl.BlockSpec((None, dh, tq), lambda h, i: (h, 0, i)),
                   pl.BlockSpec((None, 1, tq), lambda h, i: (h, 0, i)),
                   pl.BlockSpec((None, None, HALO, LANES), lambda h, i: (h, i, 0, 0))],
        out_shape=[jax.ShapeDtypeStruct((H, dh, T), F32), jax.ShapeDtypeStruct((H, 1, T), F32),
                   jax.ShapeDtypeStruct((H, T // tq, HALO, LANES), F32)],
        scratch_shapes=[pltpu.VMEM((dh, tq), F32), pltpu.VMEM((1, tq), F32)],
        compiler_params=_cparams("parallel", "parallel"),
    )(q, k, v)


def sb_bwd(q, k, v, dy, btot, first, *, name, q_scale):
    H, dh, T = q.shape
    tq, tk = _attn_tiles(T, queries=SB_QUERIES)

    def body(q_ref, k_ref, v_ref, dy_ref, bt_ref, first_ref, dq_ref, dk_ref, dv_ref, dq, pb, pg, dyb):
        @pl.when(pl.program_id(1) == 0)
        def _():
            dk_ref[...] = jnp.zeros_like(dk_ref)
            dv_ref[...] = jnp.zeros_like(dv_ref)

        dq[...] = jnp.zeros_like(dq)
        pb[...] = jnp.zeros_like(pb)
        pg[...] = jnp.zeros_like(pg)
        dyb[...] = dy_ref[...].astype(BF16)
        u_upto = _tri(tk, lambda r, c: r >= c)
        u_before = _tri(tk, lambda r, c: r > c)

        def block(k0, r0, masked):
            kb = k_ref[pl.ds(k0, tk), :]
            vb = v_ref[pl.ds(k0, tk), :]
            qv = q_ref[:, r0:]
            dyv = dyb[:, r0:]
            z = jnp.dot(kb, qv, preferred_element_type=F32)
            a, b = _log_gates(z)
            if masked:
                valid = _after_diag(tk, tq - r0, True)
                b = jnp.where(valid, b, 0.0)
            w = jnp.exp(a + (bt_ref[:, r0:] - pb[:, r0:] - _tri_dot(u_upto, b, 2)))
            if masked:
                w = jnp.where(valid, w, 0.0)
            g = w * jnp.dot(vb, dyv, preferred_element_type=F32)
            dz = g - jnp.exp(a) * (g + pg[:, r0:] + _tri_dot(u_before, g, 2))
            if masked:
                dz = jnp.where(valid, dz, 0.0)
            dz = dz.astype(BF16)
            dq[:, r0:] += lax.dot_general(kb, dz, TN, preferred_element_type=F32)
            dk_ref[pl.ds(k0, tk), :] += lax.dot_general(dz, qv, NT, preferred_element_type=F32)
            dv_ref[pl.ds(k0, tk), :] += lax.dot_general(w.astype(BF16), dyv, NT, preferred_element_type=F32)
            pb[:, r0:] += jnp.sum(b, axis=0, keepdims=True)
            pg[:, r0:] += jnp.sum(g, axis=0, keepdims=True)

        i = pl.program_id(1)
        first = jnp.clip(jnp.max(first_ref[...]).astype(jnp.int32), 0, i * (tq // tk))
        _causal_sweep(i, tq, tk, block, descending=False, first_block=first)
        dq_ref[...] = dq[...] * q_scale

    return pl.pallas_call(
        body, name=name, grid=(H, T // tq),
        in_specs=[pl.BlockSpec((None, dh, tq), lambda h, i: (h, 0, i)),
                  pl.BlockSpec((None, T, dh), lambda h, i: (h, 0, 0)),
                  pl.BlockSpec((None, T, dh), lambda h, i: (h, 0, 0)),
                  pl.BlockSpec((None, dh, tq), lambda h, i: (h, 0, i)),
                  pl.BlockSpec((None, 1, tq), lambda h, i: (h, 0, i)),
                  pl.BlockSpec((None, None, HALO, LANES), lambda h, i: (h, i, 0, 0))],
        out_specs=[pl.BlockSpec((None, dh, tq), lambda h, i: (h, 0, i)),
                   pl.BlockSpec((None, T, dh), lambda h, i: (h, 0, 0)),
                   pl.BlockSpec((None, T, dh), lambda h, i: (h, 0, 0))],
        out_shape=[jax.ShapeDtypeStruct((H, dh, T), F32), jax.ShapeDtypeStruct((H, T, dh), F32),
                   jax.ShapeDtypeStruct((H, T, dh), F32)],
        scratch_shapes=[pltpu.VMEM((dh, tq), F32), pltpu.VMEM((1, tq), F32), pltpu.VMEM((1, tq), F32),
                        pltpu.VMEM((dh, tq), BF16)],
        compiler_params=_cparams("parallel", "arbitrary"),
    )(q, k, v, dy, btot, first)


NEG = -1e30
SB_ZERO_BELOW = -105.0
MLA_KEYS = 512


def _call_with_exchange(body, exchange, *, name, grid, in_specs, out_specs, out_shape, scratch_shapes, args):
    if exchange is None:
        return pl.pallas_call(body, name=name, grid=grid, in_specs=in_specs, out_specs=out_specs, out_shape=out_shape,
                              scratch_shapes=scratch_shapes, compiler_params=_cparams("parallel", "arbitrary"))(*args)
    kind, src = exchange
    n_in, n_out, n_scr = len(in_specs), len(out_specs), len(scratch_shapes)
    R = src.shape[-2]

    def wrapped(*refs):
        ins, src_ref = refs[:n_in], refs[n_in]
        outs, xout = refs[n_in + 1:n_in + 1 + n_out], refs[n_in + 1 + n_out]
        scr = refs[n_in + 2 + n_out:n_in + 2 + n_out + n_scr]
        start, finish = _direct_exchange(kind, src_ref, xout, *refs[-3:])
        step = pl.program_id(0) * pl.num_programs(1) + pl.program_id(1)
        pl.when(step == 0)(start)
        body(*ins, *outs, *scr)
        pl.when(step == pl.num_programs(0) * pl.num_programs(1) - 1)(finish)

    return pl.pallas_call(
        wrapped, name=name, grid=grid, in_specs=list(in_specs) + [HBM_SPEC], out_specs=list(out_specs) + [HBM_SPEC],
        out_shape=list(out_shape) + [jax.ShapeDtypeStruct((8, R, LANES), src.dtype)],
        scratch_shapes=list(scratch_shapes) + [pltpu.SemaphoreType.DMA((7,)), pltpu.SemaphoreType.DMA((7,)),
                                               pltpu.SemaphoreType.DMA],
        compiler_params=_cparams("arbitrary", "arbitrary"))(*args, src)


def mla_fwd(q, k, v, *, name, exchange=None):
    H, dk, T = q.shape
    dv = v.shape[1] // H
    tq, tk = _attn_tiles(T, MLA_KEYS)

    def body(q_ref, k_ref, v_ref, o_ref, l_ref, acc, m_s, l_s):
        acc[...] = jnp.zeros_like(acc)
        m_s[...] = jnp.full_like(m_s, NEG)
        l_s[...] = jnp.zeros_like(l_s)

        def block(k0, r0, masked):
            kb = k_ref[pl.ds(k0, tk), :]
            vb = v_ref[pl.ds(k0, tk), :]
            s = jnp.dot(kb, q_ref[:, r0:], preferred_element_type=F32)
            if masked:
                s = jnp.where(_after_diag(tk, tq - r0, False), s, NEG)
            m = m_s[:, r0:]
            m_new = jnp.maximum(m, jnp.max(s, axis=0, keepdims=True))
            p = jnp.exp(s - m_new)
            alpha = jnp.exp(m - m_new)
            l_s[:, r0:] = alpha * l_s[:, r0:] + jnp.sum(p, axis=0, keepdims=True)
            acc[:, r0:] = alpha * acc[:, r0:] + lax.dot_general(vb, p.astype(BF16), TN, preferred_element_type=F32)
            m_s[:, r0:] = m_new

        _causal_sweep(pl.program_id(1), tq, tk, block, descending=False)
        o_ref[...] = acc[...] / l_s[...]
        l_ref[...] = m_s[...] + jnp.log(l_s[...])

    return _call_with_exchange(
        body, exchange, name=name, grid=(H, T // tq),
        in_specs=[pl.BlockSpec((None, dk, tq), lambda h, i: (h, 0, i)),
                  pl.BlockSpec((T, dk), lambda h, i: (0, h)),
                  pl.BlockSpec((T, dv), lambda h, i: (0, h))],
        out_specs=[pl.BlockSpec((None, dv, tq), lambda h, i: (h, 0, i)),
                   pl.BlockSpec((None, 1, tq), lambda h, i: (h, 0, i))],
        out_shape=[jax.ShapeDtypeStruct((H, dv, T), F32), jax.ShapeDtypeStruct((H, 1, T), F32)],
        scratch_shapes=[pltpu.VMEM((dv, tq), F32), pltpu.VMEM((1, tq), F32), pltpu.VMEM((1, tq), F32)],
        args=(q, k, v))


def mla_bwd(q, k, v, do, o, lse, *, name, exchange=None):
    H, dk, T = q.shape
    dv = v.shape[1] // H
    tq, tk = _attn_tiles(T, MLA_KEYS)

    def body(q_ref, k_ref, v_ref, do_ref, o_ref, l_ref, dq_ref, dk_ref, dv_ref, dq, delta, dob):
        @pl.when(pl.program_id(1) == 0)
        def _():
            dk_ref[...] = jnp.zeros_like(dk_ref)
            dv_ref[...] = jnp.zeros_like(dv_ref)

        dq[...] = jnp.zeros_like(dq)
        dov = do_ref[...].astype(F32)
        dob[...] = dov.astype(BF16)
        delta[...] = jnp.sum(dov * o_ref[...], axis=0, keepdims=True)

        def block(k0, r0, masked):
            kb = k_ref[pl.ds(k0, tk), :]
            vb = v_ref[pl.ds(k0, tk), :]
            qv = q_ref[:, r0:]
            dov_b = dob[:, r0:]
            s = jnp.dot(kb, qv, preferred_element_type=F32)
            p = jnp.exp(s - l_ref[:, r0:])
            if masked:
                p = jnp.where(_after_diag(tk, tq - r0, False), p, 0.0)
            dp = jnp.dot(vb, dov_b, preferred_element_type=F32)
            ds = (p * (dp - delta[:, r0:])).astype(BF16)
            dq[:, r0:] += lax.dot_general(kb, ds, TN, preferred_element_type=F32)
            dk_ref[pl.ds(k0, tk), :] += lax.dot_general(ds, qv, NT, preferred_element_type=F32)
            dv_ref[pl.ds(k0, tk), :] += lax.dot_general(p.astype(BF16), dov_b, NT, preferred_element_type=F32)

        _causal_sweep(pl.program_id(1), tq, tk, block, descending=False)
        dq_ref[...] = dq[...]

    return _call_with_exchange(
        body, exchange, name=name, grid=(H, T // tq),
        in_specs=[pl.BlockSpec((None, dk, tq), lambda h, i: (h, 0, i)),
                  pl.BlockSpec((T, dk), lambda h, i: (0, h)),
                  pl.BlockSpec((T, dv), lambda h, i: (0, h)),
                  pl.BlockSpec((None, dv, tq), lambda h, i: (h, 0, i)),
                  pl.BlockSpec((None, dv, tq), lambda h, i: (h, 0, i)),
                  pl.BlockSpec((None, 1, tq), lambda h, i: (h, 0, i))],
        out_specs=[pl.BlockSpec((None, dk, tq), lambda h, i: (h, 0, i)),
                   pl.BlockSpec((T, dk), lambda h, i: (0, h)),
                   pl.BlockSpec((T, dv), lambda h, i: (0, h))],
        out_shape=[jax.ShapeDtypeStruct((H, dk, T), F32), jax.ShapeDtypeStruct((T, H * dk), F32),
                   jax.ShapeDtypeStruct((T, H * dv), F32)],
        scratch_shapes=[pltpu.VMEM((dk, tq), F32), pltpu.VMEM((1, tq), F32), pltpu.VMEM((dv, tq), BF16)],
        args=(q, k, v, do, o, lse))


def _lane_pick(x, h):
    lane = lax.broadcasted_iota(jnp.int32, (1, x.shape[1]), 1)
    return jnp.sum(jnp.where(lane == h, x, 0.0), axis=1, keepdims=True)


def _row_pick(x, h):
    sub = lax.broadcasted_iota(jnp.int32, (x.shape[0], 1), 0)
    return jnp.sum(jnp.where(sub == h, x, 0.0), axis=0, keepdims=True)


def ssd_chunk_fn(*args):
    nh, ng = SSM_HEADS, SSM_GROUPS
    xs = args[:nh]
    bs = args[nh:nh + ng]
    cs = args[nh + ng:nh + 2 * ng]
    dt_raw = args[nh + 2 * ng]
    st = args[nh + 2 * ng + 1:nh + 2 * ng + 1 + nh]
    dt_bias, a_log, d_skip = args[nh + 2 * ng + 1 + nh:]
    L = dt_raw.shape[0]
    dt = _softplus(dt_raw + dt_bias)
    da = dt * (-jnp.exp(a_log))
    dcs = csum_rows(da)
    dcs_t = dcs.T
    total = jnp.sum(da, axis=0, keepdims=True)
    causal = lax.broadcasted_iota(jnp.int32, (L, L), 0) >= lax.broadcasted_iota(jnp.int32, (L, L), 1)
    cb = [mm_nt(cs[g], bs[g]) for g in range(ng)]
    ys, new_st = [], []
    for h in range(nh):
        g = h // (nh // ng)
        dcs_h = _lane_pick(dcs, h)
        dt_h = _lane_pick(dt, h)
        tot_h = _lane_pick(total, h)
        dsk_h = _lane_pick(d_skip, h)
        decay = jnp.exp(jnp.where(causal, dcs_h - _row_pick(dcs_t, h), NEG))
        xdt = xs[h] * dt_h
        y = mm_nn(cb[g] * decay, xdt)
        y = y + mm_nn(cs[g] * jnp.exp(dcs_h), st[h])
        ys.append(y + xs[h] * dsk_h)
        new_st.append(st[h] * jnp.exp(tot_h) + mm_tn(bs[g] * jnp.exp(tot_h - dcs_h), xdt))
    return tuple(ys) + tuple(new_st)


def ssd_fwd(x_hm, b_hm, c_hm, proj, dt_bias, a_log, d_skip, *, name):
    nh, T, P = x_hm.shape
    ng, N = b_hm.shape[0], b_hm.shape[2]
    L = SSM_CHUNK
    nc = T // L
    dtb = OFF_DT // LANES

    def body(x_ref, b_ref, c_ref, dt_ref, db_ref, al_ref, ds_ref, y_ref, s_ref, state):
        @pl.when(pl.program_id(0) == 0)
        def _():
            state[...] = jnp.zeros_like(state)

        s_ref[...] = state[...]
        args = ([x_ref[h] for h in range(nh)] + [b_ref[g] for g in range(ng)] + [c_ref[g] for g in range(ng)]
                + [dt_ref[...]] + [state[h] for h in range(nh)] + [db_ref[...], al_ref[...], ds_ref[...]])
        res = ssd_chunk_fn(*args)
        for h in range(nh):
            y_ref[:, h * P:(h + 1) * P] = res[h]
            state[h] = res[nh + h]

    par = pl.BlockSpec((1, LANES), lambda i: (0, 0))
    return pl.pallas_call(
        body, name=name, grid=(nc,),
        in_specs=[pl.BlockSpec((nh, L, P), lambda i: (0, i, 0)), pl.BlockSpec((ng, L, N), lambda i: (0, i, 0)),
                  pl.BlockSpec((ng, L, N), lambda i: (0, i, 0)), pl.BlockSpec((L, LANES), lambda i: (i, dtb)),
                  par, par, par],
        out_specs=[pl.BlockSpec((L, nh * P), lambda i: (i, 0)),
                   pl.BlockSpec((None, nh, N, P), lambda i: (i, 0, 0, 0))],
        out_shape=[jax.ShapeDtypeStruct((T, nh * P), F32), jax.ShapeDtypeStruct((nc, nh, N, P), F32)],
        scratch_shapes=[pltpu.VMEM((nh, N, P), F32)],
        compiler_params=_cparams("arbitrary"),
    )(x_hm, b_hm, c_hm, proj, dt_bias, a_log, d_skip)


def ssd_bwd(x_hm, b_hm, c_hm, proj, states, dt_bias, a_log, d_skip, dy, *, name):
    nh, T, P = x_hm.shape
    ng, N = b_hm.shape[0], b_hm.shape[2]
    L = SSM_CHUNK
    nc = T // L
    dtb = OFF_DT // LANES

    def body(x_ref, b_ref, c_ref, dt_ref, s_ref, db_ref, al_ref, ds_ref, dy_ref,
             dxbc_ref, ddt_ref, gdb_ref, gal_ref, gds_ref, dstate):
        @pl.when(pl.program_id(0) == 0)
        def _():
            dstate[...] = jnp.zeros_like(dstate)
            gdb_ref[...] = jnp.zeros_like(gdb_ref)
            gal_ref[...] = jnp.zeros_like(gal_ref)
            gds_ref[...] = jnp.zeros_like(gds_ref)

        args = ([x_ref[h] for h in range(nh)] + [b_ref[g] for g in range(ng)] + [c_ref[g] for g in range(ng)]
                + [dt_ref[...]] + [s_ref[h] for h in range(nh)] + [db_ref[...], al_ref[...], ds_ref[...]])
        _, vjp = jax.vjp(ssd_chunk_fn, *args)
        g = vjp(tuple([dy_ref[h] for h in range(nh)] + [dstate[h] for h in range(nh)]))
        for j in range(nh + 2 * ng):
            dxbc_ref[:, j * P:(j + 1) * P] = g[j]
        ddt_ref[...] = g[nh + 2 * ng]
        for h in range(nh):
            dstate[h] = g[nh + 2 * ng + 1 + h]
        gdb_ref[...] += g[-3]
        gal_ref[...] += g[-2]
        gds_ref[...] += g[-1]

    rev = lambda i: nc - 1 - i
    par = pl.BlockSpec((1, LANES), lambda i: (0, 0))
    return pl.pallas_call(
        body, name=name, grid=(nc,),
        in_specs=[pl.BlockSpec((nh, L, P), lambda i: (0, rev(i), 0)), pl.BlockSpec((ng, L, N), lambda i: (0, rev(i), 0)),
                  pl.BlockSpec((ng, L, N), lambda i: (0, rev(i), 0)), pl.BlockSpec((L, LANES), lambda i: (rev(i), dtb)),
                  pl.BlockSpec((None, nh, N, P), lambda i: (rev(i), 0, 0, 0)), par, par, par,
                  pl.BlockSpec((nh, L, P), lambda i: (0, rev(i), 0))],
        out_specs=[pl.BlockSpec((L, (nh + 2 * ng) * P), lambda i: (rev(i), 0)),
                   pl.BlockSpec((L, LANES), lambda i: (rev(i), 0)), par, par, par],
        out_shape=[jax.ShapeDtypeStruct((T, (nh + 2 * ng) * P), F32), jax.ShapeDtypeStruct((T, LANES), F32),
                   jax.ShapeDtypeStruct((1, LANES), F32), jax.ShapeDtypeStruct((1, LANES), F32),
                   jax.ShapeDtypeStruct((1, LANES), F32)],
        scratch_shapes=[pltpu.VMEM((nh, N, P), F32)],
        compiler_params=_cparams("arbitrary"),
    )(x_hm, b_hm, c_hm, proj, states, dt_bias, a_log, d_skip, dy)


def loss_head(h, target, g, *, name, tile=512):
    T, C = h.shape
    tl = min(T, tile)

    def body(h_ref, t_ref, g_ref, dh_ref, dg_ref, ls_ref):
        @pl.when(pl.program_id(0) == 0)
        def _():
            dg_ref[...] = jnp.zeros_like(dg_ref)
            ls_ref[...] = jnp.zeros_like(ls_ref)

        (y,), vjp = jax.vjp(rms_fn, h_ref[...], g_ref[...])
        err = y - t_ref[...]
        ls_ref[...] += jnp.sum(err * err, axis=0, keepdims=True) * (0.5 / C)
        dh, dg = vjp((err * (1.0 / C),))
        dh_ref[...] = dh
        dg_ref[...] += dg

    row = pl.BlockSpec((tl, C), lambda i: (i, 0))
    par = pl.BlockSpec((1, C), lambda i: (0, 0))
    return pl.pallas_call(
        body, name=name, grid=(T // tl,), in_specs=[row, row, par], out_specs=[row, par, par],
        out_shape=[jax.ShapeDtypeStruct((T, C), F32), jax.ShapeDtypeStruct((1, C), F32),
                   jax.ShapeDtypeStruct((1, C), F32)],
        compiler_params=_cparams("arbitrary"),
    )(h, target, g)


def adamw(w, g, m, v, *, name):
    R, C = w.shape
    tr = R
    for d in range(8, min(R, 512) + 1, 8):
        if R % d == 0:
            tr = d
    c1 = 1.0 - ADAM_B1 ** ADAM_STEP
    c2 = 1.0 - ADAM_B2 ** ADAM_STEP

    def body(w_ref, g_ref, m_ref, v_ref, d_ref, nm_ref, nv_ref):
        gv = g_ref[...]
        nm = ADAM_B1 * m_ref[...] + (1.0 - ADAM_B1) * gv
        nv = ADAM_B2 * v_ref[...] + (1.0 - ADAM_B2) * (gv * gv)
        d_ref[...] = -ADAM_LR * ((nm / c1) / (jnp.sqrt(nv / c2) + ADAM_EPS) + ADAM_WD * w_ref[...])
        nm_ref[...] = nm
        nv_ref[...] = nv

    spec = pl.BlockSpec((tr, C), lambda i: (i, 0))
    return pl.pallas_call(
        body, name=name, grid=(R // tr,), in_specs=[spec] * 4, out_specs=[spec] * 3,
        out_shape=[jax.ShapeDtypeStruct((R, C), F32)] * 3,
        compiler_params=_cparams("parallel"),
    )(w, g, m, v)


MESH = pl.DeviceIdType.MESH
HBM_SPEC = pl.BlockSpec(memory_space=pltpu.HBM)


def _place():
    return lax.axis_index("x"), lax.axis_index("y"), lax.axis_index("c")


def _direct_exchange(kind, src_ref, out_ref, send_sems, recv_sems, local_sem):
    x, y, c = _place()
    me = 4 * x + 2 * y + c

    def block_for(dest):
        return src_ref if kind == "gather" else src_ref.at[dest]

    own = pltpu.make_async_copy(block_for(me), out_ref.at[me], local_sem)
    sends, arrivals = [], []
    for f in range(1, 8):
        px = jnp.where((f >> 2) & 1, 1 - x, x)
        py = jnp.where((f >> 1) & 1, 1 - y, y)
        pc = jnp.where(f & 1, 1 - c, c)
        peer = 4 * px + 2 * py + pc
        for dst, group in ((out_ref.at[me], sends), (out_ref.at[peer], arrivals)):
            group.append(pltpu.make_async_remote_copy(
                src_ref=block_for(peer), dst_ref=dst, send_sem=send_sems.at[f - 1], recv_sem=recv_sems.at[f - 1],
                device_id=(px, py, pc), device_id_type=MESH))

    def start():
        own.start()
        for cp in sends:
            cp.start()

    def finish():
        for cp in arrivals:
            cp.wait_recv()
        for cp in sends:
            cp.wait_send()
        own.wait()

    return start, finish


def allgather_blocks(mine, *, name):
    R = mine.shape[0]

    def body(x_ref, out_ref, send_sems, recv_sems, local_sem):
        x, y, c = _place()
        me, sibling = (x, y, c), (x, y, 1 - c)
        chips = [(1 - x, y), (x, 1 - y), (1 - x, 1 - y)]

        def slot(px, py, pc):
            return out_ref.at[4 * px + 2 * py + pc]

        def copy(k, block, to, src=None):
            return pltpu.make_async_remote_copy(
                src_ref=slot(*block) if src is None else src, dst_ref=slot(*block),
                send_sem=send_sems.at[k], recv_sem=recv_sems.at[k], device_id=to, device_id_type=MESH)

        own = pltpu.make_async_copy(x_ref, slot(*me), local_sem)
        own.start()
        first = [copy(0, me, sibling, src=x_ref)]
        first += [copy(1 + j, me, (*chip, c), src=x_ref) for j, chip in enumerate(chips)]
        for cp in first:
            cp.start()
        passed = [copy(4 + j, (*chip, c), sibling) for j, chip in enumerate(chips)]
        for j, chip in enumerate(chips):
            copy(1 + j, (*chip, c), me).wait_recv()
            passed[j].start()
        copy(0, sibling, me).wait_recv()
        for j, chip in enumerate(chips):
            copy(4 + j, (*chip, 1 - c), me).wait_recv()
        for cp in first + passed:
            cp.wait_send()
        own.wait()

    return pl.pallas_call(
        body, name=name, out_shape=jax.ShapeDtypeStruct((8, R, LANES), mine.dtype),
        in_specs=[HBM_SPEC], out_specs=HBM_SPEC,
        scratch_shapes=[pltpu.SemaphoreType.DMA((7,)), pltpu.SemaphoreType.DMA((7,)), pltpu.SemaphoreType.DMA],
    )(mine)


def allgather_direct(mine, *, name):
    R = mine.shape[0]

    def body(x_ref, out_ref, send_sems, recv_sems, local_sem):
        start, finish = _direct_exchange("gather", x_ref, out_ref, send_sems, recv_sems, local_sem)
        start()
        finish()

    return pl.pallas_call(
        body, name=name, out_shape=jax.ShapeDtypeStruct((8, R, LANES), mine.dtype),
        in_specs=[HBM_SPEC], out_specs=HBM_SPEC,
        scratch_shapes=[pltpu.SemaphoreType.DMA((7,)), pltpu.SemaphoreType.DMA((7,)), pltpu.SemaphoreType.DMA],
    )(mine)


def send_to_sibling(v, *, name):
    def body(v_ref, out_ref, send_sem, recv_sem):
        x, y, c = _place()
        cp = pltpu.make_async_remote_copy(src_ref=v_ref, dst_ref=out_ref, send_sem=send_sem, recv_sem=recv_sem,
                                          device_id=(x, y, 1 - c), device_id_type=MESH)
        cp.start()
        cp.wait()

    return pl.pallas_call(
        body, name=name, out_shape=jax.ShapeDtypeStruct(v.shape, v.dtype), in_specs=[HBM_SPEC], out_specs=HBM_SPEC,
        scratch_shapes=[pltpu.SemaphoreType.DMA, pltpu.SemaphoreType.DMA],
    )(v)


def chip_exchange(p, *, name):
    R = p.shape[1]

    def body(p_ref, out_ref, send_sems, recv_sems):
        x, y, c = _place()
        chips = [(1 - x, y), (x, 1 - y), (1 - x, 1 - y)]
        sends = [pltpu.make_async_remote_copy(
            src_ref=p_ref.at[2 * px + py], dst_ref=out_ref.at[j], send_sem=send_sems.at[j], recv_sem=recv_sems.at[j],
            device_id=(px, py, c), device_id_type=MESH) for j, (px, py) in enumerate(chips)]
        for cp in sends:
            cp.start()
        for cp in sends:
            cp.wait()

    return pl.pallas_call(
        body, name=name, out_shape=jax.ShapeDtypeStruct((3, R, LANES), p.dtype), in_specs=[HBM_SPEC],
        out_specs=HBM_SPEC,
        scratch_shapes=[pltpu.SemaphoreType.DMA((3,)), pltpu.SemaphoreType.DMA((3,))],
    )(p)


def add_blocks(terms, out_dtype, *, name, tile=1024):
    terms = [t if isinstance(t, tuple) else (t, None) for t in terms]
    R = terms[0][0].shape[-2]
    tr = R
    for d in range(16, min(R, tile) + 1, 16):
        if R % d == 0:
            tr = d

    def body(*refs):
        acc = refs[0][...].astype(F32)
        for ref in refs[1:-1]:
            acc = acc + ref[...].astype(F32)
        refs[-1][...] = acc.astype(out_dtype)

    spec = pl.BlockSpec((tr, LANES), lambda i: (i, 0))
    in_specs = [spec if slot is None else pl.BlockSpec((None, tr, LANES), lambda i, slot=slot: (slot, i, 0))
                for _, slot in terms]
    return pl.pallas_call(
        body, name=name, grid=(R // tr,), in_specs=in_specs, out_specs=spec,
        out_shape=jax.ShapeDtypeStruct((R, LANES), out_dtype), compiler_params=_cparams("parallel"),
    )(*[a for a, _ in terms])


FLAT_ROW_STEP = 640


def _half_rows(arr, cc):
    hr = arr.shape[0] // 2
    return lax.dynamic_slice_in_dim(arr, cc * hr, hr, axis=0).reshape(-1)


def _flat_half(shards, cc, dtype):
    flat = jnp.concatenate([_half_rows(shards[n], cc).astype(dtype) for n in BIG])
    rows = -(-flat.shape[0] // (FLAT_ROW_STEP * LANES)) * FLAT_ROW_STEP
    return jnp.pad(flat, (0, rows * LANES - flat.shape[0])).reshape(rows, LANES)


def _flat_rows(shapes):
    n = sum((R // 2) * C for R, C in shapes.values()) // LANES
    return -(-n // FLAT_ROW_STEP) * FLAT_ROW_STEP


def _to_blocks(full, shapes, dtype):
    pieces = []
    for n in BIG:
        R, C = shapes[n]
        a = full[n].astype(dtype)
        if BIG_AXIS[n] == 2:
            a = a.reshape(2, R // 2, 4, C).transpose(2, 0, 1, 3)
        pieces.append(a.reshape(8, (R // 2) * C // LANES, LANES))
    flat = jnp.concatenate(pieces, axis=1)
    return jnp.pad(flat, ((0, 0), (0, _flat_rows(shapes) - flat.shape[1]), (0, 0)))


def _from_blocks(g8, shapes):
    out, off = {}, 0
    for n in BIG:
        R, C = shapes[n]
        rows = (R // 2) * C // LANES
        a = g8[:, off:off + rows, :].reshape(4, 2, R // 2, C)
        out[n] = a.transpose(1, 2, 0, 3).reshape(R, 4 * C) if BIG_AXIS[n] == 2 else a.reshape(4 * R, C)
        off += rows
    return out


def _unflat_halves(flat_by_c, shapes):
    out, off = {}, 0
    for n in BIG:
        R, C = shapes[n]
        sz = (R // 2) * C
        out[n] = jnp.concatenate([flat_by_c[c][off:off + sz].reshape(R // 2, C) for c in range(2)], axis=0)
        off += sz
    return out


def _to_heads(a, nh):
    T = a.shape[0]
    return a.reshape(T, nh, a.shape[1] // nh).transpose(1, 0, 2)


def _from_heads(a):
    nh, T, d = a.shape
    return a.transpose(1, 0, 2).reshape(T, nh * d)


def _to_heads_t(a, nh):
    T = a.shape[0]
    return a.reshape(T, nh, a.shape[1] // nh).transpose(1, 2, 0)


def _from_heads_t(a):
    nh, d, T = a.shape
    return a.transpose(2, 0, 1).reshape(T, nh * d)


def _pad_cols(a, n):
    return jnp.pad(a, ((0, 0), (0, n - a.shape[1])))


def _pack_w_in(w):
    offs = [sum(IN_SPLITS[:i]) for i in range(len(IN_SPLITS) + 1)]
    sb, z, xbc, dt, cq, ckv, kr = [w[:, offs[i]:offs[i + 1]] for i in range(len(IN_SPLITS))]
    zeros = lambda n: jnp.zeros((w.shape[0], n), w.dtype)
    h = MLA_ROPE // 2
    kra = jnp.concatenate([zeros(MLA_NOPE), kr, zeros(LANES - MLA_QK)], axis=1)
    krb = jnp.concatenate([zeros(MLA_NOPE), -kr[:, h:], kr[:, :h], zeros(LANES - MLA_QK)], axis=1)
    return sb, jnp.concatenate([z, xbc, cq, ckv, _pad_cols(dt, LANES), kra, krb], axis=1)


def _unpack_gw_in(g_sb, g):
    h = MLA_ROPE // 2
    ga, gb = g[:, OFF_KRA:OFF_KRA + LANES], g[:, OFF_KRB:OFF_KRB + LANES]
    gkr = ga[:, MLA_NOPE:MLA_QK] + jnp.concatenate([gb[:, MLA_NOPE + h:MLA_QK], -gb[:, MLA_NOPE:MLA_NOPE + h]], axis=1)
    return jnp.concatenate([g_sb, g[:, OFF_Z:OFF_Z + 512], g[:, OFF_XBC:OFF_XBC + 768],
                            g[:, OFF_DT:OFF_DT + 8], g[:, OFF_CQ:OFF_CQ + 256], g[:, OFF_CKV:OFF_CKV + 128], gkr], axis=1)


def _pack_w_uq(w):
    zeros = lambda n: jnp.zeros((w.shape[0], n), w.dtype)
    h = MLA_ROPE // 2
    pp, rr = [], []
    for i in range(MLA_HEADS):
        nope = w[:, MLA_QK * i:MLA_QK * i + MLA_NOPE]
        rope = w[:, MLA_QK * i + MLA_NOPE:MLA_QK * (i + 1)]
        pp += [nope, rope, zeros(LANES - MLA_QK)]
        rr += [zeros(MLA_NOPE), -rope[:, h:], rope[:, :h], zeros(LANES - MLA_QK)]
    return jnp.concatenate(pp, axis=1), jnp.concatenate(rr, axis=1)


def _unpack_gw_uq(gp, gr):
    h = MLA_ROPE // 2
    out = []
    for i in range(MLA_HEADS):
        b = LANES * i
        out.append(gp[:, b:b + MLA_NOPE])
        out.append(gp[:, b + MLA_NOPE:b + MLA_NOPE + h] + gr[:, b + MLA_NOPE + h:b + MLA_QK])
        out.append(gp[:, b + MLA_NOPE + h:b + MLA_QK] - gr[:, b + MLA_NOPE:b + MLA_NOPE + h])
    return jnp.concatenate(out, axis=1)


def _pack_w_ukv(w):
    zeros = lambda n: jnp.zeros((w.shape[0], n), w.dtype)
    kk, vv = [], []
    for i in range(MLA_HEADS):
        b = (MLA_NOPE + MLA_V) * i
        kk += [w[:, b:b + MLA_NOPE], zeros(LANES - MLA_NOPE)]
        vv += [w[:, b + MLA_NOPE:b + MLA_NOPE + MLA_V], zeros(LANES - MLA_V)]
    return jnp.concatenate(kk, axis=1), jnp.concatenate(vv, axis=1)


def _unpack_gw_ukv(gk, gv):
    out = []
    for i in range(MLA_HEADS):
        out += [gk[:, LANES * i:LANES * i + MLA_NOPE], gv[:, LANES * i:LANES * i + MLA_V]]
    return jnp.concatenate(out, axis=1)


def _rope_tables(positions):
    inv_freq = 1.0 / (ROPE_THETA ** (jnp.arange(0, MLA_ROPE, 2, dtype=F32) / MLA_ROPE))
    ang = positions.astype(F32)[:, None] * inv_freq
    cos, sin = jnp.cos(ang), jnp.sin(ang)
    T = positions.shape[0]
    one, zero = jnp.ones((T, MLA_NOPE), F32), jnp.zeros((T, MLA_NOPE), F32)
    pad1, pad0 = jnp.ones((T, LANES - MLA_QK), F32), jnp.zeros((T, LANES - MLA_QK), F32)
    return jnp.concatenate([one, cos, cos, pad1], axis=1), jnp.concatenate([zero, sin, sin, pad0], axis=1)


def _row(v):
    return v.reshape(1, -1)


def _pad_row(v):
    return _pad_cols(v.reshape(1, -1), LANES)


def _layer_weights(full, small, li):
    p = {}
    p["w_sb"], p["w_rest"] = _pack_w_in(full["w_in"])
    q_scale = jnp.concatenate([jnp.full((1, SB_HEADS * SB_DIM), SB_DIM ** -0.5, BF16),
                               jnp.ones((1, 2 * SB_HEADS * SB_DIM), BF16)], axis=1)
    p["w_sb_fwd"] = p["w_sb"] * q_scale
    p["wqp"], p["wqr"] = _pack_w_uq(full["mla_w_uq"])
    p["wkp"], p["wvp"] = _pack_w_ukv(full["mla_w_ukv"])
    p["w_out"] = full["w_out"]
    p["w_up"] = full["ffn_w_up"]
    p["w_down"] = full["ffn_w_down"]
    for n in ("mix_norm", "sb_out_norm", "ssm_conv_b", "ssm_out_norm", "mla_q_norm", "mla_kv_norm", "mla_out_norm",
              "ffn_norm", "ffn_conv_b"):
        p[n] = _row(small[n][li])
    for n in ("ssm_dt_bias", "ssm_a_log", "ssm_d"):
        p[n] = _pad_row(small[n][li])
    p["ssm_conv_w"] = small["ssm_conv_w"][li]
    p["ffn_conv_w"] = small["ffn_conv_w"][li]
    return p


def _layer_fwd(h, p, cos, sin, li, exchange=None):
    T = h.shape[0]
    nm = lambda s: "l%d_%s" % (li, s)
    s = {"h": h}
    (n1,) = rowwise(rms_fn, [h], [p["mix_norm"]], [(D_MODEL, BF16)], name=nm("mix_norm"))
    proj = matmul(n1, p["w_rest"], name=nm("in_proj"))
    qkv = matmul(n1, p["w_sb_fwd"], name=nm("in_proj_sb"), out_dtype=BF16)
    s["n1"], s["proj"] = n1, proj
    s["sb_q"] = _to_heads_t(qkv[:, 0:256], SB_HEADS)
    s["sb_k"] = _to_heads(qkv[:, 256:512], SB_HEADS)
    s["sb_v"] = _to_heads(qkv[:, 512:768], SB_HEADS)
    y_sb_hm, s["sb_bt"], s["sb_first"] = sb_fwd(s["sb_q"], s["sb_k"], s["sb_v"], name=nm("sb_fwd"))
    s["y_sb"] = _from_heads_t(y_sb_hm)
    s["x_hm"], s["b_hm"], s["c_hm"] = ssm_conv_act(proj, p["ssm_conv_w"], p["ssm_conv_b"], name=nm("ssm_conv"))
    s["y_ssm"], s["states"] = ssd_fwd(s["x_hm"], s["b_hm"], s["c_hm"], proj, p["ssm_dt_bias"], p["ssm_a_log"],
                                      p["ssm_d"], name=nm("ssd_fwd"))
    rows = [(proj, 256, OFF_CQ // 256), (proj, 128, OFF_CKV // 128), (proj, 128, OFF_KRA // 128),
            (proj, 128, OFF_KRB // 128), cos, sin]
    qp, kp, vv = rowwise(mla_prep_fn, rows, [p["mla_q_norm"], p["mla_kv_norm"], p["wqp"], p["wqr"], p["wkp"], p["wvp"]],
                         [(512, BF16), (512, BF16), (512, BF16)], name=nm("mla_prep"))
    s["mla_q"], s["mla_k"], s["mla_v"] = _to_heads_t(qp, MLA_HEADS), kp, vv
    s["mla_o"], s["mla_lse"], *rode = mla_fwd(s["mla_q"], kp, vv, name=nm("mla_fwd"), exchange=exchange)
    s["y_mla"] = _from_heads_t(s["mla_o"][:, :MLA_V, :])
    (cat,) = rowwise(merge_fn, [s["y_sb"], s["y_ssm"], (proj, 512, OFF_Z // 512), s["y_mla"]],
                     [p["sb_out_norm"], p["ssm_out_norm"], p["mla_out_norm"]], [(D_MODEL, BF16)], name=nm("merge"),
                     post=lambda a, b, c: (jnp.concatenate([a, b, c], axis=1),))
    s["cat"] = cat
    h1 = matmul(cat, p["w_out"], name=nm("out_proj"), residual=h)
    s["h1"] = h1
    (n2,) = rowwise(rms_fn, [h1], [p["ffn_norm"]], [(D_MODEL, BF16)], name=nm("ffn_norm"))
    up = matmul(n2, p["w_up"], name=nm("ffn_up"))
    act = ffn_act(up, p["ffn_conv_w"], p["ffn_conv_b"], name=nm("ffn_act"))
    s["n2"], s["up"], s["act"] = n2, up, act
    h2 = matmul(act, p["w_down"], name=nm("ffn_down"), residual=h1)
    return h2, s, (rode[0] if rode else None)


def _layer_bwd(dh2, s, p, cos, sin, li, exchange=None):
    nm = lambda t: "l%d_%s" % (li, t)
    g = {}
    proj = s["proj"]
    g["ffn_w_down"] = matmul(s["act"], dh2, name=nm("g_w_down"), ta=True)
    d_act = matmul(dh2, p["w_down"], name=nm("d_act"), out_dtype=BF16, tb=True)
    d_up_g, d_up_v, gwg, gwv, gbg, gbv = ffn_bwd_fused(s["up"], p["ffn_conv_w"], p["ffn_conv_b"], d_act,
                                                       name=nm("ffn_act_bwd"))
    g["ffn_conv_w"] = jnp.concatenate([gwg, gwv], axis=1)
    g["ffn_conv_b"] = jnp.concatenate([gbg[0], gbv[0]])
    g["ffn_w_up"] = jnp.concatenate([matmul(s["n2"], d_up_g, name=nm("g_w_up_gate"), ta=True),
                                     matmul(s["n2"], d_up_v, name=nm("g_w_up_val"), ta=True)], axis=1)
    d_n2 = matmul(d_up_g, p["w_up"], name=nm("d_n2_gate"), tb=True)
    d_n2 = matmul(d_up_v, p["w_up"], name=nm("d_n2_val"), tb=True, b_k0=D_FF, residual=d_n2)
    (dh1,), (gn,) = rowwise_bwd(rms_fn, [s["h1"]], [], [p["ffn_norm"]], [d_n2], [F32], name=nm("ffn_norm_bwd"),
                                add0=dh2)
    g["ffn_norm"] = gn[0]
    g["w_out"] = matmul(s["cat"], dh1, name=nm("g_w_out"), ta=True)
    d_cat = matmul(dh1, p["w_out"], name=nm("d_cat"), tb=True)
    (d_ysb, d_yssm, d_z, d_ymla), (g1, g2, g3) = rowwise_bwd(
        merge_fn, [s["y_sb"], s["y_ssm"], (proj, 512, OFF_Z // 512), s["y_mla"]], [],
        [p["sb_out_norm"], p["ssm_out_norm"], p["mla_out_norm"]], [d_cat], [F32, F32, BF16, F32], name=nm("merge_bwd"),
        pre_ct=lambda d: (d[:, 0:256], d[:, 256:768], d[:, 768:1024]))
    g["sb_out_norm"], g["ssm_out_norm"], g["mla_out_norm"] = g1[0], g2[0], g3[0]
    dq, dk, dv = sb_bwd(s["sb_q"], s["sb_k"], s["sb_v"], _to_heads_t(d_ysb, SB_HEADS), s["sb_bt"], s["sb_first"], name=nm("sb_bwd"),
                        q_scale=SB_DIM ** -0.5)
    d_sb = jnp.concatenate([_from_heads_t(dq), _from_heads(dk), _from_heads(dv)], axis=1).astype(BF16)
    do_t = jnp.pad(_to_heads_t(d_ymla, MLA_HEADS), ((0, 0), (0, LANES - MLA_V), (0, 0)))
    dqp, dkp, dvv, *rode = mla_bwd(s["mla_q"], s["mla_k"], s["mla_v"], do_t, s["mla_o"], s["mla_lse"],
                                   name=nm("mla_bwd"), exchange=exchange)
    rows = [(proj, 256, OFF_CQ // 256), (proj, 128, OFF_CKV // 128), (proj, 128, OFF_KRA // 128),
            (proj, 128, OFF_KRB // 128)]
    (d_cq, d_ckv, d_kra, d_krb), (gqn, gkvn, gwqp, gwqr, gwkp, gwvp) = rowwise_bwd(
        mla_prep_fn, rows, [cos, sin], [p["mla_q_norm"], p["mla_kv_norm"], p["wqp"], p["wqr"], p["wkp"], p["wvp"]],
        [_from_heads_t(dqp), dkp, dvv], [BF16] * 4, name=nm("mla_prep_bwd"), tile=256)
    g["mla_q_norm"], g["mla_kv_norm"] = gqn[0], gkvn[0]
    g["mla_w_uq"] = _unpack_gw_uq(gwqp, gwqr)
    g["mla_w_ukv"] = _unpack_gw_ukv(gwkp, gwvp)
    d_xbc_act, d_dt, gdb, gal, gds = ssd_bwd(
        s["x_hm"], s["b_hm"], s["c_hm"], proj, s["states"], p["ssm_dt_bias"], p["ssm_a_log"], p["ssm_d"],
        _to_heads(d_yssm, SSM_HEADS), name=nm("ssd_bwd"))
    g["ssm_dt_bias"], g["ssm_a_log"], g["ssm_d"] = gdb[0, :8], gal[0, :8], gds[0, :8]
    d_pre = ssm_conv_bwd_a(proj, p["ssm_conv_w"], p["ssm_conv_b"], d_xbc_act, name=nm("ssm_conv_bwd_a"))
    d_xbc, g["ssm_conv_w"], gscb = conv_bwd_b(d_pre, proj, OFF_XBC, p["ssm_conv_w"], name=nm("ssm_conv_bwd_b"),
                                              out_dtype=BF16, tc=256)
    g["ssm_conv_b"] = gscb[0]
    d_proj = jnp.concatenate([d_z, d_xbc, d_cq, d_ckv, d_dt.astype(BF16), d_kra, d_krb], axis=1)
    g["w_in"] = _unpack_gw_in(matmul(s["n1"], d_sb, name=nm("g_w_in_sb"), ta=True),
                              matmul(s["n1"], d_proj, name=nm("g_w_in"), ta=True))
    d_n1 = matmul(d_sb, p["w_sb"], name=nm("d_n1_sb"), tb=True)
    d_n1 = matmul(d_proj, p["w_rest"], name=nm("d_n1"), tb=True, residual=d_n1)
    (dh0,), (gm,) = rowwise_bwd(rms_fn, [s["h"]], [], [p["mix_norm"]], [d_n1], [F32], name=nm("mix_norm_bwd"),
                                add0=dh1)
    g["mix_norm"] = gm[0]
    return dh0, g, (rode[0] if rode else None)


def kernel(x, positions, mix_norm, w_in, sb_out_norm, ssm_conv_w, ssm_conv_b, ssm_dt_bias, ssm_a_log, ssm_d, ssm_out_norm, mla_q_norm, mla_w_uq, mla_kv_norm, mla_w_ukv, mla_out_norm, w_out, ffn_norm, ffn_w_up, ffn_conv_w, ffn_conv_b, ffn_w_down, final_norm, loss_target, m_mix_norm, m_w_in, m_sb_out_norm, m_ssm_conv_w, m_ssm_conv_b, m_ssm_dt_bias, m_ssm_a_log, m_ssm_d, m_ssm_out_norm, m_mla_q_norm, m_mla_w_uq, m_mla_kv_norm, m_mla_w_ukv, m_mla_out_norm, m_w_out, m_ffn_norm, m_ffn_w_up, m_ffn_conv_w, m_ffn_conv_b, m_ffn_w_down, m_final_norm, v_mix_norm, v_w_in, v_sb_out_norm, v_ssm_conv_w, v_ssm_conv_b, v_ssm_dt_bias, v_ssm_a_log, v_ssm_d, v_ssm_out_norm, v_mla_q_norm, v_mla_w_uq, v_mla_kv_norm, v_mla_w_ukv, v_mla_out_norm, v_w_out, v_ffn_norm, v_ffn_w_up, v_ffn_conv_w, v_ffn_conv_b, v_ffn_w_down, v_final_norm):
    W = dict(mix_norm=mix_norm, w_in=w_in, sb_out_norm=sb_out_norm, ssm_conv_w=ssm_conv_w, ssm_conv_b=ssm_conv_b,
             ssm_dt_bias=ssm_dt_bias, ssm_a_log=ssm_a_log, ssm_d=ssm_d, ssm_out_norm=ssm_out_norm,
             mla_q_norm=mla_q_norm, mla_w_uq=mla_w_uq, mla_kv_norm=mla_kv_norm, mla_w_ukv=mla_w_ukv,
             mla_out_norm=mla_out_norm, w_out=w_out, ffn_norm=ffn_norm, ffn_w_up=ffn_w_up, ffn_conv_w=ffn_conv_w,
             ffn_conv_b=ffn_conv_b, ffn_w_down=ffn_w_down, final_norm=final_norm)
    M = dict(mix_norm=m_mix_norm, w_in=m_w_in, sb_out_norm=m_sb_out_norm, ssm_conv_w=m_ssm_conv_w,
             ssm_conv_b=m_ssm_conv_b, ssm_dt_bias=m_ssm_dt_bias, ssm_a_log=m_ssm_a_log, ssm_d=m_ssm_d,
             ssm_out_norm=m_ssm_out_norm, mla_q_norm=m_mla_q_norm, mla_w_uq=m_mla_w_uq, mla_kv_norm=m_mla_kv_norm,
             mla_w_ukv=m_mla_w_ukv, mla_out_norm=m_mla_out_norm, w_out=m_w_out, ffn_norm=m_ffn_norm,
             ffn_w_up=m_ffn_w_up, ffn_conv_w=m_ffn_conv_w, ffn_conv_b=m_ffn_conv_b, ffn_w_down=m_ffn_w_down,
             final_norm=m_final_norm)
    V = dict(mix_norm=v_mix_norm, w_in=v_w_in, sb_out_norm=v_sb_out_norm, ssm_conv_w=v_ssm_conv_w,
             ssm_conv_b=v_ssm_conv_b, ssm_dt_bias=v_ssm_dt_bias, ssm_a_log=v_ssm_a_log, ssm_d=v_ssm_d,
             ssm_out_norm=v_ssm_out_norm, mla_q_norm=v_mla_q_norm, mla_w_uq=v_mla_w_uq, mla_kv_norm=v_mla_kv_norm,
             mla_w_ukv=v_mla_w_ukv, mla_out_norm=v_mla_out_norm, w_out=v_w_out, ffn_norm=v_ffn_norm,
             ffn_w_up=v_ffn_w_up, ffn_conv_w=v_ffn_conv_w, ffn_conv_b=v_ffn_conv_b, ffn_w_down=v_ffn_w_down,
             final_norm=v_final_norm)
    depth = mix_norm.shape[0]
    cx, cy, cc = _place()
    chip = 2 * cx + cy
    T = x.shape[1]

    assert depth == 2
    shard_shapes = {n: W[n].shape[1:] for n in BIG}

    def layer_of(d, li):
        return {n: d[n][li] for n in BIG}

    def assemble(g8):
        return _from_blocks(g8, shard_shapes)

    full0 = assemble(allgather_blocks(_flat_half(layer_of(W, 0), cc, BF16), name="gather_weights_l0"))
    conv_full = {}
    small = {n: W[n] for n in SMALL_REPL}
    cw_flat = jnp.concatenate([W[n].reshape(-1) for n in SMALL_SHARD])
    cw_rows = -(-cw_flat.shape[0] // (8 * LANES)) * 8
    cw_all = allgather_direct(jnp.pad(cw_flat, (0, cw_rows * LANES - cw_flat.shape[0])).reshape(cw_rows, LANES),
                              name="gather_conv_taps")
    off = 0
    for n in SMALL_SHARD:
        sz = W[n].size
        conv_full[n] = jnp.concatenate(
            [cw_all[2 * k].reshape(-1)[off:off + sz].reshape(W[n].shape) for k in range(4)], axis=2)
        off += sz
    small.update(conv_full)

    cos, sin = _rope_tables(positions[0])
    params0 = _layer_weights(full0, small, 0)
    h, s0, g8 = _layer_fwd(x[0], params0, cos, sin, 0, exchange=("gather", _flat_half(layer_of(W, 1), cc, BF16)))
    params1 = _layer_weights(assemble(g8), small, 1)
    h, s1, _ = _layer_fwd(h, params1, cos, sin, 1)
    dh, g_final, loss_lanes = loss_head(h, loss_target[0], _row(final_norm), name="loss_head")

    dh, g1, _ = _layer_bwd(dh, s1, params1, cos, sin, 1)
    by_dest = _to_blocks(g1, shard_shapes, BF16)
    dh, g0, from_all = _layer_bwd(dh, s0, params0, cos, sin, 0, exchange=("all_to_all", by_dest))
    grad_x = dh[None]
    grads = [g0, g1]
    G = {n: jnp.stack([grads[li][n] for li in range(depth)]) for n in WEIGHTS if n != "final_norm" and n not in BIG}
    G["final_norm"] = g_final[0]
    half1 = add_blocks([(from_all, d) for d in range(8)], F32, name="grads_l1_sum")

    blocks0 = _to_blocks(g0, shard_shapes, BF16)
    R = blocks0.shape[1]
    blocks0 = blocks0.reshape(4, 2, R, LANES)
    mine_first = lax.dynamic_index_in_dim(blocks0, cc, 1, keepdims=False)
    for_sibling = lax.dynamic_index_in_dim(blocks0, 1 - cc, 1, keepdims=False)
    from_sibling = send_to_sibling(for_sibling.reshape(4 * R, LANES), name="grads_to_sibling")
    pair = add_blocks([mine_first.reshape(4 * R, LANES), from_sibling], BF16, name="grads_pair_sum").reshape(4, R, LANES)
    others = chip_exchange(pair, name="grads_chip_exchange")
    own = lax.dynamic_index_in_dim(pair, chip, 0, keepdims=False)
    half0 = add_blocks([own, (others, 0), (others, 1), (others, 2)], F32, name="grads_chip_sum")
    half = jnp.concatenate([half0, half1])
    other = send_to_sibling(half, name="grads_pair_swap")
    by_core = [jnp.where(cc == 0, half, other), jnp.where(cc == 0, other, half)]
    g_big_l = [_unflat_halves([a[li * R:(li + 1) * R].reshape(-1) for a in by_core], shard_shapes) for li in range(depth)]
    g_big = {n: jnp.stack([g_big_l[li][n] for li in range(depth)]) for n in BIG}

    small_list = [G[n].reshape(-1) for n in SMALL_REPL] + [G[n].reshape(-1) for n in SMALL_SHARD]
    small_list.append(jnp.sum(loss_lanes).reshape(1))
    sm = jnp.concatenate(small_list)
    n_small = sm.shape[0]
    sm_rows = -(-n_small // (16 * LANES)) * 16
    sm_all = allgather_direct(jnp.pad(sm, (0, sm_rows * LANES - n_small)).reshape(sm_rows, LANES), name="gather_small")
    sm_sum = add_blocks([(sm_all, d) for d in range(8)], F32, name="small_sum").reshape(-1)
    g_small, off = {}, 0
    for n in SMALL_REPL:
        g_small[n] = sm_sum[off:off + W[n].size].reshape(W[n].shape)
        off += W[n].size
    for n in SMALL_SHARD:
        full_shape = conv_full[n].shape
        sz = conv_full[n].size
        gfull = sm_sum[off:off + sz].reshape(full_shape)
        width = W[n].shape[2]
        g_small[n] = lax.dynamic_slice_in_dim(gfull, chip * width, width, axis=2)
        off += sz
    loss = sm_sum[off]

    grad_out, delta, new_m, new_v = {}, {}, {}, {}
    for n in BIG:
        shp = W[n].shape
        two_d = lambda a: a.reshape(shp[0] * shp[1], shp[2])
        d, nm_, nv_ = adamw(two_d(W[n]), two_d(g_big[n]), two_d(M[n]), two_d(V[n]), name="adamw_" + n)
        grad_out[n], delta[n], new_m[n], new_v[n] = g_big[n], d.reshape(shp), nm_.reshape(shp), nv_.reshape(shp)
    small_names = SMALL_REPL + SMALL_SHARD

    def flat_small(d):
        f = jnp.concatenate([d[n].reshape(-1) for n in small_names])
        rows = -(-f.shape[0] // (8 * LANES)) * 8
        return jnp.pad(f, (0, rows * LANES - f.shape[0])).reshape(rows, LANES)

    vpad = flat_small(V)
    d, nm_, nv_ = adamw(flat_small(W), flat_small(g_small), flat_small(M), vpad, name="adamw_small")
    off = 0
    for n in small_names:
        sz = W[n].size
        grad_out[n] = g_small[n]
        delta[n] = d.reshape(-1)[off:off + sz].reshape(W[n].shape)
        new_m[n] = nm_.reshape(-1)[off:off + sz].reshape(W[n].shape)
        new_v[n] = nv_.reshape(-1)[off:off + sz].reshape(W[n].shape)
        off += sz

    return (loss, grad_x, *[grad_out[n] for n in WEIGHTS], *[delta[n] for n in WEIGHTS],
            *[new_m[n] for n in WEIGHTS], *[new_v[n] for n in WEIGHTS])
```

```python
import functools
import math

import jax
import jax.numpy as jnp
from jax import lax
from jax.experimental import pallas as pl
from jax.experimental.pallas import tpu as pltpu

F32 = jnp.float32
BF16 = jnp.bfloat16

EPS = 1e-6
D_MODEL = 1024
SB_HEADS, SB_DIM = 4, 64
SSM_HEADS, SSM_DIM, SSM_GROUPS, SSM_STATE, SSM_CHUNK = 8, 64, 2, 64, 128
SSM_INNER = SSM_HEADS * SSM_DIM
SSM_CONV_DIM = SSM_INNER + 2 * SSM_GROUPS * SSM_STATE
MLA_HEADS, MLA_NOPE, MLA_ROPE, MLA_V = 4, 64, 32, 64
MLA_QK = MLA_NOPE + MLA_ROPE
MLA_SCALE = MLA_QK ** -0.5
ROPE_THETA = 10000.0
D_FF = 2816
IN_SPLITS = (768, 512, 768, 8, 256, 128, 32)

OFF_Z, OFF_XBC, OFF_CQ, OFF_CKV, OFF_DT, OFF_KRA, OFF_KRB = 0, 512, 1280, 1536, 1664, 1792, 1920
D_REST = 2048
LANES = 128

ADAM_LR, ADAM_B1, ADAM_B2, ADAM_EPS, ADAM_WD, ADAM_STEP = 0.001, 0.9, 0.999, 1e-08, 0.01, 10

V7X_VMEM_LIMIT = 48 * 1024 * 1024

NT = (((1,), (1,)), ((), ()))
TN = (((0,), (0,)), ((), ()))

BIG = ("w_in", "mla_w_uq", "mla_w_ukv", "w_out", "ffn_w_up", "ffn_w_down")
BIG_AXIS = {"w_in": 2, "mla_w_uq": 2, "mla_w_ukv": 2, "w_out": 1, "ffn_w_up": 2, "ffn_w_down": 1}
SMALL_REPL = ("mix_norm", "sb_out_norm", "ssm_conv_b", "ssm_dt_bias", "ssm_a_log", "ssm_d", "ssm_out_norm",
              "mla_q_norm", "mla_kv_norm", "mla_out_norm", "ffn_norm", "ffn_conv_b", "final_norm")
SMALL_SHARD = ("ssm_conv_w", "ffn_conv_w")
WEIGHTS = ("mix_norm", "w_in", "sb_out_norm", "ssm_conv_w", "ssm_conv_b", "ssm_dt_bias", "ssm_a_log", "ssm_d",
           "ssm_out_norm", "mla_q_norm", "mla_w_uq", "mla_kv_norm", "mla_w_ukv", "mla_out_norm", "w_out", "ffn_norm",
           "ffn_w_up", "ffn_conv_w", "ffn_conv_b", "ffn_w_down", "final_norm")


def _cparams(*sem):
    return pltpu.CompilerParams(dimension_semantics=sem if sem else None, vmem_limit_bytes=V7X_VMEM_LIMIT)


def _pick(n, target, mult=LANES):
    best = None
    for d in range(mult, min(n, target) + 1, mult):
        if n % d == 0:
            best = d
    return best or n


def _sigmoid(x):
    return 1.0 / (1.0 + jnp.exp(-x))


def _softplus(x):
    ax = jnp.where(x > 0, x, -x)
    return jnp.where(x > 0, x, 0.0) + jnp.log(1.0 + jnp.exp(-ax))


def _rms(x, g):
    return x * lax.rsqrt(jnp.mean(x * x, axis=-1, keepdims=True) + EPS) * g


def _raw_nn(a, b):
    return jnp.dot(a.astype(BF16), b.astype(BF16), preferred_element_type=F32)


def _raw_nt(a, b):
    return lax.dot_general(a.astype(BF16), b.astype(BF16), NT, preferred_element_type=F32)


def _raw_tn(a, b):
    return lax.dot_general(a.astype(BF16), b.astype(BF16), TN, preferred_element_type=F32)


@jax.custom_vjp
def mm_nn(a, b):
    return _raw_nn(a, b)


mm_nn.defvjp(lambda a, b: (_raw_nn(a, b), (a, b)),
             lambda r, ct: (_raw_nt(ct, r[1]), _raw_tn(r[0], ct)))


@jax.custom_vjp
def mm_nt(a, b):
    return _raw_nt(a, b)


mm_nt.defvjp(lambda a, b: (_raw_nt(a, b), (a, b)),
             lambda r, ct: (_raw_nn(ct, r[1]), _raw_tn(ct, r[0])))


@jax.custom_vjp
def mm_tn(a, b):
    return _raw_tn(a, b)


mm_tn.defvjp(lambda a, b: (_raw_tn(a, b), (a, b)),
             lambda r, ct: (_raw_nt(r[1], ct), _raw_nn(r[0], ct)))


def _tri_dot(tri, x, terms=3):
    parts = []
    r = x
    for t in range(terms):
        xt = r.astype(BF16)
        parts.append(xt)
        if t + 1 < terms:
            r = r - xt.astype(F32)
    return jnp.dot(jnp.concatenate([tri] * terms, axis=1), jnp.concatenate(parts, axis=0),
                   preferred_element_type=F32)


def _tri(n, cmp):
    r = lax.broadcasted_iota(jnp.int32, (n, n), 0)
    c = lax.broadcasted_iota(jnp.int32, (n, n), 1)
    return cmp(r, c).astype(BF16)


@jax.custom_vjp
def csum_rows(x):
    return _tri_dot(_tri(x.shape[0], lambda r, c: r >= c), x)


csum_rows.defvjp(lambda x: (csum_rows(x), None),
                 lambda _, ct: (_tri_dot(_tri(ct.shape[0], lambda r, c: r <= c), ct),))


def matmul(a, b, *, name, out_dtype=F32, ta=False, tb=False, b_k0=0, residual=None):
    if ta:
        K, M = a.shape
    else:
        M, K = a.shape
    N = b.shape[0] if tb else b.shape[1]
    tm = _pick(M, 1408)
    tn = _pick(N, 1408)
    tk = _pick(K, 1408)
    nk = K // tk
    kb0 = b_k0 // tk
    assert b_k0 % tk == 0 and (tb or b_k0 == 0)
    has_res = residual is not None

    def body(*refs):
        if has_res:
            a_ref, b_ref, r_ref, o_ref, acc = refs
        else:
            a_ref, b_ref, o_ref, acc = refs
        k = pl.program_id(2)

        @pl.when(k == 0)
        def _():
            acc[...] = jnp.zeros_like(acc)

        av = a_ref[...].astype(BF16)
        bv = b_ref[...].astype(BF16)
        if ta:
            acc[...] += lax.dot_general(av, bv, TN, preferred_element_type=F32)
        elif tb:
            acc[...] += lax.dot_general(av, bv, NT, preferred_element_type=F32)
        else:
            acc[...] += jnp.dot(av, bv, preferred_element_type=F32)

        @pl.when(k == nk - 1)
        def _():
            r = acc[...]
            if has_res:
                r = r + r_ref[...].astype(F32)
            o_ref[...] = r.astype(o_ref.dtype)

    a_spec = pl.BlockSpec((tk, tm), lambda i, j, k: (k, i)) if ta else pl.BlockSpec((tm, tk), lambda i, j, k: (i, k))
    b_spec = pl.BlockSpec((tn, tk), lambda i, j, k: (j, kb0 + k)) if tb else pl.BlockSpec((tk, tn), lambda i, j, k: (k, j))
    in_specs = [a_spec, b_spec]
    args = [a, b]
    if has_res:
        in_specs.append(pl.BlockSpec((tm, tn), lambda i, j, k: (i, j)))
        args.append(residual)
    return pl.pallas_call(
        body, name=name, grid=(M // tm, N // tn, nk),
        in_specs=in_specs, out_specs=pl.BlockSpec((tm, tn), lambda i, j, k: (i, j)),
        out_shape=jax.ShapeDtypeStruct((M, N), out_dtype),
        scratch_shapes=[pltpu.VMEM((tm, tn), F32)],
        compiler_params=_cparams("parallel", "parallel", "arbitrary"),
    )(*args)


def _row_spec(entry, tl):
    if isinstance(entry, tuple):
        arr, width, cb = entry
        return arr, pl.BlockSpec((tl, width), lambda i, cb=cb: (i, cb))
    return entry, pl.BlockSpec((tl, entry.shape[1]), lambda i: (i, 0))


def _rows_T(entry):
    return (entry[0] if isinstance(entry, tuple) else entry).shape[0]


def rowwise(fn, rows, params, outs, *, name, tile=512, post=None):
    T = _rows_T(rows[0])
    tl = min(T, tile)
    nr, npar = len(rows), len(params)

    def body(*refs):
        r = [ref[...].astype(F32) for ref in refs[:nr]]
        p = [ref[...].astype(F32) for ref in refs[nr:nr + npar]]
        res = fn(*r, *p)
        if post is not None:
            res = post(*res)
        for o_ref, val in zip(refs[nr + npar:], res):
            o_ref[...] = val.astype(o_ref.dtype)

    arrs, specs = [], []
    for e in rows:
        a, s = _row_spec(e, tl)
        arrs.append(a)
        specs.append(s)
    for p in params:
        arrs.append(p)
        specs.append(pl.BlockSpec(p.shape, lambda i: (0, 0)))
    res = pl.pallas_call(
        body, name=name, grid=(T // tl,), in_specs=specs,
        out_specs=[pl.BlockSpec((tl, c), lambda i: (i, 0)) for c, _ in outs],
        out_shape=[jax.ShapeDtypeStruct((T, c), dt) for c, dt in outs],
        compiler_params=_cparams("parallel"),
    )(*arrs)
    return res


def rowwise_bwd(fn, rows, nd_rows, params, cts, grad_dtypes, *, name, tile=512, pre_ct=None, add0=None):
    T = _rows_T(rows[0])
    tl = min(T, tile)
    nr, nn, npar, nc = len(rows), len(nd_rows), len(params), len(cts)
    has_add = add0 is not None

    def body(*refs):
        pos = 0
        r = [ref[...].astype(F32) for ref in refs[pos:pos + nr]]
        pos += nr
        nd = [ref[...].astype(F32) for ref in refs[pos:pos + nn]]
        pos += nn
        p = [ref[...].astype(F32) for ref in refs[pos:pos + npar]]
        pos += npar
        c = [ref[...].astype(F32) for ref in refs[pos:pos + nc]]
        pos += nc
        if has_add:
            addv = refs[pos][...].astype(F32)
            pos += 1
        rg_refs = refs[pos:pos + nr]
        pg_refs = refs[pos + nr:pos + nr + npar]
        if pre_ct is not None:
            c = list(pre_ct(*c))
        _, vjp = jax.vjp(lambda *a: fn(*a[:nr], *nd, *a[nr:]), *r, *p)
        g = vjp(tuple(c))
        for j, ref in enumerate(rg_refs):
            val = g[j]
            if has_add and j == 0:
                val = val + addv
            ref[...] = val.astype(ref.dtype)
        if npar:
            @pl.when(pl.program_id(0) == 0)
            def _():
                for ref in pg_refs:
                    ref[...] = jnp.zeros_like(ref)
            for j, ref in enumerate(pg_refs):
                ref[...] += g[nr + j]

    arrs, specs = [], []
    widths = []
    for e in list(rows) + list(nd_rows):
        a, s = _row_spec(e, tl)
        arrs.append(a)
        specs.append(s)
        widths.append(s.block_shape[1])
    for p in params:
        arrs.append(p)
        specs.append(pl.BlockSpec(p.shape, lambda i: (0, 0)))
    for e in cts:
        a, s = _row_spec(e, tl)
        arrs.append(a)
        specs.append(s)
    if has_add:
        a, s = _row_spec(add0, tl)
        arrs.append(a)
        specs.append(s)
    out_specs = [pl.BlockSpec((tl, widths[j]), lambda i: (i, 0)) for j in range(nr)]
    out_shape = [jax.ShapeDtypeStruct((T, widths[j]), grad_dtypes[j]) for j in range(nr)]
    out_specs += [pl.BlockSpec(p.shape, lambda i: (0, 0)) for p in params]
    out_shape += [jax.ShapeDtypeStruct(p.shape, F32) for p in params]
    res = pl.pallas_call(
        body, name=name, grid=(T // tl,), in_specs=specs, out_specs=out_specs, out_shape=out_shape,
        compiler_params=_cparams("arbitrary"),
    )(*arrs)
    return list(res[:nr]), list(res[nr:])


def rms_fn(h, g):
    return (_rms(h, g),)


def merge_fn(ysb, yssm, z, ymla, g_sb, g_ssm, g_mla):
    ya = _rms(ysb, g_sb)
    yb = _rms(yssm * (z * _sigmoid(z)), g_ssm)
    yc = _rms(ymla, g_mla)
    return ya, yb, yc


def mla_prep_fn(cq, ckv, kra, krb, cos, sin, qn, kvn, wqp, wqr, wkp, wvp):
    cos4 = jnp.concatenate([cos] * MLA_HEADS, axis=1)
    sin4 = jnp.concatenate([sin] * MLA_HEADS, axis=1)
    nq = _rms(cq, qn)
    q = (mm_nn(nq, wqp) * cos4 + mm_nn(nq, wqr) * sin4) * MLA_SCALE
    nkv = _rms(ckv, kvn)
    kpe = kra * cos + krb * sin
    k = mm_nn(nkv, wkp) + jnp.concatenate([kpe] * MLA_HEADS, axis=1)
    v = mm_nn(nkv, wvp)
    return q, k, v


HALO = 8


def _prev_halo_spec(tl, tc, col_of):
    return pl.BlockSpec((HALO, tc), lambda i, j: (jnp.maximum(i * (tl // HALO) - 1, 0), col_of(j)))


def _fill_prev(buf, x_ref, halo_ref, i):
    buf[0:HALO, :] = jnp.where(i > 0, halo_ref[...].astype(F32), 0.0)
    buf[HALO:, :] = x_ref[...].astype(F32)


def _conv_from(buf, w_ref, b_ref, K, tl):
    acc = b_ref[...].astype(F32) + jnp.zeros((tl, buf.shape[1]), F32)
    for k in range(K):
        acc = acc + buf[pl.ds(HALO - (K - 1 - k), tl), :] * w_ref[k:k + 1, :].astype(F32)
    return acc


def ssm_conv_act(proj, w, b, *, name, tile=512, tc=256):
    T = proj.shape[0]
    K, C = w.shape
    tl = min(T, tile)
    c0 = OFF_XBC // tc
    nb = C // tc
    hd = SSM_DIM
    per = tc // hd

    def body(*refs):
        xs, halos = refs[0:nb], refs[nb:2 * nb]
        w_ref, b_ref = refs[2 * nb:2 * nb + 2]
        x_out, b_out, c_out = refs[2 * nb + 2:2 * nb + 5]
        bufs = refs[2 * nb + 5:]
        for j in range(nb):
            cols = slice(j * tc, (j + 1) * tc)
            _fill_prev(bufs[j], xs[j], halos[j], pl.program_id(0))
            u = b_ref[:, cols].astype(F32) + jnp.zeros((tl, tc), F32)
            for k in range(K):
                u = u + bufs[j][pl.ds(HALO - (K - 1 - k), tl), :] * w_ref[k:k + 1, cols].astype(F32)
            act = u * _sigmoid(u)
            for hh in range(per):
                piece = act[:, hh * hd:(hh + 1) * hd]
                head = j * per + hh
                if head < SSM_HEADS:
                    x_out[head] = piece
                elif head < SSM_HEADS + SSM_GROUPS:
                    b_out[head - SSM_HEADS] = piece
                else:
                    c_out[head - SSM_HEADS - SSM_GROUPS] = piece

    in_specs = ([pl.BlockSpec((tl, tc), lambda i, j=j: (i, c0 + j)) for j in range(nb)]
                + [pl.BlockSpec((HALO, tc), lambda i, j=j: (jnp.maximum(i * (tl // HALO) - 1, 0), c0 + j)) for j in range(nb)]
                + [pl.BlockSpec((K, C), lambda i: (0, 0)), pl.BlockSpec((1, C), lambda i: (0, 0))])
    return pl.pallas_call(
        body, name=name, grid=(T // tl,), in_specs=in_specs,
        out_specs=[pl.BlockSpec((SSM_HEADS, tl, hd), lambda i: (0, i, 0)),
                   pl.BlockSpec((SSM_GROUPS, tl, hd), lambda i: (0, i, 0)),
                   pl.BlockSpec((SSM_GROUPS, tl, hd), lambda i: (0, i, 0))],
        out_shape=[jax.ShapeDtypeStruct((SSM_HEADS, T, hd), F32), jax.ShapeDtypeStruct((SSM_GROUPS, T, hd), F32),
                   jax.ShapeDtypeStruct((SSM_GROUPS, T, hd), F32)],
        scratch_shapes=[pltpu.VMEM((tl + HALO, tc), F32)] * nb,
        compiler_params=_cparams("parallel"),
    )(*([proj] * (2 * nb)), w, b)


def ssm_conv_bwd_a(proj, w, b, d_out, *, name, tile=512, tc=256):
    T = proj.shape[0]
    K, C = w.shape
    tl = min(T, tile)
    c0 = OFF_XBC // tc

    def body(x_ref, halo_ref, w_ref, b_ref, d_ref, o_ref, buf):
        _fill_prev(buf, x_ref, halo_ref, pl.program_id(0))
        u = _conv_from(buf, w_ref, b_ref, K, tl)
        s = _sigmoid(u)
        o_ref[...] = d_ref[...].astype(F32) * (s * (1.0 + u * (1.0 - s)))

    return pl.pallas_call(
        body, name=name, grid=(T // tl, C // tc),
        in_specs=[pl.BlockSpec((tl, tc), lambda i, j: (i, c0 + j)), _prev_halo_spec(tl, tc, lambda j: c0 + j),
                  pl.BlockSpec((K, tc), lambda i, j: (0, j)), pl.BlockSpec((1, tc), lambda i, j: (0, j)),
                  pl.BlockSpec((tl, tc), lambda i, j: (i, j))],
        out_specs=pl.BlockSpec((tl, tc), lambda i, j: (i, j)),
        out_shape=jax.ShapeDtypeStruct((T, C), F32),
        scratch_shapes=[pltpu.VMEM((tl + HALO, tc), F32)],
        compiler_params=_cparams("parallel", "parallel"),
    )(proj, proj, w, b, d_out)


def ffn_act(up, w, b, *, name, tile=512, tc=1408):
    T = up.shape[0]
    K = w.shape[0]
    tl = min(T, tile)
    nj = D_FF // tc

    def body(xg_ref, hg_ref, xv_ref, hv_ref, wg_ref, wv_ref, bg_ref, bv_ref, o_ref, bufg, bufv):
        i = pl.program_id(0)
        _fill_prev(bufg, xg_ref, hg_ref, i)
        _fill_prev(bufv, xv_ref, hv_ref, i)
        gate = _conv_from(bufg, wg_ref, bg_ref, K, tl)
        val = _conv_from(bufv, wv_ref, bv_ref, K, tl)
        o_ref[...] = (gate * _sigmoid(gate) * val).astype(o_ref.dtype)

    return pl.pallas_call(
        body, name=name, grid=(T // tl, nj),
        in_specs=[pl.BlockSpec((tl, tc), lambda i, j: (i, j)), _prev_halo_spec(tl, tc, lambda j: j),
                  pl.BlockSpec((tl, tc), lambda i, j: (i, nj + j)), _prev_halo_spec(tl, tc, lambda j: nj + j),
                  pl.BlockSpec((K, tc), lambda i, j: (0, j)), pl.BlockSpec((K, tc), lambda i, j: (0, nj + j)),
                  pl.BlockSpec((1, tc), lambda i, j: (0, j)), pl.BlockSpec((1, tc), lambda i, j: (0, nj + j))],
        out_specs=pl.BlockSpec((tl, tc), lambda i, j: (i, j)),
        out_shape=jax.ShapeDtypeStruct((T, D_FF), BF16),
        scratch_shapes=[pltpu.VMEM((tl + HALO, tc), F32), pltpu.VMEM((tl + HALO, tc), F32)],
        compiler_params=_cparams("parallel", "parallel"),
    )(up, up, up, up, w, w, b, b)


def ffn_bwd_fused(up, w, b, d_act, *, name, tile=1024, tc=256):
    T = up.shape[0]
    K = w.shape[0]
    tl = min(T, tile)
    nj = D_FF // tc
    nblk = T // HALO
    ext = tl + HALO

    def body(xg, hgp, hgn, xv, hvp, hvn, wg, wv, bg, bv, d, dn, og, ov, dwg, dwv, dbg, dbv, bufg, bufv, dgb, dvb):
        i = pl.program_id(1)
        last = pl.num_programs(1) - 1

        def fill(buf, x_ref, prev_ref, next_ref):
            buf[0:HALO, :] = jnp.where(i > 0, prev_ref[...].astype(F32), 0.0)
            buf[HALO:HALO + tl, :] = x_ref[...].astype(F32)
            buf[HALO + tl:, :] = jnp.where(i < last, next_ref[...].astype(F32), 0.0)

        def conv_ext(buf, w_ref, b_ref):
            acc = b_ref[...].astype(F32) + jnp.zeros((ext, tc), F32)
            for k in range(K):
                acc = acc + buf[pl.ds(HALO - (K - 1 - k), ext), :] * w_ref[k:k + 1, :].astype(F32)
            return acc

        fill(bufg, xg, hgp, hgn)
        fill(bufv, xv, hvp, hvn)
        gate = conv_ext(bufg, wg, bg)
        val = conv_ext(bufv, wv, bv)
        dd = jnp.concatenate([d[...].astype(F32), jnp.where(i < last, dn[...].astype(F32)[0:HALO], 0.0)], axis=0)
        s = _sigmoid(gate)
        dgb[...] = dd * val * (s * (1.0 + gate * (1.0 - s)))
        dvb[...] = dd * (gate * s)

        @pl.when(i == 0)
        def _():
            for ref in (dwg, dwv, dbg, dbv):
                ref[...] = jnp.zeros_like(ref)

        for dbuf, xbuf, w_ref, o_ref, dw_ref, db_ref in ((dgb, bufg, wg, og, dwg, dbg), (dvb, bufv, wv, ov, dwv, dbv)):
            cur = dbuf[0:tl, :]
            dx = jnp.zeros((tl, tc), F32)
            for k in range(K):
                sft = K - 1 - k
                dx = dx + dbuf[pl.ds(sft, tl), :] * w_ref[k:k + 1, :].astype(F32)
                dw_ref[k:k + 1, :] += jnp.sum(cur * xbuf[pl.ds(HALO - sft, tl), :], axis=0, keepdims=True)
            db_ref[...] += jnp.sum(cur, axis=0, keepdims=True)
            o_ref[...] = dx.astype(o_ref.dtype)

    prev = lambda i: jnp.maximum(i * (tl // HALO) - 1, 0)
    nxt = lambda i: jnp.minimum((i + 1) * (tl // HALO), nblk - 1)

    def x_specs(col):
        return [pl.BlockSpec((tl, tc), lambda j, i: (i, col(j))), pl.BlockSpec((HALO, tc), lambda j, i: (prev(i), col(j))),
                pl.BlockSpec((HALO, tc), lambda j, i: (nxt(i), col(j)))]

    gcol, vcol = (lambda j: j), (lambda j: nj + j)
    in_specs = (x_specs(gcol) + x_specs(vcol)
                + [pl.BlockSpec((K, tc), lambda j, i: (0, j)), pl.BlockSpec((K, tc), lambda j, i: (0, nj + j)),
                   pl.BlockSpec((1, tc), lambda j, i: (0, j)), pl.BlockSpec((1, tc), lambda j, i: (0, nj + j)),
                   pl.BlockSpec((tl, tc), lambda j, i: (i, j)),
                   pl.BlockSpec((2 * HALO, tc), lambda j, i: (jnp.minimum((i + 1) * (tl // (2 * HALO)), nblk // 2 - 1), j))])
    row_out = pl.BlockSpec((tl, tc), lambda j, i: (i, j))
    w_out = pl.BlockSpec((K, tc), lambda j, i: (0, j))
    b_out = pl.BlockSpec((1, tc), lambda j, i: (0, j))
    return pl.pallas_call(
        body, name=name, grid=(nj, T // tl), in_specs=in_specs,
        out_specs=[row_out, row_out, w_out, w_out, b_out, b_out],
        out_shape=[jax.ShapeDtypeStruct((T, D_FF), BF16)] * 2 + [jax.ShapeDtypeStruct((K, D_FF), F32)] * 2
        + [jax.ShapeDtypeStruct((1, D_FF), F32)] * 2,
        scratch_shapes=[pltpu.VMEM((tl + 2 * HALO, tc), F32)] * 2 + [pltpu.VMEM((ext, tc), F32)] * 2,
        compiler_params=_cparams("parallel", "arbitrary"),
    )(up, up, up, up, up, up, w, w, b, b, d_act, d_act)


def conv_bwd_b(du, x, x_off, w, *, name, out_dtype, tile=512, tc=256):
    T, C = du.shape
    K = w.shape[0]
    tl = min(T, tile)
    c0 = x_off // tc
    nblk = T // HALO

    def body(du_ref, nx_ref, x_ref, w_ref, dx_ref, dw_ref, db_ref, dbuf):
        i = pl.program_id(1)
        last = pl.num_programs(1) - 1
        d = du_ref[...].astype(F32)
        dbuf[0:tl, :] = d
        dbuf[tl:, :] = jnp.where(i < last, nx_ref[...].astype(F32), 0.0)

        @pl.when(i == 0)
        def _():
            dw_ref[...] = jnp.zeros_like(dw_ref)
            db_ref[...] = jnp.zeros_like(db_ref)

        xin = x_ref[...].astype(F32)
        dx = jnp.zeros((tl, tc), F32)
        for k in range(K):
            s = K - 1 - k
            shifted = dbuf[pl.ds(s, tl), :]
            dx = dx + shifted * w_ref[k:k + 1, :].astype(F32)
            dw_ref[k:k + 1, :] += jnp.sum(shifted * xin, axis=0, keepdims=True)
        db_ref[...] += jnp.sum(d, axis=0, keepdims=True)
        dx_ref[...] = dx.astype(dx_ref.dtype)

    return pl.pallas_call(
        body, name=name, grid=(C // tc, T // tl),
        in_specs=[pl.BlockSpec((tl, tc), lambda j, i: (i, j)),
                  pl.BlockSpec((HALO, tc), lambda j, i: (jnp.minimum((i + 1) * (tl // HALO), nblk - 1), j)),
                  pl.BlockSpec((tl, tc), lambda j, i: (i, c0 + j)),
                  pl.BlockSpec((K, tc), lambda j, i: (0, j))],
        out_specs=[pl.BlockSpec((tl, tc), lambda j, i: (i, j)), pl.BlockSpec((K, tc), lambda j, i: (0, j)),
                   pl.BlockSpec((1, tc), lambda j, i: (0, j))],
        out_shape=[jax.ShapeDtypeStruct((T, C), out_dtype), jax.ShapeDtypeStruct((K, C), F32),
                   jax.ShapeDtypeStruct((1, C), F32)],
        scratch_shapes=[pltpu.VMEM((tl + HALO, tc), F32)],
        compiler_params=_cparams("parallel", "arbitrary"),
    )(du, du, x, w)


SB_QUERIES = 1024


def _attn_tiles(T, keys=256, queries=1024):
    return min(T, queries), min(T, keys)


def _after_diag(keys, queries, strict):
    d = lax.broadcasted_iota(jnp.int32, (keys, queries), 1) - lax.broadcasted_iota(jnp.int32, (keys, queries), 0)
    return d > 0 if strict else d >= 0


def _log_gates(z):
    l1p = jnp.log(1.0 + jnp.exp(-jnp.abs(z)))
    a = jnp.minimum(z, 0.0) - l1p
    return a, a - z


def _causal_sweep(i, tq, tk, block, descending, keep_going=None, first_block=None):
    nb = tq // tk
    n_full = i * nb

    def band():
        order = reversed(range(nb)) if descending else range(nb)
        for bb in order:
            block(pl.multiple_of(i * tq + bb * tk, tk), bb * tk, True)

    def full():
        if descending and keep_going is not None:
            def step(j):
                block(pl.multiple_of((n_full - 1 - j) * tk, tk), 0, False)
                return j + 1
            done = lax.while_loop(lambda j: jnp.logical_and(j < n_full, keep_going()), step, jnp.int32(0))
            return n_full - done

        def step(j, c):
            kb = (n_full - 1 - j) if descending else j
            block(pl.multiple_of(kb * tk, tk), 0, False)
            return c
        lax.fori_loop(0 if first_block is None else first_block, n_full, step, 0)
        return None

    if descending:
        band()
        return full()
    full()
    band()
    return None


def sb_fwd(q, k, v, *, name):
    H, dh, T = q.shape
    tq, tk = _attn_tiles(T, queries=SB_QUERIES)

    def body(q_ref, k_ref, v_ref, y_ref, bt_ref, first_ref, acc, run):
        acc[...] = jnp.zeros_like(acc)
        run[...] = jnp.zeros_like(run)
        u_after = _tri(tk, lambda r, c: r < c)

        def block(k0, r0, masked):
            kb = k_ref[pl.ds(k0, tk), :]
            vb = v_ref[pl.ds(k0, tk), :]
            z = jnp.dot(kb, q_ref[:, r0:], preferred_element_type=F32)
            a, b = _log_gates(z)
            if masked:
                valid = _after_diag(tk, tq - r0, True)
                b = jnp.where(valid, b, 0.0)
            w = jnp.exp(a + _tri_dot(u_after, b, 2) + run[:, r0:])
            if masked:
                w = jnp.where(valid, w, 0.0)
            acc[:, r0:] += lax.dot_general(vb, w.astype(BF16), TN, preferred_element_type=F32)
            run[:, r0:] += jnp.sum(b, axis=0, keepdims=True)

        first = _causal_sweep(pl.program_id(1), tq, tk, block, descending=True,
                              keep_going=lambda: jnp.max(run[...]) >= SB_ZERO_BELOW)
        y_ref[...] = acc[...]
        bt_ref[...] = run[...]
        first_ref[...] = jnp.zeros(first_ref.shape, F32) + first.astype(F32)

    return pl.pallas_call(
        body, name=name, grid=(H, T // tq),
        in_specs=[pl.BlockSpec((None, dh, tq), lambda h, i: (h, 0, i)),
                  pl.BlockSpec((None, T, dh), lambda h, i: (h, 0, 0)),
                  pl.BlockSpec((None, T, dh), lambda h, i: (h, 0, 0))],
        out_specs=[pl.BlockSpec((None, dh, tq), lambda h, i: (h, 0, i)),
                   pl.BlockSpec((None, 1, tq), lambda h, i: (h, 0, i)),
                   pl.BlockSpec((None, None, HALO, LANES), lambda h, i: (h, i, 0, 0))],
        out_shape=[jax.ShapeDtypeStruct((H, dh, T), F32), jax.ShapeDtypeStruct((H, 1, T), F32),
                   jax.ShapeDtypeStruct((H, T // tq, HALO, LANES), F32)],
        scratch_shapes=[pltpu.VMEM((dh, tq), F32), pltpu.VMEM((1, tq), F32)],
        compiler_params=_cparams("parallel", "parallel"),
    )(q, k, v)


def sb_bwd(q, k, v, dy, btot, first, *, name, q_scale):
    H, dh, T = q.shape
    tq, tk = _attn_tiles(T, queries=SB_QUERIES)

    def body(q_ref, k_ref, v_ref, dy_ref, bt_ref, first_ref, dq_ref, dk_ref, dv_ref, dq, pb, pg, dyb):
        @pl.when(pl.program_id(1) == 0)
        def _():
            dk_ref[...] = jnp.zeros_like(dk_ref)
            dv_ref[...] = jnp.zeros_like(dv_ref)

        dq[...] = jnp.zeros_like(dq)
        pb[...] = jnp.zeros_like(pb)
        pg[...] = jnp.zeros_like(pg)
        dyb[...] = dy_ref[...].astype(BF16)
        u_upto = _tri(tk, lambda r, c: r >= c)
        u_before = _tri(tk, lambda r, c: r > c)

        def block(k0, r0, masked):
            kb = k_ref[pl.ds(k0, tk), :]
            vb = v_ref[pl.ds(k0, tk), :]
            qv = q_ref[:, r0:]
            dyv = dyb[:, r0:]
            z = jnp.dot(kb, qv, preferred_element_type=F32)
            a, b = _log_gates(z)
            if masked:
                valid = _after_diag(tk, tq - r0, True)
                b = jnp.where(valid, b, 0.0)
            w = jnp.exp(a + (bt_ref[:, r0:] - pb[:, r0:] - _tri_dot(u_upto, b, 2)))
            if masked:
                w = jnp.where(valid, w, 0.0)
            g = w * jnp.dot(vb, dyv, preferred_element_type=F32)
            dz = g - jnp.exp(a) * (g + pg[:, r0:] + _tri_dot(u_before, g, 2))
            if masked:
                dz = jnp.where(valid, dz, 0.0)
            dz = dz.astype(BF16)
            dq[:, r0:] += lax.dot_general(kb, dz, TN, preferred_element_type=F32)
            dk_ref[pl.ds(k0, tk), :] += lax.dot_general(dz, qv, NT, preferred_element_type=F32)
            dv_ref[pl.ds(k0, tk), :] += lax.dot_general(w.astype(BF16), dyv, NT, preferred_element_type=F32)
            pb[:, r0:] += jnp.sum(b, axis=0, keepdims=True)
            pg[:, r0:] += jnp.sum(g, axis=0, keepdims=True)

        i = pl.program_id(1)
        first = jnp.clip(jnp.max(first_ref[...]).astype(jnp.int32), 0, i * (tq // tk))
        _causal_sweep(i, tq, tk, block, descending=False, first_block=first)
        dq_ref[...] = dq[...] * q_scale

    return pl.pallas_call(
        body, name=name, grid=(H, T // tq),
        in_specs=[pl.BlockSpec((None, dh, tq), lambda h, i: (h, 0, i)),
                  pl.BlockSpec((None, T, dh), lambda h, i: (h, 0, 0)),
                  pl.BlockSpec((None, T, dh), lambda h, i: (h, 0, 0)),
                  pl.BlockSpec((None, dh, tq), lambda h, i: (h, 0, i)),
                  pl.BlockSpec((None, 1, tq), lambda h, i: (h, 0, i)),
                  pl.BlockSpec((None, None, HALO, LANES), lambda h, i: (h, i, 0, 0))],
        out_specs=[pl.BlockSpec((None, dh, tq), lambda h, i: (h, 0, i)),
                   pl.BlockSpec((None, T, dh), lambda h, i: (h, 0, 0)),
                   pl.BlockSpec((None, T, dh), lambda h, i: (h, 0, 0))],
        out_shape=[jax.ShapeDtypeStruct((H, dh, T), F32), jax.ShapeDtypeStruct((H, T, dh), F32),
                   jax.ShapeDtypeStruct((H, T, dh), F32)],
        scratch_shapes=[pltpu.VMEM((dh, tq), F32), pltpu.VMEM((1, tq), F32), pltpu.VMEM((1, tq), F32),
                        pltpu.VMEM((dh, tq), BF16)],
        compiler_params=_cparams("parallel", "arbitrary"),
    )(q, k, v, dy, btot, first)


NEG = -1e30
SB_ZERO_BELOW = -105.0
MLA_KEYS = 512


def _call_with_exchange(body, exchange, *, name, grid, in_specs, out_specs, out_shape, scratch_shapes, args):
    if exchange is None:
        return pl.pallas_call(body, name=name, grid=grid, in_specs=in_specs, out_specs=out_specs, out_shape=out_shape,
                              scratch_shapes=scratch_shapes, compiler_params=_cparams("parallel", "arbitrary"))(*args)
    kind, src = exchange
    n_in, n_out, n_scr = len(in_specs), len(out_specs), len(scratch_shapes)
    R = src.shape[-2]

    def wrapped(*refs):
        ins, src_ref = refs[:n_in], refs[n_in]
        outs, xout = refs[n_in + 1:n_in + 1 + n_out], refs[n_in + 1 + n_out]
        scr = refs[n_in + 2 + n_out:n_in + 2 + n_out + n_scr]
        start, finish = _direct_exchange(kind, src_ref, xout, *refs[-3:])
        step = pl.program_id(0) * pl.num_programs(1) + pl.program_id(1)
        pl.when(step == 0)(start)
        body(*ins, *outs, *scr)
        pl.when(step == pl.num_programs(0) * pl.num_programs(1) - 1)(finish)

    return pl.pallas_call(
        wrapped, name=name, grid=grid, in_specs=list(in_specs) + [HBM_SPEC], out_specs=list(out_specs) + [HBM_SPEC],
        out_shape=list(out_shape) + [jax.ShapeDtypeStruct((8, R, LANES), src.dtype)],
        scratch_shapes=list(scratch_shapes) + [pltpu.SemaphoreType.DMA((7,)), pltpu.SemaphoreType.DMA((7,)),
                                               pltpu.SemaphoreType.DMA],
        compiler_params=_cparams("arbitrary", "arbitrary"))(*args, src)


def mla_fwd(q, k, v, *, name, exchange=None):
    H, dk, T = q.shape
    dv = v.shape[1] // H
    tq, tk = _attn_tiles(T, MLA_KEYS)

    def body(q_ref, k_ref, v_ref, o_ref, l_ref, acc, m_s, l_s):
        acc[...] = jnp.zeros_like(acc)
        m_s[...] = jnp.full_like(m_s, NEG)
        l_s[...] = jnp.zeros_like(l_s)

        def block(k0, r0, masked):
            kb = k_ref[pl.ds(k0, tk), :]
            vb = v_ref[pl.ds(k0, tk), :]
            s = jnp.dot(kb, q_ref[:, r0:], preferred_element_type=F32)
            if masked:
                s = jnp.where(_after_diag(tk, tq - r0, False), s, NEG)
            m = m_s[:, r0:]
            m_new = jnp.maximum(m, jnp.max(s, axis=0, keepdims=True))
            p = jnp.exp(s - m_new)
            alpha = jnp.exp(m - m_new)
            l_s[:, r0:] = alpha * l_s[:, r0:] + jnp.sum(p, axis=0, keepdims=True)
            acc[:, r0:] = alpha * acc[:, r0:] + lax.dot_general(vb, p.astype(BF16), TN, preferred_element_type=F32)
            m_s[:, r0:] = m_new

        _causal_sweep(pl.program_id(1), tq, tk, block, descending=False)
        o_ref[...] = acc[...] / l_s[...]
        l_ref[...] = m_s[...] + jnp.log(l_s[...])

    return _call_with_exchange(
        body, exchange, name=name, grid=(H, T // tq),
        in_specs=[pl.BlockSpec((None, dk, tq), lambda h, i: (h, 0, i)),
                  pl.BlockSpec((T, dk), lambda h, i: (0, h)),
                  pl.BlockSpec((T, dv), lambda h, i: (0, h))],
        out_specs=[pl.BlockSpec((None, dv, tq), lambda h, i: (h, 0, i)),
                   pl.BlockSpec((None, 1, tq), lambda h, i: (h, 0, i))],
        out_shape=[jax.ShapeDtypeStruct((H, dv, T), F32), jax.ShapeDtypeStruct((H, 1, T), F32)],
        scratch_shapes=[pltpu.VMEM((dv, tq), F32), pltpu.VMEM((1, tq), F32), pltpu.VMEM((1, tq), F32)],
        args=(q, k, v))


def mla_bwd(q, k, v, do, o, lse, *, name, exchange=None):
    H, dk, T = q.shape
    dv = v.shape[1] // H
    tq, tk = _attn_tiles(T, MLA_KEYS)

    def body(q_ref, k_ref, v_ref, do_ref, o_ref, l_ref, dq_ref, dk_ref, dv_ref, dq, delta, dob):
        @pl.when(pl.program_id(1) == 0)
        def _():
            dk_ref[...] = jnp.zeros_like(dk_ref)
            dv_ref[...] = jnp.zeros_like(dv_ref)

        dq[...] = jnp.zeros_like(dq)
        dov = do_ref[...].astype(F32)
        dob[...] = dov.astype(BF16)
        delta[...] = jnp.sum(dov * o_ref[...], axis=0, keepdims=True)

        def block(k0, r0, masked):
            kb = k_ref[pl.ds(k0, tk), :]
            vb = v_ref[pl.ds(k0, tk), :]
            qv = q_ref[:, r0:]
            dov_b = dob[:, r0:]
            s = jnp.dot(kb, qv, preferred_element_type=F32)
            p = jnp.exp(s - l_ref[:, r0:])
            if masked:
                p = jnp.where(_after_diag(tk, tq - r0, False), p, 0.0)
            dp = jnp.dot(vb, dov_b, preferred_element_type=F32)
            ds = (p * (dp - delta[:, r0:])).astype(BF16)
            dq[:, r0:] += lax.dot_general(kb, ds, TN, preferred_element_type=F32)
            dk_ref[pl.ds(k0, tk), :] += lax.dot_general(ds, qv, NT, preferred_element_type=F32)
            dv_ref[pl.ds(k0, tk), :] += lax.dot_general(p.astype(BF16), dov_b, NT, preferred_element_type=F32)

        _causal_sweep(pl.program_id(1), tq, tk, block, descending=False)
        dq_ref[...] = dq[...]

    return _call_with_exchange(
        body, exchange, name=name, grid=(H, T // tq),
        in_specs=[pl.BlockSpec((None, dk, tq), lambda h, i: (h, 0, i)),
                  pl.BlockSpec((T, dk), lambda h, i: (0, h)),
                  pl.BlockSpec((T, dv), lambda h, i: (0, h)),
                  pl.BlockSpec((None, dv, tq), lambda h, i: (h, 0, i)),
                  pl.BlockSpec((None, dv, tq), lambda h, i: (h, 0, i)),
                  pl.BlockSpec((None, 1, tq), lambda h, i: (h, 0, i))],
        out_specs=[pl.BlockSpec((None, dk, tq), lambda h, i: (h, 0, i)),
                   pl.BlockSpec((T, dk), lambda h, i: (0, h)),
                   pl.BlockSpec((T, dv), lambda h, i: (0, h))],
        out_shape=[jax.ShapeDtypeStruct((H, dk, T), F32), jax.ShapeDtypeStruct((T, H * dk), F32),
                   jax.ShapeDtypeStruct((T, H * dv), F32)],
        scratch_shapes=[pltpu.VMEM((dk, tq), F32), pltpu.VMEM((1, tq), F32), pltpu.VMEM((dv, tq), BF16)],
        args=(q, k, v, do, o, lse))


def _lane_pick(x, h):
    lane = lax.broadcasted_iota(jnp.int32, (1, x.shape[1]), 1)
    return jnp.sum(jnp.where(lane == h, x, 0.0), axis=1, keepdims=True)


def _row_pick(x, h):
    sub = lax.broadcasted_iota(jnp.int32, (x.shape[0], 1), 0)
    return jnp.sum(jnp.where(sub == h, x, 0.0), axis=0, keepdims=True)


def ssd_chunk_fn(*args):
    nh, ng = SSM_HEADS, SSM_GROUPS
    xs = args[:nh]
    bs = args[nh:nh + ng]
    cs = args[nh + ng:nh + 2 * ng]
    dt_raw = args[nh + 2 * ng]
    st = args[nh + 2 * ng + 1:nh + 2 * ng + 1 + nh]
    dt_bias, a_log, d_skip = args[nh + 2 * ng + 1 + nh:]
    L = dt_raw.shape[0]
    dt = _softplus(dt_raw + dt_bias)
    da = dt * (-jnp.exp(a_log))
    dcs = csum_rows(da)
    dcs_t = dcs.T
    total = jnp.sum(da, axis=0, keepdims=True)
    causal = lax.broadcasted_iota(jnp.int32, (L, L), 0) >= lax.broadcasted_iota(jnp.int32, (L, L), 1)
    cb = [mm_nt(cs[g], bs[g]) for g in range(ng)]
    ys, new_st = [], []
    for h in range(nh):
        g = h // (nh // ng)
        dcs_h = _lane_pick(dcs, h)
        dt_h = _lane_pick(dt, h)
        tot_h = _lane_pick(total, h)
        dsk_h = _lane_pick(d_skip, h)
        decay = jnp.exp(jnp.where(causal, dcs_h - _row_pick(dcs_t, h), NEG))
        xdt = xs[h] * dt_h
        y = mm_nn(cb[g] * decay, xdt)
        y = y + mm_nn(cs[g] * jnp.exp(dcs_h), st[h])
        ys.append(y + xs[h] * dsk_h)
        new_st.append(st[h] * jnp.exp(tot_h) + mm_tn(bs[g] * jnp.exp(tot_h - dcs_h), xdt))
    return tuple(ys) + tuple(new_st)


def ssd_fwd(x_hm, b_hm, c_hm, proj, dt_bias, a_log, d_skip, *, name):
    nh, T, P = x_hm.shape
    ng, N = b_hm.shape[0], b_hm.shape[2]
    L = SSM_CHUNK
    nc = T // L
    dtb = OFF_DT // LANES

    def body(x_ref, b_ref, c_ref, dt_ref, db_ref, al_ref, ds_ref, y_ref, s_ref, state):
        @pl.when(pl.program_id(0) == 0)
        def _():
            state[...] = jnp.zeros_like(state)

        s_ref[...] = state[...]
        args = ([x_ref[h] for h in range(nh)] + [b_ref[g] for g in range(ng)] + [c_ref[g] for g in range(ng)]
                + [dt_ref[...]] + [state[h] for h in range(nh)] + [db_ref[...], al_ref[...], ds_ref[...]])
        res = ssd_chunk_fn(*args)
        for h in range(nh):
            y_ref[:, h * P:(h + 1) * P] = res[h]
            state[h] = res[nh + h]

    par = pl.BlockSpec((1, LANES), lambda i: (0, 0))
    return pl.pallas_call(
        body, name=name, grid=(nc,),
        in_specs=[pl.BlockSpec((nh, L, P), lambda i: (0, i, 0)), pl.BlockSpec((ng, L, N), lambda i: (0, i, 0)),
                  pl.BlockSpec((ng, L, N), lambda i: (0, i, 0)), pl.BlockSpec((L, LANES), lambda i: (i, dtb)),
                  par, par, par],
        out_specs=[pl.BlockSpec((L, nh * P), lambda i: (i, 0)),
                   pl.BlockSpec((None, nh, N, P), lambda i: (i, 0, 0, 0))],
        out_shape=[jax.ShapeDtypeStruct((T, nh * P), F32), jax.ShapeDtypeStruct((nc, nh, N, P), F32)],
        scratch_shapes=[pltpu.VMEM((nh, N, P), F32)],
        compiler_params=_cparams("arbitrary"),
    )(x_hm, b_hm, c_hm, proj, dt_bias, a_log, d_skip)


def ssd_bwd(x_hm, b_hm, c_hm, proj, states, dt_bias, a_log, d_skip, dy, *, name):
    nh, T, P = x_hm.shape
    ng, N = b_hm.shape[0], b_hm.shape[2]
    L = SSM_CHUNK
    nc = T // L
    dtb = OFF_DT // LANES

    def body(x_ref, b_ref, c_ref, dt_ref, s_ref, db_ref, al_ref, ds_ref, dy_ref,
             dxbc_ref, ddt_ref, gdb_ref, gal_ref, gds_ref, dstate):
        @pl.when(pl.program_id(0) == 0)
        def _():
            dstate[...] = jnp.zeros_like(dstate)
            gdb_ref[...] = jnp.zeros_like(gdb_ref)
            gal_ref[...] = jnp.zeros_like(gal_ref)
            gds_ref[...] = jnp.zeros_like(gds_ref)

        args = ([x_ref[h] for h in range(nh)] + [b_ref[g] for g in range(ng)] + [c_ref[g] for g in range(ng)]
                + [dt_ref[...]] + [s_ref[h] for h in range(nh)] + [db_ref[...], al_ref[...], ds_ref[...]])
        _, vjp = jax.vjp(ssd_chunk_fn, *args)
        g = vjp(tuple([dy_ref[h] for h in range(nh)] + [dstate[h] for h in range(nh)]))
        for j in range(nh + 2 * ng):
            dxbc_ref[:, j * P:(j + 1) * P] = g[j]
        ddt_ref[...] = g[nh + 2 * ng]
        for h in range(nh):
            dstate[h] = g[nh + 2 * ng + 1 + h]
        gdb_ref[...] += g[-3]
        gal_ref[...] += g[-2]
        gds_ref[...] += g[-1]

    rev = lambda i: nc - 1 - i
    par = pl.BlockSpec((1, LANES), lambda i: (0, 0))
    return pl.pallas_call(
        body, name=name, grid=(nc,),
        in_specs=[pl.BlockSpec((nh, L, P), lambda i: (0, rev(i), 0)), pl.BlockSpec((ng, L, N), lambda i: (0, rev(i), 0)),
                  pl.BlockSpec((ng, L, N), lambda i: (0, rev(i), 0)), pl.BlockSpec((L, LANES), lambda i: (rev(i), dtb)),
                  pl.BlockSpec((None, nh, N, P), lambda i: (rev(i), 0, 0, 0)), par, par, par,
                  pl.BlockSpec((nh, L, P), lambda i: (0, rev(i), 0))],
        out_specs=[pl.BlockSpec((L, (nh + 2 * ng) * P), lambda i: (rev(i), 0)),
                   pl.BlockSpec((L, LANES), lambda i: (rev(i), 0)), par, par, par],
        out_shape=[jax.ShapeDtypeStruct((T, (nh + 2 * ng) * P), F32), jax.ShapeDtypeStruct((T, LANES), F32),
                   jax.ShapeDtypeStruct((1, LANES), F32), jax.ShapeDtypeStruct((1, LANES), F32),
                   jax.ShapeDtypeStruct((1, LANES), F32)],
        scratch_shapes=[pltpu.VMEM((nh, N, P), F32)],
        compiler_params=_cparams("arbitrary"),
    )(x_hm, b_hm, c_hm, proj, states, dt_bias, a_log, d_skip, dy)


def loss_head(h, target, g, *, name, tile=512):
    T, C = h.shape
    tl = min(T, tile)

    def body(h_ref, t_ref, g_ref, dh_ref, dg_ref, ls_ref):
        @pl.when(pl.program_id(0) == 0)
        def _():
            dg_ref[...] = jnp.zeros_like(dg_ref)
            ls_ref[...] = jnp.zeros_like(ls_ref)

        (y,), vjp = jax.vjp(rms_fn, h_ref[...], g_ref[...])
        err = y - t_ref[...]
        ls_ref[...] += jnp.sum(err * err, axis=0, keepdims=True) * (0.5 / C)
        dh, dg = vjp((err * (1.0 / C),))
        dh_ref[...] = dh
        dg_ref[...] += dg

    row = pl.BlockSpec((tl, C), lambda i: (i, 0))
    par = pl.BlockSpec((1, C), lambda i: (0, 0))
    return pl.pallas_call(
        body, name=name, grid=(T // tl,), in_specs=[row, row, par], out_specs=[row, par, par],
        out_shape=[jax.ShapeDtypeStruct((T, C), F32), jax.ShapeDtypeStruct((1, C), F32),
                   jax.ShapeDtypeStruct((1, C), F32)],
        compiler_params=_cparams("arbitrary"),
    )(h, target, g)


def adamw(w, g, m, v, *, name):
    R, C = w.shape
    tr = R
    for d in range(8, min(R, 512) + 1, 8):
        if R % d == 0:
            tr = d
    c1 = 1.0 - ADAM_B1 ** ADAM_STEP
    c2 = 1.0 - ADAM_B2 ** ADAM_STEP

    def body(w_ref, g_ref, m_ref, v_ref, d_ref, nm_ref, nv_ref):
        gv = g_ref[...]
        nm = ADAM_B1 * m_ref[...] + (1.0 - ADAM_B1) * gv
        nv = ADAM_B2 * v_ref[...] + (1.0 - ADAM_B2) * (gv * gv)
        d_ref[...] = -ADAM_LR * ((nm / c1) / (jnp.sqrt(nv / c2) + ADAM_EPS) + ADAM_WD * w_ref[...])
        nm_ref[...] = nm
        nv_ref[...] = nv

    spec = pl.BlockSpec((tr, C), lambda i: (i, 0))
    return pl.pallas_call(
        body, name=name, grid=(R // tr,), in_specs=[spec] * 4, out_specs=[spec] * 3,
        out_shape=[jax.ShapeDtypeStruct((R, C), F32)] * 3,
        compiler_params=_cparams("parallel"),
    )(w, g, m, v)


MESH = pl.DeviceIdType.MESH
HBM_SPEC = pl.BlockSpec(memory_space=pltpu.HBM)


def _place():
    return lax.axis_index("x"), lax.axis_index("y"), lax.axis_index("c")


def _direct_exchange(kind, src_ref, out_ref, send_sems, recv_sems, local_sem):
    x, y, c = _place()
    me = 4 * x + 2 * y + c

    def block_for(dest):
        return src_ref if kind == "gather" else src_ref.at[dest]

    own = pltpu.make_async_copy(block_for(me), out_ref.at[me], local_sem)
    sends, arrivals = [], []
    for f in range(1, 8):
        px = jnp.where((f >> 2) & 1, 1 - x, x)
        py = jnp.where((f >> 1) & 1, 1 - y, y)
        pc = jnp.where(f & 1, 1 - c, c)
        peer = 4 * px + 2 * py + pc
        for dst, group in ((out_ref.at[me], sends), (out_ref.at[peer], arrivals)):
            group.append(pltpu.make_async_remote_copy(
                src_ref=block_for(peer), dst_ref=dst, send_sem=send_sems.at[f - 1], recv_sem=recv_sems.at[f - 1],
                device_id=(px, py, pc), device_id_type=MESH))

    def start():
        own.start()
        for cp in sends:
            cp.start()

    def finish():
        for cp in arrivals:
            cp.wait_recv()
        for cp in sends:
            cp.wait_send()
        own.wait()

    return start, finish


def allgather_blocks(mine, *, name):
    R = mine.shape[0]

    def body(x_ref, out_ref, send_sems, recv_sems, local_sem):
        x, y, c = _place()
        me, sibling = (x, y, c), (x, y, 1 - c)
        chips = [(1 - x, y), (x, 1 - y), (1 - x, 1 - y)]

        def slot(px, py, pc):
            return out_ref.at[4 * px + 2 * py + pc]

        def copy(k, block, to, src=None):
            return pltpu.make_async_remote_copy(
                src_ref=slot(*block) if src is None else src, dst_ref=slot(*block),
                send_sem=send_sems.at[k], recv_sem=recv_sems.at[k], device_id=to, device_id_type=MESH)

        own = pltpu.make_async_copy(x_ref, slot(*me), local_sem)
        own.start()
        first = [copy(0, me, sibling, src=x_ref)]
        first += [copy(1 + j, me, (*chip, c), src=x_ref) for j, chip in enumerate(chips)]
        for cp in first:
            cp.start()
        passed = [copy(4 + j, (*chip, c), sibling) for j, chip in enumerate(chips)]
        for j, chip in enumerate(chips):
            copy(1 + j, (*chip, c), me).wait_recv()
            passed[j].start()
        copy(0, sibling, me).wait_recv()
        for j, chip in enumerate(chips):
            copy(4 + j, (*chip, 1 - c), me).wait_recv()
        for cp in first + passed:
            cp.wait_send()
        own.wait()

    return pl.pallas_call(
        body, name=name, out_shape=jax.ShapeDtypeStruct((8, R, LANES), mine.dtype),
        in_specs=[HBM_SPEC], out_specs=HBM_SPEC,
        scratch_shapes=[pltpu.SemaphoreType.DMA((7,)), pltpu.SemaphoreType.DMA((7,)), pltpu.SemaphoreType.DMA],
    )(mine)


def allgather_direct(mine, *, name):
    R = mine.shape[0]

    def body(x_ref, out_ref, send_sems, recv_sems, local_sem):
        start, finish = _direct_exchange("gather", x_ref, out_ref, send_sems, recv_sems, local_sem)
        start()
        finish()

    return pl.pallas_call(
        body, name=name, out_shape=jax.ShapeDtypeStruct((8, R, LANES), mine.dtype),
        in_specs=[HBM_SPEC], out_specs=HBM_SPEC,
        scratch_shapes=[pltpu.SemaphoreType.DMA((7,)), pltpu.SemaphoreType.DMA((7,)), pltpu.SemaphoreType.DMA],
    )(mine)


def send_to_sibling(v, *, name):
    def body(v_ref, out_ref, send_sem, recv_sem):
        x, y, c = _place()
        cp = pltpu.make_async_remote_copy(src_ref=v_ref, dst_ref=out_ref, send_sem=send_sem, recv_sem=recv_sem,
                                          device_id=(x, y, 1 - c), device_id_type=MESH)
        cp.start()
        cp.wait()

    return pl.pallas_call(
        body, name=name, out_shape=jax.ShapeDtypeStruct(v.shape, v.dtype), in_specs=[HBM_SPEC], out_specs=HBM_SPEC,
        scratch_shapes=[pltpu.SemaphoreType.DMA, pltpu.SemaphoreType.DMA],
    )(v)


def chip_exchange(p, *, name):
    R = p.shape[1]

    def body(p_ref, out_ref, send_sems, recv_sems):
        x, y, c = _place()
        chips = [(1 - x, y), (x, 1 - y), (1 - x, 1 - y)]
        sends = [pltpu.make_async_remote_copy(
            src_ref=p_ref.at[2 * px + py], dst_ref=out_ref.at[j], send_sem=send_sems.at[j], recv_sem=recv_sems.at[j],
            device_id=(px, py, c), device_id_type=MESH) for j, (px, py) in enumerate(chips)]
        for cp in sends:
            cp.start()
        for cp in sends:
            cp.wait()

    return pl.pallas_call(
        body, name=name, out_shape=jax.ShapeDtypeStruct((3, R, LANES), p.dtype), in_specs=[HBM_SPEC],
        out_specs=HBM_SPEC,
        scratch_shapes=[pltpu.SemaphoreType.DMA((3,)), pltpu.SemaphoreType.DMA((3,))],
    )(p)


def add_blocks(terms, out_dtype, *, name, tile=1024):
    terms = [t if isinstance(t, tuple) else (t, None) for t in terms]
    R = terms[0][0].shape[-2]
    tr = R
    for d in range(16, min(R, tile) + 1, 16):
        if R % d == 0:
            tr = d

    def body(*refs):
        acc = refs[0][...].astype(F32)
        for ref in refs[1:-1]:
            acc = acc + ref[...].astype(F32)
        refs[-1][...] = acc.astype(out_dtype)

    spec = pl.BlockSpec((tr, LANES), lambda i: (i, 0))
    in_specs = [spec if slot is None else pl.BlockSpec((None, tr, LANES), lambda i, slot=slot: (slot, i, 0))
                for _, slot in terms]
    return pl.pallas_call(
        body, name=name, grid=(R // tr,), in_specs=in_specs, out_specs=spec,
        out_shape=jax.ShapeDtypeStruct((R, LANES), out_dtype), compiler_params=_cparams("parallel"),
    )(*[a for a, _ in terms])


FLAT_ROW_STEP = 640


def _half_rows(arr, cc):
    hr = arr.shape[0] // 2
    return lax.dynamic_slice_in_dim(arr, cc * hr, hr, axis=0).reshape(-1)


def _flat_half(shards, cc, dtype):
    flat = jnp.concatenate([_half_rows(shards[n], cc).astype(dtype) for n in BIG])
    rows = -(-flat.shape[0] // (FLAT_ROW_STEP * LANES)) * FLAT_ROW_STEP
    return jnp.pad(flat, (0, rows * LANES - flat.shape[0])).reshape(rows, LANES)


def _flat_rows(shapes, names=BIG):
    n = sum((shapes[m][0] // 2) * shapes[m][1] for m in names) // LANES
    return -(-n // FLAT_ROW_STEP) * FLAT_ROW_STEP


def _to_blocks(full, shapes, dtype, names=BIG):
    pieces = []
    for n in names:
        R, C = shapes[n]
        a = full[n].astype(dtype)
        if BIG_AXIS[n] == 2:
            a = a.reshape(2, R // 2, 4, C).transpose(2, 0, 1, 3)
        pieces.append(a.reshape(8, (R // 2) * C // LANES, LANES))
    flat = jnp.concatenate(pieces, axis=1)
    return jnp.pad(flat, ((0, 0), (0, _flat_rows(shapes, names) - flat.shape[1]), (0, 0)))


def _from_blocks(g8, shapes):
    out, off = {}, 0
    for n in BIG:
        R, C = shapes[n]
        rows = (R // 2) * C // LANES
        a = g8[:, off:off + rows, :].reshape(4, 2, R // 2, C)
        out[n] = a.transpose(1, 2, 0, 3).reshape(R, 4 * C) if BIG_AXIS[n] == 2 else a.reshape(4 * R, C)
        off += rows
    return out


def _unflat_halves(flat_by_c, shapes, names=BIG):
    out, off = {}, 0
    for n in names:
        R, C = shapes[n]
        sz = (R // 2) * C
        out[n] = jnp.concatenate([flat_by_c[c][off:off + sz].reshape(R // 2, C) for c in range(2)], axis=0)
        off += sz
    return out


def _to_heads(a, nh):
    T = a.shape[0]
    return a.reshape(T, nh, a.shape[1] // nh).transpose(1, 0, 2)


def _from_heads(a):
    nh, T, d = a.shape
    return a.transpose(1, 0, 2).reshape(T, nh * d)


def _to_heads_t(a, nh):
    T = a.shape[0]
    return a.reshape(T, nh, a.shape[1] // nh).transpose(1, 2, 0)


def _from_heads_t(a):
    nh, d, T = a.shape
    return a.transpose(2, 0, 1).reshape(T, nh * d)


def _pad_cols(a, n):
    return jnp.pad(a, ((0, 0), (0, n - a.shape[1])))


def _pack_w_in(w):
    offs = [sum(IN_SPLITS[:i]) for i in range(len(IN_SPLITS) + 1)]
    sb, z, xbc, dt, cq, ckv, kr = [w[:, offs[i]:offs[i + 1]] for i in range(len(IN_SPLITS))]
    zeros = lambda n: jnp.zeros((w.shape[0], n), w.dtype)
    h = MLA_ROPE // 2
    kra = jnp.concatenate([zeros(MLA_NOPE), kr, zeros(LANES - MLA_QK)], axis=1)
    krb = jnp.concatenate([zeros(MLA_NOPE), -kr[:, h:], kr[:, :h], zeros(LANES - MLA_QK)], axis=1)
    return sb, jnp.concatenate([z, xbc, cq, ckv, _pad_cols(dt, LANES), kra, krb], axis=1)


def _unpack_gw_in(g_sb, g):
    h = MLA_ROPE // 2
    ga, gb = g[:, OFF_KRA:OFF_KRA + LANES], g[:, OFF_KRB:OFF_KRB + LANES]
    gkr = ga[:, MLA_NOPE:MLA_QK] + jnp.concatenate([gb[:, MLA_NOPE + h:MLA_QK], -gb[:, MLA_NOPE:MLA_NOPE + h]], axis=1)
    return jnp.concatenate([g_sb, g[:, OFF_Z:OFF_Z + 512], g[:, OFF_XBC:OFF_XBC + 768],
                            g[:, OFF_DT:OFF_DT + 8], g[:, OFF_CQ:OFF_CQ + 256], g[:, OFF_CKV:OFF_CKV + 128], gkr], axis=1)


def _pack_w_uq(w):
    zeros = lambda n: jnp.zeros((w.shape[0], n), w.dtype)
    h = MLA_ROPE // 2
    pp, rr = [], []
    for i in range(MLA_HEADS):
        nope = w[:, MLA_QK * i:MLA_QK * i + MLA_NOPE]
        rope = w[:, MLA_QK * i + MLA_NOPE:MLA_QK * (i + 1)]
        pp += [nope, rope, zeros(LANES - MLA_QK)]
        rr += [zeros(MLA_NOPE), -rope[:, h:], rope[:, :h], zeros(LANES - MLA_QK)]
    return jnp.concatenate(pp, axis=1), jnp.concatenate(rr, axis=1)


def _unpack_gw_uq(gp, gr):
    h = MLA_ROPE // 2
    out = []
    for i in range(MLA_HEADS):
        b = LANES * i
        out.append(gp[:, b:b + MLA_NOPE])
        out.append(gp[:, b + MLA_NOPE:b + MLA_NOPE + h] + gr[:, b + MLA_NOPE + h:b + MLA_QK])
        out.append(gp[:, b + MLA_NOPE + h:b + MLA_QK] - gr[:, b + MLA_NOPE:b + MLA_NOPE + h])
    return jnp.concatenate(out, axis=1)


def _pack_w_ukv(w):
    zeros = lambda n: jnp.zeros((w.shape[0], n), w.dtype)
    kk, vv = [], []
    for i in range(MLA_HEADS):
        b = (MLA_NOPE + MLA_V) * i
        kk += [w[:, b:b + MLA_NOPE], zeros(LANES - MLA_NOPE)]
        vv += [w[:, b + MLA_NOPE:b + MLA_NOPE + MLA_V], zeros(LANES - MLA_V)]
    return jnp.concatenate(kk, axis=1), jnp.concatenate(vv, axis=1)


def _unpack_gw_ukv(gk, gv):
    out = []
    for i in range(MLA_HEADS):
        out += [gk[:, LANES * i:LANES * i + MLA_NOPE], gv[:, LANES * i:LANES * i + MLA_V]]
    return jnp.concatenate(out, axis=1)


def _rope_tables(positions):
    inv_freq = 1.0 / (ROPE_THETA ** (jnp.arange(0, MLA_ROPE, 2, dtype=F32) / MLA_ROPE))
    ang = positions.astype(F32)[:, None] * inv_freq
    cos, sin = jnp.cos(ang), jnp.sin(ang)
    T = positions.shape[0]
    one, zero = jnp.ones((T, MLA_NOPE), F32), jnp.zeros((T, MLA_NOPE), F32)
    pad1, pad0 = jnp.ones((T, LANES - MLA_QK), F32), jnp.zeros((T, LANES - MLA_QK), F32)
    return jnp.concatenate([one, cos, cos, pad1], axis=1), jnp.concatenate([zero, sin, sin, pad0], axis=1)


def _row(v):
    return v.reshape(1, -1)


def _pad_row(v):
    return _pad_cols(v.reshape(1, -1), LANES)


def _layer_weights(full, small, li):
    p = {}
    p["w_sb"], p["w_rest"] = _pack_w_in(full["w_in"])
    q_scale = jnp.concatenate([jnp.full((1, SB_HEADS * SB_DIM), SB_DIM ** -0.5, BF16),
                               jnp.ones((1, 2 * SB_HEADS * SB_DIM), BF16)], axis=1)
    p["w_sb_fwd"] = p["w_sb"] * q_scale
    p["wqp"], p["wqr"] = _pack_w_uq(full["mla_w_uq"])
    p["wkp"], p["wvp"] = _pack_w_ukv(full["mla_w_ukv"])
    p["w_out"] = full["w_out"]
    p["w_up"] = full["ffn_w_up"]
    p["w_down"] = full["ffn_w_down"]
    for n in ("mix_norm", "sb_out_norm", "ssm_conv_b", "ssm_out_norm", "mla_q_norm", "mla_kv_norm", "mla_out_norm",
              "ffn_norm", "ffn_conv_b"):
        p[n] = _row(small[n][li])
    for n in ("ssm_dt_bias", "ssm_a_log", "ssm_d"):
        p[n] = _pad_row(small[n][li])
    p["ssm_conv_w"] = small["ssm_conv_w"][li]
    p["ffn_conv_w"] = small["ffn_conv_w"][li]
    return p


def _layer_fwd(h, p, cos, sin, li, exchange=None):
    T = h.shape[0]
    nm = lambda s: "l%d_%s" % (li, s)
    s = {"h": h}
    (n1,) = rowwise(rms_fn, [h], [p["mix_norm"]], [(D_MODEL, BF16)], name=nm("mix_norm"))
    proj = matmul(n1, p["w_rest"], name=nm("in_proj"))
    qkv = matmul(n1, p["w_sb_fwd"], name=nm("in_proj_sb"), out_dtype=BF16)
    s["n1"], s["proj"] = n1, proj
    s["sb_q"] = _to_heads_t(qkv[:, 0:256], SB_HEADS)
    s["sb_k"] = _to_heads(qkv[:, 256:512], SB_HEADS)
    s["sb_v"] = _to_heads(qkv[:, 512:768], SB_HEADS)
    y_sb_hm, s["sb_bt"], s["sb_first"] = sb_fwd(s["sb_q"], s["sb_k"], s["sb_v"], name=nm("sb_fwd"))
    s["y_sb"] = _from_heads_t(y_sb_hm)
    s["x_hm"], s["b_hm"], s["c_hm"] = ssm_conv_act(proj, p["ssm_conv_w"], p["ssm_conv_b"], name=nm("ssm_conv"))
    s["y_ssm"], s["states"] = ssd_fwd(s["x_hm"], s["b_hm"], s["c_hm"], proj, p["ssm_dt_bias"], p["ssm_a_log"],
                                      p["ssm_d"], name=nm("ssd_fwd"))
    rows = [(proj, 256, OFF_CQ // 256), (proj, 128, OFF_CKV // 128), (proj, 128, OFF_KRA // 128),
            (proj, 128, OFF_KRB // 128), cos, sin]
    qp, kp, vv = rowwise(mla_prep_fn, rows, [p["mla_q_norm"], p["mla_kv_norm"], p["wqp"], p["wqr"], p["wkp"], p["wvp"]],
                         [(512, BF16), (512, BF16), (512, BF16)], name=nm("mla_prep"))
    s["mla_q"], s["mla_k"], s["mla_v"] = _to_heads_t(qp, MLA_HEADS), kp, vv
    s["mla_o"], s["mla_lse"], *rode = mla_fwd(s["mla_q"], kp, vv, name=nm("mla_fwd"), exchange=exchange)
    s["y_mla"] = _from_heads_t(s["mla_o"][:, :MLA_V, :])
    (cat,) = rowwise(merge_fn, [s["y_sb"], s["y_ssm"], (proj, 512, OFF_Z // 512), s["y_mla"]],
                     [p["sb_out_norm"], p["ssm_out_norm"], p["mla_out_norm"]], [(D_MODEL, BF16)], name=nm("merge"),
                     post=lambda a, b, c: (jnp.concatenate([a, b, c], axis=1),))
    s["cat"] = cat
    h1 = matmul(cat, p["w_out"], name=nm("out_proj"), residual=h)
    s["h1"] = h1
    (n2,) = rowwise(rms_fn, [h1], [p["ffn_norm"]], [(D_MODEL, BF16)], name=nm("ffn_norm"))
    up = matmul(n2, p["w_up"], name=nm("ffn_up"))
    act = ffn_act(up, p["ffn_conv_w"], p["ffn_conv_b"], name=nm("ffn_act"))
    s["n2"], s["up"], s["act"] = n2, up, act
    h2 = matmul(act, p["w_down"], name=nm("ffn_down"), residual=h1)
    return h2, s, (rode[0] if rode else None)


def _layer_bwd(dh2, s, p, cos, sin, li, exchange=None):
    nm = lambda t: "l%d_%s" % (li, t)
    g = {}
    proj = s["proj"]
    g["ffn_w_down"] = matmul(s["act"], dh2, name=nm("g_w_down"), ta=True)
    d_act = matmul(dh2, p["w_down"], name=nm("d_act"), out_dtype=BF16, tb=True)
    d_up_g, d_up_v, gwg, gwv, gbg, gbv = ffn_bwd_fused(s["up"], p["ffn_conv_w"], p["ffn_conv_b"], d_act,
                                                       name=nm("ffn_act_bwd"))
    g["ffn_conv_w"] = jnp.concatenate([gwg, gwv], axis=1)
    g["ffn_conv_b"] = jnp.concatenate([gbg[0], gbv[0]])
    g["ffn_w_up"] = jnp.concatenate([matmul(s["n2"], d_up_g, name=nm("g_w_up_gate"), ta=True),
                                     matmul(s["n2"], d_up_v, name=nm("g_w_up_val"), ta=True)], axis=1)
    d_n2 = matmul(d_up_g, p["w_up"], name=nm("d_n2_gate"), tb=True)
    d_n2 = matmul(d_up_v, p["w_up"], name=nm("d_n2_val"), tb=True, b_k0=D_FF, residual=d_n2)
    (dh1,), (gn,) = rowwise_bwd(rms_fn, [s["h1"]], [], [p["ffn_norm"]], [d_n2], [F32], name=nm("ffn_norm_bwd"),
                                add0=dh2)
    g["ffn_norm"] = gn[0]
    g["w_out"] = matmul(s["cat"], dh1, name=nm("g_w_out"), ta=True)
    d_cat = matmul(dh1, p["w_out"], name=nm("d_cat"), tb=True)
    (d_ysb, d_yssm, d_z, d_ymla), (g1, g2, g3) = rowwise_bwd(
        merge_fn, [s["y_sb"], s["y_ssm"], (proj, 512, OFF_Z // 512), s["y_mla"]], [],
        [p["sb_out_norm"], p["ssm_out_norm"], p["mla_out_norm"]], [d_cat], [F32, F32, BF16, F32], name=nm("merge_bwd"),
        pre_ct=lambda d: (d[:, 0:256], d[:, 256:768], d[:, 768:1024]))
    g["sb_out_norm"], g["ssm_out_norm"], g["mla_out_norm"] = g1[0], g2[0], g3[0]
    dq, dk, dv = sb_bwd(s["sb_q"], s["sb_k"], s["sb_v"], _to_heads_t(d_ysb, SB_HEADS), s["sb_bt"], s["sb_first"], name=nm("sb_bwd"),
                        q_scale=SB_DIM ** -0.5)
    d_sb = jnp.concatenate([_from_heads_t(dq), _from_heads(dk), _from_heads(dv)], axis=1).astype(BF16)
    if callable(exchange):
        exchange = exchange(g)
    do_t = jnp.pad(_to_heads_t(d_ymla, MLA_HEADS), ((0, 0), (0, LANES - MLA_V), (0, 0)))
    dqp, dkp, dvv, *rode = mla_bwd(s["mla_q"], s["mla_k"], s["mla_v"], do_t, s["mla_o"], s["mla_lse"],
                                   name=nm("mla_bwd"), exchange=exchange)
    rows = [(proj, 256, OFF_CQ // 256), (proj, 128, OFF_CKV // 128), (proj, 128, OFF_KRA // 128),
            (proj, 128, OFF_KRB // 128)]
    (d_cq, d_ckv, d_kra, d_krb), (gqn, gkvn, gwqp, gwqr, gwkp, gwvp) = rowwise_bwd(
        mla_prep_fn, rows, [cos, sin], [p["mla_q_norm"], p["mla_kv_norm"], p["wqp"], p["wqr"], p["wkp"], p["wvp"]],
        [_from_heads_t(dqp), dkp, dvv], [BF16] * 4, name=nm("mla_prep_bwd"), tile=256)
    g["mla_q_norm"], g["mla_kv_norm"] = gqn[0], gkvn[0]
    g["mla_w_uq"] = _unpack_gw_uq(gwqp, gwqr)
    g["mla_w_ukv"] = _unpack_gw_ukv(gwkp, gwvp)
    d_xbc_act, d_dt, gdb, gal, gds = ssd_bwd(
        s["x_hm"], s["b_hm"], s["c_hm"], proj, s["states"], p["ssm_dt_bias"], p["ssm_a_log"], p["ssm_d"],
        _to_heads(d_yssm, SSM_HEADS), name=nm("ssd_bwd"))
    g["ssm_dt_bias"], g["ssm_a_log"], g["ssm_d"] = gdb[0, :8], gal[0, :8], gds[0, :8]
    d_pre = ssm_conv_bwd_a(proj, p["ssm_conv_w"], p["ssm_conv_b"], d_xbc_act, name=nm("ssm_conv_bwd_a"))
    d_xbc, g["ssm_conv_w"], gscb = conv_bwd_b(d_pre, proj, OFF_XBC, p["ssm_conv_w"], name=nm("ssm_conv_bwd_b"),
                                              out_dtype=BF16, tc=256)
    g["ssm_conv_b"] = gscb[0]
    d_proj = jnp.concatenate([d_z, d_xbc, d_cq, d_ckv, d_dt.astype(BF16), d_kra, d_krb], axis=1)
    g["w_in"] = _unpack_gw_in(matmul(s["n1"], d_sb, name=nm("g_w_in_sb"), ta=True),
                              matmul(s["n1"], d_proj, name=nm("g_w_in"), ta=True))
    d_n1 = matmul(d_sb, p["w_sb"], name=nm("d_n1_sb"), tb=True)
    d_n1 = matmul(d_proj, p["w_rest"], name=nm("d_n1"), tb=True, residual=d_n1)
    (dh0,), (gm,) = rowwise_bwd(rms_fn, [s["h"]], [], [p["mix_norm"]], [d_n1], [F32], name=nm("mix_norm_bwd"),
                                add0=dh1)
    g["mix_norm"] = gm[0]
    return dh0, g, (rode[0] if rode else None)


def kernel(x, positions, mix_norm, w_in, sb_out_norm, ssm_conv_w, ssm_conv_b, ssm_dt_bias, ssm_a_log, ssm_d, ssm_out_norm, mla_q_norm, mla_w_uq, mla_kv_norm, mla_w_ukv, mla_out_norm, w_out, ffn_norm, ffn_w_up, ffn_conv_w, ffn_conv_b, ffn_w_down, final_norm, loss_target, m_mix_norm, m_w_in, m_sb_out_norm, m_ssm_conv_w, m_ssm_conv_b, m_ssm_dt_bias, m_ssm_a_log, m_ssm_d, m_ssm_out_norm, m_mla_q_norm, m_mla_w_uq, m_mla_kv_norm, m_mla_w_ukv, m_mla_out_norm, m_w_out, m_ffn_norm, m_ffn_w_up, m_ffn_conv_w, m_ffn_conv_b, m_ffn_w_down, m_final_norm, v_mix_norm, v_w_in, v_sb_out_norm, v_ssm_conv_w, v_ssm_conv_b, v_ssm_dt_bias, v_ssm_a_log, v_ssm_d, v_ssm_out_norm, v_mla_q_norm, v_mla_w_uq, v_mla_kv_norm, v_mla_w_ukv, v_mla_out_norm, v_w_out, v_ffn_norm, v_ffn_w_up, v_ffn_conv_w, v_ffn_conv_b, v_ffn_w_down, v_final_norm):
    W = dict(mix_norm=mix_norm, w_in=w_in, sb_out_norm=sb_out_norm, ssm_conv_w=ssm_conv_w, ssm_conv_b=ssm_conv_b,
             ssm_dt_bias=ssm_dt_bias, ssm_a_log=ssm_a_log, ssm_d=ssm_d, ssm_out_norm=ssm_out_norm,
             mla_q_norm=mla_q_norm, mla_w_uq=mla_w_uq, mla_kv_norm=mla_kv_norm, mla_w_ukv=mla_w_ukv,
             mla_out_norm=mla_out_norm, w_out=w_out, ffn_norm=ffn_norm, ffn_w_up=ffn_w_up, ffn_conv_w=ffn_conv_w,
             ffn_conv_b=ffn_conv_b, ffn_w_down=ffn_w_down, final_norm=final_norm)
    M = dict(mix_norm=m_mix_norm, w_in=m_w_in, sb_out_norm=m_sb_out_norm, ssm_conv_w=m_ssm_conv_w,
             ssm_conv_b=m_ssm_conv_b, ssm_dt_bias=m_ssm_dt_bias, ssm_a_log=m_ssm_a_log, ssm_d=m_ssm_d,
             ssm_out_norm=m_ssm_out_norm, mla_q_norm=m_mla_q_norm, mla_w_uq=m_mla_w_uq, mla_kv_norm=m_mla_kv_norm,
             mla_w_ukv=m_mla_w_ukv, mla_out_norm=m_mla_out_norm, w_out=m_w_out, ffn_norm=m_ffn_norm,
             ffn_w_up=m_ffn_w_up, ffn_conv_w=m_ffn_conv_w, ffn_conv_b=m_ffn_conv_b, ffn_w_down=m_ffn_w_down,
             final_norm=m_final_norm)
    V = dict(mix_norm=v_mix_norm, w_in=v_w_in, sb_out_norm=v_sb_out_norm, ssm_conv_w=v_ssm_conv_w,
             ssm_conv_b=v_ssm_conv_b, ssm_dt_bias=v_ssm_dt_bias, ssm_a_log=v_ssm_a_log, ssm_d=v_ssm_d,
             ssm_out_norm=v_ssm_out_norm, mla_q_norm=v_mla_q_norm, mla_w_uq=v_mla_w_uq, mla_kv_norm=v_mla_kv_norm,
             mla_w_ukv=v_mla_w_ukv, mla_out_norm=v_mla_out_norm, w_out=v_w_out, ffn_norm=v_ffn_norm,
             ffn_w_up=v_ffn_w_up, ffn_conv_w=v_ffn_conv_w, ffn_conv_b=v_ffn_conv_b, ffn_w_down=v_ffn_w_down,
             final_norm=v_final_norm)
    depth = mix_norm.shape[0]
    cx, cy, cc = _place()
    chip = 2 * cx + cy
    T = x.shape[1]

    assert depth == 2
    shard_shapes = {n: W[n].shape[1:] for n in BIG}

    def layer_of(d, li):
        return {n: d[n][li] for n in BIG}

    def assemble(g8):
        return _from_blocks(g8, shard_shapes)

    full0 = assemble(allgather_blocks(_flat_half(layer_of(W, 0), cc, BF16), name="gather_weights_l0"))
    conv_full = {}
    small = {n: W[n] for n in SMALL_REPL}
    cw_flat = jnp.concatenate([W[n].reshape(-1) for n in SMALL_SHARD])
    cw_rows = -(-cw_flat.shape[0] // (8 * LANES)) * 8
    cw_all = allgather_direct(jnp.pad(cw_flat, (0, cw_rows * LANES - cw_flat.shape[0])).reshape(cw_rows, LANES),
                              name="gather_conv_taps")
    off = 0
    for n in SMALL_SHARD:
        sz = W[n].size
        conv_full[n] = jnp.concatenate(
            [cw_all[2 * k].reshape(-1)[off:off + sz].reshape(W[n].shape) for k in range(4)], axis=2)
        off += sz
    small.update(conv_full)

    cos, sin = _rope_tables(positions[0])
    params0 = _layer_weights(full0, small, 0)
    h, s0, g8 = _layer_fwd(x[0], params0, cos, sin, 0, exchange=("gather", _flat_half(layer_of(W, 1), cc, BF16)))
    params1 = _layer_weights(assemble(g8), small, 1)
    h, s1, _ = _layer_fwd(h, params1, cos, sin, 1)
    dh, g_final, loss_lanes = loss_head(h, loss_target[0], _row(final_norm), name="loss_head")

    ffn_w = ("ffn_w_up", "ffn_w_down")
    mix_w = tuple(n for n in BIG if n not in ffn_w)
    dh, g1, _ = _layer_bwd(dh, s1, params1, cos, sin, 1)
    blocks1 = _to_blocks(g1, shard_shapes, BF16)
    R1, Rf = blocks1.shape[1], _flat_rows(shard_shapes, ffn_w)

    def rider(g0_so_far):
        return "all_to_all", jnp.concatenate([blocks1, _to_blocks(g0_so_far, shard_shapes, BF16, ffn_w)], axis=1)

    dh, g0, from_all = _layer_bwd(dh, s0, params0, cos, sin, 0, exchange=rider)
    grad_x = dh[None]
    grads = [g0, g1]
    G = {n: jnp.stack([grads[li][n] for li in range(depth)]) for n in WEIGHTS if n != "final_norm" and n not in BIG}
    G["final_norm"] = g_final[0]
    half1 = add_blocks([(from_all, d) for d in range(8)], F32, name="grads_rode_sum")

    blocks0 = _to_blocks(g0, shard_shapes, BF16, mix_w)
    R = blocks0.shape[1]
    blocks0 = blocks0.reshape(4, 2, R, LANES)
    mine_first = lax.dynamic_index_in_dim(blocks0, cc, 1, keepdims=False)
    for_sibling = lax.dynamic_index_in_dim(blocks0, 1 - cc, 1, keepdims=False)
    from_sibling = send_to_sibling(for_sibling.reshape(4 * R, LANES), name="grads_to_sibling")
    pair = add_blocks([mine_first.reshape(4 * R, LANES), from_sibling], BF16, name="grads_pair_sum").reshape(4, R, LANES)
    others = chip_exchange(pair, name="grads_chip_exchange")
    own = lax.dynamic_index_in_dim(pair, chip, 0, keepdims=False)
    half0 = add_blocks([own, (others, 0), (others, 1), (others, 2)], F32, name="grads_chip_sum")
    half = jnp.concatenate([half0, half1])
    other = send_to_sibling(half, name="grads_pair_swap")
    by_core = [jnp.where(cc == 0, half, other), jnp.where(cc == 0, other, half)]

    def unflat(lo, hi, names):
        return _unflat_halves([a[lo:hi].reshape(-1) for a in by_core], shard_shapes, names)

    g_big_l = [{**unflat(0, R, mix_w), **unflat(R + R1, R + R1 + Rf, ffn_w)}, unflat(R, R + R1, BIG)]
    g_big = {n: jnp.stack([g_big_l[li][n] for li in range(depth)]) for n in BIG}

    small_list = [G[n].reshape(-1) for n in SMALL_REPL] + [G[n].reshape(-1) for n in SMALL_SHARD]
    small_list.append(jnp.sum(loss_lanes).reshape(1))
    sm = jnp.concatenate(small_list)
    n_small = sm.shape[0]
    sm_rows = -(-n_small // (16 * LANES)) * 16
    sm_all = allgather_direct(jnp.pad(sm, (0, sm_rows * LANES - n_small)).reshape(sm_rows, LANES), name="gather_small")
    sm_sum = add_blocks([(sm_all, d) for d in range(8)], F32, name="small_sum").reshape(-1)
    g_small, off = {}, 0
    for n in SMALL_REPL:
        g_small[n] = sm_sum[off:off + W[n].size].reshape(W[n].shape)
        off += W[n].size
    for n in SMALL_SHARD:
        full_shape = conv_full[n].shape
        sz = conv_full[n].size
        gfull = sm_sum[off:off + sz].reshape(full_shape)
        width = W[n].shape[2]
        g_small[n] = lax.dynamic_slice_in_dim(gfull, chip * width, width, axis=2)
        off += sz
    loss = sm_sum[off]

    grad_out, delta, new_m, new_v = {}, {}, {}, {}
    for n in BIG:
        shp = W[n].shape
        two_d = lambda a: a.reshape(shp[0] * shp[1], shp[2])
        d, nm_, nv_ = adamw(two_d(W[n]), two_d(g_big[n]), two_d(M[n]), two_d(V[n]), name="adamw_" + n)
        grad_out[n], delta[n], new_m[n], new_v[n] = g_big[n], d.reshape(shp), nm_.reshape(shp), nv_.reshape(shp)
    small_names = SMALL_REPL + SMALL_SHARD

    def flat_small(d):
        f = jnp.concatenate([d[n].reshape(-1) for n in small_names])
        rows = -(-f.shape[0] // (8 * LANES)) * 8
        return jnp.pad(f, (0, rows * LANES - f.shape[0])).reshape(rows, LANES)

    vpad = flat_small(V)
    d, nm_, nv_ = adamw(flat_small(W), flat_small(g_small), flat_small(M), vpad, name="adamw_small")
    off = 0
    for n in small_names:
        sz = W[n].size
        grad_out[n] = g_small[n]
        delta[n] = d.reshape(-1)[off:off + sz].reshape(W[n].shape)
        new_m[n] = nm_.reshape(-1)[off:off + sz].reshape(W[n].shape)
        new_v[n] = nv_.reshape(-1)[off:off + sz].reshape(W[n].shape)
        off += sz

    return (loss, grad_x, *[grad_out[n] for n in WEIGHTS], *[delta[n] for n in WEIGHTS],
            *[new_m[n] for n in WEIGHTS], *[new_v[n] for n in WEIGHTS])
```

```python
import functools
import math

import jax
import jax.numpy as jnp
from jax import lax
from jax.experimental import pallas as pl
from jax.experimental.pallas import tpu as pltpu

F32 = jnp.float32
BF16 = jnp.bfloat16

EPS = 1e-6
D_MODEL = 1024
SB_HEADS, SB_DIM = 4, 64
SSM_HEADS, SSM_DIM, SSM_GROUPS, SSM_STATE, SSM_CHUNK = 8, 64, 2, 64, 128
SSM_INNER = SSM_HEADS * SSM_DIM
SSM_CONV_DIM = SSM_INNER + 2 * SSM_GROUPS * SSM_STATE
MLA_HEADS, MLA_NOPE, MLA_ROPE, MLA_V = 4, 64, 32, 64
MLA_QK = MLA_NOPE + MLA_ROPE
MLA_SCALE = MLA_QK ** -0.5
ROPE_THETA = 10000.0
D_FF = 2816
IN_SPLITS = (768, 512, 768, 8, 256, 128, 32)

OFF_Z, OFF_XBC, OFF_CQ, OFF_CKV, OFF_DT, OFF_KRA, OFF_KRB = 0, 512, 1280, 1536, 1664, 1792, 1920
D_REST = 2048
LANES = 128

ADAM_LR, ADAM_B1, ADAM_B2, ADAM_EPS, ADAM_WD, ADAM_STEP = 0.001, 0.9, 0.999, 1e-08, 0.01, 10

V7X_VMEM_LIMIT = 48 * 1024 * 1024

NT = (((1,), (1,)), ((), ()))
TN = (((0,), (0,)), ((), ()))

BIG = ("w_in", "mla_w_uq", "mla_w_ukv", "w_out", "ffn_w_up", "ffn_w_down")
BIG_AXIS = {"w_in": 2, "mla_w_uq": 2, "mla_w_ukv": 2, "w_out": 1, "ffn_w_up": 2, "ffn_w_down": 1}
SMALL_REPL = ("mix_norm", "sb_out_norm", "ssm_conv_b", "ssm_dt_bias", "ssm_a_log", "ssm_d", "ssm_out_norm",
              "mla_q_norm", "mla_kv_norm", "mla_out_norm", "ffn_norm", "ffn_conv_b", "final_norm")
SMALL_SHARD = ("ssm_conv_w", "ffn_conv_w")
WEIGHTS = ("mix_norm", "w_in", "sb_out_norm", "ssm_conv_w", "ssm_conv_b", "ssm_dt_bias", "ssm_a_log", "ssm_d",
           "ssm_out_norm", "mla_q_norm", "mla_w_uq", "mla_kv_norm", "mla_w_ukv", "mla_out_norm", "w_out", "ffn_norm",
           "ffn_w_up", "ffn_conv_w", "ffn_conv_b", "ffn_w_down", "final_norm")


def _cparams(*sem):
    return pltpu.CompilerParams(dimension_semantics=sem if sem else None, vmem_limit_bytes=V7X_VMEM_LIMIT)


def _pick(n, target, mult=LANES):
    best = None
    for d in range(mult, min(n, target) + 1, mult):
        if n % d == 0:
            best = d
    return best or n


def _sigmoid(x):
    return 1.0 / (1.0 + jnp.exp(-x))


def _softplus(x):
    ax = jnp.where(x > 0, x, -x)
    return jnp.where(x > 0, x, 0.0) + jnp.log(1.0 + jnp.exp(-ax))


def _rms(x, g):
    return x * lax.rsqrt(jnp.mean(x * x, axis=-1, keepdims=True) + EPS) * g


def _raw_nn(a, b):
    return jnp.dot(a.astype(BF16), b.astype(BF16), preferred_element_type=F32)


def _raw_nt(a, b):
    return lax.dot_general(a.astype(BF16), b.astype(BF16), NT, preferred_element_type=F32)


def _raw_tn(a, b):
    return lax.dot_general(a.astype(BF16), b.astype(BF16), TN, preferred_element_type=F32)


@jax.custom_vjp
def mm_nn(a, b):
    return _raw_nn(a, b)


mm_nn.defvjp(lambda a, b: (_raw_nn(a, b), (a, b)),
             lambda r, ct: (_raw_nt(ct, r[1]), _raw_tn(r[0], ct)))


@jax.custom_vjp
def mm_nt(a, b):
    return _raw_nt(a, b)


mm_nt.defvjp(lambda a, b: (_raw_nt(a, b), (a, b)),
             lambda r, ct: (_raw_nn(ct, r[1]), _raw_tn(ct, r[0])))


@jax.custom_vjp
def mm_tn(a, b):
    return _raw_tn(a, b)


mm_tn.defvjp(lambda a, b: (_raw_tn(a, b), (a, b)),
             lambda r, ct: (_raw_nt(r[1], ct), _raw_nn(r[0], ct)))


def _tri_dot(tri, x, terms=3):
    parts = []
    r = x
    for t in range(terms):
        xt = r.astype(BF16)
        parts.append(xt)
        if t + 1 < terms:
            r = r - xt.astype(F32)
    return jnp.dot(jnp.concatenate([tri] * terms, axis=1), jnp.concatenate(parts, axis=0),
                   preferred_element_type=F32)


def _tri(n, cmp):
    r = lax.broadcasted_iota(jnp.int32, (n, n), 0)
    c = lax.broadcasted_iota(jnp.int32, (n, n), 1)
    return cmp(r, c).astype(BF16)


@jax.custom_vjp
def csum_rows(x):
    return _tri_dot(_tri(x.shape[0], lambda r, c: r >= c), x)


csum_rows.defvjp(lambda x: (csum_rows(x), None),
                 lambda _, ct: (_tri_dot(_tri(ct.shape[0], lambda r, c: r <= c), ct),))


def matmul(a, b, *, name, out_dtype=F32, ta=False, tb=False, b_k0=0, residual=None):
    if ta:
        K, M = a.shape
    else:
        M, K = a.shape
    N = b.shape[0] if tb else b.shape[1]
    tm = _pick(M, 1408)
    tn = _pick(N, 1408)
    tk = _pick(K, 1408)
    nk = K // tk
    kb0 = b_k0 // tk
    assert b_k0 % tk == 0 and (tb or b_k0 == 0)
    has_res = residual is not None

    def body(*refs):
        if has_res:
            a_ref, b_ref, r_ref, o_ref, acc = refs
        else:
            a_ref, b_ref, o_ref, acc = refs
        k = pl.program_id(2)

        @pl.when(k == 0)
        def _():
            acc[...] = jnp.zeros_like(acc)

        av = a_ref[...].astype(BF16)
        bv = b_ref[...].astype(BF16)
        if ta:
            acc[...] += lax.dot_general(av, bv, TN, preferred_element_type=F32)
        elif tb:
            acc[...] += lax.dot_general(av, bv, NT, preferred_element_type=F32)
        else:
            acc[...] += jnp.dot(av, bv, preferred_element_type=F32)

        @pl.when(k == nk - 1)
        def _():
            r = acc[...]
            if has_res:
                r = r + r_ref[...].astype(F32)
            o_ref[...] = r.astype(o_ref.dtype)

    a_spec = pl.BlockSpec((tk, tm), lambda i, j, k: (k, i)) if ta else pl.BlockSpec((tm, tk), lambda i, j, k: (i, k))
    b_spec = pl.BlockSpec((tn, tk), lambda i, j, k: (j, kb0 + k)) if tb else pl.BlockSpec((tk, tn), lambda i, j, k: (k, j))
    in_specs = [a_spec, b_spec]
    args = [a, b]
    if has_res:
        in_specs.append(pl.BlockSpec((tm, tn), lambda i, j, k: (i, j)))
        args.append(residual)
    return pl.pallas_call(
        body, name=name, grid=(M // tm, N // tn, nk),
        in_specs=in_specs, out_specs=pl.BlockSpec((tm, tn), lambda i, j, k: (i, j)),
        out_shape=jax.ShapeDtypeStruct((M, N), out_dtype),
        scratch_shapes=[pltpu.VMEM((tm, tn), F32)],
        compiler_params=_cparams("parallel", "parallel", "arbitrary"),
    )(*args)


def _row_spec(entry, tl):
    if isinstance(entry, tuple):
        arr, width, cb = entry
        return arr, pl.BlockSpec((tl, width), lambda i, cb=cb: (i, cb))
    return entry, pl.BlockSpec((tl, entry.shape[1]), lambda i: (i, 0))


def _rows_T(entry):
    return (entry[0] if isinstance(entry, tuple) else entry).shape[0]


def rowwise(fn, rows, params, outs, *, name, tile=512, post=None):
    T = _rows_T(rows[0])
    tl = min(T, tile)
    nr, npar = len(rows), len(params)

    def body(*refs):
        r = [ref[...].astype(F32) for ref in refs[:nr]]
        p = [ref[...].astype(F32) for ref in refs[nr:nr + npar]]
        res = fn(*r, *p)
        if post is not None:
            res = post(*res)
        for o_ref, val in zip(refs[nr + npar:], res):
            o_ref[...] = val.astype(o_ref.dtype)

    arrs, specs = [], []
    for e in rows:
        a, s = _row_spec(e, tl)
        arrs.append(a)
        specs.append(s)
    for p in params:
        arrs.append(p)
        specs.append(pl.BlockSpec(p.shape, lambda i: (0, 0)))
    res = pl.pallas_call(
        body, name=name, grid=(T // tl,), in_specs=specs,
        out_specs=[pl.BlockSpec((tl, c), lambda i: (i, 0)) for c, _ in outs],
        out_shape=[jax.ShapeDtypeStruct((T, c), dt) for c, dt in outs],
        compiler_params=_cparams("parallel"),
    )(*arrs)
    return res


def rowwise_bwd(fn, rows, nd_rows, params, cts, grad_dtypes, *, name, tile=512, pre_ct=None, add0=None):
    T = _rows_T(rows[0])
    tl = min(T, tile)
    nr, nn, npar, nc = len(rows), len(nd_rows), len(params), len(cts)
    has_add = add0 is not None

    def body(*refs):
        pos = 0
        r = [ref[...].astype(F32) for ref in refs[pos:pos + nr]]
        pos += nr
        nd = [ref[...].astype(F32) for ref in refs[pos:pos + nn]]
        pos += nn
        p = [ref[...].astype(F32) for ref in refs[pos:pos + npar]]
        pos += npar
        c = [ref[...].astype(F32) for ref in refs[pos:pos + nc]]
        pos += nc
        if has_add:
            addv = refs[pos][...].astype(F32)
            pos += 1
        rg_refs = refs[pos:pos + nr]
        pg_refs = refs[pos + nr:pos + nr + npar]
        if pre_ct is not None:
            c = list(pre_ct(*c))
        _, vjp = jax.vjp(lambda *a: fn(*a[:nr], *nd, *a[nr:]), *r, *p)
        g = vjp(tuple(c))
        for j, ref in enumerate(rg_refs):
            val = g[j]
            if has_add and j == 0:
                val = val + addv
            ref[...] = val.astype(ref.dtype)
        if npar:
            @pl.when(pl.program_id(0) == 0)
            def _():
                for ref in pg_refs:
                    ref[...] = jnp.zeros_like(ref)
            for j, ref in enumerate(pg_refs):
                ref[...] += g[nr + j]

    arrs, specs = [], []
    widths = []
    for e in list(rows) + list(nd_rows):
        a, s = _row_spec(e, tl)
        arrs.append(a)
        specs.append(s)
        widths.append(s.block_shape[1])
    for p in params:
        arrs.append(p)
        specs.append(pl.BlockSpec(p.shape, lambda i: (0, 0)))
    for e in cts:
        a, s = _row_spec(e, tl)
        arrs.append(a)
        specs.append(s)
    if has_add:
        a, s = _row_spec(add0, tl)
        arrs.append(a)
        specs.append(s)
    out_specs = [pl.BlockSpec((tl, widths[j]), lambda i: (i, 0)) for j in range(nr)]
    out_shape = [jax.ShapeDtypeStruct((T, widths[j]), grad_dtypes[j]) for j in range(nr)]
    out_specs += [pl.BlockSpec(p.shape, lambda i: (0, 0)) for p in params]
    out_shape += [jax.ShapeDtypeStruct(p.shape, F32) for p in params]
    res = pl.pallas_call(
        body, name=name, grid=(T // tl,), in_specs=specs, out_specs=out_specs, out_shape=out_shape,
        compiler_params=_cparams("arbitrary"),
    )(*arrs)
    return list(res[:nr]), list(res[nr:])


def rms_fn(h, g):
    return (_rms(h, g),)


def merge_fn(ysb, yssm, z, ymla, g_sb, g_ssm, g_mla):
    ya = _rms(ysb, g_sb)
    yb = _rms(yssm * (z * _sigmoid(z)), g_ssm)
    yc = _rms(ymla, g_mla)
    return ya, yb, yc


def mla_prep_fn(cq, ckv, kra, krb, cos, sin, qn, kvn, wqp, wqr, wkp, wvp):
    cos4 = jnp.concatenate([cos] * MLA_HEADS, axis=1)
    sin4 = jnp.concatenate([sin] * MLA_HEADS, axis=1)
    nq = _rms(cq, qn)
    q = (mm_nn(nq, wqp) * cos4 + mm_nn(nq, wqr) * sin4) * MLA_SCALE
    nkv = _rms(ckv, kvn)
    kpe = kra * cos + krb * sin
    k = mm_nn(nkv, wkp) + jnp.concatenate([kpe] * MLA_HEADS, axis=1)
    v = mm_nn(nkv, wvp)
    return q, k, v


HALO = 8


def _prev_halo_spec(tl, tc, col_of):
    return pl.BlockSpec((HALO, tc), lambda i, j: (jnp.maximum(i * (tl // HALO) - 1, 0), col_of(j)))


def _fill_prev(buf, x_ref, halo_ref, i):
    buf[0:HALO, :] = jnp.where(i > 0, halo_ref[...].astype(F32), 0.0)
    buf[HALO:, :] = x_ref[...].astype(F32)


def _conv_from(buf, w_ref, b_ref, K, tl):
    acc = b_ref[...].astype(F32) + jnp.zeros((tl, buf.shape[1]), F32)
    for k in range(K):
        acc = acc + buf[pl.ds(HALO - (K - 1 - k), tl), :] * w_ref[k:k + 1, :].astype(F32)
    return acc


def ssm_conv_act(proj, w, b, *, name, tile=512, tc=256):
    T = proj.shape[0]
    K, C = w.shape
    tl = min(T, tile)
    c0 = OFF_XBC // tc
    nb = C // tc
    hd = SSM_DIM
    per = tc // hd

    def body(*refs):
        xs, halos = refs[0:nb], refs[nb:2 * nb]
        w_ref, b_ref = refs[2 * nb:2 * nb + 2]
        x_out, b_out, c_out = refs[2 * nb + 2:2 * nb + 5]
        bufs = refs[2 * nb + 5:]
        for j in range(nb):
            cols = slice(j * tc, (j + 1) * tc)
            _fill_prev(bufs[j], xs[j], halos[j], pl.program_id(0))
            u = b_ref[:, cols].astype(F32) + jnp.zeros((tl, tc), F32)
            for k in range(K):
                u = u + bufs[j][pl.ds(HALO - (K - 1 - k), tl), :] * w_ref[k:k + 1, cols].astype(F32)
            act = u * _sigmoid(u)
            for hh in range(per):
                piece = act[:, hh * hd:(hh + 1) * hd]
                head = j * per + hh
                if head < SSM_HEADS:
                    x_out[head] = piece
                elif head < SSM_HEADS + SSM_GROUPS:
                    b_out[head - SSM_HEADS] = piece
                else:
                    c_out[head - SSM_HEADS - SSM_GROUPS] = piece

    in_specs = ([pl.BlockSpec((tl, tc), lambda i, j=j: (i, c0 + j)) for j in range(nb)]
                + [pl.BlockSpec((HALO, tc), lambda i, j=j: (jnp.maximum(i * (tl // HALO) - 1, 0), c0 + j)) for j in range(nb)]
                + [pl.BlockSpec((K, C), lambda i: (0, 0)), pl.BlockSpec((1, C), lambda i: (0, 0))])
    return pl.pallas_call(
        body, name=name, grid=(T // tl,), in_specs=in_specs,
        out_specs=[pl.BlockSpec((SSM_HEADS, tl, hd), lambda i: (0, i, 0)),
                   pl.BlockSpec((SSM_GROUPS, tl, hd), lambda i: (0, i, 0)),
                   pl.BlockSpec((SSM_GROUPS, tl, hd), lambda i: (0, i, 0))],
        out_shape=[jax.ShapeDtypeStruct((SSM_HEADS, T, hd), F32), jax.ShapeDtypeStruct((SSM_GROUPS, T, hd), F32),
                   jax.ShapeDtypeStruct((SSM_GROUPS, T, hd), F32)],
        scratch_shapes=[pltpu.VMEM((tl + HALO, tc), F32)] * nb,
        compiler_params=_cparams("parallel"),
    )(*([proj] * (2 * nb)), w, b)


def ssm_conv_bwd_a(proj, w, b, d_out, *, name, tile=512, tc=256):
    T = proj.shape[0]
    K, C = w.shape
    tl = min(T, tile)
    c0 = OFF_XBC // tc

    def body(x_ref, halo_ref, w_ref, b_ref, d_ref, o_ref, buf):
        _fill_prev(buf, x_ref, halo_ref, pl.program_id(0))
        u = _conv_from(buf, w_ref, b_ref, K, tl)
        s = _sigmoid(u)
        o_ref[...] = d_ref[...].astype(F32) * (s * (1.0 + u * (1.0 - s)))

    return pl.pallas_call(
        body, name=name, grid=(T // tl, C // tc),
        in_specs=[pl.BlockSpec((tl, tc), lambda i, j: (i, c0 + j)), _prev_halo_spec(tl, tc, lambda j: c0 + j),
                  pl.BlockSpec((K, tc), lambda i, j: (0, j)), pl.BlockSpec((1, tc), lambda i, j: (0, j)),
                  pl.BlockSpec((tl, tc), lambda i, j: (i, j))],
        out_specs=pl.BlockSpec((tl, tc), lambda i, j: (i, j)),
        out_shape=jax.ShapeDtypeStruct((T, C), F32),
        scratch_shapes=[pltpu.VMEM((tl + HALO, tc), F32)],
        compiler_params=_cparams("parallel", "parallel"),
    )(proj, proj, w, b, d_out)


def ffn_act(up, w, b, *, name, tile=512, tc=1408):
    T = up.shape[0]
    K = w.shape[0]
    tl = min(T, tile)
    nj = D_FF // tc

    def body(xg_ref, hg_ref, xv_ref, hv_ref, wg_ref, wv_ref, bg_ref, bv_ref, o_ref, bufg, bufv):
        i = pl.program_id(0)
        _fill_prev(bufg, xg_ref, hg_ref, i)
        _fill_prev(bufv, xv_ref, hv_ref, i)
        gate = _conv_from(bufg, wg_ref, bg_ref, K, tl)
        val = _conv_from(bufv, wv_ref, bv_ref, K, tl)
        o_ref[...] = (gate * _sigmoid(gate) * val).astype(o_ref.dtype)

    return pl.pallas_call(
        body, name=name, grid=(T // tl, nj),
        in_specs=[pl.BlockSpec((tl, tc), lambda i, j: (i, j)), _prev_halo_spec(tl, tc, lambda j: j),
                  pl.BlockSpec((tl, tc), lambda i, j: (i, nj + j)), _prev_halo_spec(tl, tc, lambda j: nj + j),
                  pl.BlockSpec((K, tc), lambda i, j: (0, j)), pl.BlockSpec((K, tc), lambda i, j: (0, nj + j)),
                  pl.BlockSpec((1, tc), lambda i, j: (0, j)), pl.BlockSpec((1, tc), lambda i, j: (0, nj + j))],
        out_specs=pl.BlockSpec((tl, tc), lambda i, j: (i, j)),
        out_shape=jax.ShapeDtypeStruct((T, D_FF), BF16),
        scratch_shapes=[pltpu.VMEM((tl + HALO, tc), F32), pltpu.VMEM((tl + HALO, tc), F32)],
        compiler_params=_cparams("parallel", "parallel"),
    )(up, up, up, up, w, w, b, b)


def ffn_bwd_fused(up, w, b, d_act, *, name, tile=1024, tc=256, exchange=None):
    T = up.shape[0]
    K = w.shape[0]
    tl = min(T, tile)
    nj = D_FF // tc
    nblk = T // HALO
    ext = tl + HALO

    def body(xg, hgp, hgn, xv, hvp, hvn, wg, wv, bg, bv, d, dn, og, ov, dwg, dwv, dbg, dbv, bufg, bufv, dgb, dvb):
        i = pl.program_id(1)
        last = pl.num_programs(1) - 1

        def fill(buf, x_ref, prev_ref, next_ref):
            buf[0:HALO, :] = jnp.where(i > 0, prev_ref[...].astype(F32), 0.0)
            buf[HALO:HALO + tl, :] = x_ref[...].astype(F32)
            buf[HALO + tl:, :] = jnp.where(i < last, next_ref[...].astype(F32), 0.0)

        def conv_ext(buf, w_ref, b_ref):
            acc = b_ref[...].astype(F32) + jnp.zeros((ext, tc), F32)
            for k in range(K):
                acc = acc + buf[pl.ds(HALO - (K - 1 - k), ext), :] * w_ref[k:k + 1, :].astype(F32)
            return acc

        fill(bufg, xg, hgp, hgn)
        fill(bufv, xv, hvp, hvn)
        gate = conv_ext(bufg, wg, bg)
        val = conv_ext(bufv, wv, bv)
        dd = jnp.concatenate([d[...].astype(F32), jnp.where(i < last, dn[...].astype(F32)[0:HALO], 0.0)], axis=0)
        s = _sigmoid(gate)
        dgb[...] = dd * val * (s * (1.0 + gate * (1.0 - s)))
        dvb[...] = dd * (gate * s)

        @pl.when(i == 0)
        def _():
            for ref in (dwg, dwv, dbg, dbv):
                ref[...] = jnp.zeros_like(ref)

        for dbuf, xbuf, w_ref, o_ref, dw_ref, db_ref in ((dgb, bufg, wg, og, dwg, dbg), (dvb, bufv, wv, ov, dwv, dbv)):
            cur = dbuf[0:tl, :]
            dx = jnp.zeros((tl, tc), F32)
            for k in range(K):
                sft = K - 1 - k
                dx = dx + dbuf[pl.ds(sft, tl), :] * w_ref[k:k + 1, :].astype(F32)
                dw_ref[k:k + 1, :] += jnp.sum(cur * xbuf[pl.ds(HALO - sft, tl), :], axis=0, keepdims=True)
            db_ref[...] += jnp.sum(cur, axis=0, keepdims=True)
            o_ref[...] = dx.astype(o_ref.dtype)

    prev = lambda i: jnp.maximum(i * (tl // HALO) - 1, 0)
    nxt = lambda i: jnp.minimum((i + 1) * (tl // HALO), nblk - 1)

    def x_specs(col):
        return [pl.BlockSpec((tl, tc), lambda j, i: (i, col(j))), pl.BlockSpec((HALO, tc), lambda j, i: (prev(i), col(j))),
                pl.BlockSpec((HALO, tc), lambda j, i: (nxt(i), col(j)))]

    gcol, vcol = (lambda j: j), (lambda j: nj + j)
    in_specs = (x_specs(gcol) + x_specs(vcol)
                + [pl.BlockSpec((K, tc), lambda j, i: (0, j)), pl.BlockSpec((K, tc), lambda j, i: (0, nj + j)),
                   pl.BlockSpec((1, tc), lambda j, i: (0, j)), pl.BlockSpec((1, tc), lambda j, i: (0, nj + j)),
                   pl.BlockSpec((tl, tc), lambda j, i: (i, j)),
                   pl.BlockSpec((2 * HALO, tc), lambda j, i: (jnp.minimum((i + 1) * (tl // (2 * HALO)), nblk // 2 - 1), j))])
    row_out = pl.BlockSpec((tl, tc), lambda j, i: (i, j))
    w_out = pl.BlockSpec((K, tc), lambda j, i: (0, j))
    b_out = pl.BlockSpec((1, tc), lambda j, i: (0, j))
    return _call_with_exchange(
        body, exchange, name=name, grid=(nj, T // tl), in_specs=in_specs,
        out_specs=[row_out, row_out, w_out, w_out, b_out, b_out],
        out_shape=[jax.ShapeDtypeStruct((T, D_FF), BF16)] * 2 + [jax.ShapeDtypeStruct((K, D_FF), F32)] * 2
        + [jax.ShapeDtypeStruct((1, D_FF), F32)] * 2,
        scratch_shapes=[pltpu.VMEM((tl + 2 * HALO, tc), F32)] * 2 + [pltpu.VMEM((ext, tc), F32)] * 2,
        args=(up, up, up, up, up, up, w, w, b, b, d_act, d_act))


def conv_bwd_b(du, x, x_off, w, *, name, out_dtype, tile=512, tc=256):
    T, C = du.shape
    K = w.shape[0]
    tl = min(T, tile)
    c0 = x_off // tc
    nblk = T // HALO

    def body(du_ref, nx_ref, x_ref, w_ref, dx_ref, dw_ref, db_ref, dbuf):
        i = pl.program_id(1)
        last = pl.num_programs(1) - 1
        d = du_ref[...].astype(F32)
        dbuf[0:tl, :] = d
        dbuf[tl:, :] = jnp.where(i < last, nx_ref[...].astype(F32), 0.0)

        @pl.when(i == 0)
        def _():
            dw_ref[...] = jnp.zeros_like(dw_ref)
            db_ref[...] = jnp.zeros_like(db_ref)

        xin = x_ref[...].astype(F32)
        dx = jnp.zeros((tl, tc), F32)
        for k in range(K):
            s = K - 1 - k
            shifted = dbuf[pl.ds(s, tl), :]
            dx = dx + shifted * w_ref[k:k + 1, :].astype(F32)
            dw_ref[k:k + 1, :] += jnp.sum(shifted * xin, axis=0, keepdims=True)
        db_ref[...] += jnp.sum(d, axis=0, keepdims=True)
        dx_ref[...] = dx.astype(dx_ref.dtype)

    return pl.pallas_call(
        body, name=name, grid=(C // tc, T // tl),
        in_specs=[pl.BlockSpec((tl, tc), lambda j, i: (i, j)),
                  pl.BlockSpec((HALO, tc), lambda j, i: (jnp.minimum((i + 1) * (tl // HALO), nblk - 1), j)),
                  pl.BlockSpec((tl, tc), lambda j, i: (i, c0 + j)),
                  pl.BlockSpec((K, tc), lambda j, i: (0, j))],
        out_specs=[pl.BlockSpec((tl, tc), lambda j, i: (i, j)), pl.BlockSpec((K, tc), lambda j, i: (0, j)),
                   pl.BlockSpec((1, tc), lambda j, i: (0, j))],
        out_shape=[jax.ShapeDtypeStruct((T, C), out_dtype), jax.ShapeDtypeStruct((K, C), F32),
                   jax.ShapeDtypeStruct((1, C), F32)],
        scratch_shapes=[pltpu.VMEM((tl + HALO, tc), F32)],
        compiler_params=_cparams("parallel", "arbitrary"),
    )(du, du, x, w)


SB_QUERIES = 1024


def _attn_tiles(T, keys=256, queries=1024):
    return min(T, queries), min(T, keys)


def _after_diag(keys, queries, strict):
    d = lax.broadcasted_iota(jnp.int32, (keys, queries), 1) - lax.broadcasted_iota(jnp.int32, (keys, queries), 0)
    return d > 0 if strict else d >= 0


def _log_gates(z):
    l1p = jnp.log(1.0 + jnp.exp(-jnp.abs(z)))
    a = jnp.minimum(z, 0.0) - l1p
    return a, a - z


def _causal_sweep(i, tq, tk, block, descending, keep_going=None, first_block=None):
    nb = tq // tk
    n_full = i * nb

    def band():
        order = reversed(range(nb)) if descending else range(nb)
        for bb in order:
            block(pl.multiple_of(i * tq + bb * tk, tk), bb * tk, True)

    def full():
        if descending and keep_going is not None:
            def step(j):
                block(pl.multiple_of((n_full - 1 - j) * tk, tk), 0, False)
                return j + 1
            done = lax.while_loop(lambda j: jnp.logical_and(j < n_full, keep_going()), step, jnp.int32(0))
            return n_full - done

        def step(j, c):
            kb = (n_full - 1 - j) if descending else j
            block(pl.multiple_of(kb * tk, tk), 0, False)
            return c
        lax.fori_loop(0 if first_block is None else first_block, n_full, step, 0)
        return None

    if descending:
        band()
        return full()
    full()
    band()
    return None


def sb_fwd(q, k, v, *, name):
    H, dh, T = q.shape
    tq, tk = _attn_tiles(T, queries=SB_QUERIES)

    def body(q_ref, k_ref, v_ref, y_ref, bt_ref, first_ref, acc, run):
        acc[...] = jnp.zeros_like(acc)
        run[...] = jnp.zeros_like(run)
        u_after = _tri(tk, lambda r, c: r < c)

        def block(k0, r0, masked):
            kb = k_ref[pl.ds(k0, tk), :]
            vb = v_ref[pl.ds(k0, tk), :]
            z = jnp.dot(kb, q_ref[:, r0:], preferred_element_type=F32)
            a, b = _log_gates(z)
            if masked:
                valid = _after_diag(tk, tq - r0, True)
                b = jnp.where(valid, b, 0.0)
            w = jnp.exp(a + _tri_dot(u_after, b, 2) + run[:, r0:])
            if masked:
                w = jnp.where(valid, w, 0.0)
            acc[:, r0:] += lax.dot_general(vb, w.astype(BF16), TN, preferred_element_type=F32)
            run[:, r0:] += jnp.sum(b, axis=0, keepdims=True)

        first = _causal_sweep(pl.program_id(1), tq, tk, block, descending=True,
                              keep_going=lambda: jnp.max(run[...]) >= SB_ZERO_BELOW)
        y_ref[...] = acc[...]
        bt_ref[...] = run[...]
        first_ref[...] = jnp.zeros(first_ref.shape, F32) + first.astype(F32)

    return pl.pallas_call(
        body, name=name, grid=(H, T // tq),
        in_specs=[pl.BlockSpec((None, dh, tq), lambda h, i: (h, 0, i)),
                  pl.BlockSpec((None, T, dh), lambda h, i: (h, 0, 0)),
                  pl.BlockSpec((None, T, dh), lambda h, i: (h, 0, 0))],
        out_specs=[pl.BlockSpec((None, dh, tq), lambda h, i: (h, 0, i)),
                   pl.BlockSpec((None, 1, tq), lambda h, i: (h, 0, i)),
                   pl.BlockSpec((None, None, HALO, LANES), lambda h, i: (h, i, 0, 0))],
        out_shape=[jax.ShapeDtypeStruct((H, dh, T), F32), jax.ShapeDtypeStruct((H, 1, T), F32),
                   jax.ShapeDtypeStruct((H, T // tq, HALO, LANES), F32)],
        scratch_shapes=[pltpu.VMEM((dh, tq), F32), pltpu.VMEM((1, tq), F32)],
        compiler_params=_cparams("parallel", "parallel"),
    )(q, k, v)


def sb_bwd(q, k, v, dy, btot, first, *, name, q_scale):
    H, dh, T = q.shape
    tq, tk = _attn_tiles(T, queries=SB_QUERIES)

    def body(q_ref, k_ref, v_ref, dy_ref, bt_ref, first_ref, dq_ref, dk_ref, dv_ref, dq, pb, pg, dyb):
        @pl.when(pl.program_id(1) == 0)
        def _():
            dk_ref[...] = jnp.zeros_like(dk_ref)
            dv_ref[...] = jnp.zeros_like(dv_ref)

        dq[...] = jnp.zeros_like(dq)
        pb[...] = jnp.zeros_like(pb)
        pg[...] = jnp.zeros_like(pg)
        dyb[...] = dy_ref[...].astype(BF16)
        u_upto = _tri(tk, lambda r, c: r >= c)
        u_before = _tri(tk, lambda r, c: r > c)

        def block(k0, r0, masked):
            kb = k_ref[pl.ds(k0, tk), :]
            vb = v_ref[pl.ds(k0, tk), :]
            qv = q_ref[:, r0:]
            dyv = dyb[:, r0:]
            z = jnp.dot(kb, qv, preferred_element_type=F32)
            a, b = _log_gates(z)
            if masked:
                valid = _after_diag(tk, tq - r0, True)
                b = jnp.where(valid, b, 0.0)
            w = jnp.exp(a + (bt_ref[:, r0:] - pb[:, r0:] - _tri_dot(u_upto, b, 2)))
            if masked:
                w = jnp.where(valid, w, 0.0)
            g = w * jnp.dot(vb, dyv, preferred_element_type=F32)
            dz = g - jnp.exp(a) * (g + pg[:, r0:] + _tri_dot(u_before, g, 2))
            if masked:
                dz = jnp.where(valid, dz, 0.0)
            dz = dz.astype(BF16)
            dq[:, r0:] += lax.dot_general(kb, dz, TN, preferred_element_type=F32)
            dk_ref[pl.ds(k0, tk), :] += lax.dot_general(dz, qv, NT, preferred_element_type=F32)
            dv_ref[pl.ds(k0, tk), :] += lax.dot_general(w.astype(BF16), dyv, NT, preferred_element_type=F32)
            pb[:, r0:] += jnp.sum(b, axis=0, keepdims=True)
            pg[:, r0:] += jnp.sum(g, axis=0, keepdims=True)

        i = pl.program_id(1)
        first = jnp.clip(jnp.max(first_ref[...]).astype(jnp.int32), 0, i * (tq // tk))
        _causal_sweep(i, tq, tk, block, descending=False, first_block=first)
        dq_ref[...] = dq[...] * q_scale

    return pl.pallas_call(
        body, name=name, grid=(H, T // tq),
        in_specs=[pl.BlockSpec((None, dh, tq), lambda h, i: (h, 0, i)),
                  pl.BlockSpec((None, T, dh), lambda h, i: (h, 0, 0)),
                  pl.BlockSpec((None, T, dh), lambda h, i: (h, 0, 0)),
                  pl.BlockSpec((None, dh, tq), lambda h, i: (h, 0, i)),
                  pl.BlockSpec((None, 1, tq), lambda h, i: (h, 0, i)),
                  pl.BlockSpec((None, None, HALO, LANES), lambda h, i: (h, i, 0, 0))],
        out_specs=[pl.BlockSpec((None, dh, tq), lambda h, i: (h, 0, i)),
                   pl.BlockSpec((None, T, dh), lambda h, i: (h, 0, 0)),
                   pl.BlockSpec((None, T, dh), lambda h, i: (h, 0, 0))],
        out_shape=[jax.ShapeDtypeStruct((H, dh, T), F32), jax.ShapeDtypeStruct((H, T, dh), F32),
                   jax.ShapeDtypeStruct((H, T, dh), F32)],
        scratch_shapes=[pltpu.VMEM((dh, tq), F32), pltpu.VMEM((1, tq), F32), pltpu.VMEM((1, tq), F32),
                        pltpu.VMEM((dh, tq), BF16)],
        compiler_params=_cparams("parallel", "arbitrary"),
    )(q, k, v, dy, btot, first)


NEG = -1e30
SB_ZERO_BELOW = -105.0
MLA_KEYS = 512


def _call_with_exchange(body, exchange, *, name, grid, in_specs, out_specs, out_shape, scratch_shapes, args):
    if exchange is None:
        return pl.pallas_call(body, name=name, grid=grid, in_specs=in_specs, out_specs=out_specs, out_shape=out_shape,
                              scratch_shapes=scratch_shapes, compiler_params=_cparams("parallel", "arbitrary"))(*args)
    kind, src = exchange
    n_in, n_out, n_scr = len(in_specs), len(out_specs), len(scratch_shapes)
    R = src.shape[-2]

    def wrapped(*refs):
        ins, src_ref = refs[:n_in], refs[n_in]
        outs, xout = refs[n_in + 1:n_in + 1 + n_out], refs[n_in + 1 + n_out]
        scr = refs[n_in + 2 + n_out:n_in + 2 + n_out + n_scr]
        start, finish = _direct_exchange(kind, src_ref, xout, *refs[-3:])
        step = pl.program_id(0) * pl.num_programs(1) + pl.program_id(1)
        pl.when(step == 0)(start)
        body(*ins, *outs, *scr)
        pl.when(step == pl.num_programs(0) * pl.num_programs(1) - 1)(finish)

    return pl.pallas_call(
        wrapped, name=name, grid=grid, in_specs=list(in_specs) + [HBM_SPEC], out_specs=list(out_specs) + [HBM_SPEC],
        out_shape=list(out_shape) + [jax.ShapeDtypeStruct((8, R, LANES), src.dtype)],
        scratch_shapes=list(scratch_shapes) + [pltpu.SemaphoreType.DMA((7,)), pltpu.SemaphoreType.DMA((7,)),
                                               pltpu.SemaphoreType.DMA],
        compiler_params=_cparams("arbitrary", "arbitrary"))(*args, src)


def mla_fwd(q, k, v, *, name, exchange=None):
    H, dk, T = q.shape
    dv = v.shape[1] // H
    tq, tk = _attn_tiles(T, MLA_KEYS)

    def body(q_ref, k_ref, v_ref, o_ref, l_ref, acc, m_s, l_s):
        acc[...] = jnp.zeros_like(acc)
        m_s[...] = jnp.full_like(m_s, NEG)
        l_s[...] = jnp.zeros_like(l_s)

        def block(k0, r0, masked):
            kb = k_ref[pl.ds(k0, tk), :]
            vb = v_ref[pl.ds(k0, tk), :]
            s = jnp.dot(kb, q_ref[:, r0:], preferred_element_type=F32)
            if masked:
                s = jnp.where(_after_diag(tk, tq - r0, False), s, NEG)
            m = m_s[:, r0:]
            m_new = jnp.maximum(m, jnp.max(s, axis=0, keepdims=True))
            p = jnp.exp(s - m_new)
            alpha = jnp.exp(m - m_new)
            l_s[:, r0:] = alpha * l_s[:, r0:] + jnp.sum(p, axis=0, keepdims=True)
            acc[:, r0:] = alpha * acc[:, r0:] + lax.dot_general(vb, p.astype(BF16), TN, preferred_element_type=F32)
            m_s[:, r0:] = m_new

        _causal_sweep(pl.program_id(1), tq, tk, block, descending=False)
        o_ref[...] = acc[...] / l_s[...]
        l_ref[...] = m_s[...] + jnp.log(l_s[...])

    return _call_with_exchange(
        body, exchange, name=name, grid=(H, T // tq),
        in_specs=[pl.BlockSpec((None, dk, tq), lambda h, i: (h, 0, i)),
                  pl.BlockSpec((T, dk), lambda h, i: (0, h)),
                  pl.BlockSpec((T, dv), lambda h, i: (0, h))],
        out_specs=[pl.BlockSpec((None, dv, tq), lambda h, i: (h, 0, i)),
                   pl.BlockSpec((None, 1, tq), lambda h, i: (h, 0, i))],
        out_shape=[jax.ShapeDtypeStruct((H, dv, T), F32), jax.ShapeDtypeStruct((H, 1, T), F32)],
        scratch_shapes=[pltpu.VMEM((dv, tq), F32), pltpu.VMEM((1, tq), F32), pltpu.VMEM((1, tq), F32)],
        args=(q, k, v))


def mla_bwd(q, k, v, do, o, lse, *, name, exchange=None):
    H, dk, T = q.shape
    dv = v.shape[1] // H
    tq, tk = _attn_tiles(T, MLA_KEYS)

    def body(q_ref, k_ref, v_ref, do_ref, o_ref, l_ref, dq_ref, dk_ref, dv_ref, dq, delta, dob):
        @pl.when(pl.program_id(1) == 0)
        def _():
            dk_ref[...] = jnp.zeros_like(dk_ref)
            dv_ref[...] = jnp.zeros_like(dv_ref)

        dq[...] = jnp.zeros_like(dq)
        dov = do_ref[...].astype(F32)
        dob[...] = dov.astype(BF16)
        delta[...] = jnp.sum(dov * o_ref[...], axis=0, keepdims=True)

        def block(k0, r0, masked):
            kb = k_ref[pl.ds(k0, tk), :]
            vb = v_ref[pl.ds(k0, tk), :]
            qv = q_ref[:, r0:]
            dov_b = dob[:, r0:]
            s = jnp.dot(kb, qv, preferred_element_type=F32)
            p = jnp.exp(s - l_ref[:, r0:])
            if masked:
                p = jnp.where(_after_diag(tk, tq - r0, False), p, 0.0)
            dp = jnp.dot(vb, dov_b, preferred_element_type=F32)
            ds = (p * (dp - delta[:, r0:])).astype(BF16)
            dq[:, r0:] += lax.dot_general(kb, ds, TN, preferred_element_type=F32)
            dk_ref[pl.ds(k0, tk), :] += lax.dot_general(ds, qv, NT, preferred_element_type=F32)
            dv_ref[pl.ds(k0, tk), :] += lax.dot_general(p.astype(BF16), dov_b, NT, preferred_element_type=F32)

        _causal_sweep(pl.program_id(1), tq, tk, block, descending=False)
        dq_ref[...] = dq[...]

    return _call_with_exchange(
        body, exchange, name=name, grid=(H, T // tq),
        in_specs=[pl.BlockSpec((None, dk, tq), lambda h, i: (h, 0, i)),
                  pl.BlockSpec((T, dk), lambda h, i: (0, h)),
                  pl.BlockSpec((T, dv), lambda h, i: (0, h)),
                  pl.BlockSpec((None, dv, tq), lambda h, i: (h, 0, i)),
                  pl.BlockSpec((None, dv, tq), lambda h, i: (h, 0, i)),
                  pl.BlockSpec((None, 1, tq), lambda h, i: (h, 0, i))],
        out_specs=[pl.BlockSpec((None, dk, tq), lambda h, i: (h, 0, i)),
                   pl.BlockSpec((T, dk), lambda h, i: (0, h)),
                   pl.BlockSpec((T, dv), lambda h, i: (0, h))],
        out_shape=[jax.ShapeDtypeStruct((H, dk, T), F32), jax.ShapeDtypeStruct((T, H * dk), F32),
                   jax.ShapeDtypeStruct((T, H * dv), F32)],
        scratch_shapes=[pltpu.VMEM((dk, tq), F32), pltpu.VMEM((1, tq), F32), pltpu.VMEM((dv, tq), BF16)],
        args=(q, k, v, do, o, lse))


def _lane_pick(x, h):
    lane = lax.broadcasted_iota(jnp.int32, (1, x.shape[1]), 1)
    return jnp.sum(jnp.where(lane == h, x, 0.0), axis=1, keepdims=True)


def _row_pick(x, h):
    sub = lax.broadcasted_iota(jnp.int32, (x.shape[0], 1), 0)
    return jnp.sum(jnp.where(sub == h, x, 0.0), axis=0, keepdims=True)


def ssd_chunk_fn(*args):
    nh, ng = SSM_HEADS, SSM_GROUPS
    xs = args[:nh]
    bs = args[nh:nh + ng]
    cs = args[nh + ng:nh + 2 * ng]
    dt_raw = args[nh + 2 * ng]
    st = args[nh + 2 * ng + 1:nh + 2 * ng + 1 + nh]
    dt_bias, a_log, d_skip = args[nh + 2 * ng + 1 + nh:]
    L = dt_raw.shape[0]
    dt = _softplus(dt_raw + dt_bias)
    da = dt * (-jnp.exp(a_log))
    dcs = csum_rows(da)
    dcs_t = dcs.T
    total = jnp.sum(da, axis=0, keepdims=True)
    causal = lax.broadcasted_iota(jnp.int32, (L, L), 0) >= lax.broadcasted_iota(jnp.int32, (L, L), 1)
    cb = [mm_nt(cs[g], bs[g]) for g in range(ng)]
    ys, new_st = [], []
    for h in range(nh):
        g = h // (nh // ng)
        dcs_h = _lane_pick(dcs, h)
        dt_h = _lane_pick(dt, h)
        tot_h = _lane_pick(total, h)
        dsk_h = _lane_pick(d_skip, h)
        decay = jnp.exp(jnp.where(causal, dcs_h - _row_pick(dcs_t, h), NEG))
        xdt = xs[h] * dt_h
        y = mm_nn(cb[g] * decay, xdt)
        y = y + mm_nn(cs[g] * jnp.exp(dcs_h), st[h])
        ys.append(y + xs[h] * dsk_h)
        new_st.append(st[h] * jnp.exp(tot_h) + mm_tn(bs[g] * jnp.exp(tot_h - dcs_h), xdt))
    return tuple(ys) + tuple(new_st)


def ssd_fwd(x_hm, b_hm, c_hm, proj, dt_bias, a_log, d_skip, *, name):
    nh, T, P = x_hm.shape
    ng, N = b_hm.shape[0], b_hm.shape[2]
    L = SSM_CHUNK
    nc = T // L
    dtb = OFF_DT // LANES

    def body(x_ref, b_ref, c_ref, dt_ref, db_ref, al_ref, ds_ref, y_ref, s_ref, state):
        @pl.when(pl.program_id(0) == 0)
        def _():
            state[...] = jnp.zeros_like(state)

        s_ref[...] = state[...]
        args = ([x_ref[h] for h in range(nh)] + [b_ref[g] for g in range(ng)] + [c_ref[g] for g in range(ng)]
                + [dt_ref[...]] + [state[h] for h in range(nh)] + [db_ref[...], al_ref[...], ds_ref[...]])
        res = ssd_chunk_fn(*args)
        for h in range(nh):
            y_ref[:, h * P:(h + 1) * P] = res[h]
            state[h] = res[nh + h]

    par = pl.BlockSpec((1, LANES), lambda i: (0, 0))
    return pl.pallas_call(
        body, name=name, grid=(nc,),
        in_specs=[pl.BlockSpec((nh, L, P), lambda i: (0, i, 0)), pl.BlockSpec((ng, L, N), lambda i: (0, i, 0)),
                  pl.BlockSpec((ng, L, N), lambda i: (0, i, 0)), pl.BlockSpec((L, LANES), lambda i: (i, dtb)),
                  par, par, par],
        out_specs=[pl.BlockSpec((L, nh * P), lambda i: (i, 0)),
                   pl.BlockSpec((None, nh, N, P), lambda i: (i, 0, 0, 0))],
        out_shape=[jax.ShapeDtypeStruct((T, nh * P), F32), jax.ShapeDtypeStruct((nc, nh, N, P), F32)],
        scratch_shapes=[pltpu.VMEM((nh, N, P), F32)],
        compiler_params=_cparams("arbitrary"),
    )(x_hm, b_hm, c_hm, proj, dt_bias, a_log, d_skip)


def ssd_bwd(x_hm, b_hm, c_hm, proj, states, dt_bias, a_log, d_skip, dy, *, name):
    nh, T, P = x_hm.shape
    ng, N = b_hm.shape[0], b_hm.shape[2]
    L = SSM_CHUNK
    nc = T // L
    dtb = OFF_DT // LANES

    def body(x_ref, b_ref, c_ref, dt_ref, s_ref, db_ref, al_ref, ds_ref, dy_ref,
             dxbc_ref, ddt_ref, gdb_ref, gal_ref, gds_ref, dstate):
        @pl.when(pl.program_id(0) == 0)
        def _():
            dstate[...] = jnp.zeros_like(dstate)
            gdb_ref[...] = jnp.zeros_like(gdb_ref)
            gal_ref[...] = jnp.zeros_like(gal_ref)
            gds_ref[...] = jnp.zeros_like(gds_ref)

        args = ([x_ref[h] for h in range(nh)] + [b_ref[g] for g in range(ng)] + [c_ref[g] for g in range(ng)]
                + [dt_ref[...]] + [s_ref[h] for h in range(nh)] + [db_ref[...], al_ref[...], ds_ref[...]])
        _, vjp = jax.vjp(ssd_chunk_fn, *args)
        g = vjp(tuple([dy_ref[h] for h in range(nh)] + [dstate[h] for h in range(nh)]))
        for j in range(nh + 2 * ng):
            dxbc_ref[:, j * P:(j + 1) * P] = g[j]
        ddt_ref[...] = g[nh + 2 * ng]
        for h in range(nh):
            dstate[h] = g[nh + 2 * ng + 1 + h]
        gdb_ref[...] += g[-3]
        gal_ref[...] += g[-2]
        gds_ref[...] += g[-1]

    rev = lambda i: nc - 1 - i
    par = pl.BlockSpec((1, LANES), lambda i: (0, 0))
    return pl.pallas_call(
        body, name=name, grid=(nc,),
        in_specs=[pl.BlockSpec((nh, L, P), lambda i: (0, rev(i), 0)), pl.BlockSpec((ng, L, N), lambda i: (0, rev(i), 0)),
                  pl.BlockSpec((ng, L, N), lambda i: (0, rev(i), 0)), pl.BlockSpec((L, LANES), lambda i: (rev(i), dtb)),
                  pl.BlockSpec((None, nh, N, P), lambda i: (rev(i), 0, 0, 0)), par, par, par,
                  pl.BlockSpec((nh, L, P), lambda i: (0, rev(i), 0))],
        out_specs=[pl.BlockSpec((L, (nh + 2 * ng) * P), lambda i: (rev(i), 0)),
                   pl.BlockSpec((L, LANES), lambda i: (rev(i), 0)), par, par, par],
        out_shape=[jax.ShapeDtypeStruct((T, (nh + 2 * ng) * P), F32), jax.ShapeDtypeStruct((T, LANES), F32),
                   jax.ShapeDtypeStruct((1, LANES), F32), jax.ShapeDtypeStruct((1, LANES), F32),
                   jax.ShapeDtypeStruct((1, LANES), F32)],
        scratch_shapes=[pltpu.VMEM((nh, N, P), F32)],
        compiler_params=_cparams("arbitrary"),
    )(x_hm, b_hm, c_hm, proj, states, dt_bias, a_log, d_skip, dy)


def loss_head(h, target, g, *, name, tile=512):
    T, C = h.shape
    tl = min(T, tile)

    def body(h_ref, t_ref, g_ref, dh_ref, dg_ref, ls_ref):
        @pl.when(pl.program_id(0) == 0)
        def _():
            dg_ref[...] = jnp.zeros_like(dg_ref)
            ls_ref[...] = jnp.zeros_like(ls_ref)

        (y,), vjp = jax.vjp(rms_fn, h_ref[...], g_ref[...])
        err = y - t_ref[...]
        ls_ref[...] += jnp.sum(err * err, axis=0, keepdims=True) * (0.5 / C)
        dh, dg = vjp((err * (1.0 / C),))
        dh_ref[...] = dh
        dg_ref[...] += dg

    row = pl.BlockSpec((tl, C), lambda i: (i, 0))
    par = pl.BlockSpec((1, C), lambda i: (0, 0))
    return pl.pallas_call(
        body, name=name, grid=(T // tl,), in_specs=[row, row, par], out_specs=[row, par, par],
        out_shape=[jax.ShapeDtypeStruct((T, C), F32), jax.ShapeDtypeStruct((1, C), F32),
                   jax.ShapeDtypeStruct((1, C), F32)],
        compiler_params=_cparams("arbitrary"),
    )(h, target, g)


def adamw(w, g, m, v, *, name):
    R, C = w.shape
    tr = R
    for d in range(8, min(R, 512) + 1, 8):
        if R % d == 0:
            tr = d
    c1 = 1.0 - ADAM_B1 ** ADAM_STEP
    c2 = 1.0 - ADAM_B2 ** ADAM_STEP

    def body(w_ref, g_ref, m_ref, v_ref, d_ref, nm_ref, nv_ref):
        gv = g_ref[...]
        nm = ADAM_B1 * m_ref[...] + (1.0 - ADAM_B1) * gv
        nv = ADAM_B2 * v_ref[...] + (1.0 - ADAM_B2) * (gv * gv)
        d_ref[...] = -ADAM_LR * ((nm / c1) / (jnp.sqrt(nv / c2) + ADAM_EPS) + ADAM_WD * w_ref[...])
        nm_ref[...] = nm
        nv_ref[...] = nv

    spec = pl.BlockSpec((tr, C), lambda i: (i, 0))
    return pl.pallas_call(
        body, name=name, grid=(R // tr,), in_specs=[spec] * 4, out_specs=[spec] * 3,
        out_shape=[jax.ShapeDtypeStruct((R, C), F32)] * 3,
        compiler_params=_cparams("parallel"),
    )(w, g, m, v)


MESH = pl.DeviceIdType.MESH
HBM_SPEC = pl.BlockSpec(memory_space=pltpu.HBM)


def _place():
    return lax.axis_index("x"), lax.axis_index("y"), lax.axis_index("c")


def _direct_exchange(kind, src_ref, out_ref, send_sems, recv_sems, local_sem):
    x, y, c = _place()
    me = 4 * x + 2 * y + c

    def block_for(dest):
        return src_ref if kind == "gather" else src_ref.at[dest]

    own = pltpu.make_async_copy(block_for(me), out_ref.at[me], local_sem)
    sends, arrivals = [], []
    for f in range(1, 8):
        px = jnp.where((f >> 2) & 1, 1 - x, x)
        py = jnp.where((f >> 1) & 1, 1 - y, y)
        pc = jnp.where(f & 1, 1 - c, c)
        peer = 4 * px + 2 * py + pc
        for dst, group in ((out_ref.at[me], sends), (out_ref.at[peer], arrivals)):
            group.append(pltpu.make_async_remote_copy(
                src_ref=block_for(peer), dst_ref=dst, send_sem=send_sems.at[f - 1], recv_sem=recv_sems.at[f - 1],
                device_id=(px, py, pc), device_id_type=MESH))

    def start():
        own.start()
        for cp in sends:
            cp.start()

    def finish():
        for cp in arrivals:
            cp.wait_recv()
        for cp in sends:
            cp.wait_send()
        own.wait()

    return start, finish


def allgather_blocks(mine, *, name):
    R = mine.shape[0]

    def body(x_ref, out_ref, send_sems, recv_sems, local_sem):
        x, y, c = _place()
        me, sibling = (x, y, c), (x, y, 1 - c)
        chips = [(1 - x, y), (x, 1 - y), (1 - x, 1 - y)]

        def slot(px, py, pc):
            return out_ref.at[4 * px + 2 * py + pc]

        def copy(k, block, to, src=None):
            return pltpu.make_async_remote_copy(
                src_ref=slot(*block) if src is None else src, dst_ref=slot(*block),
                send_sem=send_sems.at[k], recv_sem=recv_sems.at[k], device_id=to, device_id_type=MESH)

        own = pltpu.make_async_copy(x_ref, slot(*me), local_sem)
        own.start()
        first = [copy(0, me, sibling, src=x_ref)]
        first += [copy(1 + j, me, (*chip, c), src=x_ref) for j, chip in enumerate(chips)]
        for cp in first:
            cp.start()
        passed = [copy(4 + j, (*chip, c), sibling) for j, chip in enumerate(chips)]
        for j, chip in enumerate(chips):
            copy(1 + j, (*chip, c), me).wait_recv()
            passed[j].start()
        copy(0, sibling, me).wait_recv()
        for j, chip in enumerate(chips):
            copy(4 + j, (*chip, 1 - c), me).wait_recv()
        for cp in first + passed:
            cp.wait_send()
        own.wait()

    return pl.pallas_call(
        body, name=name, out_shape=jax.ShapeDtypeStruct((8, R, LANES), mine.dtype),
        in_specs=[HBM_SPEC], out_specs=HBM_SPEC,
        scratch_shapes=[pltpu.SemaphoreType.DMA((7,)), pltpu.SemaphoreType.DMA((7,)), pltpu.SemaphoreType.DMA],
    )(mine)


def allgather_direct(mine, *, name):
    R = mine.shape[0]

    def body(x_ref, out_ref, send_sems, recv_sems, local_sem):
        start, finish = _direct_exchange("gather", x_ref, out_ref, send_sems, recv_sems, local_sem)
        start()
        finish()

    return pl.pallas_call(
        body, name=name, out_shape=jax.ShapeDtypeStruct((8, R, LANES), mine.dtype),
        in_specs=[HBM_SPEC], out_specs=HBM_SPEC,
        scratch_shapes=[pltpu.SemaphoreType.DMA((7,)), pltpu.SemaphoreType.DMA((7,)), pltpu.SemaphoreType.DMA],
    )(mine)


def send_to_sibling(v, *, name):
    def body(v_ref, out_ref, send_sem, recv_sem):
        x, y, c = _place()
        cp = pltpu.make_async_remote_copy(src_ref=v_ref, dst_ref=out_ref, send_sem=send_sem, recv_sem=recv_sem,
                                          device_id=(x, y, 1 - c), device_id_type=MESH)
        cp.start()
        cp.wait()

    return pl.pallas_call(
        body, name=name, out_shape=jax.ShapeDtypeStruct(v.shape, v.dtype), in_specs=[HBM_SPEC], out_specs=HBM_SPEC,
        scratch_shapes=[pltpu.SemaphoreType.DMA, pltpu.SemaphoreType.DMA],
    )(v)


def chip_exchange(p, *, name):
    R = p.shape[1]

    def body(p_ref, out_ref, send_sems, recv_sems):
        x, y, c = _place()
        chips = [(1 - x, y), (x, 1 - y), (1 - x, 1 - y)]
        sends = [pltpu.make_async_remote_copy(
            src_ref=p_ref.at[2 * px + py], dst_ref=out_ref.at[j], send_sem=send_sems.at[j], recv_sem=recv_sems.at[j],
            device_id=(px, py, c), device_id_type=MESH) for j, (px, py) in enumerate(chips)]
        for cp in sends:
            cp.start()
        for cp in sends:
            cp.wait()

    return pl.pallas_call(
        body, name=name, out_shape=jax.ShapeDtypeStruct((3, R, LANES), p.dtype), in_specs=[HBM_SPEC],
        out_specs=HBM_SPEC,
        scratch_shapes=[pltpu.SemaphoreType.DMA((3,)), pltpu.SemaphoreType.DMA((3,))],
    )(p)


def add_blocks(terms, out_dtype, *, name, tile=1024):
    terms = [t if isinstance(t, tuple) else (t, None) for t in terms]
    R = terms[0][0].shape[-2]
    tr = R
    for d in range(16, min(R, tile) + 1, 16):
        if R % d == 0:
            tr = d

    def body(*refs):
        acc = refs[0][...].astype(F32)
        for ref in refs[1:-1]:
            acc = acc + ref[...].astype(F32)
        refs[-1][...] = acc.astype(out_dtype)

    spec = pl.BlockSpec((tr, LANES), lambda i: (i, 0))
    in_specs = [spec if slot is None else pl.BlockSpec((None, tr, LANES), lambda i, slot=slot: (slot, i, 0))
                for _, slot in terms]
    return pl.pallas_call(
        body, name=name, grid=(R // tr,), in_specs=in_specs, out_specs=spec,
        out_shape=jax.ShapeDtypeStruct((R, LANES), out_dtype), compiler_params=_cparams("parallel"),
    )(*[a for a, _ in terms])


FLAT_ROW_STEP = 640


def _half_rows(arr, cc):
    hr = arr.shape[0] // 2
    return lax.dynamic_slice_in_dim(arr, cc * hr, hr, axis=0).reshape(-1)


def _flat_half(shards, cc, dtype):
    flat = jnp.concatenate([_half_rows(shards[n], cc).astype(dtype) for n in BIG])
    rows = -(-flat.shape[0] // (FLAT_ROW_STEP * LANES)) * FLAT_ROW_STEP
    return jnp.pad(flat, (0, rows * LANES - flat.shape[0])).reshape(rows, LANES)


def _flat_rows(shapes, names=BIG):
    n = sum((shapes[m][0] // 2) * shapes[m][1] for m in names) // LANES
    return -(-n // FLAT_ROW_STEP) * FLAT_ROW_STEP


def _to_blocks(full, shapes, dtype, names=BIG):
    pieces = []
    for n in names:
        R, C = shapes[n]
        a = full[n].astype(dtype)
        if BIG_AXIS[n] == 2:
            a = a.reshape(2, R // 2, 4, C).transpose(2, 0, 1, 3)
        pieces.append(a.reshape(8, (R // 2) * C // LANES, LANES))
    flat = jnp.concatenate(pieces, axis=1)
    return jnp.pad(flat, ((0, 0), (0, _flat_rows(shapes, names) - flat.shape[1]), (0, 0)))


def _from_blocks(g8, shapes):
    out, off = {}, 0
    for n in BIG:
        R, C = shapes[n]
        rows = (R // 2) * C // LANES
        a = g8[:, off:off + rows, :].reshape(4, 2, R // 2, C)
        out[n] = a.transpose(1, 2, 0, 3).reshape(R, 4 * C) if BIG_AXIS[n] == 2 else a.reshape(4 * R, C)
        off += rows
    return out


def _unflat_halves(flat_by_c, shapes, names=BIG):
    out, off = {}, 0
    for n in names:
        R, C = shapes[n]
        sz = (R // 2) * C
        out[n] = jnp.concatenate([flat_by_c[c][off:off + sz].reshape(R // 2, C) for c in range(2)], axis=0)
        off += sz
    return out


def _to_heads(a, nh):
    T = a.shape[0]
    return a.reshape(T, nh, a.shape[1] // nh).transpose(1, 0, 2)


def _from_heads(a):
    nh, T, d = a.shape
    return a.transpose(1, 0, 2).reshape(T, nh * d)


def _to_heads_t(a, nh):
    T = a.shape[0]
    return a.reshape(T, nh, a.shape[1] // nh).transpose(1, 2, 0)


def _from_heads_t(a):
    nh, d, T = a.shape
    return a.transpose(2, 0, 1).reshape(T, nh * d)


def _pad_cols(a, n):
    return jnp.pad(a, ((0, 0), (0, n - a.shape[1])))


def _pack_w_in(w):
    offs = [sum(IN_SPLITS[:i]) for i in range(len(IN_SPLITS) + 1)]
    sb, z, xbc, dt, cq, ckv, kr = [w[:, offs[i]:offs[i + 1]] for i in range(len(IN_SPLITS))]
    zeros = lambda n: jnp.zeros((w.shape[0], n), w.dtype)
    h = MLA_ROPE // 2
    kra = jnp.concatenate([zeros(MLA_NOPE), kr, zeros(LANES - MLA_QK)], axis=1)
    krb = jnp.concatenate([zeros(MLA_NOPE), -kr[:, h:], kr[:, :h], zeros(LANES - MLA_QK)], axis=1)
    return sb, jnp.concatenate([z, xbc, cq, ckv, _pad_cols(dt, LANES), kra, krb], axis=1)


def _unpack_gw_in(g_sb, g):
    h = MLA_ROPE // 2
    ga, gb = g[:, OFF_KRA:OFF_KRA + LANES], g[:, OFF_KRB:OFF_KRB + LANES]
    gkr = ga[:, MLA_NOPE:MLA_QK] + jnp.concatenate([gb[:, MLA_NOPE + h:MLA_QK], -gb[:, MLA_NOPE:MLA_NOPE + h]], axis=1)
    return jnp.concatenate([g_sb, g[:, OFF_Z:OFF_Z + 512], g[:, OFF_XBC:OFF_XBC + 768],
                            g[:, OFF_DT:OFF_DT + 8], g[:, OFF_CQ:OFF_CQ + 256], g[:, OFF_CKV:OFF_CKV + 128], gkr], axis=1)


def _pack_w_uq(w):
    zeros = lambda n: jnp.zeros((w.shape[0], n), w.dtype)
    h = MLA_ROPE // 2
    pp, rr = [], []
    for i in range(MLA_HEADS):
        nope = w[:, MLA_QK * i:MLA_QK * i + MLA_NOPE]
        rope = w[:, MLA_QK * i + MLA_NOPE:MLA_QK * (i + 1)]
        pp += [nope, rope, zeros(LANES - MLA_QK)]
        rr += [zeros(MLA_NOPE), -rope[:, h:], rope[:, :h], zeros(LANES - MLA_QK)]
    return jnp.concatenate(pp, axis=1), jnp.concatenate(rr, axis=1)


def _unpack_gw_uq(gp, gr):
    h = MLA_ROPE // 2
    out = []
    for i in range(MLA_HEADS):
        b = LANES * i
        out.append(gp[:, b:b + MLA_NOPE])
        out.append(gp[:, b + MLA_NOPE:b + MLA_NOPE + h] + gr[:, b + MLA_NOPE + h:b + MLA_QK])
        out.append(gp[:, b + MLA_NOPE + h:b + MLA_QK] - gr[:, b + MLA_NOPE:b + MLA_NOPE + h])
    return jnp.concatenate(out, axis=1)


def _pack_w_ukv(w):
    zeros = lambda n: jnp.zeros((w.shape[0], n), w.dtype)
    kk, vv = [], []
    for i in range(MLA_HEADS):
        b = (MLA_NOPE + MLA_V) * i
        kk += [w[:, b:b + MLA_NOPE], zeros(LANES - MLA_NOPE)]
        vv += [w[:, b + MLA_NOPE:b + MLA_NOPE + MLA_V], zeros(LANES - MLA_V)]
    return jnp.concatenate(kk, axis=1), jnp.concatenate(vv, axis=1)


def _unpack_gw_ukv(gk, gv):
    out = []
    for i in range(MLA_HEADS):
        out += [gk[:, LANES * i:LANES * i + MLA_NOPE], gv[:, LANES * i:LANES * i + MLA_V]]
    return jnp.concatenate(out, axis=1)


def _rope_tables(positions):
    inv_freq = 1.0 / (ROPE_THETA ** (jnp.arange(0, MLA_ROPE, 2, dtype=F32) / MLA_ROPE))
    ang = positions.astype(F32)[:, None] * inv_freq
    cos, sin = jnp.cos(ang), jnp.sin(ang)
    T = positions.shape[0]
    one, zero = jnp.ones((T, MLA_NOPE), F32), jnp.zeros((T, MLA_NOPE), F32)
    pad1, pad0 = jnp.ones((T, LANES - MLA_QK), F32), jnp.zeros((T, LANES - MLA_QK), F32)
    return jnp.concatenate([one, cos, cos, pad1], axis=1), jnp.concatenate([zero, sin, sin, pad0], axis=1)


def _row(v):
    return v.reshape(1, -1)


def _pad_row(v):
    return _pad_cols(v.reshape(1, -1), LANES)


def _layer_weights(full, small, li):
    p = {}
    p["w_sb"], p["w_rest"] = _pack_w_in(full["w_in"])
    q_scale = jnp.concatenate([jnp.full((1, SB_HEADS * SB_DIM), SB_DIM ** -0.5, BF16),
                               jnp.ones((1, 2 * SB_HEADS * SB_DIM), BF16)], axis=1)
    p["w_sb_fwd"] = p["w_sb"] * q_scale
    p["wqp"], p["wqr"] = _pack_w_uq(full["mla_w_uq"])
    p["wkp"], p["wvp"] = _pack_w_ukv(full["mla_w_ukv"])
    p["w_out"] = full["w_out"]
    p["w_up"] = full["ffn_w_up"]
    p["w_down"] = full["ffn_w_down"]
    for n in ("mix_norm", "sb_out_norm", "ssm_conv_b", "ssm_out_norm", "mla_q_norm", "mla_kv_norm", "mla_out_norm",
              "ffn_norm", "ffn_conv_b"):
        p[n] = _row(small[n][li])
    for n in ("ssm_dt_bias", "ssm_a_log", "ssm_d"):
        p[n] = _pad_row(small[n][li])
    p["ssm_conv_w"] = small["ssm_conv_w"][li]
    p["ffn_conv_w"] = small["ffn_conv_w"][li]
    return p


def _layer_fwd(h, p, cos, sin, li, exchange=None):
    T = h.shape[0]
    nm = lambda s: "l%d_%s" % (li, s)
    s = {"h": h}
    (n1,) = rowwise(rms_fn, [h], [p["mix_norm"]], [(D_MODEL, BF16)], name=nm("mix_norm"))
    proj = matmul(n1, p["w_rest"], name=nm("in_proj"))
    qkv = matmul(n1, p["w_sb_fwd"], name=nm("in_proj_sb"), out_dtype=BF16)
    s["n1"], s["proj"] = n1, proj
    s["sb_q"] = _to_heads_t(qkv[:, 0:256], SB_HEADS)
    s["sb_k"] = _to_heads(qkv[:, 256:512], SB_HEADS)
    s["sb_v"] = _to_heads(qkv[:, 512:768], SB_HEADS)
    y_sb_hm, s["sb_bt"], s["sb_first"] = sb_fwd(s["sb_q"], s["sb_k"], s["sb_v"], name=nm("sb_fwd"))
    s["y_sb"] = _from_heads_t(y_sb_hm)
    s["x_hm"], s["b_hm"], s["c_hm"] = ssm_conv_act(proj, p["ssm_conv_w"], p["ssm_conv_b"], name=nm("ssm_conv"))
    s["y_ssm"], s["states"] = ssd_fwd(s["x_hm"], s["b_hm"], s["c_hm"], proj, p["ssm_dt_bias"], p["ssm_a_log"],
                                      p["ssm_d"], name=nm("ssd_fwd"))
    rows = [(proj, 256, OFF_CQ // 256), (proj, 128, OFF_CKV // 128), (proj, 128, OFF_KRA // 128),
            (proj, 128, OFF_KRB // 128), cos, sin]
    qp, kp, vv = rowwise(mla_prep_fn, rows, [p["mla_q_norm"], p["mla_kv_norm"], p["wqp"], p["wqr"], p["wkp"], p["wvp"]],
                         [(512, BF16), (512, BF16), (512, BF16)], name=nm("mla_prep"))
    s["mla_q"], s["mla_k"], s["mla_v"] = _to_heads_t(qp, MLA_HEADS), kp, vv
    s["mla_o"], s["mla_lse"], *rode = mla_fwd(s["mla_q"], kp, vv, name=nm("mla_fwd"), exchange=exchange)
    s["y_mla"] = _from_heads_t(s["mla_o"][:, :MLA_V, :])
    (cat,) = rowwise(merge_fn, [s["y_sb"], s["y_ssm"], (proj, 512, OFF_Z // 512), s["y_mla"]],
                     [p["sb_out_norm"], p["ssm_out_norm"], p["mla_out_norm"]], [(D_MODEL, BF16)], name=nm("merge"),
                     post=lambda a, b, c: (jnp.concatenate([a, b, c], axis=1),))
    s["cat"] = cat
    h1 = matmul(cat, p["w_out"], name=nm("out_proj"), residual=h)
    s["h1"] = h1
    (n2,) = rowwise(rms_fn, [h1], [p["ffn_norm"]], [(D_MODEL, BF16)], name=nm("ffn_norm"))
    up = matmul(n2, p["w_up"], name=nm("ffn_up"))
    act = ffn_act(up, p["ffn_conv_w"], p["ffn_conv_b"], name=nm("ffn_act"))
    s["n2"], s["up"], s["act"] = n2, up, act
    h2 = matmul(act, p["w_down"], name=nm("ffn_down"), residual=h1)
    return h2, s, (rode[0] if rode else None)


def _layer_bwd(dh2, s, p, cos, sin, li, exchange=None, exchange_ffn=None):
    nm = lambda t: "l%d_%s" % (li, t)
    g = {}
    proj = s["proj"]
    g["ffn_w_down"] = matmul(s["act"], dh2, name=nm("g_w_down"), ta=True)
    d_act = matmul(dh2, p["w_down"], name=nm("d_act"), out_dtype=BF16, tb=True)
    d_up_g, d_up_v, gwg, gwv, gbg, gbv, *rode_ffn = ffn_bwd_fused(s["up"], p["ffn_conv_w"], p["ffn_conv_b"], d_act,
                                                                  name=nm("ffn_act_bwd"), exchange=exchange_ffn)
    g["ffn_conv_w"] = jnp.concatenate([gwg, gwv], axis=1)
    g["ffn_conv_b"] = jnp.concatenate([gbg[0], gbv[0]])
    g["ffn_w_up"] = jnp.concatenate([matmul(s["n2"], d_up_g, name=nm("g_w_up_gate"), ta=True),
                                     matmul(s["n2"], d_up_v, name=nm("g_w_up_val"), ta=True)], axis=1)
    d_n2 = matmul(d_up_g, p["w_up"], name=nm("d_n2_gate"), tb=True)
    d_n2 = matmul(d_up_v, p["w_up"], name=nm("d_n2_val"), tb=True, b_k0=D_FF, residual=d_n2)
    (dh1,), (gn,) = rowwise_bwd(rms_fn, [s["h1"]], [], [p["ffn_norm"]], [d_n2], [F32], name=nm("ffn_norm_bwd"),
                                add0=dh2)
    g["ffn_norm"] = gn[0]
    g["w_out"] = matmul(s["cat"], dh1, name=nm("g_w_out"), ta=True)
    d_cat = matmul(dh1, p["w_out"], name=nm("d_cat"), tb=True)
    (d_ysb, d_yssm, d_z, d_ymla), (g1, g2, g3) = rowwise_bwd(
        merge_fn, [s["y_sb"], s["y_ssm"], (proj, 512, OFF_Z // 512), s["y_mla"]], [],
        [p["sb_out_norm"], p["ssm_out_norm"], p["mla_out_norm"]], [d_cat], [F32, F32, BF16, F32], name=nm("merge_bwd"),
        pre_ct=lambda d: (d[:, 0:256], d[:, 256:768], d[:, 768:1024]))
    g["sb_out_norm"], g["ssm_out_norm"], g["mla_out_norm"] = g1[0], g2[0], g3[0]
    dq, dk, dv = sb_bwd(s["sb_q"], s["sb_k"], s["sb_v"], _to_heads_t(d_ysb, SB_HEADS), s["sb_bt"], s["sb_first"], name=nm("sb_bwd"),
                        q_scale=SB_DIM ** -0.5)
    d_sb = jnp.concatenate([_from_heads_t(dq), _from_heads(dk), _from_heads(dv)], axis=1).astype(BF16)
    if callable(exchange):
        exchange = exchange(g)
    do_t = jnp.pad(_to_heads_t(d_ymla, MLA_HEADS), ((0, 0), (0, LANES - MLA_V), (0, 0)))
    dqp, dkp, dvv, *rode = mla_bwd(s["mla_q"], s["mla_k"], s["mla_v"], do_t, s["mla_o"], s["mla_lse"],
                                   name=nm("mla_bwd"), exchange=exchange)
    rows = [(proj, 256, OFF_CQ // 256), (proj, 128, OFF_CKV // 128), (proj, 128, OFF_KRA // 128),
            (proj, 128, OFF_KRB // 128)]
    (d_cq, d_ckv, d_kra, d_krb), (gqn, gkvn, gwqp, gwqr, gwkp, gwvp) = rowwise_bwd(
        mla_prep_fn, rows, [cos, sin], [p["mla_q_norm"], p["mla_kv_norm"], p["wqp"], p["wqr"], p["wkp"], p["wvp"]],
        [_from_heads_t(dqp), dkp, dvv], [BF16] * 4, name=nm("mla_prep_bwd"), tile=256)
    g["mla_q_norm"], g["mla_kv_norm"] = gqn[0], gkvn[0]
    g["mla_w_uq"] = _unpack_gw_uq(gwqp, gwqr)
    g["mla_w_ukv"] = _unpack_gw_ukv(gwkp, gwvp)
    d_xbc_act, d_dt, gdb, gal, gds = ssd_bwd(
        s["x_hm"], s["b_hm"], s["c_hm"], proj, s["states"], p["ssm_dt_bias"], p["ssm_a_log"], p["ssm_d"],
        _to_heads(d_yssm, SSM_HEADS), name=nm("ssd_bwd"))
    g["ssm_dt_bias"], g["ssm_a_log"], g["ssm_d"] = gdb[0, :8], gal[0, :8], gds[0, :8]
    d_pre = ssm_conv_bwd_a(proj, p["ssm_conv_w"], p["ssm_conv_b"], d_xbc_act, name=nm("ssm_conv_bwd_a"))
    d_xbc, g["ssm_conv_w"], gscb = conv_bwd_b(d_pre, proj, OFF_XBC, p["ssm_conv_w"], name=nm("ssm_conv_bwd_b"),
                                              out_dtype=BF16, tc=256)
    g["ssm_conv_b"] = gscb[0]
    d_proj = jnp.concatenate([d_z, d_xbc, d_cq, d_ckv, d_dt.astype(BF16), d_kra, d_krb], axis=1)
    g["w_in"] = _unpack_gw_in(matmul(s["n1"], d_sb, name=nm("g_w_in_sb"), ta=True),
                              matmul(s["n1"], d_proj, name=nm("g_w_in"), ta=True))
    d_n1 = matmul(d_sb, p["w_sb"], name=nm("d_n1_sb"), tb=True)
    d_n1 = matmul(d_proj, p["w_rest"], name=nm("d_n1"), tb=True, residual=d_n1)
    (dh0,), (gm,) = rowwise_bwd(rms_fn, [s["h"]], [], [p["mix_norm"]], [d_n1], [F32], name=nm("mix_norm_bwd"),
                                add0=dh1)
    g["mix_norm"] = gm[0]
    return dh0, g, (rode[0] if rode else None), (rode_ffn[0] if rode_ffn else None)


def kernel(x, positions, mix_norm, w_in, sb_out_norm, ssm_conv_w, ssm_conv_b, ssm_dt_bias, ssm_a_log, ssm_d, ssm_out_norm, mla_q_norm, mla_w_uq, mla_kv_norm, mla_w_ukv, mla_out_norm, w_out, ffn_norm, ffn_w_up, ffn_conv_w, ffn_conv_b, ffn_w_down, final_norm, loss_target, m_mix_norm, m_w_in, m_sb_out_norm, m_ssm_conv_w, m_ssm_conv_b, m_ssm_dt_bias, m_ssm_a_log, m_ssm_d, m_ssm_out_norm, m_mla_q_norm, m_mla_w_uq, m_mla_kv_norm, m_mla_w_ukv, m_mla_out_norm, m_w_out, m_ffn_norm, m_ffn_w_up, m_ffn_conv_w, m_ffn_conv_b, m_ffn_w_down, m_final_norm, v_mix_norm, v_w_in, v_sb_out_norm, v_ssm_conv_w, v_ssm_conv_b, v_ssm_dt_bias, v_ssm_a_log, v_ssm_d, v_ssm_out_norm, v_mla_q_norm, v_mla_w_uq, v_mla_kv_norm, v_mla_w_ukv, v_mla_out_norm, v_w_out, v_ffn_norm, v_ffn_w_up, v_ffn_conv_w, v_ffn_conv_b, v_ffn_w_down, v_final_norm):
    W = dict(mix_norm=mix_norm, w_in=w_in, sb_out_norm=sb_out_norm, ssm_conv_w=ssm_conv_w, ssm_conv_b=ssm_conv_b,
             ssm_dt_bias=ssm_dt_bias, ssm_a_log=ssm_a_log, ssm_d=ssm_d, ssm_out_norm=ssm_out_norm,
             mla_q_norm=mla_q_norm, mla_w_uq=mla_w_uq, mla_kv_norm=mla_kv_norm, mla_w_ukv=mla_w_ukv,
             mla_out_norm=mla_out_norm, w_out=w_out, ffn_norm=ffn_norm, ffn_w_up=ffn_w_up, ffn_conv_w=ffn_conv_w,
             ffn_conv_b=ffn_conv_b, ffn_w_down=ffn_w_down, final_norm=final_norm)
    M = dict(mix_norm=m_mix_norm, w_in=m_w_in, sb_out_norm=m_sb_out_norm, ssm_conv_w=m_ssm_conv_w,
             ssm_conv_b=m_ssm_conv_b, ssm_dt_bias=m_ssm_dt_bias, ssm_a_log=m_ssm_a_log, ssm_d=m_ssm_d,
             ssm_out_norm=m_ssm_out_norm, mla_q_norm=m_mla_q_norm, mla_w_uq=m_mla_w_uq, mla_kv_norm=m_mla_kv_norm,
             mla_w_ukv=m_mla_w_ukv, mla_out_norm=m_mla_out_norm, w_out=m_w_out, ffn_norm=m_ffn_norm,
             ffn_w_up=m_ffn_w_up, ffn_conv_w=m_ffn_conv_w, ffn_conv_b=m_ffn_conv_b, ffn_w_down=m_ffn_w_down,
             final_norm=m_final_norm)
    V = dict(mix_norm=v_mix_norm, w_in=v_w_in, sb_out_norm=v_sb_out_norm, ssm_conv_w=v_ssm_conv_w,
             ssm_conv_b=v_ssm_conv_b, ssm_dt_bias=v_ssm_dt_bias, ssm_a_log=v_ssm_a_log, ssm_d=v_ssm_d,
             ssm_out_norm=v_ssm_out_norm, mla_q_norm=v_mla_q_norm, mla_w_uq=v_mla_w_uq, mla_kv_norm=v_mla_kv_norm,
             mla_w_ukv=v_mla_w_ukv, mla_out_norm=v_mla_out_norm, w_out=v_w_out, ffn_norm=v_ffn_norm,
             ffn_w_up=v_ffn_w_up, ffn_conv_w=v_ffn_conv_w, ffn_conv_b=v_ffn_conv_b, ffn_w_down=v_ffn_w_down,
             final_norm=v_final_norm)
    depth = mix_norm.shape[0]
    cx, cy, cc = _place()
    chip = 2 * cx + cy
    T = x.shape[1]

    assert depth == 2
    shard_shapes = {n: W[n].shape[1:] for n in BIG}

    def layer_of(d, li):
        return {n: d[n][li] for n in BIG}

    def assemble(g8):
        return _from_blocks(g8, shard_shapes)

    full0 = assemble(allgather_blocks(_flat_half(layer_of(W, 0), cc, BF16), name="gather_weights_l0"))
    conv_full = {}
    small = {n: W[n] for n in SMALL_REPL}
    cw_flat = jnp.concatenate([W[n].reshape(-1) for n in SMALL_SHARD])
    cw_rows = -(-cw_flat.shape[0] // (8 * LANES)) * 8
    cw_all = allgather_direct(jnp.pad(cw_flat, (0, cw_rows * LANES - cw_flat.shape[0])).reshape(cw_rows, LANES),
                              name="gather_conv_taps")
    off = 0
    for n in SMALL_SHARD:
        sz = W[n].size
        conv_full[n] = jnp.concatenate(
            [cw_all[2 * k].reshape(-1)[off:off + sz].reshape(W[n].shape) for k in range(4)], axis=2)
        off += sz
    small.update(conv_full)

    cos, sin = _rope_tables(positions[0])
    params0 = _layer_weights(full0, small, 0)
    h, s0, g8 = _layer_fwd(x[0], params0, cos, sin, 0, exchange=("gather", _flat_half(layer_of(W, 1), cc, BF16)))
    params1 = _layer_weights(assemble(g8), small, 1)
    h, s1, _ = _layer_fwd(h, params1, cos, sin, 1)
    dh, g_final, loss_lanes = loss_head(h, loss_target[0], _row(final_norm), name="loss_head")

    ffn_w = ("ffn_w_up", "ffn_w_down")
    mix_w = tuple(n for n in BIG if n not in ffn_w)
    dh, g1, _, _ = _layer_bwd(dh, s1, params1, cos, sin, 1)
    blocks1 = _to_blocks(g1, shard_shapes, BF16)
    R1, Rf = blocks1.shape[1], _flat_rows(shard_shapes, ffn_w)

    def rider(g0_so_far):
        return "all_to_all", _to_blocks(g0_so_far, shard_shapes, BF16, ffn_w)

    dh, g0, from_all_f, from_all_1 = _layer_bwd(dh, s0, params0, cos, sin, 0, exchange=rider,
                                                exchange_ffn=("all_to_all", blocks1))
    grad_x = dh[None]
    grads = [g0, g1]
    G = {n: jnp.stack([grads[li][n] for li in range(depth)]) for n in WEIGHTS if n != "final_norm" and n not in BIG}
    G["final_norm"] = g_final[0]
    half1 = jnp.concatenate([add_blocks([(from_all_1, d) for d in range(8)], F32, name="grads_l1_sum"),
                             add_blocks([(from_all_f, d) for d in range(8)], F32, name="grads_l0_ffn_sum")])

    blocks0 = _to_blocks(g0, shard_shapes, BF16, mix_w)
    R = blocks0.shape[1]
    blocks0 = blocks0.reshape(4, 2, R, LANES)
    mine_first = lax.dynamic_index_in_dim(blocks0, cc, 1, keepdims=False)
    for_sibling = lax.dynamic_index_in_dim(blocks0, 1 - cc, 1, keepdims=False)
    from_sibling = send_to_sibling(for_sibling.reshape(4 * R, LANES), name="grads_to_sibling")
    pair = add_blocks([mine_first.reshape(4 * R, LANES), from_sibling], BF16, name="grads_pair_sum").reshape(4, R, LANES)
    others = chip_exchange(pair, name="grads_chip_exchange")
    own = lax.dynamic_index_in_dim(pair, chip, 0, keepdims=False)
    half0 = add_blocks([own, (others, 0), (others, 1), (others, 2)], F32, name="grads_chip_sum")
    half = jnp.concatenate([half0, half1])
    other = send_to_sibling(half, name="grads_pair_swap")
    by_core = [jnp.where(cc == 0, half, other), jnp.where(cc == 0, other, half)]

    def unflat(lo, hi, names):
        return _unflat_halves([a[lo:hi].reshape(-1) for a in by_core], shard_shapes, names)

    g_big_l = [{**unflat(0, R, mix_w), **unflat(R + R1, R + R1 + Rf, ffn_w)}, unflat(R, R + R1, BIG)]
    g_big = {n: jnp.stack([g_big_l[li][n] for li in range(depth)]) for n in BIG}

    small_list = [G[n].reshape(-1) for n in SMALL_REPL] + [G[n].reshape(-1) for n in SMALL_SHARD]
    small_list.append(jnp.sum(loss_lanes).reshape(1))
    sm = jnp.concatenate(small_list)
    n_small = sm.shape[0]
    sm_rows = -(-n_small // (16 * LANES)) * 16
    sm_all = allgather_direct(jnp.pad(sm, (0, sm_rows * LANES - n_small)).reshape(sm_rows, LANES), name="gather_small")
    sm_sum = add_blocks([(sm_all, d) for d in range(8)], F32, name="small_sum").reshape(-1)
    g_small, off = {}, 0
    for n in SMALL_REPL:
        g_small[n] = sm_sum[off:off + W[n].size].reshape(W[n].shape)
        off += W[n].size
    for n in SMALL_SHARD:
        full_shape = conv_full[n].shape
        sz = conv_full[n].size
        gfull = sm_sum[off:off + sz].reshape(full_shape)
        width = W[n].shape[2]
        g_small[n] = lax.dynamic_slice_in_dim(gfull, chip * width, width, axis=2)
        off += sz
    loss = sm_sum[off]

    grad_out, delta, new_m, new_v = {}, {}, {}, {}
    for n in BIG:
        shp = W[n].shape
        two_d = lambda a: a.reshape(shp[0] * shp[1], shp[2])
        d, nm_, nv_ = adamw(two_d(W[n]), two_d(g_big[n]), two_d(M[n]), two_d(V[n]), name="adamw_" + n)
        grad_out[n], delta[n], new_m[n], new_v[n] = g_big[n], d.reshape(shp), nm_.reshape(shp), nv_.reshape(shp)
    small_names = SMALL_REPL + SMALL_SHARD

    def flat_small(d):
        f = jnp.concatenate([d[n].reshape(-1) for n in small_names])
        rows = -(-f.shape[0] // (8 * LANES)) * 8
        return jnp.pad(f, (0, rows * LANES - f.shape[0])).reshape(rows, LANES)

    vpad = flat_small(V)
    d, nm_, nv_ = adamw(flat_small(W), flat_small(g_small), flat_small(M), vpad, name="adamw_small")
    off = 0
    for n in small_names:
        sz = W[n].size
        grad_out[n] = g_small[n]
        delta[n] = d.reshape(-1)[off:off + sz].reshape(W[n].shape)
        new_m[n] = nm_.reshape(-1)[off:off + sz].reshape(W[n].shape)
        new_v[n] = nv_.reshape(-1)[off:off + sz].reshape(W[n].shape)
        off += sz

    return (loss, grad_x, *[grad_out[n] for n in WEIGHTS], *[delta[n] for n in WEIGHTS],
            *[new_m[n] for n in WEIGHTS], *[new_v[n] for n in WEIGHTS])
```

```python
import functools
import math

import jax
import jax.numpy as jnp
from jax import lax
from jax.experimental import pallas as pl
from jax.experimental.pallas import tpu as pltpu

F32 = jnp.float32
BF16 = jnp.bfloat16

EPS = 1e-6
D_MODEL = 1024
SB_HEADS, SB_DIM = 4, 64
SSM_HEADS, SSM_DIM, SSM_GROUPS, SSM_STATE, SSM_CHUNK = 8, 64, 2, 64, 128
SSM_INNER = SSM_HEADS * SSM_DIM
SSM_CONV_DIM = SSM_INNER + 2 * SSM_GROUPS * SSM_STATE
MLA_HEADS, MLA_NOPE, MLA_ROPE, MLA_V = 4, 64, 32, 64
MLA_QK = MLA_NOPE + MLA_ROPE
MLA_SCALE = MLA_QK ** -0.5
ROPE_THETA = 10000.0
D_FF = 2816
IN_SPLITS = (768, 512, 768, 8, 256, 128, 32)

OFF_Z, OFF_XBC, OFF_CQ, OFF_CKV, OFF_DT, OFF_KRA, OFF_KRB = 0, 512, 1280, 1536, 1664, 1792, 1920
D_REST = 2048
LANES = 128

ADAM_LR, ADAM_B1, ADAM_B2, ADAM_EPS, ADAM_WD, ADAM_STEP = 0.001, 0.9, 0.999, 1e-08, 0.01, 10

V7X_VMEM_LIMIT = 48 * 1024 * 1024

NT = (((1,), (1,)), ((), ()))
TN = (((0,), (0,)), ((), ()))

BIG = ("w_in", "mla_w_uq", "mla_w_ukv", "w_out", "ffn_w_up", "ffn_w_down")
BIG_AXIS = {"w_in": 2, "mla_w_uq": 2, "mla_w_ukv": 2, "w_out": 1, "ffn_w_up": 2, "ffn_w_down": 1}
SMALL_REPL = ("mix_norm", "sb_out_norm", "ssm_conv_b", "ssm_dt_bias", "ssm_a_log", "ssm_d", "ssm_out_norm",
              "mla_q_norm", "mla_kv_norm", "mla_out_norm", "ffn_norm", "ffn_conv_b", "final_norm")
SMALL_SHARD = ("ssm_conv_w", "ffn_conv_w")
WEIGHTS = ("mix_norm", "w_in", "sb_out_norm", "ssm_conv_w", "ssm_conv_b", "ssm_dt_bias", "ssm_a_log", "ssm_d",
           "ssm_out_norm", "mla_q_norm", "mla_w_uq", "mla_kv_norm", "mla_w_ukv", "mla_out_norm", "w_out", "ffn_norm",
           "ffn_w_up", "ffn_conv_w", "ffn_conv_b", "ffn_w_down", "final_norm")


def _cparams(*sem):
    return pltpu.CompilerParams(dimension_semantics=sem if sem else None, vmem_limit_bytes=V7X_VMEM_LIMIT)


def _pick(n, target, mult=LANES):
    best = None
    for d in range(mult, min(n, target) + 1, mult):
        if n % d == 0:
            best = d
    return best or n


def _sigmoid(x):
    return 1.0 / (1.0 + jnp.exp(-x))


def _softplus(x):
    ax = jnp.where(x > 0, x, -x)
    return jnp.where(x > 0, x, 0.0) + jnp.log(1.0 + jnp.exp(-ax))


def _rms(x, g):
    return x * lax.rsqrt(jnp.mean(x * x, axis=-1, keepdims=True) + EPS) * g


def _raw_nn(a, b):
    return jnp.dot(a.astype(BF16), b.astype(BF16), preferred_element_type=F32)


def _raw_nt(a, b):
    return lax.dot_general(a.astype(BF16), b.astype(BF16), NT, preferred_element_type=F32)


def _raw_tn(a, b):
    return lax.dot_general(a.astype(BF16), b.astype(BF16), TN, preferred_element_type=F32)


@jax.custom_vjp
def mm_nn(a, b):
    return _raw_nn(a, b)


mm_nn.defvjp(lambda a, b: (_raw_nn(a, b), (a, b)),
             lambda r, ct: (_raw_nt(ct, r[1]), _raw_tn(r[0], ct)))


@jax.custom_vjp
def mm_nt(a, b):
    return _raw_nt(a, b)


mm_nt.defvjp(lambda a, b: (_raw_nt(a, b), (a, b)),
             lambda r, ct: (_raw_nn(ct, r[1]), _raw_tn(ct, r[0])))


@jax.custom_vjp
def mm_tn(a, b):
    return _raw_tn(a, b)


mm_tn.defvjp(lambda a, b: (_raw_tn(a, b), (a, b)),
             lambda r, ct: (_raw_nt(r[1], ct), _raw_nn(r[0], ct)))


def _tri_dot(tri, x, terms=3):
    parts = []
    r = x
    for t in range(terms):
        xt = r.astype(BF16)
        parts.append(xt)
        if t + 1 < terms:
            r = r - xt.astype(F32)
    return jnp.dot(jnp.concatenate([tri] * terms, axis=1), jnp.concatenate(parts, axis=0),
                   preferred_element_type=F32)


def _tri(n, cmp):
    r = lax.broadcasted_iota(jnp.int32, (n, n), 0)
    c = lax.broadcasted_iota(jnp.int32, (n, n), 1)
    return cmp(r, c).astype(BF16)


@jax.custom_vjp
def csum_rows(x):
    return _tri_dot(_tri(x.shape[0], lambda r, c: r >= c), x)


csum_rows.defvjp(lambda x: (csum_rows(x), None),
                 lambda _, ct: (_tri_dot(_tri(ct.shape[0], lambda r, c: r <= c), ct),))


def matmul(a, b, *, name, out_dtype=F32, ta=False, tb=False, b_k0=0, residual=None):
    if ta:
        K, M = a.shape
    else:
        M, K = a.shape
    N = b.shape[0] if tb else b.shape[1]
    tm = _pick(M, 1408)
    tn = _pick(N, 1408)
    tk = _pick(K, 1408)
    nk = K // tk
    kb0 = b_k0 // tk
    assert b_k0 % tk == 0 and (tb or b_k0 == 0)
    has_res = residual is not None

    def body(*refs):
        if has_res:
            a_ref, b_ref, r_ref, o_ref, acc = refs
        else:
            a_ref, b_ref, o_ref, acc = refs
        k = pl.program_id(2)

        @pl.when(k == 0)
        def _():
            acc[...] = jnp.zeros_like(acc)

        av = a_ref[...].astype(BF16)
        bv = b_ref[...].astype(BF16)
        if ta:
            acc[...] += lax.dot_general(av, bv, TN, preferred_element_type=F32)
        elif tb:
            acc[...] += lax.dot_general(av, bv, NT, preferred_element_type=F32)
        else:
            acc[...] += jnp.dot(av, bv, preferred_element_type=F32)

        @pl.when(k == nk - 1)
        def _():
            r = acc[...]
            if has_res:
                r = r + r_ref[...].astype(F32)
            o_ref[...] = r.astype(o_ref.dtype)

    a_spec = pl.BlockSpec((tk, tm), lambda i, j, k: (k, i)) if ta else pl.BlockSpec((tm, tk), lambda i, j, k: (i, k))
    b_spec = pl.BlockSpec((tn, tk), lambda i, j, k: (j, kb0 + k)) if tb else pl.BlockSpec((tk, tn), lambda i, j, k: (k, j))
    in_specs = [a_spec, b_spec]
    args = [a, b]
    if has_res:
        in_specs.append(pl.BlockSpec((tm, tn), lambda i, j, k: (i, j)))
        args.append(residual)
    return pl.pallas_call(
        body, name=name, grid=(M // tm, N // tn, nk),
        in_specs=in_specs, out_specs=pl.BlockSpec((tm, tn), lambda i, j, k: (i, j)),
        out_shape=jax.ShapeDtypeStruct((M, N), out_dtype),
        scratch_shapes=[pltpu.VMEM((tm, tn), F32)],
        compiler_params=_cparams("parallel", "parallel", "arbitrary"),
    )(*args)


def _row_spec(entry, tl):
    if isinstance(entry, tuple):
        arr, width, cb = entry
        return arr, pl.BlockSpec((tl, width), lambda i, cb=cb: (i, cb))
    return entry, pl.BlockSpec((tl, entry.shape[1]), lambda i: (i, 0))


def _rows_T(entry):
    return (entry[0] if isinstance(entry, tuple) else entry).shape[0]


def rowwise(fn, rows, params, outs, *, name, tile=512, post=None):
    T = _rows_T(rows[0])
    tl = min(T, tile)
    nr, npar = len(rows), len(params)

    def body(*refs):
        r = [ref[...].astype(F32) for ref in refs[:nr]]
        p = [ref[...].astype(F32) for ref in refs[nr:nr + npar]]
        res = fn(*r, *p)
        if post is not None:
            res = post(*res)
        for o_ref, val in zip(refs[nr + npar:], res):
            o_ref[...] = val.astype(o_ref.dtype)

    arrs, specs = [], []
    for e in rows:
        a, s = _row_spec(e, tl)
        arrs.append(a)
        specs.append(s)
    for p in params:
        arrs.append(p)
        specs.append(pl.BlockSpec(p.shape, lambda i: (0, 0)))
    res = pl.pallas_call(
        body, name=name, grid=(T // tl,), in_specs=specs,
        out_specs=[pl.BlockSpec((tl, c), lambda i: (i, 0)) for c, _ in outs],
        out_shape=[jax.ShapeDtypeStruct((T, c), dt) for c, dt in outs],
        compiler_params=_cparams("parallel"),
    )(*arrs)
    return res


def rowwise_bwd(fn, rows, nd_rows, params, cts, grad_dtypes, *, name, tile=512, pre_ct=None, add0=None):
    T = _rows_T(rows[0])
    tl = min(T, tile)
    nr, nn, npar, nc = len(rows), len(nd_rows), len(params), len(cts)
    has_add = add0 is not None

    def body(*refs):
        pos = 0
        r = [ref[...].astype(F32) for ref in refs[pos:pos + nr]]
        pos += nr
        nd = [ref[...].astype(F32) for ref in refs[pos:pos + nn]]
        pos += nn
        p = [ref[...].astype(F32) for ref in refs[pos:pos + npar]]
        pos += npar
        c = [ref[...].astype(F32) for ref in refs[pos:pos + nc]]
        pos += nc
        if has_add:
            addv = refs[pos][...].astype(F32)
            pos += 1
        rg_refs = refs[pos:pos + nr]
        pg_refs = refs[pos + nr:pos + nr + npar]
        if pre_ct is not None:
            c = list(pre_ct(*c))
        _, vjp = jax.vjp(lambda *a: fn(*a[:nr], *nd, *a[nr:]), *r, *p)
        g = vjp(tuple(c))
        for j, ref in enumerate(rg_refs):
            val = g[j]
            if has_add and j == 0:
                val = val + addv
            ref[...] = val.astype(ref.dtype)
        if npar:
            @pl.when(pl.program_id(0) == 0)
            def _():
                for ref in pg_refs:
                    ref[...] = jnp.zeros_like(ref)
            for j, ref in enumerate(pg_refs):
                ref[...] += g[nr + j]

    arrs, specs = [], []
    widths = []
    for e in list(rows) + list(nd_rows):
        a, s = _row_spec(e, tl)
        arrs.append(a)
        specs.append(s)
        widths.append(s.block_shape[1])
    for p in params:
        arrs.append(p)
        specs.append(pl.BlockSpec(p.shape, lambda i: (0, 0)))
    for e in cts:
        a, s = _row_spec(e, tl)
        arrs.append(a)
        specs.append(s)
    if has_add:
        a, s = _row_spec(add0, tl)
        arrs.append(a)
        specs.append(s)
    out_specs = [pl.BlockSpec((tl, widths[j]), lambda i: (i, 0)) for j in range(nr)]
    out_shape = [jax.ShapeDtypeStruct((T, widths[j]), grad_dtypes[j]) for j in range(nr)]
    out_specs += [pl.BlockSpec(p.shape, lambda i: (0, 0)) for p in params]
    out_shape += [jax.ShapeDtypeStruct(p.shape, F32) for p in params]
    res = pl.pallas_call(
        body, name=name, grid=(T // tl,), in_specs=specs, out_specs=out_specs, out_shape=out_shape,
        compiler_params=_cparams("arbitrary"),
    )(*arrs)
    return list(res[:nr]), list(res[nr:])


def rms_fn(h, g):
    return (_rms(h, g),)


def merge_fn(ysb, yssm, z, ymla, g_sb, g_ssm, g_mla):
    ya = _rms(ysb, g_sb)
    yb = _rms(yssm * (z * _sigmoid(z)), g_ssm)
    yc = _rms(ymla, g_mla)
    return ya, yb, yc


def mla_prep_fn(cq, ckv, kra, krb, cos, sin, qn, kvn, wqp, wqr, wkp, wvp):
    cos4 = jnp.concatenate([cos] * MLA_HEADS, axis=1)
    sin4 = jnp.concatenate([sin] * MLA_HEADS, axis=1)
    nq = _rms(cq, qn)
    q = (mm_nn(nq, wqp) * cos4 + mm_nn(nq, wqr) * sin4) * MLA_SCALE
    nkv = _rms(ckv, kvn)
    kpe = kra * cos + krb * sin
    k = mm_nn(nkv, wkp) + jnp.concatenate([kpe] * MLA_HEADS, axis=1)
    v = mm_nn(nkv, wvp)
    return q, k, v


HALO = 8


def _prev_halo_spec(tl, tc, col_of):
    return pl.BlockSpec((HALO, tc), lambda i, j: (jnp.maximum(i * (tl // HALO) - 1, 0), col_of(j)))


def _fill_prev(buf, x_ref, halo_ref, i):
    buf[0:HALO, :] = jnp.where(i > 0, halo_ref[...].astype(F32), 0.0)
    buf[HALO:, :] = x_ref[...].astype(F32)


def _conv_from(buf, w_ref, b_ref, K, tl):
    acc = b_ref[...].astype(F32) + jnp.zeros((tl, buf.shape[1]), F32)
    for k in range(K):
        acc = acc + buf[pl.ds(HALO - (K - 1 - k), tl), :] * w_ref[k:k + 1, :].astype(F32)
    return acc


def ssm_conv_act(proj, w, b, *, name, tile=512, tc=256):
    T = proj.shape[0]
    K, C = w.shape
    tl = min(T, tile)
    c0 = OFF_XBC // tc
    nb = C // tc
    hd = SSM_DIM
    per = tc // hd

    def body(*refs):
        xs, halos = refs[0:nb], refs[nb:2 * nb]
        w_ref, b_ref = refs[2 * nb:2 * nb + 2]
        x_out, b_out, c_out = refs[2 * nb + 2:2 * nb + 5]
        bufs = refs[2 * nb + 5:]
        for j in range(nb):
            cols = slice(j * tc, (j + 1) * tc)
            _fill_prev(bufs[j], xs[j], halos[j], pl.program_id(0))
            u = b_ref[:, cols].astype(F32) + jnp.zeros((tl, tc), F32)
            for k in range(K):
                u = u + bufs[j][pl.ds(HALO - (K - 1 - k), tl), :] * w_ref[k:k + 1, cols].astype(F32)
            act = u * _sigmoid(u)
            for hh in range(per):
                piece = act[:, hh * hd:(hh + 1) * hd]
                head = j * per + hh
                if head < SSM_HEADS:
                    x_out[head] = piece
                elif head < SSM_HEADS + SSM_GROUPS:
                    b_out[head - SSM_HEADS] = piece
                else:
                    c_out[head - SSM_HEADS - SSM_GROUPS] = piece

    in_specs = ([pl.BlockSpec((tl, tc), lambda i, j=j: (i, c0 + j)) for j in range(nb)]
                + [pl.BlockSpec((HALO, tc), lambda i, j=j: (jnp.maximum(i * (tl // HALO) - 1, 0), c0 + j)) for j in range(nb)]
                + [pl.BlockSpec((K, C), lambda i: (0, 0)), pl.BlockSpec((1, C), lambda i: (0, 0))])
    return pl.pallas_call(
        body, name=name, grid=(T // tl,), in_specs=in_specs,
        out_specs=[pl.BlockSpec((SSM_HEADS, tl, hd), lambda i: (0, i, 0)),
                   pl.BlockSpec((SSM_GROUPS, tl, hd), lambda i: (0, i, 0)),
                   pl.BlockSpec((SSM_GROUPS, tl, hd), lambda i: (0, i, 0))],
        out_shape=[jax.ShapeDtypeStruct((SSM_HEADS, T, hd), F32), jax.ShapeDtypeStruct((SSM_GROUPS, T, hd), F32),
                   jax.ShapeDtypeStruct((SSM_GROUPS, T, hd), F32)],
        scratch_shapes=[pltpu.VMEM((tl + HALO, tc), F32)] * nb,
        compiler_params=_cparams("parallel"),
    )(*([proj] * (2 * nb)), w, b)


def ssm_conv_bwd_a(proj, w, b, d_out, *, name, tile=512, tc=256):
    T = proj.shape[0]
    K, C = w.shape
    tl = min(T, tile)
    c0 = OFF_XBC // tc

    def body(x_ref, halo_ref, w_ref, b_ref, d_ref, o_ref, buf):
        _fill_prev(buf, x_ref, halo_ref, pl.program_id(0))
        u = _conv_from(buf, w_ref, b_ref, K, tl)
        s = _sigmoid(u)
        o_ref[...] = d_ref[...].astype(F32) * (s * (1.0 + u * (1.0 - s)))

    return pl.pallas_call(
        body, name=name, grid=(T // tl, C // tc),
        in_specs=[pl.BlockSpec((tl, tc), lambda i, j: (i, c0 + j)), _prev_halo_spec(tl, tc, lambda j: c0 + j),
                  pl.BlockSpec((K, tc), lambda i, j: (0, j)), pl.BlockSpec((1, tc), lambda i, j: (0, j)),
                  pl.BlockSpec((tl, tc), lambda i, j: (i, j))],
        out_specs=pl.BlockSpec((tl, tc), lambda i, j: (i, j)),
        out_shape=jax.ShapeDtypeStruct((T, C), F32),
        scratch_shapes=[pltpu.VMEM((tl + HALO, tc), F32)],
        compiler_params=_cparams("parallel", "parallel"),
    )(proj, proj, w, b, d_out)


def ffn_act(up, w, b, *, name, tile=512, tc=1408):
    T = up.shape[0]
    K = w.shape[0]
    tl = min(T, tile)
    nj = D_FF // tc

    def body(xg_ref, hg_ref, xv_ref, hv_ref, wg_ref, wv_ref, bg_ref, bv_ref, o_ref, bufg, bufv):
        i = pl.program_id(0)
        _fill_prev(bufg, xg_ref, hg_ref, i)
        _fill_prev(bufv, xv_ref, hv_ref, i)
        gate = _conv_from(bufg, wg_ref, bg_ref, K, tl)
        val = _conv_from(bufv, wv_ref, bv_ref, K, tl)
        o_ref[...] = (gate * _sigmoid(gate) * val).astype(o_ref.dtype)

    return pl.pallas_call(
        body, name=name, grid=(T // tl, nj),
        in_specs=[pl.BlockSpec((tl, tc), lambda i, j: (i, j)), _prev_halo_spec(tl, tc, lambda j: j),
                  pl.BlockSpec((tl, tc), lambda i, j: (i, nj + j)), _prev_halo_spec(tl, tc, lambda j: nj + j),
                  pl.BlockSpec((K, tc), lambda i, j: (0, j)), pl.BlockSpec((K, tc), lambda i, j: (0, nj + j)),
                  pl.BlockSpec((1, tc), lambda i, j: (0, j)), pl.BlockSpec((1, tc), lambda i, j: (0, nj + j))],
        out_specs=pl.BlockSpec((tl, tc), lambda i, j: (i, j)),
        out_shape=jax.ShapeDtypeStruct((T, D_FF), BF16),
        scratch_shapes=[pltpu.VMEM((tl + HALO, tc), F32), pltpu.VMEM((tl + HALO, tc), F32)],
        compiler_params=_cparams("parallel", "parallel"),
    )(up, up, up, up, w, w, b, b)


def ffn_bwd_fused(up, w, b, d_act, *, name, tile=1024, tc=256, exchange=None):
    T = up.shape[0]
    K = w.shape[0]
    tl = min(T, tile)
    nj = D_FF // tc
    nblk = T // HALO
    ext = tl + HALO

    def body(xg, hgp, hgn, xv, hvp, hvn, wg, wv, bg, bv, d, dn, og, ov, dwg, dwv, dbg, dbv, bufg, bufv, dgb, dvb):
        i = pl.program_id(1)
        last = pl.num_programs(1) - 1

        def fill(buf, x_ref, prev_ref, next_ref):
            buf[0:HALO, :] = jnp.where(i > 0, prev_ref[...].astype(F32), 0.0)
            buf[HALO:HALO + tl, :] = x_ref[...].astype(F32)
            buf[HALO + tl:, :] = jnp.where(i < last, next_ref[...].astype(F32), 0.0)

        def conv_ext(buf, w_ref, b_ref):
            acc = b_ref[...].astype(F32) + jnp.zeros((ext, tc), F32)
            for k in range(K):
                acc = acc + buf[pl.ds(HALO - (K - 1 - k), ext), :] * w_ref[k:k + 1, :].astype(F32)
            return acc

        fill(bufg, xg, hgp, hgn)
        fill(bufv, xv, hvp, hvn)
        gate = conv_ext(bufg, wg, bg)
        val = conv_ext(bufv, wv, bv)
        dd = jnp.concatenate([d[...].astype(F32), jnp.where(i < last, dn[...].astype(F32)[0:HALO], 0.0)], axis=0)
        s = _sigmoid(gate)
        dgb[...] = dd * val * (s * (1.0 + gate * (1.0 - s)))
        dvb[...] = dd * (gate * s)

        @pl.when(i == 0)
        def _():
            for ref in (dwg, dwv, dbg, dbv):
                ref[...] = jnp.zeros_like(ref)

        for dbuf, xbuf, w_ref, o_ref, dw_ref, db_ref in ((dgb, bufg, wg, og, dwg, dbg), (dvb, bufv, wv, ov, dwv, dbv)):
            cur = dbuf[0:tl, :]
            dx = jnp.zeros((tl, tc), F32)
            for k in range(K):
                sft = K - 1 - k
                dx = dx + dbuf[pl.ds(sft, tl), :] * w_ref[k:k + 1, :].astype(F32)
                dw_ref[k:k + 1, :] += jnp.sum(cur * xbuf[pl.ds(HALO - sft, tl), :], axis=0, keepdims=True)
            db_ref[...] += jnp.sum(cur, axis=0, keepdims=True)
            o_ref[...] = dx.astype(o_ref.dtype)

    prev = lambda i: jnp.maximum(i * (tl // HALO) - 1, 0)
    nxt = lambda i: jnp.minimum((i + 1) * (tl // HALO), nblk - 1)

    def x_specs(col):
        return [pl.BlockSpec((tl, tc), lambda j, i: (i, col(j))), pl.BlockSpec((HALO, tc), lambda j, i: (prev(i), col(j))),
                pl.BlockSpec((HALO, tc), lambda j, i: (nxt(i), col(j)))]

    gcol, vcol = (lambda j: j), (lambda j: nj + j)
    in_specs = (x_specs(gcol) + x_specs(vcol)
                + [pl.BlockSpec((K, tc), lambda j, i: (0, j)), pl.BlockSpec((K, tc), lambda j, i: (0, nj + j)),
                   pl.BlockSpec((1, tc), lambda j, i: (0, j)), pl.BlockSpec((1, tc), lambda j, i: (0, nj + j)),
                   pl.BlockSpec((tl, tc), lambda j, i: (i, j)),
                   pl.BlockSpec((2 * HALO, tc), lambda j, i: (jnp.minimum((i + 1) * (tl // (2 * HALO)), nblk // 2 - 1), j))])
    row_out = pl.BlockSpec((tl, tc), lambda j, i: (i, j))
    w_out = pl.BlockSpec((K, tc), lambda j, i: (0, j))
    b_out = pl.BlockSpec((1, tc), lambda j, i: (0, j))
    return _call_with_exchange(
        body, exchange, name=name, grid=(nj, T // tl), in_specs=in_specs,
        out_specs=[row_out, row_out, w_out, w_out, b_out, b_out],
        out_shape=[jax.ShapeDtypeStruct((T, D_FF), BF16)] * 2 + [jax.ShapeDtypeStruct((K, D_FF), F32)] * 2
        + [jax.ShapeDtypeStruct((1, D_FF), F32)] * 2,
        scratch_shapes=[pltpu.VMEM((tl + 2 * HALO, tc), F32)] * 2 + [pltpu.VMEM((ext, tc), F32)] * 2,
        args=(up, up, up, up, up, up, w, w, b, b, d_act, d_act))


def conv_bwd_b(du, x, x_off, w, *, name, out_dtype, tile=512, tc=256):
    T, C = du.shape
    K = w.shape[0]
    tl = min(T, tile)
    c0 = x_off // tc
    nblk = T // HALO

    def body(du_ref, nx_ref, x_ref, w_ref, dx_ref, dw_ref, db_ref, dbuf):
        i = pl.program_id(1)
        last = pl.num_programs(1) - 1
        d = du_ref[...].astype(F32)
        dbuf[0:tl, :] = d
        dbuf[tl:, :] = jnp.where(i < last, nx_ref[...].astype(F32), 0.0)

        @pl.when(i == 0)
        def _():
            dw_ref[...] = jnp.zeros_like(dw_ref)
            db_ref[...] = jnp.zeros_like(db_ref)

        xin = x_ref[...].astype(F32)
        dx = jnp.zeros((tl, tc), F32)
        for k in range(K):
            s = K - 1 - k
            shifted = dbuf[pl.ds(s, tl), :]
            dx = dx + shifted * w_ref[k:k + 1, :].astype(F32)
            dw_ref[k:k + 1, :] += jnp.sum(shifted * xin, axis=0, keepdims=True)
        db_ref[...] += jnp.sum(d, axis=0, keepdims=True)
        dx_ref[...] = dx.astype(dx_ref.dtype)

    return pl.pallas_call(
        body, name=name, grid=(C // tc, T // tl),
        in_specs=[pl.BlockSpec((tl, tc), lambda j, i: (i, j)),
                  pl.BlockSpec((HALO, tc), lambda j, i: (jnp.minimum((i + 1) * (tl // HALO), nblk - 1), j)),
                  pl.BlockSpec((tl, tc), lambda j, i: (i, c0 + j)),
                  pl.BlockSpec((K, tc), lambda j, i: (0, j))],
        out_specs=[pl.BlockSpec((tl, tc), lambda j, i: (i, j)), pl.BlockSpec((K, tc), lambda j, i: (0, j)),
                   pl.BlockSpec((1, tc), lambda j, i: (0, j))],
        out_shape=[jax.ShapeDtypeStruct((T, C), out_dtype), jax.ShapeDtypeStruct((K, C), F32),
                   jax.ShapeDtypeStruct((1, C), F32)],
        scratch_shapes=[pltpu.VMEM((tl + HALO, tc), F32)],
        compiler_params=_cparams("parallel", "arbitrary"),
    )(du, du, x, w)


SB_QUERIES = 1024


def _attn_tiles(T, keys=256, queries=1024):
    return min(T, queries), min(T, keys)


def _after_diag(keys, queries, strict):
    d = lax.broadcasted_iota(jnp.int32, (keys, queries), 1) - lax.broadcasted_iota(jnp.int32, (keys, queries), 0)
    return d > 0 if strict else d >= 0


def _log_gates(z):
    l1p = jnp.log(1.0 + jnp.exp(-jnp.abs(z)))
    a = jnp.minimum(z, 0.0) - l1p
    return a, a - z


def _causal_sweep(i, tq, tk, block, descending, keep_going=None, first_block=None):
    nb = tq // tk
    n_full = i * nb

    def band():
        order = reversed(range(nb)) if descending else range(nb)
        for bb in order:
            block(pl.multiple_of(i * tq + bb * tk, tk), bb * tk, True)

    def full():
        if descending and keep_going is not None:
            def step(j):
                block(pl.multiple_of((n_full - 1 - j) * tk, tk), 0, False)
                return j + 1
            done = lax.while_loop(lambda j: jnp.logical_and(j < n_full, keep_going()), step, jnp.int32(0))
            return n_full - done

        def step(j, c):
            kb = (n_full - 1 - j) if descending else j
            block(pl.multiple_of(kb * tk, tk), 0, False)
            return c
        lax.fori_loop(0 if first_block is None else first_block, n_full, step, 0)
        return None

    if descending:
        band()
        return full()
    full()
    band()
    return None


def sb_fwd(q, k, v, *, name):
    H, dh, T = q.shape
    tq, tk = _attn_tiles(T, queries=SB_QUERIES)

    def body(q_ref, k_ref, v_ref, y_ref, bt_ref, first_ref, acc, run):
        acc[...] = jnp.zeros_like(acc)
        run[...] = jnp.zeros_like(run)
        u_after = _tri(tk, lambda r, c: r < c)

        def block(k0, r0, masked):
            kb = k_ref[pl.ds(k0, tk), :]
            vb = v_ref[pl.ds(k0, tk), :]
            z = jnp.dot(kb, q_ref[:, r0:], preferred_element_type=F32)
            a, b = _log_gates(z)
            if masked:
                valid = _after_diag(tk, tq - r0, True)
                b = jnp.where(valid, b, 0.0)
            w = jnp.exp(a + _tri_dot(u_after, b, 2) + run[:, r0:])
            if masked:
                w = jnp.where(valid, w, 0.0)
            acc[:, r0:] += lax.dot_general(vb, w.astype(BF16), TN, preferred_element_type=F32)
            run[:, r0:] += jnp.sum(b, axis=0, keepdims=True)

        first = _causal_sweep(pl.program_id(1), tq, tk, block, descending=True,
                              keep_going=lambda: jnp.max(run[...]) >= SB_ZERO_BELOW)
        y_ref[...] = acc[...]
        bt_ref[...] = run[...]
        first_ref[...] = jnp.zeros(first_ref.shape, F32) + first.astype(F32)

    return pl.pallas_call(
        body, name=name, grid=(H, T // tq),
        in_specs=[pl.BlockSpec((None, dh, tq), lambda h, i: (h, 0, i)),
                  pl.BlockSpec((None, T, dh), lambda h, i: (h, 0, 0)),
                  pl.BlockSpec((None, T, dh), lambda h, i: (h, 0, 0))],
        out_specs=[pl.BlockSpec((None, dh, tq), lambda h, i: (h, 0, i)),
                   pl.BlockSpec((None, 1, tq), lambda h, i: (h, 0, i)),
                   pl.BlockSpec((None, None, HALO, LANES), lambda h, i: (h, i, 0, 0))],
        out_shape=[jax.ShapeDtypeStruct((H, dh, T), F32), jax.ShapeDtypeStruct((H, 1, T), F32),
                   jax.ShapeDtypeStruct((H, T // tq, HALO, LANES), F32)],
        scratch_shapes=[pltpu.VMEM((dh, tq), F32), pltpu.VMEM((1, tq), F32)],
        compiler_params=_cparams("parallel", "parallel"),
    )(q, k, v)


def sb_bwd(q, k, v, dy, btot, first, *, name, q_scale):
    H, dh, T = q.shape
    tq, tk = _attn_tiles(T, queries=SB_QUERIES)

    def body(q_ref, k_ref, v_ref, dy_ref, bt_ref, first_ref, dq_ref, dk_ref, dv_ref, dq, pb, pg, dyb):
        @pl.when(pl.program_id(1) == 0)
        def _():
            dk_ref[...] = jnp.zeros_like(dk_ref)
            dv_ref[...] = jnp.zeros_like(dv_ref)

        dq[...] = jnp.zeros_like(dq)
        pb[...] = jnp.zeros_like(pb)
        pg[...] = jnp.zeros_like(pg)
        dyb[...] = dy_ref[...].astype(BF16)
        u_upto = _tri(tk, lambda r, c: r >= c)
        u_before = _tri(tk, lambda r, c: r > c)

        def block(k0, r0, masked):
            kb = k_ref[pl.ds(k0, tk), :]
            vb = v_ref[pl.ds(k0, tk), :]
            qv = q_ref[:, r0:]
            dyv = dyb[:, r0:]
            z = jnp.dot(kb, qv, preferred_element_type=F32)
            a, b = _log_gates(z)
            if masked:
                valid = _after_diag(tk, tq - r0, True)
                b = jnp.where(valid, b, 0.0)
            w = jnp.exp(a + (bt_ref[:, r0:] - pb[:, r0:] - _tri_dot(u_upto, b, 2)))
            if masked:
                w = jnp.where(valid, w, 0.0)
            g = w * jnp.dot(vb, dyv, preferred_element_type=F32)
            dz = g - jnp.exp(a) * (g + pg[:, r0:] + _tri_dot(u_before, g, 2))
            if masked:
                dz = jnp.where(valid, dz, 0.0)
            dz = dz.astype(BF16)
            dq[:, r0:] += lax.dot_general(kb, dz, TN, preferred_element_type=F32)
            dk_ref[pl.ds(k0, tk), :] += lax.dot_general(dz, qv, NT, preferred_element_type=F32)
            dv_ref[pl.ds(k0, tk), :] += lax.dot_general(w.astype(BF16), dyv, NT, preferred_element_type=F32)
            pb[:, r0:] += jnp.sum(b, axis=0, keepdims=True)
            pg[:, r0:] += jnp.sum(g, axis=0, keepdims=True)

        i = pl.program_id(1)
        first = jnp.clip(jnp.max(first_ref[...]).astype(jnp.int32), 0, i * (tq // tk))
        _causal_sweep(i, tq, tk, block, descending=False, first_block=first)
        dq_ref[...] = dq[...] * q_scale

    return pl.pallas_call(
        body, name=name, grid=(H, T // tq),
        in_specs=[pl.BlockSpec((None, dh, tq), lambda h, i: (h, 0, i)),
                  pl.BlockSpec((None, T, dh), lambda h, i: (h, 0, 0)),
                  pl.BlockSpec((None, T, dh), lambda h, i: (h, 0, 0)),
                  pl.BlockSpec((None, dh, tq), lambda h, i: (h, 0, i)),
                  pl.BlockSpec((None, 1, tq), lambda h, i: (h, 0, i)),
                  pl.BlockSpec((None, None, HALO, LANES), lambda h, i: (h, i, 0, 0))],
        out_specs=[pl.BlockSpec((None, dh, tq), lambda h, i: (h, 0, i)),
                   pl.BlockSpec((None, T, dh), lambda h, i: (h, 0, 0)),
                   pl.BlockSpec((None, T, dh), lambda h, i: (h, 0, 0))],
        out_shape=[jax.ShapeDtypeStruct((H, dh, T), F32), jax.ShapeDtypeStruct((H, T, dh), F32),
                   jax.ShapeDtypeStruct((H, T, dh), F32)],
        scratch_shapes=[pltpu.VMEM((dh, tq), F32), pltpu.VMEM((1, tq), F32), pltpu.VMEM((1, tq), F32),
                        pltpu.VMEM((dh, tq), BF16)],
        compiler_params=_cparams("parallel", "arbitrary"),
    )(q, k, v, dy, btot, first)


NEG = -1e30
SB_ZERO_BELOW = -105.0
MLA_KEYS = 512


def _call_with_exchange(body, exchange, *, name, grid, in_specs, out_specs, out_shape, scratch_shapes, args):
    if exchange is None:
        return pl.pallas_call(body, name=name, grid=grid, in_specs=in_specs, out_specs=out_specs, out_shape=out_shape,
                              scratch_shapes=scratch_shapes, compiler_params=_cparams("parallel", "arbitrary"))(*args)
    kind, src = exchange
    n_in, n_out, n_scr = len(in_specs), len(out_specs), len(scratch_shapes)
    R = src.shape[-2]

    def wrapped(*refs):
        ins, src_ref = refs[:n_in], refs[n_in]
        outs, xout = refs[n_in + 1:n_in + 1 + n_out], refs[n_in + 1 + n_out]
        scr = refs[n_in + 2 + n_out:n_in + 2 + n_out + n_scr]
        start, finish = _direct_exchange(kind, src_ref, xout, *refs[-3:])
        step = pl.program_id(0) * pl.num_programs(1) + pl.program_id(1)
        pl.when(step == 0)(start)
        body(*ins, *outs, *scr)
        pl.when(step == pl.num_programs(0) * pl.num_programs(1) - 1)(finish)

    return pl.pallas_call(
        wrapped, name=name, grid=grid, in_specs=list(in_specs) + [HBM_SPEC], out_specs=list(out_specs) + [HBM_SPEC],
        out_shape=list(out_shape) + [jax.ShapeDtypeStruct((8, R, LANES), src.dtype)],
        scratch_shapes=list(scratch_shapes) + [pltpu.SemaphoreType.DMA((7,)), pltpu.SemaphoreType.DMA((7,)),
                                               pltpu.SemaphoreType.DMA],
        compiler_params=_cparams("arbitrary", "arbitrary"))(*args, src)


def mla_fwd(q, k, v, *, name, exchange=None):
    H, dk, T = q.shape
    dv = v.shape[1] // H
    tq, tk = _attn_tiles(T, MLA_KEYS)

    def body(q_ref, k_ref, v_ref, o_ref, l_ref, acc, m_s, l_s):
        acc[...] = jnp.zeros_like(acc)
        m_s[...] = jnp.full_like(m_s, NEG)
        l_s[...] = jnp.zeros_like(l_s)

        def block(k0, r0, masked):
            kb = k_ref[pl.ds(k0, tk), :]
            vb = v_ref[pl.ds(k0, tk), :]
            s = jnp.dot(kb, q_ref[:, r0:], preferred_element_type=F32)
            if masked:
                s = jnp.where(_after_diag(tk, tq - r0, False), s, NEG)
            m = m_s[:, r0:]
            m_new = jnp.maximum(m, jnp.max(s, axis=0, keepdims=True))
            p = jnp.exp(s - m_new)
            alpha = jnp.exp(m - m_new)
            l_s[:, r0:] = alpha * l_s[:, r0:] + jnp.sum(p, axis=0, keepdims=True)
            acc[:, r0:] = alpha * acc[:, r0:] + lax.dot_general(vb, p.astype(BF16), TN, preferred_element_type=F32)
            m_s[:, r0:] = m_new

        _causal_sweep(pl.program_id(1), tq, tk, block, descending=False)
        o_ref[...] = acc[...] / l_s[...]
        l_ref[...] = m_s[...] + jnp.log(l_s[...])

    return _call_with_exchange(
        body, exchange, name=name, grid=(H, T // tq),
        in_specs=[pl.BlockSpec((None, dk, tq), lambda h, i: (h, 0, i)),
                  pl.BlockSpec((T, dk), lambda h, i: (0, h)),
                  pl.BlockSpec((T, dv), lambda h, i: (0, h))],
        out_specs=[pl.BlockSpec((None, dv, tq), lambda h, i: (h, 0, i)),
                   pl.BlockSpec((None, 1, tq), lambda h, i: (h, 0, i))],
        out_shape=[jax.ShapeDtypeStruct((H, dv, T), F32), jax.ShapeDtypeStruct((H, 1, T), F32)],
        scratch_shapes=[pltpu.VMEM((dv, tq), F32), pltpu.VMEM((1, tq), F32), pltpu.VMEM((1, tq), F32)],
        args=(q, k, v))


def mla_bwd(q, k, v, do, o, lse, *, name, exchange=None):
    H, dk, T = q.shape
    dv = v.shape[1] // H
    tq, tk = _attn_tiles(T, MLA_KEYS)

    def body(q_ref, k_ref, v_ref, do_ref, o_ref, l_ref, dq_ref, dk_ref, dv_ref, dq, delta, dob):
        @pl.when(pl.program_id(1) == 0)
        def _():
            dk_ref[...] = jnp.zeros_like(dk_ref)
            dv_ref[...] = jnp.zeros_like(dv_ref)

        dq[...] = jnp.zeros_like(dq)
        dov = do_ref[...].astype(F32)
        dob[...] = dov.astype(BF16)
        delta[...] = jnp.sum(dov * o_ref[...], axis=0, keepdims=True)

        def block(k0, r0, masked):
            kb = k_ref[pl.ds(k0, tk), :]
            vb = v_ref[pl.ds(k0, tk), :]
            qv = q_ref[:, r0:]
            dov_b = dob[:, r0:]
            s = jnp.dot(kb, qv, preferred_element_type=F32)
            p = jnp.exp(s - l_ref[:, r0:])
            if masked:
                p = jnp.where(_after_diag(tk, tq - r0, False), p, 0.0)
            dp = jnp.dot(vb, dov_b, preferred_element_type=F32)
            ds = (p * (dp - delta[:, r0:])).astype(BF16)
            dq[:, r0:] += lax.dot_general(kb, ds, TN, preferred_element_type=F32)
            dk_ref[pl.ds(k0, tk), :] += lax.dot_general(ds, qv, NT, preferred_element_type=F32)
            dv_ref[pl.ds(k0, tk), :] += lax.dot_general(p.astype(BF16), dov_b, NT, preferred_element_type=F32)

        _causal_sweep(pl.program_id(1), tq, tk, block, descending=False)
        dq_ref[...] = dq[...]

    return _call_with_exchange(
        body, exchange, name=name, grid=(H, T // tq),
        in_specs=[pl.BlockSpec((None, dk, tq), lambda h, i: (h, 0, i)),
                  pl.BlockSpec((T, dk), lambda h, i: (0, h)),
                  pl.BlockSpec((T, dv), lambda h, i: (0, h)),
                  pl.BlockSpec((None, dv, tq), lambda h, i: (h, 0, i)),
                  pl.BlockSpec((None, dv, tq), lambda h, i: (h, 0, i)),
                  pl.BlockSpec((None, 1, tq), lambda h, i: (h, 0, i))],
        out_specs=[pl.BlockSpec((None, dk, tq), lambda h, i: (h, 0, i)),
                   pl.BlockSpec((T, dk), lambda h, i: (0, h)),
                   pl.BlockSpec((T, dv), lambda h, i: (0, h))],
        out_shape=[jax.ShapeDtypeStruct((H, dk, T), F32), jax.ShapeDtypeStruct((T, H * dk), F32),
                   jax.ShapeDtypeStruct((T, H * dv), F32)],
        scratch_shapes=[pltpu.VMEM((dk, tq), F32), pltpu.VMEM((1, tq), F32), pltpu.VMEM((dv, tq), BF16)],
        args=(q, k, v, do, o, lse))


def _lane_pick(x, h):
    lane = lax.broadcasted_iota(jnp.int32, (1, x.shape[1]), 1)
    return jnp.sum(jnp.where(lane == h, x, 0.0), axis=1, keepdims=True)


def _row_pick(x, h):
    sub = lax.broadcasted_iota(jnp.int32, (x.shape[0], 1), 0)
    return jnp.sum(jnp.where(sub == h, x, 0.0), axis=0, keepdims=True)


def ssd_chunk_fn(*args):
    nh, ng = SSM_HEADS, SSM_GROUPS
    xs = args[:nh]
    bs = args[nh:nh + ng]
    cs = args[nh + ng:nh + 2 * ng]
    dt_raw = args[nh + 2 * ng]
    st = args[nh + 2 * ng + 1:nh + 2 * ng + 1 + nh]
    dt_bias, a_log, d_skip = args[nh + 2 * ng + 1 + nh:]
    L = dt_raw.shape[0]
    dt = _softplus(dt_raw + dt_bias)
    da = dt * (-jnp.exp(a_log))
    dcs = csum_rows(da)
    dcs_t = dcs.T
    total = jnp.sum(da, axis=0, keepdims=True)
    causal = lax.broadcasted_iota(jnp.int32, (L, L), 0) >= lax.broadcasted_iota(jnp.int32, (L, L), 1)
    cb = [mm_nt(cs[g], bs[g]) for g in range(ng)]
    ys, new_st = [], []
    for h in range(nh):
        g = h // (nh // ng)
        dcs_h = _lane_pick(dcs, h)
        dt_h = _lane_pick(dt, h)
        tot_h = _lane_pick(total, h)
        dsk_h = _lane_pick(d_skip, h)
        decay = jnp.exp(jnp.where(causal, dcs_h - _row_pick(dcs_t, h), NEG))
        xdt = xs[h] * dt_h
        y = mm_nn(cb[g] * decay, xdt)
        y = y + mm_nn(cs[g] * jnp.exp(dcs_h), st[h])
        ys.append(y + xs[h] * dsk_h)
        new_st.append(st[h] * jnp.exp(tot_h) + mm_tn(bs[g] * jnp.exp(tot_h - dcs_h), xdt))
    return tuple(ys) + tuple(new_st)


def ssd_fwd(x_hm, b_hm, c_hm, proj, dt_bias, a_log, d_skip, *, name):
    nh, T, P = x_hm.shape
    ng, N = b_hm.shape[0], b_hm.shape[2]
    L = SSM_CHUNK
    nc = T // L
    dtb = OFF_DT // LANES

    def body(x_ref, b_ref, c_ref, dt_ref, db_ref, al_ref, ds_ref, y_ref, s_ref, state):
        @pl.when(pl.program_id(0) == 0)
        def _():
            state[...] = jnp.zeros_like(state)

        s_ref[...] = state[...]
        args = ([x_ref[h] for h in range(nh)] + [b_ref[g] for g in range(ng)] + [c_ref[g] for g in range(ng)]
                + [dt_ref[...]] + [state[h] for h in range(nh)] + [db_ref[...], al_ref[...], ds_ref[...]])
        res = ssd_chunk_fn(*args)
        for h in range(nh):
            y_ref[:, h * P:(h + 1) * P] = res[h]
            state[h] = res[nh + h]

    par = pl.BlockSpec((1, LANES), lambda i: (0, 0))
    return pl.pallas_call(
        body, name=name, grid=(nc,),
        in_specs=[pl.BlockSpec((nh, L, P), lambda i: (0, i, 0)), pl.BlockSpec((ng, L, N), lambda i: (0, i, 0)),
                  pl.BlockSpec((ng, L, N), lambda i: (0, i, 0)), pl.BlockSpec((L, LANES), lambda i: (i, dtb)),
                  par, par, par],
        out_specs=[pl.BlockSpec((L, nh * P), lambda i: (i, 0)),
                   pl.BlockSpec((None, nh, N, P), lambda i: (i, 0, 0, 0))],
        out_shape=[jax.ShapeDtypeStruct((T, nh * P), F32), jax.ShapeDtypeStruct((nc, nh, N, P), F32)],
        scratch_shapes=[pltpu.VMEM((nh, N, P), F32)],
        compiler_params=_cparams("arbitrary"),
    )(x_hm, b_hm, c_hm, proj, dt_bias, a_log, d_skip)


def ssd_bwd(x_hm, b_hm, c_hm, proj, states, dt_bias, a_log, d_skip, dy, *, name):
    nh, T, P = x_hm.shape
    ng, N = b_hm.shape[0], b_hm.shape[2]
    L = SSM_CHUNK
    nc = T // L
    dtb = OFF_DT // LANES

    def body(x_ref, b_ref, c_ref, dt_ref, s_ref, db_ref, al_ref, ds_ref, dy_ref,
             dxbc_ref, ddt_ref, gdb_ref, gal_ref, gds_ref, dstate):
        @pl.when(pl.program_id(0) == 0)
        def _():
            dstate[...] = jnp.zeros_like(dstate)
            gdb_ref[...] = jnp.zeros_like(gdb_ref)
            gal_ref[...] = jnp.zeros_like(gal_ref)
            gds_ref[...] = jnp.zeros_like(gds_ref)

        args = ([x_ref[h] for h in range(nh)] + [b_ref[g] for g in range(ng)] + [c_ref[g] for g in range(ng)]
                + [dt_ref[...]] + [s_ref[h] for h in range(nh)] + [db_ref[...], al_ref[...], ds_ref[...]])
        _, vjp = jax.vjp(ssd_chunk_fn, *args)
        g = vjp(tuple([dy_ref[h] for h in range(nh)] + [dstate[h] for h in range(nh)]))
        for j in range(nh + 2 * ng):
            dxbc_ref[:, j * P:(j + 1) * P] = g[j]
        ddt_ref[...] = g[nh + 2 * ng]
        for h in range(nh):
            dstate[h] = g[nh + 2 * ng + 1 + h]
        gdb_ref[...] += g[-3]
        gal_ref[...] += g[-2]
        gds_ref[...] += g[-1]

    rev = lambda i: nc - 1 - i
    par = pl.BlockSpec((1, LANES), lambda i: (0, 0))
    return pl.pallas_call(
        body, name=name, grid=(nc,),
        in_specs=[pl.BlockSpec((nh, L, P), lambda i: (0, rev(i), 0)), pl.BlockSpec((ng, L, N), lambda i: (0, rev(i), 0)),
                  pl.BlockSpec((ng, L, N), lambda i: (0, rev(i), 0)), pl.BlockSpec((L, LANES), lambda i: (rev(i), dtb)),
                  pl.BlockSpec((None, nh, N, P), lambda i: (rev(i), 0, 0, 0)), par, par, par,
                  pl.BlockSpec((nh, L, P), lambda i: (0, rev(i), 0))],
        out_specs=[pl.BlockSpec((L, (nh + 2 * ng) * P), lambda i: (rev(i), 0)),
                   pl.BlockSpec((L, LANES), lambda i: (rev(i), 0)), par, par, par],
        out_shape=[jax.ShapeDtypeStruct((T, (nh + 2 * ng) * P), F32), jax.ShapeDtypeStruct((T, LANES), F32),
                   jax.ShapeDtypeStruct((1, LANES), F32), jax.ShapeDtypeStruct((1, LANES), F32),
                   jax.ShapeDtypeStruct((1, LANES), F32)],
        scratch_shapes=[pltpu.VMEM((nh, N, P), F32)],
        compiler_params=_cparams("arbitrary"),
    )(x_hm, b_hm, c_hm, proj, states, dt_bias, a_log, d_skip, dy)


def loss_head(h, target, g, *, name, tile=512):
    T, C = h.shape
    tl = min(T, tile)

    def body(h_ref, t_ref, g_ref, dh_ref, dg_ref, ls_ref):
        @pl.when(pl.program_id(0) == 0)
        def _():
            dg_ref[...] = jnp.zeros_like(dg_ref)
            ls_ref[...] = jnp.zeros_like(ls_ref)

        (y,), vjp = jax.vjp(rms_fn, h_ref[...], g_ref[...])
        err = y - t_ref[...]
        ls_ref[...] += jnp.sum(err * err, axis=0, keepdims=True) * (0.5 / C)
        dh, dg = vjp((err * (1.0 / C),))
        dh_ref[...] = dh
        dg_ref[...] += dg

    row = pl.BlockSpec((tl, C), lambda i: (i, 0))
    par = pl.BlockSpec((1, C), lambda i: (0, 0))
    return pl.pallas_call(
        body, name=name, grid=(T // tl,), in_specs=[row, row, par], out_specs=[row, par, par],
        out_shape=[jax.ShapeDtypeStruct((T, C), F32), jax.ShapeDtypeStruct((1, C), F32),
                   jax.ShapeDtypeStruct((1, C), F32)],
        compiler_params=_cparams("arbitrary"),
    )(h, target, g)


def adamw(w, g, m, v, *, name):
    R, C = w.shape
    tr = R
    for d in range(8, min(R, 512) + 1, 8):
        if R % d == 0:
            tr = d
    c1 = 1.0 - ADAM_B1 ** ADAM_STEP
    c2 = 1.0 - ADAM_B2 ** ADAM_STEP

    def body(w_ref, g_ref, m_ref, v_ref, d_ref, nm_ref, nv_ref):
        gv = g_ref[...]
        nm = ADAM_B1 * m_ref[...] + (1.0 - ADAM_B1) * gv
        nv = ADAM_B2 * v_ref[...] + (1.0 - ADAM_B2) * (gv * gv)
        d_ref[...] = -ADAM_LR * ((nm / c1) / (jnp.sqrt(nv / c2) + ADAM_EPS) + ADAM_WD * w_ref[...])
        nm_ref[...] = nm
        nv_ref[...] = nv

    spec = pl.BlockSpec((tr, C), lambda i: (i, 0))
    return pl.pallas_call(
        body, name=name, grid=(R // tr,), in_specs=[spec] * 4, out_specs=[spec] * 3,
        out_shape=[jax.ShapeDtypeStruct((R, C), F32)] * 3,
        compiler_params=_cparams("parallel"),
    )(w, g, m, v)


MESH = pl.DeviceIdType.MESH
HBM_SPEC = pl.BlockSpec(memory_space=pltpu.HBM)


def _place():
    return lax.axis_index("x"), lax.axis_index("y"), lax.axis_index("c")


def _direct_exchange(kind, src_ref, out_ref, send_sems, recv_sems, local_sem):
    x, y, c = _place()
    me = 4 * x + 2 * y + c

    def block_for(dest):
        return src_ref if kind == "gather" else src_ref.at[dest]

    own = pltpu.make_async_copy(block_for(me), out_ref.at[me], local_sem)
    sends, arrivals = [], []
    for f in range(1, 8):
        px = jnp.where((f >> 2) & 1, 1 - x, x)
        py = jnp.where((f >> 1) & 1, 1 - y, y)
        pc = jnp.where(f & 1, 1 - c, c)
        peer = 4 * px + 2 * py + pc
        for dst, group in ((out_ref.at[me], sends), (out_ref.at[peer], arrivals)):
            group.append(pltpu.make_async_remote_copy(
                src_ref=block_for(peer), dst_ref=dst, send_sem=send_sems.at[f - 1], recv_sem=recv_sems.at[f - 1],
                device_id=(px, py, pc), device_id_type=MESH))

    def start():
        own.start()
        for cp in sends:
            cp.start()

    def finish():
        for cp in arrivals:
            cp.wait_recv()
        for cp in sends:
            cp.wait_send()
        own.wait()

    return start, finish


def allgather_blocks(mine, *, name):
    R = mine.shape[0]

    def body(x_ref, out_ref, send_sems, recv_sems, local_sem):
        x, y, c = _place()
        me, sibling = (x, y, c), (x, y, 1 - c)
        chips = [(1 - x, y), (x, 1 - y), (1 - x, 1 - y)]

        def slot(px, py, pc):
            return out_ref.at[4 * px + 2 * py + pc]

        def copy(k, block, to, src=None):
            return pltpu.make_async_remote_copy(
                src_ref=slot(*block) if src is None else src, dst_ref=slot(*block),
                send_sem=send_sems.at[k], recv_sem=recv_sems.at[k], device_id=to, device_id_type=MESH)

        own = pltpu.make_async_copy(x_ref, slot(*me), local_sem)
        own.start()
        first = [copy(0, me, sibling, src=x_ref)]
        first += [copy(1 + j, me, (*chip, c), src=x_ref) for j, chip in enumerate(chips)]
        for cp in first:
            cp.start()
        passed = [copy(4 + j, (*chip, c), sibling) for j, chip in enumerate(chips)]
        for j, chip in enumerate(chips):
            copy(1 + j, (*chip, c), me).wait_recv()
            passed[j].start()
        copy(0, sibling, me).wait_recv()
        for j, chip in enumerate(chips):
            copy(4 + j, (*chip, 1 - c), me).wait_recv()
        for cp in first + passed:
            cp.wait_send()
        own.wait()

    return pl.pallas_call(
        body, name=name, out_shape=jax.ShapeDtypeStruct((8, R, LANES), mine.dtype),
        in_specs=[HBM_SPEC], out_specs=HBM_SPEC,
        scratch_shapes=[pltpu.SemaphoreType.DMA((7,)), pltpu.SemaphoreType.DMA((7,)), pltpu.SemaphoreType.DMA],
    )(mine)


def allgather_direct(mine, *, name):
    R = mine.shape[0]

    def body(x_ref, out_ref, send_sems, recv_sems, local_sem):
        start, finish = _direct_exchange("gather", x_ref, out_ref, send_sems, recv_sems, local_sem)
        start()
        finish()

    return pl.pallas_call(
        body, name=name, out_shape=jax.ShapeDtypeStruct((8, R, LANES), mine.dtype),
        in_specs=[HBM_SPEC], out_specs=HBM_SPEC,
        scratch_shapes=[pltpu.SemaphoreType.DMA((7,)), pltpu.SemaphoreType.DMA((7,)), pltpu.SemaphoreType.DMA],
    )(mine)


def send_to_sibling(v, *, name):
    def body(v_ref, out_ref, send_sem, recv_sem):
        x, y, c = _place()
        cp = pltpu.make_async_remote_copy(src_ref=v_ref, dst_ref=out_ref, send_sem=send_sem, recv_sem=recv_sem,
                                          device_id=(x, y, 1 - c), device_id_type=MESH)
        cp.start()
        cp.wait()

    return pl.pallas_call(
        body, name=name, out_shape=jax.ShapeDtypeStruct(v.shape, v.dtype), in_specs=[HBM_SPEC], out_specs=HBM_SPEC,
        scratch_shapes=[pltpu.SemaphoreType.DMA, pltpu.SemaphoreType.DMA],
    )(v)


def chip_exchange(p, *, name):
    R = p.shape[1]

    def body(p_ref, out_ref, send_sems, recv_sems):
        x, y, c = _place()
        chips = [(1 - x, y), (x, 1 - y), (1 - x, 1 - y)]
        sends = [pltpu.make_async_remote_copy(
            src_ref=p_ref.at[2 * px + py], dst_ref=out_ref.at[j], send_sem=send_sems.at[j], recv_sem=recv_sems.at[j],
            device_id=(px, py, c), device_id_type=MESH) for j, (px, py) in enumerate(chips)]
        for cp in sends:
            cp.start()
        for cp in sends:
            cp.wait()

    return pl.pallas_call(
        body, name=name, out_shape=jax.ShapeDtypeStruct((3, R, LANES), p.dtype), in_specs=[HBM_SPEC],
        out_specs=HBM_SPEC,
        scratch_shapes=[pltpu.SemaphoreType.DMA((3,)), pltpu.SemaphoreType.DMA((3,))],
    )(p)


def add_blocks(terms, out_dtype, *, name, tile=1024):
    terms = [t if isinstance(t, tuple) else (t, None) for t in terms]
    R = terms[0][0].shape[-2]
    tr = R
    for d in range(16, min(R, tile) + 1, 16):
        if R % d == 0:
            tr = d

    def body(*refs):
        acc = refs[0][...].astype(F32)
        for ref in refs[1:-1]:
            acc = acc + ref[...].astype(F32)
        refs[-1][...] = acc.astype(out_dtype)

    spec = pl.BlockSpec((tr, LANES), lambda i: (i, 0))
    in_specs = [spec if slot is None else pl.BlockSpec((None, tr, LANES), lambda i, slot=slot: (slot, i, 0))
                for _, slot in terms]
    return pl.pallas_call(
        body, name=name, grid=(R // tr,), in_specs=in_specs, out_specs=spec,
        out_shape=jax.ShapeDtypeStruct((R, LANES), out_dtype), compiler_params=_cparams("parallel"),
    )(*[a for a, _ in terms])


FLAT_ROW_STEP = 640


def _half_rows(arr, cc):
    hr = arr.shape[0] // 2
    return lax.dynamic_slice_in_dim(arr, cc * hr, hr, axis=0).reshape(-1)


def _flat_half(shards, cc, dtype, names=BIG):
    flat = jnp.concatenate([_half_rows(shards[n], cc).astype(dtype) for n in names])
    rows = -(-flat.shape[0] // (FLAT_ROW_STEP * LANES)) * FLAT_ROW_STEP
    return jnp.pad(flat, (0, rows * LANES - flat.shape[0])).reshape(rows, LANES)


def _flat_rows(shapes, names=BIG):
    n = sum((shapes[m][0] // 2) * shapes[m][1] for m in names) // LANES
    return -(-n // FLAT_ROW_STEP) * FLAT_ROW_STEP


def _to_blocks(full, shapes, dtype, names=BIG):
    pieces = []
    for n in names:
        R, C = shapes[n]
        a = full[n].astype(dtype)
        if BIG_AXIS[n] == 2:
            a = a.reshape(2, R // 2, 4, C).transpose(2, 0, 1, 3)
        pieces.append(a.reshape(8, (R // 2) * C // LANES, LANES))
    flat = jnp.concatenate(pieces, axis=1)
    return jnp.pad(flat, ((0, 0), (0, _flat_rows(shapes, names) - flat.shape[1]), (0, 0)))


def _from_blocks(g8, shapes, names=BIG):
    out, off = {}, 0
    for n in names:
        R, C = shapes[n]
        rows = (R // 2) * C // LANES
        a = g8[:, off:off + rows, :].reshape(4, 2, R // 2, C)
        out[n] = a.transpose(1, 2, 0, 3).reshape(R, 4 * C) if BIG_AXIS[n] == 2 else a.reshape(4 * R, C)
        off += rows
    return out


def _unflat_halves(flat_by_c, shapes, names=BIG):
    out, off = {}, 0
    for n in names:
        R, C = shapes[n]
        sz = (R // 2) * C
        out[n] = jnp.concatenate([flat_by_c[c][off:off + sz].reshape(R // 2, C) for c in range(2)], axis=0)
        off += sz
    return out


def _to_heads(a, nh):
    T = a.shape[0]
    return a.reshape(T, nh, a.shape[1] // nh).transpose(1, 0, 2)


def _from_heads(a):
    nh, T, d = a.shape
    return a.transpose(1, 0, 2).reshape(T, nh * d)


def _to_heads_t(a, nh):
    T = a.shape[0]
    return a.reshape(T, nh, a.shape[1] // nh).transpose(1, 2, 0)


def _from_heads_t(a):
    nh, d, T = a.shape
    return a.transpose(2, 0, 1).reshape(T, nh * d)


def _pad_cols(a, n):
    return jnp.pad(a, ((0, 0), (0, n - a.shape[1])))


def _pack_w_in(w):
    offs = [sum(IN_SPLITS[:i]) for i in range(len(IN_SPLITS) + 1)]
    sb, z, xbc, dt, cq, ckv, kr = [w[:, offs[i]:offs[i + 1]] for i in range(len(IN_SPLITS))]
    zeros = lambda n: jnp.zeros((w.shape[0], n), w.dtype)
    h = MLA_ROPE // 2
    kra = jnp.concatenate([zeros(MLA_NOPE), kr, zeros(LANES - MLA_QK)], axis=1)
    krb = jnp.concatenate([zeros(MLA_NOPE), -kr[:, h:], kr[:, :h], zeros(LANES - MLA_QK)], axis=1)
    return sb, jnp.concatenate([z, xbc, cq, ckv, _pad_cols(dt, LANES), kra, krb], axis=1)


def _unpack_gw_in(g_sb, g):
    h = MLA_ROPE // 2
    ga, gb = g[:, OFF_KRA:OFF_KRA + LANES], g[:, OFF_KRB:OFF_KRB + LANES]
    gkr = ga[:, MLA_NOPE:MLA_QK] + jnp.concatenate([gb[:, MLA_NOPE + h:MLA_QK], -gb[:, MLA_NOPE:MLA_NOPE + h]], axis=1)
    return jnp.concatenate([g_sb, g[:, OFF_Z:OFF_Z + 512], g[:, OFF_XBC:OFF_XBC + 768],
                            g[:, OFF_DT:OFF_DT + 8], g[:, OFF_CQ:OFF_CQ + 256], g[:, OFF_CKV:OFF_CKV + 128], gkr], axis=1)


def _pack_w_uq(w):
    zeros = lambda n: jnp.zeros((w.shape[0], n), w.dtype)
    h = MLA_ROPE // 2
    pp, rr = [], []
    for i in range(MLA_HEADS):
        nope = w[:, MLA_QK * i:MLA_QK * i + MLA_NOPE]
        rope = w[:, MLA_QK * i + MLA_NOPE:MLA_QK * (i + 1)]
        pp += [nope, rope, zeros(LANES - MLA_QK)]
        rr += [zeros(MLA_NOPE), -rope[:, h:], rope[:, :h], zeros(LANES - MLA_QK)]
    return jnp.concatenate(pp, axis=1), jnp.concatenate(rr, axis=1)


def _unpack_gw_uq(gp, gr):
    h = MLA_ROPE // 2
    out = []
    for i in range(MLA_HEADS):
        b = LANES * i
        out.append(gp[:, b:b + MLA_NOPE])
        out.append(gp[:, b + MLA_NOPE:b + MLA_NOPE + h] + gr[:, b + MLA_NOPE + h:b + MLA_QK])
        out.append(gp[:, b + MLA_NOPE + h:b + MLA_QK] - gr[:, b + MLA_NOPE:b + MLA_NOPE + h])
    return jnp.concatenate(out, axis=1)


def _pack_w_ukv(w):
    zeros = lambda n: jnp.zeros((w.shape[0], n), w.dtype)
    kk, vv = [], []
    for i in range(MLA_HEADS):
        b = (MLA_NOPE + MLA_V) * i
        kk += [w[:, b:b + MLA_NOPE], zeros(LANES - MLA_NOPE)]
        vv += [w[:, b + MLA_NOPE:b + MLA_NOPE + MLA_V], zeros(LANES - MLA_V)]
    return jnp.concatenate(kk, axis=1), jnp.concatenate(vv, axis=1)


def _unpack_gw_ukv(gk, gv):
    out = []
    for i in range(MLA_HEADS):
        out += [gk[:, LANES * i:LANES * i + MLA_NOPE], gv[:, LANES * i:LANES * i + MLA_V]]
    return jnp.concatenate(out, axis=1)


def _rope_tables(positions):
    inv_freq = 1.0 / (ROPE_THETA ** (jnp.arange(0, MLA_ROPE, 2, dtype=F32) / MLA_ROPE))
    ang = positions.astype(F32)[:, None] * inv_freq
    cos, sin = jnp.cos(ang), jnp.sin(ang)
    T = positions.shape[0]
    one, zero = jnp.ones((T, MLA_NOPE), F32), jnp.zeros((T, MLA_NOPE), F32)
    pad1, pad0 = jnp.ones((T, LANES - MLA_QK), F32), jnp.zeros((T, LANES - MLA_QK), F32)
    return jnp.concatenate([one, cos, cos, pad1], axis=1), jnp.concatenate([zero, sin, sin, pad0], axis=1)


def _row(v):
    return v.reshape(1, -1)


def _pad_row(v):
    return _pad_cols(v.reshape(1, -1), LANES)


def _layer_weights(full, small, li):
    p = {}
    p["w_sb"], p["w_rest"] = _pack_w_in(full["w_in"])
    q_scale = jnp.concatenate([jnp.full((1, SB_HEADS * SB_DIM), SB_DIM ** -0.5, BF16),
                               jnp.ones((1, 2 * SB_HEADS * SB_DIM), BF16)], axis=1)
    p["w_sb_fwd"] = p["w_sb"] * q_scale
    p["wqp"], p["wqr"] = _pack_w_uq(full["mla_w_uq"])
    p["wkp"], p["wvp"] = _pack_w_ukv(full["mla_w_ukv"])
    p["w_out"] = full["w_out"]
    if "ffn_w_up" in full:
        p["w_up"] = full["ffn_w_up"]
        p["w_down"] = full["ffn_w_down"]
    for n in ("mix_norm", "sb_out_norm", "ssm_conv_b", "ssm_out_norm", "mla_q_norm", "mla_kv_norm", "mla_out_norm",
              "ffn_norm", "ffn_conv_b"):
        p[n] = _row(small[n][li])
    for n in ("ssm_dt_bias", "ssm_a_log", "ssm_d"):
        p[n] = _pad_row(small[n][li])
    p["ssm_conv_w"] = small["ssm_conv_w"][li]
    p["ffn_conv_w"] = small["ffn_conv_w"][li]
    return p


def _layer_fwd(h, p, cos, sin, li, exchange=None, late=None):
    T = h.shape[0]
    nm = lambda s: "l%d_%s" % (li, s)
    s = {"h": h}
    (n1,) = rowwise(rms_fn, [h], [p["mix_norm"]], [(D_MODEL, BF16)], name=nm("mix_norm"))
    proj = matmul(n1, p["w_rest"], name=nm("in_proj"))
    qkv = matmul(n1, p["w_sb_fwd"], name=nm("in_proj_sb"), out_dtype=BF16)
    s["n1"], s["proj"] = n1, proj
    s["sb_q"] = _to_heads_t(qkv[:, 0:256], SB_HEADS)
    s["sb_k"] = _to_heads(qkv[:, 256:512], SB_HEADS)
    s["sb_v"] = _to_heads(qkv[:, 512:768], SB_HEADS)
    y_sb_hm, s["sb_bt"], s["sb_first"] = sb_fwd(s["sb_q"], s["sb_k"], s["sb_v"], name=nm("sb_fwd"))
    s["y_sb"] = _from_heads_t(y_sb_hm)
    s["x_hm"], s["b_hm"], s["c_hm"] = ssm_conv_act(proj, p["ssm_conv_w"], p["ssm_conv_b"], name=nm("ssm_conv"))
    s["y_ssm"], s["states"] = ssd_fwd(s["x_hm"], s["b_hm"], s["c_hm"], proj, p["ssm_dt_bias"], p["ssm_a_log"],
                                      p["ssm_d"], name=nm("ssd_fwd"))
    rows = [(proj, 256, OFF_CQ // 256), (proj, 128, OFF_CKV // 128), (proj, 128, OFF_KRA // 128),
            (proj, 128, OFF_KRB // 128), cos, sin]
    qp, kp, vv = rowwise(mla_prep_fn, rows, [p["mla_q_norm"], p["mla_kv_norm"], p["wqp"], p["wqr"], p["wkp"], p["wvp"]],
                         [(512, BF16), (512, BF16), (512, BF16)], name=nm("mla_prep"))
    s["mla_q"], s["mla_k"], s["mla_v"] = _to_heads_t(qp, MLA_HEADS), kp, vv
    s["mla_o"], s["mla_lse"], *rode = mla_fwd(s["mla_q"], kp, vv, name=nm("mla_fwd"), exchange=exchange)
    if late is not None:
        p.update(late(rode[0]))
    s["y_mla"] = _from_heads_t(s["mla_o"][:, :MLA_V, :])
    (cat,) = rowwise(merge_fn, [s["y_sb"], s["y_ssm"], (proj, 512, OFF_Z // 512), s["y_mla"]],
                     [p["sb_out_norm"], p["ssm_out_norm"], p["mla_out_norm"]], [(D_MODEL, BF16)], name=nm("merge"),
                     post=lambda a, b, c: (jnp.concatenate([a, b, c], axis=1),))
    s["cat"] = cat
    h1 = matmul(cat, p["w_out"], name=nm("out_proj"), residual=h)
    s["h1"] = h1
    (n2,) = rowwise(rms_fn, [h1], [p["ffn_norm"]], [(D_MODEL, BF16)], name=nm("ffn_norm"))
    up = matmul(n2, p["w_up"], name=nm("ffn_up"))
    act = ffn_act(up, p["ffn_conv_w"], p["ffn_conv_b"], name=nm("ffn_act"))
    s["n2"], s["up"], s["act"] = n2, up, act
    h2 = matmul(act, p["w_down"], name=nm("ffn_down"), residual=h1)
    return h2, s, (rode[0] if rode else None)


def _layer_bwd(dh2, s, p, cos, sin, li, exchange=None, exchange_ffn=None):
    nm = lambda t: "l%d_%s" % (li, t)
    g = {}
    proj = s["proj"]
    g["ffn_w_down"] = matmul(s["act"], dh2, name=nm("g_w_down"), ta=True)
    d_act = matmul(dh2, p["w_down"], name=nm("d_act"), out_dtype=BF16, tb=True)
    d_up_g, d_up_v, gwg, gwv, gbg, gbv, *rode_ffn = ffn_bwd_fused(s["up"], p["ffn_conv_w"], p["ffn_conv_b"], d_act,
                                                                  name=nm("ffn_act_bwd"), exchange=exchange_ffn)
    g["ffn_conv_w"] = jnp.concatenate([gwg, gwv], axis=1)
    g["ffn_conv_b"] = jnp.concatenate([gbg[0], gbv[0]])
    g["ffn_w_up"] = jnp.concatenate([matmul(s["n2"], d_up_g, name=nm("g_w_up_gate"), ta=True),
                                     matmul(s["n2"], d_up_v, name=nm("g_w_up_val"), ta=True)], axis=1)
    d_n2 = matmul(d_up_g, p["w_up"], name=nm("d_n2_gate"), tb=True)
    d_n2 = matmul(d_up_v, p["w_up"], name=nm("d_n2_val"), tb=True, b_k0=D_FF, residual=d_n2)
    (dh1,), (gn,) = rowwise_bwd(rms_fn, [s["h1"]], [], [p["ffn_norm"]], [d_n2], [F32], name=nm("ffn_norm_bwd"),
                                add0=dh2)
    g["ffn_norm"] = gn[0]
    g["w_out"] = matmul(s["cat"], dh1, name=nm("g_w_out"), ta=True)
    d_cat = matmul(dh1, p["w_out"], name=nm("d_cat"), tb=True)
    (d_ysb, d_yssm, d_z, d_ymla), (g1, g2, g3) = rowwise_bwd(
        merge_fn, [s["y_sb"], s["y_ssm"], (proj, 512, OFF_Z // 512), s["y_mla"]], [],
        [p["sb_out_norm"], p["ssm_out_norm"], p["mla_out_norm"]], [d_cat], [F32, F32, BF16, F32], name=nm("merge_bwd"),
        pre_ct=lambda d: (d[:, 0:256], d[:, 256:768], d[:, 768:1024]))
    g["sb_out_norm"], g["ssm_out_norm"], g["mla_out_norm"] = g1[0], g2[0], g3[0]
    dq, dk, dv = sb_bwd(s["sb_q"], s["sb_k"], s["sb_v"], _to_heads_t(d_ysb, SB_HEADS), s["sb_bt"], s["sb_first"], name=nm("sb_bwd"),
                        q_scale=SB_DIM ** -0.5)
    d_sb = jnp.concatenate([_from_heads_t(dq), _from_heads(dk), _from_heads(dv)], axis=1).astype(BF16)
    if callable(exchange):
        exchange = exchange(g)
    do_t = jnp.pad(_to_heads_t(d_ymla, MLA_HEADS), ((0, 0), (0, LANES - MLA_V), (0, 0)))
    dqp, dkp, dvv, *rode = mla_bwd(s["mla_q"], s["mla_k"], s["mla_v"], do_t, s["mla_o"], s["mla_lse"],
                                   name=nm("mla_bwd"), exchange=exchange)
    rows = [(proj, 256, OFF_CQ // 256), (proj, 128, OFF_CKV // 128), (proj, 128, OFF_KRA // 128),
            (proj, 128, OFF_KRB // 128)]
    (d_cq, d_ckv, d_kra, d_krb), (gqn, gkvn, gwqp, gwqr, gwkp, gwvp) = rowwise_bwd(
        mla_prep_fn, rows, [cos, sin], [p["mla_q_norm"], p["mla_kv_norm"], p["wqp"], p["wqr"], p["wkp"], p["wvp"]],
        [_from_heads_t(dqp), dkp, dvv], [BF16] * 4, name=nm("mla_prep_bwd"), tile=256)
    g["mla_q_norm"], g["mla_kv_norm"] = gqn[0], gkvn[0]
    g["mla_w_uq"] = _unpack_gw_uq(gwqp, gwqr)
    g["mla_w_ukv"] = _unpack_gw_ukv(gwkp, gwvp)
    d_xbc_act, d_dt, gdb, gal, gds = ssd_bwd(
        s["x_hm"], s["b_hm"], s["c_hm"], proj, s["states"], p["ssm_dt_bias"], p["ssm_a_log"], p["ssm_d"],
        _to_heads(d_yssm, SSM_HEADS), name=nm("ssd_bwd"))
    g["ssm_dt_bias"], g["ssm_a_log"], g["ssm_d"] = gdb[0, :8], gal[0, :8], gds[0, :8]
    d_pre = ssm_conv_bwd_a(proj, p["ssm_conv_w"], p["ssm_conv_b"], d_xbc_act, name=nm("ssm_conv_bwd_a"))
    d_xbc, g["ssm_conv_w"], gscb = conv_bwd_b(d_pre, proj, OFF_XBC, p["ssm_conv_w"], name=nm("ssm_conv_bwd_b"),
                                              out_dtype=BF16, tc=256)
    g["ssm_conv_b"] = gscb[0]
    d_proj = jnp.concatenate([d_z, d_xbc, d_cq, d_ckv, d_dt.astype(BF16), d_kra, d_krb], axis=1)
    g["w_in"] = _unpack_gw_in(matmul(s["n1"], d_sb, name=nm("g_w_in_sb"), ta=True),
                              matmul(s["n1"], d_proj, name=nm("g_w_in"), ta=True))
    d_n1 = matmul(d_sb, p["w_sb"], name=nm("d_n1_sb"), tb=True)
    d_n1 = matmul(d_proj, p["w_rest"], name=nm("d_n1"), tb=True, residual=d_n1)
    (dh0,), (gm,) = rowwise_bwd(rms_fn, [s["h"]], [], [p["mix_norm"]], [d_n1], [F32], name=nm("mix_norm_bwd"),
                                add0=dh1)
    g["mix_norm"] = gm[0]
    return dh0, g, (rode[0] if rode else None), (rode_ffn[0] if rode_ffn else None)


def kernel(x, positions, mix_norm, w_in, sb_out_norm, ssm_conv_w, ssm_conv_b, ssm_dt_bias, ssm_a_log, ssm_d, ssm_out_norm, mla_q_norm, mla_w_uq, mla_kv_norm, mla_w_ukv, mla_out_norm, w_out, ffn_norm, ffn_w_up, ffn_conv_w, ffn_conv_b, ffn_w_down, final_norm, loss_target, m_mix_norm, m_w_in, m_sb_out_norm, m_ssm_conv_w, m_ssm_conv_b, m_ssm_dt_bias, m_ssm_a_log, m_ssm_d, m_ssm_out_norm, m_mla_q_norm, m_mla_w_uq, m_mla_kv_norm, m_mla_w_ukv, m_mla_out_norm, m_w_out, m_ffn_norm, m_ffn_w_up, m_ffn_conv_w, m_ffn_conv_b, m_ffn_w_down, m_final_norm, v_mix_norm, v_w_in, v_sb_out_norm, v_ssm_conv_w, v_ssm_conv_b, v_ssm_dt_bias, v_ssm_a_log, v_ssm_d, v_ssm_out_norm, v_mla_q_norm, v_mla_w_uq, v_mla_kv_norm, v_mla_w_ukv, v_mla_out_norm, v_w_out, v_ffn_norm, v_ffn_w_up, v_ffn_conv_w, v_ffn_conv_b, v_ffn_w_down, v_final_norm):
    W = dict(mix_norm=mix_norm, w_in=w_in, sb_out_norm=sb_out_norm, ssm_conv_w=ssm_conv_w, ssm_conv_b=ssm_conv_b,
             ssm_dt_bias=ssm_dt_bias, ssm_a_log=ssm_a_log, ssm_d=ssm_d, ssm_out_norm=ssm_out_norm,
             mla_q_norm=mla_q_norm, mla_w_uq=mla_w_uq, mla_kv_norm=mla_kv_norm, mla_w_ukv=mla_w_ukv,
             mla_out_norm=mla_out_norm, w_out=w_out, ffn_norm=ffn_norm, ffn_w_up=ffn_w_up, ffn_conv_w=ffn_conv_w,
             ffn_conv_b=ffn_conv_b, ffn_w_down=ffn_w_down, final_norm=final_norm)
    M = dict(mix_norm=m_mix_norm, w_in=m_w_in, sb_out_norm=m_sb_out_norm, ssm_conv_w=m_ssm_conv_w,
             ssm_conv_b=m_ssm_conv_b, ssm_dt_bias=m_ssm_dt_bias, ssm_a_log=m_ssm_a_log, ssm_d=m_ssm_d,
             ssm_out_norm=m_ssm_out_norm, mla_q_norm=m_mla_q_norm, mla_w_uq=m_mla_w_uq, mla_kv_norm=m_mla_kv_norm,
             mla_w_ukv=m_mla_w_ukv, mla_out_norm=m_mla_out_norm, w_out=m_w_out, ffn_norm=m_ffn_norm,
             ffn_w_up=m_ffn_w_up, ffn_conv_w=m_ffn_conv_w, ffn_conv_b=m_ffn_conv_b, ffn_w_down=m_ffn_w_down,
             final_norm=m_final_norm)
    V = dict(mix_norm=v_mix_norm, w_in=v_w_in, sb_out_norm=v_sb_out_norm, ssm_conv_w=v_ssm_conv_w,
             ssm_conv_b=v_ssm_conv_b, ssm_dt_bias=v_ssm_dt_bias, ssm_a_log=v_ssm_a_log, ssm_d=v_ssm_d,
             ssm_out_norm=v_ssm_out_norm, mla_q_norm=v_mla_q_norm, mla_w_uq=v_mla_w_uq, mla_kv_norm=v_mla_kv_norm,
             mla_w_ukv=v_mla_w_ukv, mla_out_norm=v_mla_out_norm, w_out=v_w_out, ffn_norm=v_ffn_norm,
             ffn_w_up=v_ffn_w_up, ffn_conv_w=v_ffn_conv_w, ffn_conv_b=v_ffn_conv_b, ffn_w_down=v_ffn_w_down,
             final_norm=v_final_norm)
    depth = mix_norm.shape[0]
    cx, cy, cc = _place()
    chip = 2 * cx + cy
    T = x.shape[1]

    assert depth == 2
    shard_shapes = {n: W[n].shape[1:] for n in BIG}

    def layer_of(d, li):
        return {n: d[n][li] for n in BIG}

    ffn_w = ("ffn_w_up", "ffn_w_down")
    mix_w = tuple(n for n in BIG if n not in ffn_w)
    Rf, Rm = _flat_rows(shard_shapes, ffn_w), _flat_rows(shard_shapes, mix_w)

    def half(li, names):
        return _flat_half(layer_of(W, li), cc, BF16, names)

    def ffn_from(g8):
        f = _from_blocks(g8[:, :Rf], shard_shapes, ffn_w)
        return {"w_up": f["ffn_w_up"], "w_down": f["ffn_w_down"]}

    full0 = _from_blocks(allgather_blocks(half(0, mix_w), name="gather_weights_l0"), shard_shapes, mix_w)
    conv_full = {}
    small = {n: W[n] for n in SMALL_REPL}
    cw_flat = jnp.concatenate([W[n].reshape(-1) for n in SMALL_SHARD])
    cw_rows = -(-cw_flat.shape[0] // (8 * LANES)) * 8
    cw_all = allgather_direct(jnp.pad(cw_flat, (0, cw_rows * LANES - cw_flat.shape[0])).reshape(cw_rows, LANES),
                              name="gather_conv_taps")
    off = 0
    for n in SMALL_SHARD:
        sz = W[n].size
        conv_full[n] = jnp.concatenate(
            [cw_all[2 * k].reshape(-1)[off:off + sz].reshape(W[n].shape) for k in range(4)], axis=2)
        off += sz
    small.update(conv_full)

    cos, sin = _rope_tables(positions[0])
    params0 = _layer_weights(full0, small, 0)
    h, s0, g8 = _layer_fwd(x[0], params0, cos, sin, 0, late=ffn_from,
                           exchange=("gather", jnp.concatenate([half(0, ffn_w), half(1, mix_w)])))
    params1 = _layer_weights(_from_blocks(g8[:, Rf:Rf + Rm], shard_shapes, mix_w), small, 1)
    h, s1, _ = _layer_fwd(h, params1, cos, sin, 1, late=ffn_from, exchange=("gather", half(1, ffn_w)))
    dh, g_final, loss_lanes = loss_head(h, loss_target[0], _row(final_norm), name="loss_head")

    dh, g1, _, _ = _layer_bwd(dh, s1, params1, cos, sin, 1)
    blocks1 = _to_blocks(g1, shard_shapes, BF16)
    R1, Rf = blocks1.shape[1], _flat_rows(shard_shapes, ffn_w)

    def rider(g0_so_far):
        return "all_to_all", _to_blocks(g0_so_far, shard_shapes, BF16, ffn_w)

    dh, g0, from_all_f, from_all_1 = _layer_bwd(dh, s0, params0, cos, sin, 0, exchange=rider,
                                                exchange_ffn=("all_to_all", blocks1))
    grad_x = dh[None]
    grads = [g0, g1]
    G = {n: jnp.stack([grads[li][n] for li in range(depth)]) for n in WEIGHTS if n != "final_norm" and n not in BIG}
    G["final_norm"] = g_final[0]
    half1 = jnp.concatenate([add_blocks([(from_all_1, d) for d in range(8)], F32, name="grads_l1_sum"),
                             add_blocks([(from_all_f, d) for d in range(8)], F32, name="grads_l0_ffn_sum")])

    blocks0 = _to_blocks(g0, shard_shapes, BF16, mix_w)
    R = blocks0.shape[1]
    blocks0 = blocks0.reshape(4, 2, R, LANES)
    mine_first = lax.dynamic_index_in_dim(blocks0, cc, 1, keepdims=False)
    for_sibling = lax.dynamic_index_in_dim(blocks0, 1 - cc, 1, keepdims=False)
    from_sibling = send_to_sibling(for_sibling.reshape(4 * R, LANES), name="grads_to_sibling")
    pair = add_blocks([mine_first.reshape(4 * R, LANES), from_sibling], BF16, name="grads_pair_sum").reshape(4, R, LANES)
    others = chip_exchange(pair, name="grads_chip_exchange")
    own = lax.dynamic_index_in_dim(pair, chip, 0, keepdims=False)
    half0 = add_blocks([own, (others, 0), (others, 1), (others, 2)], F32, name="grads_chip_sum")
    half = jnp.concatenate([half0, half1])
    other = send_to_sibling(half, name="grads_pair_swap")
    by_core = [jnp.where(cc == 0, half, other), jnp.where(cc == 0, other, half)]

    def unflat(lo, hi, names):
        return _unflat_halves([a[lo:hi].reshape(-1) for a in by_core], shard_shapes, names)

    g_big_l = [{**unflat(0, R, mix_w), **unflat(R + R1, R + R1 + Rf, ffn_w)}, unflat(R, R + R1, BIG)]
    g_big = {n: jnp.stack([g_big_l[li][n] for li in range(depth)]) for n in BIG}

    small_list = [G[n].reshape(-1) for n in SMALL_REPL] + [G[n].reshape(-1) for n in SMALL_SHARD]
    small_list.append(jnp.sum(loss_lanes).reshape(1))
    sm = jnp.concatenate(small_list)
    n_small = sm.shape[0]
    sm_rows = -(-n_small // (16 * LANES)) * 16
    sm_all = allgather_direct(jnp.pad(sm, (0, sm_rows * LANES - n_small)).reshape(sm_rows, LANES), name="gather_small")
    sm_sum = add_blocks([(sm_all, d) for d in range(8)], F32, name="small_sum").reshape(-1)
    g_small, off = {}, 0
    for n in SMALL_REPL:
        g_small[n] = sm_sum[off:off + W[n].size].reshape(W[n].shape)
        off += W[n].size
    for n in SMALL_SHARD:
        full_shape = conv_full[n].shape
        sz = conv_full[n].size
        gfull = sm_sum[off:off + sz].reshape(full_shape)
        width = W[n].shape[2]
        g_small[n] = lax.dynamic_slice_in_dim(gfull, chip * width, width, axis=2)
        off += sz
    loss = sm_sum[off]

    grad_out, delta, new_m, new_v = {}, {}, {}, {}
    for n in BIG:
        shp = W[n].shape
        two_d = lambda a: a.reshape(shp[0] * shp[1], shp[2])
        d, nm_, nv_ = adamw(two_d(W[n]), two_d(g_big[n]), two_d(M[n]), two_d(V[n]), name="adamw_" + n)
        grad_out[n], delta[n], new_m[n], new_v[n] = g_big[n], d.reshape(shp), nm_.reshape(shp), nv_.reshape(shp)
    small_names = SMALL_REPL + SMALL_SHARD

    def flat_small(d):
        f = jnp.concatenate([d[n].reshape(-1) for n in small_names])
        rows = -(-f.shape[0] // (8 * LANES)) * 8
        return jnp.pad(f, (0, rows * LANES - f.shape[0])).reshape(rows, LANES)

    vpad = flat_small(V)
    d, nm_, nv_ = adamw(flat_small(W), flat_small(g_small), flat_small(M), vpad, name="adamw_small")
    off = 0
    for n in small_names:
        sz = W[n].size
        grad_out[n] = g_small[n]
        delta[n] = d.reshape(-1)[off:off + sz].reshape(W[n].shape)
        new_m[n] = nm_.reshape(-1)[off:off + sz].reshape(W[n].shape)
        new_v[n] = nv_.reshape(-1)[off:off + sz].reshape(W[n].shape)
        off += sz

    return (loss, grad_x, *[grad_out[n] for n in WEIGHTS], *[delta[n] for n in WEIGHTS],
            *[new_m[n] for n in WEIGHTS], *[new_v[n] for n in WEIGHTS])
```
